```python
import jax, jax.numpy as jnp
from jax import lax
import numpy as np

D_MODEL = 1024
BATCH = 8
SEQ = 2048
DEPTH = 2

D_MIX = D_MODEL
ATTN_DIM = D_MIX // 2
CONV_DIM = D_MIX - ATTN_DIM
HEAD_DIM = 64
N_HEADS = ATTN_DIM // HEAD_DIM
CONV_WIDTH = 31
Q_BLOCK = 128
PLE_DIM = 256
D_IN = 4 * ATTN_DIM + 3 * CONV_DIM
EPS = 1e-6

kernel_name = "hymba_conformer_stickbreaking_ple"


def rms_norm(x, g):
    xf = x.astype(jnp.float32)
    y = xf * lax.rsqrt(jnp.mean(xf * xf, axis=-1, keepdims=True) + EPS)
    return (y * g.astype(jnp.float32)).astype(x.dtype)


def layer_norm(x, g, b):
    xf = x.astype(jnp.float32)
    mu = jnp.mean(xf, axis=-1, keepdims=True)
    xc = xf - mu
    y = xc * lax.rsqrt(jnp.mean(xc * xc, axis=-1, keepdims=True) + EPS)
    return (y * g.astype(jnp.float32) + b.astype(jnp.float32)).astype(x.dtype)


def stick_breaking_attention(q, k, v):
    S = q.shape[1]
    scale = HEAD_DIM ** -0.5
    outs = []
    for blk in range(S // Q_BLOCK):
        q0 = blk * Q_BLOCK
        kend = q0 + Q_BLOCK
        qb = q[:, q0:kend]
        kb = k[:, :kend]
        vb = v[:, :kend]
        z = jnp.einsum('bqhd,bkhd->bhqk', qb, kb).astype(jnp.float32) * scale
        qpos = q0 + jnp.arange(Q_BLOCK)[:, None]
        kpos = jnp.arange(kend)[None, :]
        causal = kpos < qpos
        log_1m_beta = jnp.where(causal, -jax.nn.softplus(z), 0.0)
        suffix = lax.cumsum(log_1m_beta, axis=3, reverse=True) - log_1m_beta
        log_a = jax.nn.log_sigmoid(z) + suffix
        a = jnp.where(causal, jnp.exp(log_a), 0.0)
        outs.append(jnp.einsum('bhqk,bkhd->bqhd', a.astype(v.dtype), vb))
    return jnp.concatenate(outs, axis=1)


def causal_depthwise_conv(x, w, b):
    rhs = w[:, None, :].astype(x.dtype)
    y = lax.conv_general_dilated(
        x, rhs, window_strides=(1,), padding=((CONV_WIDTH - 1, 0),),
        dimension_numbers=('NWC', 'WIO', 'NWC'), feature_group_count=x.shape[-1])
    return y + b.astype(x.dtype)


def _fwd_setup_inputs(seed: int = 0) -> dict:
    key = jax.random.key(seed)
    ks = jax.random.split(key, 16)
    f32 = jnp.float32
    nrm = lambda k, shape, s: jax.random.normal(k, shape, f32) * s
    return {
        "x": nrm(ks[0], (BATCH, SEQ, D_MODEL), 1.0),
        "p": nrm(ks[1], (DEPTH, BATCH, SEQ, PLE_DIM), 1.0),
        "norm_g": 1.0 + nrm(ks[2], (DEPTH, D_MODEL), 0.02),
        "w_in": nrm(ks[3], (DEPTH, D_MODEL, D_IN), D_MODEL ** -0.5),
        "attn_out_g": 1.0 + nrm(ks[4], (DEPTH, HEAD_DIM), 0.02),
        "dw_w": nrm(ks[5], (DEPTH, CONV_WIDTH, CONV_DIM), CONV_WIDTH ** -0.5),
        "dw_b": nrm(ks[6], (DEPTH, CONV_DIM), 0.02),
        "conv_ln_g": 1.0 + nrm(ks[7], (DEPTH, CONV_DIM), 0.02),
        "conv_ln_b": nrm(ks[8], (DEPTH, CONV_DIM), 0.02),
        "w_pw": nrm(ks[9], (DEPTH, CONV_DIM, CONV_DIM), CONV_DIM ** -0.5),
        "conv_out_g": 1.0 + nrm(ks[10], (DEPTH, CONV_DIM), 0.02),
        "w_out": nrm(ks[11], (DEPTH, D_MIX, D_MODEL), D_MIX ** -0.5),
        "ple_norm_g": 1.0 + nrm(ks[12], (DEPTH, D_MODEL), 0.02),
        "w_ple_gate": nrm(ks[13], (DEPTH, D_MODEL, D_MODEL), D_MODEL ** -0.5),
        "w_ple": nrm(ks[14], (DEPTH, PLE_DIM, D_MODEL), PLE_DIM ** -0.5),
        "final_g": 1.0 + nrm(ks[15], (D_MODEL,), 0.02),
    }


def _fwd_reference(x, p, norm_g, w_in, attn_out_g, dw_w, dw_b, conv_ln_g, conv_ln_b,
              w_pw, conv_out_g, w_out, ple_norm_g, w_ple_gate, w_ple, final_g):
    B, S, _ = x.shape
    split_at = np.cumsum([ATTN_DIM, ATTN_DIM, ATTN_DIM, ATTN_DIM,
                          CONV_DIM, CONV_DIM])
    h = x
    for i in range(DEPTH):
        hn = rms_norm(h, norm_g[i])
        u = hn @ w_in[i]
        q, k, v, g_attn, c_val, c_gate, g_conv = jnp.split(u, split_at, axis=-1)

        heads = lambda t: t.reshape(B, S, N_HEADS, HEAD_DIM)
        o = stick_breaking_attention(heads(q), heads(k), heads(v))
        o = rms_norm(o, attn_out_g[i]).reshape(B, S, ATTN_DIM)
        y_attn = o * jax.nn.silu(g_attn)

        c = c_val * jax.nn.sigmoid(c_gate)
        c = causal_depthwise_conv(c, dw_w[i], dw_b[i])
        c = jax.nn.silu(layer_norm(c, conv_ln_g[i], conv_ln_b[i]))
        c = c @ w_pw[i]
        y_conv = rms_norm(c, conv_out_g[i]) * jax.nn.silu(g_conv)

        y = jnp.concatenate([y_attn, y_conv], axis=-1) @ w_out[i]
        h = h + y

        gate = jax.nn.sigmoid(rms_norm(h, ple_norm_g[i]) @ w_ple_gate[i])
        h = h + (p[i].astype(h.dtype) @ w_ple[i]) * gate
    return rms_norm(h, final_g)


import jax as _jax
import jax.numpy as _jnp

TWIN_FORMAT = 'train_step'
FWD_PARAMS = ['x', 'p', 'norm_g', 'w_in', 'attn_out_g', 'dw_w', 'dw_b', 'conv_ln_g', 'conv_ln_b', 'w_pw', 'conv_out_g', 'w_out', 'ple_norm_g', 'w_ple_gate', 'w_ple', 'final_g']
TWIN_WEIGHTS = ['norm_g', 'w_in', 'attn_out_g', 'dw_w', 'dw_b', 'conv_ln_g', 'conv_ln_b', 'w_pw', 'conv_out_g', 'w_out', 'ple_norm_g', 'w_ple_gate', 'w_ple', 'final_g']
TWIN_DIFF_INPUT = 'x'
TWIN_INPUTS = ['x', 'p', 'norm_g', 'w_in', 'attn_out_g', 'dw_w', 'dw_b', 'conv_ln_g', 'conv_ln_b', 'w_pw', 'conv_out_g', 'w_out', 'ple_norm_g', 'w_ple_gate', 'w_ple', 'final_g', 'loss_target', 'm_norm_g', 'm_w_in', 'm_attn_out_g', 'm_dw_w', 'm_dw_b', 'm_conv_ln_g', 'm_conv_ln_b', 'm_w_pw', 'm_conv_out_g', 'm_w_out', 'm_ple_norm_g', 'm_w_ple_gate', 'm_w_ple', 'm_final_g', 'v_norm_g', 'v_w_in', 'v_attn_out_g', 'v_dw_w', 'v_dw_b', 'v_conv_ln_g', 'v_conv_ln_b', 'v_w_pw', 'v_conv_out_g', 'v_w_out', 'v_ple_norm_g', 'v_w_ple_gate', 'v_w_ple', 'v_final_g']
TWIN_OUTPUTS = ['loss', 'grad_x', 'grad_norm_g', 'grad_w_in', 'grad_attn_out_g', 'grad_dw_w', 'grad_dw_b', 'grad_conv_ln_g', 'grad_conv_ln_b', 'grad_w_pw', 'grad_conv_out_g', 'grad_w_out', 'grad_ple_norm_g', 'grad_w_ple_gate', 'grad_w_ple', 'grad_final_g', 'delta_norm_g', 'delta_w_in', 'delta_attn_out_g', 'delta_dw_w', 'delta_dw_b', 'delta_conv_ln_g', 'delta_conv_ln_b', 'delta_w_pw', 'delta_conv_out_g', 'delta_w_out', 'delta_ple_norm_g', 'delta_w_ple_gate', 'delta_w_ple', 'delta_final_g', 'new_m_norm_g', 'new_m_w_in', 'new_m_attn_out_g', 'new_m_dw_w', 'new_m_dw_b', 'new_m_conv_ln_g', 'new_m_conv_ln_b', 'new_m_w_pw', 'new_m_conv_out_g', 'new_m_w_out', 'new_m_ple_norm_g', 'new_m_w_ple_gate', 'new_m_w_ple', 'new_m_final_g', 'new_v_norm_g', 'new_v_w_in', 'new_v_attn_out_g', 'new_v_dw_w', 'new_v_dw_b', 'new_v_conv_ln_g', 'new_v_conv_ln_b', 'new_v_w_pw', 'new_v_conv_out_g', 'new_v_w_out', 'new_v_ple_norm_g', 'new_v_w_ple_gate', 'new_v_w_ple', 'new_v_final_g']
TWIN_LEAF_KINDS = {'loss': 'loss', 'grad_x': 'grad_x', 'grad_norm_g': 'grad_w', 'grad_w_in': 'grad_w', 'grad_attn_out_g': 'grad_w', 'grad_dw_w': 'grad_w', 'grad_dw_b': 'grad_w', 'grad_conv_ln_g': 'grad_w', 'grad_conv_ln_b': 'grad_w', 'grad_w_pw': 'grad_w', 'grad_conv_out_g': 'grad_w', 'grad_w_out': 'grad_w', 'grad_ple_norm_g': 'grad_w', 'grad_w_ple_gate': 'grad_w', 'grad_w_ple': 'grad_w', 'grad_final_g': 'grad_w', 'delta_norm_g': 'delta_w', 'delta_w_in': 'delta_w', 'delta_attn_out_g': 'delta_w', 'delta_dw_w': 'delta_w', 'delta_dw_b': 'delta_w', 'delta_conv_ln_g': 'delta_w', 'delta_conv_ln_b': 'delta_w', 'delta_w_pw': 'delta_w', 'delta_conv_out_g': 'delta_w', 'delta_w_out': 'delta_w', 'delta_ple_norm_g': 'delta_w', 'delta_w_ple_gate': 'delta_w', 'delta_w_ple': 'delta_w', 'delta_final_g': 'delta_w', 'new_m_norm_g': 'new_m', 'new_m_w_in': 'new_m', 'new_m_attn_out_g': 'new_m', 'new_m_dw_w': 'new_m', 'new_m_dw_b': 'new_m', 'new_m_conv_ln_g': 'new_m', 'new_m_conv_ln_b': 'new_m', 'new_m_w_pw': 'new_m', 'new_m_conv_out_g': 'new_m', 'new_m_w_out': 'new_m', 'new_m_ple_norm_g': 'new_m', 'new_m_w_ple_gate': 'new_m', 'new_m_w_ple': 'new_m', 'new_m_final_g': 'new_m', 'new_v_norm_g': 'new_v', 'new_v_w_in': 'new_v', 'new_v_attn_out_g': 'new_v', 'new_v_dw_w': 'new_v', 'new_v_dw_b': 'new_v', 'new_v_conv_ln_g': 'new_v', 'new_v_conv_ln_b': 'new_v', 'new_v_w_pw': 'new_v', 'new_v_conv_out_g': 'new_v', 'new_v_w_out': 'new_v', 'new_v_ple_norm_g': 'new_v', 'new_v_w_ple_gate': 'new_v', 'new_v_w_ple': 'new_v', 'new_v_final_g': 'new_v'}


def _forward(args):
    return _fwd_reference(*[args[k] for k in FWD_PARAMS])


def _output_shape():
    out = _jax.eval_shape(lambda: _forward(_fwd_setup_inputs(0)))
    return out.shape, out.dtype

N_MICROBATCH = 1
ADAM_LR = 0.001
ADAM_B1 = 0.9
ADAM_B2 = 0.999
ADAM_EPS = 1e-08
ADAM_WD = 0.01
ADAM_STEP = 10
PER_EXAMPLE_BATCH_AXIS = {'x': 0, 'p': 1, 'loss_target': 0}
SHARED_INPUTS = []
_WEIGHT_DTYPES = {'norm_g': _jnp.float32, 'w_in': _jnp.float32, 'attn_out_g': _jnp.float32, 'dw_w': _jnp.float32, 'dw_b': _jnp.float32, 'conv_ln_g': _jnp.float32, 'conv_ln_b': _jnp.float32, 'w_pw': _jnp.float32, 'conv_out_g': _jnp.float32, 'w_out': _jnp.float32, 'ple_norm_g': _jnp.float32, 'w_ple_gate': _jnp.float32, 'w_ple': _jnp.float32, 'final_g': _jnp.float32}
MOMENT_SCALE = {'norm_g': 8.886435e-02, 'w_in': 4.703762e-02, 'attn_out_g': 1.519977e-01, 'dw_w': 5.935206e-02, 'dw_b': 1.297985e-01, 'conv_ln_g': 6.917068e-02, 'conv_ln_b': 6.091440e-02, 'w_pw': 5.719834e-02, 'conv_out_g': 5.951889e-02, 'w_out': 5.643359e-02, 'ple_norm_g': 1.983002e-02, 'w_ple_gate': 1.981999e-02, 'w_ple': 5.068044e-02, 'final_g': 1.597371e+01}


def _to_microbatches(a, axis):
    t = _jnp.moveaxis(a, axis, 0)
    t = t.reshape((N_MICROBATCH, t.shape[0] // N_MICROBATCH) + t.shape[1:])
    return _jnp.moveaxis(t, 1, axis + 1)


def setup_inputs(seed: int = 0) -> dict:
    inp = _fwd_setup_inputs(seed)
    key = _jax.random.fold_in(_jax.random.key(seed), 7919)
    shape, _ = _output_shape()
    out = dict(inp)
    out["loss_target"] = _jax.random.normal(_jax.random.fold_in(key, 0), shape, _jnp.float32)
    for i, name in enumerate(TWIN_WEIGHTS):
        w = inp[name].astype(_jnp.float32)
        if MOMENT_SCALE is None:
            s = _jnp.sqrt(_jnp.mean(_jnp.square(w)) + 1e-30)
        else:
            s = MOMENT_SCALE[name]
        km, kv = _jax.random.split(_jax.random.fold_in(key, i + 1))
        out[name] = w
        out["m_" + name] = s * _jax.random.normal(km, w.shape, _jnp.float32)
        out["v_" + name] = (s * s) * _jax.random.uniform(kv, w.shape, _jnp.float32, 0.5, 1.5)
    if N_MICROBATCH > 1:
        for name, axis in PER_EXAMPLE_BATCH_AXIS.items():
            out[name] = _to_microbatches(out[name], axis)
    return {'x': out['x'], 'p': out['p'], 'norm_g': out['norm_g'], 'w_in': out['w_in'], 'attn_out_g': out['attn_out_g'], 'dw_w': out['dw_w'], 'dw_b': out['dw_b'], 'conv_ln_g': out['conv_ln_g'], 'conv_ln_b': out['conv_ln_b'], 'w_pw': out['w_pw'], 'conv_out_g': out['conv_out_g'], 'w_out': out['w_out'], 'ple_norm_g': out['ple_norm_g'], 'w_ple_gate': out['w_ple_gate'], 'w_ple': out['w_ple'], 'final_g': out['final_g'], 'loss_target': out['loss_target'], 'm_norm_g': out['m_norm_g'], 'm_w_in': out['m_w_in'], 'm_attn_out_g': out['m_attn_out_g'], 'm_dw_w': out['m_dw_w'], 'm_dw_b': out['m_dw_b'], 'm_conv_ln_g': out['m_conv_ln_g'], 'm_conv_ln_b': out['m_conv_ln_b'], 'm_w_pw': out['m_w_pw'], 'm_conv_out_g': out['m_conv_out_g'], 'm_w_out': out['m_w_out'], 'm_ple_norm_g': out['m_ple_norm_g'], 'm_w_ple_gate': out['m_w_ple_gate'], 'm_w_ple': out['m_w_ple'], 'm_final_g': out['m_final_g'], 'v_norm_g': out['v_norm_g'], 'v_w_in': out['v_w_in'], 'v_attn_out_g': out['v_attn_out_g'], 'v_dw_w': out['v_dw_w'], 'v_dw_b': out['v_dw_b'], 'v_conv_ln_g': out['v_conv_ln_g'], 'v_conv_ln_b': out['v_conv_ln_b'], 'v_w_pw': out['v_w_pw'], 'v_conv_out_g': out['v_conv_out_g'], 'v_w_out': out['v_w_out'], 'v_ple_norm_g': out['v_ple_norm_g'], 'v_w_ple_gate': out['v_w_ple_gate'], 'v_w_ple': out['v_w_ple'], 'v_final_g': out['v_final_g']}


def _loss(weights, diff, rest, loss_target):
    with _jax.named_scope("forward"):
        args = {**rest, TWIN_DIFF_INPUT: diff, **{k: w.astype(_WEIGHT_DTYPES[k]) for k, w in weights.items()}}
        y = _forward(args)
    with _jax.named_scope("loss_head"):
        err = _jnp.square(y.astype(_jnp.float32) - loss_target)
        return 0.5 * _jnp.sum(_jnp.mean(err, axis=-1)) if err.ndim else 0.5 * err


def _adamw(w, g, m, v):
    m = ADAM_B1 * m + (1.0 - ADAM_B1) * g
    v = ADAM_B2 * v + (1.0 - ADAM_B2) * _jnp.square(g)
    m_hat = m / (1.0 - ADAM_B1 ** ADAM_STEP)
    v_hat = v / (1.0 - ADAM_B2 ** ADAM_STEP)
    delta = -ADAM_LR * (m_hat / (_jnp.sqrt(v_hat) + ADAM_EPS) + ADAM_WD * w)
    return delta, m, v


def reference(x, p, norm_g, w_in, attn_out_g, dw_w, dw_b, conv_ln_g, conv_ln_b, w_pw, conv_out_g, w_out, ple_norm_g, w_ple_gate, w_ple, final_g, loss_target, m_norm_g, m_w_in, m_attn_out_g, m_dw_w, m_dw_b, m_conv_ln_g, m_conv_ln_b, m_w_pw, m_conv_out_g, m_w_out, m_ple_norm_g, m_w_ple_gate, m_w_ple, m_final_g, v_norm_g, v_w_in, v_attn_out_g, v_dw_w, v_dw_b, v_conv_ln_g, v_conv_ln_b, v_w_pw, v_conv_out_g, v_w_out, v_ple_norm_g, v_w_ple_gate, v_w_ple, v_final_g):
    given = dict(x=x, p=p, norm_g=norm_g, w_in=w_in, attn_out_g=attn_out_g, dw_w=dw_w, dw_b=dw_b, conv_ln_g=conv_ln_g, conv_ln_b=conv_ln_b, w_pw=w_pw, conv_out_g=conv_out_g, w_out=w_out, ple_norm_g=ple_norm_g, w_ple_gate=w_ple_gate, w_ple=w_ple, final_g=final_g, loss_target=loss_target, m_norm_g=m_norm_g, m_w_in=m_w_in, m_attn_out_g=m_attn_out_g, m_dw_w=m_dw_w, m_dw_b=m_dw_b, m_conv_ln_g=m_conv_ln_g, m_conv_ln_b=m_conv_ln_b, m_w_pw=m_w_pw, m_conv_out_g=m_conv_out_g, m_w_out=m_w_out, m_ple_norm_g=m_ple_norm_g, m_w_ple_gate=m_w_ple_gate, m_w_ple=m_w_ple, m_final_g=m_final_g, v_norm_g=v_norm_g, v_w_in=v_w_in, v_attn_out_g=v_attn_out_g, v_dw_w=v_dw_w, v_dw_b=v_dw_b, v_conv_ln_g=v_conv_ln_g, v_conv_ln_b=v_conv_ln_b, v_w_pw=v_w_pw, v_conv_out_g=v_conv_out_g, v_w_out=v_w_out, v_ple_norm_g=v_ple_norm_g, v_w_ple_gate=v_w_ple_gate, v_w_ple=v_w_ple, v_final_g=v_final_g)
    weights = {n: given[n] for n in TWIN_WEIGHTS}
    shared = {n: given[n] for n in SHARED_INPUTS}
    per_example = {n: given[n] for n in ['x', 'p']}
    grad_fn = _jax.value_and_grad(_loss, argnums=(0, 1))

    def one_microbatch(ex, loss_target):
        ex = dict(ex)
        diff = ex.pop(TWIN_DIFF_INPUT)
        return grad_fn(weights, diff, {**shared, **ex}, loss_target)

    if N_MICROBATCH == 1:
        loss, (grad_w, grad_x) = one_microbatch(per_example, given["loss_target"])
    else:
        def body(carry, xs):
            loss_sum, grad_sum = carry
            l_k, (gw_k, gx_k) = one_microbatch(xs[0], xs[1])
            with _jax.named_scope("update"):
                return (loss_sum + l_k, _jax.tree.map(_jnp.add, grad_sum, gw_k)), gx_k

        init = (_jnp.zeros((), _jnp.float32), _jax.tree.map(_jnp.zeros_like, weights))
        (loss, grad_w), grad_x = _jax.lax.scan(body, init, (per_example, given["loss_target"]))
    with _jax.named_scope("update"):
        delta_w, new_m, new_v = {}, {}, {}
        for n in TWIN_WEIGHTS:
            delta_w[n], new_m[n], new_v[n] = _adamw(weights[n], grad_w[n], given["m_" + n], given["v_" + n])
    return (loss, grad_x, *[grad_w[n] for n in TWIN_WEIGHTS], *[delta_w[n] for n in TWIN_WEIGHTS],
            *[new_m[n] for n in TWIN_WEIGHTS], *[new_v[n] for n in TWIN_WEIGHTS])
```

```python
import functools

import jax
import jax.numpy as jnp
from jax import lax
from jax.experimental import pallas as pl
from jax.experimental.pallas import tpu as pltpu

F32 = jnp.float32
BF16 = jnp.bfloat16
MESH = pl.DeviceIdType.MESH

N_DEV = 8
D_MODEL = 1024
ATTN_DIM = 512
CONV_DIM = 512
HEAD_DIM = 64
N_HEADS = 8
CONV_WIDTH = 31
PLE_DIM = 256
CHUNK = 512
N_CHUNK = 7
EPS = 1e-6
ADAM_LR = 0.001
ADAM_B1 = 0.9
ADAM_B2 = 0.999
ADAM_EPS = 1e-08
ADAM_WD = 0.01
ADAM_STEP = 10

LANES = 128
BLK = 256
TM = 256
HALO = 32
VMEM_LIMIT = 56 * 1024 * 1024
SMALL_ROWS = 16
PACK_ROWS = 48


def _nn(a, b):
    return lax.dot_general(a, b, (((1,), (0,)), ((), ())), preferred_element_type=F32)


def _nt(a, b):
    return lax.dot_general(a, b, (((1,), (1,)), ((), ())), preferred_element_type=F32)


def _tn(a, b):
    return lax.dot_general(a, b, (((0,), (0,)), ((), ())), preferred_element_type=F32)


def _split(x):
    hi = x.astype(BF16)
    lo = (x - hi.astype(F32)).astype(BF16)
    return hi, lo


def _dot_hilo(x, m):
    hi, lo = _split(x)
    return _nn(hi, m) + _nn(lo, m)


def _sigmoid(x):
    return jax.nn.sigmoid(x)


def _dsilu(x, s):
    return s * (1.0 + x * (1.0 - s))


def _params(n_grid=0, vmem=VMEM_LIMIT):
    sem = ("arbitrary",) * n_grid if n_grid else None
    return pltpu.CompilerParams(dimension_semantics=sem, vmem_limit_bytes=vmem)


def _rows(tm, cols, col=0):
    return pl.BlockSpec((tm, cols), lambda i: (i, col))


def _whole(shape):
    zeros = (0,) * len(shape)
    return pl.BlockSpec(shape, lambda *_: zeros)


def _my_position():
    return lax.axis_index("x"), lax.axis_index("y"), lax.axis_index("c")


def _block(ref, axis, idx, size):
    start = pl.multiple_of(idx * size, size)
    if axis == 0:
        return ref.at[pl.ds(start, size), :]
    return ref.at[:, pl.ds(start, size)]


def _all_gather(shards, axes, name):
    n = len(shards)
    sizes = [s.shape[a] for s, a in zip(shards, axes)]

    def full_shape(s, a):
        shape = list(s.shape)
        shape[a] *= N_DEV
        return jax.ShapeDtypeStruct(tuple(shape), s.dtype)

    def body(*refs):
        ins, outs = refs[:n], refs[n:2 * n]
        send_sems, recv_sems, local_sems = refs[2 * n:]
        x, y, c = _my_position()
        me, sibling = (x, y, c), (x, y, 1 - c)
        chips = [(1 - x, y), (x, 1 - y), (1 - x, 1 - y)]

        def place(i, dev):
            return _block(outs[i], axes[i], 4 * dev[0] + 2 * dev[1] + dev[2], sizes[i])

        def copy(k, i, dev, to, src=None):
            return pltpu.make_async_remote_copy(
                src_ref=place(i, dev) if src is None else src, dst_ref=place(i, dev),
                send_sem=send_sems.at[k, i], recv_sem=recv_sems.at[k, i],
                device_id=to, device_id_type=MESH)

        mine = [pltpu.make_async_copy(ins[i], place(i, me), local_sems.at[i]) for i in range(n)]
        for cp in mine:
            cp.start()
        first = [copy(0, i, me, sibling, src=ins[i]) for i in range(n)]
        for j, chip in enumerate(chips):
            first += [copy(1 + j, i, me, (*chip, c), src=ins[i]) for i in range(n)]
        for cp in first:
            cp.start()
        passed = []
        for j, chip in enumerate(chips):
            for i in range(n):
                copy(1 + j, i, (*chip, c), me).wait_recv()
            hop = [copy(4 + j, i, (*chip, c), sibling) for i in range(n)]
            for cp in hop:
                cp.start()
            passed += hop
        for i in range(n):
            copy(0, i, sibling, me).wait_recv()
        for j, chip in enumerate(chips):
            for i in range(n):
                copy(4 + j, i, (*chip, 1 - c), me).wait_recv()
        for cp in first + passed:
            cp.wait_send()
        for cp in mine:
            cp.wait()

    any_spec = pl.BlockSpec(memory_space=pl.ANY)
    return pl.pallas_call(
        body, name=name,
        out_shape=[full_shape(s, a) for s, a in zip(shards, axes)],
        in_specs=[any_spec] * n, out_specs=[any_spec] * n,
        scratch_shapes=[pltpu.SemaphoreType.DMA((7, n)), pltpu.SemaphoreType.DMA((7, n)),
                        pltpu.SemaphoreType.DMA((n,))],
    )(*shards)


def _scatter_partials(fulls, axes, name):
    n = len(fulls)
    sizes = [f.shape[a] // N_DEV for f, a in zip(fulls, axes)]

    def slot_shape(f, a):
        shape = list(f.shape)
        shape[a] //= N_DEV
        return jax.ShapeDtypeStruct((N_DEV, *shape), f.dtype)

    def body(*refs):
        ins, outs = refs[:n], refs[n:2 * n]
        send_sems, recv_sems, local_sems = refs[2 * n:]
        x, y, c = _my_position()
        my_idx = 4 * x + 2 * y + c

        def peer_of(k):
            px = 1 - x if k & 4 else x
            py = 1 - y if k & 2 else y
            pc = 1 - c if k & 1 else c
            return (px, py, pc), 4 * px + 2 * py + pc

        def copy(k, i):
            peer, peer_idx = peer_of(k)
            return pltpu.make_async_remote_copy(
                src_ref=_block(ins[i], axes[i], peer_idx, sizes[i]), dst_ref=outs[i].at[my_idx],
                send_sem=send_sems.at[k - 1, i], recv_sem=recv_sems.at[k - 1, i],
                device_id=peer, device_id_type=MESH)

        def arrival(k, i):
            peer, peer_idx = peer_of(k)
            return pltpu.make_async_remote_copy(
                src_ref=_block(ins[i], axes[i], my_idx, sizes[i]), dst_ref=outs[i].at[peer_idx],
                send_sem=send_sems.at[k - 1, i], recv_sem=recv_sems.at[k - 1, i],
                device_id=peer, device_id_type=MESH)

        mine = [pltpu.make_async_copy(_block(ins[i], axes[i], my_idx, sizes[i]), outs[i].at[my_idx],
                                      local_sems.at[i]) for i in range(n)]
        sends = [copy(k, i) for k in range(1, N_DEV) for i in range(n)]
        for cp in mine + sends:
            cp.start()
        for k in range(1, N_DEV):
            for i in range(n):
                arrival(k, i).wait_recv()
        for cp in sends:
            cp.wait_send()
        for cp in mine:
            cp.wait()

    any_spec = pl.BlockSpec(memory_space=pl.ANY)
    return pl.pallas_call(
        body, name=name,
        out_shape=[slot_shape(f, a) for f, a in zip(fulls, axes)],
        in_specs=[any_spec] * n, out_specs=[any_spec] * n,
        scratch_shapes=[pltpu.SemaphoreType.DMA((7, n)), pltpu.SemaphoreType.DMA((7, n)),
                        pltpu.SemaphoreType.DMA((n,))],
    )(*fulls)


def _prenorm_inproj(h, gain, w_in_t, name):
    T = h.shape[0]

    def body(h_ref, g_ref, w_ref, q_ref, k_ref, v_ref, ug_ref, hn_ref):
        hv = h_ref[...]
        r = lax.rsqrt(jnp.mean(hv * hv, axis=-1, keepdims=True) + EPS)
        hn = (hv * r * g_ref[...]).astype(BF16)
        hn_ref[...] = hn
        for j in range(N_CHUNK):
            u = _nt(hn, w_ref[j * CHUNK:(j + 1) * CHUNK, :])
            if j == 0:
                q_ref[...] = (u * (HEAD_DIM ** -0.5)).astype(BF16)
            elif j == 1:
                k_ref[...] = u.astype(BF16)
            elif j == 2:
                v_ref[...] = u.astype(BF16)
            else:
                ug_ref[:, (j - 3) * CHUNK:(j - 2) * CHUNK] = u

    act = jax.ShapeDtypeStruct((T, CHUNK), BF16)
    return pl.pallas_call(
        body, name=name, grid=(T // TM,),
        in_specs=[_rows(TM, D_MODEL), _whole((1, D_MODEL)), _whole((N_CHUNK * CHUNK, D_MODEL))],
        out_specs=[_rows(TM, CHUNK)] * 3 + [_rows(TM, 4 * CHUNK), _rows(TM, D_MODEL)],
        out_shape=[act, act, act, jax.ShapeDtypeStruct((T, 4 * CHUNK), F32),
                   jax.ShapeDtypeStruct((T, D_MODEL), BF16)],
        compiler_params=_params(1),
    )(h, gain, w_in_t)


def _softplus_parts(z):
    e = jnp.exp(-jnp.abs(z))
    return e, jnp.maximum(z, 0.0) + jnp.log(1.0 + e)


def _attn_fwd(qs, k, v, tri, name):
    T = qs.shape[0]

    def body(q_ref, k_ref, v_ref, m_ref, o_ref, cs_ref):
        qi = pl.program_id(1)
        q = q_ref[...]
        lane = lax.broadcasted_iota(jnp.int32, (BLK, LANES), 1)
        causal = (lax.broadcasted_iota(jnp.int32, (BLK, BLK), 1)
                  < lax.broadcasted_iota(jnp.int32, (BLK, BLK), 0))
        tri_m = m_ref[...]
        o_tot = jnp.zeros((BLK, LANES), F32)
        cs_tot = jnp.zeros((BLK, LANES), F32)
        for half in range(2):
            sel = (lane < HEAD_DIM) if half == 0 else (lane >= HEAD_DIM)
            qh = jnp.where(sel, q, jnp.zeros_like(q))

            def step(kb, carry, acc, cvals, masked, qh=qh, half=half):
                start = pl.multiple_of(kb * BLK, BLK)
                kblk = k_ref[pl.ds(start, BLK), :]
                vblk = v_ref[pl.ds(start, BLK), :]
                z = _nt(qh, kblk)
                _, sp = _softplus_parts(z)
                if masked:
                    sp = jnp.where(causal, sp, 0.0)
                incl = _dot_hilo(sp, tri_m)
                a = jnp.exp(z - incl - carry)
                if masked:
                    a = jnp.where(causal, a, 0.0)
                acc = acc + _nn(a.astype(BF16), vblk)
                cvals = jnp.where(lane == kb + HEAD_DIM * half, carry, cvals)
                return carry + incl[:, 0:1], acc, cvals

            state = step(qi, jnp.zeros((BLK, 1), F32), jnp.zeros((BLK, LANES), F32),
                         jnp.zeros((BLK, LANES), F32), True)
            state = lax.fori_loop(0, qi, lambda it, st: step(qi - 1 - it, *st, False), state)
            o_tot = jnp.where(sel, state[1], o_tot)
            cs_tot = cs_tot + state[2]
        o_ref[...] = o_tot
        cs_ref[...] = cs_tot

    blk = pl.BlockSpec((BLK, LANES), lambda j, i: (i, j))
    col = pl.BlockSpec((T, LANES), lambda j, i: (0, j))
    out = jax.ShapeDtypeStruct((T, ATTN_DIM), F32)
    return pl.pallas_call(
        body, name=name, grid=(ATTN_DIM // LANES, T // BLK),
        in_specs=[blk, col, col, _whole((BLK, BLK))],
        out_specs=[blk, blk], out_shape=[out, out],
        compiler_params=_params(2),
    )(qs, k, v, tri)


def _conv_fwd(ug, dw_w, dw_b, ln_g, ln_b, name):
    T = ug.shape[0]
    per = TM // HALO

    def body(cv_ref, cg_ref, cvh_ref, cgh_ref, w_ref, b_ref, g_ref, beta_ref, conv_ref, c2_ref, pad_ref):
        i = pl.program_id(0)
        halo = cvh_ref[...] * _sigmoid(cgh_ref[...])
        pad_ref[0:HALO, :] = jnp.where(i == 0, 0.0, halo)
        pad_ref[HALO:, :] = cv_ref[...] * _sigmoid(cg_ref[...])
        acc = jnp.zeros((TM, CONV_DIM), F32) + b_ref[...]
        for t in range(CONV_WIDTH):
            lo = HALO - (CONV_WIDTH - 1) + t
            acc = acc + w_ref[t:t + 1, :] * pad_ref[lo:lo + TM, :]
        conv_ref[...] = acc
        mu = jnp.mean(acc, axis=-1, keepdims=True)
        xc = acc - mu
        rs = lax.rsqrt(jnp.mean(xc * xc, axis=-1, keepdims=True) + EPS)
        ln = xc * rs * g_ref[...] + beta_ref[...]
        c2_ref[...] = (ln * _sigmoid(ln)).astype(BF16)

    prev = lambda col: pl.BlockSpec((HALO, CHUNK), lambda i: (jnp.maximum(i * per - 1, 0), col))
    vec = _whole((1, CONV_DIM))
    return pl.pallas_call(
        body, name=name, grid=(T // TM,),
        in_specs=[_rows(TM, CHUNK, 1), _rows(TM, CHUNK, 2), prev(1), prev(2),
                  _whole((CONV_WIDTH, CONV_DIM)), vec, vec, vec],
        out_specs=[_rows(TM, CONV_DIM), _rows(TM, CONV_DIM)],
        out_shape=[jax.ShapeDtypeStruct((T, CONV_DIM), F32), jax.ShapeDtypeStruct((T, CONV_DIM), BF16)],
        scratch_shapes=[pltpu.VMEM((TM + HALO, CONV_DIM), F32)],
        compiler_params=_params(1),
    )(ug, ug, ug, ug, dw_w, dw_b, ln_g, ln_b)


def _mix_out_ple(o, ug, c2, h, p, head_mean, g_attn, g_conv, g_ple, w_pw, w_out, w_gate, w_ple, name):
    T = h.shape[0]

    def body(o_ref, ga_ref, gc_ref, c2_ref, h_ref, p_ref, hm_ref, gao_ref, gco_ref, gpn_ref,
             wpw_ref, wout_ref, wg_ref, wple_ref,
             h2_ref, h1_ref, ycat_ref, hn2_ref, gate_ref, e_ref, c3_ref):
        ov = o_ref[...]
        rh = lax.rsqrt(_dot_hilo(ov * ov, hm_ref[...]) + EPS)
        ga = ga_ref[...]
        ya = (ov * rh * gao_ref[...] * (ga * _sigmoid(ga))).astype(BF16)
        c3 = _nn(c2_ref[...], wpw_ref[...])
        c3_ref[...] = c3
        rc = lax.rsqrt(jnp.mean(c3 * c3, axis=-1, keepdims=True) + EPS)
        gc = gc_ref[...]
        yc = (c3 * rc * gco_ref[...] * (gc * _sigmoid(gc))).astype(BF16)
        ycat_ref[:, :ATTN_DIM] = ya
        ycat_ref[:, ATTN_DIM:] = yc
        h1 = h_ref[...] + _nn(ya, wout_ref[:ATTN_DIM, :]) + _nn(yc, wout_ref[ATTN_DIM:, :])
        h1_ref[...] = h1
        r1 = lax.rsqrt(jnp.mean(h1 * h1, axis=-1, keepdims=True) + EPS)
        hn2 = (h1 * r1 * gpn_ref[...]).astype(BF16)
        hn2_ref[...] = hn2
        gate = _sigmoid(_nn(hn2, wg_ref[...]))
        e = _nn(p_ref[...].astype(BF16), wple_ref[...])
        gate_ref[...] = gate
        e_ref[...] = e
        h2_ref[...] = h1 + e * gate

    f32 = lambda cols: jax.ShapeDtypeStruct((T, cols), F32)
    bf = lambda cols: jax.ShapeDtypeStruct((T, cols), BF16)
    return pl.pallas_call(
        body, name=name, grid=(T // TM,),
        in_specs=[_rows(TM, ATTN_DIM), _rows(TM, CHUNK, 0), _rows(TM, CHUNK, 3), _rows(TM, CONV_DIM),
                  _rows(TM, D_MODEL), _rows(TM, PLE_DIM), _whole((ATTN_DIM, ATTN_DIM)),
                  _whole((1, ATTN_DIM)), _whole((1, CONV_DIM)), _whole((1, D_MODEL)),
                  _whole((CONV_DIM, CONV_DIM)), _whole((D_MODEL, D_MODEL)), _whole((D_MODEL, D_MODEL)),
                  _whole((PLE_DIM, D_MODEL))],
        out_specs=[_rows(TM, D_MODEL), _rows(TM, D_MODEL), _rows(TM, D_MODEL), _rows(TM, D_MODEL),
                   _rows(TM, D_MODEL), _rows(TM, D_MODEL), _rows(TM, CONV_DIM)],
        out_shape=[f32(D_MODEL), f32(D_MODEL), bf(D_MODEL), bf(D_MODEL), f32(D_MODEL), f32(D_MODEL),
                   f32(CONV_DIM)],
        compiler_params=_params(1),
    )(o, ug, ug, c2, h, p, head_mean, g_attn, g_conv, g_ple, w_pw, w_out, w_gate, w_ple)


def _final_loss(h, target, gain, name):
    T = h.shape[0]

    def body(h_ref, t_ref, g_ref, dh_ref, gsum_ref, loss_ref):
        @pl.when(pl.program_id(0) == 0)
        def _():
            gsum_ref[...] = jnp.zeros_like(gsum_ref)
            loss_ref[...] = jnp.zeros_like(loss_ref)

        hv = h_ref[...]
        r = lax.rsqrt(jnp.mean(hv * hv, axis=-1, keepdims=True) + EPS)
        xh = hv * r
        diff = xh * g_ref[...] - t_ref[...]
        loss_ref[...] += 0.5 * jnp.sum(jnp.mean(diff * diff, axis=-1, keepdims=True), axis=0, keepdims=True)
        dy = diff * (1.0 / D_MODEL)
        gsum_ref[...] += jnp.sum(dy * xh, axis=0, keepdims=True)
        dxh = dy * g_ref[...]
        dh_ref[...] = r * (dxh - xh * jnp.mean(dxh * xh, axis=-1, keepdims=True))

    return pl.pallas_call(
        body, name=name, grid=(T // TM,),
        in_specs=[_rows(TM, D_MODEL), _rows(TM, D_MODEL), _whole((1, D_MODEL))],
        out_specs=[_rows(TM, D_MODEL), _whole((1, D_MODEL)), _whole((1, LANES))],
        out_shape=[jax.ShapeDtypeStruct((T, D_MODEL), F32), jax.ShapeDtypeStruct((1, D_MODEL), F32),
                   jax.ShapeDtypeStruct((1, LANES), F32)],
        compiler_params=_params(1),
    )(h, target, gain)


def _ple_out_bwd(dh2, gate, e, h1, g_ple, w_gate, w_out, name):
    T = dh2.shape[0]

    def body(dh2_ref, gate_ref, e_ref, h1_ref, gpn_ref, wg_ref, wout_ref,
             dh1_ref, dh1b_ref, dzg_ref, de_ref, dycat_ref, gsum_ref):
        @pl.when(pl.program_id(0) == 0)
        def _():
            gsum_ref[...] = jnp.zeros_like(gsum_ref)

        dh2v = dh2_ref[...]
        gate = gate_ref[...]
        de_ref[...] = (dh2v * gate).astype(BF16)
        dzg = (dh2v * e_ref[...] * gate * (1.0 - gate)).astype(BF16)
        dzg_ref[...] = dzg
        dhn2 = _nt(dzg, wg_ref[...])
        h1 = h1_ref[...]
        r1 = lax.rsqrt(jnp.mean(h1 * h1, axis=-1, keepdims=True) + EPS)
        xh = h1 * r1
        gsum_ref[...] += jnp.sum(dhn2 * xh, axis=0, keepdims=True)
        dxh = dhn2 * gpn_ref[...]
        dh1 = dh2v + r1 * (dxh - xh * jnp.mean(dxh * xh, axis=-1, keepdims=True))
        dh1_ref[...] = dh1
        dh1b = dh1.astype(BF16)
        dh1b_ref[...] = dh1b
        dycat_ref[...] = _nt(dh1b, wout_ref[...])

    f32 = jax.ShapeDtypeStruct((T, D_MODEL), F32)
    bf = jax.ShapeDtypeStruct((T, D_MODEL), BF16)
    full = _rows(TM, D_MODEL)
    return pl.pallas_call(
        body, name=name, grid=(T // TM,),
        in_specs=[full, full, full, full, _whole((1, D_MODEL)), _whole((D_MODEL, D_MODEL)),
                  _whole((D_MODEL, D_MODEL))],
        out_specs=[full, full, full, full, full, _whole((1, D_MODEL))],
        out_shape=[f32, bf, bf, bf, f32, jax.ShapeDtypeStruct((1, D_MODEL), F32)],
        compiler_params=_params(1),
    )(dh2, gate, e, h1, g_ple, w_gate, w_out)


def _branch_bwd(dycat, o, ug, c3, conv, head_mean, g_attn, g_conv, ln_g, ln_b, w_pw, name):
    T = o.shape[0]

    def body(dya_ref, dyc_ref, o_ref, ga_ref, gc_ref, c3_ref, conv_ref, hm_ref, gao_ref, gco_ref,
             lng_ref, lnb_ref, wpw_ref,
             do_ref, dga_ref, dgc_ref, dc3_ref, dconv_ref, sums_ref):
        @pl.when(pl.program_id(0) == 0)
        def _():
            sums_ref[...] = jnp.zeros_like(sums_ref)

        hm = hm_ref[...]
        col = lambda x: jnp.sum(x, axis=0, keepdims=True)
        ov = o_ref[...]
        rh = lax.rsqrt(_dot_hilo(ov * ov, hm) + EPS)
        xh = ov * rh
        ga = ga_ref[...]
        sg = _sigmoid(ga)
        dya = dya_ref[...]
        don = dya * (ga * sg)
        dga_ref[...] = (dya * xh * gao_ref[...] * _dsilu(ga, sg)).astype(BF16)
        sums_ref[0:1, :] += col(don * xh)
        dxh = don * gao_ref[...]
        do_ref[...] = (rh * (dxh - xh * _dot_hilo(dxh * xh, hm))).astype(BF16)
        c3 = c3_ref[...]
        rc = lax.rsqrt(jnp.mean(c3 * c3, axis=-1, keepdims=True) + EPS)
        xh3 = c3 * rc
        gc = gc_ref[...]
        sgc = _sigmoid(gc)
        dyc = dyc_ref[...]
        dn3 = dyc * (gc * sgc)
        dgc_ref[...] = (dyc * xh3 * gco_ref[...] * _dsilu(gc, sgc)).astype(BF16)
        sums_ref[1:2, :] += col(dn3 * xh3)
        dxh3 = dn3 * gco_ref[...]
        dc3 = (rc * (dxh3 - xh3 * jnp.mean(dxh3 * xh3, axis=-1, keepdims=True))).astype(BF16)
        dc3_ref[...] = dc3
        dc2 = _nt(dc3, wpw_ref[...])
        cv = conv_ref[...]
        mu = jnp.mean(cv, axis=-1, keepdims=True)
        xc = cv - mu
        rs = lax.rsqrt(jnp.mean(xc * xc, axis=-1, keepdims=True) + EPS)
        xn = xc * rs
        ln = xn * lng_ref[...] + lnb_ref[...]
        dln = dc2 * _dsilu(ln, _sigmoid(ln))
        sums_ref[2:3, :] += col(dln * xn)
        sums_ref[3:4, :] += col(dln)
        dxn = dln * lng_ref[...]
        dconv = rs * (dxn - jnp.mean(dxn, axis=-1, keepdims=True)
                      - xn * jnp.mean(dxn * xn, axis=-1, keepdims=True))
        dconv_ref[...] = dconv
        sums_ref[4:5, :] += col(dconv)

    half = lambda dt: jax.ShapeDtypeStruct((T, CHUNK), dt)
    tile = _rows(TM, CHUNK)
    vec = _whole((1, CHUNK))
    return pl.pallas_call(
        body, name=name, grid=(T // TM,),
        in_specs=[_rows(TM, CHUNK, 0), _rows(TM, CHUNK, 1), tile, _rows(TM, CHUNK, 0), _rows(TM, CHUNK, 3),
                  tile, tile, _whole((ATTN_DIM, ATTN_DIM)), vec, vec, vec, vec, _whole((CONV_DIM, CONV_DIM))],
        out_specs=[tile, tile, tile, tile, tile, _whole((8, CHUNK))],
        out_shape=[half(BF16), half(BF16), half(BF16), half(BF16), half(F32),
                   jax.ShapeDtypeStruct((8, CHUNK), F32)],
        compiler_params=_params(1),
    )(dycat, dycat, o, ug, ug, c3, conv, head_mean, g_attn, g_conv, ln_g, ln_b, w_pw)


def _conv_bwd(dconv, ug, dw_w, name):
    T = dconv.shape[0]
    per = TM // HALO
    last = T // HALO - 1

    def body(d_ref, dn_ref, cv_ref, cg_ref, cvh_ref, cgh_ref, w_ref, dcv_ref, dcg_ref, dw_ref, dpad_ref, cpad_ref):
        i = pl.program_id(0)

        @pl.when(i == 0)
        def _():
            dw_ref[...] = jnp.zeros_like(dw_ref)

        d = d_ref[...]
        dpad_ref[0:TM, :] = d
        dpad_ref[TM:, :] = jnp.where(i == pl.num_programs(0) - 1, 0.0, dn_ref[...])
        halo = cvh_ref[...] * _sigmoid(cgh_ref[...])
        cpad_ref[0:HALO, :] = jnp.where(i == 0, 0.0, halo)
        cv = cv_ref[...]
        sg = _sigmoid(cg_ref[...])
        cpad_ref[HALO:, :] = cv * sg
        dc = jnp.zeros((TM, CONV_DIM), F32)
        for t in range(CONV_WIDTH):
            up = CONV_WIDTH - 1 - t
            dc = dc + w_ref[t:t + 1, :] * dpad_ref[up:up + TM, :]
            lo = HALO - (CONV_WIDTH - 1) + t
            dw_ref[t:t + 1, :] += jnp.sum(d * cpad_ref[lo:lo + TM, :], axis=0, keepdims=True)
        dcv_ref[...] = (dc * sg).astype(BF16)
        dcg_ref[...] = (dc * cv * sg * (1.0 - sg)).astype(BF16)

    prev = lambda col: pl.BlockSpec((HALO, CHUNK), lambda i: (jnp.maximum(i * per - 1, 0), col))
    nxt = pl.BlockSpec((HALO, CONV_DIM), lambda i: (jnp.minimum((i + 1) * per, last), 0))
    half = jax.ShapeDtypeStruct((T, CHUNK), BF16)
    return pl.pallas_call(
        body, name=name, grid=(T // TM,),
        in_specs=[_rows(TM, CONV_DIM), nxt, _rows(TM, CHUNK, 1), _rows(TM, CHUNK, 2), prev(1), prev(2),
                  _whole((CONV_WIDTH, CONV_DIM))],
        out_specs=[_rows(TM, CHUNK), _rows(TM, CHUNK), _whole((HALO, CONV_DIM))],
        out_shape=[half, half, jax.ShapeDtypeStruct((HALO, CONV_DIM), F32)],
        scratch_shapes=[pltpu.VMEM((TM + HALO, CONV_DIM), F32), pltpu.VMEM((TM + HALO, CONV_DIM), F32)],
        compiler_params=_params(1),
    )(dconv, dconv, ug, ug, ug, ug, dw_w)


def _attn_bwd(qs, k, v, do, cs, tri, tri_t, name):
    T = qs.shape[0]
    nq = T // BLK

    def body(q_ref, k_ref, v_ref, do_ref, cs_ref, m_ref, mt_ref, dq_ref, dk_ref, dv_ref, dk_acc, dv_acc):
        qi = pl.program_id(1)

        @pl.when(qi == 0)
        def _():
            dk_acc[...] = jnp.zeros_like(dk_acc)
            dv_acc[...] = jnp.zeros_like(dv_acc)

        q = q_ref[...]
        dov = do_ref[...]
        cs = cs_ref[...]
        lane = lax.broadcasted_iota(jnp.int32, (BLK, LANES), 1)
        causal = (lax.broadcasted_iota(jnp.int32, (BLK, BLK), 1)
                  < lax.broadcasted_iota(jnp.int32, (BLK, BLK), 0))
        tri_m = m_ref[...]
        tri_mt = mt_ref[...]
        dq_tot = jnp.zeros((BLK, LANES), F32)
        for half in range(2):
            sel = (lane < HEAD_DIM) if half == 0 else (lane >= HEAD_DIM)
            qh = jnp.where(sel, q, jnp.zeros_like(q))
            doh = jnp.where(sel, dov, jnp.zeros_like(dov))

            def step(kb, prefix, dq_acc, masked, qh=qh, doh=doh, half=half):
                start = pl.multiple_of(kb * BLK, BLK)
                kblk = k_ref[pl.ds(start, BLK), :]
                vblk = v_ref[pl.ds(start, BLK), :]
                z = _nt(qh, kblk)
                e, sp = _softplus_parts(z)
                if masked:
                    sp = jnp.where(causal, sp, 0.0)
                incl = _dot_hilo(sp, tri_m)
                carry = jnp.sum(jnp.where(lane == kb + HEAD_DIM * half, cs, 0.0), axis=1, keepdims=True)
                a = jnp.exp(z - incl - carry)
                if masked:
                    a = jnp.where(causal, a, 0.0)
                beta = jnp.where(z >= 0.0, 1.0, e) / (1.0 + e)
                w = a * _nt(doh, vblk)
                pinc = _dot_hilo(w, tri_mt)
                dz = w - beta * (pinc + prefix)
                if masked:
                    dz = jnp.where(causal, dz, 0.0)
                dzb = dz.astype(BF16)
                dq_acc = dq_acc + _nn(dzb, kblk)
                dk_acc[pl.ds(start, BLK), :] += _tn(dzb, qh)
                dv_acc[pl.ds(start, BLK), :] += _tn(a.astype(BF16), doh)
                return prefix + pinc[:, BLK - 1:BLK], dq_acc

            state = (jnp.zeros((BLK, 1), F32), jnp.zeros((BLK, LANES), F32))
            state = lax.fori_loop(0, qi, lambda kb, st: step(kb, *st, False), state)
            state = step(qi, *state, True)
            dq_tot = jnp.where(sel, state[1], dq_tot)
        dq_ref[...] = (dq_tot * (HEAD_DIM ** -0.5)).astype(BF16)

        @pl.when(qi == nq - 1)
        def _():
            dk_ref[...] = dk_acc[...].astype(BF16)
            dv_ref[...] = dv_acc[...].astype(BF16)

    blk = pl.BlockSpec((BLK, LANES), lambda j, i: (i, j))
    col = pl.BlockSpec((T, LANES), lambda j, i: (0, j))
    out = jax.ShapeDtypeStruct((T, ATTN_DIM), BF16)
    return pl.pallas_call(
        body, name=name, grid=(ATTN_DIM // LANES, nq),
        in_specs=[blk, col, col, blk, blk, _whole((BLK, BLK)), _whole((BLK, BLK))],
        out_specs=[blk, col, col], out_shape=[out, out, out],
        scratch_shapes=[pltpu.VMEM((T, LANES), F32), pltpu.VMEM((T, LANES), F32)],
        compiler_params=_params(2),
    )(qs, k, v, do, cs, tri, tri_t)


def _inproj_bwd(du, w_in_t, h, dh1, gain, name):
    T = h.shape[0]

    def body(*refs):
        du_refs = refs[:N_CHUNK]
        w_ref, h_ref, dh1_ref, g_ref, dh_ref, gsum_ref = refs[N_CHUNK:]

        @pl.when(pl.program_id(0) == 0)
        def _():
            gsum_ref[...] = jnp.zeros_like(gsum_ref)

        dhn = jnp.zeros((TM, D_MODEL), F32)
        for j in range(N_CHUNK):
            dhn = dhn + _nn(du_refs[j][...], w_ref[j * CHUNK:(j + 1) * CHUNK, :])
        hv = h_ref[...]
        r = lax.rsqrt(jnp.mean(hv * hv, axis=-1, keepdims=True) + EPS)
        xh = hv * r
        gsum_ref[...] += jnp.sum(dhn * xh, axis=0, keepdims=True)
        dxh = dhn * g_ref[...]
        dh_ref[...] = dh1_ref[...] + r * (dxh - xh * jnp.mean(dxh * xh, axis=-1, keepdims=True))

    full = _rows(TM, D_MODEL)
    return pl.pallas_call(
        body, name=name, grid=(T // TM,),
        in_specs=[_rows(TM, CHUNK)] * N_CHUNK + [_whole((N_CHUNK * CHUNK, D_MODEL)), full, full,
                                                 _whole((1, D_MODEL))],
        out_specs=[full, _whole((1, D_MODEL))],
        out_shape=[jax.ShapeDtypeStruct((T, D_MODEL), F32), jax.ShapeDtypeStruct((1, D_MODEL), F32)],
        compiler_params=_params(1),
    )(*du, w_in_t, h, dh1, gain)


def _weight_grad(lhs_list, rhs, name, tk=CHUNK):
    T, n_rhs = rhs.shape
    n = len(lhs_list)
    ka = lhs_list[0].shape[1]
    per = ka // tk

    def body(*refs):
        a_refs, b_ref, out_ref = refs[:n], refs[n], refs[n + 1]
        step = pl.program_id(0)
        for j in range(n):
            for s in range(per):
                @pl.when(step == j * per + s)
                def _(j=j, s=s):
                    out_ref[...] = _tn(a_refs[j][:, s * tk:(s + 1) * tk], b_ref[...]).astype(BF16)

    return pl.pallas_call(
        body, name=name, grid=(n * per,),
        in_specs=[_whole((T, ka))] * n + [_whole((T, n_rhs))],
        out_specs=pl.BlockSpec((tk, n_rhs), lambda i: (i, 0)),
        out_shape=jax.ShapeDtypeStruct((n * ka, n_rhs), BF16),
        compiler_params=_params(1),
    )(*lhs_list, rhs)


def _sum_slots(slots, name):
    n = len(slots)

    def body(*refs):
        for src, dst in zip(refs[:n], refs[n:]):
            acc = src[0].astype(F32)
            for s in range(1, N_DEV):
                acc = acc + src[s].astype(F32)
            dst[...] = acc

    return pl.pallas_call(
        body, name=name,
        out_shape=[jax.ShapeDtypeStruct(s.shape[1:], F32) for s in slots],
        compiler_params=_params(),
    )(*slots)


def _adamw(w, g, m, v, name):
    R, C = w.shape
    tr = R
    for cand in (512, 256, 128, 64):
        if R % cand == 0 and R > cand:
            tr = cand
            break

    def body(w_ref, g_ref, m_ref, v_ref, d_ref, nm_ref, nv_ref):
        gv = g_ref[...]
        nm = ADAM_B1 * m_ref[...] + (1.0 - ADAM_B1) * gv
        nv = ADAM_B2 * v_ref[...] + (1.0 - ADAM_B2) * (gv * gv)
        m_hat = nm / (1.0 - ADAM_B1 ** ADAM_STEP)
        v_hat = nv / (1.0 - ADAM_B2 ** ADAM_STEP)
        d_ref[...] = -ADAM_LR * (m_hat / (jnp.sqrt(v_hat) + ADAM_EPS) + ADAM_WD * w_ref[...])
        nm_ref[...] = nm
        nv_ref[...] = nv

    spec = pl.BlockSpec((tr, C), lambda i: (i, 0))
    out = jax.ShapeDtypeStruct((R, C), F32)
    return pl.pallas_call(
        body, name=name, grid=(R // tr,),
        in_specs=[spec] * 4, out_specs=[spec] * 3, out_shape=[out, out, out],
        compiler_params=_params(1),
    )(w, g, m, v)


def _pack_small(norm_g, ple_norm_g, final_g, dw_b, conv_ln_g, conv_ln_b, conv_out_g, attn_out_g, scalar=None):
    flat = lambda a: a.reshape(1, -1)
    pad = lambda a: jnp.pad(a, ((0, 0), (0, D_MODEL - a.shape[1])))
    last = jnp.zeros((1, D_MODEL), F32) if scalar is None else pad(scalar.reshape(1, 1))
    rows = [norm_g, ple_norm_g, flat(final_g), flat(dw_b), flat(conv_ln_g), flat(conv_ln_b),
            flat(conv_out_g), pad(flat(attn_out_g)), last]
    used = sum(r.shape[0] for r in rows)
    return jnp.concatenate(rows + [jnp.zeros((SMALL_ROWS - used, D_MODEL), F32)], axis=0)


def _unpack_small(a):
    two = lambda r: a[r].reshape(2, -1)
    return dict(norm_g=a[0:2], ple_norm_g=a[2:4], final_g=a[4], dw_b=two(5), conv_ln_g=two(6),
                conv_ln_b=two(7), conv_out_g=two(8), attn_out_g=a[9, :2 * HEAD_DIM].reshape(2, HEAD_DIM))


def kernel(x, p, norm_g, w_in, attn_out_g, dw_w, dw_b, conv_ln_g, conv_ln_b, w_pw, conv_out_g, w_out, ple_norm_g, w_ple_gate, w_ple, final_g, loss_target, m_norm_g, m_w_in, m_attn_out_g, m_dw_w, m_dw_b, m_conv_ln_g, m_conv_ln_b, m_w_pw, m_conv_out_g, m_w_out, m_ple_norm_g, m_w_ple_gate, m_w_ple, m_final_g, v_norm_g, v_w_in, v_attn_out_g, v_dw_w, v_dw_b, v_conv_ln_g, v_conv_ln_b, v_w_pw, v_conv_out_g, v_w_out, v_ple_norm_g, v_w_ple_gate, v_w_ple, v_final_g):
    depth = w_in.shape[0]
    T = x.shape[1]
    my_idx = 4 * lax.axis_index("x") + 2 * lax.axis_index("y") + lax.axis_index("c")

    ids = jnp.arange(BLK)
    tri = (ids[:, None] >= ids[None, :]).astype(BF16)
    tri_t = (ids[:, None] <= ids[None, :]).astype(BF16)
    hid = jnp.arange(ATTN_DIM) // HEAD_DIM
    head_mean = ((hid[:, None] == hid[None, :]).astype(F32) / HEAD_DIM).astype(BF16)

    shards, axes = [], []
    for l in range(depth):
        shards += [w_in[l].T.astype(BF16), w_pw[l].astype(BF16), w_out[l].astype(BF16),
                   w_ple_gate[l].astype(BF16), w_ple[l].astype(BF16), dw_w[l].T]
        axes += [0, 0, 0, 0, 1, 0]
    gathered = _all_gather(shards, axes, "gather_weights")
    layers = []
    for l in range(depth):
        w_in_t, wpw, wout, wgate, wple, dww_t = gathered[6 * l:6 * l + 6]
        layers.append(dict(
            w_in_t=w_in_t, w_pw=wpw, w_out=wout, w_gate=wgate, w_ple=wple, dw_w=dww_t.T,
            g_norm=norm_g[l][None], g_attn=jnp.tile(attn_out_g[l], N_HEADS)[None], dw_b=dw_b[l][None],
            ln_g=conv_ln_g[l][None], ln_b=conv_ln_b[l][None], g_conv=conv_out_g[l][None],
            g_ple=ple_norm_g[l][None], p=p[l, 0]))

    h = x[0]
    saved = []
    for l, w in enumerate(layers):
        qs, k, v, ug, hn = _prenorm_inproj(h, w["g_norm"], w["w_in_t"], f"inproj_{l}")
        o, cs = _attn_fwd(qs, k, v, tri, f"attn_fwd_{l}")
        conv, c2 = _conv_fwd(ug, w["dw_w"], w["dw_b"], w["ln_g"], w["ln_b"], f"conv_fwd_{l}")
        h2, h1, ycat, hn2, gate, e, c3 = _mix_out_ple(
            o, ug, c2, h, w["p"], head_mean, w["g_attn"], w["g_conv"], w["g_ple"],
            w["w_pw"], w["w_out"], w["w_gate"], w["w_ple"], f"mix_{l}")
        saved.append(dict(h=h, qs=qs, k=k, v=v, ug=ug, hn=hn, o=o, cs=cs, conv=conv, c2=c2, h1=h1,
                          ycat=ycat, hn2=hn2, gate=gate, e=e, c3=c3))
        h = h2
    dh, g_final, loss_part = _final_loss(h, loss_target[0], final_g[None], "final_loss")

    small = {}
    dww_parts = [None] * depth
    slots = [None] * depth
    for l in reversed(range(depth)):
        w, s = layers[l], saved[l]
        dh1, dh1b, dzg, de, dycat, g_ple_sum = _ple_out_bwd(
            dh, s["gate"], s["e"], s["h1"], w["g_ple"], w["w_gate"], w["w_out"], f"ple_bwd_{l}")
        do, dga, dgc, dc3, dconv, sums = _branch_bwd(
            dycat, s["o"], s["ug"], s["c3"], s["conv"], head_mean, w["g_attn"], w["g_conv"],
            w["ln_g"], w["ln_b"], w["w_pw"], f"branch_bwd_{l}")
        dcv, dcg, dww = _conv_bwd(dconv, s["ug"], w["dw_w"], f"conv_bwd_{l}")
        dq, dk, dv = _attn_bwd(s["qs"], s["k"], s["v"], do, s["cs"], tri, tri_t, f"attn_bwd_{l}")
        du = [dq, dk, dv, dga, dcv, dcg, dgc]
        dh, g_norm_sum = _inproj_bwd(du, w["w_in_t"], s["h"], dh1, w["g_norm"], f"inproj_bwd_{l}")
        grads = [
            _weight_grad(du, s["hn"], f"grad_w_in_{l}"),
            _weight_grad([s["c2"]], dc3, f"grad_w_pw_{l}"),
            _weight_grad([s["ycat"]], dh1b, f"grad_w_out_{l}"),
            _weight_grad([s["hn2"]], dzg, f"grad_w_gate_{l}"),
            _weight_grad([w["p"].astype(BF16)], de, f"grad_w_ple_{l}", tk=PLE_DIM),
        ]
        slots[l] = _scatter_partials(grads, [0, 0, 0, 0, 1], f"scatter_grads_{l}")
        small[l] = dict(norm_g=g_norm_sum, ple_norm_g=g_ple_sum, attn_out_g=sums[0].reshape(N_HEADS, HEAD_DIM).sum(0),
                        conv_out_g=sums[1], conv_ln_g=sums[2], conv_ln_b=sums[3], dw_b=sums[4])
        dww_parts[l] = dww[:CONV_WIDTH]
    grad_x = dh[None]

    stack = lambda name: jnp.stack([small[l][name].reshape(-1) for l in range(depth)])
    small_part = _pack_small(stack("norm_g"), stack("ple_norm_g"), g_final[0], stack("dw_b"), stack("conv_ln_g"),
                             stack("conv_ln_b"), stack("conv_out_g"), stack("attn_out_g"), scalar=loss_part[0, 0])
    pack = jnp.concatenate([small_part, jnp.concatenate(dww_parts, axis=1),
                            jnp.zeros((PACK_ROWS - SMALL_ROWS - CONV_WIDTH, D_MODEL), F32)], axis=0)
    (all_packs,) = _all_gather([pack], [0], "gather_small_grads")
    (pack_sum,) = _sum_slots([all_packs.reshape(N_DEV, PACK_ROWS, D_MODEL)], "sum_small_grads")
    loss = pack_sum[SMALL_ROWS - 8 + 2, 0]
    g_small = pack_sum[:SMALL_ROWS].at[SMALL_ROWS - 8 + 2, 0].set(0.0)
    dww_full = pack_sum[SMALL_ROWS:SMALL_ROWS + CONV_WIDTH].reshape(CONV_WIDTH, depth, CONV_DIM).transpose(1, 0, 2)
    g_dw_w = lax.dynamic_slice_in_dim(dww_full, my_idx * (CONV_DIM // N_DEV), CONV_DIM // N_DEV, axis=2)

    w_small = _pack_small(norm_g, ple_norm_g, final_g, dw_b, conv_ln_g, conv_ln_b, conv_out_g, attn_out_g)
    m_small = _pack_small(m_norm_g, m_ple_norm_g, m_final_g, m_dw_b, m_conv_ln_g, m_conv_ln_b, m_conv_out_g, m_attn_out_g)
    v_small = _pack_small(v_norm_g, v_ple_norm_g, v_final_g, v_dw_b, v_conv_ln_g, v_conv_ln_b, v_conv_out_g, v_attn_out_g)
    d_small, nm_small, nv_small = _adamw(w_small, g_small, m_small, v_small, "adamw_small")
    res = {"g": _unpack_small(g_small), "d": _unpack_small(d_small), "m": _unpack_small(nm_small),
           "v": _unpack_small(nv_small)}

    sums = [_sum_slots(slots[l], f"sum_grads_{l}") for l in range(depth)]
    big = {
        "w_in": jnp.stack([sums[l][0].T for l in range(depth)]),
        "w_pw": jnp.stack([sums[l][1] for l in range(depth)]),
        "w_out": jnp.stack([sums[l][2] for l in range(depth)]),
        "w_ple_gate": jnp.stack([sums[l][3] for l in range(depth)]),
        "w_ple": jnp.stack([sums[l][4] for l in range(depth)]),
        "dw_w": g_dw_w,
    }
    state = {"w_in": (w_in, m_w_in, v_w_in), "w_pw": (w_pw, m_w_pw, v_w_pw), "w_out": (w_out, m_w_out, v_w_out),
             "w_ple_gate": (w_ple_gate, m_w_ple_gate, v_w_ple_gate), "w_ple": (w_ple, m_w_ple, v_w_ple),
             "dw_w": (dw_w, m_dw_w, v_dw_w)}
    for name, g in big.items():
        wv, mv, vv = state[name]
        flat = lambda a: a.reshape(-1, a.shape[-1])
        d, nm, nv = _adamw(flat(wv), flat(g), flat(mv), flat(vv), f"adamw_{name}")
        res["g"][name] = g
        res["d"][name], res["m"][name], res["v"][name] = (d.reshape(wv.shape), nm.reshape(wv.shape),
                                                         nv.reshape(wv.shape))

    order = ["norm_g", "w_in", "attn_out_g", "dw_w", "dw_b", "conv_ln_g", "conv_ln_b", "w_pw", "conv_out_g",
             "w_out", "ple_norm_g", "w_ple_gate", "w_ple", "final_g"]
    return (loss, grad_x, *[res["g"][n] for n in order], *[res["d"][n] for n in order],
            *[res["m"][n] for n in order], *[res["v"][n] for n in order])
```

```python
import functools

import jax
import jax.numpy as jnp
from jax import lax
from jax.experimental import pallas as pl
from jax.experimental.pallas import tpu as pltpu

F32 = jnp.float32
BF16 = jnp.bfloat16
MESH = pl.DeviceIdType.MESH

N_DEV = 8
D_MODEL = 1024
ATTN_DIM = 512
CONV_DIM = 512
HEAD_DIM = 64
N_HEADS = 8
CONV_WIDTH = 31
PLE_DIM = 256
CHUNK = 512
N_CHUNK = 7
EPS = 1e-6
ADAM_LR = 0.001
ADAM_B1 = 0.9
ADAM_B2 = 0.999
ADAM_EPS = 1e-08
ADAM_WD = 0.01
ADAM_STEP = 10

LANES = 128
BLK = 256
ATT_COLS = 2
TM = 256
HALO = 32
VMEM_LIMIT = 56 * 1024 * 1024
SMALL_ROWS = 16
PACK_ROWS = 48


def _nn(a, b):
    return lax.dot_general(a, b, (((1,), (0,)), ((), ())), preferred_element_type=F32)


def _nt(a, b):
    return lax.dot_general(a, b, (((1,), (1,)), ((), ())), preferred_element_type=F32)


def _tn(a, b):
    return lax.dot_general(a, b, (((0,), (0,)), ((), ())), preferred_element_type=F32)


def _split(x):
    hi = x.astype(BF16)
    lo = (x - hi.astype(F32)).astype(BF16)
    return hi, lo


def _dot_hilo(x, m):
    hi, lo = _split(x)
    return _nn(hi, m) + _nn(lo, m)


def _sigmoid(x):
    return jax.nn.sigmoid(x)


def _dsilu(x, s):
    return s * (1.0 + x * (1.0 - s))


def _params(n_grid=0, vmem=VMEM_LIMIT):
    sem = ("arbitrary",) * n_grid if n_grid else None
    return pltpu.CompilerParams(dimension_semantics=sem, vmem_limit_bytes=vmem)


def _rows(tm, cols, col=0):
    return pl.BlockSpec((tm, cols), lambda i: (i, col))


def _whole(shape):
    zeros = (0,) * len(shape)
    return pl.BlockSpec(shape, lambda *_: zeros)


def _my_position():
    return lax.axis_index("x"), lax.axis_index("y"), lax.axis_index("c")


def _block(ref, axis, idx, size):
    start = pl.multiple_of(idx * size, size)
    if axis == 0:
        return ref.at[pl.ds(start, size), :]
    return ref.at[:, pl.ds(start, size)]


def _all_gather(shards, axes, name):
    n = len(shards)
    sizes = [s.shape[a] for s, a in zip(shards, axes)]

    def full_shape(s, a):
        shape = list(s.shape)
        shape[a] *= N_DEV
        return jax.ShapeDtypeStruct(tuple(shape), s.dtype)

    def body(*refs):
        ins, outs = refs[:n], refs[n:2 * n]
        send_sems, recv_sems, local_sems = refs[2 * n:]
        x, y, c = _my_position()
        me, sibling = (x, y, c), (x, y, 1 - c)
        chips = [(1 - x, y), (x, 1 - y), (1 - x, 1 - y)]

        def place(i, dev):
            return _block(outs[i], axes[i], 4 * dev[0] + 2 * dev[1] + dev[2], sizes[i])

        def copy(k, i, dev, to, src=None):
            return pltpu.make_async_remote_copy(
                src_ref=place(i, dev) if src is None else src, dst_ref=place(i, dev),
                send_sem=send_sems.at[k, i], recv_sem=recv_sems.at[k, i],
                device_id=to, device_id_type=MESH)

        mine = [pltpu.make_async_copy(ins[i], place(i, me), local_sems.at[i]) for i in range(n)]
        for cp in mine:
            cp.start()
        first = [copy(0, i, me, sibling, src=ins[i]) for i in range(n)]
        for j, chip in enumerate(chips):
            first += [copy(1 + j, i, me, (*chip, c), src=ins[i]) for i in range(n)]
        for cp in first:
            cp.start()
        passed = []
        for j, chip in enumerate(chips):
            for i in range(n):
                copy(1 + j, i, (*chip, c), me).wait_recv()
            hop = [copy(4 + j, i, (*chip, c), sibling) for i in range(n)]
            for cp in hop:
                cp.start()
            passed += hop
        for i in range(n):
            copy(0, i, sibling, me).wait_recv()
        for j, chip in enumerate(chips):
            for i in range(n):
                copy(4 + j, i, (*chip, 1 - c), me).wait_recv()
        for cp in first + passed:
            cp.wait_send()
        for cp in mine:
            cp.wait()

    any_spec = pl.BlockSpec(memory_space=pl.ANY)
    return pl.pallas_call(
        body, name=name,
        out_shape=[full_shape(s, a) for s, a in zip(shards, axes)],
        in_specs=[any_spec] * n, out_specs=[any_spec] * n,
        scratch_shapes=[pltpu.SemaphoreType.DMA((7, n)), pltpu.SemaphoreType.DMA((7, n)),
                        pltpu.SemaphoreType.DMA((n,))],
    )(*shards)


def _scatter_partials(fulls, axes, name):
    n = len(fulls)
    sizes = [f.shape[a] // N_DEV for f, a in zip(fulls, axes)]

    def slot_shape(f, a):
        shape = list(f.shape)
        shape[a] //= N_DEV
        return jax.ShapeDtypeStruct((N_DEV, *shape), f.dtype)

    def body(*refs):
        ins, outs = refs[:n], refs[n:2 * n]
        send_sems, recv_sems, local_sems = refs[2 * n:]
        x, y, c = _my_position()
        my_idx = 4 * x + 2 * y + c

        def peer_of(k):
            px = 1 - x if k & 4 else x
            py = 1 - y if k & 2 else y
            pc = 1 - c if k & 1 else c
            return (px, py, pc), 4 * px + 2 * py + pc

        def copy(k, i):
            peer, peer_idx = peer_of(k)
            return pltpu.make_async_remote_copy(
                src_ref=_block(ins[i], axes[i], peer_idx, sizes[i]), dst_ref=outs[i].at[my_idx],
                send_sem=send_sems.at[k - 1, i], recv_sem=recv_sems.at[k - 1, i],
                device_id=peer, device_id_type=MESH)

        def arrival(k, i):
            peer, peer_idx = peer_of(k)
            return pltpu.make_async_remote_copy(
                src_ref=_block(ins[i], axes[i], my_idx, sizes[i]), dst_ref=outs[i].at[peer_idx],
                send_sem=send_sems.at[k - 1, i], recv_sem=recv_sems.at[k - 1, i],
                device_id=peer, device_id_type=MESH)

        mine = [pltpu.make_async_copy(_block(ins[i], axes[i], my_idx, sizes[i]), outs[i].at[my_idx],
                                      local_sems.at[i]) for i in range(n)]
        sends = [copy(k, i) for k in range(1, N_DEV) for i in range(n)]
        for cp in mine + sends:
            cp.start()
        for k in range(1, N_DEV):
            for i in range(n):
                arrival(k, i).wait_recv()
        for cp in sends:
            cp.wait_send()
        for cp in mine:
            cp.wait()

    any_spec = pl.BlockSpec(memory_space=pl.ANY)
    return pl.pallas_call(
        body, name=name,
        out_shape=[slot_shape(f, a) for f, a in zip(fulls, axes)],
        in_specs=[any_spec] * n, out_specs=[any_spec] * n,
        scratch_shapes=[pltpu.SemaphoreType.DMA((7, n)), pltpu.SemaphoreType.DMA((7, n)),
                        pltpu.SemaphoreType.DMA((n,))],
    )(*fulls)


def _prenorm_inproj(h, gain, w_in_t, name):
    T = h.shape[0]

    def body(h_ref, g_ref, w_ref, q_ref, k_ref, v_ref, ug_ref, hn_ref):
        hv = h_ref[...]
        r = lax.rsqrt(jnp.mean(hv * hv, axis=-1, keepdims=True) + EPS)
        hn = (hv * r * g_ref[...]).astype(BF16)
        hn_ref[...] = hn
        for j in range(N_CHUNK):
            u = _nt(hn, w_ref[j * CHUNK:(j + 1) * CHUNK, :])
            if j == 0:
                q_ref[...] = (u * (HEAD_DIM ** -0.5)).astype(BF16)
            elif j == 1:
                k_ref[...] = u.astype(BF16)
            elif j == 2:
                v_ref[...] = u.astype(BF16)
            else:
                ug_ref[:, (j - 3) * CHUNK:(j - 2) * CHUNK] = u

    act = jax.ShapeDtypeStruct((T, CHUNK), BF16)
    return pl.pallas_call(
        body, name=name, grid=(T // TM,),
        in_specs=[_rows(TM, D_MODEL), _whole((1, D_MODEL)), _whole((N_CHUNK * CHUNK, D_MODEL))],
        out_specs=[_rows(TM, CHUNK)] * 3 + [_rows(TM, 4 * CHUNK), _rows(TM, D_MODEL)],
        out_shape=[act, act, act, jax.ShapeDtypeStruct((T, 4 * CHUNK), F32),
                   jax.ShapeDtypeStruct((T, D_MODEL), BF16)],
        compiler_params=_params(1),
    )(h, gain, w_in_t)


def _softplus_parts(z):
    e = jnp.exp(-jnp.abs(z))
    return e, jnp.maximum(z, 0.0) + jnp.log(1.0 + e)


def _attn_fwd(qs, k, v, tri, name):
    T = qs.shape[0]
    width = LANES * ATT_COLS
    chains = [(c, half) for c in range(ATT_COLS) for half in range(2)]

    def body(q_ref, k_ref, v_ref, m_ref, o_ref, cs_ref):
        qi = pl.program_id(1)
        lane = lax.broadcasted_iota(jnp.int32, (BLK, LANES), 1)
        first = lane < HEAD_DIM
        causal = (lax.broadcasted_iota(jnp.int32, (BLK, BLK), 1)
                  < lax.broadcasted_iota(jnp.int32, (BLK, BLK), 0))
        tri_m = m_ref[...]
        qh = {}
        for c in range(ATT_COLS):
            q = q_ref[:, c * LANES:(c + 1) * LANES]
            zero = jnp.zeros_like(q)
            qh[c, 0], qh[c, 1] = jnp.where(first, q, zero), jnp.where(first, zero, q)

        def step(kb, state, masked):
            carries, accs, cvals = state
            start = pl.multiple_of(kb * BLK, BLK)
            kblk = [k_ref[pl.ds(start, BLK), c * LANES:(c + 1) * LANES] for c in range(ATT_COLS)]
            vblk = [v_ref[pl.ds(start, BLK), c * LANES:(c + 1) * LANES] for c in range(ATT_COLS)]
            z = [_nt(qh[ch], kblk[ch[0]]) for ch in chains]
            sp = [_softplus_parts(zi)[1] for zi in z]
            if masked:
                sp = [jnp.where(causal, s, 0.0) for s in sp]
            incl = [_dot_hilo(s, tri_m) for s in sp]
            a = [jnp.exp(zi - ii - ci) for zi, ii, ci in zip(z, incl, carries)]
            if masked:
                a = [jnp.where(causal, ai, 0.0) for ai in a]
            accs, cvals = list(accs), list(cvals)
            for n, (c, half) in enumerate(chains):
                zero = jnp.zeros_like(vblk[c])
                vh = jnp.where(first, vblk[c], zero) if half == 0 else jnp.where(first, zero, vblk[c])
                accs[c] = accs[c] + _nn(a[n].astype(BF16), vh)
                cvals[c] = jnp.where(lane == kb + HEAD_DIM * half, carries[n], cvals[c])
            carries = tuple(ci + ii[:, 0:1] for ci, ii in zip(carries, incl))
            return carries, tuple(accs), tuple(cvals)

        zeros = tuple(jnp.zeros((BLK, LANES), F32) for _ in range(ATT_COLS))
        state = (tuple(jnp.zeros((BLK, 1), F32) for _ in chains), zeros, zeros)
        state = step(qi, state, True)
        state = lax.fori_loop(0, qi, lambda it, st: step(qi - 1 - it, st, False), state)
        for c in range(ATT_COLS):
            o_ref[:, c * LANES:(c + 1) * LANES] = state[1][c]
            cs_ref[:, c * LANES:(c + 1) * LANES] = state[2][c]

    blk = pl.BlockSpec((BLK, width), lambda j, i: (i, j))
    col = pl.BlockSpec((T, width), lambda j, i: (0, j))
    out = jax.ShapeDtypeStruct((T, ATTN_DIM), F32)
    return pl.pallas_call(
        body, name=name, grid=(ATTN_DIM // width, T // BLK),
        in_specs=[blk, col, col, _whole((BLK, BLK))],
        out_specs=[blk, blk], out_shape=[out, out],
        compiler_params=_params(2),
    )(qs, k, v, tri)


def _conv_fwd(ug, dw_w, dw_b, ln_g, ln_b, name):
    T = ug.shape[0]
    per = TM // HALO

    def body(cv_ref, cg_ref, cvh_ref, cgh_ref, w_ref, b_ref, g_ref, beta_ref, conv_ref, c2_ref, pad_ref):
        i = pl.program_id(0)
        halo = cvh_ref[...] * _sigmoid(cgh_ref[...])
        pad_ref[0:HALO, :] = jnp.where(i == 0, 0.0, halo)
        pad_ref[HALO:, :] = cv_ref[...] * _sigmoid(cg_ref[...])
        acc = jnp.zeros((TM, CONV_DIM), F32) + b_ref[...]
        for t in range(CONV_WIDTH):
            lo = HALO - (CONV_WIDTH - 1) + t
            acc = acc + w_ref[t:t + 1, :] * pad_ref[lo:lo + TM, :]
        conv_ref[...] = acc
        mu = jnp.mean(acc, axis=-1, keepdims=True)
        xc = acc - mu
        rs = lax.rsqrt(jnp.mean(xc * xc, axis=-1, keepdims=True) + EPS)
        ln = xc * rs * g_ref[...] + beta_ref[...]
        c2_ref[...] = (ln * _sigmoid(ln)).astype(BF16)

    prev = lambda col: pl.BlockSpec((HALO, CHUNK), lambda i: (jnp.maximum(i * per - 1, 0), col))
    vec = _whole((1, CONV_DIM))
    return pl.pallas_call(
        body, name=name, grid=(T // TM,),
        in_specs=[_rows(TM, CHUNK, 1), _rows(TM, CHUNK, 2), prev(1), prev(2),
                  _whole((CONV_WIDTH, CONV_DIM)), vec, vec, vec],
        out_specs=[_rows(TM, CONV_DIM), _rows(TM, CONV_DIM)],
        out_shape=[jax.ShapeDtypeStruct((T, CONV_DIM), F32), jax.ShapeDtypeStruct((T, CONV_DIM), BF16)],
        scratch_shapes=[pltpu.VMEM((TM + HALO, CONV_DIM), F32)],
        compiler_params=_params(1),
    )(ug, ug, ug, ug, dw_w, dw_b, ln_g, ln_b)


def _mix_out_ple(o, ug, c2, h, p, head_mean, g_attn, g_conv, g_ple, w_pw, w_out, w_gate, w_ple, name):
    T = h.shape[0]

    def body(o_ref, ga_ref, gc_ref, c2_ref, h_ref, p_ref, hm_ref, gao_ref, gco_ref, gpn_ref,
             wpw_ref, wout_ref, wg_ref, wple_ref,
             h2_ref, h1_ref, ycat_ref, hn2_ref, gate_ref, e_ref, c3_ref):
        ov = o_ref[...]
        rh = lax.rsqrt(_dot_hilo(ov * ov, hm_ref[...]) + EPS)
        ga = ga_ref[...]
        ya = (ov * rh * gao_ref[...] * (ga * _sigmoid(ga))).astype(BF16)
        c3 = _nn(c2_ref[...], wpw_ref[...])
        c3_ref[...] = c3
        rc = lax.rsqrt(jnp.mean(c3 * c3, axis=-1, keepdims=True) + EPS)
        gc = gc_ref[...]
        yc = (c3 * rc * gco_ref[...] * (gc * _sigmoid(gc))).astype(BF16)
        ycat_ref[:, :ATTN_DIM] = ya
        ycat_ref[:, ATTN_DIM:] = yc
        h1 = h_ref[...] + _nn(ya, wout_ref[:ATTN_DIM, :]) + _nn(yc, wout_ref[ATTN_DIM:, :])
        h1_ref[...] = h1
        r1 = lax.rsqrt(jnp.mean(h1 * h1, axis=-1, keepdims=True) + EPS)
        hn2 = (h1 * r1 * gpn_ref[...]).astype(BF16)
        hn2_ref[...] = hn2
        gate = _sigmoid(_nn(hn2, wg_ref[...]))
        e = _nn(p_ref[...].astype(BF16), wple_ref[...])
        gate_ref[...] = gate
        e_ref[...] = e
        h2_ref[...] = h1 + e * gate

    f32 = lambda cols: jax.ShapeDtypeStruct((T, cols), F32)
    bf = lambda cols: jax.ShapeDtypeStruct((T, cols), BF16)
    return pl.pallas_call(
        body, name=name, grid=(T // TM,),
        in_specs=[_rows(TM, ATTN_DIM), _rows(TM, CHUNK, 0), _rows(TM, CHUNK, 3), _rows(TM, CONV_DIM),
                  _rows(TM, D_MODEL), _rows(TM, PLE_DIM), _whole((ATTN_DIM, ATTN_DIM)),
                  _whole((1, ATTN_DIM)), _whole((1, CONV_DIM)), _whole((1, D_MODEL)),
                  _whole((CONV_DIM, CONV_DIM)), _whole((D_MODEL, D_MODEL)), _whole((D_MODEL, D_MODEL)),
                  _whole((PLE_DIM, D_MODEL))],
        out_specs=[_rows(TM, D_MODEL), _rows(TM, D_MODEL), _rows(TM, D_MODEL), _rows(TM, D_MODEL),
                   _rows(TM, D_MODEL), _rows(TM, D_MODEL), _rows(TM, CONV_DIM)],
        out_shape=[f32(D_MODEL), f32(D_MODEL), bf(D_MODEL), bf(D_MODEL), f32(D_MODEL), f32(D_MODEL),
                   f32(CONV_DIM)],
        compiler_params=_params(1),
    )(o, ug, ug, c2, h, p, head_mean, g_attn, g_conv, g_ple, w_pw, w_out, w_gate, w_ple)


def _final_loss(h, target, gain, name):
    T = h.shape[0]

    def body(h_ref, t_ref, g_ref, dh_ref, gsum_ref, loss_ref):
        @pl.when(pl.program_id(0) == 0)
        def _():
            gsum_ref[...] = jnp.zeros_like(gsum_ref)
            loss_ref[...] = jnp.zeros_like(loss_ref)

        hv = h_ref[...]
        r = lax.rsqrt(jnp.mean(hv * hv, axis=-1, keepdims=True) + EPS)
        xh = hv * r
        diff = xh * g_ref[...] - t_ref[...]
        loss_ref[...] += 0.5 * jnp.sum(jnp.mean(diff * diff, axis=-1, keepdims=True), axis=0, keepdims=True)
        dy = diff * (1.0 / D_MODEL)
        gsum_ref[...] += jnp.sum(dy * xh, axis=0, keepdims=True)
        dxh = dy * g_ref[...]
        dh_ref[...] = r * (dxh - xh * jnp.mean(dxh * xh, axis=-1, keepdims=True))

    return pl.pallas_call(
        body, name=name, grid=(T // TM,),
        in_specs=[_rows(TM, D_MODEL), _rows(TM, D_MODEL), _whole((1, D_MODEL))],
        out_specs=[_rows(TM, D_MODEL), _whole((1, D_MODEL)), _whole((1, LANES))],
        out_shape=[jax.ShapeDtypeStruct((T, D_MODEL), F32), jax.ShapeDtypeStruct((1, D_MODEL), F32),
                   jax.ShapeDtypeStruct((1, LANES), F32)],
        compiler_params=_params(1),
    )(h, target, gain)


def _ple_out_bwd(dh2, gate, e, h1, g_ple, w_gate, w_out, name):
    T = dh2.shape[0]

    def body(dh2_ref, gate_ref, e_ref, h1_ref, gpn_ref, wg_ref, wout_ref,
             dh1_ref, dh1b_ref, dzg_ref, de_ref, dycat_ref, gsum_ref):
        @pl.when(pl.program_id(0) == 0)
        def _():
            gsum_ref[...] = jnp.zeros_like(gsum_ref)

        dh2v = dh2_ref[...]
        gate = gate_ref[...]
        de_ref[...] = (dh2v * gate).astype(BF16)
        dzg = (dh2v * e_ref[...] * gate * (1.0 - gate)).astype(BF16)
        dzg_ref[...] = dzg
        dhn2 = _nt(dzg, wg_ref[...])
        h1 = h1_ref[...]
        r1 = lax.rsqrt(jnp.mean(h1 * h1, axis=-1, keepdims=True) + EPS)
        xh = h1 * r1
        gsum_ref[...] += jnp.sum(dhn2 * xh, axis=0, keepdims=True)
        dxh = dhn2 * gpn_ref[...]
        dh1 = dh2v + r1 * (dxh - xh * jnp.mean(dxh * xh, axis=-1, keepdims=True))
        dh1_ref[...] = dh1
        dh1b = dh1.astype(BF16)
        dh1b_ref[...] = dh1b
        dycat_ref[...] = _nt(dh1b, wout_ref[...])

    f32 = jax.ShapeDtypeStruct((T, D_MODEL), F32)
    bf = jax.ShapeDtypeStruct((T, D_MODEL), BF16)
    full = _rows(TM, D_MODEL)
    return pl.pallas_call(
        body, name=name, grid=(T // TM,),
        in_specs=[full, full, full, full, _whole((1, D_MODEL)), _whole((D_MODEL, D_MODEL)),
                  _whole((D_MODEL, D_MODEL))],
        out_specs=[full, full, full, full, full, _whole((1, D_MODEL))],
        out_shape=[f32, bf, bf, bf, f32, jax.ShapeDtypeStruct((1, D_MODEL), F32)],
        compiler_params=_params(1),
    )(dh2, gate, e, h1, g_ple, w_gate, w_out)


def _branch_bwd(dycat, o, ug, c3, conv, head_mean, g_attn, g_conv, ln_g, ln_b, w_pw, name):
    T = o.shape[0]

    def body(dya_ref, dyc_ref, o_ref, ga_ref, gc_ref, c3_ref, conv_ref, hm_ref, gao_ref, gco_ref,
             lng_ref, lnb_ref, wpw_ref,
             do_ref, dga_ref, dgc_ref, dc3_ref, dconv_ref, sums_ref):
        @pl.when(pl.program_id(0) == 0)
        def _():
            sums_ref[...] = jnp.zeros_like(sums_ref)

        hm = hm_ref[...]
        col = lambda x: jnp.sum(x, axis=0, keepdims=True)
        ov = o_ref[...]
        rh = lax.rsqrt(_dot_hilo(ov * ov, hm) + EPS)
        xh = ov * rh
        ga = ga_ref[...]
        sg = _sigmoid(ga)
        dya = dya_ref[...]
        don = dya * (ga * sg)
        dga_ref[...] = (dya * xh * gao_ref[...] * _dsilu(ga, sg)).astype(BF16)
        sums_ref[0:1, :] += col(don * xh)
        dxh = don * gao_ref[...]
        do_ref[...] = (rh * (dxh - xh * _dot_hilo(dxh * xh, hm))).astype(BF16)
        c3 = c3_ref[...]
        rc = lax.rsqrt(jnp.mean(c3 * c3, axis=-1, keepdims=True) + EPS)
        xh3 = c3 * rc
        gc = gc_ref[...]
        sgc = _sigmoid(gc)
        dyc = dyc_ref[...]
        dn3 = dyc * (gc * sgc)
        dgc_ref[...] = (dyc * xh3 * gco_ref[...] * _dsilu(gc, sgc)).astype(BF16)
        sums_ref[1:2, :] += col(dn3 * xh3)
        dxh3 = dn3 * gco_ref[...]
        dc3 = (rc * (dxh3 - xh3 * jnp.mean(dxh3 * xh3, axis=-1, keepdims=True))).astype(BF16)
        dc3_ref[...] = dc3
        dc2 = _nt(dc3, wpw_ref[...])
        cv = conv_ref[...]
        mu = jnp.mean(cv, axis=-1, keepdims=True)
        xc = cv - mu
        rs = lax.rsqrt(jnp.mean(xc * xc, axis=-1, keepdims=True) + EPS)
        xn = xc * rs
        ln = xn * lng_ref[...] + lnb_ref[...]
        dln = dc2 * _dsilu(ln, _sigmoid(ln))
        sums_ref[2:3, :] += col(dln * xn)
        sums_ref[3:4, :] += col(dln)
        dxn = dln * lng_ref[...]
        dconv = rs * (dxn - jnp.mean(dxn, axis=-1, keepdims=True)
                      - xn * jnp.mean(dxn * xn, axis=-1, keepdims=True))
        dconv_ref[...] = dconv
        sums_ref[4:5, :] += col(dconv)

    half = lambda dt: jax.ShapeDtypeStruct((T, CHUNK), dt)
    tile = _rows(TM, CHUNK)
    vec = _whole((1, CHUNK))
    return pl.pallas_call(
        body, name=name, grid=(T // TM,),
        in_specs=[_rows(TM, CHUNK, 0), _rows(TM, CHUNK, 1), tile, _rows(TM, CHUNK, 0), _rows(TM, CHUNK, 3),
                  tile, tile, _whole((ATTN_DIM, ATTN_DIM)), vec, vec, vec, vec, _whole((CONV_DIM, CONV_DIM))],
        out_specs=[tile, tile, tile, tile, tile, _whole((8, CHUNK))],
        out_shape=[half(BF16), half(BF16), half(BF16), half(BF16), half(F32),
                   jax.ShapeDtypeStruct((8, CHUNK), F32)],
        compiler_params=_params(1),
    )(dycat, dycat, o, ug, ug, c3, conv, head_mean, g_attn, g_conv, ln_g, ln_b, w_pw)


def _conv_bwd(dconv, ug, dw_w, name):
    T = dconv.shape[0]
    per = TM // HALO
    last = T // HALO - 1

    def body(d_ref, dn_ref, cv_ref, cg_ref, cvh_ref, cgh_ref, w_ref, dcv_ref, dcg_ref, dw_ref, dpad_ref, cpad_ref):
        i = pl.program_id(0)

        @pl.when(i == 0)
        def _():
            dw_ref[...] = jnp.zeros_like(dw_ref)

        d = d_ref[...]
        dpad_ref[0:TM, :] = d
        dpad_ref[TM:, :] = jnp.where(i == pl.num_programs(0) - 1, 0.0, dn_ref[...])
        halo = cvh_ref[...] * _sigmoid(cgh_ref[...])
        cpad_ref[0:HALO, :] = jnp.where(i == 0, 0.0, halo)
        cv = cv_ref[...]
        sg = _sigmoid(cg_ref[...])
        cpad_ref[HALO:, :] = cv * sg
        dc = jnp.zeros((TM, CONV_DIM), F32)
        for t in range(CONV_WIDTH):
            up = CONV_WIDTH - 1 - t
            dc = dc + w_ref[t:t + 1, :] * dpad_ref[up:up + TM, :]
            lo = HALO - (CONV_WIDTH - 1) + t
            dw_ref[t:t + 1, :] += jnp.sum(d * cpad_ref[lo:lo + TM, :], axis=0, keepdims=True)
        dcv_ref[...] = (dc * sg).astype(BF16)
        dcg_ref[...] = (dc * cv * sg * (1.0 - sg)).astype(BF16)

    prev = lambda col: pl.BlockSpec((HALO, CHUNK), lambda i: (jnp.maximum(i * per - 1, 0), col))
    nxt = pl.BlockSpec((HALO, CONV_DIM), lambda i: (jnp.minimum((i + 1) * per, last), 0))
    half = jax.ShapeDtypeStruct((T, CHUNK), BF16)
    return pl.pallas_call(
        body, name=name, grid=(T // TM,),
        in_specs=[_rows(TM, CONV_DIM), nxt, _rows(TM, CHUNK, 1), _rows(TM, CHUNK, 2), prev(1), prev(2),
                  _whole((CONV_WIDTH, CONV_DIM))],
        out_specs=[_rows(TM, CHUNK), _rows(TM, CHUNK), _whole((HALO, CONV_DIM))],
        out_shape=[half, half, jax.ShapeDtypeStruct((HALO, CONV_DIM), F32)],
        scratch_shapes=[pltpu.VMEM((TM + HALO, CONV_DIM), F32), pltpu.VMEM((TM + HALO, CONV_DIM), F32)],
        compiler_params=_params(1),
    )(dconv, dconv, ug, ug, ug, ug, dw_w)


def _attn_bwd(qs, k, v, do, cs, tri, tri_t, name):
    T = qs.shape[0]
    nq = T // BLK
    width = LANES * ATT_COLS
    chains = [(c, half) for c in range(ATT_COLS) for half in range(2)]

    def body(q_ref, k_ref, v_ref, do_ref, cs_ref, m_ref, mt_ref, dq_ref, dk_ref, dv_ref, dk_acc, dv_acc):
        qi = pl.program_id(1)

        @pl.when(qi == 0)
        def _():
            dk_acc[...] = jnp.zeros_like(dk_acc)
            dv_acc[...] = jnp.zeros_like(dv_acc)

        lane = lax.broadcasted_iota(jnp.int32, (BLK, LANES), 1)
        first = lane < HEAD_DIM
        causal = (lax.broadcasted_iota(jnp.int32, (BLK, BLK), 1)
                  < lax.broadcasted_iota(jnp.int32, (BLK, BLK), 0))
        tri_m = m_ref[...]
        tri_mt = mt_ref[...]

        def halves(x):
            zero = jnp.zeros_like(x)
            return jnp.where(first, x, zero), jnp.where(first, zero, x)

        qh, doh, cs = {}, {}, []
        for c in range(ATT_COLS):
            qh[c, 0], qh[c, 1] = halves(q_ref[:, c * LANES:(c + 1) * LANES])
            doh[c, 0], doh[c, 1] = halves(do_ref[:, c * LANES:(c + 1) * LANES])
            cs.append(cs_ref[:, c * LANES:(c + 1) * LANES])

        def step(kb, state, masked):
            prefixes, dq_accs = state
            start = pl.multiple_of(kb * BLK, BLK)
            kblk = [k_ref[pl.ds(start, BLK), c * LANES:(c + 1) * LANES] for c in range(ATT_COLS)]
            vblk = [v_ref[pl.ds(start, BLK), c * LANES:(c + 1) * LANES] for c in range(ATT_COLS)]
            z = [_nt(qh[ch], kblk[ch[0]]) for ch in chains]
            da = [_nt(doh[ch], vblk[ch[0]]) for ch in chains]
            parts = [_softplus_parts(zi) for zi in z]
            sp = [pt[1] for pt in parts]
            if masked:
                sp = [jnp.where(causal, s, 0.0) for s in sp]
            incl = [_dot_hilo(s, tri_m) for s in sp]
            carries = [jnp.sum(jnp.where(lane == kb + HEAD_DIM * half, cs[c], 0.0), axis=1, keepdims=True)
                       for c, half in chains]
            a = [jnp.exp(zi - ii - ci) for zi, ii, ci in zip(z, incl, carries)]
            if masked:
                a = [jnp.where(causal, ai, 0.0) for ai in a]
            w = [ai * di for ai, di in zip(a, da)]
            pinc = [_dot_hilo(wi, tri_mt) for wi in w]
            beta = [jnp.where(zi >= 0.0, 1.0, pt[0]) / (1.0 + pt[0]) for zi, pt in zip(z, parts)]
            dz = [wi - bi * (pi + pre) for wi, bi, pi, pre in zip(w, beta, pinc, prefixes)]
            if masked:
                dz = [jnp.where(causal, d, 0.0) for d in dz]
            dq_accs = list(dq_accs)
            for c in range(ATT_COLS):
                k0, k1 = halves(kblk[c])
                dz0, dz1 = dz[2 * c].astype(BF16), dz[2 * c + 1].astype(BF16)
                a0, a1 = a[2 * c].astype(BF16), a[2 * c + 1].astype(BF16)
                dq_accs[c] = dq_accs[c] + _nn(dz0, k0) + _nn(dz1, k1)
                dk_acc[pl.ds(start, BLK), c * LANES:(c + 1) * LANES] += _tn(dz0, qh[c, 0]) + _tn(dz1, qh[c, 1])
                dv_acc[pl.ds(start, BLK), c * LANES:(c + 1) * LANES] += _tn(a0, doh[c, 0]) + _tn(a1, doh[c, 1])
            prefixes = tuple(pre + pi[:, BLK - 1:BLK] for pre, pi in zip(prefixes, pinc))
            return prefixes, tuple(dq_accs)

        state = (tuple(jnp.zeros((BLK, 1), F32) for _ in chains),
                 tuple(jnp.zeros((BLK, LANES), F32) for _ in range(ATT_COLS)))
        state = lax.fori_loop(0, qi, lambda kb, st: step(kb, st, False), state)
        state = step(qi, state, True)
        for c in range(ATT_COLS):
            dq_ref[:, c * LANES:(c + 1) * LANES] = (state[1][c] * (HEAD_DIM ** -0.5)).astype(BF16)

        @pl.when(qi == nq - 1)
        def _():
            dk_ref[...] = dk_acc[...].astype(BF16)
            dv_ref[...] = dv_acc[...].astype(BF16)

    blk = pl.BlockSpec((BLK, width), lambda j, i: (i, j))
    col = pl.BlockSpec((T, width), lambda j, i: (0, j))
    out = jax.ShapeDtypeStruct((T, ATTN_DIM), BF16)
    return pl.pallas_call(
        body, name=name, grid=(ATTN_DIM // width, nq),
        in_specs=[blk, col, col, blk, blk, _whole((BLK, BLK)), _whole((BLK, BLK))],
        out_specs=[blk, col, col], out_shape=[out, out, out],
        scratch_shapes=[pltpu.VMEM((T, width), F32), pltpu.VMEM((T, width), F32)],
        compiler_params=_params(2),
    )(qs, k, v, do, cs, tri, tri_t)


def _inproj_bwd(du, w_in_t, h, dh1, gain, name):
    T = h.shape[0]

    def body(*refs):
        du_refs = refs[:N_CHUNK]
        w_ref, h_ref, dh1_ref, g_ref, dh_ref, gsum_ref = refs[N_CHUNK:]

        @pl.when(pl.program_id(0) == 0)
        def _():
            gsum_ref[...] = jnp.zeros_like(gsum_ref)

        dhn = jnp.zeros((TM, D_MODEL), F32)
        for j in range(N_CHUNK):
            dhn = dhn + _nn(du_refs[j][...], w_ref[j * CHUNK:(j + 1) * CHUNK, :])
        hv = h_ref[...]
        r = lax.rsqrt(jnp.mean(hv * hv, axis=-1, keepdims=True) + EPS)
        xh = hv * r
        gsum_ref[...] += jnp.sum(dhn * xh, axis=0, keepdims=True)
        dxh = dhn * g_ref[...]
        dh_ref[...] = dh1_ref[...] + r * (dxh - xh * jnp.mean(dxh * xh, axis=-1, keepdims=True))

    full = _rows(TM, D_MODEL)
    return pl.pallas_call(
        body, name=name, grid=(T // TM,),
        in_specs=[_rows(TM, CHUNK)] * N_CHUNK + [_whole((N_CHUNK * CHUNK, D_MODEL)), full, full,
                                                 _whole((1, D_MODEL))],
        out_specs=[full, _whole((1, D_MODEL))],
        out_shape=[jax.ShapeDtypeStruct((T, D_MODEL), F32), jax.ShapeDtypeStruct((1, D_MODEL), F32)],
        compiler_params=_params(1),
    )(*du, w_in_t, h, dh1, gain)


def _weight_grad(lhs_list, rhs, name, tk=CHUNK):
    T, n_rhs = rhs.shape
    n = len(lhs_list)
    ka = lhs_list[0].shape[1]
    per = ka // tk

    def body(*refs):
        a_refs, b_ref, out_ref = refs[:n], refs[n], refs[n + 1]
        step = pl.program_id(0)
        for j in range(n):
            for s in range(per):
                @pl.when(step == j * per + s)
                def _(j=j, s=s):
                    out_ref[...] = _tn(a_refs[j][:, s * tk:(s + 1) * tk], b_ref[...]).astype(BF16)

    return pl.pallas_call(
        body, name=name, grid=(n * per,),
        in_specs=[_whole((T, ka))] * n + [_whole((T, n_rhs))],
        out_specs=pl.BlockSpec((tk, n_rhs), lambda i: (i, 0)),
        out_shape=jax.ShapeDtypeStruct((n * ka, n_rhs), BF16),
        compiler_params=_params(1),
    )(*lhs_list, rhs)


def _sum_slots(slots, name):
    n = len(slots)

    def body(*refs):
        for src, dst in zip(refs[:n], refs[n:]):
            acc = src[0].astype(F32)
            for s in range(1, N_DEV):
                acc = acc + src[s].astype(F32)
            dst[...] = acc

    return pl.pallas_call(
        body, name=name,
        out_shape=[jax.ShapeDtypeStruct(s.shape[1:], F32) for s in slots],
        compiler_params=_params(),
    )(*slots)


def _adamw(w, g, m, v, name):
    R, C = w.shape
    tr = R
    for cand in (512, 256, 128, 64):
        if R % cand == 0 and R > cand:
            tr = cand
            break

    def body(w_ref, g_ref, m_ref, v_ref, d_ref, nm_ref, nv_ref):
        gv = g_ref[...]
        nm = ADAM_B1 * m_ref[...] + (1.0 - ADAM_B1) * gv
        nv = ADAM_B2 * v_ref[...] + (1.0 - ADAM_B2) * (gv * gv)
        m_hat = nm / (1.0 - ADAM_B1 ** ADAM_STEP)
        v_hat = nv / (1.0 - ADAM_B2 ** ADAM_STEP)
        d_ref[...] = -ADAM_LR * (m_hat / (jnp.sqrt(v_hat) + ADAM_EPS) + ADAM_WD * w_ref[...])
        nm_ref[...] = nm
        nv_ref[...] = nv

    spec = pl.BlockSpec((tr, C), lambda i: (i, 0))
    out = jax.ShapeDtypeStruct((R, C), F32)
    return pl.pallas_call(
        body, name=name, grid=(R // tr,),
        in_specs=[spec] * 4, out_specs=[spec] * 3, out_shape=[out, out, out],
        compiler_params=_params(1),
    )(w, g, m, v)


def _pack_small(norm_g, ple_norm_g, final_g, dw_b, conv_ln_g, conv_ln_b, conv_out_g, attn_out_g, scalar=None):
    flat = lambda a: a.reshape(1, -1)
    pad = lambda a: jnp.pad(a, ((0, 0), (0, D_MODEL - a.shape[1])))
    last = jnp.zeros((1, D_MODEL), F32) if scalar is None else pad(scalar.reshape(1, 1))
    rows = [norm_g, ple_norm_g, flat(final_g), flat(dw_b), flat(conv_ln_g), flat(conv_ln_b),
            flat(conv_out_g), pad(flat(attn_out_g)), last]
    used = sum(r.shape[0] for r in rows)
    return jnp.concatenate(rows + [jnp.zeros((SMALL_ROWS - used, D_MODEL), F32)], axis=0)


def _unpack_small(a):
    two = lambda r: a[r].reshape(2, -1)
    return dict(norm_g=a[0:2], ple_norm_g=a[2:4], final_g=a[4], dw_b=two(5), conv_ln_g=two(6),
                conv_ln_b=two(7), conv_out_g=two(8), attn_out_g=a[9, :2 * HEAD_DIM].reshape(2, HEAD_DIM))


def kernel(x, p, norm_g, w_in, attn_out_g, dw_w, dw_b, conv_ln_g, conv_ln_b, w_pw, conv_out_g, w_out, ple_norm_g, w_ple_gate, w_ple, final_g, loss_target, m_norm_g, m_w_in, m_attn_out_g, m_dw_w, m_dw_b, m_conv_ln_g, m_conv_ln_b, m_w_pw, m_conv_out_g, m_w_out, m_ple_norm_g, m_w_ple_gate, m_w_ple, m_final_g, v_norm_g, v_w_in, v_attn_out_g, v_dw_w, v_dw_b, v_conv_ln_g, v_conv_ln_b, v_w_pw, v_conv_out_g, v_w_out, v_ple_norm_g, v_w_ple_gate, v_w_ple, v_final_g):
    depth = w_in.shape[0]
    T = x.shape[1]
    my_idx = 4 * lax.axis_index("x") + 2 * lax.axis_index("y") + lax.axis_index("c")

    ids = jnp.arange(BLK)
    tri = (ids[:, None] >= ids[None, :]).astype(BF16)
    tri_t = (ids[:, None] <= ids[None, :]).astype(BF16)
    hid = jnp.arange(ATTN_DIM) // HEAD_DIM
    head_mean = ((hid[:, None] == hid[None, :]).astype(F32) / HEAD_DIM).astype(BF16)

    shards, axes = [], []
    for l in range(depth):
        shards += [w_in[l].T.astype(BF16), w_pw[l].astype(BF16), w_out[l].astype(BF16),
                   w_ple_gate[l].astype(BF16), w_ple[l].astype(BF16), dw_w[l].T]
        axes += [0, 0, 0, 0, 1, 0]
    gathered = _all_gather(shards, axes, "gather_weights")
    layers = []
    for l in range(depth):
        w_in_t, wpw, wout, wgate, wple, dww_t = gathered[6 * l:6 * l + 6]
        layers.append(dict(
            w_in_t=w_in_t, w_pw=wpw, w_out=wout, w_gate=wgate, w_ple=wple, dw_w=dww_t.T,
            g_norm=norm_g[l][None], g_attn=jnp.tile(attn_out_g[l], N_HEADS)[None], dw_b=dw_b[l][None],
            ln_g=conv_ln_g[l][None], ln_b=conv_ln_b[l][None], g_conv=conv_out_g[l][None],
            g_ple=ple_norm_g[l][None], p=p[l, 0]))

    h = x[0]
    saved = []
    for l, w in enumerate(layers):
        qs, k, v, ug, hn = _prenorm_inproj(h, w["g_norm"], w["w_in_t"], f"inproj_{l}")
        o, cs = _attn_fwd(qs, k, v, tri, f"attn_fwd_{l}")
        conv, c2 = _conv_fwd(ug, w["dw_w"], w["dw_b"], w["ln_g"], w["ln_b"], f"conv_fwd_{l}")
        h2, h1, ycat, hn2, gate, e, c3 = _mix_out_ple(
            o, ug, c2, h, w["p"], head_mean, w["g_attn"], w["g_conv"], w["g_ple"],
            w["w_pw"], w["w_out"], w["w_gate"], w["w_ple"], f"mix_{l}")
        saved.append(dict(h=h, qs=qs, k=k, v=v, ug=ug, hn=hn, o=o, cs=cs, conv=conv, c2=c2, h1=h1,
                          ycat=ycat, hn2=hn2, gate=gate, e=e, c3=c3))
        h = h2
    dh, g_final, loss_part = _final_loss(h, loss_target[0], final_g[None], "final_loss")

    small = {}
    dww_parts = [None] * depth
    slots = [None] * depth
    for l in reversed(range(depth)):
        w, s = layers[l], saved[l]
        dh1, dh1b, dzg, de, dycat, g_ple_sum = _ple_out_bwd(
            dh, s["gate"], s["e"], s["h1"], w["g_ple"], w["w_gate"], w["w_out"], f"ple_bwd_{l}")
        do, dga, dgc, dc3, dconv, sums = _branch_bwd(
            dycat, s["o"], s["ug"], s["c3"], s["conv"], head_mean, w["g_attn"], w["g_conv"],
            w["ln_g"], w["ln_b"], w["w_pw"], f"branch_bwd_{l}")
        dcv, dcg, dww = _conv_bwd(dconv, s["ug"], w["dw_w"], f"conv_bwd_{l}")
        dq, dk, dv = _attn_bwd(s["qs"], s["k"], s["v"], do, s["cs"], tri, tri_t, f"attn_bwd_{l}")
        du = [dq, dk, dv, dga, dcv, dcg, dgc]
        dh, g_norm_sum = _inproj_bwd(du, w["w_in_t"], s["h"], dh1, w["g_norm"], f"inproj_bwd_{l}")
        grads = [
            _weight_grad(du, s["hn"], f"grad_w_in_{l}"),
            _weight_grad([s["c2"]], dc3, f"grad_w_pw_{l}"),
            _weight_grad([s["ycat"]], dh1b, f"grad_w_out_{l}"),
            _weight_grad([s["hn2"]], dzg, f"grad_w_gate_{l}"),
            _weight_grad([w["p"].astype(BF16)], de, f"grad_w_ple_{l}", tk=PLE_DIM),
        ]
        slots[l] = _scatter_partials(grads, [0, 0, 0, 0, 1], f"scatter_grads_{l}")
        small[l] = dict(norm_g=g_norm_sum, ple_norm_g=g_ple_sum, attn_out_g=sums[0].reshape(N_HEADS, HEAD_DIM).sum(0),
                        conv_out_g=sums[1], conv_ln_g=sums[2], conv_ln_b=sums[3], dw_b=sums[4])
        dww_parts[l] = dww[:CONV_WIDTH]
    grad_x = dh[None]

    stack = lambda name: jnp.stack([small[l][name].reshape(-1) for l in range(depth)])
    small_part = _pack_small(stack("norm_g"), stack("ple_norm_g"), g_final[0], stack("dw_b"), stack("conv_ln_g"),
                             stack("conv_ln_b"), stack("conv_out_g"), stack("attn_out_g"), scalar=loss_part[0, 0])
    pack = jnp.concatenate([small_part, jnp.concatenate(dww_parts, axis=1),
                            jnp.zeros((PACK_ROWS - SMALL_ROWS - CONV_WIDTH, D_MODEL), F32)], axis=0)
    (all_packs,) = _all_gather([pack], [0], "gather_small_grads")
    (pack_sum,) = _sum_slots([all_packs.reshape(N_DEV, PACK_ROWS, D_MODEL)], "sum_small_grads")
    loss = pack_sum[SMALL_ROWS - 8 + 2, 0]
    g_small = pack_sum[:SMALL_ROWS].at[SMALL_ROWS - 8 + 2, 0].set(0.0)
    dww_full = pack_sum[SMALL_ROWS:SMALL_ROWS + CONV_WIDTH].reshape(CONV_WIDTH, depth, CONV_DIM).transpose(1, 0, 2)
    g_dw_w = lax.dynamic_slice_in_dim(dww_full, my_idx * (CONV_DIM // N_DEV), CONV_DIM // N_DEV, axis=2)

    w_small = _pack_small(norm_g, ple_norm_g, final_g, dw_b, conv_ln_g, conv_ln_b, conv_out_g, attn_out_g)
    m_small = _pack_small(m_norm_g, m_ple_norm_g, m_final_g, m_dw_b, m_conv_ln_g, m_conv_ln_b, m_conv_out_g, m_attn_out_g)
    v_small = _pack_small(v_norm_g, v_ple_norm_g, v_final_g, v_dw_b, v_conv_ln_g, v_conv_ln_b, v_conv_out_g, v_attn_out_g)
    d_small, nm_small, nv_small = _adamw(w_small, g_small, m_small, v_small, "adamw_small")
    res = {"g": _unpack_small(g_small), "d": _unpack_small(d_small), "m": _unpack_small(nm_small),
           "v": _unpack_small(nv_small)}

    sums = [_sum_slots(slots[l], f"sum_grads_{l}") for l in range(depth)]
    big = {
        "w_in": jnp.stack([sums[l][0].T for l in range(depth)]),
        "w_pw": jnp.stack([sums[l][1] for l in range(depth)]),
        "w_out": jnp.stack([sums[l][2] for l in range(depth)]),
        "w_ple_gate": jnp.stack([sums[l][3] for l in range(depth)]),
        "w_ple": jnp.stack([sums[l][4] for l in range(depth)]),
        "dw_w": g_dw_w,
    }
    state = {"w_in": (w_in, m_w_in, v_w_in), "w_pw": (w_pw, m_w_pw, v_w_pw), "w_out": (w_out, m_w_out, v_w_out),
             "w_ple_gate": (w_ple_gate, m_w_ple_gate, v_w_ple_gate), "w_ple": (w_ple, m_w_ple, v_w_ple),
             "dw_w": (dw_w, m_dw_w, v_dw_w)}
    for name, g in big.items():
        wv, mv, vv = state[name]
        flat = lambda a: a.reshape(-1, a.shape[-1])
        d, nm, nv = _adamw(flat(wv), flat(g), flat(mv), flat(vv), f"adamw_{name}")
        res["g"][name] = g
        res["d"][name], res["m"][name], res["v"][name] = (d.reshape(wv.shape), nm.reshape(wv.shape),
                                                         nv.reshape(wv.shape))

    order = ["norm_g", "w_in", "attn_out_g", "dw_w", "dw_b", "conv_ln_g", "conv_ln_b", "w_pw", "conv_out_g",
             "w_out", "ple_norm_g", "w_ple_gate", "w_ple", "final_g"]
    return (loss, grad_x, *[res["g"][n] for n in order], *[res["d"][n] for n in order],
            *[res["m"][n] for n in order], *[res["v"][n] for n in order])
```

```python
import functools

import jax
import jax.numpy as jnp
from jax import lax
from jax.experimental import pallas as pl
from jax.experimental.pallas import tpu as pltpu

F32 = jnp.float32
BF16 = jnp.bfloat16
MESH = pl.DeviceIdType.MESH

N_DEV = 8
D_MODEL = 1024
ATTN_DIM = 512
CONV_DIM = 512
HEAD_DIM = 64
N_HEADS = 8
CONV_WIDTH = 31
PLE_DIM = 256
CHUNK = 512
N_CHUNK = 7
EPS = 1e-6
ADAM_LR = 0.001
ADAM_B1 = 0.9
ADAM_B2 = 0.999
ADAM_EPS = 1e-08
ADAM_WD = 0.01
ADAM_STEP = 10

LANES = 128
BLK = 256
ATT_COLS = 2
TM = 256
HALO = 32
VMEM_LIMIT = 56 * 1024 * 1024
SMALL_ROWS = 16
PACK_ROWS = 48


def _nn(a, b):
    return lax.dot_general(a, b, (((1,), (0,)), ((), ())), preferred_element_type=F32)


def _nt(a, b):
    return lax.dot_general(a, b, (((1,), (1,)), ((), ())), preferred_element_type=F32)


def _tn(a, b):
    return lax.dot_general(a, b, (((0,), (0,)), ((), ())), preferred_element_type=F32)


def _split(x):
    hi = x.astype(BF16)
    lo = (x - hi.astype(F32)).astype(BF16)
    return hi, lo


def _dot_hilo(x, m):
    hi, lo = _split(x)
    return _nn(hi, m) + _nn(lo, m)


def _sigmoid(x):
    return jax.nn.sigmoid(x)


def _dsilu(x, s):
    return s * (1.0 + x * (1.0 - s))


def _params(n_grid=0, vmem=VMEM_LIMIT):
    sem = ("arbitrary",) * n_grid if n_grid else None
    return pltpu.CompilerParams(dimension_semantics=sem, vmem_limit_bytes=vmem)


def _rows(tm, cols, col=0):
    return pl.BlockSpec((tm, cols), lambda i: (i, col))


def _whole(shape):
    zeros = (0,) * len(shape)
    return pl.BlockSpec(shape, lambda *_: zeros)


def _my_position():
    return lax.axis_index("x"), lax.axis_index("y"), lax.axis_index("c")


def _block(ref, axis, idx, size):
    start = pl.multiple_of(idx * size, size)
    if axis == 0:
        return ref.at[pl.ds(start, size), :]
    return ref.at[:, pl.ds(start, size)]


def _all_gather(shards, axes, name):
    n = len(shards)
    sizes = [s.shape[a] for s, a in zip(shards, axes)]

    def full_shape(s, a):
        shape = list(s.shape)
        shape[a] *= N_DEV
        return jax.ShapeDtypeStruct(tuple(shape), s.dtype)

    def body(*refs):
        ins, outs, token = refs[:n], refs[n:2 * n], refs[2 * n]
        send_sems, recv_sems, local_sems = refs[2 * n + 1:]
        token[...] = jnp.zeros_like(token)
        x, y, c = _my_position()
        me, sibling = (x, y, c), (x, y, 1 - c)
        chips = [(1 - x, y), (x, 1 - y), (1 - x, 1 - y)]

        def place(i, dev):
            return _block(outs[i], axes[i], 4 * dev[0] + 2 * dev[1] + dev[2], sizes[i])

        def copy(k, i, dev, to, src=None):
            return pltpu.make_async_remote_copy(
                src_ref=place(i, dev) if src is None else src, dst_ref=place(i, dev),
                send_sem=send_sems.at[k, i], recv_sem=recv_sems.at[k, i],
                device_id=to, device_id_type=MESH)

        mine = [pltpu.make_async_copy(ins[i], place(i, me), local_sems.at[i]) for i in range(n)]
        for cp in mine:
            cp.start()
        first = [copy(0, i, me, sibling, src=ins[i]) for i in range(n)]
        for j, chip in enumerate(chips):
            first += [copy(1 + j, i, me, (*chip, c), src=ins[i]) for i in range(n)]
        for cp in first:
            cp.start()
        passed = []
        for j, chip in enumerate(chips):
            for i in range(n):
                copy(1 + j, i, (*chip, c), me).wait_recv()
            hop = [copy(4 + j, i, (*chip, c), sibling) for i in range(n)]
            for cp in hop:
                cp.start()
            passed += hop
        for i in range(n):
            copy(0, i, sibling, me).wait_recv()
        for j, chip in enumerate(chips):
            for i in range(n):
                copy(4 + j, i, (*chip, 1 - c), me).wait_recv()
        for cp in first + passed:
            cp.wait_send()
        for cp in mine:
            cp.wait()

    any_spec = pl.BlockSpec(memory_space=pl.ANY)
    return pl.pallas_call(
        body, name=name,
        out_shape=[full_shape(s, a) for s, a in zip(shards, axes)] + [jax.ShapeDtypeStruct((8, LANES), F32)],
        in_specs=[any_spec] * n, out_specs=[any_spec] * n + [pl.BlockSpec(memory_space=pltpu.VMEM)],
        scratch_shapes=[pltpu.SemaphoreType.DMA((7, n)), pltpu.SemaphoreType.DMA((7, n)),
                        pltpu.SemaphoreType.DMA((n,))],
    )(*shards)


def _scatter_partials(fulls, axes, name):
    n = len(fulls)
    sizes = [f.shape[a] // N_DEV for f, a in zip(fulls, axes)]

    def slot_shape(f, a):
        shape = list(f.shape)
        shape[a] //= N_DEV
        return jax.ShapeDtypeStruct((N_DEV, *shape), f.dtype)

    def body(*refs):
        ins, outs = refs[:n], refs[n:2 * n]
        send_sems, recv_sems, local_sems = refs[2 * n:]
        x, y, c = _my_position()
        my_idx = 4 * x + 2 * y + c

        def peer_of(k):
            px = 1 - x if k & 4 else x
            py = 1 - y if k & 2 else y
            pc = 1 - c if k & 1 else c
            return (px, py, pc), 4 * px + 2 * py + pc

        def copy(k, i):
            peer, peer_idx = peer_of(k)
            return pltpu.make_async_remote_copy(
                src_ref=_block(ins[i], axes[i], peer_idx, sizes[i]), dst_ref=outs[i].at[my_idx],
                send_sem=send_sems.at[k - 1, i], recv_sem=recv_sems.at[k - 1, i],
                device_id=peer, device_id_type=MESH)

        def arrival(k, i):
            peer, peer_idx = peer_of(k)
            return pltpu.make_async_remote_copy(
                src_ref=_block(ins[i], axes[i], my_idx, sizes[i]), dst_ref=outs[i].at[peer_idx],
                send_sem=send_sems.at[k - 1, i], recv_sem=recv_sems.at[k - 1, i],
                device_id=peer, device_id_type=MESH)

        mine = [pltpu.make_async_copy(_block(ins[i], axes[i], my_idx, sizes[i]), outs[i].at[my_idx],
                                      local_sems.at[i]) for i in range(n)]
        sends = [copy(k, i) for k in range(1, N_DEV) for i in range(n)]
        for cp in mine + sends:
            cp.start()
        for k in range(1, N_DEV):
            for i in range(n):
                arrival(k, i).wait_recv()
        for cp in sends:
            cp.wait_send()
        for cp in mine:
            cp.wait()

    any_spec = pl.BlockSpec(memory_space=pl.ANY)
    return pl.pallas_call(
        body, name=name,
        out_shape=[slot_shape(f, a) for f, a in zip(fulls, axes)],
        in_specs=[any_spec] * n, out_specs=[any_spec] * n,
        scratch_shapes=[pltpu.SemaphoreType.DMA((7, n)), pltpu.SemaphoreType.DMA((7, n)),
                        pltpu.SemaphoreType.DMA((n,))],
    )(*fulls)


class _Exchange:
    def __init__(self, kind, srcs, lands, axes, name):
        self.kind, self.axes, self.name, self.n = kind, axes, name, len(srcs)
        if kind == "gather":
            self.sizes = [s.shape[a] for s, a in zip(srcs, axes)]
        else:
            self.sizes = [s.shape[a] // N_DEV for s, a in zip(srcs, axes)]
        self.shapes = [pltpu.HBM(a.shape, a.dtype) for a in (*srcs, *lands)]
        self._start(srcs, lands)

    def _copies(self, src_refs, land_refs, send_sems, recv_sems):
        x, y, c = _my_position()
        my_idx = 4 * x + 2 * y + c
        out = []
        for k in range(1, N_DEV):
            px = 1 - x if k & 4 else x
            py = 1 - y if k & 2 else y
            pc = 1 - c if k & 1 else c
            peer_idx = 4 * px + 2 * py + pc
            for i in range(self.n):
                blk = lambda ref, idx, i=i: _block(ref, self.axes[i], idx, self.sizes[i])
                if self.kind == "gather":
                    src, dst, landed = src_refs[i], blk(land_refs[i], my_idx), blk(land_refs[i], peer_idx)
                else:
                    src, dst, landed = blk(src_refs[i], peer_idx), land_refs[i].at[my_idx], land_refs[i].at[peer_idx]
                mk = lambda d, src=src, i=i, k=k: pltpu.make_async_remote_copy(
                    src_ref=src, dst_ref=d, send_sem=send_sems.at[(k - 1) * self.n + i],
                    recv_sem=recv_sems.at[(k - 1) * self.n + i],
                    device_id=(px, py, pc), device_id_type=MESH)
                out.append((mk(dst), mk(landed)))
        return out

    def _start(self, srcs, lands):
        n = self.n

        def body(*refs):
            for send, _ in self._copies(refs[:n], refs[n:2 * n], refs[2 * n], refs[2 * n + 1]):
                send.start()
            refs[-1][...] = jnp.zeros_like(refs[-1])

        hbm = pl.BlockSpec(memory_space=pltpu.HBM)
        sem = pl.BlockSpec(memory_space=pltpu.SEMAPHORE)
        sems = pltpu.SemaphoreType.DMA(((N_DEV - 1) * n,))
        res = pl.pallas_call(
            body, name=self.name + "_start",
            out_shape=(sems, sems, *self.shapes, jax.ShapeDtypeStruct((8, LANES), F32)),
            in_specs=[hbm] * (2 * n),
            out_specs=(sem, sem, *[hbm] * (2 * n), pl.BlockSpec(memory_space=pltpu.VMEM)),
            input_output_aliases={j: 2 + j for j in range(2 * n)},
            compiler_params=pltpu.CompilerParams(has_side_effects=pltpu.SideEffectType.DATAFLOW_SIDE_EFFECTING),
        )(*[pltpu.with_memory_space_constraint(a, pltpu.HBM) for a in (*srcs, *lands)])
        self.sems, self.thru, self.token = res[:2], res[2:2 + 2 * n], res[-1]

    def zero(self, dtype=F32):
        return self.token[0, 0].astype(dtype)

    def wait(self, after):
        n = self.n

        def body(*refs):
            for send, landed in self._copies(refs[:n], refs[n:2 * n], refs[2 * n], refs[2 * n + 1]):
                send.wait_send()
                landed.wait_recv()

        hbm = pl.BlockSpec(memory_space=pltpu.HBM)
        sem = pl.BlockSpec(memory_space=pltpu.SEMAPHORE)
        res = pl.pallas_call(
            body, name=self.name + "_wait", out_shape=tuple(self.shapes),
            in_specs=[hbm] * (2 * n) + [sem, sem, pl.BlockSpec(memory_space=pl.ANY)],
            out_specs=tuple([hbm] * (2 * n)),
            input_output_aliases={j: j for j in range(2 * n)},
            compiler_params=pltpu.CompilerParams(has_side_effects=pltpu.SideEffectType.DATAFLOW_SIDE_EFFECTING),
        )(*self.thru, *self.sems, after)
        return res[n:]


def _own_block_placed(kind, src, axis, my_idx):
    if kind == "gather":
        shape = list(src.shape)
        shape[axis] *= N_DEV
        start = [0, 0]
        start[axis] = my_idx * src.shape[axis]
        return lax.dynamic_update_slice(lax.empty(tuple(shape), src.dtype), src, tuple(start))
    size = src.shape[axis] // N_DEV
    own = lax.dynamic_slice_in_dim(src, my_idx * size, size, axis=axis)
    return lax.dynamic_update_slice(lax.empty((N_DEV, *own.shape), src.dtype), own[None], (my_idx, 0, 0))


def _prenorm_inproj(h, gain, w_in_t, name):
    T = h.shape[0]

    def body(h_ref, g_ref, w_ref, q_ref, k_ref, v_ref, ug_ref, hn_ref):
        hv = h_ref[...]
        r = lax.rsqrt(jnp.mean(hv * hv, axis=-1, keepdims=True) + EPS)
        hn = (hv * r * g_ref[...]).astype(BF16)
        hn_ref[...] = hn
        for j in range(N_CHUNK):
            u = _nt(hn, w_ref[j * CHUNK:(j + 1) * CHUNK, :])
            if j == 0:
                q_ref[...] = (u * (HEAD_DIM ** -0.5)).astype(BF16)
            elif j == 1:
                k_ref[...] = u.astype(BF16)
            elif j == 2:
                v_ref[...] = u.astype(BF16)
            else:
                ug_ref[:, (j - 3) * CHUNK:(j - 2) * CHUNK] = u

    act = jax.ShapeDtypeStruct((T, CHUNK), BF16)
    return pl.pallas_call(
        body, name=name, grid=(T // TM,),
        in_specs=[_rows(TM, D_MODEL), _whole((1, D_MODEL)), _whole((N_CHUNK * CHUNK, D_MODEL))],
        out_specs=[_rows(TM, CHUNK)] * 3 + [_rows(TM, 4 * CHUNK), _rows(TM, D_MODEL)],
        out_shape=[act, act, act, jax.ShapeDtypeStruct((T, 4 * CHUNK), F32),
                   jax.ShapeDtypeStruct((T, D_MODEL), BF16)],
        compiler_params=_params(1),
    )(h, gain, w_in_t)


def _softplus_parts(z):
    e = jnp.exp(-jnp.abs(z))
    return e, jnp.maximum(z, 0.0) + jnp.log(1.0 + e)


def _attn_fwd(qs, k, v, tri, name):
    T = qs.shape[0]
    width = LANES * ATT_COLS
    chains = [(c, half) for c in range(ATT_COLS) for half in range(2)]

    def body(q_ref, k_ref, v_ref, m_ref, o_ref, cs_ref):
        qi = pl.program_id(1)
        lane = lax.broadcasted_iota(jnp.int32, (BLK, LANES), 1)
        first = lane < HEAD_DIM
        causal = (lax.broadcasted_iota(jnp.int32, (BLK, BLK), 1)
                  < lax.broadcasted_iota(jnp.int32, (BLK, BLK), 0))
        tri_m = m_ref[...]
        qh = {}
        for c in range(ATT_COLS):
            q = q_ref[:, c * LANES:(c + 1) * LANES]
            zero = jnp.zeros_like(q)
            qh[c, 0], qh[c, 1] = jnp.where(first, q, zero), jnp.where(first, zero, q)

        def step(kb, state, masked):
            carries, accs, cvals = state
            start = pl.multiple_of(kb * BLK, BLK)
            kblk = [k_ref[pl.ds(start, BLK), c * LANES:(c + 1) * LANES] for c in range(ATT_COLS)]
            vblk = [v_ref[pl.ds(start, BLK), c * LANES:(c + 1) * LANES] for c in range(ATT_COLS)]
            z = [_nt(qh[ch], kblk[ch[0]]) for ch in chains]
            sp = [_softplus_parts(zi)[1] for zi in z]
            if masked:
                sp = [jnp.where(causal, s, 0.0) for s in sp]
            incl = [_dot_hilo(s, tri_m) for s in sp]
            a = [jnp.exp(zi - ii - ci) for zi, ii, ci in zip(z, incl, carries)]
            if masked:
                a = [jnp.where(causal, ai, 0.0) for ai in a]
            accs, cvals = list(accs), list(cvals)
            for n, (c, half) in enumerate(chains):
                zero = jnp.zeros_like(vblk[c])
                vh = jnp.where(first, vblk[c], zero) if half == 0 else jnp.where(first, zero, vblk[c])
                accs[c] = accs[c] + _nn(a[n].astype(BF16), vh)
                cvals[c] = jnp.where(lane == kb + HEAD_DIM * half, carries[n], cvals[c])
            carries = tuple(ci + ii[:, 0:1] for ci, ii in zip(carries, incl))
            return carries, tuple(accs), tuple(cvals)

        zeros = tuple(jnp.zeros((BLK, LANES), F32) for _ in range(ATT_COLS))
        state = (tuple(jnp.zeros((BLK, 1), F32) for _ in chains), zeros, zeros)
        state = step(qi, state, True)
        state = lax.fori_loop(0, qi, lambda it, st: step(qi - 1 - it, st, False), state)
        for c in range(ATT_COLS):
            o_ref[:, c * LANES:(c + 1) * LANES] = state[1][c]
            cs_ref[:, c * LANES:(c + 1) * LANES] = state[2][c]

    blk = pl.BlockSpec((BLK, width), lambda j, i: (i, j))
    col = pl.BlockSpec((T, width), lambda j, i: (0, j))
    out = jax.ShapeDtypeStruct((T, ATTN_DIM), F32)
    return pl.pallas_call(
        body, name=name, grid=(ATTN_DIM // width, T // BLK),
        in_specs=[blk, col, col, _whole((BLK, BLK))],
        out_specs=[blk, blk], out_shape=[out, out],
        compiler_params=_params(2),
    )(qs, k, v, tri)


def _conv_fwd(ug, dw_w, dw_b, ln_g, ln_b, name):
    T = ug.shape[0]
    per = TM // HALO

    def body(cv_ref, cg_ref, cvh_ref, cgh_ref, w_ref, b_ref, g_ref, beta_ref, conv_ref, c2_ref, pad_ref):
        i = pl.program_id(0)
        halo = cvh_ref[...] * _sigmoid(cgh_ref[...])
        pad_ref[0:HALO, :] = jnp.where(i == 0, 0.0, halo)
        pad_ref[HALO:, :] = cv_ref[...] * _sigmoid(cg_ref[...])
        acc = jnp.zeros((TM, CONV_DIM), F32) + b_ref[...]
        for t in range(CONV_WIDTH):
            lo = HALO - (CONV_WIDTH - 1) + t
            acc = acc + w_ref[t:t + 1, :] * pad_ref[lo:lo + TM, :]
        conv_ref[...] = acc
        mu = jnp.mean(acc, axis=-1, keepdims=True)
        xc = acc - mu
        rs = lax.rsqrt(jnp.mean(xc * xc, axis=-1, keepdims=True) + EPS)
        ln = xc * rs * g_ref[...] + beta_ref[...]
        c2_ref[...] = (ln * _sigmoid(ln)).astype(BF16)

    prev = lambda col: pl.BlockSpec((HALO, CHUNK), lambda i: (jnp.maximum(i * per - 1, 0), col))
    vec = _whole((1, CONV_DIM))
    return pl.pallas_call(
        body, name=name, grid=(T // TM,),
        in_specs=[_rows(TM, CHUNK, 1), _rows(TM, CHUNK, 2), prev(1), prev(2),
                  _whole((CONV_WIDTH, CONV_DIM)), vec, vec, vec],
        out_specs=[_rows(TM, CONV_DIM), _rows(TM, CONV_DIM)],
        out_shape=[jax.ShapeDtypeStruct((T, CONV_DIM), F32), jax.ShapeDtypeStruct((T, CONV_DIM), BF16)],
        scratch_shapes=[pltpu.VMEM((TM + HALO, CONV_DIM), F32)],
        compiler_params=_params(1),
    )(ug, ug, ug, ug, dw_w, dw_b, ln_g, ln_b)


def _mix_out_ple(o, ug, c2, h, p, head_mean, g_attn, g_conv, g_ple, w_pw, w_out, w_gate, w_ple, name):
    T = h.shape[0]

    def body(o_ref, ga_ref, gc_ref, c2_ref, h_ref, p_ref, hm_ref, gao_ref, gco_ref, gpn_ref,
             wpw_ref, wout_ref, wg_ref, wple_ref,
             h2_ref, h1_ref, ycat_ref, hn2_ref, gate_ref, e_ref, c3_ref):
        ov = o_ref[...]
        rh = lax.rsqrt(_dot_hilo(ov * ov, hm_ref[...]) + EPS)
        ga = ga_ref[...]
        ya = (ov * rh * gao_ref[...] * (ga * _sigmoid(ga))).astype(BF16)
        c3 = _nn(c2_ref[...], wpw_ref[...])
        c3_ref[...] = c3
        rc = lax.rsqrt(jnp.mean(c3 * c3, axis=-1, keepdims=True) + EPS)
        gc = gc_ref[...]
        yc = (c3 * rc * gco_ref[...] * (gc * _sigmoid(gc))).astype(BF16)
        ycat_ref[:, :ATTN_DIM] = ya
        ycat_ref[:, ATTN_DIM:] = yc
        h1 = h_ref[...] + _nn(ya, wout_ref[:ATTN_DIM, :]) + _nn(yc, wout_ref[ATTN_DIM:, :])
        h1_ref[...] = h1
        r1 = lax.rsqrt(jnp.mean(h1 * h1, axis=-1, keepdims=True) + EPS)
        hn2 = (h1 * r1 * gpn_ref[...]).astype(BF16)
        hn2_ref[...] = hn2
        gate = _sigmoid(_nn(hn2, wg_ref[...]))
        e = _nn(p_ref[...].astype(BF16), wple_ref[...])
        gate_ref[...] = gate
        e_ref[...] = e
        h2_ref[...] = h1 + e * gate

    f32 = lambda cols: jax.ShapeDtypeStruct((T, cols), F32)
    bf = lambda cols: jax.ShapeDtypeStruct((T, cols), BF16)
    return pl.pallas_call(
        body, name=name, grid=(T // TM,),
        in_specs=[_rows(TM, ATTN_DIM), _rows(TM, CHUNK, 0), _rows(TM, CHUNK, 3), _rows(TM, CONV_DIM),
                  _rows(TM, D_MODEL), _rows(TM, PLE_DIM), _whole((ATTN_DIM, ATTN_DIM)),
                  _whole((1, ATTN_DIM)), _whole((1, CONV_DIM)), _whole((1, D_MODEL)),
                  _whole((CONV_DIM, CONV_DIM)), _whole((D_MODEL, D_MODEL)), _whole((D_MODEL, D_MODEL)),
                  _whole((PLE_DIM, D_MODEL))],
        out_specs=[_rows(TM, D_MODEL), _rows(TM, D_MODEL), _rows(TM, D_MODEL), _rows(TM, D_MODEL),
                   _rows(TM, D_MODEL), _rows(TM, D_MODEL), _rows(TM, CONV_DIM)],
        out_shape=[f32(D_MODEL), f32(D_MODEL), bf(D_MODEL), bf(D_MODEL), f32(D_MODEL), f32(D_MODEL),
                   f32(CONV_DIM)],
        compiler_params=_params(1),
    )(o, ug, ug, c2, h, p, head_mean, g_attn, g_conv, g_ple, w_pw, w_out, w_gate, w_ple)


def _final_loss(h, target, gain, name):
    T = h.shape[0]

    def body(h_ref, t_ref, g_ref, dh_ref, gsum_ref, loss_ref):
        @pl.when(pl.program_id(0) == 0)
        def _():
            gsum_ref[...] = jnp.zeros_like(gsum_ref)
            loss_ref[...] = jnp.zeros_like(loss_ref)

        hv = h_ref[...]
        r = lax.rsqrt(jnp.mean(hv * hv, axis=-1, keepdims=True) + EPS)
        xh = hv * r
        diff = xh * g_ref[...] - t_ref[...]
        loss_ref[...] += 0.5 * jnp.sum(jnp.mean(diff * diff, axis=-1, keepdims=True), axis=0, keepdims=True)
        dy = diff * (1.0 / D_MODEL)
        gsum_ref[...] += jnp.sum(dy * xh, axis=0, keepdims=True)
        dxh = dy * g_ref[...]
        dh_ref[...] = r * (dxh - xh * jnp.mean(dxh * xh, axis=-1, keepdims=True))

    return pl.pallas_call(
        body, name=name, grid=(T // TM,),
        in_specs=[_rows(TM, D_MODEL), _rows(TM, D_MODEL), _whole((1, D_MODEL))],
        out_specs=[_rows(TM, D_MODEL), _whole((1, D_MODEL)), _whole((1, LANES))],
        out_shape=[jax.ShapeDtypeStruct((T, D_MODEL), F32), jax.ShapeDtypeStruct((1, D_MODEL), F32),
                   jax.ShapeDtypeStruct((1, LANES), F32)],
        compiler_params=_params(1),
    )(h, target, gain)


def _ple_out_bwd(dh2, gate, e, h1, g_ple, w_gate, w_out, name):
    T = dh2.shape[0]

    def body(dh2_ref, gate_ref, e_ref, h1_ref, gpn_ref, wg_ref, wout_ref,
             dh1_ref, dh1b_ref, dzg_ref, de_ref, dycat_ref, gsum_ref):
        @pl.when(pl.program_id(0) == 0)
        def _():
            gsum_ref[...] = jnp.zeros_like(gsum_ref)

        dh2v = dh2_ref[...]
        gate = gate_ref[...]
        de_ref[...] = (dh2v * gate).astype(BF16)
        dzg = (dh2v * e_ref[...] * gate * (1.0 - gate)).astype(BF16)
        dzg_ref[...] = dzg
        dhn2 = _nt(dzg, wg_ref[...])
        h1 = h1_ref[...]
        r1 = lax.rsqrt(jnp.mean(h1 * h1, axis=-1, keepdims=True) + EPS)
        xh = h1 * r1
        gsum_ref[...] += jnp.sum(dhn2 * xh, axis=0, keepdims=True)
        dxh = dhn2 * gpn_ref[...]
        dh1 = dh2v + r1 * (dxh - xh * jnp.mean(dxh * xh, axis=-1, keepdims=True))
        dh1_ref[...] = dh1
        dh1b = dh1.astype(BF16)
        dh1b_ref[...] = dh1b
        dycat_ref[...] = _nt(dh1b, wout_ref[...])

    f32 = jax.ShapeDtypeStruct((T, D_MODEL), F32)
    bf = jax.ShapeDtypeStruct((T, D_MODEL), BF16)
    full = _rows(TM, D_MODEL)
    return pl.pallas_call(
        body, name=name, grid=(T // TM,),
        in_specs=[full, full, full, full, _whole((1, D_MODEL)), _whole((D_MODEL, D_MODEL)),
                  _whole((D_MODEL, D_MODEL))],
        out_specs=[full, full, full, full, full, _whole((1, D_MODEL))],
        out_shape=[f32, bf, bf, bf, f32, jax.ShapeDtypeStruct((1, D_MODEL), F32)],
        compiler_params=_params(1),
    )(dh2, gate, e, h1, g_ple, w_gate, w_out)


def _branch_bwd(dycat, o, ug, c3, conv, head_mean, g_attn, g_conv, ln_g, ln_b, w_pw, name):
    T = o.shape[0]

    def body(dya_ref, dyc_ref, o_ref, ga_ref, gc_ref, c3_ref, conv_ref, hm_ref, gao_ref, gco_ref,
             lng_ref, lnb_ref, wpw_ref,
             do_ref, dga_ref, dgc_ref, dc3_ref, dconv_ref, sums_ref):
        @pl.when(pl.program_id(0) == 0)
        def _():
            sums_ref[...] = jnp.zeros_like(sums_ref)

        hm = hm_ref[...]
        col = lambda x: jnp.sum(x, axis=0, keepdims=True)
        ov = o_ref[...]
        rh = lax.rsqrt(_dot_hilo(ov * ov, hm) + EPS)
        xh = ov * rh
        ga = ga_ref[...]
        sg = _sigmoid(ga)
        dya = dya_ref[...]
        don = dya * (ga * sg)
        dga_ref[...] = (dya * xh * gao_ref[...] * _dsilu(ga, sg)).astype(BF16)
        sums_ref[0:1, :] += col(don * xh)
        dxh = don * gao_ref[...]
        do_ref[...] = (rh * (dxh - xh * _dot_hilo(dxh * xh, hm))).astype(BF16)
        c3 = c3_ref[...]
        rc = lax.rsqrt(jnp.mean(c3 * c3, axis=-1, keepdims=True) + EPS)
        xh3 = c3 * rc
        gc = gc_ref[...]
        sgc = _sigmoid(gc)
        dyc = dyc_ref[...]
        dn3 = dyc * (gc * sgc)
        dgc_ref[...] = (dyc * xh3 * gco_ref[...] * _dsilu(gc, sgc)).astype(BF16)
        sums_ref[1:2, :] += col(dn3 * xh3)
        dxh3 = dn3 * gco_ref[...]
        dc3 = (rc * (dxh3 - xh3 * jnp.mean(dxh3 * xh3, axis=-1, keepdims=True))).astype(BF16)
        dc3_ref[...] = dc3
        dc2 = _nt(dc3, wpw_ref[...])
        cv = conv_ref[...]
        mu = jnp.mean(cv, axis=-1, keepdims=True)
        xc = cv - mu
        rs = lax.rsqrt(jnp.mean(xc * xc, axis=-1, keepdims=True) + EPS)
        xn = xc * rs
        ln = xn * lng_ref[...] + lnb_ref[...]
        dln = dc2 * _dsilu(ln, _sigmoid(ln))
        sums_ref[2:3, :] += col(dln * xn)
        sums_ref[3:4, :] += col(dln)
        dxn = dln * lng_ref[...]
        dconv = rs * (dxn - jnp.mean(dxn, axis=-1, keepdims=True)
                      - xn * jnp.mean(dxn * xn, axis=-1, keepdims=True))
        dconv_ref[...] = dconv
        sums_ref[4:5, :] += col(dconv)

    half = lambda dt: jax.ShapeDtypeStruct((T, CHUNK), dt)
    tile = _rows(TM, CHUNK)
    vec = _whole((1, CHUNK))
    return pl.pallas_call(
        body, name=name, grid=(T // TM,),
        in_specs=[_rows(TM, CHUNK, 0), _rows(TM, CHUNK, 1), tile, _rows(TM, CHUNK, 0), _rows(TM, CHUNK, 3),
                  tile, tile, _whole((ATTN_DIM, ATTN_DIM)), vec, vec, vec, vec, _whole((CONV_DIM, CONV_DIM))],
        out_specs=[tile, tile, tile, tile, tile, _whole((8, CHUNK))],
        out_shape=[half(BF16), half(BF16), half(BF16), half(BF16), half(F32),
                   jax.ShapeDtypeStruct((8, CHUNK), F32)],
        compiler_params=_params(1),
    )(dycat, dycat, o, ug, ug, c3, conv, head_mean, g_attn, g_conv, ln_g, ln_b, w_pw)


def _conv_bwd(dconv, ug, dw_w, name):
    T = dconv.shape[0]
    per = TM // HALO
    last = T // HALO - 1

    def body(d_ref, dn_ref, cv_ref, cg_ref, cvh_ref, cgh_ref, w_ref, dcv_ref, dcg_ref, dw_ref, dpad_ref, cpad_ref):
        i = pl.program_id(0)

        @pl.when(i == 0)
        def _():
            dw_ref[...] = jnp.zeros_like(dw_ref)

        d = d_ref[...]
        dpad_ref[0:TM, :] = d
        dpad_ref[TM:, :] = jnp.where(i == pl.num_programs(0) - 1, 0.0, dn_ref[...])
        halo = cvh_ref[...] * _sigmoid(cgh_ref[...])
        cpad_ref[0:HALO, :] = jnp.where(i == 0, 0.0, halo)
        cv = cv_ref[...]
        sg = _sigmoid(cg_ref[...])
        cpad_ref[HALO:, :] = cv * sg
        dc = jnp.zeros((TM, CONV_DIM), F32)
        for t in range(CONV_WIDTH):
            up = CONV_WIDTH - 1 - t
            dc = dc + w_ref[t:t + 1, :] * dpad_ref[up:up + TM, :]
            lo = HALO - (CONV_WIDTH - 1) + t
            dw_ref[t:t + 1, :] += jnp.sum(d * cpad_ref[lo:lo + TM, :], axis=0, keepdims=True)
        dcv_ref[...] = (dc * sg).astype(BF16)
        dcg_ref[...] = (dc * cv * sg * (1.0 - sg)).astype(BF16)

    prev = lambda col: pl.BlockSpec((HALO, CHUNK), lambda i: (jnp.maximum(i * per - 1, 0), col))
    nxt = pl.BlockSpec((HALO, CONV_DIM), lambda i: (jnp.minimum((i + 1) * per, last), 0))
    half = jax.ShapeDtypeStruct((T, CHUNK), BF16)
    return pl.pallas_call(
        body, name=name, grid=(T // TM,),
        in_specs=[_rows(TM, CONV_DIM), nxt, _rows(TM, CHUNK, 1), _rows(TM, CHUNK, 2), prev(1), prev(2),
                  _whole((CONV_WIDTH, CONV_DIM))],
        out_specs=[_rows(TM, CHUNK), _rows(TM, CHUNK), _whole((HALO, CONV_DIM))],
        out_shape=[half, half, jax.ShapeDtypeStruct((HALO, CONV_DIM), F32)],
        scratch_shapes=[pltpu.VMEM((TM + HALO, CONV_DIM), F32), pltpu.VMEM((TM + HALO, CONV_DIM), F32)],
        compiler_params=_params(1),
    )(dconv, dconv, ug, ug, ug, ug, dw_w)


def _attn_bwd(qs, k, v, do, cs, tri, tri_t, name):
    T = qs.shape[0]
    nq = T // BLK
    width = LANES * ATT_COLS
    chains = [(c, half) for c in range(ATT_COLS) for half in range(2)]

    def body(q_ref, k_ref, v_ref, do_ref, cs_ref, m_ref, mt_ref, dq_ref, dk_ref, dv_ref, dk_acc, dv_acc):
        qi = pl.program_id(1)

        @pl.when(qi == 0)
        def _():
            dk_acc[...] = jnp.zeros_like(dk_acc)
            dv_acc[...] = jnp.zeros_like(dv_acc)

        lane = lax.broadcasted_iota(jnp.int32, (BLK, LANES), 1)
        first = lane < HEAD_DIM
        causal = (lax.broadcasted_iota(jnp.int32, (BLK, BLK), 1)
                  < lax.broadcasted_iota(jnp.int32, (BLK, BLK), 0))
        tri_m = m_ref[...]
        tri_mt = mt_ref[...]

        def halves(x):
            zero = jnp.zeros_like(x)
            return jnp.where(first, x, zero), jnp.where(first, zero, x)

        qh, doh, cs = {}, {}, []
        for c in range(ATT_COLS):
            qh[c, 0], qh[c, 1] = halves(q_ref[:, c * LANES:(c + 1) * LANES])
            doh[c, 0], doh[c, 1] = halves(do_ref[:, c * LANES:(c + 1) * LANES])
            cs.append(cs_ref[:, c * LANES:(c + 1) * LANES])

        def step(kb, state, masked):
            prefixes, dq_accs = state
            start = pl.multiple_of(kb * BLK, BLK)
            kblk = [k_ref[pl.ds(start, BLK), c * LANES:(c + 1) * LANES] for c in range(ATT_COLS)]
            vblk = [v_ref[pl.ds(start, BLK), c * LANES:(c + 1) * LANES] for c in range(ATT_COLS)]
            z = [_nt(qh[ch], kblk[ch[0]]) for ch in chains]
            da = [_nt(doh[ch], vblk[ch[0]]) for ch in chains]
            parts = [_softplus_parts(zi) for zi in z]
            sp = [pt[1] for pt in parts]
            if masked:
                sp = [jnp.where(causal, s, 0.0) for s in sp]
            incl = [_dot_hilo(s, tri_m) for s in sp]
            carries = [jnp.sum(jnp.where(lane == kb + HEAD_DIM * half, cs[c], 0.0), axis=1, keepdims=True)
                       for c, half in chains]
            a = [jnp.exp(zi - ii - ci) for zi, ii, ci in zip(z, incl, carries)]
            if masked:
                a = [jnp.where(causal, ai, 0.0) for ai in a]
            w = [ai * di for ai, di in zip(a, da)]
            pinc = [_dot_hilo(wi, tri_mt) for wi in w]
            beta = [jnp.where(zi >= 0.0, 1.0, pt[0]) / (1.0 + pt[0]) for zi, pt in zip(z, parts)]
            dz = [wi - bi * (pi + pre) for wi, bi, pi, pre in zip(w, beta, pinc, prefixes)]
            if masked:
                dz = [jnp.where(causal, d, 0.0) for d in dz]
            dq_accs = list(dq_accs)
            for c in range(ATT_COLS):
                k0, k1 = halves(kblk[c])
                dz0, dz1 = dz[2 * c].astype(BF16), dz[2 * c + 1].astype(BF16)
                a0, a1 = a[2 * c].astype(BF16), a[2 * c + 1].astype(BF16)
                dq_accs[c] = dq_accs[c] + _nn(dz0, k0) + _nn(dz1, k1)
                dk_acc[pl.ds(start, BLK), c * LANES:(c + 1) * LANES] += _tn(dz0, qh[c, 0]) + _tn(dz1, qh[c, 1])
                dv_acc[pl.ds(start, BLK), c * LANES:(c + 1) * LANES] += _tn(a0, doh[c, 0]) + _tn(a1, doh[c, 1])
            prefixes = tuple(pre + pi[:, BLK - 1:BLK] for pre, pi in zip(prefixes, pinc))
            return prefixes, tuple(dq_accs)

        state = (tuple(jnp.zeros((BLK, 1), F32) for _ in chains),
                 tuple(jnp.zeros((BLK, LANES), F32) for _ in range(ATT_COLS)))
        state = lax.fori_loop(0, qi, lambda kb, st: step(kb, st, False), state)
        state = step(qi, state, True)
        for c in range(ATT_COLS):
            dq_ref[:, c * LANES:(c + 1) * LANES] = (state[1][c] * (HEAD_DIM ** -0.5)).astype(BF16)

        @pl.when(qi == nq - 1)
        def _():
            dk_ref[...] = dk_acc[...].astype(BF16)
            dv_ref[...] = dv_acc[...].astype(BF16)

    blk = pl.BlockSpec((BLK, width), lambda j, i: (i, j))
    col = pl.BlockSpec((T, width), lambda j, i: (0, j))
    out = jax.ShapeDtypeStruct((T, ATTN_DIM), BF16)
    return pl.pallas_call(
        body, name=name, grid=(ATTN_DIM // width, nq),
        in_specs=[blk, col, col, blk, blk, _whole((BLK, BLK)), _whole((BLK, BLK))],
        out_specs=[blk, col, col], out_shape=[out, out, out],
        scratch_shapes=[pltpu.VMEM((T, width), F32), pltpu.VMEM((T, width), F32)],
        compiler_params=_params(2),
    )(qs, k, v, do, cs, tri, tri_t)


def _inproj_bwd(du, w_in_t, h, dh1, gain, name):
    T = h.shape[0]

    def body(*refs):
        du_refs = refs[:N_CHUNK]
        w_ref, h_ref, dh1_ref, g_ref, dh_ref, gsum_ref = refs[N_CHUNK:]

        @pl.when(pl.program_id(0) == 0)
        def _():
            gsum_ref[...] = jnp.zeros_like(gsum_ref)

        dhn = jnp.zeros((TM, D_MODEL), F32)
        for j in range(N_CHUNK):
            dhn = dhn + _nn(du_refs[j][...], w_ref[j * CHUNK:(j + 1) * CHUNK, :])
        hv = h_ref[...]
        r = lax.rsqrt(jnp.mean(hv * hv, axis=-1, keepdims=True) + EPS)
        xh = hv * r
        gsum_ref[...] += jnp.sum(dhn * xh, axis=0, keepdims=True)
        dxh = dhn * g_ref[...]
        dh_ref[...] = dh1_ref[...] + r * (dxh - xh * jnp.mean(dxh * xh, axis=-1, keepdims=True))

    full = _rows(TM, D_MODEL)
    return pl.pallas_call(
        body, name=name, grid=(T // TM,),
        in_specs=[_rows(TM, CHUNK)] * N_CHUNK + [_whole((N_CHUNK * CHUNK, D_MODEL)), full, full,
                                                 _whole((1, D_MODEL))],
        out_specs=[full, _whole((1, D_MODEL))],
        out_shape=[jax.ShapeDtypeStruct((T, D_MODEL), F32), jax.ShapeDtypeStruct((1, D_MODEL), F32)],
        compiler_params=_params(1),
    )(*du, w_in_t, h, dh1, gain)


def _weight_grad(lhs_list, rhs, name, tk=CHUNK):
    T, n_rhs = rhs.shape
    n = len(lhs_list)
    ka = lhs_list[0].shape[1]
    per = ka // tk

    def body(*refs):
        a_refs, b_ref, out_ref = refs[:n], refs[n], refs[n + 1]
        step = pl.program_id(0)
        for j in range(n):
            for s in range(per):
                @pl.when(step == j * per + s)
                def _(j=j, s=s):
                    out_ref[...] = _tn(a_refs[j][:, s * tk:(s + 1) * tk], b_ref[...]).astype(BF16)

    return pl.pallas_call(
        body, name=name, grid=(n * per,),
        in_specs=[_whole((T, ka))] * n + [_whole((T, n_rhs))],
        out_specs=pl.BlockSpec((tk, n_rhs), lambda i: (i, 0)),
        out_shape=jax.ShapeDtypeStruct((n * ka, n_rhs), BF16),
        compiler_params=_params(1),
    )(*lhs_list, rhs)


def _sum_slots(slots, name):
    n = len(slots)

    def body(*refs):
        for src, dst in zip(refs[:n], refs[n:]):
            acc = src[0].astype(F32)
            for s in range(1, N_DEV):
                acc = acc + src[s].astype(F32)
            dst[...] = acc

    return pl.pallas_call(
        body, name=name,
        out_shape=[jax.ShapeDtypeStruct(s.shape[1:], F32) for s in slots],
        compiler_params=_params(),
    )(*slots)


def _adamw(w, g, m, v, name):
    R, C = w.shape
    tr = R
    for cand in (512, 256, 128, 64):
        if R % cand == 0 and R > cand:
            tr = cand
            break

    def body(w_ref, g_ref, m_ref, v_ref, d_ref, nm_ref, nv_ref):
        gv = g_ref[...]
        nm = ADAM_B1 * m_ref[...] + (1.0 - ADAM_B1) * gv
        nv = ADAM_B2 * v_ref[...] + (1.0 - ADAM_B2) * (gv * gv)
        m_hat = nm / (1.0 - ADAM_B1 ** ADAM_STEP)
        v_hat = nv / (1.0 - ADAM_B2 ** ADAM_STEP)
        d_ref[...] = -ADAM_LR * (m_hat / (jnp.sqrt(v_hat) + ADAM_EPS) + ADAM_WD * w_ref[...])
        nm_ref[...] = nm
        nv_ref[...] = nv

    spec = pl.BlockSpec((tr, C), lambda i: (i, 0))
    out = jax.ShapeDtypeStruct((R, C), F32)
    return pl.pallas_call(
        body, name=name, grid=(R // tr,),
        in_specs=[spec] * 4, out_specs=[spec] * 3, out_shape=[out, out, out],
        compiler_params=_params(1),
    )(w, g, m, v)


def _pack_small(norm_g, ple_norm_g, final_g, dw_b, conv_ln_g, conv_ln_b, conv_out_g, attn_out_g, scalar=None):
    flat = lambda a: a.reshape(1, -1)
    pad = lambda a: jnp.pad(a, ((0, 0), (0, D_MODEL - a.shape[1])))
    last = jnp.zeros((1, D_MODEL), F32) if scalar is None else pad(scalar.reshape(1, 1))
    rows = [norm_g, ple_norm_g, flat(final_g), flat(dw_b), flat(conv_ln_g), flat(conv_ln_b),
            flat(conv_out_g), pad(flat(attn_out_g)), last]
    used = sum(r.shape[0] for r in rows)
    return jnp.concatenate(rows + [jnp.zeros((SMALL_ROWS - used, D_MODEL), F32)], axis=0)


def _unpack_small(a):
    two = lambda r: a[r].reshape(2, -1)
    return dict(norm_g=a[0:2], ple_norm_g=a[2:4], final_g=a[4], dw_b=two(5), conv_ln_g=two(6),
                conv_ln_b=two(7), conv_out_g=two(8), attn_out_g=a[9, :2 * HEAD_DIM].reshape(2, HEAD_DIM))


def kernel(x, p, norm_g, w_in, attn_out_g, dw_w, dw_b, conv_ln_g, conv_ln_b, w_pw, conv_out_g, w_out, ple_norm_g, w_ple_gate, w_ple, final_g, loss_target, m_norm_g, m_w_in, m_attn_out_g, m_dw_w, m_dw_b, m_conv_ln_g, m_conv_ln_b, m_w_pw, m_conv_out_g, m_w_out, m_ple_norm_g, m_w_ple_gate, m_w_ple, m_final_g, v_norm_g, v_w_in, v_attn_out_g, v_dw_w, v_dw_b, v_conv_ln_g, v_conv_ln_b, v_w_pw, v_conv_out_g, v_w_out, v_ple_norm_g, v_w_ple_gate, v_w_ple, v_final_g):
    depth = w_in.shape[0]
    T = x.shape[1]
    my_idx = 4 * lax.axis_index("x") + 2 * lax.axis_index("y") + lax.axis_index("c")

    ids = jnp.arange(BLK)
    tri = (ids[:, None] >= ids[None, :]).astype(BF16)
    tri_t = (ids[:, None] <= ids[None, :]).astype(BF16)
    hid = jnp.arange(ATTN_DIM) // HEAD_DIM
    head_mean = ((hid[:, None] == hid[None, :]).astype(F32) / HEAD_DIM).astype(BF16)

    w_names = ("w_in_t", "w_pw", "w_out", "w_gate", "w_ple")
    w_axes = [0, 0, 0, 0, 1]

    def shards_of(l, zero=None):
        out = [w_in[l].T.astype(BF16), w_pw[l].astype(BF16), w_out[l].astype(BF16),
               w_ple_gate[l].astype(BF16), w_ple[l].astype(BF16)]
        return out if zero is None else [s + zero.astype(BF16) for s in out]

    *first, gathered_zero = _all_gather(shards_of(0) + [dw_w[l].T for l in range(depth)],
                                        w_axes + [0] * depth, "gather_weights_0")
    gathers = []
    for l in range(1, depth):
        srcs = shards_of(l, gathered_zero[0, 0])
        lands = [_own_block_placed("gather", s, a, my_idx) for s, a in zip(srcs, w_axes)]
        gathers.append(_Exchange("gather", srcs, lands, w_axes, f"gather_weights_{l}"))
    layers = []
    for l in range(depth):
        layers.append(dict(
            dw_w=first[len(w_names) + l].T,
            g_norm=norm_g[l][None], g_attn=jnp.tile(attn_out_g[l], N_HEADS)[None], dw_b=dw_b[l][None],
            ln_g=conv_ln_g[l][None], ln_b=conv_ln_b[l][None], g_conv=conv_out_g[l][None],
            g_ple=ple_norm_g[l][None], p=p[l, 0]))
    layers[0].update(zip(w_names, first))
    for g in gathers:
        layers[0]["g_norm"] = layers[0]["g_norm"] + g.zero()

    h = x[0]
    saved = []
    for l, w in enumerate(layers):
        if l > 0:
            w.update(zip(w_names, gathers[l - 1].wait(h)))
        qs, k, v, ug, hn = _prenorm_inproj(h, w["g_norm"], w["w_in_t"], f"inproj_{l}")
        o, cs = _attn_fwd(qs, k, v, tri, f"attn_fwd_{l}")
        conv, c2 = _conv_fwd(ug, w["dw_w"], w["dw_b"], w["ln_g"], w["ln_b"], f"conv_fwd_{l}")
        h2, h1, ycat, hn2, gate, e, c3 = _mix_out_ple(
            o, ug, c2, h, w["p"], head_mean, w["g_attn"], w["g_conv"], w["g_ple"],
            w["w_pw"], w["w_out"], w["w_gate"], w["w_ple"], f"mix_{l}")
        saved.append(dict(h=h, qs=qs, k=k, v=v, ug=ug, hn=hn, o=o, cs=cs, conv=conv, c2=c2, h1=h1,
                          ycat=ycat, hn2=hn2, gate=gate, e=e, c3=c3))
        h = h2
    dh, g_final, loss_part = _final_loss(h, loss_target[0], final_g[None], "final_loss")

    small = {}
    dww_parts = [None] * depth
    slots = [None] * depth
    scatters = []
    started = jnp.zeros((), F32)

    def scatter_later(l, at, grads, axes, name):
        lands = [_own_block_placed("scatter", g, a, my_idx) for g, a in zip(grads, axes)]
        ex = _Exchange("scatter", grads, lands, axes, name)
        scatters.append((l, at, ex))
        return ex.zero()

    for l in reversed(range(depth)):
        w, s = layers[l], saved[l]
        dh1, dh1b, dzg, de, dycat, g_ple_sum = _ple_out_bwd(
            dh, s["gate"], s["e"], s["h1"], w["g_ple"] + started, w["w_gate"], w["w_out"], f"ple_bwd_{l}")
        do, dga, dgc, dc3, dconv, sums = _branch_bwd(
            dycat, s["o"], s["ug"], s["c3"], s["conv"], head_mean, w["g_attn"], w["g_conv"],
            w["ln_g"], w["ln_b"], w["w_pw"], f"branch_bwd_{l}")
        rest = [
            _weight_grad([s["c2"]], dc3, f"grad_w_pw_{l}"),
            _weight_grad([s["ycat"]], dh1b, f"grad_w_out_{l}"),
            _weight_grad([s["hn2"]], dzg, f"grad_w_gate_{l}"),
            _weight_grad([w["p"].astype(BF16)], de, f"grad_w_ple_{l}", tk=PLE_DIM),
        ]
        tri_l = tri
        if l == 0:
            tri_l = tri + scatter_later(l, 1, rest, w_axes[1:], f"scatter_rest_{l}").astype(BF16)
        dcv, dcg, dww = _conv_bwd(dconv, s["ug"], w["dw_w"], f"conv_bwd_{l}")
        dq, dk, dv = _attn_bwd(s["qs"], s["k"], s["v"], do, s["cs"], tri_l, tri_t, f"attn_bwd_{l}")
        du = [dq, dk, dv, dga, dcv, dcg, dgc]
        dh, g_norm_sum = _inproj_bwd(du, w["w_in_t"], s["h"], dh1, w["g_norm"], f"inproj_bwd_{l}")
        g_w_in = _weight_grad(du, s["hn"], f"grad_w_in_{l}")
        slots[l] = [None] * len(w_names)
        if l == 0:
            slots[l][0] = _scatter_partials([g_w_in], w_axes[:1], f"scatter_w_in_{l}")[0]
        else:
            started = scatter_later(l, 0, [g_w_in] + rest, w_axes, f"scatter_grads_{l}")
        small[l] = dict(norm_g=g_norm_sum, ple_norm_g=g_ple_sum, attn_out_g=sums[0].reshape(N_HEADS, HEAD_DIM).sum(0),
                        conv_out_g=sums[1], conv_ln_g=sums[2], conv_ln_b=sums[3], dw_b=sums[4])
        dww_parts[l] = dww[:CONV_WIDTH]
    grad_x = dh[None]
    for l, at, ex in scatters:
        landed = ex.wait(slots[0][0])
        slots[l][at:at + len(landed)] = landed

    stack = lambda name: jnp.stack([small[l][name].reshape(-1) for l in range(depth)])
    small_part = _pack_small(stack("norm_g"), stack("ple_norm_g"), g_final[0], stack("dw_b"), stack("conv_ln_g"),
                             stack("conv_ln_b"), stack("conv_out_g"), stack("attn_out_g"), scalar=loss_part[0, 0])
    pack = jnp.concatenate([small_part, jnp.concatenate(dww_parts, axis=1),
                            jnp.zeros((PACK_ROWS - SMALL_ROWS - CONV_WIDTH, D_MODEL), F32)], axis=0)
    all_packs, _ = _all_gather([pack], [0], "gather_small_grads")
    (pack_sum,) = _sum_slots([all_packs.reshape(N_DEV, PACK_ROWS, D_MODEL)], "sum_small_grads")
    loss = pack_sum[SMALL_ROWS - 8 + 2, 0]
    g_small = pack_sum[:SMALL_ROWS].at[SMALL_ROWS - 8 + 2, 0].set(0.0)
    dww_full = pack_sum[SMALL_ROWS:SMALL_ROWS + CONV_WIDTH].reshape(CONV_WIDTH, depth, CONV_DIM).transpose(1, 0, 2)
    g_dw_w = lax.dynamic_slice_in_dim(dww_full, my_idx * (CONV_DIM // N_DEV), CONV_DIM // N_DEV, axis=2)

    w_small = _pack_small(norm_g, ple_norm_g, final_g, dw_b, conv_ln_g, conv_ln_b, conv_out_g, attn_out_g)
    m_small = _pack_small(m_norm_g, m_ple_norm_g, m_final_g, m_dw_b, m_conv_ln_g, m_conv_ln_b, m_conv_out_g, m_attn_out_g)
    v_small = _pack_small(v_norm_g, v_ple_norm_g, v_final_g, v_dw_b, v_conv_ln_g, v_conv_ln_b, v_conv_out_g, v_attn_out_g)
    d_small, nm_small, nv_small = _adamw(w_small, g_small, m_small, v_small, "adamw_small")
    res = {"g": _unpack_small(g_small), "d": _unpack_small(d_small), "m": _unpack_small(nm_small),
           "v": _unpack_small(nv_small)}

    sums = [_sum_slots(slots[l], f"sum_grads_{l}") for l in range(depth)]
    big = {
        "w_in": jnp.stack([sums[l][0].T for l in range(depth)]),
        "w_pw": jnp.stack([sums[l][1] for l in range(depth)]),
        "w_out": jnp.stack([sums[l][2] for l in range(depth)]),
        "w_ple_gate": jnp.stack([sums[l][3] for l in range(depth)]),
        "w_ple": jnp.stack([sums[l][4] for l in range(depth)]),
        "dw_w": g_dw_w,
    }
    state = {"w_in": (w_in, m_w_in, v_w_in), "w_pw": (w_pw, m_w_pw, v_w_pw), "w_out": (w_out, m_w_out, v_w_out),
             "w_ple_gate": (w_ple_gate, m_w_ple_gate, v_w_ple_gate), "w_ple": (w_ple, m_w_ple, v_w_ple),
             "dw_w": (dw_w, m_dw_w, v_dw_w)}
    for name, g in big.items():
        wv, mv, vv = state[name]
        flat = lambda a: a.reshape(-1, a.shape[-1])
        d, nm, nv = _adamw(flat(wv), flat(g), flat(mv), flat(vv), f"adamw_{name}")
        res["g"][name] = g
        res["d"][name], res["m"][name], res["v"][name] = (d.reshape(wv.shape), nm.reshape(wv.shape),
                                                         nv.reshape(wv.shape))

    order = ["norm_g", "w_in", "attn_out_g", "dw_w", "dw_b", "conv_ln_g", "conv_ln_b", "w_pw", "conv_out_g",
             "w_out", "ple_norm_g", "w_ple_gate", "w_ple", "final_g"]
    return (loss, grad_x, *[res["g"][n] for n in order], *[res["d"][n] for n in order],
            *[res["m"][n] for n in order], *[res["v"][n] for n in order])
```

```python
import functools

import jax
import jax.numpy as jnp
from jax import lax
from jax.experimental import pallas as pl
from jax.experimental.pallas import tpu as pltpu

F32 = jnp.float32
BF16 = jnp.bfloat16
MESH = pl.DeviceIdType.MESH

N_DEV = 8
D_MODEL = 1024
ATTN_DIM = 512
CONV_DIM = 512
HEAD_DIM = 64
N_HEADS = 8
CONV_WIDTH = 31
PLE_DIM = 256
CHUNK = 512
N_CHUNK = 7
EPS = 1e-6
ADAM_LR = 0.001
ADAM_B1 = 0.9
ADAM_B2 = 0.999
ADAM_EPS = 1e-08
ADAM_WD = 0.01
ADAM_STEP = 10

LANES = 128
BLK = 256
ATT_COLS = 2
DEAD_AT = 110.0
FIRST_BLOCK_LANE = HEAD_DIM - 1
TM = 256
HALO = 32
VMEM_LIMIT = 56 * 1024 * 1024
SMALL_ROWS = 16
W_IN_QUARTERS = ((0, 96), (96, 144), (240, 96), (336, 112))
W_IN_GRAD_PARTS = ((0, 112), (112, 96), (208, 240))
PACK_ROWS = 48


def _nn(a, b):
    return lax.dot_general(a, b, (((1,), (0,)), ((), ())), preferred_element_type=F32)


def _nt(a, b):
    return lax.dot_general(a, b, (((1,), (1,)), ((), ())), preferred_element_type=F32)


def _tn(a, b):
    return lax.dot_general(a, b, (((0,), (0,)), ((), ())), preferred_element_type=F32)


def _split(x):
    hi = x.astype(BF16)
    lo = (x - hi.astype(F32)).astype(BF16)
    return hi, lo


def _dot_hilo(x, m):
    hi, lo = _split(x)
    return _nn(hi, m) + _nn(lo, m)


def _sigmoid(x):
    return jax.nn.sigmoid(x)


def _dsilu(x, s):
    return s * (1.0 + x * (1.0 - s))


def _params(n_grid=0, vmem=VMEM_LIMIT):
    sem = ("arbitrary",) * n_grid if n_grid else None
    return pltpu.CompilerParams(dimension_semantics=sem, vmem_limit_bytes=vmem)


def _rows(tm, cols, col=0):
    return pl.BlockSpec((tm, cols), lambda i: (i, col))


def _whole(shape):
    zeros = (0,) * len(shape)
    return pl.BlockSpec(shape, lambda *_: zeros)


def _my_position():
    return lax.axis_index("x"), lax.axis_index("y"), lax.axis_index("c")


def _block(ref, axis, idx, size):
    start = pl.multiple_of(idx * size, size)
    if axis == 0:
        return ref.at[pl.ds(start, size), :]
    return ref.at[:, pl.ds(start, size)]


def _all_gather(shards, axes, name):
    n = len(shards)
    sizes = [s.shape[a] for s, a in zip(shards, axes)]

    def full_shape(s, a):
        shape = list(s.shape)
        shape[a] *= N_DEV
        return jax.ShapeDtypeStruct(tuple(shape), s.dtype)

    def body(*refs):
        ins, outs = refs[:n], refs[n:2 * n]
        send_sems, recv_sems, local_sems = refs[2 * n:]
        x, y, c = _my_position()
        me, sibling = (x, y, c), (x, y, 1 - c)
        chips = [(1 - x, y), (x, 1 - y), (1 - x, 1 - y)]

        def place(i, dev):
            return _block(outs[i], axes[i], 4 * dev[0] + 2 * dev[1] + dev[2], sizes[i])

        def copy(k, i, dev, to, src=None):
            return pltpu.make_async_remote_copy(
                src_ref=place(i, dev) if src is None else src, dst_ref=place(i, dev),
                send_sem=send_sems.at[k, i], recv_sem=recv_sems.at[k, i],
                device_id=to, device_id_type=MESH)

        mine = [pltpu.make_async_copy(ins[i], place(i, me), local_sems.at[i]) for i in range(n)]
        for cp in mine:
            cp.start()
        first = [copy(0, i, me, sibling, src=ins[i]) for i in range(n)]
        for j, chip in enumerate(chips):
            first += [copy(1 + j, i, me, (*chip, c), src=ins[i]) for i in range(n)]
        for cp in first:
            cp.start()
        passed = []
        for j, chip in enumerate(chips):
            for i in range(n):
                copy(1 + j, i, (*chip, c), me).wait_recv()
            hop = [copy(4 + j, i, (*chip, c), sibling) for i in range(n)]
            for cp in hop:
                cp.start()
            passed += hop
        for i in range(n):
            copy(0, i, sibling, me).wait_recv()
        for j, chip in enumerate(chips):
            for i in range(n):
                copy(4 + j, i, (*chip, 1 - c), me).wait_recv()
        for cp in first + passed:
            cp.wait_send()
        for cp in mine:
            cp.wait()

    any_spec = pl.BlockSpec(memory_space=pl.ANY)
    return pl.pallas_call(
        body, name=name,
        out_shape=[full_shape(s, a) for s, a in zip(shards, axes)],
        in_specs=[any_spec] * n, out_specs=[any_spec] * n,
        scratch_shapes=[pltpu.SemaphoreType.DMA((7, n)), pltpu.SemaphoreType.DMA((7, n)),
                        pltpu.SemaphoreType.DMA((n,))],
    )(*shards)


def _scatter_partials(fulls, axes, name):
    n = len(fulls)
    sizes = [f.shape[a] // N_DEV for f, a in zip(fulls, axes)]

    def slot_shape(f, a):
        shape = list(f.shape)
        shape[a] //= N_DEV
        return jax.ShapeDtypeStruct((N_DEV, *shape), f.dtype)

    def body(*refs):
        ins, outs = refs[:n], refs[n:2 * n]
        send_sems, recv_sems, local_sems = refs[2 * n:]
        x, y, c = _my_position()
        my_idx = 4 * x + 2 * y + c

        def peer_of(k):
            px = 1 - x if k & 4 else x
            py = 1 - y if k & 2 else y
            pc = 1 - c if k & 1 else c
            return (px, py, pc), 4 * px + 2 * py + pc

        def copy(k, i):
            peer, peer_idx = peer_of(k)
            return pltpu.make_async_remote_copy(
                src_ref=_block(ins[i], axes[i], peer_idx, sizes[i]), dst_ref=outs[i].at[my_idx],
                send_sem=send_sems.at[k - 1, i], recv_sem=recv_sems.at[k - 1, i],
                device_id=peer, device_id_type=MESH)

        def arrival(k, i):
            peer, peer_idx = peer_of(k)
            return pltpu.make_async_remote_copy(
                src_ref=_block(ins[i], axes[i], my_idx, sizes[i]), dst_ref=outs[i].at[peer_idx],
                send_sem=send_sems.at[k - 1, i], recv_sem=recv_sems.at[k - 1, i],
                device_id=peer, device_id_type=MESH)

        mine = [pltpu.make_async_copy(_block(ins[i], axes[i], my_idx, sizes[i]), outs[i].at[my_idx],
                                      local_sems.at[i]) for i in range(n)]
        sends = [copy(k, i) for k in range(1, N_DEV) for i in range(n)]
        for cp in mine + sends:
            cp.start()
        for k in range(1, N_DEV):
            for i in range(n):
                arrival(k, i).wait_recv()
        for cp in sends:
            cp.wait_send()
        for cp in mine:
            cp.wait()

    any_spec = pl.BlockSpec(memory_space=pl.ANY)
    return pl.pallas_call(
        body, name=name,
        out_shape=[slot_shape(f, a) for f, a in zip(fulls, axes)],
        in_specs=[any_spec] * n, out_specs=[any_spec] * n,
        scratch_shapes=[pltpu.SemaphoreType.DMA((7, n)), pltpu.SemaphoreType.DMA((7, n)),
                        pltpu.SemaphoreType.DMA((n,))],
    )(*fulls)


class _Ride:
    def __init__(self, parts):
        self.parts = [p for p in parts if p is not None]

    @staticmethod
    def gather(src, axis, land=None, lo=0, n=None):
        return ("gather", src, land, axis, lo, src.shape[axis] if n is None else n)

    @staticmethod
    def scatter(src, axis, land=None, lo=0, n=None):
        return ("scatter", src, land, axis, lo, src.shape[axis] // N_DEV if n is None else n)

    def arrays(self):
        return [p[1] for p in self.parts] + [p[2] for p in self.parts if p[2] is not None]

    def out_shapes(self):
        out = []
        for kind, src, _, axis, _, _ in self.parts:
            shape = list(src.shape)
            if kind == "gather":
                shape[axis] *= N_DEV
            else:
                shape[axis] //= N_DEV
                shape = [N_DEV] + shape
            out.append(jax.ShapeDtypeStruct(tuple(shape), src.dtype))
        return out

    def aliases(self, n_in, n_out):
        m, out = len(self.parts), {}
        for j, p in enumerate(self.parts):
            if p[2] is not None:
                out[n_in + m + len(out)] = n_out + j
        return out

    def scratch(self):
        m = len(self.parts)
        return [pltpu.SemaphoreType.DMA((N_DEV - 1, m)), pltpu.SemaphoreType.DMA((N_DEV - 1, m)),
                pltpu.SemaphoreType.DMA((m,))]

    def _copies(self, src_refs, land_refs, sems):
        send_sems, recv_sems, local_sems = sems
        x, y, c = _my_position()
        my_idx = 4 * x + 2 * y + c
        own, sends, lands = [], [], []
        for j, (kind, src, _, axis, lo, n) in enumerate(self.parts):
            size = src.shape[axis] if kind == "gather" else src.shape[axis] // N_DEV
            align = 16 if axis == 0 else LANES

            def rows(ref, idx, lead=None, axis=axis, lo=lo, n=n, size=size, align=align):
                at = pl.ds(pl.multiple_of(idx * size + lo, align), n)
                where = (at, slice(None)) if axis == 0 else (slice(None), at)
                return ref.at[where] if lead is None else ref.at[(lead, *where)]

            def in_shard(ref):
                return rows(ref, 0)

            def in_slot(ref, s):
                return rows(ref, 0, lead=s)

            for k in range(N_DEV):
                px = 1 - x if k & 4 else x
                py = 1 - y if k & 2 else y
                pc = 1 - c if k & 1 else c
                peer_idx = 4 * px + 2 * py + pc
                if kind == "gather":
                    a, b, landed = in_shard(src_refs[j]), rows(land_refs[j], my_idx), rows(land_refs[j], peer_idx)
                else:
                    a, b, landed = rows(src_refs[j], peer_idx), in_slot(land_refs[j], my_idx), in_slot(land_refs[j], peer_idx)
                if k == 0:
                    own.append(pltpu.make_async_copy(a, b, local_sems.at[j]))
                    continue
                mk = lambda dst, a=a, k=k, j=j, to=(px, py, pc): pltpu.make_async_remote_copy(
                    src_ref=a, dst_ref=dst, send_sem=send_sems.at[k - 1, j], recv_sem=recv_sems.at[k - 1, j],
                    device_id=to, device_id_type=MESH)
                sends.append(mk(b))
                lands.append(mk(landed))
        return own, sends, lands

    def start(self, src_refs, land_refs, sems):
        own, sends, _ = self._copies(src_refs, land_refs, sems)
        for cp in own + sends:
            cp.start()

    def wait(self, src_refs, land_refs, sems):
        own, sends, lands = self._copies(src_refs, land_refs, sems)
        for cp in lands:
            cp.wait_recv()
        for cp in sends:
            cp.wait_send()
        for cp in own:
            cp.wait()


def _call(body, *, name, grid, in_specs, out_specs, out_shape, args, scratch_shapes=(), ride=None):
    in_specs, out_specs, out_shape = list(in_specs), list(out_specs), list(out_shape)
    n_in, n_out, n_sc = len(in_specs), len(out_specs), len(scratch_shapes)
    if ride is None or not ride.parts:
        res = pl.pallas_call(body, name=name, grid=grid, in_specs=in_specs, out_specs=out_specs,
                             out_shape=out_shape, scratch_shapes=list(scratch_shapes),
                             compiler_params=_params(len(grid)))(*args)
        return list(res), []
    extra, m = ride.arrays(), len(ride.parts)

    def riding(*refs):
        a = n_in + len(extra)
        b = a + n_out
        srcs, lands, sems = refs[n_in:n_in + m], refs[b:b + m], refs[b + m + n_sc:]
        at = [pl.program_id(d) for d in range(len(grid))]

        @pl.when(functools.reduce(jnp.logical_and, [i == 0 for i in at]))
        def _():
            ride.start(srcs, lands, sems)

        body(*refs[:n_in], *refs[a:b], *refs[b + m:b + m + n_sc])

        @pl.when(functools.reduce(jnp.logical_and, [i == g - 1 for i, g in zip(at, grid)]))
        def _():
            ride.wait(srcs, lands, sems)

    hbm = pl.BlockSpec(memory_space=pl.ANY)
    res = pl.pallas_call(
        riding, name=name, grid=grid, in_specs=in_specs + [hbm] * len(extra), out_specs=out_specs + [hbm] * m,
        out_shape=out_shape + ride.out_shapes(), scratch_shapes=list(scratch_shapes) + ride.scratch(),
        input_output_aliases=ride.aliases(n_in, n_out), compiler_params=_params(len(grid)),
    )(*args, *extra)
    return list(res[:n_out]), list(res[n_out:])


def _prenorm_inproj(h, gain, w_in_t, name, ride=None):
    T = h.shape[0]

    def body(h_ref, g_ref, w_ref, q_ref, k_ref, v_ref, ug_ref, hn_ref):
        hv = h_ref[...]
        r = lax.rsqrt(jnp.mean(hv * hv, axis=-1, keepdims=True) + EPS)
        hn = (hv * r * g_ref[...]).astype(BF16)
        hn_ref[...] = hn
        for j in range(N_CHUNK):
            u = _nt(hn, w_ref[j * CHUNK:(j + 1) * CHUNK, :])
            if j == 0:
                q_ref[...] = (u * (HEAD_DIM ** -0.5)).astype(BF16)
            elif j == 1:
                k_ref[...] = u.astype(BF16)
            elif j == 2:
                v_ref[...] = u.astype(BF16)
            else:
                ug_ref[:, (j - 3) * CHUNK:(j - 2) * CHUNK] = u

    act = jax.ShapeDtypeStruct((T, CHUNK), BF16)
    return _call(
        body, name=name, grid=(T // TM,),
        in_specs=[_rows(TM, D_MODEL), _whole((1, D_MODEL)), _whole((N_CHUNK * CHUNK, D_MODEL))],
        out_specs=[_rows(TM, CHUNK)] * 3 + [_rows(TM, 4 * CHUNK), _rows(TM, D_MODEL)],
        out_shape=[act, act, act, jax.ShapeDtypeStruct((T, 4 * CHUNK), F32),
                   jax.ShapeDtypeStruct((T, D_MODEL), BF16)],
        args=(h, gain, w_in_t,), ride=ride)


def _softplus_parts(z):
    e = jnp.exp(-jnp.abs(z))
    return e, jnp.maximum(z, 0.0) + jnp.log(1.0 + e)


def _attn_fwd(qs, k, v, tri, name, ride=None):
    T = qs.shape[0]
    assert T // BLK <= FIRST_BLOCK_LANE, "one lane per key block below the lane of the first block"
    width = LANES * ATT_COLS
    chains = [(c, half) for c in range(ATT_COLS) for half in range(2)]

    def body(q_ref, k_ref, v_ref, m_ref, o_ref, cs_ref):
        qi = pl.program_id(1)
        lane = lax.broadcasted_iota(jnp.int32, (BLK, LANES), 1)
        first = lane < HEAD_DIM
        causal = (lax.broadcasted_iota(jnp.int32, (BLK, BLK), 1)
                  < lax.broadcasted_iota(jnp.int32, (BLK, BLK), 0))
        tri_m = m_ref[...]
        qh = {}
        for c in range(ATT_COLS):
            q = q_ref[:, c * LANES:(c + 1) * LANES]
            zero = jnp.zeros_like(q)
            qh[c, 0], qh[c, 1] = jnp.where(first, q, zero), jnp.where(first, zero, q)

        def step(kb, state, masked):
            carries, accs, cvals = state
            start = pl.multiple_of(kb * BLK, BLK)
            kblk = [k_ref[pl.ds(start, BLK), c * LANES:(c + 1) * LANES] for c in range(ATT_COLS)]
            vblk = [v_ref[pl.ds(start, BLK), c * LANES:(c + 1) * LANES] for c in range(ATT_COLS)]
            z = [_nt(qh[ch], kblk[ch[0]]) for ch in chains]
            sp = [_softplus_parts(zi)[1] for zi in z]
            if masked:
                sp = [jnp.where(causal, s, 0.0) for s in sp]
            incl = [_dot_hilo(s, tri_m) for s in sp]
            a = [jnp.exp(zi - ii - ci) for zi, ii, ci in zip(z, incl, carries)]
            if masked:
                a = [jnp.where(causal, ai, 0.0) for ai in a]
            accs, cvals = list(accs), list(cvals)
            for n, (c, half) in enumerate(chains):
                zero = jnp.zeros_like(vblk[c])
                vh = jnp.where(first, vblk[c], zero) if half == 0 else jnp.where(first, zero, vblk[c])
                accs[c] = accs[c] + _nn(a[n].astype(BF16), vh)
                cvals[c] = jnp.where(lane == kb + HEAD_DIM * half, carries[n], cvals[c])
            carries = tuple(ci + ii[:, 0:1] for ci, ii in zip(carries, incl))
            return carries, tuple(accs), tuple(cvals)

        zeros = tuple(jnp.zeros((BLK, LANES), F32) for _ in range(ATT_COLS))
        state = (tuple(jnp.zeros((BLK, 1), F32) for _ in chains), zeros, zeros)
        state = step(qi, state, True)

        def reaches_further(st):
            it, (carries, _, _) = st
            least = functools.reduce(jnp.minimum, carries)
            return jnp.logical_and(it < qi, jnp.min(least) < DEAD_AT)

        done, state = lax.while_loop(reaches_further, lambda st: (st[0] + 1, step(qi - 1 - st[0], st[1], False)),
                                     (jnp.int32(0), state))
        first_block = (qi - done).astype(F32)
        for c in range(ATT_COLS):
            o_ref[:, c * LANES:(c + 1) * LANES] = state[1][c]
            cs_ref[:, c * LANES:(c + 1) * LANES] = jnp.where(lane == FIRST_BLOCK_LANE, first_block, state[2][c])

    blk = pl.BlockSpec((BLK, width), lambda j, i: (i, j))
    col = pl.BlockSpec((T, width), lambda j, i: (0, j))
    out = jax.ShapeDtypeStruct((T, ATTN_DIM), F32)
    return _call(
        body, name=name, grid=(ATTN_DIM // width, T // BLK),
        in_specs=[blk, col, col, _whole((BLK, BLK))],
        out_specs=[blk, blk], out_shape=[out, out],
        args=(qs, k, v, tri,), ride=ride)


def _conv_fwd(ug, dw_w, dw_b, ln_g, ln_b, name, ride=None):
    T = ug.shape[0]
    per = TM // HALO

    def body(cv_ref, cg_ref, cvh_ref, cgh_ref, w_ref, b_ref, g_ref, beta_ref, conv_ref, c2_ref, pad_ref):
        i = pl.program_id(0)
        halo = cvh_ref[...] * _sigmoid(cgh_ref[...])
        pad_ref[0:HALO, :] = jnp.where(i == 0, 0.0, halo)
        pad_ref[HALO:, :] = cv_ref[...] * _sigmoid(cg_ref[...])
        acc = jnp.zeros((TM, CONV_DIM), F32) + b_ref[...]
        for t in range(CONV_WIDTH):
            lo = HALO - (CONV_WIDTH - 1) + t
            acc = acc + w_ref[t:t + 1, :] * pad_ref[lo:lo + TM, :]
        conv_ref[...] = acc
        mu = jnp.mean(acc, axis=-1, keepdims=True)
        xc = acc - mu
        rs = lax.rsqrt(jnp.mean(xc * xc, axis=-1, keepdims=True) + EPS)
        ln = xc * rs * g_ref[...] + beta_ref[...]
        c2_ref[...] = (ln * _sigmoid(ln)).astype(BF16)

    prev = lambda col: pl.BlockSpec((HALO, CHUNK), lambda i: (jnp.maximum(i * per - 1, 0), col))
    vec = _whole((1, CONV_DIM))
    return _call(
        body, name=name, grid=(T // TM,),
        in_specs=[_rows(TM, CHUNK, 1), _rows(TM, CHUNK, 2), prev(1), prev(2),
                  _whole((CONV_WIDTH, CONV_DIM)), vec, vec, vec],
        out_specs=[_rows(TM, CONV_DIM), _rows(TM, CONV_DIM)],
        out_shape=[jax.ShapeDtypeStruct((T, CONV_DIM), F32), jax.ShapeDtypeStruct((T, CONV_DIM), BF16)],
        scratch_shapes=[pltpu.VMEM((TM + HALO, CONV_DIM), F32)],
        args=(ug, ug, ug, ug, dw_w, dw_b, ln_g, ln_b,), ride=ride)


def _mix_out_ple(o, ug, c2, h, p, head_mean, g_attn, g_conv, g_ple, w_pw, w_out, w_gate, w_ple, name, ride=None):
    T = h.shape[0]

    def body(o_ref, ga_ref, gc_ref, c2_ref, h_ref, p_ref, hm_ref, gao_ref, gco_ref, gpn_ref,
             wpw_ref, wout_ref, wg_ref, wple_ref,
             h2_ref, h1_ref, ycat_ref, hn2_ref, gate_ref, e_ref, c3_ref):
        ov = o_ref[...]
        rh = lax.rsqrt(_dot_hilo(ov * ov, hm_ref[...]) + EPS)
        ga = ga_ref[...]
        ya = (ov * rh * gao_ref[...] * (ga * _sigmoid(ga))).astype(BF16)
        c3 = _nn(c2_ref[...], wpw_ref[...])
        c3_ref[...] = c3
        rc = lax.rsqrt(jnp.mean(c3 * c3, axis=-1, keepdims=True) + EPS)
        gc = gc_ref[...]
        yc = (c3 * rc * gco_ref[...] * (gc * _sigmoid(gc))).astype(BF16)
        ycat_ref[:, :ATTN_DIM] = ya
        ycat_ref[:, ATTN_DIM:] = yc
        h1 = h_ref[...] + _nn(ya, wout_ref[:ATTN_DIM, :]) + _nn(yc, wout_ref[ATTN_DIM:, :])
        h1_ref[...] = h1
        r1 = lax.rsqrt(jnp.mean(h1 * h1, axis=-1, keepdims=True) + EPS)
        hn2 = (h1 * r1 * gpn_ref[...]).astype(BF16)
        hn2_ref[...] = hn2
        gate = _sigmoid(_nn(hn2, wg_ref[...]))
        e = _nn(p_ref[...].astype(BF16), wple_ref[...])
        gate_ref[...] = gate
        e_ref[...] = e
        h2_ref[...] = h1 + e * gate

    f32 = lambda cols: jax.ShapeDtypeStruct((T, cols), F32)
    bf = lambda cols: jax.ShapeDtypeStruct((T, cols), BF16)
    return _call(
        body, name=name, grid=(T // TM,),
        in_specs=[_rows(TM, ATTN_DIM), _rows(TM, CHUNK, 0), _rows(TM, CHUNK, 3), _rows(TM, CONV_DIM),
                  _rows(TM, D_MODEL), _rows(TM, PLE_DIM), _whole((ATTN_DIM, ATTN_DIM)),
                  _whole((1, ATTN_DIM)), _whole((1, CONV_DIM)), _whole((1, D_MODEL)),
                  _whole((CONV_DIM, CONV_DIM)), _whole((D_MODEL, D_MODEL)), _whole((D_MODEL, D_MODEL)),
                  _whole((PLE_DIM, D_MODEL))],
        out_specs=[_rows(TM, D_MODEL), _rows(TM, D_MODEL), _rows(TM, D_MODEL), _rows(TM, D_MODEL),
                   _rows(TM, D_MODEL), _rows(TM, D_MODEL), _rows(TM, CONV_DIM)],
        out_shape=[f32(D_MODEL), f32(D_MODEL), bf(D_MODEL), bf(D_MODEL), f32(D_MODEL), f32(D_MODEL),
                   f32(CONV_DIM)],
        args=(o, ug, ug, c2, h, p, head_mean, g_attn, g_conv, g_ple, w_pw, w_out, w_gate, w_ple,), ride=ride)


def _final_loss(h, target, gain, name):
    T = h.shape[0]

    def body(h_ref, t_ref, g_ref, dh_ref, gsum_ref, loss_ref):
        @pl.when(pl.program_id(0) == 0)
        def _():
            gsum_ref[...] = jnp.zeros_like(gsum_ref)
            loss_ref[...] = jnp.zeros_like(loss_ref)

        hv = h_ref[...]
        r = lax.rsqrt(jnp.mean(hv * hv, axis=-1, keepdims=True) + EPS)
        xh = hv * r
        diff = xh * g_ref[...] - t_ref[...]
        loss_ref[...] += 0.5 * jnp.sum(jnp.mean(diff * diff, axis=-1, keepdims=True), axis=0, keepdims=True)
        dy = diff * (1.0 / D_MODEL)
        gsum_ref[...] += jnp.sum(dy * xh, axis=0, keepdims=True)
        dxh = dy * g_ref[...]
        dh_ref[...] = r * (dxh - xh * jnp.mean(dxh * xh, axis=-1, keepdims=True))

    return pl.pallas_call(
        body, name=name, grid=(T // TM,),
        in_specs=[_rows(TM, D_MODEL), _rows(TM, D_MODEL), _whole((1, D_MODEL))],
        out_specs=[_rows(TM, D_MODEL), _whole((1, D_MODEL)), _whole((1, LANES))],
        out_shape=[jax.ShapeDtypeStruct((T, D_MODEL), F32), jax.ShapeDtypeStruct((1, D_MODEL), F32),
                   jax.ShapeDtypeStruct((1, LANES), F32)],
        compiler_params=_params(1),
    )(h, target, gain)


def _ple_out_bwd(dh2, gate, e, h1, g_ple, w_gate, w_out, name, ride=None):
    T = dh2.shape[0]

    def body(dh2_ref, gate_ref, e_ref, h1_ref, gpn_ref, wg_ref, wout_ref,
             dh1_ref, dh1b_ref, dzg_ref, de_ref, dycat_ref, gsum_ref):
        @pl.when(pl.program_id(0) == 0)
        def _():
            gsum_ref[...] = jnp.zeros_like(gsum_ref)

        dh2v = dh2_ref[...]
        gate = gate_ref[...]
        de_ref[...] = (dh2v * gate).astype(BF16)
        dzg = (dh2v * e_ref[...] * gate * (1.0 - gate)).astype(BF16)
        dzg_ref[...] = dzg
        dhn2 = _nt(dzg, wg_ref[...])
        h1 = h1_ref[...]
        r1 = lax.rsqrt(jnp.mean(h1 * h1, axis=-1, keepdims=True) + EPS)
        xh = h1 * r1
        gsum_ref[...] += jnp.sum(dhn2 * xh, axis=0, keepdims=True)
        dxh = dhn2 * gpn_ref[...]
        dh1 = dh2v + r1 * (dxh - xh * jnp.mean(dxh * xh, axis=-1, keepdims=True))
        dh1_ref[...] = dh1
        dh1b = dh1.astype(BF16)
        dh1b_ref[...] = dh1b
        dycat_ref[...] = _nt(dh1b, wout_ref[...])

    f32 = jax.ShapeDtypeStruct((T, D_MODEL), F32)
    bf = jax.ShapeDtypeStruct((T, D_MODEL), BF16)
    full = _rows(TM, D_MODEL)
    return _call(
        body, name=name, grid=(T // TM,),
        in_specs=[full, full, full, full, _whole((1, D_MODEL)), _whole((D_MODEL, D_MODEL)),
                  _whole((D_MODEL, D_MODEL))],
        out_specs=[full, full, full, full, full, _whole((1, D_MODEL))],
        out_shape=[f32, bf, bf, bf, f32, jax.ShapeDtypeStruct((1, D_MODEL), F32)],
        args=(dh2, gate, e, h1, g_ple, w_gate, w_out,), ride=ride)


def _branch_bwd(dycat, o, ug, c3, conv, head_mean, g_attn, g_conv, ln_g, ln_b, w_pw, name, ride=None):
    T = o.shape[0]

    def body(dya_ref, dyc_ref, o_ref, ga_ref, gc_ref, c3_ref, conv_ref, hm_ref, gao_ref, gco_ref,
             lng_ref, lnb_ref, wpw_ref,
             do_ref, dga_ref, dgc_ref, dc3_ref, dconv_ref, sums_ref):
        @pl.when(pl.program_id(0) == 0)
        def _():
            sums_ref[...] = jnp.zeros_like(sums_ref)

        hm = hm_ref[...]
        col = lambda x: jnp.sum(x, axis=0, keepdims=True)
        ov = o_ref[...]
        rh = lax.rsqrt(_dot_hilo(ov * ov, hm) + EPS)
        xh = ov * rh
        ga = ga_ref[...]
        sg = _sigmoid(ga)
        dya = dya_ref[...]
        don = dya * (ga * sg)
        dga_ref[...] = (dya * xh * gao_ref[...] * _dsilu(ga, sg)).astype(BF16)
        sums_ref[0:1, :] += col(don * xh)
        dxh = don * gao_ref[...]
        do_ref[...] = (rh * (dxh - xh * _dot_hilo(dxh * xh, hm))).astype(BF16)
        c3 = c3_ref[...]
        rc = lax.rsqrt(jnp.mean(c3 * c3, axis=-1, keepdims=True) + EPS)
        xh3 = c3 * rc
        gc = gc_ref[...]
        sgc = _sigmoid(gc)
        dyc = dyc_ref[...]
        dn3 = dyc * (gc * sgc)
        dgc_ref[...] = (dyc * xh3 * gco_ref[...] * _dsilu(gc, sgc)).astype(BF16)
        sums_ref[1:2, :] += col(dn3 * xh3)
        dxh3 = dn3 * gco_ref[...]
        dc3 = (rc * (dxh3 - xh3 * jnp.mean(dxh3 * xh3, axis=-1, keepdims=True))).astype(BF16)
        dc3_ref[...] = dc3
        dc2 = _nt(dc3, wpw_ref[...])
        cv = conv_ref[...]
        mu = jnp.mean(cv, axis=-1, keepdims=True)
        xc = cv - mu
        rs = lax.rsqrt(jnp.mean(xc * xc, axis=-1, keepdims=True) + EPS)
        xn = xc * rs
        ln = xn * lng_ref[...] + lnb_ref[...]
        dln = dc2 * _dsilu(ln, _sigmoid(ln))
        sums_ref[2:3, :] += col(dln * xn)
        sums_ref[3:4, :] += col(dln)
        dxn = dln * lng_ref[...]
        dconv = rs * (dxn - jnp.mean(dxn, axis=-1, keepdims=True)
                      - xn * jnp.mean(dxn * xn, axis=-1, keepdims=True))
        dconv_ref[...] = dconv
        sums_ref[4:5, :] += col(dconv)

    half = lambda dt: jax.ShapeDtypeStruct((T, CHUNK), dt)
    tile = _rows(TM, CHUNK)
    vec = _whole((1, CHUNK))
    return _call(
        body, name=name, grid=(T // TM,),
        in_specs=[_rows(TM, CHUNK, 0), _rows(TM, CHUNK, 1), tile, _rows(TM, CHUNK, 0), _rows(TM, CHUNK, 3),
                  tile, tile, _whole((ATTN_DIM, ATTN_DIM)), vec, vec, vec, vec, _whole((CONV_DIM, CONV_DIM))],
        out_specs=[tile, tile, tile, tile, tile, _whole((8, CHUNK))],
        out_shape=[half(BF16), half(BF16), half(BF16), half(BF16), half(F32),
                   jax.ShapeDtypeStruct((8, CHUNK), F32)],
        args=(dycat, dycat, o, ug, ug, c3, conv, head_mean, g_attn, g_conv, ln_g, ln_b, w_pw,), ride=ride)


def _conv_bwd(dconv, ug, dw_w, name, ride=None):
    T = dconv.shape[0]
    per = TM // HALO
    last = T // HALO - 1

    def body(d_ref, dn_ref, cv_ref, cg_ref, cvh_ref, cgh_ref, w_ref, dcv_ref, dcg_ref, dw_ref, dpad_ref, cpad_ref):
        i = pl.program_id(0)

        @pl.when(i == 0)
        def _():
            dw_ref[...] = jnp.zeros_like(dw_ref)

        d = d_ref[...]
        dpad_ref[0:TM, :] = d
        dpad_ref[TM:, :] = jnp.where(i == pl.num_programs(0) - 1, 0.0, dn_ref[...])
        halo = cvh_ref[...] * _sigmoid(cgh_ref[...])
        cpad_ref[0:HALO, :] = jnp.where(i == 0, 0.0, halo)
        cv = cv_ref[...]
        sg = _sigmoid(cg_ref[...])
        cpad_ref[HALO:, :] = cv * sg
        dc = jnp.zeros((TM, CONV_DIM), F32)
        for t in range(CONV_WIDTH):
            up = CONV_WIDTH - 1 - t
            dc = dc + w_ref[t:t + 1, :] * dpad_ref[up:up + TM, :]
            lo = HALO - (CONV_WIDTH - 1) + t
            dw_ref[t:t + 1, :] += jnp.sum(d * cpad_ref[lo:lo + TM, :], axis=0, keepdims=True)
        dcv_ref[...] = (dc * sg).astype(BF16)
        dcg_ref[...] = (dc * cv * sg * (1.0 - sg)).astype(BF16)

    prev = lambda col: pl.BlockSpec((HALO, CHUNK), lambda i: (jnp.maximum(i * per - 1, 0), col))
    nxt = pl.BlockSpec((HALO, CONV_DIM), lambda i: (jnp.minimum((i + 1) * per, last), 0))
    half = jax.ShapeDtypeStruct((T, CHUNK), BF16)
    return _call(
        body, name=name, grid=(T // TM,),
        in_specs=[_rows(TM, CONV_DIM), nxt, _rows(TM, CHUNK, 1), _rows(TM, CHUNK, 2), prev(1), prev(2),
                  _whole((CONV_WIDTH, CONV_DIM))],
        out_specs=[_rows(TM, CHUNK), _rows(TM, CHUNK), _whole((HALO, CONV_DIM))],
        out_shape=[half, half, jax.ShapeDtypeStruct((HALO, CONV_DIM), F32)],
        scratch_shapes=[pltpu.VMEM((TM + HALO, CONV_DIM), F32), pltpu.VMEM((TM + HALO, CONV_DIM), F32)],
        args=(dconv, dconv, ug, ug, ug, ug, dw_w,), ride=ride)


def _attn_bwd(qs, k, v, do, cs, tri, tri_t, name, ride=None):
    T = qs.shape[0]
    nq = T // BLK
    width = LANES * ATT_COLS
    chains = [(c, half) for c in range(ATT_COLS) for half in range(2)]

    def body(q_ref, k_ref, v_ref, do_ref, cs_ref, m_ref, mt_ref, dq_ref, dk_ref, dv_ref, dk_acc, dv_acc):
        qi = pl.program_id(1)

        @pl.when(qi == 0)
        def _():
            dk_acc[...] = jnp.zeros_like(dk_acc)
            dv_acc[...] = jnp.zeros_like(dv_acc)

        lane = lax.broadcasted_iota(jnp.int32, (BLK, LANES), 1)
        first = lane < HEAD_DIM
        causal = (lax.broadcasted_iota(jnp.int32, (BLK, BLK), 1)
                  < lax.broadcasted_iota(jnp.int32, (BLK, BLK), 0))
        tri_m = m_ref[...]
        tri_mt = mt_ref[...]

        def halves(x):
            zero = jnp.zeros_like(x)
            return jnp.where(first, x, zero), jnp.where(first, zero, x)

        qh, doh, cs = {}, {}, []
        for c in range(ATT_COLS):
            qh[c, 0], qh[c, 1] = halves(q_ref[:, c * LANES:(c + 1) * LANES])
            doh[c, 0], doh[c, 1] = halves(do_ref[:, c * LANES:(c + 1) * LANES])
            cs.append(cs_ref[:, c * LANES:(c + 1) * LANES])

        def step(kb, state, masked):
            prefixes, dq_accs = state
            start = pl.multiple_of(kb * BLK, BLK)
            kblk = [k_ref[pl.ds(start, BLK), c * LANES:(c + 1) * LANES] for c in range(ATT_COLS)]
            vblk = [v_ref[pl.ds(start, BLK), c * LANES:(c + 1) * LANES] for c in range(ATT_COLS)]
            z = [_nt(qh[ch], kblk[ch[0]]) for ch in chains]
            da = [_nt(doh[ch], vblk[ch[0]]) for ch in chains]
            parts = [_softplus_parts(zi) for zi in z]
            sp = [pt[1] for pt in parts]
            if masked:
                sp = [jnp.where(causal, s, 0.0) for s in sp]
            incl = [_dot_hilo(s, tri_m) for s in sp]
            carries = [jnp.sum(jnp.where(lane == kb + HEAD_DIM * half, cs[c], 0.0), axis=1, keepdims=True)
                       for c, half in chains]
            a = [jnp.exp(zi - ii - ci) for zi, ii, ci in zip(z, incl, carries)]
            if masked:
                a = [jnp.where(causal, ai, 0.0) for ai in a]
            w = [ai * di for ai, di in zip(a, da)]
            pinc = [_dot_hilo(wi, tri_mt) for wi in w]
            beta = [jnp.where(zi >= 0.0, 1.0, pt[0]) / (1.0 + pt[0]) for zi, pt in zip(z, parts)]
            dz = [wi - bi * (pi + pre) for wi, bi, pi, pre in zip(w, beta, pinc, prefixes)]
            if masked:
                dz = [jnp.where(causal, d, 0.0) for d in dz]
            dq_accs = list(dq_accs)
            for c in range(ATT_COLS):
                k0, k1 = halves(kblk[c])
                dz0, dz1 = dz[2 * c].astype(BF16), dz[2 * c + 1].astype(BF16)
                a0, a1 = a[2 * c].astype(BF16), a[2 * c + 1].astype(BF16)
                dq_accs[c] = dq_accs[c] + _nn(dz0, k0) + _nn(dz1, k1)
                dk_acc[pl.ds(start, BLK), c * LANES:(c + 1) * LANES] += _tn(dz0, qh[c, 0]) + _tn(dz1, qh[c, 1])
                dv_acc[pl.ds(start, BLK), c * LANES:(c + 1) * LANES] += _tn(a0, doh[c, 0]) + _tn(a1, doh[c, 1])
            prefixes = tuple(pre + pi[:, BLK - 1:BLK] for pre, pi in zip(prefixes, pinc))
            return prefixes, tuple(dq_accs)

        state = (tuple(jnp.zeros((BLK, 1), F32) for _ in chains),
                 tuple(jnp.zeros((BLK, LANES), F32) for _ in range(ATT_COLS)))
        first_block = jnp.max(jnp.where(lane == FIRST_BLOCK_LANE, cs[0], 0.0)).astype(jnp.int32)
        state = lax.fori_loop(first_block, qi, lambda kb, st: step(kb, st, False), state)
        state = step(qi, state, True)
        for c in range(ATT_COLS):
            dq_ref[:, c * LANES:(c + 1) * LANES] = (state[1][c] * (HEAD_DIM ** -0.5)).astype(BF16)

        @pl.when(qi == nq - 1)
        def _():
            dk_ref[...] = dk_acc[...].astype(BF16)
            dv_ref[...] = dv_acc[...].astype(BF16)

    blk = pl.BlockSpec((BLK, width), lambda j, i: (i, j))
    col = pl.BlockSpec((T, width), lambda j, i: (0, j))
    out = jax.ShapeDtypeStruct((T, ATTN_DIM), BF16)
    return _call(
        body, name=name, grid=(ATTN_DIM // width, nq),
        in_specs=[blk, col, col, blk, blk, _whole((BLK, BLK)), _whole((BLK, BLK))],
        out_specs=[blk, col, col], out_shape=[out, out, out],
        scratch_shapes=[pltpu.VMEM((T, width), F32), pltpu.VMEM((T, width), F32)],
        args=(qs, k, v, do, cs, tri, tri_t,), ride=ride)


def _inproj_bwd(du, w_in_t, h, dh1, gain, name):
    T = h.shape[0]

    def body(*refs):
        du_refs = refs[:N_CHUNK]
        w_ref, h_ref, dh1_ref, g_ref, dh_ref, gsum_ref = refs[N_CHUNK:]

        @pl.when(pl.program_id(0) == 0)
        def _():
            gsum_ref[...] = jnp.zeros_like(gsum_ref)

        dhn = jnp.zeros((TM, D_MODEL), F32)
        for j in range(N_CHUNK):
            dhn = dhn + _nn(du_refs[j][...], w_ref[j * CHUNK:(j + 1) * CHUNK, :])
        hv = h_ref[...]
        r = lax.rsqrt(jnp.mean(hv * hv, axis=-1, keepdims=True) + EPS)
        xh = hv * r
        gsum_ref[...] += jnp.sum(dhn * xh, axis=0, keepdims=True)
        dxh = dhn * g_ref[...]
        dh_ref[...] = dh1_ref[...] + r * (dxh - xh * jnp.mean(dxh * xh, axis=-1, keepdims=True))

    full = _rows(TM, D_MODEL)
    return pl.pallas_call(
        body, name=name, grid=(T // TM,),
        in_specs=[_rows(TM, CHUNK)] * N_CHUNK + [_whole((N_CHUNK * CHUNK, D_MODEL)), full, full,
                                                 _whole((1, D_MODEL))],
        out_specs=[full, _whole((1, D_MODEL))],
        out_shape=[jax.ShapeDtypeStruct((T, D_MODEL), F32), jax.ShapeDtypeStruct((1, D_MODEL), F32)],
        compiler_params=_params(1),
    )(*du, w_in_t, h, dh1, gain)


def _weight_grad(lhs_list, rhs, name, tk=CHUNK):
    T, n_rhs = rhs.shape
    n = len(lhs_list)
    ka = lhs_list[0].shape[1]
    per = ka // tk

    def body(*refs):
        a_refs, b_ref, out_ref = refs[:n], refs[n], refs[n + 1]
        step = pl.program_id(0)
        for j in range(n):
            for s in range(per):
                @pl.when(step == j * per + s)
                def _(j=j, s=s):
                    out_ref[...] = _tn(a_refs[j][:, s * tk:(s + 1) * tk], b_ref[...]).astype(BF16)

    return pl.pallas_call(
        body, name=name, grid=(n * per,),
        in_specs=[_whole((T, ka))] * n + [_whole((T, n_rhs))],
        out_specs=pl.BlockSpec((tk, n_rhs), lambda i: (i, 0)),
        out_shape=jax.ShapeDtypeStruct((n * ka, n_rhs), BF16),
        compiler_params=_params(1),
    )(*lhs_list, rhs)


def _sum_slots(slots, name):
    n = len(slots)

    def body(*refs):
        for src, dst in zip(refs[:n], refs[n:]):
            acc = src[0].astype(F32)
            for s in range(1, N_DEV):
                acc = acc + src[s].astype(F32)
            dst[...] = acc

    return pl.pallas_call(
        body, name=name,
        out_shape=[jax.ShapeDtypeStruct(s.shape[1:], F32) for s in slots],
        compiler_params=_params(),
    )(*slots)


def _adamw(w, g, m, v, name):
    R, C = w.shape
    tr = R
    for cand in (512, 256, 128, 64):
        if R % cand == 0 and R > cand:
            tr = cand
            break

    def body(w_ref, g_ref, m_ref, v_ref, d_ref, nm_ref, nv_ref):
        gv = g_ref[...]
        nm = ADAM_B1 * m_ref[...] + (1.0 - ADAM_B1) * gv
        nv = ADAM_B2 * v_ref[...] + (1.0 - ADAM_B2) * (gv * gv)
        m_hat = nm / (1.0 - ADAM_B1 ** ADAM_STEP)
        v_hat = nv / (1.0 - ADAM_B2 ** ADAM_STEP)
        d_ref[...] = -ADAM_LR * (m_hat / (jnp.sqrt(v_hat) + ADAM_EPS) + ADAM_WD * w_ref[...])
        nm_ref[...] = nm
        nv_ref[...] = nv

    spec = pl.BlockSpec((tr, C), lambda i: (i, 0))
    out = jax.ShapeDtypeStruct((R, C), F32)
    return pl.pallas_call(
        body, name=name, grid=(R // tr,),
        in_specs=[spec] * 4, out_specs=[spec] * 3, out_shape=[out, out, out],
        compiler_params=_params(1),
    )(w, g, m, v)


def _pack_small(norm_g, ple_norm_g, final_g, dw_b, conv_ln_g, conv_ln_b, conv_out_g, attn_out_g, scalar=None):
    flat = lambda a: a.reshape(1, -1)
    pad = lambda a: jnp.pad(a, ((0, 0), (0, D_MODEL - a.shape[1])))
    last = jnp.zeros((1, D_MODEL), F32) if scalar is None else pad(scalar.reshape(1, 1))
    rows = [norm_g, ple_norm_g, flat(final_g), flat(dw_b), flat(conv_ln_g), flat(conv_ln_b),
            flat(conv_out_g), pad(flat(attn_out_g)), last]
    used = sum(r.shape[0] for r in rows)
    return jnp.concatenate(rows + [jnp.zeros((SMALL_ROWS - used, D_MODEL), F32)], axis=0)


def _unpack_small(a):
    two = lambda r: a[r].reshape(2, -1)
    return dict(norm_g=a[0:2], ple_norm_g=a[2:4], final_g=a[4], dw_b=two(5), conv_ln_g=two(6),
                conv_ln_b=two(7), conv_out_g=two(8), attn_out_g=a[9, :2 * HEAD_DIM].reshape(2, HEAD_DIM))


def kernel(x, p, norm_g, w_in, attn_out_g, dw_w, dw_b, conv_ln_g, conv_ln_b, w_pw, conv_out_g, w_out, ple_norm_g, w_ple_gate, w_ple, final_g, loss_target, m_norm_g, m_w_in, m_attn_out_g, m_dw_w, m_dw_b, m_conv_ln_g, m_conv_ln_b, m_w_pw, m_conv_out_g, m_w_out, m_ple_norm_g, m_w_ple_gate, m_w_ple, m_final_g, v_norm_g, v_w_in, v_attn_out_g, v_dw_w, v_dw_b, v_conv_ln_g, v_conv_ln_b, v_w_pw, v_conv_out_g, v_w_out, v_ple_norm_g, v_w_ple_gate, v_w_ple, v_final_g):
    depth = w_in.shape[0]
    T = x.shape[1]
    my_idx = 4 * lax.axis_index("x") + 2 * lax.axis_index("y") + lax.axis_index("c")

    ids = jnp.arange(BLK)
    tri = (ids[:, None] >= ids[None, :]).astype(BF16)
    tri_t = (ids[:, None] <= ids[None, :]).astype(BF16)
    hid = jnp.arange(ATTN_DIM) // HEAD_DIM
    head_mean = ((hid[:, None] == hid[None, :]).astype(F32) / HEAD_DIM).astype(BF16)

    w_names = ("w_in_t", "w_pw", "w_out", "w_gate", "w_ple")
    w_axes = dict(zip(w_names, (0, 0, 0, 0, 1)))
    shards = [dict(zip(w_names, (w_in[l].T.astype(BF16), w_pw[l].astype(BF16), w_out[l].astype(BF16),
                                 w_ple_gate[l].astype(BF16), w_ple[l].astype(BF16)))) for l in range(depth)]
    first = _all_gather([shards[0][n] for n in w_names] + [dw_w[l].T for l in range(depth)],
                        [w_axes[n] for n in w_names] + [0] * depth, "gather_weights_0")
    layers = []
    for l in range(depth):
        layers.append(dict(
            dw_w=first[len(w_names) + l].T,
            g_norm=norm_g[l][None], g_attn=jnp.tile(attn_out_g[l], N_HEADS)[None], dw_b=dw_b[l][None],
            ln_g=conv_ln_g[l][None], ln_b=conv_ln_b[l][None], g_conv=conv_out_g[l][None],
            g_ple=ple_norm_g[l][None], p=p[l, 0]))
    layers[0].update(zip(w_names, first))

    def gathered(l, names):
        return [_Ride.gather(shards[l][n], w_axes[n]) for n in names] if l > 0 else []

    h = x[0]
    saved = []
    for l, w in enumerate(layers):
        nxt = [None] if l + 1 < depth else []

        def quarter(i, l=l, nxt=nxt):
            return [_Ride.gather(shards[l + 1]["w_in_t"], 0, nxt[0], *W_IN_QUARTERS[i])] if nxt else []

        (qs, k, v, ug, hn), landed = _prenorm_inproj(h, w["g_norm"], w["w_in_t"], f"inproj_{l}", _Ride(quarter(0)))
        nxt[:1] = landed[:1]
        (o, cs), landed = _attn_fwd(qs, k, v, tri, f"attn_fwd_{l}",
                                    _Ride(quarter(1) + gathered(l, ("w_out", "w_ple"))))
        nxt[:1] = landed[:len(landed) - 2 * (l > 0)]
        w.update(zip(("w_out", "w_ple"), landed[len(landed) - 2:] if l > 0 else ()))
        (conv, c2), landed = _conv_fwd(ug, w["dw_w"], w["dw_b"], w["ln_g"], w["ln_b"], f"conv_fwd_{l}",
                                       _Ride(quarter(2) + gathered(l, ("w_gate", "w_pw"))))
        nxt[:1] = landed[:len(landed) - 2 * (l > 0)]
        w.update(zip(("w_gate", "w_pw"), landed[len(landed) - 2:] if l > 0 else ()))
        (h2, h1, ycat, hn2, gate, e, c3), landed = _mix_out_ple(
            o, ug, c2, h, w["p"], head_mean, w["g_attn"], w["g_conv"], w["g_ple"],
            w["w_pw"], w["w_out"], w["w_gate"], w["w_ple"], f"mix_{l}", _Ride(quarter(3)))
        if l + 1 < depth:
            layers[l + 1]["w_in_t"] = landed[0]
        saved.append(dict(h=h, qs=qs, k=k, v=v, ug=ug, hn=hn, o=o, cs=cs, conv=conv, c2=c2, h1=h1,
                          ycat=ycat, hn2=hn2, gate=gate, e=e, c3=c3))
        h = h2
    dh, g_final, loss_part = _final_loss(h, loss_target[0], final_g[None], "final_loss")

    small = {}
    dww_parts = [None] * depth
    slots = [dict() for _ in range(depth)]
    g_w_in = None
    for l in reversed(range(depth)):
        w, s = layers[l], saved[l]
        above = [None] if g_w_in is not None else []

        def part(i, above=above, g=g_w_in):
            return [_Ride.scatter(g, 0, above[0], *W_IN_GRAD_PARTS[i])] if above else []

        def scattered(grads, names):
            return [_Ride.scatter(grads[n], w_axes[n]) for n in names]

        (dh1, dh1b, dzg, de, dycat, g_ple_sum), landed = _ple_out_bwd(
            dh, s["gate"], s["e"], s["h1"], w["g_ple"], w["w_gate"], w["w_out"], f"ple_bwd_{l}", _Ride(part(0)))
        above[:1] = landed
        (do, dga, dgc, dc3, dconv, sums), landed = _branch_bwd(
            dycat, s["o"], s["ug"], s["c3"], s["conv"], head_mean, w["g_attn"], w["g_conv"],
            w["ln_g"], w["ln_b"], w["w_pw"], f"branch_bwd_{l}", _Ride(part(1)))
        above[:1] = landed
        grads = dict(
            w_pw=_weight_grad([s["c2"]], dc3, f"grad_w_pw_{l}"),
            w_out=_weight_grad([s["ycat"]], dh1b, f"grad_w_out_{l}"),
            w_gate=_weight_grad([s["hn2"]], dzg, f"grad_w_gate_{l}"),
            w_ple=_weight_grad([w["p"].astype(BF16)], de, f"grad_w_ple_{l}", tk=PLE_DIM))
        on_conv, on_attn = (("w_out", "w_pw"), ("w_gate", "w_ple")) if l > 0 else ((), w_names[1:])
        (dcv, dcg, dww), landed = _conv_bwd(dconv, s["ug"], w["dw_w"], f"conv_bwd_{l}",
                                            _Ride(part(2) + scattered(grads, on_conv)))
        if above:
            slots[l + 1]["w_in_t"] = landed[0]
        slots[l].update(zip(on_conv, landed[len(landed) - len(on_conv):]))
        (dq, dk, dv), landed = _attn_bwd(s["qs"], s["k"], s["v"], do, s["cs"], tri, tri_t, f"attn_bwd_{l}",
                                         _Ride(scattered(grads, on_attn)))
        slots[l].update(zip(on_attn, landed))
        du = [dq, dk, dv, dga, dcv, dcg, dgc]
        dh, g_norm_sum = _inproj_bwd(du, w["w_in_t"], s["h"], dh1, w["g_norm"], f"inproj_bwd_{l}")
        g_w_in = _weight_grad(du, s["hn"], f"grad_w_in_{l}")
        small[l] = dict(norm_g=g_norm_sum, ple_norm_g=g_ple_sum, attn_out_g=sums[0].reshape(N_HEADS, HEAD_DIM).sum(0),
                        conv_out_g=sums[1], conv_ln_g=sums[2], conv_ln_b=sums[3], dw_b=sums[4])
        dww_parts[l] = dww[:CONV_WIDTH]
    slots[0]["w_in_t"] = _scatter_partials([g_w_in], [0], "scatter_w_in_0")[0]
    slots = [[sl[n] for n in w_names] for sl in slots]
    grad_x = dh[None]

    stack = lambda name: jnp.stack([small[l][name].reshape(-1) for l in range(depth)])
    small_part = _pack_small(stack("norm_g"), stack("ple_norm_g"), g_final[0], stack("dw_b"), stack("conv_ln_g"),
                             stack("conv_ln_b"), stack("conv_out_g"), stack("attn_out_g"), scalar=loss_part[0, 0])
    pack = jnp.concatenate([small_part, jnp.concatenate(dww_parts, axis=1),
                            jnp.zeros((PACK_ROWS - SMALL_ROWS - CONV_WIDTH, D_MODEL), F32)], axis=0)
    (all_packs,) = _all_gather([pack], [0], "gather_small_grads")
    (pack_sum,) = _sum_slots([all_packs.reshape(N_DEV, PACK_ROWS, D_MODEL)], "sum_small_grads")
    loss = pack_sum[SMALL_ROWS - 8 + 2, 0]
    g_small = pack_sum[:SMALL_ROWS].at[SMALL_ROWS - 8 + 2, 0].set(0.0)
    dww_full = pack_sum[SMALL_ROWS:SMALL_ROWS + CONV_WIDTH].reshape(CONV_WIDTH, depth, CONV_DIM).transpose(1, 0, 2)
    g_dw_w = lax.dynamic_slice_in_dim(dww_full, my_idx * (CONV_DIM // N_DEV), CONV_DIM // N_DEV, axis=2)

    w_small = _pack_small(norm_g, ple_norm_g, final_g, dw_b, conv_ln_g, conv_ln_b, conv_out_g, attn_out_g)
    m_small = _pack_small(m_norm_g, m_ple_norm_g, m_final_g, m_dw_b, m_conv_ln_g, m_conv_ln_b, m_conv_out_g, m_attn_out_g)
    v_small = _pack_small(v_norm_g, v_ple_norm_g, v_final_g, v_dw_b, v_conv_ln_g, v_conv_ln_b, v_conv_out_g, v_attn_out_g)
    d_small, nm_small, nv_small = _adamw(w_small, g_small, m_small, v_small, "adamw_small")
    res = {"g": _unpack_small(g_small), "d": _unpack_small(d_small), "m": _unpack_small(nm_small),
           "v": _unpack_small(nv_small)}

    sums = [_sum_slots(slots[l], f"sum_grads_{l}") for l in range(depth)]
    big = {
        "w_in": jnp.stack([sums[l][0].T for l in range(depth)]),
        "w_pw": jnp.stack([sums[l][1] for l in range(depth)]),
        "w_out": jnp.stack([sums[l][2] for l in range(depth)]),
        "w_ple_gate": jnp.stack([sums[l][3] for l in range(depth)]),
        "w_ple": jnp.stack([sums[l][4] for l in range(depth)]),
        "dw_w": g_dw_w,
    }
    state = {"w_in": (w_in, m_w_in, v_w_in), "w_pw": (w_pw, m_w_pw, v_w_pw), "w_out": (w_out, m_w_out, v_w_out),
             "w_ple_gate": (w_ple_gate, m_w_ple_gate, v_w_ple_gate), "w_ple": (w_ple, m_w_ple, v_w_ple),
             "dw_w": (dw_w, m_dw_w, v_dw_w)}
    for name, g in big.items():
        wv, mv, vv = state[name]
        flat = lambda a: a.reshape(-1, a.shape[-1])
        d, nm, nv = _adamw(flat(wv), flat(g), flat(mv), flat(vv), f"adamw_{name}")
        res["g"][name] = g
        res["d"][name], res["m"][name], res["v"][name] = (d.reshape(wv.shape), nm.reshape(wv.shape),
                                                         nv.reshape(wv.shape))

    order = ["norm_g", "w_in", "attn_out_g", "dw_w", "dw_b", "conv_ln_g", "conv_ln_b", "w_pw", "conv_out_g",
             "w_out", "ple_norm_g", "w_ple_gate", "w_ple", "final_g"]
    return (loss, grad_x, *[res["g"][n] for n in order], *[res["d"][n] for n in order],
            *[res["m"][n] for n in order], *[res["v"][n] for n in order])
```

```python
import functools

import jax
import jax.numpy as jnp
from jax import lax
from jax.experimental import pallas as pl
from jax.experimental.pallas import tpu as pltpu

F32 = jnp.float32
BF16 = jnp.bfloat16
MESH = pl.DeviceIdType.MESH

N_DEV = 8
D_MODEL = 1024
ATTN_DIM = 512
CONV_DIM = 512
HEAD_DIM = 64
N_HEADS = 8
CONV_WIDTH = 31
PLE_DIM = 256
CHUNK = 512
N_CHUNK = 7
EPS = 1e-6
ADAM_LR = 0.001
ADAM_B1 = 0.9
ADAM_B2 = 0.999
ADAM_EPS = 1e-08
ADAM_WD = 0.01
ADAM_STEP = 10

LANES = 128
BLK = 256
ATT_COLS = 4
DEAD_AT = 110.0
FIRST_BLOCK_LANE = HEAD_DIM - 1
TM = 256
HALO = 32
SUBLANES = 8
CONV_ROWS = 32
VMEM_LIMIT = 56 * 1024 * 1024
SMALL_ROWS = 16
W_IN_QUARTERS = ((0, 96), (96, 144), (240, 96), (336, 112))
W_IN_GRAD_PARTS = ((0, 112), (112, 96), (208, 240))
PACK_ROWS = 48


def _nn(a, b):
    return lax.dot_general(a, b, (((1,), (0,)), ((), ())), preferred_element_type=F32)


def _nt(a, b):
    return lax.dot_general(a, b, (((1,), (1,)), ((), ())), preferred_element_type=F32)


def _tn(a, b):
    return lax.dot_general(a, b, (((0,), (0,)), ((), ())), preferred_element_type=F32)


def _split(x):
    hi = x.astype(BF16)
    lo = (x - hi.astype(F32)).astype(BF16)
    return hi, lo


def _dot_hilo(x, m):
    hi, lo = _split(x)
    return _nn(hi, m) + _nn(lo, m)


def _sigmoid(x):
    return jax.nn.sigmoid(x)


def _dsilu(x, s):
    return s * (1.0 + x * (1.0 - s))


def _params(n_grid=0, vmem=VMEM_LIMIT):
    sem = ("arbitrary",) * n_grid if n_grid else None
    return pltpu.CompilerParams(dimension_semantics=sem, vmem_limit_bytes=vmem)


def _rows(tm, cols, col=0):
    return pl.BlockSpec((tm, cols), lambda i: (i, col))


def _whole(shape):
    zeros = (0,) * len(shape)
    return pl.BlockSpec(shape, lambda *_: zeros)


def _my_position():
    return lax.axis_index("x"), lax.axis_index("y"), lax.axis_index("c")


def _block(ref, axis, idx, size):
    start = pl.multiple_of(idx * size, size)
    if axis == 0:
        return ref.at[pl.ds(start, size), :]
    return ref.at[:, pl.ds(start, size)]


def _all_gather(shards, axes, name):
    n = len(shards)
    sizes = [s.shape[a] for s, a in zip(shards, axes)]

    def full_shape(s, a):
        shape = list(s.shape)
        shape[a] *= N_DEV
        return jax.ShapeDtypeStruct(tuple(shape), s.dtype)

    def body(*refs):
        ins, outs = refs[:n], refs[n:2 * n]
        send_sems, recv_sems, local_sems = refs[2 * n:]
        x, y, c = _my_position()
        me, sibling = (x, y, c), (x, y, 1 - c)
        chips = [(1 - x, y), (x, 1 - y), (1 - x, 1 - y)]

        def place(i, dev):
            return _block(outs[i], axes[i], 4 * dev[0] + 2 * dev[1] + dev[2], sizes[i])

        def copy(k, i, dev, to, src=None):
            return pltpu.make_async_remote_copy(
                src_ref=place(i, dev) if src is None else src, dst_ref=place(i, dev),
                send_sem=send_sems.at[k, i], recv_sem=recv_sems.at[k, i],
                device_id=to, device_id_type=MESH)

        mine = [pltpu.make_async_copy(ins[i], place(i, me), local_sems.at[i]) for i in range(n)]
        for cp in mine:
            cp.start()
        first = [copy(0, i, me, sibling, src=ins[i]) for i in range(n)]
        for j, chip in enumerate(chips):
            first += [copy(1 + j, i, me, (*chip, c), src=ins[i]) for i in range(n)]
        for cp in first:
            cp.start()
        passed = []
        for j, chip in enumerate(chips):
            for i in range(n):
                copy(1 + j, i, (*chip, c), me).wait_recv()
            hop = [copy(4 + j, i, (*chip, c), sibling) for i in range(n)]
            for cp in hop:
                cp.start()
            passed += hop
        for i in range(n):
            copy(0, i, sibling, me).wait_recv()
        for j, chip in enumerate(chips):
            for i in range(n):
                copy(4 + j, i, (*chip, 1 - c), me).wait_recv()
        for cp in first + passed:
            cp.wait_send()
        for cp in mine:
            cp.wait()

    any_spec = pl.BlockSpec(memory_space=pl.ANY)
    return pl.pallas_call(
        body, name=name,
        out_shape=[full_shape(s, a) for s, a in zip(shards, axes)],
        in_specs=[any_spec] * n, out_specs=[any_spec] * n,
        scratch_shapes=[pltpu.SemaphoreType.DMA((7, n)), pltpu.SemaphoreType.DMA((7, n)),
                        pltpu.SemaphoreType.DMA((n,))],
    )(*shards)


def _scatter_partials(fulls, axes, name):
    n = len(fulls)
    sizes = [f.shape[a] // N_DEV for f, a in zip(fulls, axes)]

    def slot_shape(f, a):
        shape = list(f.shape)
        shape[a] //= N_DEV
        return jax.ShapeDtypeStruct((N_DEV, *shape), f.dtype)

    def body(*refs):
        ins, outs = refs[:n], refs[n:2 * n]
        send_sems, recv_sems, local_sems = refs[2 * n:]
        x, y, c = _my_position()
        my_idx = 4 * x + 2 * y + c

        def peer_of(k):
            px = 1 - x if k & 4 else x
            py = 1 - y if k & 2 else y
            pc = 1 - c if k & 1 else c
            return (px, py, pc), 4 * px + 2 * py + pc

        def copy(k, i):
            peer, peer_idx = peer_of(k)
            return pltpu.make_async_remote_copy(
                src_ref=_block(ins[i], axes[i], peer_idx, sizes[i]), dst_ref=outs[i].at[my_idx],
                send_sem=send_sems.at[k - 1, i], recv_sem=recv_sems.at[k - 1, i],
                device_id=peer, device_id_type=MESH)

        def arrival(k, i):
            peer, peer_idx = peer_of(k)
            return pltpu.make_async_remote_copy(
                src_ref=_block(ins[i], axes[i], my_idx, sizes[i]), dst_ref=outs[i].at[peer_idx],
                send_sem=send_sems.at[k - 1, i], recv_sem=recv_sems.at[k - 1, i],
                device_id=peer, device_id_type=MESH)

        mine = [pltpu.make_async_copy(_block(ins[i], axes[i], my_idx, sizes[i]), outs[i].at[my_idx],
                                      local_sems.at[i]) for i in range(n)]
        sends = [copy(k, i) for k in range(1, N_DEV) for i in range(n)]
        for cp in mine + sends:
            cp.start()
        for k in range(1, N_DEV):
            for i in range(n):
                arrival(k, i).wait_recv()
        for cp in sends:
            cp.wait_send()
        for cp in mine:
            cp.wait()

    any_spec = pl.BlockSpec(memory_space=pl.ANY)
    return pl.pallas_call(
        body, name=name,
        out_shape=[slot_shape(f, a) for f, a in zip(fulls, axes)],
        in_specs=[any_spec] * n, out_specs=[any_spec] * n,
        scratch_shapes=[pltpu.SemaphoreType.DMA((7, n)), pltpu.SemaphoreType.DMA((7, n)),
                        pltpu.SemaphoreType.DMA((n,))],
    )(*fulls)


class _Ride:
    def __init__(self, parts):
        self.parts = [p for p in parts if p is not None]

    @staticmethod
    def gather(src, axis, land=None, lo=0, n=None):
        return ("gather", src, land, axis, lo, src.shape[axis] if n is None else n)

    @staticmethod
    def scatter(src, axis, land=None, lo=0, n=None):
        return ("scatter", src, land, axis, lo, src.shape[axis] // N_DEV if n is None else n)

    def arrays(self):
        return [p[1] for p in self.parts] + [p[2] for p in self.parts if p[2] is not None]

    def out_shapes(self):
        out = []
        for kind, src, _, axis, _, _ in self.parts:
            shape = list(src.shape)
            if kind == "gather":
                shape[axis] *= N_DEV
            else:
                shape[axis] //= N_DEV
                shape = [N_DEV] + shape
            out.append(jax.ShapeDtypeStruct(tuple(shape), src.dtype))
        return out

    def aliases(self, n_in, n_out):
        m, out = len(self.parts), {}
        for j, p in enumerate(self.parts):
            if p[2] is not None:
                out[n_in + m + len(out)] = n_out + j
        return out

    def scratch(self):
        m = len(self.parts)
        return [pltpu.SemaphoreType.DMA((N_DEV - 1, m)), pltpu.SemaphoreType.DMA((N_DEV - 1, m)),
                pltpu.SemaphoreType.DMA((m,))]

    def _copies(self, src_refs, land_refs, sems):
        send_sems, recv_sems, local_sems = sems
        x, y, c = _my_position()
        my_idx = 4 * x + 2 * y + c
        own, sends, lands = [], [], []
        for j, (kind, src, _, axis, lo, n) in enumerate(self.parts):
            size = src.shape[axis] if kind == "gather" else src.shape[axis] // N_DEV
            align = 16 if axis == 0 else LANES

            def rows(ref, idx, lead=None, axis=axis, lo=lo, n=n, size=size, align=align):
                at = pl.ds(pl.multiple_of(idx * size + lo, align), n)
                where = (at, slice(None)) if axis == 0 else (slice(None), at)
                return ref.at[where] if lead is None else ref.at[(lead, *where)]

            def in_shard(ref):
                return rows(ref, 0)

            def in_slot(ref, s):
                return rows(ref, 0, lead=s)

            for k in range(N_DEV):
                px = 1 - x if k & 4 else x
                py = 1 - y if k & 2 else y
                pc = 1 - c if k & 1 else c
                peer_idx = 4 * px + 2 * py + pc
                if kind == "gather":
                    a, b, landed = in_shard(src_refs[j]), rows(land_refs[j], my_idx), rows(land_refs[j], peer_idx)
                else:
                    a, b, landed = rows(src_refs[j], peer_idx), in_slot(land_refs[j], my_idx), in_slot(land_refs[j], peer_idx)
                if k == 0:
                    own.append(pltpu.make_async_copy(a, b, local_sems.at[j]))
                    continue
                mk = lambda dst, a=a, k=k, j=j, to=(px, py, pc): pltpu.make_async_remote_copy(
                    src_ref=a, dst_ref=dst, send_sem=send_sems.at[k - 1, j], recv_sem=recv_sems.at[k - 1, j],
                    device_id=to, device_id_type=MESH)
                sends.append(mk(b))
                lands.append(mk(landed))
        return own, sends, lands

    def start(self, src_refs, land_refs, sems):
        own, sends, _ = self._copies(src_refs, land_refs, sems)
        for cp in own + sends:
            cp.start()

    def wait(self, src_refs, land_refs, sems):
        own, sends, lands = self._copies(src_refs, land_refs, sems)
        for cp in lands:
            cp.wait_recv()
        for cp in sends:
            cp.wait_send()
        for cp in own:
            cp.wait()


def _call(body, *, name, grid, in_specs, out_specs, out_shape, args, scratch_shapes=(), ride=None):
    in_specs, out_specs, out_shape = list(in_specs), list(out_specs), list(out_shape)
    n_in, n_out, n_sc = len(in_specs), len(out_specs), len(scratch_shapes)
    if ride is None or not ride.parts:
        res = pl.pallas_call(body, name=name, grid=grid, in_specs=in_specs, out_specs=out_specs,
                             out_shape=out_shape, scratch_shapes=list(scratch_shapes),
                             compiler_params=_params(len(grid)))(*args)
        return list(res), []
    extra, m = ride.arrays(), len(ride.parts)

    def riding(*refs):
        a = n_in + len(extra)
        b = a + n_out
        srcs, lands, sems = refs[n_in:n_in + m], refs[b:b + m], refs[b + m + n_sc:]
        at = [pl.program_id(d) for d in range(len(grid))]

        @pl.when(functools.reduce(jnp.logical_and, [i == 0 for i in at]))
        def _():
            ride.start(srcs, lands, sems)

        body(*refs[:n_in], *refs[a:b], *refs[b + m:b + m + n_sc])

        @pl.when(functools.reduce(jnp.logical_and, [i == g - 1 for i, g in zip(at, grid)]))
        def _():
            ride.wait(srcs, lands, sems)

    hbm = pl.BlockSpec(memory_space=pl.ANY)
    res = pl.pallas_call(
        riding, name=name, grid=grid, in_specs=in_specs + [hbm] * len(extra), out_specs=out_specs + [hbm] * m,
        out_shape=out_shape + ride.out_shapes(), scratch_shapes=list(scratch_shapes) + ride.scratch(),
        input_output_aliases=ride.aliases(n_in, n_out), compiler_params=_params(len(grid)),
    )(*args, *extra)
    return list(res[:n_out]), list(res[n_out:])


def _prenorm_inproj(h, gain, w_in_t, name, ride=None):
    T = h.shape[0]

    def body(h_ref, g_ref, w_ref, q_ref, k_ref, v_ref, ug_ref, hn_ref):
        hv = h_ref[...]
        r = lax.rsqrt(jnp.mean(hv * hv, axis=-1, keepdims=True) + EPS)
        hn = (hv * r * g_ref[...]).astype(BF16)
        hn_ref[...] = hn
        for j in range(N_CHUNK):
            u = _nt(hn, w_ref[j * CHUNK:(j + 1) * CHUNK, :])
            if j == 0:
                q_ref[...] = (u * (HEAD_DIM ** -0.5)).astype(BF16)
            elif j == 1:
                k_ref[...] = u.astype(BF16)
            elif j == 2:
                v_ref[...] = u.astype(BF16)
            else:
                ug_ref[:, (j - 3) * CHUNK:(j - 2) * CHUNK] = u

    act = jax.ShapeDtypeStruct((T, CHUNK), BF16)
    return _call(
        body, name=name, grid=(T // TM,),
        in_specs=[_rows(TM, D_MODEL), _whole((1, D_MODEL)), _whole((N_CHUNK * CHUNK, D_MODEL))],
        out_specs=[_rows(TM, CHUNK)] * 3 + [_rows(TM, 4 * CHUNK), _rows(TM, D_MODEL)],
        out_shape=[act, act, act, jax.ShapeDtypeStruct((T, 4 * CHUNK), F32),
                   jax.ShapeDtypeStruct((T, D_MODEL), BF16)],
        args=(h, gain, w_in_t,), ride=ride)


def _softplus_parts(z):
    e = jnp.exp(-jnp.abs(z))
    return e, jnp.maximum(z, 0.0) + jnp.log(1.0 + e)


def _attn_fwd(qs, k, v, tri, name, ride=None):
    T = qs.shape[0]
    assert T // BLK <= FIRST_BLOCK_LANE, "one lane per key block below the lane of the first block"
    width = LANES * ATT_COLS
    chains = [(c, half) for c in range(ATT_COLS) for half in range(2)]

    def body(q_ref, k_ref, v_ref, m_ref, o_ref, cs_ref):
        qi = pl.program_id(1)
        lane = lax.broadcasted_iota(jnp.int32, (BLK, LANES), 1)
        first = lane < HEAD_DIM
        causal = (lax.broadcasted_iota(jnp.int32, (BLK, BLK), 1)
                  < lax.broadcasted_iota(jnp.int32, (BLK, BLK), 0))
        tri_m = m_ref[...]
        qh = {}
        for c in range(ATT_COLS):
            q = q_ref[:, c * LANES:(c + 1) * LANES]
            zero = jnp.zeros_like(q)
            qh[c, 0], qh[c, 1] = jnp.where(first, q, zero), jnp.where(first, zero, q)

        def step(kb, state, masked):
            carries, accs, cvals = state
            start = pl.multiple_of(kb * BLK, BLK)
            kblk = [k_ref[pl.ds(start, BLK), c * LANES:(c + 1) * LANES] for c in range(ATT_COLS)]
            vblk = [v_ref[pl.ds(start, BLK), c * LANES:(c + 1) * LANES] for c in range(ATT_COLS)]
            z = [_nt(qh[ch], kblk[ch[0]]) for ch in chains]
            sp = [_softplus_parts(zi)[1] for zi in z]
            if masked:
                sp = [jnp.where(causal, s, 0.0) for s in sp]
            incl = [_dot_hilo(s, tri_m) for s in sp]
            a = [jnp.exp(zi - ii - ci) for zi, ii, ci in zip(z, incl, carries)]
            if masked:
                a = [jnp.where(causal, ai, 0.0) for ai in a]
            accs, cvals = list(accs), list(cvals)
            for n, (c, half) in enumerate(chains):
                zero = jnp.zeros_like(vblk[c])
                vh = jnp.where(first, vblk[c], zero) if half == 0 else jnp.where(first, zero, vblk[c])
                accs[c] = accs[c] + _nn(a[n].astype(BF16), vh)
                cvals[c] = jnp.where(lane == kb + HEAD_DIM * half, carries[n], cvals[c])
            carries = tuple(ci + ii[:, 0:1] for ci, ii in zip(carries, incl))
            return carries, tuple(accs), tuple(cvals)

        zeros = tuple(jnp.zeros((BLK, LANES), F32) for _ in range(ATT_COLS))
        state = (tuple(jnp.zeros((BLK, 1), F32) for _ in chains), zeros, zeros)
        state = step(qi, state, True)

        def reaches_further(st):
            it, (carries, _, _) = st
            least = functools.reduce(jnp.minimum, carries)
            return jnp.logical_and(it < qi, jnp.min(least) < DEAD_AT)

        done, state = lax.while_loop(reaches_further, lambda st: (st[0] + 1, step(qi - 1 - st[0], st[1], False)),
                                     (jnp.int32(0), state))
        first_block = (qi - done).astype(F32)
        for c in range(ATT_COLS):
            o_ref[:, c * LANES:(c + 1) * LANES] = state[1][c]
            cs_ref[:, c * LANES:(c + 1) * LANES] = jnp.where(lane == FIRST_BLOCK_LANE, first_block, state[2][c])

    blk = pl.BlockSpec((BLK, width), lambda j, i: (i, j))
    col = pl.BlockSpec((T, width), lambda j, i: (0, j))
    out = jax.ShapeDtypeStruct((T, ATTN_DIM), F32)
    return _call(
        body, name=name, grid=(ATTN_DIM // width, T // BLK),
        in_specs=[blk, col, col, _whole((BLK, BLK))],
        out_specs=[blk, blk], out_shape=[out, out],
        args=(qs, k, v, tri,), ride=ride)


def _shifted_copies(pad_ref, sh_ref):
    rows = sh_ref.shape[1]
    for b in range(SUBLANES):
        sh_ref[b] = pad_ref[b:b + rows, :]


def _shift_of(offset):
    return offset % SUBLANES, offset - offset % SUBLANES


def _conv_fwd(ug, dw_w, dw_b, ln_g, ln_b, name, ride=None):
    T = ug.shape[0]
    per = TM // HALO

    def body(cv_ref, cg_ref, cvh_ref, cgh_ref, w_ref, b_ref, g_ref, beta_ref, conv_ref, c2_ref, pad_ref, sh_ref):
        i = pl.program_id(0)
        halo = cvh_ref[...] * _sigmoid(cgh_ref[...])
        pad_ref[0:HALO, :] = jnp.where(i == 0, 0.0, halo)
        pad_ref[HALO:HALO + TM, :] = cv_ref[...] * _sigmoid(cg_ref[...])
        pad_ref[HALO + TM:, :] = jnp.zeros((SUBLANES, CONV_DIM), F32)
        _shifted_copies(pad_ref, sh_ref)
        taps = [w_ref[t:t + 1, :] for t in range(CONV_WIDTH)]

        def rows(j, _):
            r = pl.multiple_of(j * CONV_ROWS, CONV_ROWS)
            acc = jnp.zeros((CONV_ROWS, CONV_DIM), F32) + b_ref[...]
            for t in range(CONV_WIDTH):
                b, a = _shift_of(HALO - (CONV_WIDTH - 1) + t)
                acc = acc + taps[t] * sh_ref[b, pl.ds(r + a, CONV_ROWS), :]
            conv_ref[pl.ds(r, CONV_ROWS), :] = acc
            return 0

        lax.fori_loop(0, TM // CONV_ROWS, rows, 0)
        acc = conv_ref[...]
        mu = jnp.mean(acc, axis=-1, keepdims=True)
        xc = acc - mu
        rs = lax.rsqrt(jnp.mean(xc * xc, axis=-1, keepdims=True) + EPS)
        ln = xc * rs * g_ref[...] + beta_ref[...]
        c2_ref[...] = (ln * _sigmoid(ln)).astype(BF16)

    prev = lambda col: pl.BlockSpec((HALO, CHUNK), lambda i: (jnp.maximum(i * per - 1, 0), col))
    vec = _whole((1, CONV_DIM))
    return _call(
        body, name=name, grid=(T // TM,),
        in_specs=[_rows(TM, CHUNK, 1), _rows(TM, CHUNK, 2), prev(1), prev(2),
                  _whole((CONV_WIDTH, CONV_DIM)), vec, vec, vec],
        out_specs=[_rows(TM, CONV_DIM), _rows(TM, CONV_DIM)],
        out_shape=[jax.ShapeDtypeStruct((T, CONV_DIM), F32), jax.ShapeDtypeStruct((T, CONV_DIM), BF16)],
        scratch_shapes=[pltpu.VMEM((TM + HALO + SUBLANES, CONV_DIM), F32),
                        pltpu.VMEM((SUBLANES, TM + HALO, CONV_DIM), F32)],
        args=(ug, ug, ug, ug, dw_w, dw_b, ln_g, ln_b,), ride=ride)


def _mix_out_ple(o, ug, c2, h, p, head_mean, g_attn, g_conv, g_ple, w_pw, w_out, w_gate, w_ple, name, ride=None):
    T = h.shape[0]

    def body(o_ref, ga_ref, gc_ref, c2_ref, h_ref, p_ref, hm_ref, gao_ref, gco_ref, gpn_ref,
             wpw_ref, wout_ref, wg_ref, wple_ref,
             h2_ref, h1_ref, ycat_ref, hn2_ref, gate_ref, e_ref, c3_ref):
        ov = o_ref[...]
        rh = lax.rsqrt(_dot_hilo(ov * ov, hm_ref[...]) + EPS)
        ga = ga_ref[...]
        ya = (ov * rh * gao_ref[...] * (ga * _sigmoid(ga))).astype(BF16)
        c3 = _nn(c2_ref[...], wpw_ref[...])
        c3_ref[...] = c3
        rc = lax.rsqrt(jnp.mean(c3 * c3, axis=-1, keepdims=True) + EPS)
        gc = gc_ref[...]
        yc = (c3 * rc * gco_ref[...] * (gc * _sigmoid(gc))).astype(BF16)
        ycat_ref[:, :ATTN_DIM] = ya
        ycat_ref[:, ATTN_DIM:] = yc
        h1 = h_ref[...] + _nn(ya, wout_ref[:ATTN_DIM, :]) + _nn(yc, wout_ref[ATTN_DIM:, :])
        h1_ref[...] = h1
        r1 = lax.rsqrt(jnp.mean(h1 * h1, axis=-1, keepdims=True) + EPS)
        hn2 = (h1 * r1 * gpn_ref[...]).astype(BF16)
        hn2_ref[...] = hn2
        gate = _sigmoid(_nn(hn2, wg_ref[...]))
        e = _nn(p_ref[...].astype(BF16), wple_ref[...])
        gate_ref[...] = gate
        e_ref[...] = e
        h2_ref[...] = h1 + e * gate

    f32 = lambda cols: jax.ShapeDtypeStruct((T, cols), F32)
    bf = lambda cols: jax.ShapeDtypeStruct((T, cols), BF16)
    return _call(
        body, name=name, grid=(T // TM,),
        in_specs=[_rows(TM, ATTN_DIM), _rows(TM, CHUNK, 0), _rows(TM, CHUNK, 3), _rows(TM, CONV_DIM),
                  _rows(TM, D_MODEL), _rows(TM, PLE_DIM), _whole((ATTN_DIM, ATTN_DIM)),
                  _whole((1, ATTN_DIM)), _whole((1, CONV_DIM)), _whole((1, D_MODEL)),
                  _whole((CONV_DIM, CONV_DIM)), _whole((D_MODEL, D_MODEL)), _whole((D_MODEL, D_MODEL)),
                  _whole((PLE_DIM, D_MODEL))],
        out_specs=[_rows(TM, D_MODEL), _rows(TM, D_MODEL), _rows(TM, D_MODEL), _rows(TM, D_MODEL),
                   _rows(TM, D_MODEL), _rows(TM, D_MODEL), _rows(TM, CONV_DIM)],
        out_shape=[f32(D_MODEL), f32(D_MODEL), bf(D_MODEL), bf(D_MODEL), f32(D_MODEL), f32(D_MODEL),
                   f32(CONV_DIM)],
        args=(o, ug, ug, c2, h, p, head_mean, g_attn, g_conv, g_ple, w_pw, w_out, w_gate, w_ple,), ride=ride)


def _final_loss(h, target, gain, name):
    T = h.shape[0]

    def body(h_ref, t_ref, g_ref, dh_ref, gsum_ref, loss_ref):
        @pl.when(pl.program_id(0) == 0)
        def _():
            gsum_ref[...] = jnp.zeros_like(gsum_ref)
            loss_ref[...] = jnp.zeros_like(loss_ref)

        hv = h_ref[...]
        r = lax.rsqrt(jnp.mean(hv * hv, axis=-1, keepdims=True) + EPS)
        xh = hv * r
        diff = xh * g_ref[...] - t_ref[...]
        loss_ref[...] += 0.5 * jnp.sum(jnp.mean(diff * diff, axis=-1, keepdims=True), axis=0, keepdims=True)
        dy = diff * (1.0 / D_MODEL)
        gsum_ref[...] += jnp.sum(dy * xh, axis=0, keepdims=True)
        dxh = dy * g_ref[...]
        dh_ref[...] = r * (dxh - xh * jnp.mean(dxh * xh, axis=-1, keepdims=True))

    return pl.pallas_call(
        body, name=name, grid=(T // TM,),
        in_specs=[_rows(TM, D_MODEL), _rows(TM, D_MODEL), _whole((1, D_MODEL))],
        out_specs=[_rows(TM, D_MODEL), _whole((1, D_MODEL)), _whole((1, LANES))],
        out_shape=[jax.ShapeDtypeStruct((T, D_MODEL), F32), jax.ShapeDtypeStruct((1, D_MODEL), F32),
                   jax.ShapeDtypeStruct((1, LANES), F32)],
        compiler_params=_params(1),
    )(h, target, gain)


def _ple_out_bwd(dh2, gate, e, h1, g_ple, w_gate, w_out, name, ride=None):
    T = dh2.shape[0]

    def body(dh2_ref, gate_ref, e_ref, h1_ref, gpn_ref, wg_ref, wout_ref,
             dh1_ref, dh1b_ref, dzg_ref, de_ref, dycat_ref, gsum_ref):
        @pl.when(pl.program_id(0) == 0)
        def _():
            gsum_ref[...] = jnp.zeros_like(gsum_ref)

        dh2v = dh2_ref[...]
        gate = gate_ref[...]
        de_ref[...] = (dh2v * gate).astype(BF16)
        dzg = (dh2v * e_ref[...] * gate * (1.0 - gate)).astype(BF16)
        dzg_ref[...] = dzg
        dhn2 = _nt(dzg, wg_ref[...])
        h1 = h1_ref[...]
        r1 = lax.rsqrt(jnp.mean(h1 * h1, axis=-1, keepdims=True) + EPS)
        xh = h1 * r1
        gsum_ref[...] += jnp.sum(dhn2 * xh, axis=0, keepdims=True)
        dxh = dhn2 * gpn_ref[...]
        dh1 = dh2v + r1 * (dxh - xh * jnp.mean(dxh * xh, axis=-1, keepdims=True))
        dh1_ref[...] = dh1
        dh1b = dh1.astype(BF16)
        dh1b_ref[...] = dh1b
        dycat_ref[...] = _nt(dh1b, wout_ref[...])

    f32 = jax.ShapeDtypeStruct((T, D_MODEL), F32)
    bf = jax.ShapeDtypeStruct((T, D_MODEL), BF16)
    full = _rows(TM, D_MODEL)
    return _call(
        body, name=name, grid=(T // TM,),
        in_specs=[full, full, full, full, _whole((1, D_MODEL)), _whole((D_MODEL, D_MODEL)),
                  _whole((D_MODEL, D_MODEL))],
        out_specs=[full, full, full, full, full, _whole((1, D_MODEL))],
        out_shape=[f32, bf, bf, bf, f32, jax.ShapeDtypeStruct((1, D_MODEL), F32)],
        args=(dh2, gate, e, h1, g_ple, w_gate, w_out,), ride=ride)


def _branch_bwd(dycat, o, ug, c3, conv, head_mean, g_attn, g_conv, ln_g, ln_b, w_pw, name, ride=None):
    T = o.shape[0]

    def body(dya_ref, dyc_ref, o_ref, ga_ref, gc_ref, c3_ref, conv_ref, hm_ref, gao_ref, gco_ref,
             lng_ref, lnb_ref, wpw_ref,
             do_ref, dga_ref, dgc_ref, dc3_ref, dconv_ref, sums_ref):
        @pl.when(pl.program_id(0) == 0)
        def _():
            sums_ref[...] = jnp.zeros_like(sums_ref)

        hm = hm_ref[...]
        col = lambda x: jnp.sum(x, axis=0, keepdims=True)
        ov = o_ref[...]
        rh = lax.rsqrt(_dot_hilo(ov * ov, hm) + EPS)
        xh = ov * rh
        ga = ga_ref[...]
        sg = _sigmoid(ga)
        dya = dya_ref[...]
        don = dya * (ga * sg)
        dga_ref[...] = (dya * xh * gao_ref[...] * _dsilu(ga, sg)).astype(BF16)
        sums_ref[0:1, :] += col(don * xh)
        dxh = don * gao_ref[...]
        do_ref[...] = (rh * (dxh - xh * _dot_hilo(dxh * xh, hm))).astype(BF16)
        c3 = c3_ref[...]
        rc = lax.rsqrt(jnp.mean(c3 * c3, axis=-1, keepdims=True) + EPS)
        xh3 = c3 * rc
        gc = gc_ref[...]
        sgc = _sigmoid(gc)
        dyc = dyc_ref[...]
        dn3 = dyc * (gc * sgc)
        dgc_ref[...] = (dyc * xh3 * gco_ref[...] * _dsilu(gc, sgc)).astype(BF16)
        sums_ref[1:2, :] += col(dn3 * xh3)
        dxh3 = dn3 * gco_ref[...]
        dc3 = (rc * (dxh3 - xh3 * jnp.mean(dxh3 * xh3, axis=-1, keepdims=True))).astype(BF16)
        dc3_ref[...] = dc3
        dc2 = _nt(dc3, wpw_ref[...])
        cv = conv_ref[...]
        mu = jnp.mean(cv, axis=-1, keepdims=True)
        xc = cv - mu
        rs = lax.rsqrt(jnp.mean(xc * xc, axis=-1, keepdims=True) + EPS)
        xn = xc * rs
        ln = xn * lng_ref[...] + lnb_ref[...]
        dln = dc2 * _dsilu(ln, _sigmoid(ln))
        sums_ref[2:3, :] += col(dln * xn)
        sums_ref[3:4, :] += col(dln)
        dxn = dln * lng_ref[...]
        dconv = rs * (dxn - jnp.mean(dxn, axis=-1, keepdims=True)
                      - xn * jnp.mean(dxn * xn, axis=-1, keepdims=True))
        dconv_ref[...] = dconv
        sums_ref[4:5, :] += col(dconv)

    half = lambda dt: jax.ShapeDtypeStruct((T, CHUNK), dt)
    tile = _rows(TM, CHUNK)
    vec = _whole((1, CHUNK))
    return _call(
        body, name=name, grid=(T // TM,),
        in_specs=[_rows(TM, CHUNK, 0), _rows(TM, CHUNK, 1), tile, _rows(TM, CHUNK, 0), _rows(TM, CHUNK, 3),
                  tile, tile, _whole((ATTN_DIM, ATTN_DIM)), vec, vec, vec, vec, _whole((CONV_DIM, CONV_DIM))],
        out_specs=[tile, tile, tile, tile, tile, _whole((8, CHUNK))],
        out_shape=[half(BF16), half(BF16), half(BF16), half(BF16), half(F32),
                   jax.ShapeDtypeStruct((8, CHUNK), F32)],
        args=(dycat, dycat, o, ug, ug, c3, conv, head_mean, g_attn, g_conv, ln_g, ln_b, w_pw,), ride=ride)


def _conv_bwd(dconv, ug, dw_w, name, ride=None):
    T = dconv.shape[0]
    per = TM // HALO
    last = T // HALO - 1
    n_tiles = T // TM

    def body(d_ref, dn_ref, cv_ref, cg_ref, cvh_ref, cgh_ref, w_ref, dcv_ref, dcg_ref, dw_ref,
             dpad_ref, cpad_ref, dsh_ref, csh_ref, dw_acc):
        i = pl.program_id(0)

        @pl.when(i == 0)
        def _():
            dw_acc[...] = jnp.zeros_like(dw_acc)

        tail = jnp.zeros((SUBLANES, CONV_DIM), F32)
        dpad_ref[0:TM, :] = d_ref[...]
        dpad_ref[TM:TM + HALO, :] = jnp.where(i == n_tiles - 1, 0.0, dn_ref[...])
        dpad_ref[TM + HALO:, :] = tail
        halo = cvh_ref[...] * _sigmoid(cgh_ref[...])
        cpad_ref[0:HALO, :] = jnp.where(i == 0, 0.0, halo)
        cpad_ref[HALO:HALO + TM, :] = cv_ref[...] * _sigmoid(cg_ref[...])
        cpad_ref[HALO + TM:, :] = tail
        _shifted_copies(dpad_ref, dsh_ref)
        _shifted_copies(cpad_ref, csh_ref)
        taps = [w_ref[t:t + 1, :] for t in range(CONV_WIDTH)]

        def rows(j, _):
            r = pl.multiple_of(j * CONV_ROWS, CONV_ROWS)
            d = d_ref[pl.ds(r, CONV_ROWS), :]
            dc = jnp.zeros((CONV_ROWS, CONV_DIM), F32)
            for t in range(CONV_WIDTH):
                b, a = _shift_of(CONV_WIDTH - 1 - t)
                dc = dc + taps[t] * dsh_ref[b, pl.ds(r + a, CONV_ROWS), :]
                b, a = _shift_of(HALO - (CONV_WIDTH - 1) + t)
                prod = d * csh_ref[b, pl.ds(r + a, CONV_ROWS), :]
                dw_acc[t] += jnp.sum(prod.reshape(CONV_ROWS // SUBLANES, SUBLANES, CONV_DIM), axis=0)
            cv = cv_ref[pl.ds(r, CONV_ROWS), :]
            sg = _sigmoid(cg_ref[pl.ds(r, CONV_ROWS), :])
            dcv_ref[pl.ds(r, CONV_ROWS), :] = (dc * sg).astype(BF16)
            dcg_ref[pl.ds(r, CONV_ROWS), :] = (dc * cv * sg * (1.0 - sg)).astype(BF16)
            return 0

        lax.fori_loop(0, TM // CONV_ROWS, rows, 0)

        @pl.when(i == n_tiles - 1)
        def _():
            dw_ref[...] = jnp.zeros_like(dw_ref)
            for t in range(CONV_WIDTH):
                dw_ref[t:t + 1, :] = jnp.sum(dw_acc[t], axis=0, keepdims=True)

    prev = lambda col: pl.BlockSpec((HALO, CHUNK), lambda i: (jnp.maximum(i * per - 1, 0), col))
    nxt = pl.BlockSpec((HALO, CONV_DIM), lambda i: (jnp.minimum((i + 1) * per, last), 0))
    half = jax.ShapeDtypeStruct((T, CHUNK), BF16)
    return _call(
        body, name=name, grid=(T // TM,),
        in_specs=[_rows(TM, CONV_DIM), nxt, _rows(TM, CHUNK, 1), _rows(TM, CHUNK, 2), prev(1), prev(2),
                  _whole((CONV_WIDTH, CONV_DIM))],
        out_specs=[_rows(TM, CHUNK), _rows(TM, CHUNK), _whole((HALO, CONV_DIM))],
        out_shape=[half, half, jax.ShapeDtypeStruct((HALO, CONV_DIM), F32)],
        scratch_shapes=[pltpu.VMEM((TM + HALO + SUBLANES, CONV_DIM), F32),
                        pltpu.VMEM((TM + HALO + SUBLANES, CONV_DIM), F32),
                        pltpu.VMEM((SUBLANES, TM + HALO, CONV_DIM), F32),
                        pltpu.VMEM((SUBLANES, TM + HALO, CONV_DIM), F32),
                        pltpu.VMEM((HALO, SUBLANES, CONV_DIM), F32)],
        args=(dconv, dconv, ug, ug, ug, ug, dw_w,), ride=ride)


def _attn_bwd(qs, k, v, do, cs, tri, tri_t, name, ride=None):
    T = qs.shape[0]
    nq = T // BLK
    width = LANES * ATT_COLS
    chains = [(c, half) for c in range(ATT_COLS) for half in range(2)]

    def body(q_ref, k_ref, v_ref, do_ref, cs_ref, m_ref, mt_ref, dq_ref, dk_ref, dv_ref, dk_acc, dv_acc):
        qi = pl.program_id(1)

        @pl.when(qi == 0)
        def _():
            dk_acc[...] = jnp.zeros_like(dk_acc)
            dv_acc[...] = jnp.zeros_like(dv_acc)

        lane = lax.broadcasted_iota(jnp.int32, (BLK, LANES), 1)
        first = lane < HEAD_DIM
        causal = (lax.broadcasted_iota(jnp.int32, (BLK, BLK), 1)
                  < lax.broadcasted_iota(jnp.int32, (BLK, BLK), 0))
        tri_m = m_ref[...]
        tri_mt = mt_ref[...]

        def halves(x):
            zero = jnp.zeros_like(x)
            return jnp.where(first, x, zero), jnp.where(first, zero, x)

        qh, doh, cs = {}, {}, []
        for c in range(ATT_COLS):
            qh[c, 0], qh[c, 1] = halves(q_ref[:, c * LANES:(c + 1) * LANES])
            doh[c, 0], doh[c, 1] = halves(do_ref[:, c * LANES:(c + 1) * LANES])
            cs.append(cs_ref[:, c * LANES:(c + 1) * LANES])

        def step(kb, state, masked):
            prefixes, dq_accs = state
            start = pl.multiple_of(kb * BLK, BLK)
            kblk = [k_ref[pl.ds(start, BLK), c * LANES:(c + 1) * LANES] for c in range(ATT_COLS)]
            vblk = [v_ref[pl.ds(start, BLK), c * LANES:(c + 1) * LANES] for c in range(ATT_COLS)]
            z = [_nt(qh[ch], kblk[ch[0]]) for ch in chains]
            da = [_nt(doh[ch], vblk[ch[0]]) for ch in chains]
            parts = [_softplus_parts(zi) for zi in z]
            sp = [pt[1] for pt in parts]
            if masked:
                sp = [jnp.where(causal, s, 0.0) for s in sp]
            incl = [_dot_hilo(s, tri_m) for s in sp]
            carries = [jnp.sum(jnp.where(lane == kb + HEAD_DIM * half, cs[c], 0.0), axis=1, keepdims=True)
                       for c, half in chains]
            a = [jnp.exp(zi - ii - ci) for zi, ii, ci in zip(z, incl, carries)]
            if masked:
                a = [jnp.where(causal, ai, 0.0) for ai in a]
            w = [ai * di for ai, di in zip(a, da)]
            pinc = [_dot_hilo(wi, tri_mt) for wi in w]
            beta = [jnp.where(zi >= 0.0, 1.0, pt[0]) / (1.0 + pt[0]) for zi, pt in zip(z, parts)]
            dz = [wi - bi * (pi + pre) for wi, bi, pi, pre in zip(w, beta, pinc, prefixes)]
            if masked:
                dz = [jnp.where(causal, d, 0.0) for d in dz]
            dq_accs = list(dq_accs)
            for c in range(ATT_COLS):
                k0, k1 = halves(kblk[c])
                dz0, dz1 = dz[2 * c].astype(BF16), dz[2 * c + 1].astype(BF16)
                a0, a1 = a[2 * c].astype(BF16), a[2 * c + 1].astype(BF16)
                dq_accs[c] = dq_accs[c] + _nn(dz0, k0) + _nn(dz1, k1)
                dk_acc[pl.ds(start, BLK), c * LANES:(c + 1) * LANES] += _tn(dz0, qh[c, 0]) + _tn(dz1, qh[c, 1])
                dv_acc[pl.ds(start, BLK), c * LANES:(c + 1) * LANES] += _tn(a0, doh[c, 0]) + _tn(a1, doh[c, 1])
            prefixes = tuple(pre + pi[:, BLK - 1:BLK] for pre, pi in zip(prefixes, pinc))
            return prefixes, tuple(dq_accs)

        state = (tuple(jnp.zeros((BLK, 1), F32) for _ in chains),
                 tuple(jnp.zeros((BLK, LANES), F32) for _ in range(ATT_COLS)))
        first_block = jnp.max(jnp.where(lane == FIRST_BLOCK_LANE, cs[0], 0.0)).astype(jnp.int32)
        state = lax.fori_loop(first_block, qi, lambda kb, st: step(kb, st, False), state)
        state = step(qi, state, True)
        for c in range(ATT_COLS):
            dq_ref[:, c * LANES:(c + 1) * LANES] = (state[1][c] * (HEAD_DIM ** -0.5)).astype(BF16)

        @pl.when(qi == nq - 1)
        def _():
            dk_ref[...] = dk_acc[...].astype(BF16)
            dv_ref[...] = dv_acc[...].astype(BF16)

    blk = pl.BlockSpec((BLK, width), lambda j, i: (i, j))
    col = pl.BlockSpec((T, width), lambda j, i: (0, j))
    out = jax.ShapeDtypeStruct((T, ATTN_DIM), BF16)
    return _call(
        body, name=name, grid=(ATTN_DIM // width, nq),
        in_specs=[blk, col, col, blk, blk, _whole((BLK, BLK)), _whole((BLK, BLK))],
        out_specs=[blk, col, col], out_shape=[out, out, out],
        scratch_shapes=[pltpu.VMEM((T, width), F32), pltpu.VMEM((T, width), F32)],
        args=(qs, k, v, do, cs, tri, tri_t,), ride=ride)


def _inproj_bwd(du, w_in_t, h, dh1, gain, name):
    T = h.shape[0]

    def body(*refs):
        du_refs = refs[:N_CHUNK]
        w_ref, h_ref, dh1_ref, g_ref, dh_ref, gsum_ref = refs[N_CHUNK:]

        @pl.when(pl.program_id(0) == 0)
        def _():
            gsum_ref[...] = jnp.zeros_like(gsum_ref)

        dhn = jnp.zeros((TM, D_MODEL), F32)
        for j in range(N_CHUNK):
            dhn = dhn + _nn(du_refs[j][...], w_ref[j * CHUNK:(j + 1) * CHUNK, :])
        hv = h_ref[...]
        r = lax.rsqrt(jnp.mean(hv * hv, axis=-1, keepdims=True) + EPS)
        xh = hv * r
        gsum_ref[...] += jnp.sum(dhn * xh, axis=0, keepdims=True)
        dxh = dhn * g_ref[...]
        dh_ref[...] = dh1_ref[...] + r * (dxh - xh * jnp.mean(dxh * xh, axis=-1, keepdims=True))

    full = _rows(TM, D_MODEL)
    return pl.pallas_call(
        body, name=name, grid=(T // TM,),
        in_specs=[_rows(TM, CHUNK)] * N_CHUNK + [_whole((N_CHUNK * CHUNK, D_MODEL)), full, full,
                                                 _whole((1, D_MODEL))],
        out_specs=[full, _whole((1, D_MODEL))],
        out_shape=[jax.ShapeDtypeStruct((T, D_MODEL), F32), jax.ShapeDtypeStruct((1, D_MODEL), F32)],
        compiler_params=_params(1),
    )(*du, w_in_t, h, dh1, gain)


def _weight_grad(lhs_list, rhs, name, tk=CHUNK):
    T, n_rhs = rhs.shape
    n = len(lhs_list)
    ka = lhs_list[0].shape[1]
    per = ka // tk

    def body(*refs):
        a_refs, b_ref, out_ref = refs[:n], refs[n], refs[n + 1]
        step = pl.program_id(0)
        for j in range(n):
            for s in range(per):
                @pl.when(step == j * per + s)
                def _(j=j, s=s):
                    out_ref[...] = _tn(a_refs[j][:, s * tk:(s + 1) * tk], b_ref[...]).astype(BF16)

    return pl.pallas_call(
        body, name=name, grid=(n * per,),
        in_specs=[_whole((T, ka))] * n + [_whole((T, n_rhs))],
        out_specs=pl.BlockSpec((tk, n_rhs), lambda i: (i, 0)),
        out_shape=jax.ShapeDtypeStruct((n * ka, n_rhs), BF16),
        compiler_params=_params(1),
    )(*lhs_list, rhs)


def _sum_slots(slots, name):
    n = len(slots)

    def body(*refs):
        for src, dst in zip(refs[:n], refs[n:]):
            acc = src[0].astype(F32)
            for s in range(1, N_DEV):
                acc = acc + src[s].astype(F32)
            dst[...] = acc

    return pl.pallas_call(
        body, name=name,
        out_shape=[jax.ShapeDtypeStruct(s.shape[1:], F32) for s in slots],
        compiler_params=_params(),
    )(*slots)


def _adamw(w, g, m, v, name):
    R, C = w.shape
    tr = R
    for cand in (512, 256, 128, 64):
        if R % cand == 0 and R > cand:
            tr = cand
            break

    def body(w_ref, g_ref, m_ref, v_ref, d_ref, nm_ref, nv_ref):
        gv = g_ref[...]
        nm = ADAM_B1 * m_ref[...] + (1.0 - ADAM_B1) * gv
        nv = ADAM_B2 * v_ref[...] + (1.0 - ADAM_B2) * (gv * gv)
        m_hat = nm / (1.0 - ADAM_B1 ** ADAM_STEP)
        v_hat = nv / (1.0 - ADAM_B2 ** ADAM_STEP)
        d_ref[...] = -ADAM_LR * (m_hat / (jnp.sqrt(v_hat) + ADAM_EPS) + ADAM_WD * w_ref[...])
        nm_ref[...] = nm
        nv_ref[...] = nv

    spec = pl.BlockSpec((tr, C), lambda i: (i, 0))
    out = jax.ShapeDtypeStruct((R, C), F32)
    return pl.pallas_call(
        body, name=name, grid=(R // tr,),
        in_specs=[spec] * 4, out_specs=[spec] * 3, out_shape=[out, out, out],
        compiler_params=_params(1),
    )(w, g, m, v)


def _pack_small(norm_g, ple_norm_g, final_g, dw_b, conv_ln_g, conv_ln_b, conv_out_g, attn_out_g, scalar=None):
    flat = lambda a: a.reshape(1, -1)
    pad = lambda a: jnp.pad(a, ((0, 0), (0, D_MODEL - a.shape[1])))
    last = jnp.zeros((1, D_MODEL), F32) if scalar is None else pad(scalar.reshape(1, 1))
    rows = [norm_g, ple_norm_g, flat(final_g), flat(dw_b), flat(conv_ln_g), flat(conv_ln_b),
            flat(conv_out_g), pad(flat(attn_out_g)), last]
    used = sum(r.shape[0] for r in rows)
    return jnp.concatenate(rows + [jnp.zeros((SMALL_ROWS - used, D_MODEL), F32)], axis=0)


def _unpack_small(a):
    two = lambda r: a[r].reshape(2, -1)
    return dict(norm_g=a[0:2], ple_norm_g=a[2:4], final_g=a[4], dw_b=two(5), conv_ln_g=two(6),
                conv_ln_b=two(7), conv_out_g=two(8), attn_out_g=a[9, :2 * HEAD_DIM].reshape(2, HEAD_DIM))


def kernel(x, p, norm_g, w_in, attn_out_g, dw_w, dw_b, conv_ln_g, conv_ln_b, w_pw, conv_out_g, w_out, ple_norm_g, w_ple_gate, w_ple, final_g, loss_target, m_norm_g, m_w_in, m_attn_out_g, m_dw_w, m_dw_b, m_conv_ln_g, m_conv_ln_b, m_w_pw, m_conv_out_g, m_w_out, m_ple_norm_g, m_w_ple_gate, m_w_ple, m_final_g, v_norm_g, v_w_in, v_attn_out_g, v_dw_w, v_dw_b, v_conv_ln_g, v_conv_ln_b, v_w_pw, v_conv_out_g, v_w_out, v_ple_norm_g, v_w_ple_gate, v_w_ple, v_final_g):
    depth = w_in.shape[0]
    T = x.shape[1]
    my_idx = 4 * lax.axis_index("x") + 2 * lax.axis_index("y") + lax.axis_index("c")

    ids = jnp.arange(BLK)
    tri = (ids[:, None] >= ids[None, :]).astype(BF16)
    tri_t = (ids[:, None] <= ids[None, :]).astype(BF16)
    hid = jnp.arange(ATTN_DIM) // HEAD_DIM
    head_mean = ((hid[:, None] == hid[None, :]).astype(F32) / HEAD_DIM).astype(BF16)

    w_names = ("w_in_t", "w_pw", "w_out", "w_gate", "w_ple")
    w_axes = dict(zip(w_names, (0, 0, 0, 0, 1)))
    shards = [dict(zip(w_names, (w_in[l].T.astype(BF16), w_pw[l].astype(BF16), w_out[l].astype(BF16),
                                 w_ple_gate[l].astype(BF16), w_ple[l].astype(BF16)))) for l in range(depth)]
    first = _all_gather([shards[0][n] for n in w_names] + [dw_w[l].T for l in range(depth)],
                        [w_axes[n] for n in w_names] + [0] * depth, "gather_weights_0")
    layers = []
    for l in range(depth):
        layers.append(dict(
            dw_w=first[len(w_names) + l].T,
            g_norm=norm_g[l][None], g_attn=jnp.tile(attn_out_g[l], N_HEADS)[None], dw_b=dw_b[l][None],
            ln_g=conv_ln_g[l][None], ln_b=conv_ln_b[l][None], g_conv=conv_out_g[l][None],
            g_ple=ple_norm_g[l][None], p=p[l, 0]))
    layers[0].update(zip(w_names, first))

    def gathered(l, names):
        return [_Ride.gather(shards[l][n], w_axes[n]) for n in names] if l > 0 else []

    h = x[0]
    saved = []
    for l, w in enumerate(layers):
        nxt = [None] if l + 1 < depth else []

        def quarter(i, l=l, nxt=nxt):
            return [_Ride.gather(shards[l + 1]["w_in_t"], 0, nxt[0], *W_IN_QUARTERS[i])] if nxt else []

        (qs, k, v, ug, hn), landed = _prenorm_inproj(h, w["g_norm"], w["w_in_t"], f"inproj_{l}", _Ride(quarter(0)))
        nxt[:1] = landed[:1]
        (o, cs), landed = _attn_fwd(qs, k, v, tri, f"attn_fwd_{l}",
                                    _Ride(quarter(1) + gathered(l, ("w_out", "w_ple"))))
        nxt[:1] = landed[:len(landed) - 2 * (l > 0)]
        w.update(zip(("w_out", "w_ple"), landed[len(landed) - 2:] if l > 0 else ()))
        (conv, c2), landed = _conv_fwd(ug, w["dw_w"], w["dw_b"], w["ln_g"], w["ln_b"], f"conv_fwd_{l}",
                                       _Ride(quarter(2) + gathered(l, ("w_gate", "w_pw"))))
        nxt[:1] = landed[:len(landed) - 2 * (l > 0)]
        w.update(zip(("w_gate", "w_pw"), landed[len(landed) - 2:] if l > 0 else ()))
        (h2, h1, ycat, hn2, gate, e, c3), landed = _mix_out_ple(
            o, ug, c2, h, w["p"], head_mean, w["g_attn"], w["g_conv"], w["g_ple"],
            w["w_pw"], w["w_out"], w["w_gate"], w["w_ple"], f"mix_{l}", _Ride(quarter(3)))
        if l + 1 < depth:
            layers[l + 1]["w_in_t"] = landed[0]
        saved.append(dict(h=h, qs=qs, k=k, v=v, ug=ug, hn=hn, o=o, cs=cs, conv=conv, c2=c2, h1=h1,
                          ycat=ycat, hn2=hn2, gate=gate, e=e, c3=c3))
        h = h2
    dh, g_final, loss_part = _final_loss(h, loss_target[0], final_g[None], "final_loss")

    small = {}
    dww_parts = [None] * depth
    slots = [dict() for _ in range(depth)]
    g_w_in = None
    for l in reversed(range(depth)):
        w, s = layers[l], saved[l]
        above = [None] if g_w_in is not None else []

        def part(i, above=above, g=g_w_in):
            return [_Ride.scatter(g, 0, above[0], *W_IN_GRAD_PARTS[i])] if above else []

        def scattered(grads, names):
            return [_Ride.scatter(grads[n], w_axes[n]) for n in names]

        (dh1, dh1b, dzg, de, dycat, g_ple_sum), landed = _ple_out_bwd(
            dh, s["gate"], s["e"], s["h1"], w["g_ple"], w["w_gate"], w["w_out"], f"ple_bwd_{l}", _Ride(part(0)))
        above[:1] = landed
        (do, dga, dgc, dc3, dconv, sums), landed = _branch_bwd(
            dycat, s["o"], s["ug"], s["c3"], s["conv"], head_mean, w["g_attn"], w["g_conv"],
            w["ln_g"], w["ln_b"], w["w_pw"], f"branch_bwd_{l}", _Ride(part(1)))
        above[:1] = landed
        grads = dict(
            w_pw=_weight_grad([s["c2"]], dc3, f"grad_w_pw_{l}"),
            w_out=_weight_grad([s["ycat"]], dh1b, f"grad_w_out_{l}"),
            w_gate=_weight_grad([s["hn2"]], dzg, f"grad_w_gate_{l}"),
            w_ple=_weight_grad([w["p"].astype(BF16)], de, f"grad_w_ple_{l}", tk=PLE_DIM))
        on_conv, on_attn = (("w_out", "w_pw"), ("w_gate", "w_ple")) if l > 0 else ((), w_names[1:])
        (dcv, dcg, dww), landed = _conv_bwd(dconv, s["ug"], w["dw_w"], f"conv_bwd_{l}",
                                            _Ride(part(2) + scattered(grads, on_conv)))
        if above:
            slots[l + 1]["w_in_t"] = landed[0]
        slots[l].update(zip(on_conv, landed[len(landed) - len(on_conv):]))
        (dq, dk, dv), landed = _attn_bwd(s["qs"], s["k"], s["v"], do, s["cs"], tri, tri_t, f"attn_bwd_{l}",
                                         _Ride(scattered(grads, on_attn)))
        slots[l].update(zip(on_attn, landed))
        du = [dq, dk, dv, dga, dcv, dcg, dgc]
        dh, g_norm_sum = _inproj_bwd(du, w["w_in_t"], s["h"], dh1, w["g_norm"], f"inproj_bwd_{l}")
        g_w_in = _weight_grad(du, s["hn"], f"grad_w_in_{l}")
        small[l] = dict(norm_g=g_norm_sum, ple_norm_g=g_ple_sum, attn_out_g=sums[0].reshape(N_HEADS, HEAD_DIM).sum(0),
                        conv_out_g=sums[1], conv_ln_g=sums[2], conv_ln_b=sums[3], dw_b=sums[4])
        dww_parts[l] = dww[:CONV_WIDTH]
    slots[0]["w_in_t"] = _scatter_partials([g_w_in], [0], "scatter_w_in_0")[0]
    slots = [[sl[n] for n in w_names] for sl in slots]
    grad_x = dh[None]

    stack = lambda name: jnp.stack([small[l][name].reshape(-1) for l in range(depth)])
    small_part = _pack_small(stack("norm_g"), stack("ple_norm_g"), g_final[0], stack("dw_b"), stack("conv_ln_g"),
                             stack("conv_ln_b"), stack("conv_out_g"), stack("attn_out_g"), scalar=loss_part[0, 0])
    pack = jnp.concatenate([small_part, jnp.concatenate(dww_parts, axis=1),
                            jnp.zeros((PACK_ROWS - SMALL_ROWS - CONV_WIDTH, D_MODEL), F32)], axis=0)
    (all_packs,) = _all_gather([pack], [0], "gather_small_grads")
    (pack_sum,) = _sum_slots([all_packs.reshape(N_DEV, PACK_ROWS, D_MODEL)], "sum_small_grads")
    loss = pack_sum[SMALL_ROWS - 8 + 2, 0]
    g_small = pack_sum[:SMALL_ROWS].at[SMALL_ROWS - 8 + 2, 0].set(0.0)
    dww_full = pack_sum[SMALL_ROWS:SMALL_ROWS + CONV_WIDTH].reshape(CONV_WIDTH, depth, CONV_DIM).transpose(1, 0, 2)
    g_dw_w = lax.dynamic_slice_in_dim(dww_full, my_idx * (CONV_DIM // N_DEV), CONV_DIM // N_DEV, axis=2)

    w_small = _pack_small(norm_g, ple_norm_g, final_g, dw_b, conv_ln_g, conv_ln_b, conv_out_g, attn_out_g)
    m_small = _pack_small(m_norm_g, m_ple_norm_g, m_final_g, m_dw_b, m_conv_ln_g, m_conv_ln_b, m_conv_out_g, m_attn_out_g)
    v_small = _pack_small(v_norm_g, v_ple_norm_g, v_final_g, v_dw_b, v_conv_ln_g, v_conv_ln_b, v_conv_out_g, v_attn_out_g)
    d_small, nm_small, nv_small = _adamw(w_small, g_small, m_small, v_small, "adamw_small")
    res = {"g": _unpack_small(g_small), "d": _unpack_small(d_small), "m": _unpack_small(nm_small),
           "v": _unpack_small(nv_small)}

    sums = [_sum_slots(slots[l], f"sum_grads_{l}") for l in range(depth)]
    big = {
        "w_in": jnp.stack([sums[l][0].T for l in range(depth)]),
        "w_pw": jnp.stack([sums[l][1] for l in range(depth)]),
        "w_out": jnp.stack([sums[l][2] for l in range(depth)]),
        "w_ple_gate": jnp.stack([sums[l][3] for l in range(depth)]),
        "w_ple": jnp.stack([sums[l][4] for l in range(depth)]),
        "dw_w": g_dw_w,
    }
    state = {"w_in": (w_in, m_w_in, v_w_in), "w_pw": (w_pw, m_w_pw, v_w_pw), "w_out": (w_out, m_w_out, v_w_out),
             "w_ple_gate": (w_ple_gate, m_w_ple_gate, v_w_ple_gate), "w_ple": (w_ple, m_w_ple, v_w_ple),
             "dw_w": (dw_w, m_dw_w, v_dw_w)}
    for name, g in big.items():
        wv, mv, vv = state[name]
        flat = lambda a: a.reshape(-1, a.shape[-1])
        d, nm, nv = _adamw(flat(wv), flat(g), flat(mv), flat(vv), f"adamw_{name}")
        res["g"][name] = g
        res["d"][name], res["m"][name], res["v"][name] = (d.reshape(wv.shape), nm.reshape(wv.shape),
                                                         nv.reshape(wv.shape))

    order = ["norm_g", "w_in", "attn_out_g", "dw_w", "dw_b", "conv_ln_g", "conv_ln_b", "w_pw", "conv_out_g",
             "w_out", "ple_norm_g", "w_ple_gate", "w_ple", "final_g"]
    return (loss, grad_x, *[res["g"][n] for n in order], *[res["d"][n] for n in order],
            *[res["m"][n] for n in order], *[res["v"][n] for n in order])
```

```python
import functools

import jax
import jax.numpy as jnp
from jax import lax
from jax.experimental import pallas as pl
from jax.experimental.pallas import tpu as pltpu

F32 = jnp.float32
BF16 = jnp.bfloat16
MESH = pl.DeviceIdType.MESH

N_DEV = 8
D_MODEL = 1024
ATTN_DIM = 512
CONV_DIM = 512
HEAD_DIM = 64
N_HEADS = 8
CONV_WIDTH = 31
PLE_DIM = 256
CHUNK = 512
N_CHUNK = 7
EPS = 1e-6
ADAM_LR = 0.001
ADAM_B1 = 0.9
ADAM_B2 = 0.999
ADAM_EPS = 1e-08
ADAM_WD = 0.01
ADAM_STEP = 10

LANES = 128
BLK = 256
ATT_COLS = 4
DEAD_AT = 110.0
FIRST_BLOCK_LANE = HEAD_DIM - 1
TM = 256
HALO = 32
SUBLANES = 8
CONV_ROWS = 32
VMEM_LIMIT = 56 * 1024 * 1024
SMALL_ROWS = 16
W_IN_QUARTERS = ((0, 96), (96, 144), (240, 96), (336, 112))
W_IN_GRAD_PARTS = ((0, 96), (96, 80), (176, 144), (320, 128))
PACK_ROWS = 48


def _nn(a, b):
    return lax.dot_general(a, b, (((1,), (0,)), ((), ())), preferred_element_type=F32)


def _nt(a, b):
    return lax.dot_general(a, b, (((1,), (1,)), ((), ())), preferred_element_type=F32)


def _tn(a, b):
    return lax.dot_general(a, b, (((0,), (0,)), ((), ())), preferred_element_type=F32)


def _split(x):
    hi = x.astype(BF16)
    lo = (x - hi.astype(F32)).astype(BF16)
    return hi, lo


def _dot_hilo(x, m):
    hi, lo = _split(x)
    return _nn(hi, m) + _nn(lo, m)


def _sigmoid(x):
    return jax.nn.sigmoid(x)


def _dsilu(x, s):
    return s * (1.0 + x * (1.0 - s))


def _params(n_grid=0, vmem=VMEM_LIMIT):
    sem = ("arbitrary",) * n_grid if n_grid else None
    return pltpu.CompilerParams(dimension_semantics=sem, vmem_limit_bytes=vmem)


def _rows(tm, cols, col=0):
    return pl.BlockSpec((tm, cols), lambda i: (i, col))


def _whole(shape):
    zeros = (0,) * len(shape)
    return pl.BlockSpec(shape, lambda *_: zeros)


def _my_position():
    return lax.axis_index("x"), lax.axis_index("y"), lax.axis_index("c")


def _block(ref, axis, idx, size):
    start = pl.multiple_of(idx * size, size)
    if axis == 0:
        return ref.at[pl.ds(start, size), :]
    return ref.at[:, pl.ds(start, size)]


def _all_gather(shards, axes, name):
    n = len(shards)
    sizes = [s.shape[a] for s, a in zip(shards, axes)]

    def full_shape(s, a):
        shape = list(s.shape)
        shape[a] *= N_DEV
        return jax.ShapeDtypeStruct(tuple(shape), s.dtype)

    def body(*refs):
        ins, outs = refs[:n], refs[n:2 * n]
        send_sems, recv_sems, local_sems = refs[2 * n:]
        x, y, c = _my_position()
        me, sibling = (x, y, c), (x, y, 1 - c)
        chips = [(1 - x, y), (x, 1 - y), (1 - x, 1 - y)]

        def place(i, dev):
            return _block(outs[i], axes[i], 4 * dev[0] + 2 * dev[1] + dev[2], sizes[i])

        def copy(k, i, dev, to, src=None):
            return pltpu.make_async_remote_copy(
                src_ref=place(i, dev) if src is None else src, dst_ref=place(i, dev),
                send_sem=send_sems.at[k, i], recv_sem=recv_sems.at[k, i],
                device_id=to, device_id_type=MESH)

        mine = [pltpu.make_async_copy(ins[i], place(i, me), local_sems.at[i]) for i in range(n)]
        for cp in mine:
            cp.start()
        first = [copy(0, i, me, sibling, src=ins[i]) for i in range(n)]
        for j, chip in enumerate(chips):
            first += [copy(1 + j, i, me, (*chip, c), src=ins[i]) for i in range(n)]
        for cp in first:
            cp.start()
        passed = []
        for j, chip in enumerate(chips):
            for i in range(n):
                copy(1 + j, i, (*chip, c), me).wait_recv()
            hop = [copy(4 + j, i, (*chip, c), sibling) for i in range(n)]
            for cp in hop:
                cp.start()
            passed += hop
        for i in range(n):
            copy(0, i, sibling, me).wait_recv()
        for j, chip in enumerate(chips):
            for i in range(n):
                copy(4 + j, i, (*chip, 1 - c), me).wait_recv()
        for cp in first + passed:
            cp.wait_send()
        for cp in mine:
            cp.wait()

    any_spec = pl.BlockSpec(memory_space=pl.ANY)
    return pl.pallas_call(
        body, name=name,
        out_shape=[full_shape(s, a) for s, a in zip(shards, axes)],
        in_specs=[any_spec] * n, out_specs=[any_spec] * n,
        scratch_shapes=[pltpu.SemaphoreType.DMA((7, n)), pltpu.SemaphoreType.DMA((7, n)),
                        pltpu.SemaphoreType.DMA((n,))],
    )(*shards)


class _Ride:
    def __init__(self, parts):
        self.parts = [p for p in parts if p is not None]

    @staticmethod
    def gather(src, axis, land=None, lo=0, n=None):
        return ("gather", src, land, axis, lo, src.shape[axis] if n is None else n)

    @staticmethod
    def scatter(src, axis, land=None, lo=0, n=None):
        return ("scatter", src, land, axis, lo, src.shape[axis] // N_DEV if n is None else n)

    def arrays(self):
        return [p[1] for p in self.parts] + [p[2] for p in self.parts if p[2] is not None]

    def out_shapes(self):
        out = []
        for kind, src, _, axis, _, _ in self.parts:
            shape = list(src.shape)
            if kind == "gather":
                shape[axis] *= N_DEV
            else:
                shape[axis] //= N_DEV
                shape = [N_DEV] + shape
            out.append(jax.ShapeDtypeStruct(tuple(shape), src.dtype))
        return out

    def aliases(self, n_in, n_out):
        m, out = len(self.parts), {}
        for j, p in enumerate(self.parts):
            if p[2] is not None:
                out[n_in + m + len(out)] = n_out + j
        return out

    def scratch(self):
        m = len(self.parts)
        return [pltpu.SemaphoreType.DMA((N_DEV - 1, m)), pltpu.SemaphoreType.DMA((N_DEV - 1, m)),
                pltpu.SemaphoreType.DMA((m,))]

    def _copies(self, src_refs, land_refs, sems):
        send_sems, recv_sems, local_sems = sems
        x, y, c = _my_position()
        my_idx = 4 * x + 2 * y + c
        own, sends, lands = [], [], []
        for j, (kind, src, _, axis, lo, n) in enumerate(self.parts):
            size = src.shape[axis] if kind == "gather" else src.shape[axis] // N_DEV
            align = 16 if axis == 0 else LANES

            def rows(ref, idx, lead=None, axis=axis, lo=lo, n=n, size=size, align=align):
                at = pl.ds(pl.multiple_of(idx * size + lo, align), n)
                where = (at, slice(None)) if axis == 0 else (slice(None), at)
                return ref.at[where] if lead is None else ref.at[(lead, *where)]

            def in_shard(ref):
                return rows(ref, 0)

            def in_slot(ref, s):
                return rows(ref, 0, lead=s)

            for k in range(N_DEV):
                px = 1 - x if k & 4 else x
                py = 1 - y if k & 2 else y
                pc = 1 - c if k & 1 else c
                peer_idx = 4 * px + 2 * py + pc
                if kind == "gather":
                    a, b, landed = in_shard(src_refs[j]), rows(land_refs[j], my_idx), rows(land_refs[j], peer_idx)
                else:
                    a, b, landed = rows(src_refs[j], peer_idx), in_slot(land_refs[j], my_idx), in_slot(land_refs[j], peer_idx)
                if k == 0:
                    own.append(pltpu.make_async_copy(a, b, local_sems.at[j]))
                    continue
                mk = lambda dst, a=a, k=k, j=j, to=(px, py, pc): pltpu.make_async_remote_copy(
                    src_ref=a, dst_ref=dst, send_sem=send_sems.at[k - 1, j], recv_sem=recv_sems.at[k - 1, j],
                    device_id=to, device_id_type=MESH)
                sends.append(mk(b))
                lands.append(mk(landed))
        return own, sends, lands

    def start(self, src_refs, land_refs, sems):
        own, sends, _ = self._copies(src_refs, land_refs, sems)
        for cp in own + sends:
            cp.start()

    def wait(self, src_refs, land_refs, sems):
        own, sends, lands = self._copies(src_refs, land_refs, sems)
        for cp in lands:
            cp.wait_recv()
        for cp in sends:
            cp.wait_send()
        for cp in own:
            cp.wait()


def _call(body, *, name, grid, in_specs, out_specs, out_shape, args, scratch_shapes=(), ride=None):
    in_specs, out_specs, out_shape = list(in_specs), list(out_specs), list(out_shape)
    n_in, n_out, n_sc = len(in_specs), len(out_specs), len(scratch_shapes)
    if ride is None or not ride.parts:
        res = pl.pallas_call(body, name=name, grid=grid, in_specs=in_specs, out_specs=out_specs,
                             out_shape=out_shape, scratch_shapes=list(scratch_shapes),
                             compiler_params=_params(len(grid)))(*args)
        return list(res), []
    extra, m = ride.arrays(), len(ride.parts)

    def riding(*refs):
        a = n_in + len(extra)
        b = a + n_out
        srcs, lands, sems = refs[n_in:n_in + m], refs[b:b + m], refs[b + m + n_sc:]
        at = [pl.program_id(d) for d in range(len(grid))]

        @pl.when(functools.reduce(jnp.logical_and, [i == 0 for i in at]))
        def _():
            ride.start(srcs, lands, sems)

        body(*refs[:n_in], *refs[a:b], *refs[b + m:b + m + n_sc])

        @pl.when(functools.reduce(jnp.logical_and, [i == g - 1 for i, g in zip(at, grid)]))
        def _():
            ride.wait(srcs, lands, sems)

    hbm = pl.BlockSpec(memory_space=pl.ANY)
    res = pl.pallas_call(
        riding, name=name, grid=grid, in_specs=in_specs + [hbm] * len(extra), out_specs=out_specs + [hbm] * m,
        out_shape=out_shape + ride.out_shapes(), scratch_shapes=list(scratch_shapes) + ride.scratch(),
        input_output_aliases=ride.aliases(n_in, n_out), compiler_params=_params(len(grid)),
    )(*args, *extra)
    return list(res[:n_out]), list(res[n_out:])


def _prenorm_inproj(h, gain, w_in_t, name, ride=None):
    T = h.shape[0]

    def body(h_ref, g_ref, w_ref, q_ref, k_ref, v_ref, ug_ref, hn_ref):
        hv = h_ref[...]
        r = lax.rsqrt(jnp.mean(hv * hv, axis=-1, keepdims=True) + EPS)
        hn = (hv * r * g_ref[...]).astype(BF16)
        hn_ref[...] = hn
        for j in range(N_CHUNK):
            u = _nt(hn, w_ref[j * CHUNK:(j + 1) * CHUNK, :])
            if j == 0:
                q_ref[...] = (u * (HEAD_DIM ** -0.5)).astype(BF16)
            elif j == 1:
                k_ref[...] = u.astype(BF16)
            elif j == 2:
                v_ref[...] = u.astype(BF16)
            else:
                ug_ref[:, (j - 3) * CHUNK:(j - 2) * CHUNK] = u

    act = jax.ShapeDtypeStruct((T, CHUNK), BF16)
    return _call(
        body, name=name, grid=(T // TM,),
        in_specs=[_rows(TM, D_MODEL), _whole((1, D_MODEL)), _whole((N_CHUNK * CHUNK, D_MODEL))],
        out_specs=[_rows(TM, CHUNK)] * 3 + [_rows(TM, 4 * CHUNK), _rows(TM, D_MODEL)],
        out_shape=[act, act, act, jax.ShapeDtypeStruct((T, 4 * CHUNK), F32),
                   jax.ShapeDtypeStruct((T, D_MODEL), BF16)],
        args=(h, gain, w_in_t,), ride=ride)


def _softplus_parts(z):
    e = jnp.exp(-jnp.abs(z))
    return e, jnp.maximum(z, 0.0) + jnp.log(1.0 + e)


def _attn_fwd(qs, k, v, tri, name, ride=None):
    T = qs.shape[0]
    assert T // BLK <= FIRST_BLOCK_LANE, "one lane per key block below the lane of the first block"
    width = LANES * ATT_COLS
    chains = [(c, half) for c in range(ATT_COLS) for half in range(2)]

    def body(q_ref, k_ref, v_ref, m_ref, o_ref, cs_ref):
        qi = pl.program_id(1)
        lane = lax.broadcasted_iota(jnp.int32, (BLK, LANES), 1)
        first = lane < HEAD_DIM
        causal = (lax.broadcasted_iota(jnp.int32, (BLK, BLK), 1)
                  < lax.broadcasted_iota(jnp.int32, (BLK, BLK), 0))
        tri_m = m_ref[...]
        qh = {}
        for c in range(ATT_COLS):
            q = q_ref[:, c * LANES:(c + 1) * LANES]
            zero = jnp.zeros_like(q)
            qh[c, 0], qh[c, 1] = jnp.where(first, q, zero), jnp.where(first, zero, q)

        def step(kb, state, masked):
            carries, accs, cvals = state
            start = pl.multiple_of(kb * BLK, BLK)
            kblk = [k_ref[pl.ds(start, BLK), c * LANES:(c + 1) * LANES] for c in range(ATT_COLS)]
            vblk = [v_ref[pl.ds(start, BLK), c * LANES:(c + 1) * LANES] for c in range(ATT_COLS)]
            z = [_nt(qh[ch], kblk[ch[0]]) for ch in chains]
            sp = [_softplus_parts(zi)[1] for zi in z]
            if masked:
                sp = [jnp.where(causal, s, 0.0) for s in sp]
            incl = [_dot_hilo(s, tri_m) for s in sp]
            a = [jnp.exp(zi - ii - ci) for zi, ii, ci in zip(z, incl, carries)]
            if masked:
                a = [jnp.where(causal, ai, 0.0) for ai in a]
            accs, cvals = list(accs), list(cvals)
            for n, (c, half) in enumerate(chains):
                zero = jnp.zeros_like(vblk[c])
                vh = jnp.where(first, vblk[c], zero) if half == 0 else jnp.where(first, zero, vblk[c])
                accs[c] = accs[c] + _nn(a[n].astype(BF16), vh)
                cvals[c] = jnp.where(lane == kb + HEAD_DIM * half, carries[n], cvals[c])
            carries = tuple(ci + ii[:, 0:1] for ci, ii in zip(carries, incl))
            return carries, tuple(accs), tuple(cvals)

        zeros = tuple(jnp.zeros((BLK, LANES), F32) for _ in range(ATT_COLS))
        state = (tuple(jnp.zeros((BLK, 1), F32) for _ in chains), zeros, zeros)
        state = step(qi, state, True)

        def reaches_further(st):
            it, (carries, _, _) = st
            least = functools.reduce(jnp.minimum, carries)
            return jnp.logical_and(it < qi, jnp.min(least) < DEAD_AT)

        done, state = lax.while_loop(reaches_further, lambda st: (st[0] + 1, step(qi - 1 - st[0], st[1], False)),
                                     (jnp.int32(0), state))
        first_block = (qi - done).astype(F32)
        for c in range(ATT_COLS):
            o_ref[:, c * LANES:(c + 1) * LANES] = state[1][c]
            cs_ref[:, c * LANES:(c + 1) * LANES] = jnp.where(lane == FIRST_BLOCK_LANE, first_block, state[2][c])

    blk = pl.BlockSpec((BLK, width), lambda j, i: (i, j))
    col = pl.BlockSpec((T, width), lambda j, i: (0, j))
    out = jax.ShapeDtypeStruct((T, ATTN_DIM), F32)
    return _call(
        body, name=name, grid=(ATTN_DIM // width, T // BLK),
        in_specs=[blk, col, col, _whole((BLK, BLK))],
        out_specs=[blk, blk], out_shape=[out, out],
        args=(qs, k, v, tri,), ride=ride)


def _shifted_copies(pad_ref, sh_ref):
    rows = sh_ref.shape[1]
    for b in range(SUBLANES):
        sh_ref[b] = pad_ref[b:b + rows, :]


def _shift_of(offset):
    return offset % SUBLANES, offset - offset % SUBLANES


def _conv_fwd(ug, dw_w, dw_b, ln_g, ln_b, name, ride=None):
    T = ug.shape[0]
    per = TM // HALO

    def body(cv_ref, cg_ref, cvh_ref, cgh_ref, w_ref, b_ref, g_ref, beta_ref, conv_ref, c2_ref, pad_ref, sh_ref):
        i = pl.program_id(0)
        halo = cvh_ref[...] * _sigmoid(cgh_ref[...])
        pad_ref[0:HALO, :] = jnp.where(i == 0, 0.0, halo)
        pad_ref[HALO:HALO + TM, :] = cv_ref[...] * _sigmoid(cg_ref[...])
        pad_ref[HALO + TM:, :] = jnp.zeros((SUBLANES, CONV_DIM), F32)
        _shifted_copies(pad_ref, sh_ref)
        taps = [w_ref[t:t + 1, :] for t in range(CONV_WIDTH)]

        def rows(j, _):
            r = pl.multiple_of(j * CONV_ROWS, CONV_ROWS)
            acc = jnp.zeros((CONV_ROWS, CONV_DIM), F32) + b_ref[...]
            for t in range(CONV_WIDTH):
                b, a = _shift_of(HALO - (CONV_WIDTH - 1) + t)
                acc = acc + taps[t] * sh_ref[b, pl.ds(r + a, CONV_ROWS), :]
            conv_ref[pl.ds(r, CONV_ROWS), :] = acc
            return 0

        lax.fori_loop(0, TM // CONV_ROWS, rows, 0)
        acc = conv_ref[...]
        mu = jnp.mean(acc, axis=-1, keepdims=True)
        xc = acc - mu
        rs = lax.rsqrt(jnp.mean(xc * xc, axis=-1, keepdims=True) + EPS)
        ln = xc * rs * g_ref[...] + beta_ref[...]
        c2_ref[...] = (ln * _sigmoid(ln)).astype(BF16)

    prev = lambda col: pl.BlockSpec((HALO, CHUNK), lambda i: (jnp.maximum(i * per - 1, 0), col))
    vec = _whole((1, CONV_DIM))
    return _call(
        body, name=name, grid=(T // TM,),
        in_specs=[_rows(TM, CHUNK, 1), _rows(TM, CHUNK, 2), prev(1), prev(2),
                  _whole((CONV_WIDTH, CONV_DIM)), vec, vec, vec],
        out_specs=[_rows(TM, CONV_DIM), _rows(TM, CONV_DIM)],
        out_shape=[jax.ShapeDtypeStruct((T, CONV_DIM), F32), jax.ShapeDtypeStruct((T, CONV_DIM), BF16)],
        scratch_shapes=[pltpu.VMEM((TM + HALO + SUBLANES, CONV_DIM), F32),
                        pltpu.VMEM((SUBLANES, TM + HALO, CONV_DIM), F32)],
        args=(ug, ug, ug, ug, dw_w, dw_b, ln_g, ln_b,), ride=ride)


def _mix_out_ple(o, ug, c2, h, p, head_mean, g_attn, g_conv, g_ple, w_pw, w_out, w_gate, w_ple, name, ride=None):
    T = h.shape[0]

    def body(o_ref, ga_ref, gc_ref, c2_ref, h_ref, p_ref, hm_ref, gao_ref, gco_ref, gpn_ref,
             wpw_ref, wout_ref, wg_ref, wple_ref,
             h2_ref, h1_ref, ycat_ref, hn2_ref, gate_ref, e_ref, c3_ref):
        ov = o_ref[...]
        rh = lax.rsqrt(_dot_hilo(ov * ov, hm_ref[...]) + EPS)
        ga = ga_ref[...]
        ya = (ov * rh * gao_ref[...] * (ga * _sigmoid(ga))).astype(BF16)
        c3 = _nn(c2_ref[...], wpw_ref[...])
        c3_ref[...] = c3
        rc = lax.rsqrt(jnp.mean(c3 * c3, axis=-1, keepdims=True) + EPS)
        gc = gc_ref[...]
        yc = (c3 * rc * gco_ref[...] * (gc * _sigmoid(gc))).astype(BF16)
        ycat_ref[:, :ATTN_DIM] = ya
        ycat_ref[:, ATTN_DIM:] = yc
        h1 = h_ref[...] + _nn(ya, wout_ref[:ATTN_DIM, :]) + _nn(yc, wout_ref[ATTN_DIM:, :])
        h1_ref[...] = h1
        r1 = lax.rsqrt(jnp.mean(h1 * h1, axis=-1, keepdims=True) + EPS)
        hn2 = (h1 * r1 * gpn_ref[...]).astype(BF16)
        hn2_ref[...] = hn2
        gate = _sigmoid(_nn(hn2, wg_ref[...]))
        e = _nn(p_ref[...].astype(BF16), wple_ref[...])
        gate_ref[...] = gate
        e_ref[...] = e
        h2_ref[...] = h1 + e * gate

    f32 = lambda cols: jax.ShapeDtypeStruct((T, cols), F32)
    bf = lambda cols: jax.ShapeDtypeStruct((T, cols), BF16)
    return _call(
        body, name=name, grid=(T // TM,),
        in_specs=[_rows(TM, ATTN_DIM), _rows(TM, CHUNK, 0), _rows(TM, CHUNK, 3), _rows(TM, CONV_DIM),
                  _rows(TM, D_MODEL), _rows(TM, PLE_DIM), _whole((ATTN_DIM, ATTN_DIM)),
                  _whole((1, ATTN_DIM)), _whole((1, CONV_DIM)), _whole((1, D_MODEL)),
                  _whole((CONV_DIM, CONV_DIM)), _whole((D_MODEL, D_MODEL)), _whole((D_MODEL, D_MODEL)),
                  _whole((PLE_DIM, D_MODEL))],
        out_specs=[_rows(TM, D_MODEL), _rows(TM, D_MODEL), _rows(TM, D_MODEL), _rows(TM, D_MODEL),
                   _rows(TM, D_MODEL), _rows(TM, D_MODEL), _rows(TM, CONV_DIM)],
        out_shape=[f32(D_MODEL), f32(D_MODEL), bf(D_MODEL), bf(D_MODEL), f32(D_MODEL), f32(D_MODEL),
                   f32(CONV_DIM)],
        args=(o, ug, ug, c2, h, p, head_mean, g_attn, g_conv, g_ple, w_pw, w_out, w_gate, w_ple,), ride=ride)


def _final_loss(h, target, gain, name):
    T = h.shape[0]

    def body(h_ref, t_ref, g_ref, dh_ref, gsum_ref, loss_ref):
        @pl.when(pl.program_id(0) == 0)
        def _():
            gsum_ref[...] = jnp.zeros_like(gsum_ref)
            loss_ref[...] = jnp.zeros_like(loss_ref)

        hv = h_ref[...]
        r = lax.rsqrt(jnp.mean(hv * hv, axis=-1, keepdims=True) + EPS)
        xh = hv * r
        diff = xh * g_ref[...] - t_ref[...]
        loss_ref[...] += 0.5 * jnp.sum(jnp.mean(diff * diff, axis=-1, keepdims=True), axis=0, keepdims=True)
        dy = diff * (1.0 / D_MODEL)
        gsum_ref[...] += jnp.sum(dy * xh, axis=0, keepdims=True)
        dxh = dy * g_ref[...]
        dh_ref[...] = r * (dxh - xh * jnp.mean(dxh * xh, axis=-1, keepdims=True))

    return pl.pallas_call(
        body, name=name, grid=(T // TM,),
        in_specs=[_rows(TM, D_MODEL), _rows(TM, D_MODEL), _whole((1, D_MODEL))],
        out_specs=[_rows(TM, D_MODEL), _whole((1, D_MODEL)), _whole((1, LANES))],
        out_shape=[jax.ShapeDtypeStruct((T, D_MODEL), F32), jax.ShapeDtypeStruct((1, D_MODEL), F32),
                   jax.ShapeDtypeStruct((1, LANES), F32)],
        compiler_params=_params(1),
    )(h, target, gain)


def _ple_out_bwd(dh2, gate, e, h1, g_ple, w_gate, w_out, name, ride=None):
    T = dh2.shape[0]

    def body(dh2_ref, gate_ref, e_ref, h1_ref, gpn_ref, wg_ref, wout_ref,
             dh1_ref, dh1b_ref, dzg_ref, de_ref, dycat_ref, gsum_ref):
        @pl.when(pl.program_id(0) == 0)
        def _():
            gsum_ref[...] = jnp.zeros_like(gsum_ref)

        dh2v = dh2_ref[...]
        gate = gate_ref[...]
        de_ref[...] = (dh2v * gate).astype(BF16)
        dzg = (dh2v * e_ref[...] * gate * (1.0 - gate)).astype(BF16)
        dzg_ref[...] = dzg
        dhn2 = _nt(dzg, wg_ref[...])
        h1 = h1_ref[...]
        r1 = lax.rsqrt(jnp.mean(h1 * h1, axis=-1, keepdims=True) + EPS)
        xh = h1 * r1
        gsum_ref[...] += jnp.sum(dhn2 * xh, axis=0, keepdims=True)
        dxh = dhn2 * gpn_ref[...]
        dh1 = dh2v + r1 * (dxh - xh * jnp.mean(dxh * xh, axis=-1, keepdims=True))
        dh1_ref[...] = dh1
        dh1b = dh1.astype(BF16)
        dh1b_ref[...] = dh1b
        dycat_ref[...] = _nt(dh1b, wout_ref[...])

    f32 = jax.ShapeDtypeStruct((T, D_MODEL), F32)
    bf = jax.ShapeDtypeStruct((T, D_MODEL), BF16)
    full = _rows(TM, D_MODEL)
    return _call(
        body, name=name, grid=(T // TM,),
        in_specs=[full, full, full, full, _whole((1, D_MODEL)), _whole((D_MODEL, D_MODEL)),
                  _whole((D_MODEL, D_MODEL))],
        out_specs=[full, full, full, full, full, _whole((1, D_MODEL))],
        out_shape=[f32, bf, bf, bf, f32, jax.ShapeDtypeStruct((1, D_MODEL), F32)],
        args=(dh2, gate, e, h1, g_ple, w_gate, w_out,), ride=ride)


def _branch_bwd(dycat, o, ug, c3, conv, head_mean, g_attn, g_conv, ln_g, ln_b, w_pw, name, ride=None):
    T = o.shape[0]

    def body(dya_ref, dyc_ref, o_ref, ga_ref, gc_ref, c3_ref, conv_ref, hm_ref, gao_ref, gco_ref,
             lng_ref, lnb_ref, wpw_ref,
             do_ref, dga_ref, dgc_ref, dc3_ref, dconv_ref, sums_ref):
        @pl.when(pl.program_id(0) == 0)
        def _():
            sums_ref[...] = jnp.zeros_like(sums_ref)

        hm = hm_ref[...]
        col = lambda x: jnp.sum(x, axis=0, keepdims=True)
        ov = o_ref[...]
        rh = lax.rsqrt(_dot_hilo(ov * ov, hm) + EPS)
        xh = ov * rh
        ga = ga_ref[...]
        sg = _sigmoid(ga)
        dya = dya_ref[...]
        don = dya * (ga * sg)
        dga_ref[...] = (dya * xh * gao_ref[...] * _dsilu(ga, sg)).astype(BF16)
        sums_ref[0:1, :] += col(don * xh)
        dxh = don * gao_ref[...]
        do_ref[...] = (rh * (dxh - xh * _dot_hilo(dxh * xh, hm))).astype(BF16)
        c3 = c3_ref[...]
        rc = lax.rsqrt(jnp.mean(c3 * c3, axis=-1, keepdims=True) + EPS)
        xh3 = c3 * rc
        gc = gc_ref[...]
        sgc = _sigmoid(gc)
        dyc = dyc_ref[...]
        dn3 = dyc * (gc * sgc)
        dgc_ref[...] = (dyc * xh3 * gco_ref[...] * _dsilu(gc, sgc)).astype(BF16)
        sums_ref[1:2, :] += col(dn3 * xh3)
        dxh3 = dn3 * gco_ref[...]
        dc3 = (rc * (dxh3 - xh3 * jnp.mean(dxh3 * xh3, axis=-1, keepdims=True))).astype(BF16)
        dc3_ref[...] = dc3
        dc2 = _nt(dc3, wpw_ref[...])
        cv = conv_ref[...]
        mu = jnp.mean(cv, axis=-1, keepdims=True)
        xc = cv - mu
        rs = lax.rsqrt(jnp.mean(xc * xc, axis=-1, keepdims=True) + EPS)
        xn = xc * rs
        ln = xn * lng_ref[...] + lnb_ref[...]
        dln = dc2 * _dsilu(ln, _sigmoid(ln))
        sums_ref[2:3, :] += col(dln * xn)
        sums_ref[3:4, :] += col(dln)
        dxn = dln * lng_ref[...]
        dconv = rs * (dxn - jnp.mean(dxn, axis=-1, keepdims=True)
                      - xn * jnp.mean(dxn * xn, axis=-1, keepdims=True))
        dconv_ref[...] = dconv
        sums_ref[4:5, :] += col(dconv)

    half = lambda dt: jax.ShapeDtypeStruct((T, CHUNK), dt)
    tile = _rows(TM, CHUNK)
    vec = _whole((1, CHUNK))
    return _call(
        body, name=name, grid=(T // TM,),
        in_specs=[_rows(TM, CHUNK, 0), _rows(TM, CHUNK, 1), tile, _rows(TM, CHUNK, 0), _rows(TM, CHUNK, 3),
                  tile, tile, _whole((ATTN_DIM, ATTN_DIM)), vec, vec, vec, vec, _whole((CONV_DIM, CONV_DIM))],
        out_specs=[tile, tile, tile, tile, tile, _whole((8, CHUNK))],
        out_shape=[half(BF16), half(BF16), half(BF16), half(BF16), half(F32),
                   jax.ShapeDtypeStruct((8, CHUNK), F32)],
        args=(dycat, dycat, o, ug, ug, c3, conv, head_mean, g_attn, g_conv, ln_g, ln_b, w_pw,), ride=ride)


def _conv_bwd(dconv, ug, dw_w, name, ride=None):
    T = dconv.shape[0]
    per = TM // HALO
    last = T // HALO - 1
    n_tiles = T // TM

    def body(d_ref, dn_ref, cv_ref, cg_ref, cvh_ref, cgh_ref, w_ref, dcv_ref, dcg_ref, dw_ref,
             dpad_ref, cpad_ref, dsh_ref, csh_ref, dw_acc):
        i = pl.program_id(0)

        @pl.when(i == 0)
        def _():
            dw_acc[...] = jnp.zeros_like(dw_acc)

        tail = jnp.zeros((SUBLANES, CONV_DIM), F32)
        dpad_ref[0:TM, :] = d_ref[...]
        dpad_ref[TM:TM + HALO, :] = jnp.where(i == n_tiles - 1, 0.0, dn_ref[...])
        dpad_ref[TM + HALO:, :] = tail
        halo = cvh_ref[...] * _sigmoid(cgh_ref[...])
        cpad_ref[0:HALO, :] = jnp.where(i == 0, 0.0, halo)
        cpad_ref[HALO:HALO + TM, :] = cv_ref[...] * _sigmoid(cg_ref[...])
        cpad_ref[HALO + TM:, :] = tail
        _shifted_copies(dpad_ref, dsh_ref)
        _shifted_copies(cpad_ref, csh_ref)
        taps = [w_ref[t:t + 1, :] for t in range(CONV_WIDTH)]

        def rows(j, _):
            r = pl.multiple_of(j * CONV_ROWS, CONV_ROWS)
            d = d_ref[pl.ds(r, CONV_ROWS), :]
            dc = jnp.zeros((CONV_ROWS, CONV_DIM), F32)
            for t in range(CONV_WIDTH):
                b, a = _shift_of(CONV_WIDTH - 1 - t)
                dc = dc + taps[t] * dsh_ref[b, pl.ds(r + a, CONV_ROWS), :]
                b, a = _shift_of(HALO - (CONV_WIDTH - 1) + t)
                prod = d * csh_ref[b, pl.ds(r + a, CONV_ROWS), :]
                dw_acc[t] += jnp.sum(prod.reshape(CONV_ROWS // SUBLANES, SUBLANES, CONV_DIM), axis=0)
            cv = cv_ref[pl.ds(r, CONV_ROWS), :]
            sg = _sigmoid(cg_ref[pl.ds(r, CONV_ROWS), :])
            dcv_ref[pl.ds(r, CONV_ROWS), :] = (dc * sg).astype(BF16)
            dcg_ref[pl.ds(r, CONV_ROWS), :] = (dc * cv * sg * (1.0 - sg)).astype(BF16)
            return 0

        lax.fori_loop(0, TM // CONV_ROWS, rows, 0)

        @pl.when(i == n_tiles - 1)
        def _():
            dw_ref[...] = jnp.zeros_like(dw_ref)
            for t in range(CONV_WIDTH):
                dw_ref[t:t + 1, :] = jnp.sum(dw_acc[t], axis=0, keepdims=True)

    prev = lambda col: pl.BlockSpec((HALO, CHUNK), lambda i: (jnp.maximum(i * per - 1, 0), col))
    nxt = pl.BlockSpec((HALO, CONV_DIM), lambda i: (jnp.minimum((i + 1) * per, last), 0))
    half = jax.ShapeDtypeStruct((T, CHUNK), BF16)
    return _call(
        body, name=name, grid=(T // TM,),
        in_specs=[_rows(TM, CONV_DIM), nxt, _rows(TM, CHUNK, 1), _rows(TM, CHUNK, 2), prev(1), prev(2),
                  _whole((CONV_WIDTH, CONV_DIM))],
        out_specs=[_rows(TM, CHUNK), _rows(TM, CHUNK), _whole((HALO, CONV_DIM))],
        out_shape=[half, half, jax.ShapeDtypeStruct((HALO, CONV_DIM), F32)],
        scratch_shapes=[pltpu.VMEM((TM + HALO + SUBLANES, CONV_DIM), F32),
                        pltpu.VMEM((TM + HALO + SUBLANES, CONV_DIM), F32),
                        pltpu.VMEM((SUBLANES, TM + HALO, CONV_DIM), F32),
                        pltpu.VMEM((SUBLANES, TM + HALO, CONV_DIM), F32),
                        pltpu.VMEM((HALO, SUBLANES, CONV_DIM), F32)],
        args=(dconv, dconv, ug, ug, ug, ug, dw_w,), ride=ride)


def _attn_bwd(qs, k, v, do, cs, tri, tri_t, name, ride=None):
    T = qs.shape[0]
    nq = T // BLK
    width = LANES * ATT_COLS
    chains = [(c, half) for c in range(ATT_COLS) for half in range(2)]

    def body(q_ref, k_ref, v_ref, do_ref, cs_ref, m_ref, mt_ref, dq_ref, dk_ref, dv_ref, dk_acc, dv_acc):
        qi = pl.program_id(1)

        @pl.when(qi == 0)
        def _():
            dk_acc[...] = jnp.zeros_like(dk_acc)
            dv_acc[...] = jnp.zeros_like(dv_acc)

        lane = lax.broadcasted_iota(jnp.int32, (BLK, LANES), 1)
        first = lane < HEAD_DIM
        causal = (lax.broadcasted_iota(jnp.int32, (BLK, BLK), 1)
                  < lax.broadcasted_iota(jnp.int32, (BLK, BLK), 0))
        tri_m = m_ref[...]
        tri_mt = mt_ref[...]

        def halves(x):
            zero = jnp.zeros_like(x)
            return jnp.where(first, x, zero), jnp.where(first, zero, x)

        qh, doh, cs = {}, {}, []
        for c in range(ATT_COLS):
            qh[c, 0], qh[c, 1] = halves(q_ref[:, c * LANES:(c + 1) * LANES])
            doh[c, 0], doh[c, 1] = halves(do_ref[:, c * LANES:(c + 1) * LANES])
            cs.append(cs_ref[:, c * LANES:(c + 1) * LANES])

        def step(kb, state, masked):
            prefixes, dq_accs = state
            start = pl.multiple_of(kb * BLK, BLK)
            kblk = [k_ref[pl.ds(start, BLK), c * LANES:(c + 1) * LANES] for c in range(ATT_COLS)]
            vblk = [v_ref[pl.ds(start, BLK), c * LANES:(c + 1) * LANES] for c in range(ATT_COLS)]
            z = [_nt(qh[ch], kblk[ch[0]]) for ch in chains]
            da = [_nt(doh[ch], vblk[ch[0]]) for ch in chains]
            parts = [_softplus_parts(zi) for zi in z]
            sp = [pt[1] for pt in parts]
            if masked:
                sp = [jnp.where(causal, s, 0.0) for s in sp]
            incl = [_dot_hilo(s, tri_m) for s in sp]
            carries = [jnp.sum(jnp.where(lane == kb + HEAD_DIM * half, cs[c], 0.0), axis=1, keepdims=True)
                       for c, half in chains]
            a = [jnp.exp(zi - ii - ci) for zi, ii, ci in zip(z, incl, carries)]
            if masked:
                a = [jnp.where(causal, ai, 0.0) for ai in a]
            w = [ai * di for ai, di in zip(a, da)]
            pinc = [_dot_hilo(wi, tri_mt) for wi in w]
            beta = [jnp.where(zi >= 0.0, 1.0, pt[0]) / (1.0 + pt[0]) for zi, pt in zip(z, parts)]
            dz = [wi - bi * (pi + pre) for wi, bi, pi, pre in zip(w, beta, pinc, prefixes)]
            if masked:
                dz = [jnp.where(causal, d, 0.0) for d in dz]
            dq_accs = list(dq_accs)
            for c in range(ATT_COLS):
                k0, k1 = halves(kblk[c])
                dz0, dz1 = dz[2 * c].astype(BF16), dz[2 * c + 1].astype(BF16)
                a0, a1 = a[2 * c].astype(BF16), a[2 * c + 1].astype(BF16)
                dq_accs[c] = dq_accs[c] + _nn(dz0, k0) + _nn(dz1, k1)
                dk_acc[pl.ds(start, BLK), c * LANES:(c + 1) * LANES] += _tn(dz0, qh[c, 0]) + _tn(dz1, qh[c, 1])
                dv_acc[pl.ds(start, BLK), c * LANES:(c + 1) * LANES] += _tn(a0, doh[c, 0]) + _tn(a1, doh[c, 1])
            prefixes = tuple(pre + pi[:, BLK - 1:BLK] for pre, pi in zip(prefixes, pinc))
            return prefixes, tuple(dq_accs)

        state = (tuple(jnp.zeros((BLK, 1), F32) for _ in chains),
                 tuple(jnp.zeros((BLK, LANES), F32) for _ in range(ATT_COLS)))
        first_block = jnp.max(jnp.where(lane == FIRST_BLOCK_LANE, cs[0], 0.0)).astype(jnp.int32)
        state = lax.fori_loop(first_block, qi, lambda kb, st: step(kb, st, False), state)
        state = step(qi, state, True)
        for c in range(ATT_COLS):
            dq_ref[:, c * LANES:(c + 1) * LANES] = (state[1][c] * (HEAD_DIM ** -0.5)).astype(BF16)

        @pl.when(qi == nq - 1)
        def _():
            dk_ref[...] = dk_acc[...].astype(BF16)
            dv_ref[...] = dv_acc[...].astype(BF16)

    blk = pl.BlockSpec((BLK, width), lambda j, i: (i, j))
    col = pl.BlockSpec((T, width), lambda j, i: (0, j))
    out = jax.ShapeDtypeStruct((T, ATTN_DIM), BF16)
    return _call(
        body, name=name, grid=(ATTN_DIM // width, nq),
        in_specs=[blk, col, col, blk, blk, _whole((BLK, BLK)), _whole((BLK, BLK))],
        out_specs=[blk, col, col], out_shape=[out, out, out],
        scratch_shapes=[pltpu.VMEM((T, width), F32), pltpu.VMEM((T, width), F32)],
        args=(qs, k, v, do, cs, tri, tri_t,), ride=ride)


def _inproj_bwd(du, w_in_t, h, dh1, gain, name, ride=None):
    T = h.shape[0]

    def body(*refs):
        du_refs = refs[:N_CHUNK]
        w_ref, h_ref, dh1_ref, g_ref, dh_ref, gsum_ref = refs[N_CHUNK:]

        @pl.when(pl.program_id(0) == 0)
        def _():
            gsum_ref[...] = jnp.zeros_like(gsum_ref)

        dhn = jnp.zeros((TM, D_MODEL), F32)
        for j in range(N_CHUNK):
            dhn = dhn + _nn(du_refs[j][...], w_ref[j * CHUNK:(j + 1) * CHUNK, :])
        hv = h_ref[...]
        r = lax.rsqrt(jnp.mean(hv * hv, axis=-1, keepdims=True) + EPS)
        xh = hv * r
        gsum_ref[...] += jnp.sum(dhn * xh, axis=0, keepdims=True)
        dxh = dhn * g_ref[...]
        dh_ref[...] = dh1_ref[...] + r * (dxh - xh * jnp.mean(dxh * xh, axis=-1, keepdims=True))

    full = _rows(TM, D_MODEL)
    return _call(
        body, name=name, grid=(T // TM,),
        in_specs=[_rows(TM, CHUNK)] * N_CHUNK + [_whole((N_CHUNK * CHUNK, D_MODEL)), full, full,
                                                 _whole((1, D_MODEL))],
        out_specs=[full, _whole((1, D_MODEL))],
        out_shape=[jax.ShapeDtypeStruct((T, D_MODEL), F32), jax.ShapeDtypeStruct((1, D_MODEL), F32)],
        args=(*du, w_in_t, h, dh1, gain), ride=ride)


def _weight_grad(lhs_list, rhs, name, tk=CHUNK, ride=None):
    T, n_rhs = rhs.shape
    n = len(lhs_list)
    ka = lhs_list[0].shape[1]
    per = ka // tk

    def body(*refs):
        a_refs, b_ref, out_ref = refs[:n], refs[n], refs[n + 1]
        step = pl.program_id(0)
        for j in range(n):
            for s in range(per):
                @pl.when(step == j * per + s)
                def _(j=j, s=s):
                    out_ref[...] = _tn(a_refs[j][:, s * tk:(s + 1) * tk], b_ref[...]).astype(BF16)

    (grad,), landed = _call(
        body, name=name, grid=(n * per,),
        in_specs=[_whole((T, ka))] * n + [_whole((T, n_rhs))],
        out_specs=[pl.BlockSpec((tk, n_rhs), lambda i: (i, 0))],
        out_shape=[jax.ShapeDtypeStruct((n * ka, n_rhs), BF16)],
        args=(*lhs_list, rhs), ride=ride)
    return grad, landed


def _sum_slots(slots, name):
    n = len(slots)

    def body(*refs):
        for src, dst in zip(refs[:n], refs[n:]):
            acc = src[0].astype(F32)
            for s in range(1, N_DEV):
                acc = acc + src[s].astype(F32)
            dst[...] = acc

    return pl.pallas_call(
        body, name=name,
        out_shape=[jax.ShapeDtypeStruct(s.shape[1:], F32) for s in slots],
        compiler_params=_params(),
    )(*slots)


def _adamw(w, g, m, v, name):
    R, C = w.shape
    tr = R
    for cand in (512, 256, 128, 64):
        if R % cand == 0 and R > cand:
            tr = cand
            break

    def body(w_ref, g_ref, m_ref, v_ref, d_ref, nm_ref, nv_ref):
        gv = g_ref[...]
        nm = ADAM_B1 * m_ref[...] + (1.0 - ADAM_B1) * gv
        nv = ADAM_B2 * v_ref[...] + (1.0 - ADAM_B2) * (gv * gv)
        m_hat = nm / (1.0 - ADAM_B1 ** ADAM_STEP)
        v_hat = nv / (1.0 - ADAM_B2 ** ADAM_STEP)
        d_ref[...] = -ADAM_LR * (m_hat / (jnp.sqrt(v_hat) + ADAM_EPS) + ADAM_WD * w_ref[...])
        nm_ref[...] = nm
        nv_ref[...] = nv

    spec = pl.BlockSpec((tr, C), lambda i: (i, 0))
    out = jax.ShapeDtypeStruct((R, C), F32)
    return pl.pallas_call(
        body, name=name, grid=(R // tr,),
        in_specs=[spec] * 4, out_specs=[spec] * 3, out_shape=[out, out, out],
        compiler_params=_params(1),
    )(w, g, m, v)


def _pack_small(norm_g, ple_norm_g, final_g, dw_b, conv_ln_g, conv_ln_b, conv_out_g, attn_out_g, scalar=None):
    flat = lambda a: a.reshape(1, -1)
    pad = lambda a: jnp.pad(a, ((0, 0), (0, D_MODEL - a.shape[1])))
    last = jnp.zeros((1, D_MODEL), F32) if scalar is None else pad(scalar.reshape(1, 1))
    rows = [norm_g, ple_norm_g, flat(final_g), flat(dw_b), flat(conv_ln_g), flat(conv_ln_b),
            flat(conv_out_g), pad(flat(attn_out_g)), last]
    used = sum(r.shape[0] for r in rows)
    return jnp.concatenate(rows + [jnp.zeros((SMALL_ROWS - used, D_MODEL), F32)], axis=0)


def _unpack_small(a):
    two = lambda r: a[r].reshape(2, -1)
    return dict(norm_g=a[0:2], ple_norm_g=a[2:4], final_g=a[4], dw_b=two(5), conv_ln_g=two(6),
                conv_ln_b=two(7), conv_out_g=two(8), attn_out_g=a[9, :2 * HEAD_DIM].reshape(2, HEAD_DIM))


def kernel(x, p, norm_g, w_in, attn_out_g, dw_w, dw_b, conv_ln_g, conv_ln_b, w_pw, conv_out_g, w_out, ple_norm_g, w_ple_gate, w_ple, final_g, loss_target, m_norm_g, m_w_in, m_attn_out_g, m_dw_w, m_dw_b, m_conv_ln_g, m_conv_ln_b, m_w_pw, m_conv_out_g, m_w_out, m_ple_norm_g, m_w_ple_gate, m_w_ple, m_final_g, v_norm_g, v_w_in, v_attn_out_g, v_dw_w, v_dw_b, v_conv_ln_g, v_conv_ln_b, v_w_pw, v_conv_out_g, v_w_out, v_ple_norm_g, v_w_ple_gate, v_w_ple, v_final_g):
    depth = w_in.shape[0]
    T = x.shape[1]
    my_idx = 4 * lax.axis_index("x") + 2 * lax.axis_index("y") + lax.axis_index("c")

    ids = jnp.arange(BLK)
    tri = (ids[:, None] >= ids[None, :]).astype(BF16)
    tri_t = (ids[:, None] <= ids[None, :]).astype(BF16)
    hid = jnp.arange(ATTN_DIM) // HEAD_DIM
    head_mean = ((hid[:, None] == hid[None, :]).astype(F32) / HEAD_DIM).astype(BF16)

    w_names = ("w_in_t", "w_pw", "w_out", "w_gate", "w_ple")
    w_axes = dict(zip(w_names, (0, 0, 0, 0, 1)))
    shards = [dict(zip(w_names, (w_in[l].T.astype(BF16), w_pw[l].astype(BF16), w_out[l].astype(BF16),
                                 w_ple_gate[l].astype(BF16), w_ple[l].astype(BF16)))) for l in range(depth)]
    first = _all_gather([shards[0][n] for n in w_names] + [dw_w[l].T for l in range(depth)],
                        [w_axes[n] for n in w_names] + [0] * depth, "gather_weights_0")
    layers = []
    for l in range(depth):
        layers.append(dict(
            dw_w=first[len(w_names) + l].T,
            g_norm=norm_g[l][None], g_attn=jnp.tile(attn_out_g[l], N_HEADS)[None], dw_b=dw_b[l][None],
            ln_g=conv_ln_g[l][None], ln_b=conv_ln_b[l][None], g_conv=conv_out_g[l][None],
            g_ple=ple_norm_g[l][None], p=p[l, 0]))
    layers[0].update(zip(w_names, first))

    def gathered(l, names):
        return [_Ride.gather(shards[l][n], w_axes[n]) for n in names] if l > 0 else []

    h = x[0]
    saved = []
    for l, w in enumerate(layers):
        nxt = [None] if l + 1 < depth else []

        def quarter(i, l=l, nxt=nxt):
            return [_Ride.gather(shards[l + 1]["w_in_t"], 0, nxt[0], *W_IN_QUARTERS[i])] if nxt else []

        (qs, k, v, ug, hn), landed = _prenorm_inproj(h, w["g_norm"], w["w_in_t"], f"inproj_{l}",
                                                     _Ride(quarter(0) + gathered(l, ("w_gate",))))
        nxt[:1] = landed[:len(landed) - (l > 0)]
        w.update(zip(("w_gate",), landed[len(landed) - 1:] if l > 0 else ()))
        (o, cs), landed = _attn_fwd(qs, k, v, tri, f"attn_fwd_{l}",
                                    _Ride(quarter(1) + gathered(l, ("w_out", "w_ple"))))
        nxt[:1] = landed[:len(landed) - 2 * (l > 0)]
        w.update(zip(("w_out", "w_ple"), landed[len(landed) - 2:] if l > 0 else ()))
        (conv, c2), landed = _conv_fwd(ug, w["dw_w"], w["dw_b"], w["ln_g"], w["ln_b"], f"conv_fwd_{l}",
                                       _Ride(quarter(2) + gathered(l, ("w_pw",))))
        nxt[:1] = landed[:len(landed) - (l > 0)]
        w.update(zip(("w_pw",), landed[len(landed) - 1:] if l > 0 else ()))
        (h2, h1, ycat, hn2, gate, e, c3), landed = _mix_out_ple(
            o, ug, c2, h, w["p"], head_mean, w["g_attn"], w["g_conv"], w["g_ple"],
            w["w_pw"], w["w_out"], w["w_gate"], w["w_ple"], f"mix_{l}", _Ride(quarter(3)))
        if l + 1 < depth:
            layers[l + 1]["w_in_t"] = landed[0]
        saved.append(dict(h=h, qs=qs, k=k, v=v, ug=ug, hn=hn, o=o, cs=cs, conv=conv, c2=c2, h1=h1,
                          ycat=ycat, hn2=hn2, gate=gate, e=e, c3=c3))
        h = h2
    dh, g_final, loss_part = _final_loss(h, loss_target[0], final_g[None], "final_loss")

    small = {}
    dww_parts = [None] * depth
    slots = [dict() for _ in range(depth)]
    g_w_in = None
    for l in reversed(range(depth)):
        w, s = layers[l], saved[l]
        above = [None] if g_w_in is not None else []

        def part(i, above=above, g=g_w_in):
            return [_Ride.scatter(g, 0, above[0], *W_IN_GRAD_PARTS[i])] if above else []

        def scattered(grads, names):
            return [_Ride.scatter(grads[n], w_axes[n]) for n in names]

        (dh1, dh1b, dzg, de, dycat, g_ple_sum), landed = _ple_out_bwd(
            dh, s["gate"], s["e"], s["h1"], w["g_ple"], w["w_gate"], w["w_out"], f"ple_bwd_{l}", _Ride(part(0)))
        above[:1] = landed
        (do, dga, dgc, dc3, dconv, sums), landed = _branch_bwd(
            dycat, s["o"], s["ug"], s["c3"], s["conv"], head_mean, w["g_attn"], w["g_conv"],
            w["ln_g"], w["ln_b"], w["w_pw"], f"branch_bwd_{l}", _Ride(part(1)))
        above[:1] = landed
        grads = dict(
            w_pw=_weight_grad([s["c2"]], dc3, f"grad_w_pw_{l}")[0],
            w_out=_weight_grad([s["ycat"]], dh1b, f"grad_w_out_{l}")[0],
            w_gate=_weight_grad([s["hn2"]], dzg, f"grad_w_gate_{l}")[0],
            w_ple=_weight_grad([w["p"].astype(BF16)], de, f"grad_w_ple_{l}", tk=PLE_DIM)[0])
        on_conv, on_attn = (("w_out", "w_pw"), ("w_gate", "w_ple")) if l > 0 else (("w_pw", "w_ple"), ("w_out", "w_gate"))
        (dcv, dcg, dww), landed = _conv_bwd(dconv, s["ug"], w["dw_w"], f"conv_bwd_{l}",
                                            _Ride(part(2) + scattered(grads, on_conv)))
        above[:1] = landed[:len(landed) - len(on_conv)]
        slots[l].update(zip(on_conv, landed[len(landed) - len(on_conv):]))
        (dq, dk, dv), landed = _attn_bwd(s["qs"], s["k"], s["v"], do, s["cs"], tri, tri_t, f"attn_bwd_{l}",
                                         _Ride(scattered(grads, on_attn)))
        slots[l].update(zip(on_attn, landed))
        du = [dq, dk, dv, dga, dcv, dcg, dgc]
        g_w_in_here, landed = _weight_grad(du, s["hn"], f"grad_w_in_{l}", ride=_Ride(part(3)))
        if above:
            slots[l + 1]["w_in_t"] = landed[0]
        tail = [_Ride.scatter(g_w_in_here, 0)] if l == 0 else []
        (dh, g_norm_sum), landed = _inproj_bwd(du, w["w_in_t"], s["h"], dh1, w["g_norm"], f"inproj_bwd_{l}",
                                               _Ride(tail))
        slots[l].update(zip(("w_in_t",), landed))
        g_w_in = g_w_in_here
        small[l] = dict(norm_g=g_norm_sum, ple_norm_g=g_ple_sum, attn_out_g=sums[0].reshape(N_HEADS, HEAD_DIM).sum(0),
                        conv_out_g=sums[1], conv_ln_g=sums[2], conv_ln_b=sums[3], dw_b=sums[4])
        dww_parts[l] = dww[:CONV_WIDTH]
    slots = [[sl[n] for n in w_names] for sl in slots]
    grad_x = dh[None]

    stack = lambda name: jnp.stack([small[l][name].reshape(-1) for l in range(depth)])
    small_part = _pack_small(stack("norm_g"), stack("ple_norm_g"), g_final[0], stack("dw_b"), stack("conv_ln_g"),
                             stack("conv_ln_b"), stack("conv_out_g"), stack("attn_out_g"), scalar=loss_part[0, 0])
    pack = jnp.concatenate([small_part, jnp.concatenate(dww_parts, axis=1),
                            jnp.zeros((PACK_ROWS - SMALL_ROWS - CONV_WIDTH, D_MODEL), F32)], axis=0)
    (all_packs,) = _all_gather([pack], [0], "gather_small_grads")
    (pack_sum,) = _sum_slots([all_packs.reshape(N_DEV, PACK_ROWS, D_MODEL)], "sum_small_grads")
    loss = pack_sum[SMALL_ROWS - 8 + 2, 0]
    g_small = pack_sum[:SMALL_ROWS].at[SMALL_ROWS - 8 + 2, 0].set(0.0)
    dww_full = pack_sum[SMALL_ROWS:SMALL_ROWS + CONV_WIDTH].reshape(CONV_WIDTH, depth, CONV_DIM).transpose(1, 0, 2)
    g_dw_w = lax.dynamic_slice_in_dim(dww_full, my_idx * (CONV_DIM // N_DEV), CONV_DIM // N_DEV, axis=2)

    w_small = _pack_small(norm_g, ple_norm_g, final_g, dw_b, conv_ln_g, conv_ln_b, conv_out_g, attn_out_g)
    m_small = _pack_small(m_norm_g, m_ple_norm_g, m_final_g, m_dw_b, m_conv_ln_g, m_conv_ln_b, m_conv_out_g, m_attn_out_g)
    v_small = _pack_small(v_norm_g, v_ple_norm_g, v_final_g, v_dw_b, v_conv_ln_g, v_conv_ln_b, v_conv_out_g, v_attn_out_g)
    d_small, nm_small, nv_small = _adamw(w_small, g_small, m_small, v_small, "adamw_small")
    res = {"g": _unpack_small(g_small), "d": _unpack_small(d_small), "m": _unpack_small(nm_small),
           "v": _unpack_small(nv_small)}

    sums = [_sum_slots(slots[l], f"sum_grads_{l}") for l in range(depth)]
    big = {
        "w_in": jnp.stack([sums[l][0].T for l in range(depth)]),
        "w_pw": jnp.stack([sums[l][1] for l in range(depth)]),
        "w_out": jnp.stack([sums[l][2] for l in range(depth)]),
        "w_ple_gate": jnp.stack([sums[l][3] for l in range(depth)]),
        "w_ple": jnp.stack([sums[l][4] for l in range(depth)]),
        "dw_w": g_dw_w,
    }
    state = {"w_in": (w_in, m_w_in, v_w_in), "w_pw": (w_pw, m_w_pw, v_w_pw), "w_out": (w_out, m_w_out, v_w_out),
             "w_ple_gate": (w_ple_gate, m_w_ple_gate, v_w_ple_gate), "w_ple": (w_ple, m_w_ple, v_w_ple),
             "dw_w": (dw_w, m_dw_w, v_dw_w)}
    for name, g in big.items():
        wv, mv, vv = state[name]
        flat = lambda a: a.reshape(-1, a.shape[-1])
        d, nm, nv = _adamw(flat(wv), flat(g), flat(mv), flat(vv), f"adamw_{name}")
        res["g"][name] = g
        res["d"][name], res["m"][name], res["v"][name] = (d.reshape(wv.shape), nm.reshape(wv.shape),
                                                         nv.reshape(wv.shape))

    order = ["norm_g", "w_in", "attn_out_g", "dw_w", "dw_b", "conv_ln_g", "conv_ln_b", "w_pw", "conv_out_g",
             "w_out", "ple_norm_g", "w_ple_gate", "w_ple", "final_g"]
    return (loss, grad_x, *[res["g"][n] for n in order], *[res["d"][n] for n in order],
            *[res["m"][n] for n in order], *[res["v"][n] for n in order])
```

```python
import functools

import jax
import jax.numpy as jnp
from jax import lax
from jax.experimental import pallas as pl
from jax.experimental.pallas import tpu as pltpu

F32 = jnp.float32
BF16 = jnp.bfloat16
MESH = pl.DeviceIdType.MESH

N_DEV = 8
D_MODEL = 1024
ATTN_DIM = 512
CONV_DIM = 512
HEAD_DIM = 64
N_HEADS = 8
CONV_WIDTH = 31
PLE_DIM = 256
CHUNK = 512
N_CHUNK = 7
EPS = 1e-6
ADAM_LR = 0.001
ADAM_B1 = 0.9
ADAM_B2 = 0.999
ADAM_EPS = 1e-08
ADAM_WD = 0.01
ADAM_STEP = 10

LANES = 128
BLK = 256
ATT_COLS = 4
DEAD_AT = 110.0
FIRST_BLOCK_LANE = HEAD_DIM - 1
TM = 256
HALO = 32
SUBLANES = 8
CONV_ROWS = 32
VMEM_LIMIT = 56 * 1024 * 1024
SMALL_ROWS = 16
W_IN_QUARTERS = ((0, 96), (96, 144), (240, 96), (336, 112))
W_IN_GRAD_PARTS = ((0, 96), (96, 80), (176, 144), (320, 128))
OTHER_WEIGHTS_ON = {"inproj": (("w_gate", (0, 64)),), "attn": (("w_out", ()), ("w_ple", ())),
                    "conv": (("w_pw", ()), ("w_gate", (64, 64)))}
PACK_ROWS = 48


def _nn(a, b):
    return lax.dot_general(a, b, (((1,), (0,)), ((), ())), preferred_element_type=F32)


def _nt(a, b):
    return lax.dot_general(a, b, (((1,), (1,)), ((), ())), preferred_element_type=F32)


def _tn(a, b):
    return lax.dot_general(a, b, (((0,), (0,)), ((), ())), preferred_element_type=F32)


def _split(x):
    hi = x.astype(BF16)
    lo = (x - hi.astype(F32)).astype(BF16)
    return hi, lo


def _dot_hilo(x, m):
    hi, lo = _split(x)
    return _nn(hi, m) + _nn(lo, m)


def _sigmoid(x):
    return jax.nn.sigmoid(x)


def _dsilu(x, s):
    return s * (1.0 + x * (1.0 - s))


def _params(n_grid=0, vmem=VMEM_LIMIT):
    sem = ("arbitrary",) * n_grid if n_grid else None
    return pltpu.CompilerParams(dimension_semantics=sem, vmem_limit_bytes=vmem)


def _rows(tm, cols, col=0):
    return pl.BlockSpec((tm, cols), lambda i: (i, col))


def _whole(shape):
    zeros = (0,) * len(shape)
    return pl.BlockSpec(shape, lambda *_: zeros)


def _my_position():
    return lax.axis_index("x"), lax.axis_index("y"), lax.axis_index("c")


def _block(ref, axis, idx, size):
    start = pl.multiple_of(idx * size, size)
    if axis == 0:
        return ref.at[pl.ds(start, size), :]
    return ref.at[:, pl.ds(start, size)]


def _all_gather(shards, axes, name):
    n = len(shards)
    sizes = [s.shape[a] for s, a in zip(shards, axes)]

    def full_shape(s, a):
        shape = list(s.shape)
        shape[a] *= N_DEV
        return jax.ShapeDtypeStruct(tuple(shape), s.dtype)

    def body(*refs):
        ins, outs = refs[:n], refs[n:2 * n]
        send_sems, recv_sems, local_sems = refs[2 * n:]
        x, y, c = _my_position()
        me, sibling = (x, y, c), (x, y, 1 - c)
        chips = [(1 - x, y), (x, 1 - y), (1 - x, 1 - y)]

        def place(i, dev):
            return _block(outs[i], axes[i], 4 * dev[0] + 2 * dev[1] + dev[2], sizes[i])

        def copy(k, i, dev, to, src=None):
            return pltpu.make_async_remote_copy(
                src_ref=place(i, dev) if src is None else src, dst_ref=place(i, dev),
                send_sem=send_sems.at[k, i], recv_sem=recv_sems.at[k, i],
                device_id=to, device_id_type=MESH)

        mine = [pltpu.make_async_copy(ins[i], place(i, me), local_sems.at[i]) for i in range(n)]
        for cp in mine:
            cp.start()
        first = [copy(0, i, me, sibling, src=ins[i]) for i in range(n)]
        for j, chip in enumerate(chips):
            first += [copy(1 + j, i, me, (*chip, c), src=ins[i]) for i in range(n)]
        for cp in first:
            cp.start()
        passed = []
        for j, chip in enumerate(chips):
            for i in range(n):
                copy(1 + j, i, (*chip, c), me).wait_recv()
            hop = [copy(4 + j, i, (*chip, c), sibling) for i in range(n)]
            for cp in hop:
                cp.start()
            passed += hop
        for i in range(n):
            copy(0, i, sibling, me).wait_recv()
        for j, chip in enumerate(chips):
            for i in range(n):
                copy(4 + j, i, (*chip, 1 - c), me).wait_recv()
        for cp in first + passed:
            cp.wait_send()
        for cp in mine:
            cp.wait()

    any_spec = pl.BlockSpec(memory_space=pl.ANY)
    return pl.pallas_call(
        body, name=name,
        out_shape=[full_shape(s, a) for s, a in zip(shards, axes)],
        in_specs=[any_spec] * n, out_specs=[any_spec] * n,
        scratch_shapes=[pltpu.SemaphoreType.DMA((7, n)), pltpu.SemaphoreType.DMA((7, n)),
                        pltpu.SemaphoreType.DMA((n,))],
    )(*shards)


class _Ride:
    def __init__(self, parts):
        self.parts = [p for p in parts if p is not None]

    @staticmethod
    def gather(src, axis, land=None, lo=0, n=None):
        return ("gather", src, land, axis, lo, src.shape[axis] if n is None else n)

    @staticmethod
    def scatter(src, axis, land=None, lo=0, n=None):
        return ("scatter", src, land, axis, lo, src.shape[axis] // N_DEV if n is None else n)

    def arrays(self):
        return [p[1] for p in self.parts] + [p[2] for p in self.parts if p[2] is not None]

    def out_shapes(self):
        out = []
        for kind, src, _, axis, _, _ in self.parts:
            shape = list(src.shape)
            if kind == "gather":
                shape[axis] *= N_DEV
            else:
                shape[axis] //= N_DEV
                shape = [N_DEV] + shape
            out.append(jax.ShapeDtypeStruct(tuple(shape), src.dtype))
        return out

    def aliases(self, n_in, n_out):
        m, out = len(self.parts), {}
        for j, p in enumerate(self.parts):
            if p[2] is not None:
                out[n_in + m + len(out)] = n_out + j
        return out

    def scratch(self):
        m = len(self.parts)
        return [pltpu.SemaphoreType.DMA((N_DEV - 1, m)), pltpu.SemaphoreType.DMA((N_DEV - 1, m)),
                pltpu.SemaphoreType.DMA((m,))]

    def _copies(self, src_refs, land_refs, sems):
        send_sems, recv_sems, local_sems = sems
        x, y, c = _my_position()
        my_idx = 4 * x + 2 * y + c
        own, sends, lands = [], [], []
        for j, (kind, src, _, axis, lo, n) in enumerate(self.parts):
            size = src.shape[axis] if kind == "gather" else src.shape[axis] // N_DEV
            align = 16 if axis == 0 else LANES

            def rows(ref, idx, lead=None, axis=axis, lo=lo, n=n, size=size, align=align):
                at = pl.ds(pl.multiple_of(idx * size + lo, align), n)
                where = (at, slice(None)) if axis == 0 else (slice(None), at)
                return ref.at[where] if lead is None else ref.at[(lead, *where)]

            def in_shard(ref):
                return rows(ref, 0)

            def in_slot(ref, s):
                return rows(ref, 0, lead=s)

            for k in range(N_DEV):
                px = 1 - x if k & 4 else x
                py = 1 - y if k & 2 else y
                pc = 1 - c if k & 1 else c
                peer_idx = 4 * px + 2 * py + pc
                if kind == "gather":
                    a, b, landed = in_shard(src_refs[j]), rows(land_refs[j], my_idx), rows(land_refs[j], peer_idx)
                else:
                    a, b, landed = rows(src_refs[j], peer_idx), in_slot(land_refs[j], my_idx), in_slot(land_refs[j], peer_idx)
                if k == 0:
                    own.append(pltpu.make_async_copy(a, b, local_sems.at[j]))
                    continue
                mk = lambda dst, a=a, k=k, j=j, to=(px, py, pc): pltpu.make_async_remote_copy(
                    src_ref=a, dst_ref=dst, send_sem=send_sems.at[k - 1, j], recv_sem=recv_sems.at[k - 1, j],
                    device_id=to, device_id_type=MESH)
                sends.append(mk(b))
                lands.append(mk(landed))
        return own, sends, lands

    def start(self, src_refs, land_refs, sems):
        own, sends, _ = self._copies(src_refs, land_refs, sems)
        for cp in own + sends:
            cp.start()

    def wait(self, src_refs, land_refs, sems):
        own, sends, lands = self._copies(src_refs, land_refs, sems)
        for cp in lands:
            cp.wait_recv()
        for cp in sends:
            cp.wait_send()
        for cp in own:
            cp.wait()


def _call(body, *, name, grid, in_specs, out_specs, out_shape, args, scratch_shapes=(), ride=None):
    in_specs, out_specs, out_shape = list(in_specs), list(out_specs), list(out_shape)
    n_in, n_out, n_sc = len(in_specs), len(out_specs), len(scratch_shapes)
    if ride is None or not ride.parts:
        res = pl.pallas_call(body, name=name, grid=grid, in_specs=in_specs, out_specs=out_specs,
                             out_shape=out_shape, scratch_shapes=list(scratch_shapes),
                             compiler_params=_params(len(grid)))(*args)
        return list(res), []
    extra, m = ride.arrays(), len(ride.parts)

    def riding(*refs):
        a = n_in + len(extra)
        b = a + n_out
        srcs, lands, sems = refs[n_in:n_in + m], refs[b:b + m], refs[b + m + n_sc:]
        at = [pl.program_id(d) for d in range(len(grid))]

        @pl.when(functools.reduce(jnp.logical_and, [i == 0 for i in at]))
        def _():
            ride.start(srcs, lands, sems)

        body(*refs[:n_in], *refs[a:b], *refs[b + m:b + m + n_sc])

        @pl.when(functools.reduce(jnp.logical_and, [i == g - 1 for i, g in zip(at, grid)]))
        def _():
            ride.wait(srcs, lands, sems)

    hbm = pl.BlockSpec(memory_space=pl.ANY)
    res = pl.pallas_call(
        riding, name=name, grid=grid, in_specs=in_specs + [hbm] * len(extra), out_specs=out_specs + [hbm] * m,
        out_shape=out_shape + ride.out_shapes(), scratch_shapes=list(scratch_shapes) + ride.scratch(),
        input_output_aliases=ride.aliases(n_in, n_out), compiler_params=_params(len(grid)),
    )(*args, *extra)
    return list(res[:n_out]), list(res[n_out:])


def _prenorm_inproj(h, gain, w_in_t, name, ride=None):
    T = h.shape[0]

    def body(h_ref, g_ref, w_ref, q_ref, k_ref, v_ref, ug_ref, hn_ref):
        hv = h_ref[...]
        r = lax.rsqrt(jnp.mean(hv * hv, axis=-1, keepdims=True) + EPS)
        hn = (hv * r * g_ref[...]).astype(BF16)
        hn_ref[...] = hn
        for j in range(N_CHUNK):
            u = _nt(hn, w_ref[j * CHUNK:(j + 1) * CHUNK, :])
            if j == 0:
                q_ref[...] = (u * (HEAD_DIM ** -0.5)).astype(BF16)
            elif j == 1:
                k_ref[...] = u.astype(BF16)
            elif j == 2:
                v_ref[...] = u.astype(BF16)
            else:
                ug_ref[:, (j - 3) * CHUNK:(j - 2) * CHUNK] = u

    act = jax.ShapeDtypeStruct((T, CHUNK), BF16)
    return _call(
        body, name=name, grid=(T // TM,),
        in_specs=[_rows(TM, D_MODEL), _whole((1, D_MODEL)), _whole((N_CHUNK * CHUNK, D_MODEL))],
        out_specs=[_rows(TM, CHUNK)] * 3 + [_rows(TM, 4 * CHUNK), _rows(TM, D_MODEL)],
        out_shape=[act, act, act, jax.ShapeDtypeStruct((T, 4 * CHUNK), F32),
                   jax.ShapeDtypeStruct((T, D_MODEL), BF16)],
        args=(h, gain, w_in_t,), ride=ride)


def _softplus_parts(z):
    e = jnp.exp(-jnp.abs(z))
    return e, jnp.maximum(z, 0.0) + jnp.log(1.0 + e)


def _attn_fwd(qs, k, v, tri, name, ride=None):
    T = qs.shape[0]
    assert T // BLK <= FIRST_BLOCK_LANE, "one lane per key block below the lane of the first block"
    width = LANES * ATT_COLS
    chains = [(c, half) for c in range(ATT_COLS) for half in range(2)]

    def body(q_ref, k_ref, v_ref, m_ref, o_ref, cs_ref):
        qi = pl.program_id(1)
        lane = lax.broadcasted_iota(jnp.int32, (BLK, LANES), 1)
        first = lane < HEAD_DIM
        causal = (lax.broadcasted_iota(jnp.int32, (BLK, BLK), 1)
                  < lax.broadcasted_iota(jnp.int32, (BLK, BLK), 0))
        tri_m = m_ref[...]
        qh = {}
        for c in range(ATT_COLS):
            q = q_ref[:, c * LANES:(c + 1) * LANES]
            zero = jnp.zeros_like(q)
            qh[c, 0], qh[c, 1] = jnp.where(first, q, zero), jnp.where(first, zero, q)

        def step(kb, state, masked):
            carries, accs, cvals = state
            start = pl.multiple_of(kb * BLK, BLK)
            kblk = [k_ref[pl.ds(start, BLK), c * LANES:(c + 1) * LANES] for c in range(ATT_COLS)]
            vblk = [v_ref[pl.ds(start, BLK), c * LANES:(c + 1) * LANES] for c in range(ATT_COLS)]
            z = [_nt(qh[ch], kblk[ch[0]]) for ch in chains]
            sp = [_softplus_parts(zi)[1] for zi in z]
            if masked:
                sp = [jnp.where(causal, s, 0.0) for s in sp]
            incl = [_dot_hilo(s, tri_m) for s in sp]
            a = [jnp.exp(zi - ii - ci) for zi, ii, ci in zip(z, incl, carries)]
            if masked:
                a = [jnp.where(causal, ai, 0.0) for ai in a]
            accs, cvals = list(accs), list(cvals)
            for n, (c, half) in enumerate(chains):
                zero = jnp.zeros_like(vblk[c])
                vh = jnp.where(first, vblk[c], zero) if half == 0 else jnp.where(first, zero, vblk[c])
                accs[c] = accs[c] + _nn(a[n].astype(BF16), vh)
                cvals[c] = jnp.where(lane == kb + HEAD_DIM * half, carries[n], cvals[c])
            carries = tuple(ci + ii[:, 0:1] for ci, ii in zip(carries, incl))
            return carries, tuple(accs), tuple(cvals)

        zeros = tuple(jnp.zeros((BLK, LANES), F32) for _ in range(ATT_COLS))
        state = (tuple(jnp.zeros((BLK, 1), F32) for _ in chains), zeros, zeros)
        state = step(qi, state, True)

        def reaches_further(st):
            it, (carries, _, _) = st
            least = functools.reduce(jnp.minimum, carries)
            return jnp.logical_and(it < qi, jnp.min(least) < DEAD_AT)

        done, state = lax.while_loop(reaches_further, lambda st: (st[0] + 1, step(qi - 1 - st[0], st[1], False)),
                                     (jnp.int32(0), state))
        first_block = (qi - done).astype(F32)
        for c in range(ATT_COLS):
            o_ref[:, c * LANES:(c + 1) * LANES] = state[1][c]
            cs_ref[:, c * LANES:(c + 1) * LANES] = jnp.where(lane == FIRST_BLOCK_LANE, first_block, state[2][c])

    blk = pl.BlockSpec((BLK, width), lambda j, i: (i, j))
    col = pl.BlockSpec((T, width), lambda j, i: (0, j))
    out = jax.ShapeDtypeStruct((T, ATTN_DIM), F32)
    return _call(
        body, name=name, grid=(ATTN_DIM // width, T // BLK),
        in_specs=[blk, col, col, _whole((BLK, BLK))],
        out_specs=[blk, blk], out_shape=[out, out],
        args=(qs, k, v, tri,), ride=ride)


def _shifted_copies(pad_ref, sh_ref):
    rows = sh_ref.shape[1]
    for b in range(SUBLANES):
        sh_ref[b] = pad_ref[b:b + rows, :]


def _shift_of(offset):
    return offset % SUBLANES, offset - offset % SUBLANES


def _conv_fwd(ug, dw_w, dw_b, ln_g, ln_b, name, ride=None):
    T = ug.shape[0]
    per = TM // HALO

    def body(cv_ref, cg_ref, cvh_ref, cgh_ref, w_ref, b_ref, g_ref, beta_ref, conv_ref, c2_ref, pad_ref, sh_ref):
        i = pl.program_id(0)
        halo = cvh_ref[...] * _sigmoid(cgh_ref[...])
        pad_ref[0:HALO, :] = jnp.where(i == 0, 0.0, halo)
        pad_ref[HALO:HALO + TM, :] = cv_ref[...] * _sigmoid(cg_ref[...])
        pad_ref[HALO + TM:, :] = jnp.zeros((SUBLANES, CONV_DIM), F32)
        _shifted_copies(pad_ref, sh_ref)
        taps = [w_ref[t:t + 1, :] for t in range(CONV_WIDTH)]

        def rows(j, _):
            r = pl.multiple_of(j * CONV_ROWS, CONV_ROWS)
            acc = jnp.zeros((CONV_ROWS, CONV_DIM), F32) + b_ref[...]
            for t in range(CONV_WIDTH):
                b, a = _shift_of(HALO - (CONV_WIDTH - 1) + t)
                acc = acc + taps[t] * sh_ref[b, pl.ds(r + a, CONV_ROWS), :]
            conv_ref[pl.ds(r, CONV_ROWS), :] = acc
            return 0

        lax.fori_loop(0, TM // CONV_ROWS, rows, 0)
        acc = conv_ref[...]
        mu = jnp.mean(acc, axis=-1, keepdims=True)
        xc = acc - mu
        rs = lax.rsqrt(jnp.mean(xc * xc, axis=-1, keepdims=True) + EPS)
        ln = xc * rs * g_ref[...] + beta_ref[...]
        c2_ref[...] = (ln * _sigmoid(ln)).astype(BF16)

    prev = lambda col: pl.BlockSpec((HALO, CHUNK), lambda i: (jnp.maximum(i * per - 1, 0), col))
    vec = _whole((1, CONV_DIM))
    return _call(
        body, name=name, grid=(T // TM,),
        in_specs=[_rows(TM, CHUNK, 1), _rows(TM, CHUNK, 2), prev(1), prev(2),
                  _whole((CONV_WIDTH, CONV_DIM)), vec, vec, vec],
        out_specs=[_rows(TM, CONV_DIM), _rows(TM, CONV_DIM)],
        out_shape=[jax.ShapeDtypeStruct((T, CONV_DIM), F32), jax.ShapeDtypeStruct((T, CONV_DIM), BF16)],
        scratch_shapes=[pltpu.VMEM((TM + HALO + SUBLANES, CONV_DIM), F32),
                        pltpu.VMEM((SUBLANES, TM + HALO, CONV_DIM), F32)],
        args=(ug, ug, ug, ug, dw_w, dw_b, ln_g, ln_b,), ride=ride)


def _mix_out_ple(o, ug, c2, h, p, head_mean, g_attn, g_conv, g_ple, w_pw, w_out, w_gate, w_ple, name, ride=None):
    T = h.shape[0]

    def body(o_ref, ga_ref, gc_ref, c2_ref, h_ref, p_ref, hm_ref, gao_ref, gco_ref, gpn_ref,
             wpw_ref, wout_ref, wg_ref, wple_ref,
             h2_ref, h1_ref, ycat_ref, hn2_ref, gate_ref, e_ref, c3_ref):
        ov = o_ref[...]
        rh = lax.rsqrt(_nn((ov * ov).astype(BF16), hm_ref[...]) + EPS)
        ga = ga_ref[...]
        ya = (ov * rh * gao_ref[...] * (ga * _sigmoid(ga))).astype(BF16)
        c3 = _nn(c2_ref[...], wpw_ref[...])
        c3_ref[...] = c3
        rc = lax.rsqrt(jnp.mean(c3 * c3, axis=-1, keepdims=True) + EPS)
        gc = gc_ref[...]
        yc = (c3 * rc * gco_ref[...] * (gc * _sigmoid(gc))).astype(BF16)
        ycat_ref[:, :ATTN_DIM] = ya
        ycat_ref[:, ATTN_DIM:] = yc
        h1 = h_ref[...] + _nn(ya, wout_ref[:ATTN_DIM, :]) + _nn(yc, wout_ref[ATTN_DIM:, :])
        h1_ref[...] = h1
        r1 = lax.rsqrt(jnp.mean(h1 * h1, axis=-1, keepdims=True) + EPS)
        hn2 = (h1 * r1 * gpn_ref[...]).astype(BF16)
        hn2_ref[...] = hn2
        gate = _sigmoid(_nn(hn2, wg_ref[...]))
        e = _nn(p_ref[...].astype(BF16), wple_ref[...])
        gate_ref[...] = gate
        e_ref[...] = e
        h2_ref[...] = h1 + e * gate

    f32 = lambda cols: jax.ShapeDtypeStruct((T, cols), F32)
    bf = lambda cols: jax.ShapeDtypeStruct((T, cols), BF16)
    return _call(
        body, name=name, grid=(T // TM,),
        in_specs=[_rows(TM, ATTN_DIM), _rows(TM, CHUNK, 0), _rows(TM, CHUNK, 3), _rows(TM, CONV_DIM),
                  _rows(TM, D_MODEL), _rows(TM, PLE_DIM), _whole((ATTN_DIM, ATTN_DIM)),
                  _whole((1, ATTN_DIM)), _whole((1, CONV_DIM)), _whole((1, D_MODEL)),
                  _whole((CONV_DIM, CONV_DIM)), _whole((D_MODEL, D_MODEL)), _whole((D_MODEL, D_MODEL)),
                  _whole((PLE_DIM, D_MODEL))],
        out_specs=[_rows(TM, D_MODEL), _rows(TM, D_MODEL), _rows(TM, D_MODEL), _rows(TM, D_MODEL),
                   _rows(TM, D_MODEL), _rows(TM, D_MODEL), _rows(TM, CONV_DIM)],
        out_shape=[f32(D_MODEL), f32(D_MODEL), bf(D_MODEL), bf(D_MODEL), f32(D_MODEL), f32(D_MODEL),
                   f32(CONV_DIM)],
        args=(o, ug, ug, c2, h, p, head_mean, g_attn, g_conv, g_ple, w_pw, w_out, w_gate, w_ple,), ride=ride)


def _final_loss(h, target, gain, name):
    T = h.shape[0]

    def body(h_ref, t_ref, g_ref, dh_ref, gsum_ref, loss_ref):
        @pl.when(pl.program_id(0) == 0)
        def _():
            gsum_ref[...] = jnp.zeros_like(gsum_ref)
            loss_ref[...] = jnp.zeros_like(loss_ref)

        hv = h_ref[...]
        r = lax.rsqrt(jnp.mean(hv * hv, axis=-1, keepdims=True) + EPS)
        xh = hv * r
        diff = xh * g_ref[...] - t_ref[...]
        loss_ref[...] += 0.5 * jnp.sum(jnp.mean(diff * diff, axis=-1, keepdims=True), axis=0, keepdims=True)
        dy = diff * (1.0 / D_MODEL)
        gsum_ref[...] += jnp.sum(dy * xh, axis=0, keepdims=True)
        dxh = dy * g_ref[...]
        dh_ref[...] = r * (dxh - xh * jnp.mean(dxh * xh, axis=-1, keepdims=True))

    return pl.pallas_call(
        body, name=name, grid=(T // TM,),
        in_specs=[_rows(TM, D_MODEL), _rows(TM, D_MODEL), _whole((1, D_MODEL))],
        out_specs=[_rows(TM, D_MODEL), _whole((1, D_MODEL)), _whole((1, LANES))],
        out_shape=[jax.ShapeDtypeStruct((T, D_MODEL), F32), jax.ShapeDtypeStruct((1, D_MODEL), F32),
                   jax.ShapeDtypeStruct((1, LANES), F32)],
        compiler_params=_params(1),
    )(h, target, gain)


def _ple_out_bwd(dh2, gate, e, h1, g_ple, w_gate, w_out, name, ride=None):
    T = dh2.shape[0]

    def body(dh2_ref, gate_ref, e_ref, h1_ref, gpn_ref, wg_ref, wout_ref,
             dh1_ref, dh1b_ref, dzg_ref, de_ref, dycat_ref, gsum_ref):
        @pl.when(pl.program_id(0) == 0)
        def _():
            gsum_ref[...] = jnp.zeros_like(gsum_ref)

        dh2v = dh2_ref[...]
        gate = gate_ref[...]
        de_ref[...] = (dh2v * gate).astype(BF16)
        dzg = (dh2v * e_ref[...] * gate * (1.0 - gate)).astype(BF16)
        dzg_ref[...] = dzg
        dhn2 = _nt(dzg, wg_ref[...])
        h1 = h1_ref[...]
        r1 = lax.rsqrt(jnp.mean(h1 * h1, axis=-1, keepdims=True) + EPS)
        xh = h1 * r1
        gsum_ref[...] += jnp.sum(dhn2 * xh, axis=0, keepdims=True)
        dxh = dhn2 * gpn_ref[...]
        dh1 = dh2v + r1 * (dxh - xh * jnp.mean(dxh * xh, axis=-1, keepdims=True))
        dh1_ref[...] = dh1
        dh1b = dh1.astype(BF16)
        dh1b_ref[...] = dh1b
        dycat_ref[...] = _nt(dh1b, wout_ref[...])

    f32 = jax.ShapeDtypeStruct((T, D_MODEL), F32)
    bf = jax.ShapeDtypeStruct((T, D_MODEL), BF16)
    full = _rows(TM, D_MODEL)
    return _call(
        body, name=name, grid=(T // TM,),
        in_specs=[full, full, full, full, _whole((1, D_MODEL)), _whole((D_MODEL, D_MODEL)),
                  _whole((D_MODEL, D_MODEL))],
        out_specs=[full, full, full, full, full, _whole((1, D_MODEL))],
        out_shape=[f32, bf, bf, bf, f32, jax.ShapeDtypeStruct((1, D_MODEL), F32)],
        args=(dh2, gate, e, h1, g_ple, w_gate, w_out,), ride=ride)


def _branch_bwd(dycat, o, ug, c3, conv, head_mean, g_attn, g_conv, ln_g, ln_b, w_pw, name, ride=None):
    T = o.shape[0]

    def body(dya_ref, dyc_ref, o_ref, ga_ref, gc_ref, c3_ref, conv_ref, hm_ref, gao_ref, gco_ref,
             lng_ref, lnb_ref, wpw_ref,
             do_ref, dga_ref, dgc_ref, dc3_ref, dconv_ref, sums_ref):
        @pl.when(pl.program_id(0) == 0)
        def _():
            sums_ref[...] = jnp.zeros_like(sums_ref)

        hm = hm_ref[...]
        col = lambda x: jnp.sum(x, axis=0, keepdims=True)
        ov = o_ref[...]
        rh = lax.rsqrt(_nn((ov * ov).astype(BF16), hm) + EPS)
        xh = ov * rh
        ga = ga_ref[...]
        sg = _sigmoid(ga)
        dya = dya_ref[...]
        don = dya * (ga * sg)
        dga_ref[...] = (dya * xh * gao_ref[...] * _dsilu(ga, sg)).astype(BF16)
        sums_ref[0:1, :] += col(don * xh)
        dxh = don * gao_ref[...]
        do_ref[...] = (rh * (dxh - xh * _dot_hilo(dxh * xh, hm))).astype(BF16)
        c3 = c3_ref[...]
        rc = lax.rsqrt(jnp.mean(c3 * c3, axis=-1, keepdims=True) + EPS)
        xh3 = c3 * rc
        gc = gc_ref[...]
        sgc = _sigmoid(gc)
        dyc = dyc_ref[...]
        dn3 = dyc * (gc * sgc)
        dgc_ref[...] = (dyc * xh3 * gco_ref[...] * _dsilu(gc, sgc)).astype(BF16)
        sums_ref[1:2, :] += col(dn3 * xh3)
        dxh3 = dn3 * gco_ref[...]
        dc3 = (rc * (dxh3 - xh3 * jnp.mean(dxh3 * xh3, axis=-1, keepdims=True))).astype(BF16)
        dc3_ref[...] = dc3
        dc2 = _nt(dc3, wpw_ref[...])
        cv = conv_ref[...]
        mu = jnp.mean(cv, axis=-1, keepdims=True)
        xc = cv - mu
        rs = lax.rsqrt(jnp.mean(xc * xc, axis=-1, keepdims=True) + EPS)
        xn = xc * rs
        ln = xn * lng_ref[...] + lnb_ref[...]
        dln = dc2 * _dsilu(ln, _sigmoid(ln))
        sums_ref[2:3, :] += col(dln * xn)
        sums_ref[3:4, :] += col(dln)
        dxn = dln * lng_ref[...]
        dconv = rs * (dxn - jnp.mean(dxn, axis=-1, keepdims=True)
                      - xn * jnp.mean(dxn * xn, axis=-1, keepdims=True))
        dconv_ref[...] = dconv
        sums_ref[4:5, :] += col(dconv)

    half = lambda dt: jax.ShapeDtypeStruct((T, CHUNK), dt)
    tile = _rows(TM, CHUNK)
    vec = _whole((1, CHUNK))
    return _call(
        body, name=name, grid=(T // TM,),
        in_specs=[_rows(TM, CHUNK, 0), _rows(TM, CHUNK, 1), tile, _rows(TM, CHUNK, 0), _rows(TM, CHUNK, 3),
                  tile, tile, _whole((ATTN_DIM, ATTN_DIM)), vec, vec, vec, vec, _whole((CONV_DIM, CONV_DIM))],
        out_specs=[tile, tile, tile, tile, tile, _whole((8, CHUNK))],
        out_shape=[half(BF16), half(BF16), half(BF16), half(BF16), half(F32),
                   jax.ShapeDtypeStruct((8, CHUNK), F32)],
        args=(dycat, dycat, o, ug, ug, c3, conv, head_mean, g_attn, g_conv, ln_g, ln_b, w_pw,), ride=ride)


def _conv_bwd(dconv, ug, dw_w, name, ride=None):
    T = dconv.shape[0]
    per = TM // HALO
    last = T // HALO - 1
    n_tiles = T // TM

    def body(d_ref, dn_ref, cv_ref, cg_ref, cvh_ref, cgh_ref, w_ref, dcv_ref, dcg_ref, dw_ref,
             dpad_ref, cpad_ref, dsh_ref, csh_ref, dw_acc):
        i = pl.program_id(0)

        @pl.when(i == 0)
        def _():
            dw_acc[...] = jnp.zeros_like(dw_acc)

        tail = jnp.zeros((SUBLANES, CONV_DIM), F32)
        dpad_ref[0:TM, :] = d_ref[...]
        dpad_ref[TM:TM + HALO, :] = jnp.where(i == n_tiles - 1, 0.0, dn_ref[...])
        dpad_ref[TM + HALO:, :] = tail
        halo = cvh_ref[...] * _sigmoid(cgh_ref[...])
        cpad_ref[0:HALO, :] = jnp.where(i == 0, 0.0, halo)
        cpad_ref[HALO:HALO + TM, :] = cv_ref[...] * _sigmoid(cg_ref[...])
        cpad_ref[HALO + TM:, :] = tail
        _shifted_copies(dpad_ref, dsh_ref)
        _shifted_copies(cpad_ref, csh_ref)
        taps = [w_ref[t:t + 1, :] for t in range(CONV_WIDTH)]

        def rows(j, _):
            r = pl.multiple_of(j * CONV_ROWS, CONV_ROWS)
            d = d_ref[pl.ds(r, CONV_ROWS), :]
            dc = jnp.zeros((CONV_ROWS, CONV_DIM), F32)
            for t in range(CONV_WIDTH):
                b, a = _shift_of(CONV_WIDTH - 1 - t)
                dc = dc + taps[t] * dsh_ref[b, pl.ds(r + a, CONV_ROWS), :]
                b, a = _shift_of(HALO - (CONV_WIDTH - 1) + t)
                prod = d * csh_ref[b, pl.ds(r + a, CONV_ROWS), :]
                dw_acc[t] += jnp.sum(prod.reshape(CONV_ROWS // SUBLANES, SUBLANES, CONV_DIM), axis=0)
            cv = cv_ref[pl.ds(r, CONV_ROWS), :]
            sg = _sigmoid(cg_ref[pl.ds(r, CONV_ROWS), :])
            dcv_ref[pl.ds(r, CONV_ROWS), :] = (dc * sg).astype(BF16)
            dcg_ref[pl.ds(r, CONV_ROWS), :] = (dc * cv * sg * (1.0 - sg)).astype(BF16)
            return 0

        lax.fori_loop(0, TM // CONV_ROWS, rows, 0)

        @pl.when(i == n_tiles - 1)
        def _():
            dw_ref[...] = jnp.zeros_like(dw_ref)
            for t in range(CONV_WIDTH):
                dw_ref[t:t + 1, :] = jnp.sum(dw_acc[t], axis=0, keepdims=True)

    prev = lambda col: pl.BlockSpec((HALO, CHUNK), lambda i: (jnp.maximum(i * per - 1, 0), col))
    nxt = pl.BlockSpec((HALO, CONV_DIM), lambda i: (jnp.minimum((i + 1) * per, last), 0))
    half = jax.ShapeDtypeStruct((T, CHUNK), BF16)
    return _call(
        body, name=name, grid=(T // TM,),
        in_specs=[_rows(TM, CONV_DIM), nxt, _rows(TM, CHUNK, 1), _rows(TM, CHUNK, 2), prev(1), prev(2),
                  _whole((CONV_WIDTH, CONV_DIM))],
        out_specs=[_rows(TM, CHUNK), _rows(TM, CHUNK), _whole((HALO, CONV_DIM))],
        out_shape=[half, half, jax.ShapeDtypeStruct((HALO, CONV_DIM), F32)],
        scratch_shapes=[pltpu.VMEM((TM + HALO + SUBLANES, CONV_DIM), F32),
                        pltpu.VMEM((TM + HALO + SUBLANES, CONV_DIM), F32),
                        pltpu.VMEM((SUBLANES, TM + HALO, CONV_DIM), F32),
                        pltpu.VMEM((SUBLANES, TM + HALO, CONV_DIM), F32),
                        pltpu.VMEM((HALO, SUBLANES, CONV_DIM), F32)],
        args=(dconv, dconv, ug, ug, ug, ug, dw_w,), ride=ride)


def _attn_bwd(qs, k, v, do, cs, tri, tri_t, name, ride=None):
    T = qs.shape[0]
    nq = T // BLK
    width = LANES * ATT_COLS
    chains = [(c, half) for c in range(ATT_COLS) for half in range(2)]

    def body(q_ref, k_ref, v_ref, do_ref, cs_ref, m_ref, mt_ref, dq_ref, dk_ref, dv_ref, dk_acc, dv_acc):
        qi = pl.program_id(1)

        @pl.when(qi == 0)
        def _():
            dk_acc[...] = jnp.zeros_like(dk_acc)
            dv_acc[...] = jnp.zeros_like(dv_acc)

        lane = lax.broadcasted_iota(jnp.int32, (BLK, LANES), 1)
        first = lane < HEAD_DIM
        causal = (lax.broadcasted_iota(jnp.int32, (BLK, BLK), 1)
                  < lax.broadcasted_iota(jnp.int32, (BLK, BLK), 0))
        tri_m = m_ref[...]
        tri_mt = mt_ref[...]

        def halves(x):
            zero = jnp.zeros_like(x)
            return jnp.where(first, x, zero), jnp.where(first, zero, x)

        qh, doh, cs = {}, {}, []
        for c in range(ATT_COLS):
            qh[c, 0], qh[c, 1] = halves(q_ref[:, c * LANES:(c + 1) * LANES])
            doh[c, 0], doh[c, 1] = halves(do_ref[:, c * LANES:(c + 1) * LANES])
            cs.append(cs_ref[:, c * LANES:(c + 1) * LANES])

        def step(kb, state, masked):
            prefixes, dq_accs = state
            start = pl.multiple_of(kb * BLK, BLK)
            kblk = [k_ref[pl.ds(start, BLK), c * LANES:(c + 1) * LANES] for c in range(ATT_COLS)]
            vblk = [v_ref[pl.ds(start, BLK), c * LANES:(c + 1) * LANES] for c in range(ATT_COLS)]
            z = [_nt(qh[ch], kblk[ch[0]]) for ch in chains]
            da = [_nt(doh[ch], vblk[ch[0]]) for ch in chains]
            parts = [_softplus_parts(zi) for zi in z]
            sp = [pt[1] for pt in parts]
            if masked:
                sp = [jnp.where(causal, s, 0.0) for s in sp]
            incl = [_dot_hilo(s, tri_m) for s in sp]
            carries = [jnp.sum(jnp.where(lane == kb + HEAD_DIM * half, cs[c], 0.0), axis=1, keepdims=True)
                       for c, half in chains]
            a = [jnp.exp(zi - ii - ci) for zi, ii, ci in zip(z, incl, carries)]
            if masked:
                a = [jnp.where(causal, ai, 0.0) for ai in a]
            w = [ai * di for ai, di in zip(a, da)]
            pinc = [_nn(wi.astype(BF16), tri_mt) for wi in w]
            beta = [jnp.where(zi >= 0.0, 1.0, pt[0]) / (1.0 + pt[0]) for zi, pt in zip(z, parts)]
            dz = [wi - bi * (pi + pre) for wi, bi, pi, pre in zip(w, beta, pinc, prefixes)]
            if masked:
                dz = [jnp.where(causal, d, 0.0) for d in dz]
            dq_accs = list(dq_accs)
            for c in range(ATT_COLS):
                k0, k1 = halves(kblk[c])
                dz0, dz1 = dz[2 * c].astype(BF16), dz[2 * c + 1].astype(BF16)
                a0, a1 = a[2 * c].astype(BF16), a[2 * c + 1].astype(BF16)
                dq_accs[c] = dq_accs[c] + _nn(dz0, k0) + _nn(dz1, k1)
                dk_acc[pl.ds(start, BLK), c * LANES:(c + 1) * LANES] += _tn(dz0, qh[c, 0]) + _tn(dz1, qh[c, 1])
                dv_acc[pl.ds(start, BLK), c * LANES:(c + 1) * LANES] += _tn(a0, doh[c, 0]) + _tn(a1, doh[c, 1])
            prefixes = tuple(pre + pi[:, BLK - 1:BLK] for pre, pi in zip(prefixes, pinc))
            return prefixes, tuple(dq_accs)

        state = (tuple(jnp.zeros((BLK, 1), F32) for _ in chains),
                 tuple(jnp.zeros((BLK, LANES), F32) for _ in range(ATT_COLS)))
        first_block = jnp.max(jnp.where(lane == FIRST_BLOCK_LANE, cs[0], 0.0)).astype(jnp.int32)
        state = lax.fori_loop(first_block, qi, lambda kb, st: step(kb, st, False), state)
        state = step(qi, state, True)
        for c in range(ATT_COLS):
            dq_ref[:, c * LANES:(c + 1) * LANES] = (state[1][c] * (HEAD_DIM ** -0.5)).astype(BF16)

        @pl.when(qi == nq - 1)
        def _():
            dk_ref[...] = dk_acc[...].astype(BF16)
            dv_ref[...] = dv_acc[...].astype(BF16)

    blk = pl.BlockSpec((BLK, width), lambda j, i: (i, j))
    col = pl.BlockSpec((T, width), lambda j, i: (0, j))
    out = jax.ShapeDtypeStruct((T, ATTN_DIM), BF16)
    return _call(
        body, name=name, grid=(ATTN_DIM // width, nq),
        in_specs=[blk, col, col, blk, blk, _whole((BLK, BLK)), _whole((BLK, BLK))],
        out_specs=[blk, col, col], out_shape=[out, out, out],
        scratch_shapes=[pltpu.VMEM((T, width), F32), pltpu.VMEM((T, width), F32)],
        args=(qs, k, v, do, cs, tri, tri_t,), ride=ride)


def _inproj_bwd(du, w_in_t, h, dh1, gain, name, ride=None):
    T = h.shape[0]

    def body(*refs):
        du_refs = refs[:N_CHUNK]
        w_ref, h_ref, dh1_ref, g_ref, dh_ref, gsum_ref = refs[N_CHUNK:]

        @pl.when(pl.program_id(0) == 0)
        def _():
            gsum_ref[...] = jnp.zeros_like(gsum_ref)

        dhn = jnp.zeros((TM, D_MODEL), F32)
        for j in range(N_CHUNK):
            dhn = dhn + _nn(du_refs[j][...], w_ref[j * CHUNK:(j + 1) * CHUNK, :])
        hv = h_ref[...]
        r = lax.rsqrt(jnp.mean(hv * hv, axis=-1, keepdims=True) + EPS)
        xh = hv * r
        gsum_ref[...] += jnp.sum(dhn * xh, axis=0, keepdims=True)
        dxh = dhn * g_ref[...]
        dh_ref[...] = dh1_ref[...] + r * (dxh - xh * jnp.mean(dxh * xh, axis=-1, keepdims=True))

    full = _rows(TM, D_MODEL)
    return _call(
        body, name=name, grid=(T // TM,),
        in_specs=[_rows(TM, CHUNK)] * N_CHUNK + [_whole((N_CHUNK * CHUNK, D_MODEL)), full, full,
                                                 _whole((1, D_MODEL))],
        out_specs=[full, _whole((1, D_MODEL))],
        out_shape=[jax.ShapeDtypeStruct((T, D_MODEL), F32), jax.ShapeDtypeStruct((1, D_MODEL), F32)],
        args=(*du, w_in_t, h, dh1, gain), ride=ride)


def _weight_grad(lhs_list, rhs, name, tk=CHUNK, ride=None):
    T, n_rhs = rhs.shape
    n = len(lhs_list)
    ka = lhs_list[0].shape[1]
    per = ka // tk

    def body(*refs):
        a_refs, b_ref, out_ref = refs[:n], refs[n], refs[n + 1]
        step = pl.program_id(0)
        for j in range(n):
            for s in range(per):
                @pl.when(step == j * per + s)
                def _(j=j, s=s):
                    out_ref[...] = _tn(a_refs[j][:, s * tk:(s + 1) * tk], b_ref[...]).astype(BF16)

    (grad,), landed = _call(
        body, name=name, grid=(n * per,),
        in_specs=[_whole((T, ka))] * n + [_whole((T, n_rhs))],
        out_specs=[pl.BlockSpec((tk, n_rhs), lambda i: (i, 0))],
        out_shape=[jax.ShapeDtypeStruct((n * ka, n_rhs), BF16)],
        args=(*lhs_list, rhs), ride=ride)
    return grad, landed


def _sum_slots(slots, name):
    n = len(slots)

    def body(*refs):
        for src, dst in zip(refs[:n], refs[n:]):
            acc = src[0].astype(F32)
            for s in range(1, N_DEV):
                acc = acc + src[s].astype(F32)
            dst[...] = acc

    return pl.pallas_call(
        body, name=name,
        out_shape=[jax.ShapeDtypeStruct(s.shape[1:], F32) for s in slots],
        compiler_params=_params(),
    )(*slots)


def _adamw(w, g, m, v, name):
    R, C = w.shape
    tr = R
    for cand in (512, 256, 128, 64):
        if R % cand == 0 and R > cand:
            tr = cand
            break

    def body(w_ref, g_ref, m_ref, v_ref, d_ref, nm_ref, nv_ref):
        gv = g_ref[...]
        nm = ADAM_B1 * m_ref[...] + (1.0 - ADAM_B1) * gv
        nv = ADAM_B2 * v_ref[...] + (1.0 - ADAM_B2) * (gv * gv)
        m_hat = nm / (1.0 - ADAM_B1 ** ADAM_STEP)
        v_hat = nv / (1.0 - ADAM_B2 ** ADAM_STEP)
        d_ref[...] = -ADAM_LR * (m_hat / (jnp.sqrt(v_hat) + ADAM_EPS) + ADAM_WD * w_ref[...])
        nm_ref[...] = nm
        nv_ref[...] = nv

    spec = pl.BlockSpec((tr, C), lambda i: (i, 0))
    out = jax.ShapeDtypeStruct((R, C), F32)
    return pl.pallas_call(
        body, name=name, grid=(R // tr,),
        in_specs=[spec] * 4, out_specs=[spec] * 3, out_shape=[out, out, out],
        compiler_params=_params(1),
    )(w, g, m, v)


def _pack_small(norm_g, ple_norm_g, final_g, dw_b, conv_ln_g, conv_ln_b, conv_out_g, attn_out_g, scalar=None):
    flat = lambda a: a.reshape(1, -1)
    pad = lambda a: jnp.pad(a, ((0, 0), (0, D_MODEL - a.shape[1])))
    last = jnp.zeros((1, D_MODEL), F32) if scalar is None else pad(scalar.reshape(1, 1))
    rows = [norm_g, ple_norm_g, flat(final_g), flat(dw_b), flat(conv_ln_g), flat(conv_ln_b),
            flat(conv_out_g), pad(flat(attn_out_g)), last]
    used = sum(r.shape[0] for r in rows)
    return jnp.concatenate(rows + [jnp.zeros((SMALL_ROWS - used, D_MODEL), F32)], axis=0)


def _unpack_small(a):
    two = lambda r: a[r].reshape(2, -1)
    return dict(norm_g=a[0:2], ple_norm_g=a[2:4], final_g=a[4], dw_b=two(5), conv_ln_g=two(6),
                conv_ln_b=two(7), conv_out_g=two(8), attn_out_g=a[9, :2 * HEAD_DIM].reshape(2, HEAD_DIM))


def kernel(x, p, norm_g, w_in, attn_out_g, dw_w, dw_b, conv_ln_g, conv_ln_b, w_pw, conv_out_g, w_out, ple_norm_g, w_ple_gate, w_ple, final_g, loss_target, m_norm_g, m_w_in, m_attn_out_g, m_dw_w, m_dw_b, m_conv_ln_g, m_conv_ln_b, m_w_pw, m_conv_out_g, m_w_out, m_ple_norm_g, m_w_ple_gate, m_w_ple, m_final_g, v_norm_g, v_w_in, v_attn_out_g, v_dw_w, v_dw_b, v_conv_ln_g, v_conv_ln_b, v_w_pw, v_conv_out_g, v_w_out, v_ple_norm_g, v_w_ple_gate, v_w_ple, v_final_g):
    depth = w_in.shape[0]
    T = x.shape[1]
    my_idx = 4 * lax.axis_index("x") + 2 * lax.axis_index("y") + lax.axis_index("c")

    ids = jnp.arange(BLK)
    tri = (ids[:, None] >= ids[None, :]).astype(BF16)
    tri_t = (ids[:, None] <= ids[None, :]).astype(BF16)
    hid = jnp.arange(ATTN_DIM) // HEAD_DIM
    head_mean = ((hid[:, None] == hid[None, :]).astype(F32) / HEAD_DIM).astype(BF16)

    w_names = ("w_in_t", "w_pw", "w_out", "w_gate", "w_ple")
    w_axes = dict(zip(w_names, (0, 0, 0, 0, 1)))
    shards = [dict(zip(w_names, (w_in[l].T.astype(BF16), w_pw[l].astype(BF16), w_out[l].astype(BF16),
                                 w_ple_gate[l].astype(BF16), w_ple[l].astype(BF16)))) for l in range(depth)]
    first = _all_gather([shards[0][n] for n in w_names] + [dw_w[l].T for l in range(depth)],
                        [w_axes[n] for n in w_names] + [0] * depth, "gather_weights_0")
    layers = []
    for l in range(depth):
        layers.append(dict(
            dw_w=first[len(w_names) + l].T,
            g_norm=norm_g[l][None], g_attn=jnp.tile(attn_out_g[l], N_HEADS)[None], dw_b=dw_b[l][None],
            ln_g=conv_ln_g[l][None], ln_b=conv_ln_b[l][None], g_conv=conv_out_g[l][None],
            g_ple=ple_norm_g[l][None], p=p[l, 0]))
    layers[0].update(zip(w_names, first))

    h = x[0]
    saved = []
    for l, w in enumerate(layers):
        nxt = [None] if l + 1 < depth else []

        def quarter(i, l=l, nxt=nxt):
            return [_Ride.gather(shards[l + 1]["w_in_t"], 0, nxt[0], *W_IN_QUARTERS[i])] if nxt else []

        def others(where, l=l, w=w):
            plan = OTHER_WEIGHTS_ON[where] if l > 0 else ()
            return [_Ride.gather(shards[l][n], w_axes[n], w.get(n), *rows) for n, rows in plan]

        def keep(where, landed, l=l, w=w, nxt=nxt):
            plan = OTHER_WEIGHTS_ON[where] if l > 0 else ()
            nxt[:1] = landed[:len(landed) - len(plan)]
            w.update(zip([n for n, _ in plan], landed[len(landed) - len(plan):]))

        (qs, k, v, ug, hn), landed = _prenorm_inproj(h, w["g_norm"], w["w_in_t"], f"inproj_{l}",
                                                     _Ride(quarter(0) + others("inproj")))
        keep("inproj", landed)
        (o, cs), landed = _attn_fwd(qs, k, v, tri, f"attn_fwd_{l}", _Ride(quarter(1) + others("attn")))
        keep("attn", landed)
        (conv, c2), landed = _conv_fwd(ug, w["dw_w"], w["dw_b"], w["ln_g"], w["ln_b"], f"conv_fwd_{l}",
                                       _Ride(quarter(2) + others("conv")))
        keep("conv", landed)
        (h2, h1, ycat, hn2, gate, e, c3), landed = _mix_out_ple(
            o, ug, c2, h, w["p"], head_mean, w["g_attn"], w["g_conv"], w["g_ple"],
            w["w_pw"], w["w_out"], w["w_gate"], w["w_ple"], f"mix_{l}", _Ride(quarter(3)))
        if l + 1 < depth:
            layers[l + 1]["w_in_t"] = landed[0]
        saved.append(dict(h=h, qs=qs, k=k, v=v, ug=ug, hn=hn, o=o, cs=cs, conv=conv, c2=c2, h1=h1,
                          ycat=ycat, hn2=hn2, gate=gate, e=e, c3=c3))
        h = h2
    dh, g_final, loss_part = _final_loss(h, loss_target[0], final_g[None], "final_loss")

    small = {}
    dww_parts = [None] * depth
    slots = [dict() for _ in range(depth)]
    g_w_in = None
    for l in reversed(range(depth)):
        w, s = layers[l], saved[l]
        above = [None] if g_w_in is not None else []

        def part(i, above=above, g=g_w_in):
            return [_Ride.scatter(g, 0, above[0], *W_IN_GRAD_PARTS[i])] if above else []

        def scattered(grads, names):
            return [_Ride.scatter(grads[n], w_axes[n]) for n in names]

        (dh1, dh1b, dzg, de, dycat, g_ple_sum), landed = _ple_out_bwd(
            dh, s["gate"], s["e"], s["h1"], w["g_ple"], w["w_gate"], w["w_out"], f"ple_bwd_{l}", _Ride(part(0)))
        above[:1] = landed
        (do, dga, dgc, dc3, dconv, sums), landed = _branch_bwd(
            dycat, s["o"], s["ug"], s["c3"], s["conv"], head_mean, w["g_attn"], w["g_conv"],
            w["ln_g"], w["ln_b"], w["w_pw"], f"branch_bwd_{l}", _Ride(part(1)))
        above[:1] = landed
        grads = dict(
            w_pw=_weight_grad([s["c2"]], dc3, f"grad_w_pw_{l}")[0],
            w_out=_weight_grad([s["ycat"]], dh1b, f"grad_w_out_{l}")[0],
            w_gate=_weight_grad([s["hn2"]], dzg, f"grad_w_gate_{l}")[0],
            w_ple=_weight_grad([w["p"].astype(BF16)], de, f"grad_w_ple_{l}", tk=PLE_DIM)[0])
        (dcv, dcg, dww), landed = _conv_bwd(dconv, s["ug"], w["dw_w"], f"conv_bwd_{l}", _Ride(part(2)))
        above[:1] = landed
        (dq, dk, dv), landed = _attn_bwd(s["qs"], s["k"], s["v"], do, s["cs"], tri, tri_t, f"attn_bwd_{l}",
                                         _Ride(scattered(grads, w_names[1:])))
        slots[l].update(zip(w_names[1:], landed))
        du = [dq, dk, dv, dga, dcv, dcg, dgc]
        g_w_in_here, landed = _weight_grad(du, s["hn"], f"grad_w_in_{l}", ride=_Ride(part(3)))
        if above:
            slots[l + 1]["w_in_t"] = landed[0]
        tail = [_Ride.scatter(g_w_in_here, 0)] if l == 0 else []
        (dh, g_norm_sum), landed = _inproj_bwd(du, w["w_in_t"], s["h"], dh1, w["g_norm"], f"inproj_bwd_{l}",
                                               _Ride(tail))
        slots[l].update(zip(("w_in_t",), landed))
        g_w_in = g_w_in_here
        small[l] = dict(norm_g=g_norm_sum, ple_norm_g=g_ple_sum, attn_out_g=sums[0].reshape(N_HEADS, HEAD_DIM).sum(0),
                        conv_out_g=sums[1], conv_ln_g=sums[2], conv_ln_b=sums[3], dw_b=sums[4])
        dww_parts[l] = dww[:CONV_WIDTH]
    slots = [[sl[n] for n in w_names] for sl in slots]
    grad_x = dh[None]

    stack = lambda name: jnp.stack([small[l][name].reshape(-1) for l in range(depth)])
    small_part = _pack_small(stack("norm_g"), stack("ple_norm_g"), g_final[0], stack("dw_b"), stack("conv_ln_g"),
                             stack("conv_ln_b"), stack("conv_out_g"), stack("attn_out_g"), scalar=loss_part[0, 0])
    pack = jnp.concatenate([small_part, jnp.concatenate(dww_parts, axis=1),
                            jnp.zeros((PACK_ROWS - SMALL_ROWS - CONV_WIDTH, D_MODEL), F32)], axis=0)
    (all_packs,) = _all_gather([pack], [0], "gather_small_grads")
    (pack_sum,) = _sum_slots([all_packs.reshape(N_DEV, PACK_ROWS, D_MODEL)], "sum_small_grads")
    loss = pack_sum[SMALL_ROWS - 8 + 2, 0]
    g_small = pack_sum[:SMALL_ROWS].at[SMALL_ROWS - 8 + 2, 0].set(0.0)
    dww_full = pack_sum[SMALL_ROWS:SMALL_ROWS + CONV_WIDTH].reshape(CONV_WIDTH, depth, CONV_DIM).transpose(1, 0, 2)
    g_dw_w = lax.dynamic_slice_in_dim(dww_full, my_idx * (CONV_DIM // N_DEV), CONV_DIM // N_DEV, axis=2)

    w_small = _pack_small(norm_g, ple_norm_g, final_g, dw_b, conv_ln_g, conv_ln_b, conv_out_g, attn_out_g)
    m_small = _pack_small(m_norm_g, m_ple_norm_g, m_final_g, m_dw_b, m_conv_ln_g, m_conv_ln_b, m_conv_out_g, m_attn_out_g)
    v_small = _pack_small(v_norm_g, v_ple_norm_g, v_final_g, v_dw_b, v_conv_ln_g, v_conv_ln_b, v_conv_out_g, v_attn_out_g)
    d_small, nm_small, nv_small = _adamw(w_small, g_small, m_small, v_small, "adamw_small")
    res = {"g": _unpack_small(g_small), "d": _unpack_small(d_small), "m": _unpack_small(nm_small),
           "v": _unpack_small(nv_small)}

    sums = [_sum_slots(slots[l], f"sum_grads_{l}") for l in range(depth)]
    swap = lambda a: a.transpose(0, 2, 1)
    big = {
        "w_in": jnp.stack([sums[l][0] for l in range(depth)]),
        "w_pw": jnp.stack([sums[l][1] for l in range(depth)]),
        "w_out": jnp.stack([sums[l][2] for l in range(depth)]),
        "w_ple_gate": jnp.stack([sums[l][3] for l in range(depth)]),
        "w_ple": jnp.stack([sums[l][4] for l in range(depth)]),
        "dw_w": g_dw_w,
    }
    state = {"w_in": (w_in, m_w_in, v_w_in), "w_pw": (w_pw, m_w_pw, v_w_pw), "w_out": (w_out, m_w_out, v_w_out),
             "w_ple_gate": (w_ple_gate, m_w_ple_gate, v_w_ple_gate), "w_ple": (w_ple, m_w_ple, v_w_ple),
             "dw_w": (dw_w, m_dw_w, v_dw_w)}
    for name, g in big.items():
        wv, mv, vv = [swap(a) for a in state[name]] if name == "w_in" else state[name]
        flat = lambda a: a.reshape(-1, a.shape[-1])
        out = [g] + [a.reshape(wv.shape) for a in _adamw(flat(wv), flat(g), flat(mv), flat(vv), f"adamw_{name}")]
        out = [swap(a) for a in out] if name == "w_in" else out
        res["g"][name], res["d"][name], res["m"][name], res["v"][name] = out

    order = ["norm_g", "w_in", "attn_out_g", "dw_w", "dw_b", "conv_ln_g", "conv_ln_b", "w_pw", "conv_out_g",
             "w_out", "ple_norm_g", "w_ple_gate", "w_ple", "final_g"]
    return (loss, grad_x, *[res["g"][n] for n in order], *[res["d"][n] for n in order],
            *[res["m"][n] for n in order], *[res["v"][n] for n in order])
```

```python
import functools

import jax
import jax.numpy as jnp
from jax import lax
from jax.experimental import pallas as pl
from jax.experimental.pallas import tpu as pltpu

F32 = jnp.float32
BF16 = jnp.bfloat16
MESH = pl.DeviceIdType.MESH

N_DEV = 8
D_MODEL = 1024
ATTN_DIM = 512
CONV_DIM = 512
HEAD_DIM = 64
N_HEADS = 8
CONV_WIDTH = 31
PLE_DIM = 256
CHUNK = 512
N_CHUNK = 7
EPS = 1e-6
ADAM_LR = 0.001
ADAM_B1 = 0.9
ADAM_B2 = 0.999
ADAM_EPS = 1e-08
ADAM_WD = 0.01
ADAM_STEP = 10

LANES = 128
BLK = 256
ATT_COLS = 4
DEAD_AT = 110.0
FIRST_BLOCK_LANE = HEAD_DIM - 1
TM = 256
HALO = 32
SUBLANES = 8
CONV_ROWS = 32
ADAMW_ROWS = 64
VMEM_LIMIT = 56 * 1024 * 1024
SMALL_ROWS = 16
SMALL_LAYOUT = (("norm_g", 0, 2, D_MODEL), ("ple_norm_g", 2, 2, D_MODEL), ("final_g", 4, 1, D_MODEL),
                ("dw_b", 5, 2, CONV_DIM), ("conv_ln_g", 7, 2, CONV_DIM), ("conv_ln_b", 9, 2, CONV_DIM),
                ("conv_out_g", 11, 2, CONV_DIM), ("attn_out_g", 13, 2, HEAD_DIM))
LOSS_ROW = 15
W_IN_QUARTERS = ((0, 96), (96, 144), (240, 96), (336, 112))
W_IN_GRAD_PARTS = ((0, 96), (96, 80), (176, 144), (320, 128))
OTHER_WEIGHTS_ON = {"inproj": (("w_gate", (0, 64)),), "attn": (("w_out", ()), ("w_ple", ())),
                    "conv": (("w_pw", ()), ("w_gate", (64, 64)))}
PACK_ROWS = 48


def _nn(a, b):
    return lax.dot_general(a, b, (((1,), (0,)), ((), ())), preferred_element_type=F32)


def _nt(a, b):
    return lax.dot_general(a, b, (((1,), (1,)), ((), ())), preferred_element_type=F32)


def _tn(a, b):
    return lax.dot_general(a, b, (((0,), (0,)), ((), ())), preferred_element_type=F32)


def _split(x):
    hi = x.astype(BF16)
    lo = (x - hi.astype(F32)).astype(BF16)
    return hi, lo


def _dot_hilo(x, m):
    hi, lo = _split(x)
    return _nn(hi, m) + _nn(lo, m)


def _sigmoid(x):
    return jax.nn.sigmoid(x)


def _dsilu(x, s):
    return s * (1.0 + x * (1.0 - s))


def _params(n_grid=0, vmem=VMEM_LIMIT):
    sem = ("arbitrary",) * n_grid if n_grid else None
    return pltpu.CompilerParams(dimension_semantics=sem, vmem_limit_bytes=vmem)


def _rows(tm, cols, col=0):
    return pl.BlockSpec((tm, cols), lambda i: (i, col))


def _whole(shape):
    zeros = (0,) * len(shape)
    return pl.BlockSpec(shape, lambda *_: zeros)


def _my_position():
    return lax.axis_index("x"), lax.axis_index("y"), lax.axis_index("c")


def _block(ref, axis, idx, size):
    start = pl.multiple_of(idx * size, size)
    if axis == 0:
        return ref.at[pl.ds(start, size), :]
    return ref.at[:, pl.ds(start, size)]


def _all_gather(shards, axes, name):
    n = len(shards)
    sizes = [s.shape[a] for s, a in zip(shards, axes)]

    def full_shape(s, a):
        shape = list(s.shape)
        shape[a] *= N_DEV
        return jax.ShapeDtypeStruct(tuple(shape), s.dtype)

    def body(*refs):
        ins, outs = refs[:n], refs[n:2 * n]
        send_sems, recv_sems, local_sems = refs[2 * n:]
        x, y, c = _my_position()
        me, sibling = (x, y, c), (x, y, 1 - c)
        chips = [(1 - x, y), (x, 1 - y), (1 - x, 1 - y)]

        def place(i, dev):
            return _block(outs[i], axes[i], 4 * dev[0] + 2 * dev[1] + dev[2], sizes[i])

        def copy(k, i, dev, to, src=None):
            return pltpu.make_async_remote_copy(
                src_ref=place(i, dev) if src is None else src, dst_ref=place(i, dev),
                send_sem=send_sems.at[k, i], recv_sem=recv_sems.at[k, i],
                device_id=to, device_id_type=MESH)

        mine = [pltpu.make_async_copy(ins[i], place(i, me), local_sems.at[i]) for i in range(n)]
        for cp in mine:
            cp.start()
        first = [copy(0, i, me, sibling, src=ins[i]) for i in range(n)]
        for j, chip in enumerate(chips):
            first += [copy(1 + j, i, me, (*chip, c), src=ins[i]) for i in range(n)]
        for cp in first:
            cp.start()
        passed = []
        for j, chip in enumerate(chips):
            for i in range(n):
                copy(1 + j, i, (*chip, c), me).wait_recv()
            hop = [copy(4 + j, i, (*chip, c), sibling) for i in range(n)]
            for cp in hop:
                cp.start()
            passed += hop
        for i in range(n):
            copy(0, i, sibling, me).wait_recv()
        for j, chip in enumerate(chips):
            for i in range(n):
                copy(4 + j, i, (*chip, 1 - c), me).wait_recv()
        for cp in first + passed:
            cp.wait_send()
        for cp in mine:
            cp.wait()

    any_spec = pl.BlockSpec(memory_space=pl.ANY)
    return pl.pallas_call(
        body, name=name,
        out_shape=[full_shape(s, a) for s, a in zip(shards, axes)],
        in_specs=[any_spec] * n, out_specs=[any_spec] * n,
        scratch_shapes=[pltpu.SemaphoreType.DMA((7, n)), pltpu.SemaphoreType.DMA((7, n)),
                        pltpu.SemaphoreType.DMA((n,))],
    )(*shards)


class _Ride:
    def __init__(self, parts):
        self.parts = [p for p in parts if p is not None]

    @staticmethod
    def gather(src, axis, land=None, lo=0, n=None):
        return ("gather", src, land, axis, lo, src.shape[axis] if n is None else n)

    @staticmethod
    def scatter(src, axis, land=None, lo=0, n=None):
        return ("scatter", src, land, axis, lo, src.shape[axis] // N_DEV if n is None else n)

    def arrays(self):
        return [p[1] for p in self.parts] + [p[2] for p in self.parts if p[2] is not None]

    def out_shapes(self):
        out = []
        for kind, src, _, axis, _, _ in self.parts:
            shape = list(src.shape)
            if kind == "gather":
                shape[axis] *= N_DEV
            else:
                shape[axis] //= N_DEV
                shape = [N_DEV] + shape
            out.append(jax.ShapeDtypeStruct(tuple(shape), src.dtype))
        return out

    def aliases(self, n_in, n_out):
        m, out = len(self.parts), {}
        for j, p in enumerate(self.parts):
            if p[2] is not None:
                out[n_in + m + len(out)] = n_out + j
        return out

    def scratch(self):
        m = len(self.parts)
        return [pltpu.SemaphoreType.DMA((N_DEV - 1, m)), pltpu.SemaphoreType.DMA((N_DEV - 1, m)),
                pltpu.SemaphoreType.DMA((m,))]

    def _copies(self, src_refs, land_refs, sems):
        send_sems, recv_sems, local_sems = sems
        x, y, c = _my_position()
        my_idx = 4 * x + 2 * y + c
        own, sends, lands = [], [], []
        for j, (kind, src, _, axis, lo, n) in enumerate(self.parts):
            size = src.shape[axis] if kind == "gather" else src.shape[axis] // N_DEV
            align = 16 if axis == 0 else LANES

            def rows(ref, idx, lead=None, axis=axis, lo=lo, n=n, size=size, align=align):
                at = pl.ds(pl.multiple_of(idx * size + lo, align), n)
                where = (at, slice(None)) if axis == 0 else (slice(None), at)
                return ref.at[where] if lead is None else ref.at[(lead, *where)]

            def in_shard(ref):
                return rows(ref, 0)

            def in_slot(ref, s):
                return rows(ref, 0, lead=s)

            for k in range(N_DEV):
                px = 1 - x if k & 4 else x
                py = 1 - y if k & 2 else y
                pc = 1 - c if k & 1 else c
                peer_idx = 4 * px + 2 * py + pc
                if kind == "gather":
                    a, b, landed = in_shard(src_refs[j]), rows(land_refs[j], my_idx), rows(land_refs[j], peer_idx)
                else:
                    a, b, landed = rows(src_refs[j], peer_idx), in_slot(land_refs[j], my_idx), in_slot(land_refs[j], peer_idx)
                if k == 0:
                    own.append(pltpu.make_async_copy(a, b, local_sems.at[j]))
                    continue
                mk = lambda dst, a=a, k=k, j=j, to=(px, py, pc): pltpu.make_async_remote_copy(
                    src_ref=a, dst_ref=dst, send_sem=send_sems.at[k - 1, j], recv_sem=recv_sems.at[k - 1, j],
                    device_id=to, device_id_type=MESH)
                sends.append(mk(b))
                lands.append(mk(landed))
        return own, sends, lands

    def start(self, src_refs, land_refs, sems):
        own, sends, _ = self._copies(src_refs, land_refs, sems)
        for cp in own + sends:
            cp.start()

    def wait(self, src_refs, land_refs, sems):
        own, sends, lands = self._copies(src_refs, land_refs, sems)
        for cp in lands:
            cp.wait_recv()
        for cp in sends:
            cp.wait_send()
        for cp in own:
            cp.wait()


def _call(body, *, name, grid, in_specs, out_specs, out_shape, args, scratch_shapes=(), ride=None):
    in_specs, out_specs, out_shape = list(in_specs), list(out_specs), list(out_shape)
    n_in, n_out, n_sc = len(in_specs), len(out_specs), len(scratch_shapes)
    if ride is None or not ride.parts:
        res = pl.pallas_call(body, name=name, grid=grid, in_specs=in_specs, out_specs=out_specs,
                             out_shape=out_shape, scratch_shapes=list(scratch_shapes),
                             compiler_params=_params(len(grid)))(*args)
        return list(res), []
    extra, m = ride.arrays(), len(ride.parts)

    def riding(*refs):
        a = n_in + len(extra)
        b = a + n_out
        srcs, lands, sems = refs[n_in:n_in + m], refs[b:b + m], refs[b + m + n_sc:]
        at = [pl.program_id(d) for d in range(len(grid))]

        @pl.when(functools.reduce(jnp.logical_and, [i == 0 for i in at]))
        def _():
            ride.start(srcs, lands, sems)

        body(*refs[:n_in], *refs[a:b], *refs[b + m:b + m + n_sc])

        @pl.when(functools.reduce(jnp.logical_and, [i == g - 1 for i, g in zip(at, grid)]))
        def _():
            ride.wait(srcs, lands, sems)

    hbm = pl.BlockSpec(memory_space=pl.ANY)
    res = pl.pallas_call(
        riding, name=name, grid=grid, in_specs=in_specs + [hbm] * len(extra), out_specs=out_specs + [hbm] * m,
        out_shape=out_shape + ride.out_shapes(), scratch_shapes=list(scratch_shapes) + ride.scratch(),
        input_output_aliases=ride.aliases(n_in, n_out), compiler_params=_params(len(grid)),
    )(*args, *extra)
    return list(res[:n_out]), list(res[n_out:])


def _prenorm_inproj(h, gain, w_in_t, name, ride=None):
    T = h.shape[0]

    def body(h_ref, g_ref, w_ref, q_ref, k_ref, v_ref, ug_ref, hn_ref):
        hv = h_ref[...]
        r = lax.rsqrt(jnp.mean(hv * hv, axis=-1, keepdims=True) + EPS)
        hn = (hv * r * g_ref[...]).astype(BF16)
        hn_ref[...] = hn
        for j in range(N_CHUNK):
            u = _nt(hn, w_ref[j * CHUNK:(j + 1) * CHUNK, :])
            if j == 0:
                q_ref[...] = (u * (HEAD_DIM ** -0.5)).astype(BF16)
            elif j == 1:
                k_ref[...] = u.astype(BF16)
            elif j == 2:
                v_ref[...] = u.astype(BF16)
            else:
                ug_ref[:, (j - 3) * CHUNK:(j - 2) * CHUNK] = u

    act = jax.ShapeDtypeStruct((T, CHUNK), BF16)
    return _call(
        body, name=name, grid=(T // TM,),
        in_specs=[_rows(TM, D_MODEL), _whole((1, D_MODEL)), _whole((N_CHUNK * CHUNK, D_MODEL))],
        out_specs=[_rows(TM, CHUNK)] * 3 + [_rows(TM, 4 * CHUNK), _rows(TM, D_MODEL)],
        out_shape=[act, act, act, jax.ShapeDtypeStruct((T, 4 * CHUNK), F32),
                   jax.ShapeDtypeStruct((T, D_MODEL), BF16)],
        args=(h, gain, w_in_t,), ride=ride)


def _softplus_parts(z):
    e = jnp.exp(-jnp.abs(z))
    return e, jnp.maximum(z, 0.0) + jnp.log(1.0 + e)


def _attn_fwd(qs, k, v, tri, name, ride=None):
    T = qs.shape[0]
    assert T // BLK <= FIRST_BLOCK_LANE, "one lane per key block below the lane of the first block"
    width = LANES * ATT_COLS
    chains = [(c, half) for c in range(ATT_COLS) for half in range(2)]

    def body(q_ref, k_ref, v_ref, m_ref, o_ref, cs_ref):
        qi = pl.program_id(1)
        lane = lax.broadcasted_iota(jnp.int32, (BLK, LANES), 1)
        first = lane < HEAD_DIM
        causal = (lax.broadcasted_iota(jnp.int32, (BLK, BLK), 1)
                  < lax.broadcasted_iota(jnp.int32, (BLK, BLK), 0))
        tri_m = m_ref[...]
        qh = {}
        for c in range(ATT_COLS):
            q = q_ref[:, c * LANES:(c + 1) * LANES]
            zero = jnp.zeros_like(q)
            qh[c, 0], qh[c, 1] = jnp.where(first, q, zero), jnp.where(first, zero, q)

        def step(kb, state, masked):
            carries, accs, cvals = state
            start = pl.multiple_of(kb * BLK, BLK)
            kblk = [k_ref[pl.ds(start, BLK), c * LANES:(c + 1) * LANES] for c in range(ATT_COLS)]
            vblk = [v_ref[pl.ds(start, BLK), c * LANES:(c + 1) * LANES] for c in range(ATT_COLS)]
            z = [_nt(qh[ch], kblk[ch[0]]) for ch in chains]
            sp = [_softplus_parts(zi)[1] for zi in z]
            if masked:
                sp = [jnp.where(causal, s, 0.0) for s in sp]
            incl = [_dot_hilo(s, tri_m) for s in sp]
            a = [jnp.exp(zi - ii - ci) for zi, ii, ci in zip(z, incl, carries)]
            if masked:
                a = [jnp.where(causal, ai, 0.0) for ai in a]
            accs, cvals = list(accs), list(cvals)
            for n, (c, half) in enumerate(chains):
                zero = jnp.zeros_like(vblk[c])
                vh = jnp.where(first, vblk[c], zero) if half == 0 else jnp.where(first, zero, vblk[c])
                accs[c] = accs[c] + _nn(a[n].astype(BF16), vh)
                cvals[c] = jnp.where(lane == kb + HEAD_DIM * half, carries[n], cvals[c])
            carries = tuple(ci + ii[:, 0:1] for ci, ii in zip(carries, incl))
            return carries, tuple(accs), tuple(cvals)

        zeros = tuple(jnp.zeros((BLK, LANES), F32) for _ in range(ATT_COLS))
        state = (tuple(jnp.zeros((BLK, 1), F32) for _ in chains), zeros, zeros)
        state = step(qi, state, True)

        def reaches_further(st):
            it, (carries, _, _) = st
            least = functools.reduce(jnp.minimum, carries)
            return jnp.logical_and(it < qi, jnp.min(least) < DEAD_AT)

        done, state = lax.while_loop(reaches_further, lambda st: (st[0] + 1, step(qi - 1 - st[0], st[1], False)),
                                     (jnp.int32(0), state))
        first_block = (qi - done).astype(F32)
        for c in range(ATT_COLS):
            o_ref[:, c * LANES:(c + 1) * LANES] = state[1][c]
            cs_ref[:, c * LANES:(c + 1) * LANES] = jnp.where(lane == FIRST_BLOCK_LANE, first_block, state[2][c])

    blk = pl.BlockSpec((BLK, width), lambda j, i: (i, j))
    col = pl.BlockSpec((T, width), lambda j, i: (0, j))
    out = jax.ShapeDtypeStruct((T, ATTN_DIM), F32)
    return _call(
        body, name=name, grid=(ATTN_DIM // width, T // BLK),
        in_specs=[blk, col, col, _whole((BLK, BLK))],
        out_specs=[blk, blk], out_shape=[out, out],
        args=(qs, k, v, tri,), ride=ride)


def _shifted_copies(pad_ref, sh_ref):
    rows = sh_ref.shape[1]
    for b in range(SUBLANES):
        sh_ref[b] = pad_ref[b:b + rows, :]


def _shift_of(offset):
    return offset % SUBLANES, offset - offset % SUBLANES


def _conv_fwd(ug, dw_w, dw_b, ln_g, ln_b, name, ride=None):
    T = ug.shape[0]
    per = TM // HALO

    def body(cv_ref, cg_ref, cvh_ref, cgh_ref, w_ref, b_ref, g_ref, beta_ref, conv_ref, c2_ref, pad_ref, sh_ref):
        i = pl.program_id(0)
        halo = cvh_ref[...] * _sigmoid(cgh_ref[...])
        pad_ref[0:HALO, :] = jnp.where(i == 0, 0.0, halo)
        pad_ref[HALO:HALO + TM, :] = cv_ref[...] * _sigmoid(cg_ref[...])
        pad_ref[HALO + TM:, :] = jnp.zeros((SUBLANES, CONV_DIM), F32)
        _shifted_copies(pad_ref, sh_ref)
        taps = [w_ref[t:t + 1, :] for t in range(CONV_WIDTH)]

        def rows(j, _):
            r = pl.multiple_of(j * CONV_ROWS, CONV_ROWS)
            acc = jnp.zeros((CONV_ROWS, CONV_DIM), F32) + b_ref[...]
            for t in range(CONV_WIDTH):
                b, a = _shift_of(HALO - (CONV_WIDTH - 1) + t)
                acc = acc + taps[t] * sh_ref[b, pl.ds(r + a, CONV_ROWS), :]
            conv_ref[pl.ds(r, CONV_ROWS), :] = acc
            return 0

        lax.fori_loop(0, TM // CONV_ROWS, rows, 0)
        acc = conv_ref[...]
        mu = jnp.mean(acc, axis=-1, keepdims=True)
        xc = acc - mu
        rs = lax.rsqrt(jnp.mean(xc * xc, axis=-1, keepdims=True) + EPS)
        ln = xc * rs * g_ref[...] + beta_ref[...]
        c2_ref[...] = (ln * _sigmoid(ln)).astype(BF16)

    prev = lambda col: pl.BlockSpec((HALO, CHUNK), lambda i: (jnp.maximum(i * per - 1, 0), col))
    vec = _whole((1, CONV_DIM))
    return _call(
        body, name=name, grid=(T // TM,),
        in_specs=[_rows(TM, CHUNK, 1), _rows(TM, CHUNK, 2), prev(1), prev(2),
                  _whole((CONV_WIDTH, CONV_DIM)), vec, vec, vec],
        out_specs=[_rows(TM, CONV_DIM), _rows(TM, CONV_DIM)],
        out_shape=[jax.ShapeDtypeStruct((T, CONV_DIM), F32), jax.ShapeDtypeStruct((T, CONV_DIM), BF16)],
        scratch_shapes=[pltpu.VMEM((TM + HALO + SUBLANES, CONV_DIM), F32),
                        pltpu.VMEM((SUBLANES, TM + HALO, CONV_DIM), F32)],
        args=(ug, ug, ug, ug, dw_w, dw_b, ln_g, ln_b,), ride=ride)


def _mix_out_ple(o, ug, c2, h, p, head_mean, g_attn, g_conv, g_ple, w_pw, w_out, w_gate, w_ple, name, ride=None):
    T = h.shape[0]

    def body(o_ref, ga_ref, gc_ref, c2_ref, h_ref, p_ref, hm_ref, gao_ref, gco_ref, gpn_ref,
             wpw_ref, wout_ref, wg_ref, wple_ref,
             h2_ref, h1_ref, ycat_ref, hn2_ref, gate_ref, e_ref, c3_ref):
        ov = o_ref[...]
        rh = lax.rsqrt(_nn((ov * ov).astype(BF16), hm_ref[...]) + EPS)
        ga = ga_ref[...]
        ya = (ov * rh * gao_ref[...] * (ga * _sigmoid(ga))).astype(BF16)
        c3 = _nn(c2_ref[...], wpw_ref[...])
        c3_ref[...] = c3
        rc = lax.rsqrt(jnp.mean(c3 * c3, axis=-1, keepdims=True) + EPS)
        gc = gc_ref[...]
        yc = (c3 * rc * gco_ref[...] * (gc * _sigmoid(gc))).astype(BF16)
        ycat_ref[:, :ATTN_DIM] = ya
        ycat_ref[:, ATTN_DIM:] = yc
        h1 = h_ref[...] + _nn(ya, wout_ref[:ATTN_DIM, :]) + _nn(yc, wout_ref[ATTN_DIM:, :])
        h1_ref[...] = h1
        r1 = lax.rsqrt(jnp.mean(h1 * h1, axis=-1, keepdims=True) + EPS)
        hn2 = (h1 * r1 * gpn_ref[...]).astype(BF16)
        hn2_ref[...] = hn2
        gate = _sigmoid(_nn(hn2, wg_ref[...]))
        e = _nn(p_ref[...].astype(BF16), wple_ref[...])
        gate_ref[...] = gate
        e_ref[...] = e
        h2_ref[...] = h1 + e * gate

    f32 = lambda cols: jax.ShapeDtypeStruct((T, cols), F32)
    bf = lambda cols: jax.ShapeDtypeStruct((T, cols), BF16)
    return _call(
        body, name=name, grid=(T // TM,),
        in_specs=[_rows(TM, ATTN_DIM), _rows(TM, CHUNK, 0), _rows(TM, CHUNK, 3), _rows(TM, CONV_DIM),
                  _rows(TM, D_MODEL), _rows(TM, PLE_DIM), _whole((ATTN_DIM, ATTN_DIM)),
                  _whole((1, ATTN_DIM)), _whole((1, CONV_DIM)), _whole((1, D_MODEL)),
                  _whole((CONV_DIM, CONV_DIM)), _whole((D_MODEL, D_MODEL)), _whole((D_MODEL, D_MODEL)),
                  _whole((PLE_DIM, D_MODEL))],
        out_specs=[_rows(TM, D_MODEL), _rows(TM, D_MODEL), _rows(TM, D_MODEL), _rows(TM, D_MODEL),
                   _rows(TM, D_MODEL), _rows(TM, D_MODEL), _rows(TM, CONV_DIM)],
        out_shape=[f32(D_MODEL), f32(D_MODEL), bf(D_MODEL), bf(D_MODEL), f32(D_MODEL), f32(D_MODEL),
                   f32(CONV_DIM)],
        args=(o, ug, ug, c2, h, p, head_mean, g_attn, g_conv, g_ple, w_pw, w_out, w_gate, w_ple,), ride=ride)


def _final_loss(h, target, gain, name):
    T = h.shape[0]

    def body(h_ref, t_ref, g_ref, dh_ref, gsum_ref, loss_ref):
        @pl.when(pl.program_id(0) == 0)
        def _():
            gsum_ref[...] = jnp.zeros_like(gsum_ref)
            loss_ref[...] = jnp.zeros_like(loss_ref)

        hv = h_ref[...]
        r = lax.rsqrt(jnp.mean(hv * hv, axis=-1, keepdims=True) + EPS)
        xh = hv * r
        diff = xh * g_ref[...] - t_ref[...]
        loss_ref[...] += 0.5 * jnp.sum(jnp.mean(diff * diff, axis=-1, keepdims=True), axis=0, keepdims=True)
        dy = diff * (1.0 / D_MODEL)
        gsum_ref[...] += jnp.sum(dy * xh, axis=0, keepdims=True)
        dxh = dy * g_ref[...]
        dh_ref[...] = r * (dxh - xh * jnp.mean(dxh * xh, axis=-1, keepdims=True))

    return pl.pallas_call(
        body, name=name, grid=(T // TM,),
        in_specs=[_rows(TM, D_MODEL), _rows(TM, D_MODEL), _whole((1, D_MODEL))],
        out_specs=[_rows(TM, D_MODEL), _whole((1, D_MODEL)), _whole((1, LANES))],
        out_shape=[jax.ShapeDtypeStruct((T, D_MODEL), F32), jax.ShapeDtypeStruct((1, D_MODEL), F32),
                   jax.ShapeDtypeStruct((1, LANES), F32)],
        compiler_params=_params(1),
    )(h, target, gain)


def _ple_out_bwd(dh2, gate, e, h1, g_ple, w_gate, w_out, name, ride=None):
    T = dh2.shape[0]

    def body(dh2_ref, gate_ref, e_ref, h1_ref, gpn_ref, wg_ref, wout_ref,
             dh1_ref, dh1b_ref, dzg_ref, de_ref, dycat_ref, gsum_ref):
        @pl.when(pl.program_id(0) == 0)
        def _():
            gsum_ref[...] = jnp.zeros_like(gsum_ref)

        dh2v = dh2_ref[...]
        gate = gate_ref[...]
        de_ref[...] = (dh2v * gate).astype(BF16)
        dzg = (dh2v * e_ref[...] * gate * (1.0 - gate)).astype(BF16)
        dzg_ref[...] = dzg
        dhn2 = _nt(dzg, wg_ref[...])
        h1 = h1_ref[...]
        r1 = lax.rsqrt(jnp.mean(h1 * h1, axis=-1, keepdims=True) + EPS)
        xh = h1 * r1
        gsum_ref[...] += jnp.sum(dhn2 * xh, axis=0, keepdims=True)
        dxh = dhn2 * gpn_ref[...]
        dh1 = dh2v + r1 * (dxh - xh * jnp.mean(dxh * xh, axis=-1, keepdims=True))
        dh1_ref[...] = dh1
        dh1b = dh1.astype(BF16)
        dh1b_ref[...] = dh1b
        dycat_ref[...] = _nt(dh1b, wout_ref[...])

    f32 = jax.ShapeDtypeStruct((T, D_MODEL), F32)
    bf = jax.ShapeDtypeStruct((T, D_MODEL), BF16)
    full = _rows(TM, D_MODEL)
    return _call(
        body, name=name, grid=(T // TM,),
        in_specs=[full, full, full, full, _whole((1, D_MODEL)), _whole((D_MODEL, D_MODEL)),
                  _whole((D_MODEL, D_MODEL))],
        out_specs=[full, full, full, full, full, _whole((1, D_MODEL))],
        out_shape=[f32, bf, bf, bf, f32, jax.ShapeDtypeStruct((1, D_MODEL), F32)],
        args=(dh2, gate, e, h1, g_ple, w_gate, w_out,), ride=ride)


def _branch_bwd(dycat, o, ug, c3, conv, head_mean, g_attn, g_conv, ln_g, ln_b, w_pw, name, ride=None):
    T = o.shape[0]

    def body(dya_ref, dyc_ref, o_ref, ga_ref, gc_ref, c3_ref, conv_ref, hm_ref, gao_ref, gco_ref,
             lng_ref, lnb_ref, wpw_ref,
             do_ref, dga_ref, dgc_ref, dc3_ref, dconv_ref, sums_ref):
        @pl.when(pl.program_id(0) == 0)
        def _():
            sums_ref[...] = jnp.zeros_like(sums_ref)

        hm = hm_ref[...]
        col = lambda x: jnp.sum(x, axis=0, keepdims=True)
        ov = o_ref[...]
        rh = lax.rsqrt(_nn((ov * ov).astype(BF16), hm) + EPS)
        xh = ov * rh
        ga = ga_ref[...]
        sg = _sigmoid(ga)
        dya = dya_ref[...]
        don = dya * (ga * sg)
        dga_ref[...] = (dya * xh * gao_ref[...] * _dsilu(ga, sg)).astype(BF16)
        sums_ref[0:1, :] += col(don * xh)
        dxh = don * gao_ref[...]
        do_ref[...] = (rh * (dxh - xh * _dot_hilo(dxh * xh, hm))).astype(BF16)
        c3 = c3_ref[...]
        rc = lax.rsqrt(jnp.mean(c3 * c3, axis=-1, keepdims=True) + EPS)
        xh3 = c3 * rc
        gc = gc_ref[...]
        sgc = _sigmoid(gc)
        dyc = dyc_ref[...]
        dn3 = dyc * (gc * sgc)
        dgc_ref[...] = (dyc * xh3 * gco_ref[...] * _dsilu(gc, sgc)).astype(BF16)
        sums_ref[1:2, :] += col(dn3 * xh3)
        dxh3 = dn3 * gco_ref[...]
        dc3 = (rc * (dxh3 - xh3 * jnp.mean(dxh3 * xh3, axis=-1, keepdims=True))).astype(BF16)
        dc3_ref[...] = dc3
        dc2 = _nt(dc3, wpw_ref[...])
        cv = conv_ref[...]
        mu = jnp.mean(cv, axis=-1, keepdims=True)
        xc = cv - mu
        rs = lax.rsqrt(jnp.mean(xc * xc, axis=-1, keepdims=True) + EPS)
        xn = xc * rs
        ln = xn * lng_ref[...] + lnb_ref[...]
        dln = dc2 * _dsilu(ln, _sigmoid(ln))
        sums_ref[2:3, :] += col(dln * xn)
        sums_ref[3:4, :] += col(dln)
        dxn = dln * lng_ref[...]
        dconv = rs * (dxn - jnp.mean(dxn, axis=-1, keepdims=True)
                      - xn * jnp.mean(dxn * xn, axis=-1, keepdims=True))
        dconv_ref[...] = dconv
        sums_ref[4:5, :] += col(dconv)

    half = lambda dt: jax.ShapeDtypeStruct((T, CHUNK), dt)
    tile = _rows(TM, CHUNK)
    vec = _whole((1, CHUNK))
    return _call(
        body, name=name, grid=(T // TM,),
        in_specs=[_rows(TM, CHUNK, 0), _rows(TM, CHUNK, 1), tile, _rows(TM, CHUNK, 0), _rows(TM, CHUNK, 3),
                  tile, tile, _whole((ATTN_DIM, ATTN_DIM)), vec, vec, vec, vec, _whole((CONV_DIM, CONV_DIM))],
        out_specs=[tile, tile, tile, tile, tile, _whole((8, CHUNK))],
        out_shape=[half(BF16), half(BF16), half(BF16), half(BF16), half(F32),
                   jax.ShapeDtypeStruct((8, CHUNK), F32)],
        args=(dycat, dycat, o, ug, ug, c3, conv, head_mean, g_attn, g_conv, ln_g, ln_b, w_pw,), ride=ride)


def _conv_bwd(dconv, ug, dw_w, name, ride=None):
    T = dconv.shape[0]
    per = TM // HALO
    last = T // HALO - 1
    n_tiles = T // TM

    def body(d_ref, dn_ref, cv_ref, cg_ref, cvh_ref, cgh_ref, w_ref, dcv_ref, dcg_ref, dw_ref,
             dpad_ref, cpad_ref, dsh_ref, csh_ref, dw_acc):
        i = pl.program_id(0)

        @pl.when(i == 0)
        def _():
            dw_acc[...] = jnp.zeros_like(dw_acc)

        tail = jnp.zeros((SUBLANES, CONV_DIM), F32)
        dpad_ref[0:TM, :] = d_ref[...]
        dpad_ref[TM:TM + HALO, :] = jnp.where(i == n_tiles - 1, 0.0, dn_ref[...])
        dpad_ref[TM + HALO:, :] = tail
        halo = cvh_ref[...] * _sigmoid(cgh_ref[...])
        cpad_ref[0:HALO, :] = jnp.where(i == 0, 0.0, halo)
        cpad_ref[HALO:HALO + TM, :] = cv_ref[...] * _sigmoid(cg_ref[...])
        cpad_ref[HALO + TM:, :] = tail
        _shifted_copies(dpad_ref, dsh_ref)
        _shifted_copies(cpad_ref, csh_ref)
        taps = [w_ref[t:t + 1, :] for t in range(CONV_WIDTH)]

        def rows(j, _):
            r = pl.multiple_of(j * CONV_ROWS, CONV_ROWS)
            d = d_ref[pl.ds(r, CONV_ROWS), :]
            dc = jnp.zeros((CONV_ROWS, CONV_DIM), F32)
            for t in range(CONV_WIDTH):
                b, a = _shift_of(CONV_WIDTH - 1 - t)
                dc = dc + taps[t] * dsh_ref[b, pl.ds(r + a, CONV_ROWS), :]
                b, a = _shift_of(HALO - (CONV_WIDTH - 1) + t)
                prod = d * csh_ref[b, pl.ds(r + a, CONV_ROWS), :]
                dw_acc[t] += jnp.sum(prod.reshape(CONV_ROWS // SUBLANES, SUBLANES, CONV_DIM), axis=0)
            cv = cv_ref[pl.ds(r, CONV_ROWS), :]
            sg = _sigmoid(cg_ref[pl.ds(r, CONV_ROWS), :])
            dcv_ref[pl.ds(r, CONV_ROWS), :] = (dc * sg).astype(BF16)
            dcg_ref[pl.ds(r, CONV_ROWS), :] = (dc * cv * sg * (1.0 - sg)).astype(BF16)
            return 0

        lax.fori_loop(0, TM // CONV_ROWS, rows, 0)

        @pl.when(i == n_tiles - 1)
        def _():
            dw_ref[...] = jnp.zeros_like(dw_ref)
            for t in range(CONV_WIDTH):
                dw_ref[t:t + 1, :] = jnp.sum(dw_acc[t], axis=0, keepdims=True)

    prev = lambda col: pl.BlockSpec((HALO, CHUNK), lambda i: (jnp.maximum(i * per - 1, 0), col))
    nxt = pl.BlockSpec((HALO, CONV_DIM), lambda i: (jnp.minimum((i + 1) * per, last), 0))
    half = jax.ShapeDtypeStruct((T, CHUNK), BF16)
    return _call(
        body, name=name, grid=(T // TM,),
        in_specs=[_rows(TM, CONV_DIM), nxt, _rows(TM, CHUNK, 1), _rows(TM, CHUNK, 2), prev(1), prev(2),
                  _whole((CONV_WIDTH, CONV_DIM))],
        out_specs=[_rows(TM, CHUNK), _rows(TM, CHUNK), _whole((HALO, CONV_DIM))],
        out_shape=[half, half, jax.ShapeDtypeStruct((HALO, CONV_DIM), F32)],
        scratch_shapes=[pltpu.VMEM((TM + HALO + SUBLANES, CONV_DIM), F32),
                        pltpu.VMEM((TM + HALO + SUBLANES, CONV_DIM), F32),
                        pltpu.VMEM((SUBLANES, TM + HALO, CONV_DIM), F32),
                        pltpu.VMEM((SUBLANES, TM + HALO, CONV_DIM), F32),
                        pltpu.VMEM((HALO, SUBLANES, CONV_DIM), F32)],
        args=(dconv, dconv, ug, ug, ug, ug, dw_w,), ride=ride)


def _attn_bwd(qs, k, v, do, cs, tri, tri_t, name, ride=None):
    T = qs.shape[0]
    nq = T // BLK
    width = LANES * ATT_COLS
    chains = [(c, half) for c in range(ATT_COLS) for half in range(2)]

    def body(q_ref, k_ref, v_ref, do_ref, cs_ref, m_ref, mt_ref, dq_ref, dk_ref, dv_ref, dk_acc, dv_acc):
        qi = pl.program_id(1)

        @pl.when(qi == 0)
        def _():
            dk_acc[...] = jnp.zeros_like(dk_acc)
            dv_acc[...] = jnp.zeros_like(dv_acc)

        lane = lax.broadcasted_iota(jnp.int32, (BLK, LANES), 1)
        first = lane < HEAD_DIM
        causal = (lax.broadcasted_iota(jnp.int32, (BLK, BLK), 1)
                  < lax.broadcasted_iota(jnp.int32, (BLK, BLK), 0))
        tri_m = m_ref[...]
        tri_mt = mt_ref[...]

        def halves(x):
            zero = jnp.zeros_like(x)
            return jnp.where(first, x, zero), jnp.where(first, zero, x)

        qh, doh, cs = {}, {}, []
        for c in range(ATT_COLS):
            qh[c, 0], qh[c, 1] = halves(q_ref[:, c * LANES:(c + 1) * LANES])
            doh[c, 0], doh[c, 1] = halves(do_ref[:, c * LANES:(c + 1) * LANES])
            cs.append(cs_ref[:, c * LANES:(c + 1) * LANES])

        def step(kb, state, masked):
            prefixes, dq_accs = state
            start = pl.multiple_of(kb * BLK, BLK)
            kblk = [k_ref[pl.ds(start, BLK), c * LANES:(c + 1) * LANES] for c in range(ATT_COLS)]
            vblk = [v_ref[pl.ds(start, BLK), c * LANES:(c + 1) * LANES] for c in range(ATT_COLS)]
            z = [_nt(qh[ch], kblk[ch[0]]) for ch in chains]
            da = [_nt(doh[ch], vblk[ch[0]]) for ch in chains]
            parts = [_softplus_parts(zi) for zi in z]
            sp = [pt[1] for pt in parts]
            if masked:
                sp = [jnp.where(causal, s, 0.0) for s in sp]
            incl = [_dot_hilo(s, tri_m) for s in sp]
            carries = [jnp.sum(jnp.where(lane == kb + HEAD_DIM * half, cs[c], 0.0), axis=1, keepdims=True)
                       for c, half in chains]
            a = [jnp.exp(zi - ii - ci) for zi, ii, ci in zip(z, incl, carries)]
            if masked:
                a = [jnp.where(causal, ai, 0.0) for ai in a]
            w = [ai * di for ai, di in zip(a, da)]
            pinc = [_nn(wi.astype(BF16), tri_mt) for wi in w]
            beta = [jnp.where(zi >= 0.0, 1.0, pt[0]) / (1.0 + pt[0]) for zi, pt in zip(z, parts)]
            dz = [wi - bi * (pi + pre) for wi, bi, pi, pre in zip(w, beta, pinc, prefixes)]
            if masked:
                dz = [jnp.where(causal, d, 0.0) for d in dz]
            dq_accs = list(dq_accs)
            for c in range(ATT_COLS):
                k0, k1 = halves(kblk[c])
                dz0, dz1 = dz[2 * c].astype(BF16), dz[2 * c + 1].astype(BF16)
                a0, a1 = a[2 * c].astype(BF16), a[2 * c + 1].astype(BF16)
                dq_accs[c] = dq_accs[c] + _nn(dz0, k0) + _nn(dz1, k1)
                dk_acc[pl.ds(start, BLK), c * LANES:(c + 1) * LANES] += _tn(dz0, qh[c, 0]) + _tn(dz1, qh[c, 1])
                dv_acc[pl.ds(start, BLK), c * LANES:(c + 1) * LANES] += _tn(a0, doh[c, 0]) + _tn(a1, doh[c, 1])
            prefixes = tuple(pre + pi[:, BLK - 1:BLK] for pre, pi in zip(prefixes, pinc))
            return prefixes, tuple(dq_accs)

        state = (tuple(jnp.zeros((BLK, 1), F32) for _ in chains),
                 tuple(jnp.zeros((BLK, LANES), F32) for _ in range(ATT_COLS)))
        first_block = jnp.max(jnp.where(lane == FIRST_BLOCK_LANE, cs[0], 0.0)).astype(jnp.int32)
        state = lax.fori_loop(first_block, qi, lambda kb, st: step(kb, st, False), state)
        state = step(qi, state, True)
        for c in range(ATT_COLS):
            dq_ref[:, c * LANES:(c + 1) * LANES] = (state[1][c] * (HEAD_DIM ** -0.5)).astype(BF16)

        @pl.when(qi == nq - 1)
        def _():
            dk_ref[...] = dk_acc[...].astype(BF16)
            dv_ref[...] = dv_acc[...].astype(BF16)

    blk = pl.BlockSpec((BLK, width), lambda j, i: (i, j))
    col = pl.BlockSpec((T, width), lambda j, i: (0, j))
    out = jax.ShapeDtypeStruct((T, ATTN_DIM), BF16)
    return _call(
        body, name=name, grid=(ATTN_DIM // width, nq),
        in_specs=[blk, col, col, blk, blk, _whole((BLK, BLK)), _whole((BLK, BLK))],
        out_specs=[blk, col, col], out_shape=[out, out, out],
        scratch_shapes=[pltpu.VMEM((T, width), F32), pltpu.VMEM((T, width), F32)],
        args=(qs, k, v, do, cs, tri, tri_t,), ride=ride)


def _inproj_bwd(du, w_in_t, h, dh1, gain, name, ride=None):
    T = h.shape[0]

    def body(*refs):
        du_refs = refs[:N_CHUNK]
        w_ref, h_ref, dh1_ref, g_ref, dh_ref, gsum_ref = refs[N_CHUNK:]

        @pl.when(pl.program_id(0) == 0)
        def _():
            gsum_ref[...] = jnp.zeros_like(gsum_ref)

        dhn = jnp.zeros((TM, D_MODEL), F32)
        for j in range(N_CHUNK):
            dhn = dhn + _nn(du_refs[j][...], w_ref[j * CHUNK:(j + 1) * CHUNK, :])
        hv = h_ref[...]
        r = lax.rsqrt(jnp.mean(hv * hv, axis=-1, keepdims=True) + EPS)
        xh = hv * r
        gsum_ref[...] += jnp.sum(dhn * xh, axis=0, keepdims=True)
        dxh = dhn * g_ref[...]
        dh_ref[...] = dh1_ref[...] + r * (dxh - xh * jnp.mean(dxh * xh, axis=-1, keepdims=True))

    full = _rows(TM, D_MODEL)
    return _call(
        body, name=name, grid=(T // TM,),
        in_specs=[_rows(TM, CHUNK)] * N_CHUNK + [_whole((N_CHUNK * CHUNK, D_MODEL)), full, full,
                                                 _whole((1, D_MODEL))],
        out_specs=[full, _whole((1, D_MODEL))],
        out_shape=[jax.ShapeDtypeStruct((T, D_MODEL), F32), jax.ShapeDtypeStruct((1, D_MODEL), F32)],
        args=(*du, w_in_t, h, dh1, gain), ride=ride)


def _weight_grad(lhs_list, rhs, name, tk=CHUNK, ride=None):
    T, n_rhs = rhs.shape
    n = len(lhs_list)
    ka = lhs_list[0].shape[1]
    per = ka // tk

    def body(*refs):
        a_refs, b_ref, out_ref = refs[:n], refs[n], refs[n + 1]
        step = pl.program_id(0)
        for j in range(n):
            for s in range(per):
                @pl.when(step == j * per + s)
                def _(j=j, s=s):
                    out_ref[...] = _tn(a_refs[j][:, s * tk:(s + 1) * tk], b_ref[...]).astype(BF16)

    (grad,), landed = _call(
        body, name=name, grid=(n * per,),
        in_specs=[_whole((T, ka))] * n + [_whole((T, n_rhs))],
        out_specs=[pl.BlockSpec((tk, n_rhs), lambda i: (i, 0))],
        out_shape=[jax.ShapeDtypeStruct((n * ka, n_rhs), BF16)],
        args=(*lhs_list, rhs), ride=ride)
    return grad, landed


def _adamw_update(w, g, m, v):
    nm = ADAM_B1 * m + (1.0 - ADAM_B1) * g
    nv = ADAM_B2 * v + (1.0 - ADAM_B2) * (g * g)
    m_hat = nm / (1.0 - ADAM_B1 ** ADAM_STEP)
    v_hat = nv / (1.0 - ADAM_B2 ** ADAM_STEP)
    return -ADAM_LR * (m_hat / (jnp.sqrt(v_hat) + ADAM_EPS) + ADAM_WD * w), nm, nv


def _sum_adamw(slots, w, m, v, name):
    depth, R, C = w.shape
    tr = min(R, ADAMW_ROWS)

    def body(*refs):
        slot_refs, (w_ref, m_ref, v_ref, g_ref, d_ref, nm_ref, nv_ref) = refs[:depth], refs[depth:]
        for layer in range(depth):
            @pl.when(pl.program_id(0) == layer)
            def _(src=slot_refs[layer]):
                g = src[0].astype(F32)
                for s in range(1, N_DEV):
                    g = g + src[s].astype(F32)
                g_ref[0] = g
                d_ref[0], nm_ref[0], nv_ref[0] = _adamw_update(w_ref[0], g, m_ref[0], v_ref[0])

    slot_spec = lambda layer: pl.BlockSpec((N_DEV, tr, C), lambda l, i: (0, jnp.where(l == layer, i, 0), 0))
    spec = pl.BlockSpec((1, tr, C), lambda l, i: (l, i, 0))
    out = jax.ShapeDtypeStruct((depth, R, C), F32)
    return pl.pallas_call(
        body, name=name, grid=(depth, R // tr),
        in_specs=[slot_spec(layer) for layer in range(depth)] + [spec] * 3,
        out_specs=[spec] * 4, out_shape=[out] * 4,
        compiler_params=_params(2),
    )(*slots, w, m, v)


def _adamw(w, g, m, v, name):
    R, C = w.shape
    tr = R
    for cand in (512, 256, 128, 64):
        if R % cand == 0 and R > cand:
            tr = cand
            break

    def body(w_ref, g_ref, m_ref, v_ref, d_ref, nm_ref, nv_ref):
        d_ref[...], nm_ref[...], nv_ref[...] = _adamw_update(w_ref[...], g_ref[...], m_ref[...], v_ref[...])

    spec = pl.BlockSpec((tr, C), lambda i: (i, 0))
    out = jax.ShapeDtypeStruct((R, C), F32)
    return pl.pallas_call(
        body, name=name, grid=(R // tr,),
        in_specs=[spec] * 4, out_specs=[spec] * 3, out_shape=[out, out, out],
        compiler_params=_params(1),
    )(w, g, m, v)


def _pack_small(values, scalar=None):
    pad = lambda a: jnp.pad(a, ((0, 0), (0, D_MODEL - a.shape[1])))
    last = jnp.zeros((1, D_MODEL), F32) if scalar is None else pad(scalar.reshape(1, 1))
    return jnp.concatenate([pad(values[name].reshape(rows, cols)) for name, _, rows, cols in SMALL_LAYOUT] + [last],
                           axis=0)


def _small_update(all_packs, w, m, v, name):
    def body(packs_ref, w_ref, m_ref, v_ref, *outs):
        total = packs_ref[0]
        for s in range(1, N_DEV):
            total = total + packs_ref[s]
        row = lax.broadcasted_iota(jnp.int32, (SMALL_ROWS, D_MODEL), 0)
        g = jnp.where(row == LOSS_ROW, 0.0, total[:SMALL_ROWS])
        kinds = (g,) + _adamw_update(w_ref[...], g, m_ref[...], v_ref[...])
        for k, val in enumerate(kinds):
            for j, (_, at, rows, cols) in enumerate(SMALL_LAYOUT):
                outs[k * len(SMALL_LAYOUT) + j][...] = val[at:at + rows, :cols]
        outs[-2][...] = total[LOSS_ROW:LOSS_ROW + 1, :LANES]
        outs[-1][...] = total[SMALL_ROWS:, :]

    shapes = [jax.ShapeDtypeStruct((rows, cols), F32) for _, _, rows, cols in SMALL_LAYOUT] * 4
    shapes += [jax.ShapeDtypeStruct((1, LANES), F32), jax.ShapeDtypeStruct((PACK_ROWS - SMALL_ROWS, D_MODEL), F32)]
    res = pl.pallas_call(body, name=name, out_shape=shapes, compiler_params=_params())(all_packs, w, m, v)
    n = len(SMALL_LAYOUT)
    per_name = {item[0]: tuple(res[k * n + j] for k in range(4)) for j, item in enumerate(SMALL_LAYOUT)}
    return per_name, res[-2][0, 0], res[-1]


def kernel(x, p, norm_g, w_in, attn_out_g, dw_w, dw_b, conv_ln_g, conv_ln_b, w_pw, conv_out_g, w_out, ple_norm_g, w_ple_gate, w_ple, final_g, loss_target, m_norm_g, m_w_in, m_attn_out_g, m_dw_w, m_dw_b, m_conv_ln_g, m_conv_ln_b, m_w_pw, m_conv_out_g, m_w_out, m_ple_norm_g, m_w_ple_gate, m_w_ple, m_final_g, v_norm_g, v_w_in, v_attn_out_g, v_dw_w, v_dw_b, v_conv_ln_g, v_conv_ln_b, v_w_pw, v_conv_out_g, v_w_out, v_ple_norm_g, v_w_ple_gate, v_w_ple, v_final_g):
    depth = w_in.shape[0]
    T = x.shape[1]
    given = dict(
        norm_g=norm_g, ple_norm_g=ple_norm_g, final_g=final_g, dw_b=dw_b, conv_ln_g=conv_ln_g, conv_ln_b=conv_ln_b,
        conv_out_g=conv_out_g, attn_out_g=attn_out_g,
        m_norm_g=m_norm_g, m_ple_norm_g=m_ple_norm_g, m_final_g=m_final_g, m_dw_b=m_dw_b, m_conv_ln_g=m_conv_ln_g,
        m_conv_ln_b=m_conv_ln_b, m_conv_out_g=m_conv_out_g, m_attn_out_g=m_attn_out_g,
        v_norm_g=v_norm_g, v_ple_norm_g=v_ple_norm_g, v_final_g=v_final_g, v_dw_b=v_dw_b, v_conv_ln_g=v_conv_ln_g,
        v_conv_ln_b=v_conv_ln_b, v_conv_out_g=v_conv_out_g, v_attn_out_g=v_attn_out_g)
    my_idx = 4 * lax.axis_index("x") + 2 * lax.axis_index("y") + lax.axis_index("c")

    ids = jnp.arange(BLK)
    tri = (ids[:, None] >= ids[None, :]).astype(BF16)
    tri_t = (ids[:, None] <= ids[None, :]).astype(BF16)
    hid = jnp.arange(ATTN_DIM) // HEAD_DIM
    head_mean = ((hid[:, None] == hid[None, :]).astype(F32) / HEAD_DIM).astype(BF16)

    w_names = ("w_in_t", "w_pw", "w_out", "w_gate", "w_ple")
    w_axes = dict(zip(w_names, (0, 0, 0, 0, 1)))
    shards = [dict(zip(w_names, (w_in[l].T.astype(BF16), w_pw[l].astype(BF16), w_out[l].astype(BF16),
                                 w_ple_gate[l].astype(BF16), w_ple[l].astype(BF16)))) for l in range(depth)]
    first = _all_gather([shards[0][n] for n in w_names] + [dw_w[l].T for l in range(depth)],
                        [w_axes[n] for n in w_names] + [0] * depth, "gather_weights_0")
    layers = []
    for l in range(depth):
        layers.append(dict(
            dw_w=first[len(w_names) + l].T,
            g_norm=norm_g[l][None], g_attn=jnp.tile(attn_out_g[l], N_HEADS)[None], dw_b=dw_b[l][None],
            ln_g=conv_ln_g[l][None], ln_b=conv_ln_b[l][None], g_conv=conv_out_g[l][None],
            g_ple=ple_norm_g[l][None], p=p[l, 0]))
    layers[0].update(zip(w_names, first))

    h = x[0]
    saved = []
    for l, w in enumerate(layers):
        nxt = [None] if l + 1 < depth else []

        def quarter(i, l=l, nxt=nxt):
            return [_Ride.gather(shards[l + 1]["w_in_t"], 0, nxt[0], *W_IN_QUARTERS[i])] if nxt else []

        def others(where, l=l, w=w):
            plan = OTHER_WEIGHTS_ON[where] if l > 0 else ()
            return [_Ride.gather(shards[l][n], w_axes[n], w.get(n), *rows) for n, rows in plan]

        def keep(where, landed, l=l, w=w, nxt=nxt):
            plan = OTHER_WEIGHTS_ON[where] if l > 0 else ()
            nxt[:1] = landed[:len(landed) - len(plan)]
            w.update(zip([n for n, _ in plan], landed[len(landed) - len(plan):]))

        (qs, k, v, ug, hn), landed = _prenorm_inproj(h, w["g_norm"], w["w_in_t"], f"inproj_{l}",
                                                     _Ride(quarter(0) + others("inproj")))
        keep("inproj", landed)
        (o, cs), landed = _attn_fwd(qs, k, v, tri, f"attn_fwd_{l}", _Ride(quarter(1) + others("attn")))
        keep("attn", landed)
        (conv, c2), landed = _conv_fwd(ug, w["dw_w"], w["dw_b"], w["ln_g"], w["ln_b"], f"conv_fwd_{l}",
                                       _Ride(quarter(2) + others("conv")))
        keep("conv", landed)
        (h2, h1, ycat, hn2, gate, e, c3), landed = _mix_out_ple(
            o, ug, c2, h, w["p"], head_mean, w["g_attn"], w["g_conv"], w["g_ple"],
            w["w_pw"], w["w_out"], w["w_gate"], w["w_ple"], f"mix_{l}", _Ride(quarter(3)))
        if l + 1 < depth:
            layers[l + 1]["w_in_t"] = landed[0]
        saved.append(dict(h=h, qs=qs, k=k, v=v, ug=ug, hn=hn, o=o, cs=cs, conv=conv, c2=c2, h1=h1,
                          ycat=ycat, hn2=hn2, gate=gate, e=e, c3=c3))
        h = h2
    dh, g_final, loss_part = _final_loss(h, loss_target[0], final_g[None], "final_loss")

    small = {}
    dww_parts = [None] * depth
    slots = [dict() for _ in range(depth)]
    g_w_in = None
    for l in reversed(range(depth)):
        w, s = layers[l], saved[l]
        above = [None] if g_w_in is not None else []

        def part(i, above=above, g=g_w_in):
            return [_Ride.scatter(g, 0, above[0], *W_IN_GRAD_PARTS[i])] if above else []

        def scattered(grads, names):
            return [_Ride.scatter(grads[n], w_axes[n]) for n in names]

        (dh1, dh1b, dzg, de, dycat, g_ple_sum), landed = _ple_out_bwd(
            dh, s["gate"], s["e"], s["h1"], w["g_ple"], w["w_gate"], w["w_out"], f"ple_bwd_{l}", _Ride(part(0)))
        above[:1] = landed
        (do, dga, dgc, dc3, dconv, sums), landed = _branch_bwd(
            dycat, s["o"], s["ug"], s["c3"], s["conv"], head_mean, w["g_attn"], w["g_conv"],
            w["ln_g"], w["ln_b"], w["w_pw"], f"branch_bwd_{l}", _Ride(part(1)))
        above[:1] = landed
        grads = dict(
            w_pw=_weight_grad([s["c2"]], dc3, f"grad_w_pw_{l}")[0],
            w_out=_weight_grad([s["ycat"]], dh1b, f"grad_w_out_{l}")[0],
            w_gate=_weight_grad([s["hn2"]], dzg, f"grad_w_gate_{l}")[0],
            w_ple=_weight_grad([w["p"].astype(BF16)], de, f"grad_w_ple_{l}", tk=PLE_DIM)[0])
        (dcv, dcg, dww), landed = _conv_bwd(dconv, s["ug"], w["dw_w"], f"conv_bwd_{l}", _Ride(part(2)))
        above[:1] = landed
        (dq, dk, dv), landed = _attn_bwd(s["qs"], s["k"], s["v"], do, s["cs"], tri, tri_t, f"attn_bwd_{l}",
                                         _Ride(scattered(grads, w_names[1:])))
        slots[l].update(zip(w_names[1:], landed))
        du = [dq, dk, dv, dga, dcv, dcg, dgc]
        g_w_in_here, landed = _weight_grad(du, s["hn"], f"grad_w_in_{l}", ride=_Ride(part(3)))
        if above:
            slots[l + 1]["w_in_t"] = landed[0]
        tail = [_Ride.scatter(g_w_in_here, 0)] if l == 0 else []
        (dh, g_norm_sum), landed = _inproj_bwd(du, w["w_in_t"], s["h"], dh1, w["g_norm"], f"inproj_bwd_{l}",
                                               _Ride(tail))
        slots[l].update(zip(("w_in_t",), landed))
        g_w_in = g_w_in_here
        small[l] = dict(norm_g=g_norm_sum, ple_norm_g=g_ple_sum, attn_out_g=sums[0].reshape(N_HEADS, HEAD_DIM).sum(0),
                        conv_out_g=sums[1], conv_ln_g=sums[2], conv_ln_b=sums[3], dw_b=sums[4])
        dww_parts[l] = dww[:CONV_WIDTH]
    slots = [[sl[n] for n in w_names] for sl in slots]
    grad_x = dh[None]

    sums_of = {name: jnp.stack([small[l][name].reshape(-1) for l in range(depth)]) for name in small[0]}
    sums_of["final_g"] = g_final
    pack = jnp.concatenate([_pack_small(sums_of, scalar=loss_part[0, 0]), jnp.concatenate(dww_parts, axis=1),
                            jnp.zeros((PACK_ROWS - SMALL_ROWS - CONV_WIDTH, D_MODEL), F32)], axis=0)
    (all_packs,) = _all_gather([pack], [0], "gather_small_grads")
    packed = lambda pre: _pack_small({name: given[pre + name] for name, _, _, _ in SMALL_LAYOUT})
    updated, loss, dww_sum = _small_update(all_packs.reshape(N_DEV, PACK_ROWS, D_MODEL), packed(""), packed("m_"),
                                           packed("v_"), "update_small")
    res = {kind: {name: val[k].reshape(given[name].shape) for name, val in updated.items()}
           for k, kind in enumerate("gdmv")}
    dww_full = dww_sum[:CONV_WIDTH].reshape(CONV_WIDTH, depth, CONV_DIM).transpose(1, 0, 2)
    g_dw_w = lax.dynamic_slice_in_dim(dww_full, my_idx * (CONV_DIM // N_DEV), CONV_DIM // N_DEV, axis=2)

    swap = lambda a: a.transpose(0, 2, 1)
    state = {"w_in": (w_in, m_w_in, v_w_in), "w_pw": (w_pw, m_w_pw, v_w_pw), "w_out": (w_out, m_w_out, v_w_out),
             "w_ple_gate": (w_ple_gate, m_w_ple_gate, v_w_ple_gate), "w_ple": (w_ple, m_w_ple, v_w_ple)}
    for at, name in enumerate(state):
        wv, mv, vv = [swap(a) for a in state[name]] if name == "w_in" else state[name]
        out = _sum_adamw([slots[l][at] for l in range(depth)], wv, mv, vv, f"adamw_{name}")
        out = [swap(a) for a in out] if name == "w_in" else out
        res["g"][name], res["d"][name], res["m"][name], res["v"][name] = out
    flat = lambda a: a.reshape(-1, a.shape[-1])
    res["g"]["dw_w"] = g_dw_w
    res["d"]["dw_w"], res["m"]["dw_w"], res["v"]["dw_w"] = [
        a.reshape(dw_w.shape) for a in _adamw(flat(dw_w), flat(g_dw_w), flat(m_dw_w), flat(v_dw_w), "adamw_dw_w")]

    order = ["norm_g", "w_in", "attn_out_g", "dw_w", "dw_b", "conv_ln_g", "conv_ln_b", "w_pw", "conv_out_g",
             "w_out", "ple_norm_g", "w_ple_gate", "w_ple", "final_g"]
    return (loss, grad_x, *[res["g"][n] for n in order], *[res["d"][n] for n in order],
            *[res["m"][n] for n in order], *[res["v"][n] for n in order])
```

```python
import functools

import jax
import jax.numpy as jnp
from jax import lax
from jax.experimental import pallas as pl
from jax.experimental.pallas import tpu as pltpu

F32 = jnp.float32
BF16 = jnp.bfloat16
MESH = pl.DeviceIdType.MESH

N_DEV = 8
D_MODEL = 1024
ATTN_DIM = 512
CONV_DIM = 512
HEAD_DIM = 64
N_HEADS = 8
CONV_WIDTH = 31
PLE_DIM = 256
CHUNK = 512
N_CHUNK = 7
EPS = 1e-6
ADAM_LR = 0.001
ADAM_B1 = 0.9
ADAM_B2 = 0.999
ADAM_EPS = 1e-08
ADAM_WD = 0.01
ADAM_STEP = 10

LANES = 128
BLK = 256
ATT_COLS = 4
CHAIN_GROUP = 4
SOFTPLUS_LINEAR_AT = 20.0
DEAD_AT = 110.0
FIRST_BLOCK_LANE = HEAD_DIM - 1
TM = 512
HALO = 32
SUBLANES = 8
CONV_ROWS = 32
ADAMW_ROWS = 64
VMEM_LIMIT = 56 * 1024 * 1024
SMALL_ROWS = 16
SMALL_LAYOUT = (("norm_g", 0, 2, D_MODEL), ("ple_norm_g", 2, 2, D_MODEL), ("final_g", 4, 1, D_MODEL),
                ("dw_b", 5, 2, CONV_DIM), ("conv_ln_g", 7, 2, CONV_DIM), ("conv_ln_b", 9, 2, CONV_DIM),
                ("conv_out_g", 11, 2, CONV_DIM), ("attn_out_g", 13, 2, HEAD_DIM))
LOSS_ROW = 15
W_IN_QUARTERS = ((0, 96), (96, 144), (240, 96), (336, 112))
W_IN_GRAD_PARTS = ((0, 96), (96, 80), (176, 144), (320, 128))
OTHER_WEIGHTS_ON = {"inproj": (("w_gate", (0, 64)),), "attn": (("w_out", ()), ("w_ple", ())),
                    "conv": (("w_pw", ()), ("w_gate", (64, 64)))}
PACK_ROWS = 48


def _nn(a, b):
    return lax.dot_general(a, b, (((1,), (0,)), ((), ())), preferred_element_type=F32)


def _nt(a, b):
    return lax.dot_general(a, b, (((1,), (1,)), ((), ())), preferred_element_type=F32)


def _tn(a, b):
    return lax.dot_general(a, b, (((0,), (0,)), ((), ())), preferred_element_type=F32)


def _split(x):
    hi = x.astype(BF16)
    lo = (x - hi.astype(F32)).astype(BF16)
    return hi, lo


def _dot_hilo(x, m):
    hi, lo = _split(x)
    return _nn(hi, m) + _nn(lo, m)


def _sigmoid(x):
    return jax.nn.sigmoid(x)


def _dsilu(x, s):
    return s * (1.0 + x * (1.0 - s))


def _params(n_grid=0, vmem=VMEM_LIMIT):
    sem = ("arbitrary",) * n_grid if n_grid else None
    return pltpu.CompilerParams(dimension_semantics=sem, vmem_limit_bytes=vmem)


def _rows(tm, cols, col=0):
    return pl.BlockSpec((tm, cols), lambda i: (i, col))


def _whole(shape):
    zeros = (0,) * len(shape)
    return pl.BlockSpec(shape, lambda *_: zeros)


def _my_position():
    return lax.axis_index("x"), lax.axis_index("y"), lax.axis_index("c")


def _block(ref, axis, idx, size):
    start = pl.multiple_of(idx * size, size)
    if axis == 0:
        return ref.at[pl.ds(start, size), :]
    return ref.at[:, pl.ds(start, size)]


def _all_gather(shards, axes, name):
    n = len(shards)
    sizes = [s.shape[a] for s, a in zip(shards, axes)]

    def full_shape(s, a):
        shape = list(s.shape)
        shape[a] *= N_DEV
        return jax.ShapeDtypeStruct(tuple(shape), s.dtype)

    def body(*refs):
        ins, outs = refs[:n], refs[n:2 * n]
        send_sems, recv_sems, local_sems = refs[2 * n:]
        x, y, c = _my_position()
        me, sibling = (x, y, c), (x, y, 1 - c)
        chips = [(1 - x, y), (x, 1 - y), (1 - x, 1 - y)]

        def place(i, dev):
            return _block(outs[i], axes[i], 4 * dev[0] + 2 * dev[1] + dev[2], sizes[i])

        def copy(k, i, dev, to, src=None):
            return pltpu.make_async_remote_copy(
                src_ref=place(i, dev) if src is None else src, dst_ref=place(i, dev),
                send_sem=send_sems.at[k, i], recv_sem=recv_sems.at[k, i],
                device_id=to, device_id_type=MESH)

        mine = [pltpu.make_async_copy(ins[i], place(i, me), local_sems.at[i]) for i in range(n)]
        for cp in mine:
            cp.start()
        first = [copy(0, i, me, sibling, src=ins[i]) for i in range(n)]
        for j, chip in enumerate(chips):
            first += [copy(1 + j, i, me, (*chip, c), src=ins[i]) for i in range(n)]
        for cp in first:
            cp.start()
        passed = []
        for j, chip in enumerate(chips):
            for i in range(n):
                copy(1 + j, i, (*chip, c), me).wait_recv()
            hop = [copy(4 + j, i, (*chip, c), sibling) for i in range(n)]
            for cp in hop:
                cp.start()
            passed += hop
        for i in range(n):
            copy(0, i, sibling, me).wait_recv()
        for j, chip in enumerate(chips):
            for i in range(n):
                copy(4 + j, i, (*chip, 1 - c), me).wait_recv()
        for cp in first + passed:
            cp.wait_send()
        for cp in mine:
            cp.wait()

    any_spec = pl.BlockSpec(memory_space=pl.ANY)
    return pl.pallas_call(
        body, name=name,
        out_shape=[full_shape(s, a) for s, a in zip(shards, axes)],
        in_specs=[any_spec] * n, out_specs=[any_spec] * n,
        scratch_shapes=[pltpu.SemaphoreType.DMA((7, n)), pltpu.SemaphoreType.DMA((7, n)),
                        pltpu.SemaphoreType.DMA((n,))],
    )(*shards)


class _Ride:
    def __init__(self, parts):
        self.parts = [p for p in parts if p is not None]

    @staticmethod
    def gather(src, axis, land=None, lo=0, n=None):
        return ("gather", src, land, axis, lo, src.shape[axis] if n is None else n)

    @staticmethod
    def scatter(src, axis, land=None, lo=0, n=None):
        return ("scatter", src, land, axis, lo, src.shape[axis] // N_DEV if n is None else n)

    def arrays(self):
        return [p[1] for p in self.parts] + [p[2] for p in self.parts if p[2] is not None]

    def out_shapes(self):
        out = []
        for kind, src, _, axis, _, _ in self.parts:
            shape = list(src.shape)
            if kind == "gather":
                shape[axis] *= N_DEV
            else:
                shape[axis] //= N_DEV
                shape = [N_DEV] + shape
            out.append(jax.ShapeDtypeStruct(tuple(shape), src.dtype))
        return out

    def aliases(self, n_in, n_out):
        m, out = len(self.parts), {}
        for j, p in enumerate(self.parts):
            if p[2] is not None:
                out[n_in + m + len(out)] = n_out + j
        return out

    def scratch(self):
        m = len(self.parts)
        return [pltpu.SemaphoreType.DMA((N_DEV - 1, m)), pltpu.SemaphoreType.DMA((N_DEV - 1, m)),
                pltpu.SemaphoreType.DMA((m,))]

    def _copies(self, src_refs, land_refs, sems):
        send_sems, recv_sems, local_sems = sems
        x, y, c = _my_position()
        my_idx = 4 * x + 2 * y + c
        own, sends, lands = [], [], []
        for j, (kind, src, _, axis, lo, n) in enumerate(self.parts):
            size = src.shape[axis] if kind == "gather" else src.shape[axis] // N_DEV
            align = 16 if axis == 0 else LANES

            def rows(ref, idx, lead=None, axis=axis, lo=lo, n=n, size=size, align=align):
                at = pl.ds(pl.multiple_of(idx * size + lo, align), n)
                where = (at, slice(None)) if axis == 0 else (slice(None), at)
                return ref.at[where] if lead is None else ref.at[(lead, *where)]

            def in_shard(ref):
                return rows(ref, 0)

            def in_slot(ref, s):
                return rows(ref, 0, lead=s)

            for k in range(N_DEV):
                px = 1 - x if k & 4 else x
                py = 1 - y if k & 2 else y
                pc = 1 - c if k & 1 else c
                peer_idx = 4 * px + 2 * py + pc
                if kind == "gather":
                    a, b, landed = in_shard(src_refs[j]), rows(land_refs[j], my_idx), rows(land_refs[j], peer_idx)
                else:
                    a, b, landed = rows(src_refs[j], peer_idx), in_slot(land_refs[j], my_idx), in_slot(land_refs[j], peer_idx)
                if k == 0:
                    own.append(pltpu.make_async_copy(a, b, local_sems.at[j]))
                    continue
                mk = lambda dst, a=a, k=k, j=j, to=(px, py, pc): pltpu.make_async_remote_copy(
                    src_ref=a, dst_ref=dst, send_sem=send_sems.at[k - 1, j], recv_sem=recv_sems.at[k - 1, j],
                    device_id=to, device_id_type=MESH)
                sends.append(mk(b))
                lands.append(mk(landed))
        return own, sends, lands

    def start(self, src_refs, land_refs, sems):
        own, sends, _ = self._copies(src_refs, land_refs, sems)
        for cp in own + sends:
            cp.start()

    def wait(self, src_refs, land_refs, sems):
        own, sends, lands = self._copies(src_refs, land_refs, sems)
        for cp in lands:
            cp.wait_recv()
        for cp in sends:
            cp.wait_send()
        for cp in own:
            cp.wait()


def _call(body, *, name, grid, in_specs, out_specs, out_shape, args, scratch_shapes=(), ride=None):
    in_specs, out_specs, out_shape = list(in_specs), list(out_specs), list(out_shape)
    n_in, n_out, n_sc = len(in_specs), len(out_specs), len(scratch_shapes)
    if ride is None or not ride.parts:
        res = pl.pallas_call(body, name=name, grid=grid, in_specs=in_specs, out_specs=out_specs,
                             out_shape=out_shape, scratch_shapes=list(scratch_shapes),
                             compiler_params=_params(len(grid)))(*args)
        return list(res), []
    extra, m = ride.arrays(), len(ride.parts)

    def riding(*refs):
        a = n_in + len(extra)
        b = a + n_out
        srcs, lands, sems = refs[n_in:n_in + m], refs[b:b + m], refs[b + m + n_sc:]
        at = [pl.program_id(d) for d in range(len(grid))]

        @pl.when(functools.reduce(jnp.logical_and, [i == 0 for i in at]))
        def _():
            ride.start(srcs, lands, sems)

        body(*refs[:n_in], *refs[a:b], *refs[b + m:b + m + n_sc])

        @pl.when(functools.reduce(jnp.logical_and, [i == g - 1 for i, g in zip(at, grid)]))
        def _():
            ride.wait(srcs, lands, sems)

    hbm = pl.BlockSpec(memory_space=pl.ANY)
    res = pl.pallas_call(
        riding, name=name, grid=grid, in_specs=in_specs + [hbm] * len(extra), out_specs=out_specs + [hbm] * m,
        out_shape=out_shape + ride.out_shapes(), scratch_shapes=list(scratch_shapes) + ride.scratch(),
        input_output_aliases=ride.aliases(n_in, n_out), compiler_params=_params(len(grid)),
    )(*args, *extra)
    return list(res[:n_out]), list(res[n_out:])


def _prenorm_inproj(h, gain, w_in_t, name, ride=None):
    T = h.shape[0]

    def body(h_ref, g_ref, w_ref, q_ref, k_ref, v_ref, ug_ref, hn_ref):
        hv = h_ref[...]
        r = lax.rsqrt(jnp.mean(hv * hv, axis=-1, keepdims=True) + EPS)
        hn = (hv * r * g_ref[...]).astype(BF16)
        hn_ref[...] = hn
        for j in range(N_CHUNK):
            u = _nt(hn, w_ref[j * CHUNK:(j + 1) * CHUNK, :])
            if j == 0:
                q_ref[...] = (u * (HEAD_DIM ** -0.5)).astype(BF16)
            elif j == 1:
                k_ref[...] = u.astype(BF16)
            elif j == 2:
                v_ref[...] = u.astype(BF16)
            else:
                ug_ref[:, (j - 3) * CHUNK:(j - 2) * CHUNK] = u

    act = jax.ShapeDtypeStruct((T, CHUNK), BF16)
    return _call(
        body, name=name, grid=(T // TM,),
        in_specs=[_rows(TM, D_MODEL), _whole((1, D_MODEL)), _whole((N_CHUNK * CHUNK, D_MODEL))],
        out_specs=[_rows(TM, CHUNK)] * 3 + [_rows(TM, 4 * CHUNK), _rows(TM, D_MODEL)],
        out_shape=[act, act, act, jax.ShapeDtypeStruct((T, 4 * CHUNK), F32),
                   jax.ShapeDtypeStruct((T, D_MODEL), BF16)],
        args=(h, gain, w_in_t,), ride=ride)


def _softplus_parts(z):
    ez = jnp.exp(jnp.minimum(z, SOFTPLUS_LINEAR_AT))
    t = 1.0 + ez
    return ez * pl.reciprocal(t, approx=True), jnp.where(z > SOFTPLUS_LINEAR_AT, z, jnp.log(t))


def _attn_fwd(qs, k, v, tri, name, ride=None):
    T = qs.shape[0]
    assert T // BLK <= FIRST_BLOCK_LANE, "one lane per key block below the lane of the first block"
    width = LANES * ATT_COLS
    chains = [(c, half) for c in range(ATT_COLS) for half in range(2)]

    def body(q_ref, k_ref, v_ref, m_ref, o_ref, cs_ref):
        qi = pl.program_id(1)
        lane = lax.broadcasted_iota(jnp.int32, (BLK, LANES), 1)
        first = lane < HEAD_DIM
        causal = (lax.broadcasted_iota(jnp.int32, (BLK, BLK), 1)
                  < lax.broadcasted_iota(jnp.int32, (BLK, BLK), 0))
        tri_m = m_ref[...]
        qh = {}
        for c in range(ATT_COLS):
            q = q_ref[:, c * LANES:(c + 1) * LANES]
            zero = jnp.zeros_like(q)
            qh[c, 0], qh[c, 1] = jnp.where(first, q, zero), jnp.where(first, zero, q)

        def step(kb, state, masked):
            carries, accs, cvals = state
            start = pl.multiple_of(kb * BLK, BLK)
            kblk = [k_ref[pl.ds(start, BLK), c * LANES:(c + 1) * LANES] for c in range(ATT_COLS)]
            vblk = [v_ref[pl.ds(start, BLK), c * LANES:(c + 1) * LANES] for c in range(ATT_COLS)]
            carries, accs, cvals = list(carries), list(accs), list(cvals)
            for g0 in range(0, len(chains), CHAIN_GROUP):
                ids = range(g0, g0 + CHAIN_GROUP)
                z = [_nt(qh[chains[n]], kblk[chains[n][0]]) for n in ids]
                sp = [_softplus_parts(zi)[1] for zi in z]
                if masked:
                    sp = [jnp.where(causal, s, 0.0) for s in sp]
                incl = [_dot_hilo(s, tri_m) for s in sp]
                a = [jnp.exp(zi - ii - carries[n]) for n, zi, ii in zip(ids, z, incl)]
                if masked:
                    a = [jnp.where(causal, ai, 0.0) for ai in a]
                for n, ai, ii in zip(ids, a, incl):
                    c, half = chains[n]
                    zero = jnp.zeros_like(vblk[c])
                    vh = jnp.where(first, vblk[c], zero) if half == 0 else jnp.where(first, zero, vblk[c])
                    accs[c] = accs[c] + _nn(ai.astype(BF16), vh)
                    cvals[c] = jnp.where(lane == kb + HEAD_DIM * half, carries[n], cvals[c])
                    carries[n] = carries[n] + ii[:, 0:1]
            return tuple(carries), tuple(accs), tuple(cvals)

        zeros = tuple(jnp.zeros((BLK, LANES), F32) for _ in range(ATT_COLS))
        state = (tuple(jnp.zeros((BLK, 1), F32) for _ in chains), zeros, zeros)
        state = step(qi, state, True)

        def reaches_further(st):
            it, (carries, _, _) = st
            least = functools.reduce(jnp.minimum, carries)
            return jnp.logical_and(it < qi, jnp.min(least) < DEAD_AT)

        done, state = lax.while_loop(reaches_further, lambda st: (st[0] + 1, step(qi - 1 - st[0], st[1], False)),
                                     (jnp.int32(0), state))
        first_block = (qi - done).astype(F32)
        for c in range(ATT_COLS):
            o_ref[:, c * LANES:(c + 1) * LANES] = state[1][c]
            cs_ref[:, c * LANES:(c + 1) * LANES] = jnp.where(lane == FIRST_BLOCK_LANE, first_block, state[2][c])

    blk = pl.BlockSpec((BLK, width), lambda j, i: (i, j))
    col = pl.BlockSpec((T, width), lambda j, i: (0, j))
    out = jax.ShapeDtypeStruct((T, ATTN_DIM), F32)
    return _call(
        body, name=name, grid=(ATTN_DIM // width, T // BLK),
        in_specs=[blk, col, col, _whole((BLK, BLK))],
        out_specs=[blk, blk], out_shape=[out, out],
        args=(qs, k, v, tri,), ride=ride)


def _shifted_copies(pad_ref, sh_ref):
    rows = sh_ref.shape[1]
    for b in range(SUBLANES):
        sh_ref[b] = pad_ref[b:b + rows, :]


def _shift_of(offset):
    return offset % SUBLANES, offset - offset % SUBLANES


def _conv_fwd(ug, dw_w, dw_b, ln_g, ln_b, name, ride=None):
    T = ug.shape[0]
    per = TM // HALO

    def body(cv_ref, cg_ref, cvh_ref, cgh_ref, w_ref, b_ref, g_ref, beta_ref, conv_ref, c2_ref, pad_ref, sh_ref):
        i = pl.program_id(0)
        halo = cvh_ref[...] * _sigmoid(cgh_ref[...])
        pad_ref[0:HALO, :] = jnp.where(i == 0, 0.0, halo)
        pad_ref[HALO:HALO + TM, :] = cv_ref[...] * _sigmoid(cg_ref[...])
        pad_ref[HALO + TM:, :] = jnp.zeros((SUBLANES, CONV_DIM), F32)
        _shifted_copies(pad_ref, sh_ref)
        taps = [w_ref[t:t + 1, :] for t in range(CONV_WIDTH)]

        def rows(j, _):
            r = pl.multiple_of(j * CONV_ROWS, CONV_ROWS)
            acc = jnp.zeros((CONV_ROWS, CONV_DIM), F32) + b_ref[...]
            for t in range(CONV_WIDTH):
                b, a = _shift_of(HALO - (CONV_WIDTH - 1) + t)
                acc = acc + taps[t] * sh_ref[b, pl.ds(r + a, CONV_ROWS), :]
            conv_ref[pl.ds(r, CONV_ROWS), :] = acc
            return 0

        lax.fori_loop(0, TM // CONV_ROWS, rows, 0)
        acc = conv_ref[...]
        mu = jnp.mean(acc, axis=-1, keepdims=True)
        xc = acc - mu
        rs = lax.rsqrt(jnp.mean(xc * xc, axis=-1, keepdims=True) + EPS)
        ln = xc * rs * g_ref[...] + beta_ref[...]
        c2_ref[...] = (ln * _sigmoid(ln)).astype(BF16)

    prev = lambda col: pl.BlockSpec((HALO, CHUNK), lambda i: (jnp.maximum(i * per - 1, 0), col))
    vec = _whole((1, CONV_DIM))
    return _call(
        body, name=name, grid=(T // TM,),
        in_specs=[_rows(TM, CHUNK, 1), _rows(TM, CHUNK, 2), prev(1), prev(2),
                  _whole((CONV_WIDTH, CONV_DIM)), vec, vec, vec],
        out_specs=[_rows(TM, CONV_DIM), _rows(TM, CONV_DIM)],
        out_shape=[jax.ShapeDtypeStruct((T, CONV_DIM), F32), jax.ShapeDtypeStruct((T, CONV_DIM), BF16)],
        scratch_shapes=[pltpu.VMEM((TM + HALO + SUBLANES, CONV_DIM), F32),
                        pltpu.VMEM((SUBLANES, TM + HALO, CONV_DIM), F32)],
        args=(ug, ug, ug, ug, dw_w, dw_b, ln_g, ln_b,), ride=ride)


def _mix_out_ple(o, ug, c2, h, p, head_mean, g_attn, g_conv, g_ple, w_pw, w_out, w_gate, w_ple, name, ride=None):
    T = h.shape[0]

    def body(o_ref, ga_ref, gc_ref, c2_ref, h_ref, p_ref, hm_ref, gao_ref, gco_ref, gpn_ref,
             wpw_ref, wout_ref, wg_ref, wple_ref,
             h2_ref, h1_ref, ycat_ref, hn2_ref, gate_ref, e_ref, c3_ref):
        ov = o_ref[...]
        rh = lax.rsqrt(_nn((ov * ov).astype(BF16), hm_ref[...]) + EPS)
        ga = ga_ref[...]
        ya = (ov * rh * gao_ref[...] * (ga * _sigmoid(ga))).astype(BF16)
        c3 = _nn(c2_ref[...], wpw_ref[...])
        c3_ref[...] = c3
        rc = lax.rsqrt(jnp.mean(c3 * c3, axis=-1, keepdims=True) + EPS)
        gc = gc_ref[...]
        yc = (c3 * rc * gco_ref[...] * (gc * _sigmoid(gc))).astype(BF16)
        ycat_ref[:, :ATTN_DIM] = ya
        ycat_ref[:, ATTN_DIM:] = yc
        h1 = h_ref[...] + _nn(ya, wout_ref[:ATTN_DIM, :]) + _nn(yc, wout_ref[ATTN_DIM:, :])
        h1_ref[...] = h1
        r1 = lax.rsqrt(jnp.mean(h1 * h1, axis=-1, keepdims=True) + EPS)
        hn2 = (h1 * r1 * gpn_ref[...]).astype(BF16)
        hn2_ref[...] = hn2
        gate = _sigmoid(_nn(hn2, wg_ref[...]))
        e = _nn(p_ref[...].astype(BF16), wple_ref[...])
        gate_ref[...] = gate
        e_ref[...] = e
        h2_ref[...] = h1 + e * gate

    f32 = lambda cols: jax.ShapeDtypeStruct((T, cols), F32)
    bf = lambda cols: jax.ShapeDtypeStruct((T, cols), BF16)
    return _call(
        body, name=name, grid=(T // TM,),
        in_specs=[_rows(TM, ATTN_DIM), _rows(TM, CHUNK, 0), _rows(TM, CHUNK, 3), _rows(TM, CONV_DIM),
                  _rows(TM, D_MODEL), _rows(TM, PLE_DIM), _whole((ATTN_DIM, ATTN_DIM)),
                  _whole((1, ATTN_DIM)), _whole((1, CONV_DIM)), _whole((1, D_MODEL)),
                  _whole((CONV_DIM, CONV_DIM)), _whole((D_MODEL, D_MODEL)), _whole((D_MODEL, D_MODEL)),
                  _whole((PLE_DIM, D_MODEL))],
        out_specs=[_rows(TM, D_MODEL), _rows(TM, D_MODEL), _rows(TM, D_MODEL), _rows(TM, D_MODEL),
                   _rows(TM, D_MODEL), _rows(TM, D_MODEL), _rows(TM, CONV_DIM)],
        out_shape=[f32(D_MODEL), f32(D_MODEL), bf(D_MODEL), bf(D_MODEL), f32(D_MODEL), f32(D_MODEL),
                   f32(CONV_DIM)],
        args=(o, ug, ug, c2, h, p, head_mean, g_attn, g_conv, g_ple, w_pw, w_out, w_gate, w_ple,), ride=ride)


def _final_loss(h, target, gain, name):
    T = h.shape[0]

    def body(h_ref, t_ref, g_ref, dh_ref, gsum_ref, loss_ref):
        @pl.when(pl.program_id(0) == 0)
        def _():
            gsum_ref[...] = jnp.zeros_like(gsum_ref)
            loss_ref[...] = jnp.zeros_like(loss_ref)

        hv = h_ref[...]
        r = lax.rsqrt(jnp.mean(hv * hv, axis=-1, keepdims=True) + EPS)
        xh = hv * r
        diff = xh * g_ref[...] - t_ref[...]
        loss_ref[...] += 0.5 * jnp.sum(jnp.mean(diff * diff, axis=-1, keepdims=True), axis=0, keepdims=True)
        dy = diff * (1.0 / D_MODEL)
        gsum_ref[...] += jnp.sum(dy * xh, axis=0, keepdims=True)
        dxh = dy * g_ref[...]
        dh_ref[...] = r * (dxh - xh * jnp.mean(dxh * xh, axis=-1, keepdims=True))

    return pl.pallas_call(
        body, name=name, grid=(T // TM,),
        in_specs=[_rows(TM, D_MODEL), _rows(TM, D_MODEL), _whole((1, D_MODEL))],
        out_specs=[_rows(TM, D_MODEL), _whole((1, D_MODEL)), _whole((1, LANES))],
        out_shape=[jax.ShapeDtypeStruct((T, D_MODEL), F32), jax.ShapeDtypeStruct((1, D_MODEL), F32),
                   jax.ShapeDtypeStruct((1, LANES), F32)],
        compiler_params=_params(1),
    )(h, target, gain)


def _ple_out_bwd(dh2, gate, e, h1, g_ple, w_gate, w_out, name, ride=None):
    T = dh2.shape[0]

    def body(dh2_ref, gate_ref, e_ref, h1_ref, gpn_ref, wg_ref, wout_ref,
             dh1_ref, dh1b_ref, dzg_ref, de_ref, dycat_ref, gsum_ref):
        @pl.when(pl.program_id(0) == 0)
        def _():
            gsum_ref[...] = jnp.zeros_like(gsum_ref)

        dh2v = dh2_ref[...]
        gate = gate_ref[...]
        de_ref[...] = (dh2v * gate).astype(BF16)
        dzg = (dh2v * e_ref[...] * gate * (1.0 - gate)).astype(BF16)
        dzg_ref[...] = dzg
        dhn2 = _nt(dzg, wg_ref[...])
        h1 = h1_ref[...]
        r1 = lax.rsqrt(jnp.mean(h1 * h1, axis=-1, keepdims=True) + EPS)
        xh = h1 * r1
        gsum_ref[...] += jnp.sum(dhn2 * xh, axis=0, keepdims=True)
        dxh = dhn2 * gpn_ref[...]
        dh1 = dh2v + r1 * (dxh - xh * jnp.mean(dxh * xh, axis=-1, keepdims=True))
        dh1_ref[...] = dh1
        dh1b = dh1.astype(BF16)
        dh1b_ref[...] = dh1b
        dycat_ref[...] = _nt(dh1b, wout_ref[...])

    f32 = jax.ShapeDtypeStruct((T, D_MODEL), F32)
    bf = jax.ShapeDtypeStruct((T, D_MODEL), BF16)
    full = _rows(TM, D_MODEL)
    return _call(
        body, name=name, grid=(T // TM,),
        in_specs=[full, full, full, full, _whole((1, D_MODEL)), _whole((D_MODEL, D_MODEL)),
                  _whole((D_MODEL, D_MODEL))],
        out_specs=[full, full, full, full, full, _whole((1, D_MODEL))],
        out_shape=[f32, bf, bf, bf, f32, jax.ShapeDtypeStruct((1, D_MODEL), F32)],
        args=(dh2, gate, e, h1, g_ple, w_gate, w_out,), ride=ride)


def _branch_bwd(dycat, o, ug, c3, conv, head_mean, g_attn, g_conv, ln_g, ln_b, w_pw, name, ride=None):
    T = o.shape[0]

    def body(dya_ref, dyc_ref, o_ref, ga_ref, gc_ref, c3_ref, conv_ref, hm_ref, gao_ref, gco_ref,
             lng_ref, lnb_ref, wpw_ref,
             do_ref, dga_ref, dgc_ref, dc3_ref, dconv_ref, sums_ref):
        @pl.when(pl.program_id(0) == 0)
        def _():
            sums_ref[...] = jnp.zeros_like(sums_ref)

        hm = hm_ref[...]
        col = lambda x: jnp.sum(x, axis=0, keepdims=True)
        ov = o_ref[...]
        rh = lax.rsqrt(_nn((ov * ov).astype(BF16), hm) + EPS)
        xh = ov * rh
        ga = ga_ref[...]
        sg = _sigmoid(ga)
        dya = dya_ref[...]
        don = dya * (ga * sg)
        dga_ref[...] = (dya * xh * gao_ref[...] * _dsilu(ga, sg)).astype(BF16)
        sums_ref[0:1, :] += col(don * xh)
        dxh = don * gao_ref[...]
        do_ref[...] = (rh * (dxh - xh * _dot_hilo(dxh * xh, hm))).astype(BF16)
        c3 = c3_ref[...]
        rc = lax.rsqrt(jnp.mean(c3 * c3, axis=-1, keepdims=True) + EPS)
        xh3 = c3 * rc
        gc = gc_ref[...]
        sgc = _sigmoid(gc)
        dyc = dyc_ref[...]
        dn3 = dyc * (gc * sgc)
        dgc_ref[...] = (dyc * xh3 * gco_ref[...] * _dsilu(gc, sgc)).astype(BF16)
        sums_ref[1:2, :] += col(dn3 * xh3)
        dxh3 = dn3 * gco_ref[...]
        dc3 = (rc * (dxh3 - xh3 * jnp.mean(dxh3 * xh3, axis=-1, keepdims=True))).astype(BF16)
        dc3_ref[...] = dc3
        dc2 = _nt(dc3, wpw_ref[...])
        cv = conv_ref[...]
        mu = jnp.mean(cv, axis=-1, keepdims=True)
        xc = cv - mu
        rs = lax.rsqrt(jnp.mean(xc * xc, axis=-1, keepdims=True) + EPS)
        xn = xc * rs
        ln = xn * lng_ref[...] + lnb_ref[...]
        dln = dc2 * _dsilu(ln, _sigmoid(ln))
        sums_ref[2:3, :] += col(dln * xn)
        sums_ref[3:4, :] += col(dln)
        dxn = dln * lng_ref[...]
        dconv = rs * (dxn - jnp.mean(dxn, axis=-1, keepdims=True)
                      - xn * jnp.mean(dxn * xn, axis=-1, keepdims=True))
        dconv_ref[...] = dconv
        sums_ref[4:5, :] += col(dconv)

    half = lambda dt: jax.ShapeDtypeStruct((T, CHUNK), dt)
    tile = _rows(TM, CHUNK)
    vec = _whole((1, CHUNK))
    return _call(
        body, name=name, grid=(T // TM,),
        in_specs=[_rows(TM, CHUNK, 0), _rows(TM, CHUNK, 1), tile, _rows(TM, CHUNK, 0), _rows(TM, CHUNK, 3),
                  tile, tile, _whole((ATTN_DIM, ATTN_DIM)), vec, vec, vec, vec, _whole((CONV_DIM, CONV_DIM))],
        out_specs=[tile, tile, tile, tile, tile, _whole((8, CHUNK))],
        out_shape=[half(BF16), half(BF16), half(BF16), half(BF16), half(F32),
                   jax.ShapeDtypeStruct((8, CHUNK), F32)],
        args=(dycat, dycat, o, ug, ug, c3, conv, head_mean, g_attn, g_conv, ln_g, ln_b, w_pw,), ride=ride)


def _conv_bwd(dconv, ug, dw_w, name, ride=None):
    T = dconv.shape[0]
    per = TM // HALO
    last = T // HALO - 1
    n_tiles = T // TM

    def body(d_ref, dn_ref, cv_ref, cg_ref, cvh_ref, cgh_ref, w_ref, dcv_ref, dcg_ref, dw_ref,
             dpad_ref, cpad_ref, dsh_ref, csh_ref, dw_acc):
        i = pl.program_id(0)

        @pl.when(i == 0)
        def _():
            dw_acc[...] = jnp.zeros_like(dw_acc)

        tail = jnp.zeros((SUBLANES, CONV_DIM), F32)
        dpad_ref[0:TM, :] = d_ref[...]
        dpad_ref[TM:TM + HALO, :] = jnp.where(i == n_tiles - 1, 0.0, dn_ref[...])
        dpad_ref[TM + HALO:, :] = tail
        halo = cvh_ref[...] * _sigmoid(cgh_ref[...])
        cpad_ref[0:HALO, :] = jnp.where(i == 0, 0.0, halo)
        cpad_ref[HALO:HALO + TM, :] = cv_ref[...] * _sigmoid(cg_ref[...])
        cpad_ref[HALO + TM:, :] = tail
        _shifted_copies(dpad_ref, dsh_ref)
        _shifted_copies(cpad_ref, csh_ref)
        taps = [w_ref[t:t + 1, :] for t in range(CONV_WIDTH)]

        def rows(j, _):
            r = pl.multiple_of(j * CONV_ROWS, CONV_ROWS)
            d = d_ref[pl.ds(r, CONV_ROWS), :]
            dc = jnp.zeros((CONV_ROWS, CONV_DIM), F32)
            for t in range(CONV_WIDTH):
                b, a = _shift_of(CONV_WIDTH - 1 - t)
                dc = dc + taps[t] * dsh_ref[b, pl.ds(r + a, CONV_ROWS), :]
                b, a = _shift_of(HALO - (CONV_WIDTH - 1) + t)
                prod = d * csh_ref[b, pl.ds(r + a, CONV_ROWS), :]
                dw_acc[t] += jnp.sum(prod.reshape(CONV_ROWS // SUBLANES, SUBLANES, CONV_DIM), axis=0)
            cv = cv_ref[pl.ds(r, CONV_ROWS), :]
            sg = _sigmoid(cg_ref[pl.ds(r, CONV_ROWS), :])
            dcv_ref[pl.ds(r, CONV_ROWS), :] = (dc * sg).astype(BF16)
            dcg_ref[pl.ds(r, CONV_ROWS), :] = (dc * cv * sg * (1.0 - sg)).astype(BF16)
            return 0

        lax.fori_loop(0, TM // CONV_ROWS, rows, 0)

        @pl.when(i == n_tiles - 1)
        def _():
            dw_ref[...] = jnp.zeros_like(dw_ref)
            for t in range(CONV_WIDTH):
                dw_ref[t:t + 1, :] = jnp.sum(dw_acc[t], axis=0, keepdims=True)

    prev = lambda col: pl.BlockSpec((HALO, CHUNK), lambda i: (jnp.maximum(i * per - 1, 0), col))
    nxt = pl.BlockSpec((HALO, CONV_DIM), lambda i: (jnp.minimum((i + 1) * per, last), 0))
    half = jax.ShapeDtypeStruct((T, CHUNK), BF16)
    return _call(
        body, name=name, grid=(T // TM,),
        in_specs=[_rows(TM, CONV_DIM), nxt, _rows(TM, CHUNK, 1), _rows(TM, CHUNK, 2), prev(1), prev(2),
                  _whole((CONV_WIDTH, CONV_DIM))],
        out_specs=[_rows(TM, CHUNK), _rows(TM, CHUNK), _whole((HALO, CONV_DIM))],
        out_shape=[half, half, jax.ShapeDtypeStruct((HALO, CONV_DIM), F32)],
        scratch_shapes=[pltpu.VMEM((TM + HALO + SUBLANES, CONV_DIM), F32),
                        pltpu.VMEM((TM + HALO + SUBLANES, CONV_DIM), F32),
                        pltpu.VMEM((SUBLANES, TM + HALO, CONV_DIM), F32),
                        pltpu.VMEM((SUBLANES, TM + HALO, CONV_DIM), F32),
                        pltpu.VMEM((HALO, SUBLANES, CONV_DIM), F32)],
        args=(dconv, dconv, ug, ug, ug, ug, dw_w,), ride=ride)


def _attn_bwd(qs, k, v, do, cs, tri, tri_t, name, ride=None):
    T = qs.shape[0]
    nq = T // BLK
    width = LANES * ATT_COLS
    chains = [(c, half) for c in range(ATT_COLS) for half in range(2)]

    def body(q_ref, k_ref, v_ref, do_ref, cs_ref, m_ref, mt_ref, dq_ref, dk_ref, dv_ref, dk_acc, dv_acc):
        qi = pl.program_id(1)

        @pl.when(qi == 0)
        def _():
            dk_acc[...] = jnp.zeros_like(dk_acc)
            dv_acc[...] = jnp.zeros_like(dv_acc)

        lane = lax.broadcasted_iota(jnp.int32, (BLK, LANES), 1)
        first = lane < HEAD_DIM
        causal = (lax.broadcasted_iota(jnp.int32, (BLK, BLK), 1)
                  < lax.broadcasted_iota(jnp.int32, (BLK, BLK), 0))
        tri_m = m_ref[...]
        tri_mt = mt_ref[...]

        def halves(x):
            zero = jnp.zeros_like(x)
            return jnp.where(first, x, zero), jnp.where(first, zero, x)

        qh, doh, cs = {}, {}, []
        for c in range(ATT_COLS):
            qh[c, 0], qh[c, 1] = halves(q_ref[:, c * LANES:(c + 1) * LANES])
            doh[c, 0], doh[c, 1] = halves(do_ref[:, c * LANES:(c + 1) * LANES])
            cs.append(cs_ref[:, c * LANES:(c + 1) * LANES])

        def step(kb, state, masked):
            prefixes, dq_accs = state
            start = pl.multiple_of(kb * BLK, BLK)
            kblk = [k_ref[pl.ds(start, BLK), c * LANES:(c + 1) * LANES] for c in range(ATT_COLS)]
            vblk = [v_ref[pl.ds(start, BLK), c * LANES:(c + 1) * LANES] for c in range(ATT_COLS)]
            prefixes, dq_accs = list(prefixes), list(dq_accs)
            for g0 in range(0, len(chains), CHAIN_GROUP):
                ids = range(g0, g0 + CHAIN_GROUP)
                grp = [chains[n] for n in ids]
                z = [_nt(qh[ch], kblk[ch[0]]) for ch in grp]
                da = [_nt(doh[ch], vblk[ch[0]]) for ch in grp]
                parts = [_softplus_parts(zi) for zi in z]
                sp = [pt[1] for pt in parts]
                if masked:
                    sp = [jnp.where(causal, s, 0.0) for s in sp]
                incl = [_dot_hilo(s, tri_m) for s in sp]
                carries = [jnp.sum(jnp.where(lane == kb + HEAD_DIM * half, cs[c], 0.0), axis=1, keepdims=True)
                           for c, half in grp]
                a = [jnp.exp(zi - ii - ci) for zi, ii, ci in zip(z, incl, carries)]
                if masked:
                    a = [jnp.where(causal, ai, 0.0) for ai in a]
                w = [ai * di for ai, di in zip(a, da)]
                pinc = [_nn(wi.astype(BF16), tri_mt) for wi in w]
                dz = [wi - pt[0] * (pi + prefixes[n]) for n, wi, pt, pi in zip(ids, w, parts, pinc)]
                if masked:
                    dz = [jnp.where(causal, d, 0.0) for d in dz]
                for j in range(0, CHAIN_GROUP, 2):
                    c = grp[j][0]
                    k0, k1 = halves(kblk[c])
                    dz0, dz1 = dz[j].astype(BF16), dz[j + 1].astype(BF16)
                    a0, a1 = a[j].astype(BF16), a[j + 1].astype(BF16)
                    dq_accs[c] = dq_accs[c] + _nn(dz0, k0) + _nn(dz1, k1)
                    dk_acc[pl.ds(start, BLK), c * LANES:(c + 1) * LANES] += _tn(dz0, qh[c, 0]) + _tn(dz1, qh[c, 1])
                    dv_acc[pl.ds(start, BLK), c * LANES:(c + 1) * LANES] += _tn(a0, doh[c, 0]) + _tn(a1, doh[c, 1])
                for n, pi in zip(ids, pinc):
                    prefixes[n] = prefixes[n] + pi[:, BLK - 1:BLK]
            return tuple(prefixes), tuple(dq_accs)

        state = (tuple(jnp.zeros((BLK, 1), F32) for _ in chains),
                 tuple(jnp.zeros((BLK, LANES), F32) for _ in range(ATT_COLS)))
        first_block = jnp.max(jnp.where(lane == FIRST_BLOCK_LANE, cs[0], 0.0)).astype(jnp.int32)
        state = lax.fori_loop(first_block, qi, lambda kb, st: step(kb, st, False), state)
        state = step(qi, state, True)
        for c in range(ATT_COLS):
            dq_ref[:, c * LANES:(c + 1) * LANES] = (state[1][c] * (HEAD_DIM ** -0.5)).astype(BF16)

        @pl.when(qi == nq - 1)
        def _():
            dk_ref[...] = dk_acc[...].astype(BF16)
            dv_ref[...] = dv_acc[...].astype(BF16)

    blk = pl.BlockSpec((BLK, width), lambda j, i: (i, j))
    col = pl.BlockSpec((T, width), lambda j, i: (0, j))
    out = jax.ShapeDtypeStruct((T, ATTN_DIM), BF16)
    return _call(
        body, name=name, grid=(ATTN_DIM // width, nq),
        in_specs=[blk, col, col, blk, blk, _whole((BLK, BLK)), _whole((BLK, BLK))],
        out_specs=[blk, col, col], out_shape=[out, out, out],
        scratch_shapes=[pltpu.VMEM((T, width), F32), pltpu.VMEM((T, width), F32)],
        args=(qs, k, v, do, cs, tri, tri_t,), ride=ride)


def _inproj_bwd(du, w_in_t, h, dh1, gain, name, ride=None):
    T = h.shape[0]

    def body(*refs):
        du_refs = refs[:N_CHUNK]
        w_ref, h_ref, dh1_ref, g_ref, dh_ref, gsum_ref = refs[N_CHUNK:]

        @pl.when(pl.program_id(0) == 0)
        def _():
            gsum_ref[...] = jnp.zeros_like(gsum_ref)

        dhn = jnp.zeros((TM, D_MODEL), F32)
        for j in range(N_CHUNK):
            dhn = dhn + _nn(du_refs[j][...], w_ref[j * CHUNK:(j + 1) * CHUNK, :])
        hv = h_ref[...]
        r = lax.rsqrt(jnp.mean(hv * hv, axis=-1, keepdims=True) + EPS)
        xh = hv * r
        gsum_ref[...] += jnp.sum(dhn * xh, axis=0, keepdims=True)
        dxh = dhn * g_ref[...]
        dh_ref[...] = dh1_ref[...] + r * (dxh - xh * jnp.mean(dxh * xh, axis=-1, keepdims=True))

    full = _rows(TM, D_MODEL)
    return _call(
        body, name=name, grid=(T // TM,),
        in_specs=[_rows(TM, CHUNK)] * N_CHUNK + [_whole((N_CHUNK * CHUNK, D_MODEL)), full, full,
                                                 _whole((1, D_MODEL))],
        out_specs=[full, _whole((1, D_MODEL))],
        out_shape=[jax.ShapeDtypeStruct((T, D_MODEL), F32), jax.ShapeDtypeStruct((1, D_MODEL), F32)],
        args=(*du, w_in_t, h, dh1, gain), ride=ride)


def _weight_grad(lhs_list, rhs, name, tk=CHUNK, ride=None):
    T, n_rhs = rhs.shape
    n = len(lhs_list)
    ka = lhs_list[0].shape[1]
    per = ka // tk

    def body(*refs):
        a_refs, b_ref, out_ref = refs[:n], refs[n], refs[n + 1]
        step = pl.program_id(0)
        for j in range(n):
            for s in range(per):
                @pl.when(step == j * per + s)
                def _(j=j, s=s):
                    out_ref[...] = _tn(a_refs[j][:, s * tk:(s + 1) * tk], b_ref[...]).astype(BF16)

    (grad,), landed = _call(
        body, name=name, grid=(n * per,),
        in_specs=[_whole((T, ka))] * n + [_whole((T, n_rhs))],
        out_specs=[pl.BlockSpec((tk, n_rhs), lambda i: (i, 0))],
        out_shape=[jax.ShapeDtypeStruct((n * ka, n_rhs), BF16)],
        args=(*lhs_list, rhs), ride=ride)
    return grad, landed


def _adamw_update(w, g, m, v):
    nm = ADAM_B1 * m + (1.0 - ADAM_B1) * g
    nv = ADAM_B2 * v + (1.0 - ADAM_B2) * (g * g)
    m_hat = nm / (1.0 - ADAM_B1 ** ADAM_STEP)
    v_hat = nv / (1.0 - ADAM_B2 ** ADAM_STEP)
    return -ADAM_LR * (m_hat / (jnp.sqrt(v_hat) + ADAM_EPS) + ADAM_WD * w), nm, nv


def _sum_adamw(slots, w, m, v, name):
    depth, R, C = w.shape
    tr = min(R, ADAMW_ROWS)

    def body(*refs):
        slot_refs, (w_ref, m_ref, v_ref, g_ref, d_ref, nm_ref, nv_ref) = refs[:depth], refs[depth:]
        for layer in range(depth):
            @pl.when(pl.program_id(0) == layer)
            def _(src=slot_refs[layer]):
                g = src[0].astype(F32)
                for s in range(1, N_DEV):
                    g = g + src[s].astype(F32)
                g_ref[0] = g
                d_ref[0], nm_ref[0], nv_ref[0] = _adamw_update(w_ref[0], g, m_ref[0], v_ref[0])

    slot_spec = lambda layer: pl.BlockSpec((N_DEV, tr, C), lambda l, i: (0, jnp.where(l == layer, i, 0), 0))
    spec = pl.BlockSpec((1, tr, C), lambda l, i: (l, i, 0))
    out = jax.ShapeDtypeStruct((depth, R, C), F32)
    return pl.pallas_call(
        body, name=name, grid=(depth, R // tr),
        in_specs=[slot_spec(layer) for layer in range(depth)] + [spec] * 3,
        out_specs=[spec] * 4, out_shape=[out] * 4,
        compiler_params=_params(2),
    )(*slots, w, m, v)


def _adamw(w, g, m, v, name):
    R, C = w.shape
    tr = R
    for cand in (512, 256, 128, 64):
        if R % cand == 0 and R > cand:
            tr = cand
            break

    def body(w_ref, g_ref, m_ref, v_ref, d_ref, nm_ref, nv_ref):
        d_ref[...], nm_ref[...], nv_ref[...] = _adamw_update(w_ref[...], g_ref[...], m_ref[...], v_ref[...])

    spec = pl.BlockSpec((tr, C), lambda i: (i, 0))
    out = jax.ShapeDtypeStruct((R, C), F32)
    return pl.pallas_call(
        body, name=name, grid=(R // tr,),
        in_specs=[spec] * 4, out_specs=[spec] * 3, out_shape=[out, out, out],
        compiler_params=_params(1),
    )(w, g, m, v)


def _pack_small(values, scalar=None):
    pad = lambda a: jnp.pad(a, ((0, 0), (0, D_MODEL - a.shape[1])))
    last = jnp.zeros((1, D_MODEL), F32) if scalar is None else pad(scalar.reshape(1, 1))
    return jnp.concatenate([pad(values[name].reshape(rows, cols)) for name, _, rows, cols in SMALL_LAYOUT] + [last],
                           axis=0)


def _small_update(all_packs, state, name):
    n = len(SMALL_LAYOUT)

    def body(packs_ref, *refs):
        ins, outs = refs[:3 * n], refs[3 * n:]
        total = packs_ref[0]
        for s in range(1, N_DEV):
            total = total + packs_ref[s]
        for j, (_, at, rows, cols) in enumerate(SMALL_LAYOUT):
            g = total[at:at + rows, :cols]
            w_ref, m_ref, v_ref = ins[3 * j:3 * j + 3]
            outs[4 * j][...] = g
            outs[4 * j + 1][...], outs[4 * j + 2][...], outs[4 * j + 3][...] = _adamw_update(
                w_ref[...], g, m_ref[...], v_ref[...])
        outs[-2][...] = total[LOSS_ROW:LOSS_ROW + 1, :LANES]
        outs[-1][...] = total[SMALL_ROWS:, :]

    shapes = [jax.ShapeDtypeStruct((rows, cols), F32) for _, _, rows, cols in SMALL_LAYOUT for _ in range(4)]
    shapes += [jax.ShapeDtypeStruct((1, LANES), F32), jax.ShapeDtypeStruct((PACK_ROWS - SMALL_ROWS, D_MODEL), F32)]
    operands = [a for item in SMALL_LAYOUT for a in state[item[0]]]
    res = pl.pallas_call(body, name=name, out_shape=shapes, compiler_params=_params())(all_packs, *operands)
    per_name = {item[0]: tuple(res[4 * j:4 * j + 4]) for j, item in enumerate(SMALL_LAYOUT)}
    return per_name, res[-2][0, 0], res[-1]


def kernel(x, p, norm_g, w_in, attn_out_g, dw_w, dw_b, conv_ln_g, conv_ln_b, w_pw, conv_out_g, w_out, ple_norm_g, w_ple_gate, w_ple, final_g, loss_target, m_norm_g, m_w_in, m_attn_out_g, m_dw_w, m_dw_b, m_conv_ln_g, m_conv_ln_b, m_w_pw, m_conv_out_g, m_w_out, m_ple_norm_g, m_w_ple_gate, m_w_ple, m_final_g, v_norm_g, v_w_in, v_attn_out_g, v_dw_w, v_dw_b, v_conv_ln_g, v_conv_ln_b, v_w_pw, v_conv_out_g, v_w_out, v_ple_norm_g, v_w_ple_gate, v_w_ple, v_final_g):
    depth = w_in.shape[0]
    T = x.shape[1]
    given = dict(
        norm_g=norm_g, ple_norm_g=ple_norm_g, final_g=final_g, dw_b=dw_b, conv_ln_g=conv_ln_g, conv_ln_b=conv_ln_b,
        conv_out_g=conv_out_g, attn_out_g=attn_out_g,
        m_norm_g=m_norm_g, m_ple_norm_g=m_ple_norm_g, m_final_g=m_final_g, m_dw_b=m_dw_b, m_conv_ln_g=m_conv_ln_g,
        m_conv_ln_b=m_conv_ln_b, m_conv_out_g=m_conv_out_g, m_attn_out_g=m_attn_out_g,
        v_norm_g=v_norm_g, v_ple_norm_g=v_ple_norm_g, v_final_g=v_final_g, v_dw_b=v_dw_b, v_conv_ln_g=v_conv_ln_g,
        v_conv_ln_b=v_conv_ln_b, v_conv_out_g=v_conv_out_g, v_attn_out_g=v_attn_out_g)
    my_idx = 4 * lax.axis_index("x") + 2 * lax.axis_index("y") + lax.axis_index("c")

    ids = jnp.arange(BLK)
    tri = (ids[:, None] >= ids[None, :]).astype(BF16)
    tri_t = (ids[:, None] <= ids[None, :]).astype(BF16)
    hid = jnp.arange(ATTN_DIM) // HEAD_DIM
    head_mean = ((hid[:, None] == hid[None, :]).astype(F32) / HEAD_DIM).astype(BF16)

    w_names = ("w_in_t", "w_pw", "w_out", "w_gate", "w_ple")
    w_axes = dict(zip(w_names, (0, 0, 0, 0, 1)))
    shards = [dict(zip(w_names, (w_in[l].T.astype(BF16), w_pw[l].astype(BF16), w_out[l].astype(BF16),
                                 w_ple_gate[l].astype(BF16), w_ple[l].astype(BF16)))) for l in range(depth)]
    first = _all_gather([shards[0][n] for n in w_names] + [dw_w[l].T for l in range(depth)],
                        [w_axes[n] for n in w_names] + [0] * depth, "gather_weights_0")
    layers = []
    for l in range(depth):
        layers.append(dict(
            dw_w=first[len(w_names) + l].T,
            g_norm=norm_g[l][None], g_attn=jnp.tile(attn_out_g[l], N_HEADS)[None], dw_b=dw_b[l][None],
            ln_g=conv_ln_g[l][None], ln_b=conv_ln_b[l][None], g_conv=conv_out_g[l][None],
            g_ple=ple_norm_g[l][None], p=p[l, 0]))
    layers[0].update(zip(w_names, first))

    h = x[0]
    saved = []
    for l, w in enumerate(layers):
        nxt = [None] if l + 1 < depth else []

        def quarter(i, l=l, nxt=nxt):
            return [_Ride.gather(shards[l + 1]["w_in_t"], 0, nxt[0], *W_IN_QUARTERS[i])] if nxt else []

        def others(where, l=l, w=w):
            plan = OTHER_WEIGHTS_ON[where] if l > 0 else ()
            return [_Ride.gather(shards[l][n], w_axes[n], w.get(n), *rows) for n, rows in plan]

        def keep(where, landed, l=l, w=w, nxt=nxt):
            plan = OTHER_WEIGHTS_ON[where] if l > 0 else ()
            nxt[:1] = landed[:len(landed) - len(plan)]
            w.update(zip([n for n, _ in plan], landed[len(landed) - len(plan):]))

        (qs, k, v, ug, hn), landed = _prenorm_inproj(h, w["g_norm"], w["w_in_t"], f"inproj_{l}",
                                                     _Ride(quarter(0) + others("inproj")))
        keep("inproj", landed)
        (o, cs), landed = _attn_fwd(qs, k, v, tri, f"attn_fwd_{l}", _Ride(quarter(1) + others("attn")))
        keep("attn", landed)
        (conv, c2), landed = _conv_fwd(ug, w["dw_w"], w["dw_b"], w["ln_g"], w["ln_b"], f"conv_fwd_{l}",
                                       _Ride(quarter(2) + others("conv")))
        keep("conv", landed)
        (h2, h1, ycat, hn2, gate, e, c3), landed = _mix_out_ple(
            o, ug, c2, h, w["p"], head_mean, w["g_attn"], w["g_conv"], w["g_ple"],
            w["w_pw"], w["w_out"], w["w_gate"], w["w_ple"], f"mix_{l}", _Ride(quarter(3)))
        if l + 1 < depth:
            layers[l + 1]["w_in_t"] = landed[0]
        saved.append(dict(h=h, qs=qs, k=k, v=v, ug=ug, hn=hn, o=o, cs=cs, conv=conv, c2=c2, h1=h1,
                          ycat=ycat, hn2=hn2, gate=gate, e=e, c3=c3))
        h = h2
    dh, g_final, loss_part = _final_loss(h, loss_target[0], final_g[None], "final_loss")

    small = {}
    dww_parts = [None] * depth
    slots = [dict() for _ in range(depth)]
    g_w_in = None
    for l in reversed(range(depth)):
        w, s = layers[l], saved[l]
        above = [None] if g_w_in is not None else []

        def part(i, above=above, g=g_w_in):
            return [_Ride.scatter(g, 0, above[0], *W_IN_GRAD_PARTS[i])] if above else []

        def scattered(grads, names):
            return [_Ride.scatter(grads[n], w_axes[n]) for n in names]

        (dh1, dh1b, dzg, de, dycat, g_ple_sum), landed = _ple_out_bwd(
            dh, s["gate"], s["e"], s["h1"], w["g_ple"], w["w_gate"], w["w_out"], f"ple_bwd_{l}", _Ride(part(0)))
        above[:1] = landed
        (do, dga, dgc, dc3, dconv, sums), landed = _branch_bwd(
            dycat, s["o"], s["ug"], s["c3"], s["conv"], head_mean, w["g_attn"], w["g_conv"],
            w["ln_g"], w["ln_b"], w["w_pw"], f"branch_bwd_{l}", _Ride(part(1)))
        above[:1] = landed
        grads = dict(
            w_pw=_weight_grad([s["c2"]], dc3, f"grad_w_pw_{l}")[0],
            w_out=_weight_grad([s["ycat"]], dh1b, f"grad_w_out_{l}")[0],
            w_gate=_weight_grad([s["hn2"]], dzg, f"grad_w_gate_{l}")[0],
            w_ple=_weight_grad([w["p"].astype(BF16)], de, f"grad_w_ple_{l}", tk=PLE_DIM)[0])
        (dcv, dcg, dww), landed = _conv_bwd(dconv, s["ug"], w["dw_w"], f"conv_bwd_{l}", _Ride(part(2)))
        above[:1] = landed
        (dq, dk, dv), landed = _attn_bwd(s["qs"], s["k"], s["v"], do, s["cs"], tri, tri_t, f"attn_bwd_{l}",
                                         _Ride(scattered(grads, w_names[1:])))
        slots[l].update(zip(w_names[1:], landed))
        du = [dq, dk, dv, dga, dcv, dcg, dgc]
        g_w_in_here, landed = _weight_grad(du, s["hn"], f"grad_w_in_{l}", ride=_Ride(part(3)))
        if above:
            slots[l + 1]["w_in_t"] = landed[0]
        tail = [_Ride.scatter(g_w_in_here, 0)] if l == 0 else []
        (dh, g_norm_sum), landed = _inproj_bwd(du, w["w_in_t"], s["h"], dh1, w["g_norm"], f"inproj_bwd_{l}",
                                               _Ride(tail))
        slots[l].update(zip(("w_in_t",), landed))
        g_w_in = g_w_in_here
        small[l] = dict(norm_g=g_norm_sum, ple_norm_g=g_ple_sum, attn_out_g=sums[0].reshape(N_HEADS, HEAD_DIM).sum(0),
                        conv_out_g=sums[1], conv_ln_g=sums[2], conv_ln_b=sums[3], dw_b=sums[4])
        dww_parts[l] = dww[:CONV_WIDTH]
    slots = [[sl[n] for n in w_names] for sl in slots]
    grad_x = dh[None]

    sums_of = {name: jnp.stack([small[l][name].reshape(-1) for l in range(depth)]) for name in small[0]}
    sums_of["final_g"] = g_final
    pack = jnp.concatenate([_pack_small(sums_of, scalar=loss_part[0, 0]), jnp.concatenate(dww_parts, axis=1),
                            jnp.zeros((PACK_ROWS - SMALL_ROWS - CONV_WIDTH, D_MODEL), F32)], axis=0)
    (all_packs,) = _all_gather([pack], [0], "gather_small_grads")
    state = {name: [given[pre + name].reshape(rows, cols) for pre in ("", "m_", "v_")]
             for name, _, rows, cols in SMALL_LAYOUT}
    updated, loss, dww_sum = _small_update(all_packs.reshape(N_DEV, PACK_ROWS, D_MODEL), state, "update_small")
    res = {kind: {name: val[k].reshape(given[name].shape) for name, val in updated.items()}
           for k, kind in enumerate("gdmv")}
    dww_full = dww_sum[:CONV_WIDTH].reshape(CONV_WIDTH, depth, CONV_DIM).transpose(1, 0, 2)
    g_dw_w = lax.dynamic_slice_in_dim(dww_full, my_idx * (CONV_DIM // N_DEV), CONV_DIM // N_DEV, axis=2)

    swap = lambda a: a.transpose(0, 2, 1)
    state = {"w_in": (w_in, m_w_in, v_w_in), "w_pw": (w_pw, m_w_pw, v_w_pw), "w_out": (w_out, m_w_out, v_w_out),
             "w_ple_gate": (w_ple_gate, m_w_ple_gate, v_w_ple_gate), "w_ple": (w_ple, m_w_ple, v_w_ple)}
    for at, name in enumerate(state):
        wv, mv, vv = [swap(a) for a in state[name]] if name == "w_in" else state[name]
        out = _sum_adamw([slots[l][at] for l in range(depth)], wv, mv, vv, f"adamw_{name}")
        out = [swap(a) for a in out] if name == "w_in" else out
        res["g"][name], res["d"][name], res["m"][name], res["v"][name] = out
    flat = lambda a: a.reshape(-1, a.shape[-1])
    res["g"]["dw_w"] = g_dw_w
    res["d"]["dw_w"], res["m"]["dw_w"], res["v"]["dw_w"] = [
        a.reshape(dw_w.shape) for a in _adamw(flat(dw_w), flat(g_dw_w), flat(m_dw_w), flat(v_dw_w), "adamw_dw_w")]

    order = ["norm_g", "w_in", "attn_out_g", "dw_w", "dw_b", "conv_ln_g", "conv_ln_b", "w_pw", "conv_out_g",
             "w_out", "ple_norm_g", "w_ple_gate", "w_ple", "final_g"]
    return (loss, grad_x, *[res["g"][n] for n in order], *[res["d"][n] for n in order],
            *[res["m"][n] for n in order], *[res["v"][n] for n in order])
```

```python
import functools

import jax
import jax.numpy as jnp
from jax import lax
from jax.experimental import pallas as pl
from jax.experimental.pallas import tpu as pltpu

F32 = jnp.float32
BF16 = jnp.bfloat16
MESH = pl.DeviceIdType.MESH

N_DEV = 8
D_MODEL = 1024
ATTN_DIM = 512
CONV_DIM = 512
HEAD_DIM = 64
N_HEADS = 8
CONV_WIDTH = 31
PLE_DIM = 256
CHUNK = 512
N_CHUNK = 7
EPS = 1e-6
ADAM_LR = 0.001
ADAM_B1 = 0.9
ADAM_B2 = 0.999
ADAM_EPS = 1e-08
ADAM_WD = 0.01
ADAM_STEP = 10

LANES = 128
BLK = 256
ATT_COLS = 4
CHAIN_GROUP = 4
SOFTPLUS_LINEAR_AT = 20.0
DEAD_AT = 110.0
FIRST_BLOCK_LANE = HEAD_DIM - 1
TM = 512
HALO = 32
SUBLANES = 8
CONV_ROWS = 32
ADAMW_ROWS = 64
VMEM_LIMIT = 56 * 1024 * 1024
SMALL_ROWS = 16
SMALL_LAYOUT = (("norm_g", 0, 2, D_MODEL), ("ple_norm_g", 2, 2, D_MODEL), ("final_g", 4, 1, D_MODEL),
                ("dw_b", 5, 2, CONV_DIM), ("conv_ln_g", 7, 2, CONV_DIM), ("conv_ln_b", 9, 2, CONV_DIM),
                ("conv_out_g", 11, 2, CONV_DIM), ("attn_out_g", 13, 2, HEAD_DIM))
LOSS_ROW = 15
W_IN_QUARTERS = ((0, 96), (96, 144), (240, 96), (336, 112))
W_IN_GRAD_PARTS = ((0, 96), (96, 80), (176, 144), (320, 128))
OTHER_WEIGHTS_ON = {"inproj": (("w_gate", (0, 64)),), "attn": (("w_out", ()), ("w_ple", ())),
                    "conv": (("w_pw", ()), ("w_gate", (64, 64)))}
PACK_ROWS = 48


def _nn(a, b):
    return lax.dot_general(a, b, (((1,), (0,)), ((), ())), preferred_element_type=F32)


def _nt(a, b):
    return lax.dot_general(a, b, (((1,), (1,)), ((), ())), preferred_element_type=F32)


def _tn(a, b):
    return lax.dot_general(a, b, (((0,), (0,)), ((), ())), preferred_element_type=F32)


def _split(x):
    hi = x.astype(BF16)
    lo = (x - hi.astype(F32)).astype(BF16)
    return hi, lo


def _dot_hilo(x, m):
    hi, lo = _split(x)
    return _nn(hi, m) + _nn(lo, m)


def _sigmoid(x):
    return jax.nn.sigmoid(x)


def _dsilu(x, s):
    return s * (1.0 + x * (1.0 - s))


def _params(n_grid=0, vmem=VMEM_LIMIT):
    sem = ("arbitrary",) * n_grid if n_grid else None
    return pltpu.CompilerParams(dimension_semantics=sem, vmem_limit_bytes=vmem)


def _rows(tm, cols, col=0):
    return pl.BlockSpec((tm, cols), lambda i: (i, col))


def _whole(shape):
    zeros = (0,) * len(shape)
    return pl.BlockSpec(shape, lambda *_: zeros)


def _my_position():
    return lax.axis_index("x"), lax.axis_index("y"), lax.axis_index("c")


def _block(ref, axis, idx, size):
    start = pl.multiple_of(idx * size, size)
    if axis == 0:
        return ref.at[pl.ds(start, size), :]
    return ref.at[:, pl.ds(start, size)]


def _all_gather(shards, axes, name):
    n = len(shards)
    sizes = [s.shape[a] for s, a in zip(shards, axes)]

    def full_shape(s, a):
        shape = list(s.shape)
        shape[a] *= N_DEV
        return jax.ShapeDtypeStruct(tuple(shape), s.dtype)

    def body(*refs):
        ins, outs = refs[:n], refs[n:2 * n]
        send_sems, recv_sems, local_sems = refs[2 * n:]
        x, y, c = _my_position()
        me, sibling = (x, y, c), (x, y, 1 - c)
        chips = [(1 - x, y), (x, 1 - y), (1 - x, 1 - y)]

        def place(i, dev):
            return _block(outs[i], axes[i], 4 * dev[0] + 2 * dev[1] + dev[2], sizes[i])

        def copy(k, i, dev, to, src=None):
            return pltpu.make_async_remote_copy(
                src_ref=place(i, dev) if src is None else src, dst_ref=place(i, dev),
                send_sem=send_sems.at[k, i], recv_sem=recv_sems.at[k, i],
                device_id=to, device_id_type=MESH)

        mine = [pltpu.make_async_copy(ins[i], place(i, me), local_sems.at[i]) for i in range(n)]
        for cp in mine:
            cp.start()
        first = [copy(0, i, me, sibling, src=ins[i]) for i in range(n)]
        for j, chip in enumerate(chips):
            first += [copy(1 + j, i, me, (*chip, c), src=ins[i]) for i in range(n)]
        for cp in first:
            cp.start()
        passed = []
        for j, chip in enumerate(chips):
            for i in range(n):
                copy(1 + j, i, (*chip, c), me).wait_recv()
            hop = [copy(4 + j, i, (*chip, c), sibling) for i in range(n)]
            for cp in hop:
                cp.start()
            passed += hop
        for i in range(n):
            copy(0, i, sibling, me).wait_recv()
        for j, chip in enumerate(chips):
            for i in range(n):
                copy(4 + j, i, (*chip, 1 - c), me).wait_recv()
        for cp in first + passed:
            cp.wait_send()
        for cp in mine:
            cp.wait()

    any_spec = pl.BlockSpec(memory_space=pl.ANY)
    return pl.pallas_call(
        body, name=name,
        out_shape=[full_shape(s, a) for s, a in zip(shards, axes)],
        in_specs=[any_spec] * n, out_specs=[any_spec] * n,
        scratch_shapes=[pltpu.SemaphoreType.DMA((7, n)), pltpu.SemaphoreType.DMA((7, n)),
                        pltpu.SemaphoreType.DMA((n,))],
    )(*shards)


def _pair_reduce(g, name):
    n_chips = N_DEV // 2
    R, C = g.shape[0] // N_DEV, g.shape[1]

    def body(g_ref, out_ref, mine_ref, theirs_ref, send_sems, recv_sems, local_sems):
        x, y, c = _my_position()
        block = lambda d: g_ref.at[pl.ds(pl.multiple_of(d * R, 16), R), :]
        sends = [pltpu.make_async_remote_copy(
            src_ref=block(2 * j + 1 - c), dst_ref=theirs_ref.at[j], send_sem=send_sems.at[j],
            recv_sem=recv_sems.at[j], device_id=(x, y, 1 - c), device_id_type=MESH) for j in range(n_chips)]
        own = [pltpu.make_async_copy(block(2 * j + c), mine_ref.at[j], local_sems.at[j]) for j in range(n_chips)]
        for cp in sends + own:
            cp.start()
        for j in range(n_chips):
            own[j].wait()
            sends[j].wait_recv()
            out_ref[j] = (mine_ref[j].astype(F32) + theirs_ref[j].astype(F32)).astype(g.dtype)
        for cp in sends:
            cp.wait_send()

    half = pltpu.VMEM((n_chips, R, C), g.dtype)
    sems = pltpu.SemaphoreType.DMA((n_chips,))
    return pl.pallas_call(
        body, name=name, out_shape=jax.ShapeDtypeStruct((n_chips, R, C), g.dtype),
        in_specs=[pl.BlockSpec(memory_space=pl.ANY)], out_specs=pl.BlockSpec(memory_space=pltpu.VMEM),
        scratch_shapes=[half, half, sems, sems, sems], compiler_params=_params(),
    )(g)


class _Ride:
    def __init__(self, parts):
        self.parts = [p for p in parts if p is not None]

    @staticmethod
    def gather(src, axis, land=None, lo=0, n=None):
        return ("gather", src, land, axis, lo, src.shape[axis] if n is None else n)

    @staticmethod
    def scatter(src, axis, land=None, lo=0, n=None):
        return ("scatter", src, land, axis, lo, src.shape[axis] // N_DEV if n is None else n)

    @staticmethod
    def scatter_chips(chip_sums):
        return ("scatter_chips", chip_sums, None, 0, 0, chip_sums.shape[1])

    def arrays(self):
        return [p[1] for p in self.parts] + [p[2] for p in self.parts if p[2] is not None]

    def out_shapes(self):
        out = []
        for kind, src, _, axis, _, _ in self.parts:
            shape = list(src.shape)
            if kind == "gather":
                shape[axis] *= N_DEV
            elif kind == "scatter_chips":
                pass
            else:
                shape[axis] //= N_DEV
                shape = [N_DEV] + shape
            out.append(jax.ShapeDtypeStruct(tuple(shape), src.dtype))
        return out

    def aliases(self, n_in, n_out):
        m, out = len(self.parts), {}
        for j, p in enumerate(self.parts):
            if p[2] is not None:
                out[n_in + m + len(out)] = n_out + j
        return out

    def scratch(self):
        m = len(self.parts)
        return [pltpu.SemaphoreType.DMA((N_DEV - 1, m)), pltpu.SemaphoreType.DMA((N_DEV - 1, m)),
                pltpu.SemaphoreType.DMA((m,))]

    def _copies(self, src_refs, land_refs, sems):
        send_sems, recv_sems, local_sems = sems
        x, y, c = _my_position()
        my_idx = 4 * x + 2 * y + c
        own, sends, lands = [], [], []
        for j, (kind, src, _, axis, lo, n) in enumerate(self.parts):
            if kind == "scatter_chips":
                for k in (0, 2, 4, 6):
                    px, py = (1 - x if k & 4 else x), (1 - y if k & 2 else y)
                    a, b = src_refs[j].at[2 * px + py], land_refs[j].at[2 * x + y]
                    if k == 0:
                        own.append(pltpu.make_async_copy(a, b, local_sems.at[j]))
                        continue
                    mk = lambda dst, a=a, k=k, j=j, to=(px, py, c): pltpu.make_async_remote_copy(
                        src_ref=a, dst_ref=dst, send_sem=send_sems.at[k - 1, j], recv_sem=recv_sems.at[k - 1, j],
                        device_id=to, device_id_type=MESH)
                    sends.append(mk(b))
                    lands.append(mk(land_refs[j].at[2 * px + py]))
                continue
            size = src.shape[axis] if kind == "gather" else src.shape[axis] // N_DEV
            align = 16 if axis == 0 else LANES

            def rows(ref, idx, lead=None, axis=axis, lo=lo, n=n, size=size, align=align):
                at = pl.ds(pl.multiple_of(idx * size + lo, align), n)
                where = (at, slice(None)) if axis == 0 else (slice(None), at)
                return ref.at[where] if lead is None else ref.at[(lead, *where)]

            def in_shard(ref):
                return rows(ref, 0)

            def in_slot(ref, s):
                return rows(ref, 0, lead=s)

            for k in range(N_DEV):
                px = 1 - x if k & 4 else x
                py = 1 - y if k & 2 else y
                pc = 1 - c if k & 1 else c
                peer_idx = 4 * px + 2 * py + pc
                if kind == "gather":
                    a, b, landed = in_shard(src_refs[j]), rows(land_refs[j], my_idx), rows(land_refs[j], peer_idx)
                else:
                    a, b, landed = rows(src_refs[j], peer_idx), in_slot(land_refs[j], my_idx), in_slot(land_refs[j], peer_idx)
                if k == 0:
                    own.append(pltpu.make_async_copy(a, b, local_sems.at[j]))
                    continue
                mk = lambda dst, a=a, k=k, j=j, to=(px, py, pc): pltpu.make_async_remote_copy(
                    src_ref=a, dst_ref=dst, send_sem=send_sems.at[k - 1, j], recv_sem=recv_sems.at[k - 1, j],
                    device_id=to, device_id_type=MESH)
                sends.append(mk(b))
                lands.append(mk(landed))
        return own, sends, lands

    def start(self, src_refs, land_refs, sems):
        own, sends, _ = self._copies(src_refs, land_refs, sems)
        for cp in own + sends:
            cp.start()

    def wait(self, src_refs, land_refs, sems):
        own, sends, lands = self._copies(src_refs, land_refs, sems)
        for cp in lands:
            cp.wait_recv()
        for cp in sends:
            cp.wait_send()
        for cp in own:
            cp.wait()


def _call(body, *, name, grid, in_specs, out_specs, out_shape, args, scratch_shapes=(), ride=None):
    in_specs, out_specs, out_shape = list(in_specs), list(out_specs), list(out_shape)
    n_in, n_out, n_sc = len(in_specs), len(out_specs), len(scratch_shapes)
    if ride is None or not ride.parts:
        res = pl.pallas_call(body, name=name, grid=grid, in_specs=in_specs, out_specs=out_specs,
                             out_shape=out_shape, scratch_shapes=list(scratch_shapes),
                             compiler_params=_params(len(grid)))(*args)
        return list(res), []
    extra, m = ride.arrays(), len(ride.parts)

    def riding(*refs):
        a = n_in + len(extra)
        b = a + n_out
        srcs, lands, sems = refs[n_in:n_in + m], refs[b:b + m], refs[b + m + n_sc:]
        at = [pl.program_id(d) for d in range(len(grid))]

        @pl.when(functools.reduce(jnp.logical_and, [i == 0 for i in at]))
        def _():
            ride.start(srcs, lands, sems)

        body(*refs[:n_in], *refs[a:b], *refs[b + m:b + m + n_sc])

        @pl.when(functools.reduce(jnp.logical_and, [i == g - 1 for i, g in zip(at, grid)]))
        def _():
            ride.wait(srcs, lands, sems)

    hbm = pl.BlockSpec(memory_space=pl.ANY)
    res = pl.pallas_call(
        riding, name=name, grid=grid, in_specs=in_specs + [hbm] * len(extra), out_specs=out_specs + [hbm] * m,
        out_shape=out_shape + ride.out_shapes(), scratch_shapes=list(scratch_shapes) + ride.scratch(),
        input_output_aliases=ride.aliases(n_in, n_out), compiler_params=_params(len(grid)),
    )(*args, *extra)
    return list(res[:n_out]), list(res[n_out:])


def _prenorm_inproj(h, gain, w_in_t, name, ride=None):
    T = h.shape[0]

    def body(h_ref, g_ref, w_ref, q_ref, k_ref, v_ref, ug_ref, hn_ref):
        hv = h_ref[...]
        r = lax.rsqrt(jnp.mean(hv * hv, axis=-1, keepdims=True) + EPS)
        hn = (hv * r * g_ref[...]).astype(BF16)
        hn_ref[...] = hn
        for j in range(N_CHUNK):
            u = _nt(hn, w_ref[j * CHUNK:(j + 1) * CHUNK, :])
            if j == 0:
                q_ref[...] = (u * (HEAD_DIM ** -0.5)).astype(BF16)
            elif j == 1:
                k_ref[...] = u.astype(BF16)
            elif j == 2:
                v_ref[...] = u.astype(BF16)
            else:
                ug_ref[:, (j - 3) * CHUNK:(j - 2) * CHUNK] = u

    act = jax.ShapeDtypeStruct((T, CHUNK), BF16)
    return _call(
        body, name=name, grid=(T // TM,),
        in_specs=[_rows(TM, D_MODEL), _whole((1, D_MODEL)), _whole((N_CHUNK * CHUNK, D_MODEL))],
        out_specs=[_rows(TM, CHUNK)] * 3 + [_rows(TM, 4 * CHUNK), _rows(TM, D_MODEL)],
        out_shape=[act, act, act, jax.ShapeDtypeStruct((T, 4 * CHUNK), F32),
                   jax.ShapeDtypeStruct((T, D_MODEL), BF16)],
        args=(h, gain, w_in_t,), ride=ride)


def _softplus_parts(z):
    ez = jnp.exp(jnp.minimum(z, SOFTPLUS_LINEAR_AT))
    t = 1.0 + ez
    return ez * pl.reciprocal(t, approx=True), jnp.where(z > SOFTPLUS_LINEAR_AT, z, jnp.log(t))


def _attn_fwd(qs, k, v, tri, name, ride=None):
    T = qs.shape[0]
    assert T // BLK <= FIRST_BLOCK_LANE, "one lane per key block below the lane of the first block"
    width = LANES * ATT_COLS
    chains = [(c, half) for c in range(ATT_COLS) for half in range(2)]

    def body(q_ref, k_ref, v_ref, m_ref, o_ref, cs_ref):
        qi = pl.program_id(1)
        lane = lax.broadcasted_iota(jnp.int32, (BLK, LANES), 1)
        first = lane < HEAD_DIM
        causal = (lax.broadcasted_iota(jnp.int32, (BLK, BLK), 1)
                  < lax.broadcasted_iota(jnp.int32, (BLK, BLK), 0))
        tri_m = m_ref[...]
        qh = {}
        for c in range(ATT_COLS):
            q = q_ref[:, c * LANES:(c + 1) * LANES]
            zero = jnp.zeros_like(q)
            qh[c, 0], qh[c, 1] = jnp.where(first, q, zero), jnp.where(first, zero, q)

        def step(kb, state, masked):
            carries, accs, cvals = state
            start = pl.multiple_of(kb * BLK, BLK)
            kblk = [k_ref[pl.ds(start, BLK), c * LANES:(c + 1) * LANES] for c in range(ATT_COLS)]
            vblk = [v_ref[pl.ds(start, BLK), c * LANES:(c + 1) * LANES] for c in range(ATT_COLS)]
            carries, accs, cvals = list(carries), list(accs), list(cvals)
            for g0 in range(0, len(chains), CHAIN_GROUP):
                ids = range(g0, g0 + CHAIN_GROUP)
                z = [_nt(qh[chains[n]], kblk[chains[n][0]]) for n in ids]
                sp = [_softplus_parts(zi)[1] for zi in z]
                if masked:
                    sp = [jnp.where(causal, s, 0.0) for s in sp]
                incl = [_dot_hilo(s, tri_m) for s in sp]
                a = [jnp.exp(zi - ii - carries[n]) for n, zi, ii in zip(ids, z, incl)]
                if masked:
                    a = [jnp.where(causal, ai, 0.0) for ai in a]
                for n, ai, ii in zip(ids, a, incl):
                    c, half = chains[n]
                    zero = jnp.zeros_like(vblk[c])
                    vh = jnp.where(first, vblk[c], zero) if half == 0 else jnp.where(first, zero, vblk[c])
                    accs[c] = accs[c] + _nn(ai.astype(BF16), vh)
                    cvals[c] = jnp.where(lane == kb + HEAD_DIM * half, carries[n], cvals[c])
                    carries[n] = carries[n] + ii[:, 0:1]
            return tuple(carries), tuple(accs), tuple(cvals)

        zeros = tuple(jnp.zeros((BLK, LANES), F32) for _ in range(ATT_COLS))
        state = (tuple(jnp.zeros((BLK, 1), F32) for _ in chains), zeros, zeros)
        state = step(qi, state, True)

        def reaches_further(st):
            it, (carries, _, _) = st
            least = functools.reduce(jnp.minimum, carries)
            return jnp.logical_and(it < qi, jnp.min(least) < DEAD_AT)

        done, state = lax.while_loop(reaches_further, lambda st: (st[0] + 1, step(qi - 1 - st[0], st[1], False)),
                                     (jnp.int32(0), state))
        first_block = (qi - done).astype(F32)
        for c in range(ATT_COLS):
            o_ref[:, c * LANES:(c + 1) * LANES] = state[1][c]
            cs_ref[:, c * LANES:(c + 1) * LANES] = jnp.where(lane == FIRST_BLOCK_LANE, first_block, state[2][c])

    blk = pl.BlockSpec((BLK, width), lambda j, i: (i, j))
    col = pl.BlockSpec((T, width), lambda j, i: (0, j))
    out = jax.ShapeDtypeStruct((T, ATTN_DIM), F32)
    return _call(
        body, name=name, grid=(ATTN_DIM // width, T // BLK),
        in_specs=[blk, col, col, _whole((BLK, BLK))],
        out_specs=[blk, blk], out_shape=[out, out],
        args=(qs, k, v, tri,), ride=ride)


def _shifted_copies(pad_ref, sh_ref):
    rows = sh_ref.shape[1]
    for b in range(SUBLANES):
        sh_ref[b] = pad_ref[b:b + rows, :]


def _shift_of(offset):
    return offset % SUBLANES, offset - offset % SUBLANES


def _conv_fwd(ug, dw_w, dw_b, ln_g, ln_b, name, ride=None):
    T = ug.shape[0]
    per = TM // HALO

    def body(cv_ref, cg_ref, cvh_ref, cgh_ref, w_ref, b_ref, g_ref, beta_ref, conv_ref, c2_ref, pad_ref, sh_ref):
        i = pl.program_id(0)
        halo = cvh_ref[...] * _sigmoid(cgh_ref[...])
        pad_ref[0:HALO, :] = jnp.where(i == 0, 0.0, halo)
        pad_ref[HALO:HALO + TM, :] = cv_ref[...] * _sigmoid(cg_ref[...])
        pad_ref[HALO + TM:, :] = jnp.zeros((SUBLANES, CONV_DIM), F32)
        _shifted_copies(pad_ref, sh_ref)
        taps = [w_ref[t:t + 1, :] for t in range(CONV_WIDTH)]

        def rows(j, _):
            r = pl.multiple_of(j * CONV_ROWS, CONV_ROWS)
            acc = jnp.zeros((CONV_ROWS, CONV_DIM), F32) + b_ref[...]
            for t in range(CONV_WIDTH):
                b, a = _shift_of(HALO - (CONV_WIDTH - 1) + t)
                acc = acc + taps[t] * sh_ref[b, pl.ds(r + a, CONV_ROWS), :]
            conv_ref[pl.ds(r, CONV_ROWS), :] = acc
            return 0

        lax.fori_loop(0, TM // CONV_ROWS, rows, 0)
        acc = conv_ref[...]
        mu = jnp.mean(acc, axis=-1, keepdims=True)
        xc = acc - mu
        rs = lax.rsqrt(jnp.mean(xc * xc, axis=-1, keepdims=True) + EPS)
        ln = xc * rs * g_ref[...] + beta_ref[...]
        c2_ref[...] = (ln * _sigmoid(ln)).astype(BF16)

    prev = lambda col: pl.BlockSpec((HALO, CHUNK), lambda i: (jnp.maximum(i * per - 1, 0), col))
    vec = _whole((1, CONV_DIM))
    return _call(
        body, name=name, grid=(T // TM,),
        in_specs=[_rows(TM, CHUNK, 1), _rows(TM, CHUNK, 2), prev(1), prev(2),
                  _whole((CONV_WIDTH, CONV_DIM)), vec, vec, vec],
        out_specs=[_rows(TM, CONV_DIM), _rows(TM, CONV_DIM)],
        out_shape=[jax.ShapeDtypeStruct((T, CONV_DIM), F32), jax.ShapeDtypeStruct((T, CONV_DIM), BF16)],
        scratch_shapes=[pltpu.VMEM((TM + HALO + SUBLANES, CONV_DIM), F32),
                        pltpu.VMEM((SUBLANES, TM + HALO, CONV_DIM), F32)],
        args=(ug, ug, ug, ug, dw_w, dw_b, ln_g, ln_b,), ride=ride)


def _mix_out_ple(o, ug, c2, h, p, head_mean, g_attn, g_conv, g_ple, w_pw, w_out, w_gate, w_ple, name, ride=None):
    T = h.shape[0]

    def body(o_ref, ga_ref, gc_ref, c2_ref, h_ref, p_ref, hm_ref, gao_ref, gco_ref, gpn_ref,
             wpw_ref, wout_ref, wg_ref, wple_ref,
             h2_ref, h1_ref, ycat_ref, hn2_ref, gate_ref, e_ref, c3_ref):
        ov = o_ref[...]
        rh = lax.rsqrt(_nn((ov * ov).astype(BF16), hm_ref[...]) + EPS)
        ga = ga_ref[...]
        ya = (ov * rh * gao_ref[...] * (ga * _sigmoid(ga))).astype(BF16)
        c3 = _nn(c2_ref[...], wpw_ref[...])
        c3_ref[...] = c3
        rc = lax.rsqrt(jnp.mean(c3 * c3, axis=-1, keepdims=True) + EPS)
        gc = gc_ref[...]
        yc = (c3 * rc * gco_ref[...] * (gc * _sigmoid(gc))).astype(BF16)
        ycat_ref[:, :ATTN_DIM] = ya
        ycat_ref[:, ATTN_DIM:] = yc
        h1 = h_ref[...] + _nn(ya, wout_ref[:ATTN_DIM, :]) + _nn(yc, wout_ref[ATTN_DIM:, :])
        h1_ref[...] = h1
        r1 = lax.rsqrt(jnp.mean(h1 * h1, axis=-1, keepdims=True) + EPS)
        hn2 = (h1 * r1 * gpn_ref[...]).astype(BF16)
        hn2_ref[...] = hn2
        gate = _sigmoid(_nn(hn2, wg_ref[...]))
        e = _nn(p_ref[...].astype(BF16), wple_ref[...])
        gate_ref[...] = gate
        e_ref[...] = e
        h2_ref[...] = h1 + e * gate

    f32 = lambda cols: jax.ShapeDtypeStruct((T, cols), F32)
    bf = lambda cols: jax.ShapeDtypeStruct((T, cols), BF16)
    return _call(
        body, name=name, grid=(T // TM,),
        in_specs=[_rows(TM, ATTN_DIM), _rows(TM, CHUNK, 0), _rows(TM, CHUNK, 3), _rows(TM, CONV_DIM),
                  _rows(TM, D_MODEL), _rows(TM, PLE_DIM), _whole((ATTN_DIM, ATTN_DIM)),
                  _whole((1, ATTN_DIM)), _whole((1, CONV_DIM)), _whole((1, D_MODEL)),
                  _whole((CONV_DIM, CONV_DIM)), _whole((D_MODEL, D_MODEL)), _whole((D_MODEL, D_MODEL)),
                  _whole((PLE_DIM, D_MODEL))],
        out_specs=[_rows(TM, D_MODEL), _rows(TM, D_MODEL), _rows(TM, D_MODEL), _rows(TM, D_MODEL),
                   _rows(TM, D_MODEL), _rows(TM, D_MODEL), _rows(TM, CONV_DIM)],
        out_shape=[f32(D_MODEL), f32(D_MODEL), bf(D_MODEL), bf(D_MODEL), f32(D_MODEL), f32(D_MODEL),
                   f32(CONV_DIM)],
        args=(o, ug, ug, c2, h, p, head_mean, g_attn, g_conv, g_ple, w_pw, w_out, w_gate, w_ple,), ride=ride)


def _final_loss(h, target, gain, name):
    T = h.shape[0]

    def body(h_ref, t_ref, g_ref, dh_ref, gsum_ref, loss_ref):
        @pl.when(pl.program_id(0) == 0)
        def _():
            gsum_ref[...] = jnp.zeros_like(gsum_ref)
            loss_ref[...] = jnp.zeros_like(loss_ref)

        hv = h_ref[...]
        r = lax.rsqrt(jnp.mean(hv * hv, axis=-1, keepdims=True) + EPS)
        xh = hv * r
        diff = xh * g_ref[...] - t_ref[...]
        loss_ref[...] += 0.5 * jnp.sum(jnp.mean(diff * diff, axis=-1, keepdims=True), axis=0, keepdims=True)
        dy = diff * (1.0 / D_MODEL)
        gsum_ref[...] += jnp.sum(dy * xh, axis=0, keepdims=True)
        dxh = dy * g_ref[...]
        dh_ref[...] = r * (dxh - xh * jnp.mean(dxh * xh, axis=-1, keepdims=True))

    return pl.pallas_call(
        body, name=name, grid=(T // TM,),
        in_specs=[_rows(TM, D_MODEL), _rows(TM, D_MODEL), _whole((1, D_MODEL))],
        out_specs=[_rows(TM, D_MODEL), _whole((1, D_MODEL)), _whole((1, LANES))],
        out_shape=[jax.ShapeDtypeStruct((T, D_MODEL), F32), jax.ShapeDtypeStruct((1, D_MODEL), F32),
                   jax.ShapeDtypeStruct((1, LANES), F32)],
        compiler_params=_params(1),
    )(h, target, gain)


def _ple_out_bwd(dh2, gate, e, h1, g_ple, w_gate, w_out, name, ride=None):
    T = dh2.shape[0]

    def body(dh2_ref, gate_ref, e_ref, h1_ref, gpn_ref, wg_ref, wout_ref,
             dh1_ref, dh1b_ref, dzg_ref, de_ref, dycat_ref, gsum_ref):
        @pl.when(pl.program_id(0) == 0)
        def _():
            gsum_ref[...] = jnp.zeros_like(gsum_ref)

        dh2v = dh2_ref[...]
        gate = gate_ref[...]
        de_ref[...] = (dh2v * gate).astype(BF16)
        dzg = (dh2v * e_ref[...] * gate * (1.0 - gate)).astype(BF16)
        dzg_ref[...] = dzg
        dhn2 = _nt(dzg, wg_ref[...])
        h1 = h1_ref[...]
        r1 = lax.rsqrt(jnp.mean(h1 * h1, axis=-1, keepdims=True) + EPS)
        xh = h1 * r1
        gsum_ref[...] += jnp.sum(dhn2 * xh, axis=0, keepdims=True)
        dxh = dhn2 * gpn_ref[...]
        dh1 = dh2v + r1 * (dxh - xh * jnp.mean(dxh * xh, axis=-1, keepdims=True))
        dh1_ref[...] = dh1
        dh1b = dh1.astype(BF16)
        dh1b_ref[...] = dh1b
        dycat_ref[...] = _nt(dh1b, wout_ref[...])

    f32 = jax.ShapeDtypeStruct((T, D_MODEL), F32)
    bf = jax.ShapeDtypeStruct((T, D_MODEL), BF16)
    full = _rows(TM, D_MODEL)
    return _call(
        body, name=name, grid=(T // TM,),
        in_specs=[full, full, full, full, _whole((1, D_MODEL)), _whole((D_MODEL, D_MODEL)),
                  _whole((D_MODEL, D_MODEL))],
        out_specs=[full, full, full, full, full, _whole((1, D_MODEL))],
        out_shape=[f32, bf, bf, bf, f32, jax.ShapeDtypeStruct((1, D_MODEL), F32)],
        args=(dh2, gate, e, h1, g_ple, w_gate, w_out,), ride=ride)


def _branch_bwd(dycat, o, ug, c3, conv, head_mean, g_attn, g_conv, ln_g, ln_b, w_pw, name, ride=None):
    T = o.shape[0]

    def body(dya_ref, dyc_ref, o_ref, ga_ref, gc_ref, c3_ref, conv_ref, hm_ref, gao_ref, gco_ref,
             lng_ref, lnb_ref, wpw_ref,
             do_ref, dga_ref, dgc_ref, dc3_ref, dconv_ref, sums_ref):
        @pl.when(pl.program_id(0) == 0)
        def _():
            sums_ref[...] = jnp.zeros_like(sums_ref)

        hm = hm_ref[...]
        col = lambda x: jnp.sum(x, axis=0, keepdims=True)
        ov = o_ref[...]
        rh = lax.rsqrt(_nn((ov * ov).astype(BF16), hm) + EPS)
        xh = ov * rh
        ga = ga_ref[...]
        sg = _sigmoid(ga)
        dya = dya_ref[...]
        don = dya * (ga * sg)
        dga_ref[...] = (dya * xh * gao_ref[...] * _dsilu(ga, sg)).astype(BF16)
        sums_ref[0:1, :] += col(don * xh)
        dxh = don * gao_ref[...]
        do_ref[...] = (rh * (dxh - xh * _dot_hilo(dxh * xh, hm))).astype(BF16)
        c3 = c3_ref[...]
        rc = lax.rsqrt(jnp.mean(c3 * c3, axis=-1, keepdims=True) + EPS)
        xh3 = c3 * rc
        gc = gc_ref[...]
        sgc = _sigmoid(gc)
        dyc = dyc_ref[...]
        dn3 = dyc * (gc * sgc)
        dgc_ref[...] = (dyc * xh3 * gco_ref[...] * _dsilu(gc, sgc)).astype(BF16)
        sums_ref[1:2, :] += col(dn3 * xh3)
        dxh3 = dn3 * gco_ref[...]
        dc3 = (rc * (dxh3 - xh3 * jnp.mean(dxh3 * xh3, axis=-1, keepdims=True))).astype(BF16)
        dc3_ref[...] = dc3
        dc2 = _nt(dc3, wpw_ref[...])
        cv = conv_ref[...]
        mu = jnp.mean(cv, axis=-1, keepdims=True)
        xc = cv - mu
        rs = lax.rsqrt(jnp.mean(xc * xc, axis=-1, keepdims=True) + EPS)
        xn = xc * rs
        ln = xn * lng_ref[...] + lnb_ref[...]
        dln = dc2 * _dsilu(ln, _sigmoid(ln))
        sums_ref[2:3, :] += col(dln * xn)
        sums_ref[3:4, :] += col(dln)
        dxn = dln * lng_ref[...]
        dconv = rs * (dxn - jnp.mean(dxn, axis=-1, keepdims=True)
                      - xn * jnp.mean(dxn * xn, axis=-1, keepdims=True))
        dconv_ref[...] = dconv
        sums_ref[4:5, :] += col(dconv)

    half = lambda dt: jax.ShapeDtypeStruct((T, CHUNK), dt)
    tile = _rows(TM, CHUNK)
    vec = _whole((1, CHUNK))
    return _call(
        body, name=name, grid=(T // TM,),
        in_specs=[_rows(TM, CHUNK, 0), _rows(TM, CHUNK, 1), tile, _rows(TM, CHUNK, 0), _rows(TM, CHUNK, 3),
                  tile, tile, _whole((ATTN_DIM, ATTN_DIM)), vec, vec, vec, vec, _whole((CONV_DIM, CONV_DIM))],
        out_specs=[tile, tile, tile, tile, tile, _whole((8, CHUNK))],
        out_shape=[half(BF16), half(BF16), half(BF16), half(BF16), half(F32),
                   jax.ShapeDtypeStruct((8, CHUNK), F32)],
        args=(dycat, dycat, o, ug, ug, c3, conv, head_mean, g_attn, g_conv, ln_g, ln_b, w_pw,), ride=ride)


def _conv_bwd(dconv, ug, dw_w, name, ride=None):
    T = dconv.shape[0]
    per = TM // HALO
    last = T // HALO - 1
    n_tiles = T // TM

    def body(d_ref, dn_ref, cv_ref, cg_ref, cvh_ref, cgh_ref, w_ref, dcv_ref, dcg_ref, dw_ref,
             dpad_ref, cpad_ref, dsh_ref, csh_ref, dw_acc):
        i = pl.program_id(0)

        @pl.when(i == 0)
        def _():
            dw_acc[...] = jnp.zeros_like(dw_acc)

        tail = jnp.zeros((SUBLANES, CONV_DIM), F32)
        dpad_ref[0:TM, :] = d_ref[...]
        dpad_ref[TM:TM + HALO, :] = jnp.where(i == n_tiles - 1, 0.0, dn_ref[...])
        dpad_ref[TM + HALO:, :] = tail
        halo = cvh_ref[...] * _sigmoid(cgh_ref[...])
        cpad_ref[0:HALO, :] = jnp.where(i == 0, 0.0, halo)
        cpad_ref[HALO:HALO + TM, :] = cv_ref[...] * _sigmoid(cg_ref[...])
        cpad_ref[HALO + TM:, :] = tail
        _shifted_copies(dpad_ref, dsh_ref)
        _shifted_copies(cpad_ref, csh_ref)
        taps = [w_ref[t:t + 1, :] for t in range(CONV_WIDTH)]

        def rows(j, _):
            r = pl.multiple_of(j * CONV_ROWS, CONV_ROWS)
            d = d_ref[pl.ds(r, CONV_ROWS), :]
            dc = jnp.zeros((CONV_ROWS, CONV_DIM), F32)
            for t in range(CONV_WIDTH):
                b, a = _shift_of(CONV_WIDTH - 1 - t)
                dc = dc + taps[t] * dsh_ref[b, pl.ds(r + a, CONV_ROWS), :]
                b, a = _shift_of(HALO - (CONV_WIDTH - 1) + t)
                prod = d * csh_ref[b, pl.ds(r + a, CONV_ROWS), :]
                dw_acc[t] += jnp.sum(prod.reshape(CONV_ROWS // SUBLANES, SUBLANES, CONV_DIM), axis=0)
            cv = cv_ref[pl.ds(r, CONV_ROWS), :]
            sg = _sigmoid(cg_ref[pl.ds(r, CONV_ROWS), :])
            dcv_ref[pl.ds(r, CONV_ROWS), :] = (dc * sg).astype(BF16)
            dcg_ref[pl.ds(r, CONV_ROWS), :] = (dc * cv * sg * (1.0 - sg)).astype(BF16)
            return 0

        lax.fori_loop(0, TM // CONV_ROWS, rows, 0)

        @pl.when(i == n_tiles - 1)
        def _():
            dw_ref[...] = jnp.zeros_like(dw_ref)
            for t in range(CONV_WIDTH):
                dw_ref[t:t + 1, :] = jnp.sum(dw_acc[t], axis=0, keepdims=True)

    prev = lambda col: pl.BlockSpec((HALO, CHUNK), lambda i: (jnp.maximum(i * per - 1, 0), col))
    nxt = pl.BlockSpec((HALO, CONV_DIM), lambda i: (jnp.minimum((i + 1) * per, last), 0))
    half = jax.ShapeDtypeStruct((T, CHUNK), BF16)
    return _call(
        body, name=name, grid=(T // TM,),
        in_specs=[_rows(TM, CONV_DIM), nxt, _rows(TM, CHUNK, 1), _rows(TM, CHUNK, 2), prev(1), prev(2),
                  _whole((CONV_WIDTH, CONV_DIM))],
        out_specs=[_rows(TM, CHUNK), _rows(TM, CHUNK), _whole((HALO, CONV_DIM))],
        out_shape=[half, half, jax.ShapeDtypeStruct((HALO, CONV_DIM), F32)],
        scratch_shapes=[pltpu.VMEM((TM + HALO + SUBLANES, CONV_DIM), F32),
                        pltpu.VMEM((TM + HALO + SUBLANES, CONV_DIM), F32),
                        pltpu.VMEM((SUBLANES, TM + HALO, CONV_DIM), F32),
                        pltpu.VMEM((SUBLANES, TM + HALO, CONV_DIM), F32),
                        pltpu.VMEM((HALO, SUBLANES, CONV_DIM), F32)],
        args=(dconv, dconv, ug, ug, ug, ug, dw_w,), ride=ride)


def _attn_bwd(qs, k, v, do, cs, tri, tri_t, name, ride=None):
    T = qs.shape[0]
    nq = T // BLK
    width = LANES * ATT_COLS
    chains = [(c, half) for c in range(ATT_COLS) for half in range(2)]

    def body(q_ref, k_ref, v_ref, do_ref, cs_ref, m_ref, mt_ref, dq_ref, dk_ref, dv_ref, dk_acc, dv_acc):
        qi = pl.program_id(1)

        @pl.when(qi == 0)
        def _():
            dk_acc[...] = jnp.zeros_like(dk_acc)
            dv_acc[...] = jnp.zeros_like(dv_acc)

        lane = lax.broadcasted_iota(jnp.int32, (BLK, LANES), 1)
        first = lane < HEAD_DIM
        causal = (lax.broadcasted_iota(jnp.int32, (BLK, BLK), 1)
                  < lax.broadcasted_iota(jnp.int32, (BLK, BLK), 0))
        tri_m = m_ref[...]
        tri_mt = mt_ref[...]

        def halves(x):
            zero = jnp.zeros_like(x)
            return jnp.where(first, x, zero), jnp.where(first, zero, x)

        qh, doh, cs = {}, {}, []
        for c in range(ATT_COLS):
            qh[c, 0], qh[c, 1] = halves(q_ref[:, c * LANES:(c + 1) * LANES])
            doh[c, 0], doh[c, 1] = halves(do_ref[:, c * LANES:(c + 1) * LANES])
            cs.append(cs_ref[:, c * LANES:(c + 1) * LANES])

        def step(kb, state, masked):
            prefixes, dq_accs = state
            start = pl.multiple_of(kb * BLK, BLK)
            kblk = [k_ref[pl.ds(start, BLK), c * LANES:(c + 1) * LANES] for c in range(ATT_COLS)]
            vblk = [v_ref[pl.ds(start, BLK), c * LANES:(c + 1) * LANES] for c in range(ATT_COLS)]
            prefixes, dq_accs = list(prefixes), list(dq_accs)
            for g0 in range(0, len(chains), CHAIN_GROUP):
                ids = range(g0, g0 + CHAIN_GROUP)
                grp = [chains[n] for n in ids]
                z = [_nt(qh[ch], kblk[ch[0]]) for ch in grp]
                da = [_nt(doh[ch], vblk[ch[0]]) for ch in grp]
                parts = [_softplus_parts(zi) for zi in z]
                sp = [pt[1] for pt in parts]
                if masked:
                    sp = [jnp.where(causal, s, 0.0) for s in sp]
                incl = [_dot_hilo(s, tri_m) for s in sp]
                carries = [jnp.sum(jnp.where(lane == kb + HEAD_DIM * half, cs[c], 0.0), axis=1, keepdims=True)
                           for c, half in grp]
                a = [jnp.exp(zi - ii - ci) for zi, ii, ci in zip(z, incl, carries)]
                if masked:
                    a = [jnp.where(causal, ai, 0.0) for ai in a]
                w = [ai * di for ai, di in zip(a, da)]
                pinc = [_nn(wi.astype(BF16), tri_mt) for wi in w]
                dz = [wi - pt[0] * (pi + prefixes[n]) for n, wi, pt, pi in zip(ids, w, parts, pinc)]
                if masked:
                    dz = [jnp.where(causal, d, 0.0) for d in dz]
                for j in range(0, CHAIN_GROUP, 2):
                    c = grp[j][0]
                    k0, k1 = halves(kblk[c])
                    dz0, dz1 = dz[j].astype(BF16), dz[j + 1].astype(BF16)
                    a0, a1 = a[j].astype(BF16), a[j + 1].astype(BF16)
                    dq_accs[c] = dq_accs[c] + _nn(dz0, k0) + _nn(dz1, k1)
                    dk_acc[pl.ds(start, BLK), c * LANES:(c + 1) * LANES] += _tn(dz0, qh[c, 0]) + _tn(dz1, qh[c, 1])
                    dv_acc[pl.ds(start, BLK), c * LANES:(c + 1) * LANES] += _tn(a0, doh[c, 0]) + _tn(a1, doh[c, 1])
                for n, pi in zip(ids, pinc):
                    prefixes[n] = prefixes[n] + pi[:, BLK - 1:BLK]
            return tuple(prefixes), tuple(dq_accs)

        state = (tuple(jnp.zeros((BLK, 1), F32) for _ in chains),
                 tuple(jnp.zeros((BLK, LANES), F32) for _ in range(ATT_COLS)))
        first_block = jnp.max(jnp.where(lane == FIRST_BLOCK_LANE, cs[0], 0.0)).astype(jnp.int32)
        state = lax.fori_loop(first_block, qi, lambda kb, st: step(kb, st, False), state)
        state = step(qi, state, True)
        for c in range(ATT_COLS):
            dq_ref[:, c * LANES:(c + 1) * LANES] = (state[1][c] * (HEAD_DIM ** -0.5)).astype(BF16)

        @pl.when(qi == nq - 1)
        def _():
            dk_ref[...] = dk_acc[...].astype(BF16)
            dv_ref[...] = dv_acc[...].astype(BF16)

    blk = pl.BlockSpec((BLK, width), lambda j, i: (i, j))
    col = pl.BlockSpec((T, width), lambda j, i: (0, j))
    out = jax.ShapeDtypeStruct((T, ATTN_DIM), BF16)
    return _call(
        body, name=name, grid=(ATTN_DIM // width, nq),
        in_specs=[blk, col, col, blk, blk, _whole((BLK, BLK)), _whole((BLK, BLK))],
        out_specs=[blk, col, col], out_shape=[out, out, out],
        scratch_shapes=[pltpu.VMEM((T, width), F32), pltpu.VMEM((T, width), F32)],
        args=(qs, k, v, do, cs, tri, tri_t,), ride=ride)


def _inproj_bwd(du, w_in_t, h, dh1, gain, name, ride=None):
    T = h.shape[0]

    def body(*refs):
        du_refs = refs[:N_CHUNK]
        w_ref, h_ref, dh1_ref, g_ref, dh_ref, gsum_ref = refs[N_CHUNK:]

        @pl.when(pl.program_id(0) == 0)
        def _():
            gsum_ref[...] = jnp.zeros_like(gsum_ref)

        dhn = jnp.zeros((TM, D_MODEL), F32)
        for j in range(N_CHUNK):
            dhn = dhn + _nn(du_refs[j][...], w_ref[j * CHUNK:(j + 1) * CHUNK, :])
        hv = h_ref[...]
        r = lax.rsqrt(jnp.mean(hv * hv, axis=-1, keepdims=True) + EPS)
        xh = hv * r
        gsum_ref[...] += jnp.sum(dhn * xh, axis=0, keepdims=True)
        dxh = dhn * g_ref[...]
        dh_ref[...] = dh1_ref[...] + r * (dxh - xh * jnp.mean(dxh * xh, axis=-1, keepdims=True))

    full = _rows(TM, D_MODEL)
    return _call(
        body, name=name, grid=(T // TM,),
        in_specs=[_rows(TM, CHUNK)] * N_CHUNK + [_whole((N_CHUNK * CHUNK, D_MODEL)), full, full,
                                                 _whole((1, D_MODEL))],
        out_specs=[full, _whole((1, D_MODEL))],
        out_shape=[jax.ShapeDtypeStruct((T, D_MODEL), F32), jax.ShapeDtypeStruct((1, D_MODEL), F32)],
        args=(*du, w_in_t, h, dh1, gain), ride=ride)


def _weight_grad(lhs_list, rhs, name, tk=CHUNK, ride=None):
    T, n_rhs = rhs.shape
    n = len(lhs_list)
    ka = lhs_list[0].shape[1]
    per = ka // tk

    def body(*refs):
        a_refs, b_ref, out_ref = refs[:n], refs[n], refs[n + 1]
        step = pl.program_id(0)
        for j in range(n):
            for s in range(per):
                @pl.when(step == j * per + s)
                def _(j=j, s=s):
                    out_ref[...] = _tn(a_refs[j][:, s * tk:(s + 1) * tk], b_ref[...]).astype(BF16)

    (grad,), landed = _call(
        body, name=name, grid=(n * per,),
        in_specs=[_whole((T, ka))] * n + [_whole((T, n_rhs))],
        out_specs=[pl.BlockSpec((tk, n_rhs), lambda i: (i, 0))],
        out_shape=[jax.ShapeDtypeStruct((n * ka, n_rhs), BF16)],
        args=(*lhs_list, rhs), ride=ride)
    return grad, landed


def _adamw_update(w, g, m, v):
    nm = ADAM_B1 * m + (1.0 - ADAM_B1) * g
    nv = ADAM_B2 * v + (1.0 - ADAM_B2) * (g * g)
    m_hat = nm / (1.0 - ADAM_B1 ** ADAM_STEP)
    v_hat = nv / (1.0 - ADAM_B2 ** ADAM_STEP)
    return -ADAM_LR * (m_hat / (jnp.sqrt(v_hat) + ADAM_EPS) + ADAM_WD * w), nm, nv


def _sum_adamw(slots, w, m, v, name):
    depth, R, C = w.shape
    tr = min(R, ADAMW_ROWS)

    def body(*refs):
        slot_refs, (w_ref, m_ref, v_ref, g_ref, d_ref, nm_ref, nv_ref) = refs[:depth], refs[depth:]
        for layer in range(depth):
            @pl.when(pl.program_id(0) == layer)
            def _(src=slot_refs[layer]):
                g = src[0].astype(F32)
                for s in range(1, src.shape[0]):
                    g = g + src[s].astype(F32)
                g_ref[0] = g
                d_ref[0], nm_ref[0], nv_ref[0] = _adamw_update(w_ref[0], g, m_ref[0], v_ref[0])

    slot_spec = lambda layer: pl.BlockSpec((slots[layer].shape[0], tr, C),
                                           lambda l, i: (0, jnp.where(l == layer, i, 0), 0))
    spec = pl.BlockSpec((1, tr, C), lambda l, i: (l, i, 0))
    out = jax.ShapeDtypeStruct((depth, R, C), F32)
    return pl.pallas_call(
        body, name=name, grid=(depth, R // tr),
        in_specs=[slot_spec(layer) for layer in range(depth)] + [spec] * 3,
        out_specs=[spec] * 4, out_shape=[out] * 4,
        compiler_params=_params(2),
    )(*slots, w, m, v)


def _adamw(w, g, m, v, name):
    R, C = w.shape
    tr = R
    for cand in (512, 256, 128, 64):
        if R % cand == 0 and R > cand:
            tr = cand
            break

    def body(w_ref, g_ref, m_ref, v_ref, d_ref, nm_ref, nv_ref):
        d_ref[...], nm_ref[...], nv_ref[...] = _adamw_update(w_ref[...], g_ref[...], m_ref[...], v_ref[...])

    spec = pl.BlockSpec((tr, C), lambda i: (i, 0))
    out = jax.ShapeDtypeStruct((R, C), F32)
    return pl.pallas_call(
        body, name=name, grid=(R // tr,),
        in_specs=[spec] * 4, out_specs=[spec] * 3, out_shape=[out, out, out],
        compiler_params=_params(1),
    )(w, g, m, v)


def _pack_small(values, scalar=None):
    pad = lambda a: jnp.pad(a, ((0, 0), (0, D_MODEL - a.shape[1])))
    last = jnp.zeros((1, D_MODEL), F32) if scalar is None else pad(scalar.reshape(1, 1))
    return jnp.concatenate([pad(values[name].reshape(rows, cols)) for name, _, rows, cols in SMALL_LAYOUT] + [last],
                           axis=0)


def _small_update(all_packs, state, name):
    n = len(SMALL_LAYOUT)

    def body(packs_ref, *refs):
        ins, outs = refs[:3 * n], refs[3 * n:]
        total = packs_ref[0]
        for s in range(1, N_DEV):
            total = total + packs_ref[s]
        for j, (_, at, rows, cols) in enumerate(SMALL_LAYOUT):
            g = total[at:at + rows, :cols]
            w_ref, m_ref, v_ref = ins[3 * j:3 * j + 3]
            outs[4 * j][...] = g
            outs[4 * j + 1][...], outs[4 * j + 2][...], outs[4 * j + 3][...] = _adamw_update(
                w_ref[...], g, m_ref[...], v_ref[...])
        outs[-2][...] = total[LOSS_ROW:LOSS_ROW + 1, :LANES]
        outs[-1][...] = total[SMALL_ROWS:, :]

    shapes = [jax.ShapeDtypeStruct((rows, cols), F32) for _, _, rows, cols in SMALL_LAYOUT for _ in range(4)]
    shapes += [jax.ShapeDtypeStruct((1, LANES), F32), jax.ShapeDtypeStruct((PACK_ROWS - SMALL_ROWS, D_MODEL), F32)]
    operands = [a for item in SMALL_LAYOUT for a in state[item[0]]]
    res = pl.pallas_call(body, name=name, out_shape=shapes, compiler_params=_params())(all_packs, *operands)
    per_name = {item[0]: tuple(res[4 * j:4 * j + 4]) for j, item in enumerate(SMALL_LAYOUT)}
    return per_name, res[-2][0, 0], res[-1]


def kernel(x, p, norm_g, w_in, attn_out_g, dw_w, dw_b, conv_ln_g, conv_ln_b, w_pw, conv_out_g, w_out, ple_norm_g, w_ple_gate, w_ple, final_g, loss_target, m_norm_g, m_w_in, m_attn_out_g, m_dw_w, m_dw_b, m_conv_ln_g, m_conv_ln_b, m_w_pw, m_conv_out_g, m_w_out, m_ple_norm_g, m_w_ple_gate, m_w_ple, m_final_g, v_norm_g, v_w_in, v_attn_out_g, v_dw_w, v_dw_b, v_conv_ln_g, v_conv_ln_b, v_w_pw, v_conv_out_g, v_w_out, v_ple_norm_g, v_w_ple_gate, v_w_ple, v_final_g):
    depth = w_in.shape[0]
    T = x.shape[1]
    given = dict(
        norm_g=norm_g, ple_norm_g=ple_norm_g, final_g=final_g, dw_b=dw_b, conv_ln_g=conv_ln_g, conv_ln_b=conv_ln_b,
        conv_out_g=conv_out_g, attn_out_g=attn_out_g,
        m_norm_g=m_norm_g, m_ple_norm_g=m_ple_norm_g, m_final_g=m_final_g, m_dw_b=m_dw_b, m_conv_ln_g=m_conv_ln_g,
        m_conv_ln_b=m_conv_ln_b, m_conv_out_g=m_conv_out_g, m_attn_out_g=m_attn_out_g,
        v_norm_g=v_norm_g, v_ple_norm_g=v_ple_norm_g, v_final_g=v_final_g, v_dw_b=v_dw_b, v_conv_ln_g=v_conv_ln_g,
        v_conv_ln_b=v_conv_ln_b, v_conv_out_g=v_conv_out_g, v_attn_out_g=v_attn_out_g)
    my_idx = 4 * lax.axis_index("x") + 2 * lax.axis_index("y") + lax.axis_index("c")

    ids = jnp.arange(BLK)
    tri = (ids[:, None] >= ids[None, :]).astype(BF16)
    tri_t = (ids[:, None] <= ids[None, :]).astype(BF16)
    hid = jnp.arange(ATTN_DIM) // HEAD_DIM
    head_mean = ((hid[:, None] == hid[None, :]).astype(F32) / HEAD_DIM).astype(BF16)

    w_names = ("w_in_t", "w_pw", "w_out", "w_gate", "w_ple")
    w_axes = dict(zip(w_names, (0, 0, 0, 0, 1)))
    shards = [dict(zip(w_names, (w_in[l].T.astype(BF16), w_pw[l].astype(BF16), w_out[l].astype(BF16),
                                 w_ple_gate[l].astype(BF16), w_ple[l].astype(BF16)))) for l in range(depth)]
    first = _all_gather([shards[0][n] for n in w_names] + [dw_w[l].T for l in range(depth)],
                        [w_axes[n] for n in w_names] + [0] * depth, "gather_weights_0")
    layers = []
    for l in range(depth):
        layers.append(dict(
            dw_w=first[len(w_names) + l].T,
            g_norm=norm_g[l][None], g_attn=jnp.tile(attn_out_g[l], N_HEADS)[None], dw_b=dw_b[l][None],
            ln_g=conv_ln_g[l][None], ln_b=conv_ln_b[l][None], g_conv=conv_out_g[l][None],
            g_ple=ple_norm_g[l][None], p=p[l, 0]))
    layers[0].update(zip(w_names, first))

    h = x[0]
    saved = []
    for l, w in enumerate(layers):
        nxt = [None] if l + 1 < depth else []

        def quarter(i, l=l, nxt=nxt):
            return [_Ride.gather(shards[l + 1]["w_in_t"], 0, nxt[0], *W_IN_QUARTERS[i])] if nxt else []

        def others(where, l=l, w=w):
            plan = OTHER_WEIGHTS_ON[where] if l > 0 else ()
            return [_Ride.gather(shards[l][n], w_axes[n], w.get(n), *rows) for n, rows in plan]

        def keep(where, landed, l=l, w=w, nxt=nxt):
            plan = OTHER_WEIGHTS_ON[where] if l > 0 else ()
            nxt[:1] = landed[:len(landed) - len(plan)]
            w.update(zip([n for n, _ in plan], landed[len(landed) - len(plan):]))

        (qs, k, v, ug, hn), landed = _prenorm_inproj(h, w["g_norm"], w["w_in_t"], f"inproj_{l}",
                                                     _Ride(quarter(0) + others("inproj")))
        keep("inproj", landed)
        (o, cs), landed = _attn_fwd(qs, k, v, tri, f"attn_fwd_{l}", _Ride(quarter(1) + others("attn")))
        keep("attn", landed)
        (conv, c2), landed = _conv_fwd(ug, w["dw_w"], w["dw_b"], w["ln_g"], w["ln_b"], f"conv_fwd_{l}",
                                       _Ride(quarter(2) + others("conv")))
        keep("conv", landed)
        (h2, h1, ycat, hn2, gate, e, c3), landed = _mix_out_ple(
            o, ug, c2, h, w["p"], head_mean, w["g_attn"], w["g_conv"], w["g_ple"],
            w["w_pw"], w["w_out"], w["w_gate"], w["w_ple"], f"mix_{l}", _Ride(quarter(3)))
        if l + 1 < depth:
            layers[l + 1]["w_in_t"] = landed[0]
        saved.append(dict(h=h, qs=qs, k=k, v=v, ug=ug, hn=hn, o=o, cs=cs, conv=conv, c2=c2, h1=h1,
                          ycat=ycat, hn2=hn2, gate=gate, e=e, c3=c3))
        h = h2
    dh, g_final, loss_part = _final_loss(h, loss_target[0], final_g[None], "final_loss")

    small = {}
    dww_parts = [None] * depth
    slots = [dict() for _ in range(depth)]
    g_w_in = None
    for l in reversed(range(depth)):
        w, s = layers[l], saved[l]
        above = [None] if g_w_in is not None else []

        def part(i, above=above, g=g_w_in):
            return [_Ride.scatter(g, 0, above[0], *W_IN_GRAD_PARTS[i])] if above else []

        def scattered(grads, names):
            return [_Ride.scatter(grads[n], w_axes[n]) for n in names]

        (dh1, dh1b, dzg, de, dycat, g_ple_sum), landed = _ple_out_bwd(
            dh, s["gate"], s["e"], s["h1"], w["g_ple"], w["w_gate"], w["w_out"], f"ple_bwd_{l}", _Ride(part(0)))
        above[:1] = landed
        (do, dga, dgc, dc3, dconv, sums), landed = _branch_bwd(
            dycat, s["o"], s["ug"], s["c3"], s["conv"], head_mean, w["g_attn"], w["g_conv"],
            w["ln_g"], w["ln_b"], w["w_pw"], f"branch_bwd_{l}", _Ride(part(1)))
        above[:1] = landed
        grads = dict(
            w_pw=_weight_grad([s["c2"]], dc3, f"grad_w_pw_{l}")[0],
            w_out=_weight_grad([s["ycat"]], dh1b, f"grad_w_out_{l}")[0],
            w_gate=_weight_grad([s["hn2"]], dzg, f"grad_w_gate_{l}")[0],
            w_ple=_weight_grad([w["p"].astype(BF16)], de, f"grad_w_ple_{l}", tk=PLE_DIM)[0])
        (dcv, dcg, dww), landed = _conv_bwd(dconv, s["ug"], w["dw_w"], f"conv_bwd_{l}", _Ride(part(2)))
        above[:1] = landed
        (dq, dk, dv), landed = _attn_bwd(s["qs"], s["k"], s["v"], do, s["cs"], tri, tri_t, f"attn_bwd_{l}",
                                         _Ride(scattered(grads, w_names[1:])))
        slots[l].update(zip(w_names[1:], landed))
        du = [dq, dk, dv, dga, dcv, dcg, dgc]
        g_w_in_here, landed = _weight_grad(du, s["hn"], f"grad_w_in_{l}", ride=_Ride(part(3)))
        if above:
            slots[l + 1]["w_in_t"] = landed[0]
        tail = [_Ride.scatter_chips(_pair_reduce(g_w_in_here, f"pair_reduce_w_in_{l}"))] if l == 0 else []
        (dh, g_norm_sum), landed = _inproj_bwd(du, w["w_in_t"], s["h"], dh1, w["g_norm"], f"inproj_bwd_{l}",
                                               _Ride(tail))
        slots[l].update(zip(("w_in_t",), landed))
        g_w_in = g_w_in_here
        small[l] = dict(norm_g=g_norm_sum, ple_norm_g=g_ple_sum, attn_out_g=sums[0].reshape(N_HEADS, HEAD_DIM).sum(0),
                        conv_out_g=sums[1], conv_ln_g=sums[2], conv_ln_b=sums[3], dw_b=sums[4])
        dww_parts[l] = dww[:CONV_WIDTH]
    slots = [[sl[n] for n in w_names] for sl in slots]
    grad_x = dh[None]

    sums_of = {name: jnp.stack([small[l][name].reshape(-1) for l in range(depth)]) for name in small[0]}
    sums_of["final_g"] = g_final
    pack = jnp.concatenate([_pack_small(sums_of, scalar=loss_part[0, 0]), jnp.concatenate(dww_parts, axis=1),
                            jnp.zeros((PACK_ROWS - SMALL_ROWS - CONV_WIDTH, D_MODEL), F32)], axis=0)
    (all_packs,) = _all_gather([pack], [0], "gather_small_grads")
    state = {name: [given[pre + name].reshape(rows, cols) for pre in ("", "m_", "v_")]
             for name, _, rows, cols in SMALL_LAYOUT}
    updated, loss, dww_sum = _small_update(all_packs.reshape(N_DEV, PACK_ROWS, D_MODEL), state, "update_small")
    res = {kind: {name: val[k].reshape(given[name].shape) for name, val in updated.items()}
           for k, kind in enumerate("gdmv")}
    dww_full = dww_sum[:CONV_WIDTH].reshape(CONV_WIDTH, depth, CONV_DIM).transpose(1, 0, 2)
    g_dw_w = lax.dynamic_slice_in_dim(dww_full, my_idx * (CONV_DIM // N_DEV), CONV_DIM // N_DEV, axis=2)

    swap = lambda a: a.transpose(0, 2, 1)
    state = {"w_in": (w_in, m_w_in, v_w_in), "w_pw": (w_pw, m_w_pw, v_w_pw), "w_out": (w_out, m_w_out, v_w_out),
             "w_ple_gate": (w_ple_gate, m_w_ple_gate, v_w_ple_gate), "w_ple": (w_ple, m_w_ple, v_w_ple)}
    for at, name in enumerate(state):
        wv, mv, vv = [swap(a) for a in state[name]] if name == "w_in" else state[name]
        out = _sum_adamw([slots[l][at] for l in range(depth)], wv, mv, vv, f"adamw_{name}")
        out = [swap(a) for a in out] if name == "w_in" else out
        res["g"][name], res["d"][name], res["m"][name], res["v"][name] = out
    flat = lambda a: a.reshape(-1, a.shape[-1])
    res["g"]["dw_w"] = g_dw_w
    res["d"]["dw_w"], res["m"]["dw_w"], res["v"]["dw_w"] = [
        a.reshape(dw_w.shape) for a in _adamw(flat(dw_w), flat(g_dw_w), flat(m_dw_w), flat(v_dw_w), "adamw_dw_w")]

    order = ["norm_g", "w_in", "attn_out_g", "dw_w", "dw_b", "conv_ln_g", "conv_ln_b", "w_pw", "conv_out_g",
             "w_out", "ple_norm_g", "w_ple_gate", "w_ple", "final_g"]
    return (loss, grad_x, *[res["g"][n] for n in order], *[res["d"][n] for n in order],
            *[res["m"][n] for n in order], *[res["v"][n] for n in order])
```

```python
import functools

import jax
import jax.numpy as jnp
from jax import lax
from jax.experimental import pallas as pl
from jax.experimental.pallas import tpu as pltpu

F32 = jnp.float32
BF16 = jnp.bfloat16
MESH = pl.DeviceIdType.MESH

N_DEV = 8
D_MODEL = 1024
ATTN_DIM = 512
CONV_DIM = 512
HEAD_DIM = 64
N_HEADS = 8
CONV_WIDTH = 31
PLE_DIM = 256
CHUNK = 512
N_CHUNK = 7
EPS = 1e-6
ADAM_LR = 0.001
ADAM_B1 = 0.9
ADAM_B2 = 0.999
ADAM_EPS = 1e-08
ADAM_WD = 0.01
ADAM_STEP = 10

LANES = 128
BLK = 256
ATT_COLS = 4
CHAIN_GROUP = 4
SOFTPLUS_LINEAR_AT = 20.0
DEAD_AT = 110.0
FIRST_BLOCK_LANE = HEAD_DIM - 1
TM = 512
HALO = 32
SUBLANES = 8
CONV_ROWS = 32
ADAMW_ROWS = 64
VMEM_LIMIT = 56 * 1024 * 1024
SMALL_ROWS = 16
SMALL_LAYOUT = (("norm_g", 0, 2, D_MODEL), ("ple_norm_g", 2, 2, D_MODEL), ("final_g", 4, 1, D_MODEL),
                ("dw_b", 5, 2, CONV_DIM), ("conv_ln_g", 7, 2, CONV_DIM), ("conv_ln_b", 9, 2, CONV_DIM),
                ("conv_out_g", 11, 2, CONV_DIM), ("attn_out_g", 13, 2, HEAD_DIM))
LOSS_ROW = 15
W_IN_GRAD_PARTS = ((0, 96), (96, 80), (176, 144), (320, 128))
PACK_ROWS = 48


def _nn(a, b):
    return lax.dot_general(a, b, (((1,), (0,)), ((), ())), preferred_element_type=F32)


def _nt(a, b):
    return lax.dot_general(a, b, (((1,), (1,)), ((), ())), preferred_element_type=F32)


def _tn(a, b):
    return lax.dot_general(a, b, (((0,), (0,)), ((), ())), preferred_element_type=F32)


def _split(x):
    hi = x.astype(BF16)
    lo = (x - hi.astype(F32)).astype(BF16)
    return hi, lo


def _dot_hilo(x, m):
    hi, lo = _split(x)
    return _nn(hi, m) + _nn(lo, m)


def _sigmoid(x):
    return jax.nn.sigmoid(x)


def _dsilu(x, s):
    return s * (1.0 + x * (1.0 - s))


def _params(n_grid=0, vmem=VMEM_LIMIT):
    sem = ("arbitrary",) * n_grid if n_grid else None
    return pltpu.CompilerParams(dimension_semantics=sem, vmem_limit_bytes=vmem)


def _rows(tm, cols, col=0):
    return pl.BlockSpec((tm, cols), lambda i: (i, col))


def _whole(shape):
    zeros = (0,) * len(shape)
    return pl.BlockSpec(shape, lambda *_: zeros)


def _my_position():
    return lax.axis_index("x"), lax.axis_index("y"), lax.axis_index("c")


def _block(ref, axis, idx, size):
    start = pl.multiple_of(idx * size, size)
    if axis == 0:
        return ref.at[pl.ds(start, size), :]
    return ref.at[:, pl.ds(start, size)]


def _all_gather(shards, axes, name):
    n = len(shards)
    sizes = [s.shape[a] for s, a in zip(shards, axes)]

    def full_shape(s, a):
        shape = list(s.shape)
        shape[a] *= N_DEV
        return jax.ShapeDtypeStruct(tuple(shape), s.dtype)

    def body(*refs):
        ins, outs = refs[:n], refs[n:2 * n]
        send_sems, recv_sems, local_sems = refs[2 * n:]
        x, y, c = _my_position()
        me, sibling = (x, y, c), (x, y, 1 - c)
        chips = [(1 - x, y), (x, 1 - y), (1 - x, 1 - y)]

        def place(i, dev):
            return _block(outs[i], axes[i], 4 * dev[0] + 2 * dev[1] + dev[2], sizes[i])

        def copy(k, i, dev, to, src=None):
            return pltpu.make_async_remote_copy(
                src_ref=place(i, dev) if src is None else src, dst_ref=place(i, dev),
                send_sem=send_sems.at[k, i], recv_sem=recv_sems.at[k, i],
                device_id=to, device_id_type=MESH)

        mine = [pltpu.make_async_copy(ins[i], place(i, me), local_sems.at[i]) for i in range(n)]
        for cp in mine:
            cp.start()
        first = [copy(0, i, me, sibling, src=ins[i]) for i in range(n)]
        for j, chip in enumerate(chips):
            first += [copy(1 + j, i, me, (*chip, c), src=ins[i]) for i in range(n)]
        for cp in first:
            cp.start()
        passed = []
        for j, chip in enumerate(chips):
            for i in range(n):
                copy(1 + j, i, (*chip, c), me).wait_recv()
            hop = [copy(4 + j, i, (*chip, c), sibling) for i in range(n)]
            for cp in hop:
                cp.start()
            passed += hop
        for i in range(n):
            copy(0, i, sibling, me).wait_recv()
        for j, chip in enumerate(chips):
            for i in range(n):
                copy(4 + j, i, (*chip, 1 - c), me).wait_recv()
        for cp in first + passed:
            cp.wait_send()
        for cp in mine:
            cp.wait()

    any_spec = pl.BlockSpec(memory_space=pl.ANY)
    return pl.pallas_call(
        body, name=name,
        out_shape=[full_shape(s, a) for s, a in zip(shards, axes)],
        in_specs=[any_spec] * n, out_specs=[any_spec] * n,
        scratch_shapes=[pltpu.SemaphoreType.DMA((7, n)), pltpu.SemaphoreType.DMA((7, n)),
                        pltpu.SemaphoreType.DMA((n,))],
    )(*shards)


def _pair_reduce(g, name):
    n_chips = N_DEV // 2
    R, C = g.shape[0] // N_DEV, g.shape[1]

    def body(g_ref, out_ref, mine_ref, theirs_ref, send_sems, recv_sems, local_sems):
        x, y, c = _my_position()
        block = lambda d: g_ref.at[pl.ds(pl.multiple_of(d * R, 16), R), :]
        sends = [pltpu.make_async_remote_copy(
            src_ref=block(2 * j + 1 - c), dst_ref=theirs_ref.at[j], send_sem=send_sems.at[j],
            recv_sem=recv_sems.at[j], device_id=(x, y, 1 - c), device_id_type=MESH) for j in range(n_chips)]
        own = [pltpu.make_async_copy(block(2 * j + c), mine_ref.at[j], local_sems.at[j]) for j in range(n_chips)]
        for cp in sends + own:
            cp.start()
        for j in range(n_chips):
            own[j].wait()
            sends[j].wait_recv()
            out_ref[j] = (mine_ref[j].astype(F32) + theirs_ref[j].astype(F32)).astype(g.dtype)
        for cp in sends:
            cp.wait_send()

    half = pltpu.VMEM((n_chips, R, C), g.dtype)
    sems = pltpu.SemaphoreType.DMA((n_chips,))
    return pl.pallas_call(
        body, name=name, out_shape=jax.ShapeDtypeStruct((n_chips, R, C), g.dtype),
        in_specs=[pl.BlockSpec(memory_space=pl.ANY)], out_specs=pl.BlockSpec(memory_space=pltpu.VMEM),
        scratch_shapes=[half, half, sems, sems, sems], compiler_params=_params(),
    )(g)


class _Ride:
    def __init__(self, parts):
        self.parts = [p for p in parts if p is not None]

    @staticmethod
    def gather(src, axis, land=None, lo=0, n=None):
        return ("gather", src, land, axis, lo, src.shape[axis] if n is None else n)

    @staticmethod
    def gather2(src, axis, land=None, lo=0, n=None):
        return ("gather2", src, land, axis, lo, src.shape[axis] if n is None else n)

    @staticmethod
    def scatter(src, axis, land=None, lo=0, n=None):
        return ("scatter", src, land, axis, lo, src.shape[axis] // N_DEV if n is None else n)

    @staticmethod
    def scatter_chips(chip_sums):
        return ("scatter_chips", chip_sums, None, 0, 0, chip_sums.shape[1])

    def arrays(self):
        return [p[1] for p in self.parts] + [p[2] for p in self.parts if p[2] is not None]

    def out_shapes(self):
        out = []
        for kind, src, _, axis, _, _ in self.parts:
            shape = list(src.shape)
            if kind in ("gather", "gather2"):
                shape[axis] *= N_DEV
            elif kind == "scatter_chips":
                pass
            else:
                shape[axis] //= N_DEV
                shape = [N_DEV] + shape
            out.append(jax.ShapeDtypeStruct(tuple(shape), src.dtype))
        return out

    def aliases(self, n_in, n_out):
        m, out = len(self.parts), {}
        for j, p in enumerate(self.parts):
            if p[2] is not None:
                out[n_in + m + len(out)] = n_out + j
        return out

    def scratch(self):
        m = len(self.parts)
        return [pltpu.SemaphoreType.DMA((N_DEV - 1, m)), pltpu.SemaphoreType.DMA((N_DEV - 1, m)),
                pltpu.SemaphoreType.DMA((m,))]

    def _copies(self, src_refs, land_refs, sems):
        send_sems, recv_sems, local_sems = sems
        x, y, c = _my_position()
        my_idx = 4 * x + 2 * y + c
        own, sends, relays, lands = [], [], [], []
        for j, (kind, src, _, axis, lo, n) in enumerate(self.parts):
            if kind == "scatter_chips":
                for k in (0, 2, 4, 6):
                    px, py = (1 - x if k & 4 else x), (1 - y if k & 2 else y)
                    a, b = src_refs[j].at[2 * px + py], land_refs[j].at[2 * x + y]
                    if k == 0:
                        own.append(pltpu.make_async_copy(a, b, local_sems.at[j]))
                        continue
                    mk = lambda dst, a=a, k=k, j=j, to=(px, py, c): pltpu.make_async_remote_copy(
                        src_ref=a, dst_ref=dst, send_sem=send_sems.at[k - 1, j], recv_sem=recv_sems.at[k - 1, j],
                        device_id=to, device_id_type=MESH)
                    sends.append(mk(b))
                    lands.append(mk(land_refs[j].at[2 * px + py]))
                continue
            size = src.shape[axis] if kind in ("gather", "gather2") else src.shape[axis] // N_DEV
            align = 16 if axis == 0 else LANES

            def rows(ref, idx, lead=None, axis=axis, lo=lo, n=n, size=size, align=align):
                at = pl.ds(pl.multiple_of(idx * size + lo, align), n)
                where = (at, slice(None)) if axis == 0 else (slice(None), at)
                return ref.at[where] if lead is None else ref.at[(lead, *where)]

            def in_shard(ref):
                return rows(ref, 0)

            def in_slot(ref, s):
                return rows(ref, 0, lead=s)

            if kind == "gather2":
                chips = [(1 - x, y), (x, 1 - y), (1 - x, 1 - y)]
                place = lambda px, py, pc: rows(land_refs[j], 4 * px + 2 * py + pc)

                def copy(i, a, dst, to, j=j):
                    return pltpu.make_async_remote_copy(
                        src_ref=a, dst_ref=dst, send_sem=send_sems.at[i, j], recv_sem=recv_sems.at[i, j],
                        device_id=to, device_id_type=MESH)

                mine = in_shard(src_refs[j])
                own.append(pltpu.make_async_copy(mine, place(x, y, c), local_sems.at[j]))
                sends.append(copy(0, mine, place(x, y, c), (x, y, 1 - c)))
                lands.append(copy(0, mine, place(x, y, 1 - c), (x, y, 1 - c)))
                for i, (px, py) in enumerate(chips):
                    sends.append(copy(1 + i, mine, place(x, y, c), (px, py, c)))
                    relays.append((copy(1 + i, mine, place(px, py, c), (px, py, c)),
                                   copy(4 + i, place(px, py, c), place(px, py, c), (x, y, 1 - c))))
                    lands.append(copy(4 + i, mine, place(px, py, 1 - c), (x, y, 1 - c)))
                continue
            for k in range(N_DEV):
                px = 1 - x if k & 4 else x
                py = 1 - y if k & 2 else y
                pc = 1 - c if k & 1 else c
                peer_idx = 4 * px + 2 * py + pc
                if kind == "gather":
                    a, b, landed = in_shard(src_refs[j]), rows(land_refs[j], my_idx), rows(land_refs[j], peer_idx)
                else:
                    a, b, landed = rows(src_refs[j], peer_idx), in_slot(land_refs[j], my_idx), in_slot(land_refs[j], peer_idx)
                if k == 0:
                    own.append(pltpu.make_async_copy(a, b, local_sems.at[j]))
                    continue
                mk = lambda dst, a=a, k=k, j=j, to=(px, py, pc): pltpu.make_async_remote_copy(
                    src_ref=a, dst_ref=dst, send_sem=send_sems.at[k - 1, j], recv_sem=recv_sems.at[k - 1, j],
                    device_id=to, device_id_type=MESH)
                sends.append(mk(b))
                lands.append(mk(landed))
        return own, sends, relays, lands

    @property
    def relayed(self):
        return any(p[0] == "gather2" for p in self.parts)

    def start(self, src_refs, land_refs, sems):
        own, sends, _, _ = self._copies(src_refs, land_refs, sems)
        for cp in own + sends:
            cp.start()

    def relay(self, src_refs, land_refs, sems):
        for arrival, onward in self._copies(src_refs, land_refs, sems)[2]:
            arrival.wait_recv()
            onward.start()

    def wait(self, src_refs, land_refs, sems):
        own, sends, relays, lands = self._copies(src_refs, land_refs, sems)
        for cp in lands:
            cp.wait_recv()
        for cp in sends + [onward for _, onward in relays]:
            cp.wait_send()
        for cp in own:
            cp.wait()


def _call(body, *, name, grid, in_specs, out_specs, out_shape, args, scratch_shapes=(), ride=None):
    in_specs, out_specs, out_shape = list(in_specs), list(out_specs), list(out_shape)
    n_in, n_out, n_sc = len(in_specs), len(out_specs), len(scratch_shapes)
    if ride is None or not ride.parts:
        res = pl.pallas_call(body, name=name, grid=grid, in_specs=in_specs, out_specs=out_specs,
                             out_shape=out_shape, scratch_shapes=list(scratch_shapes),
                             compiler_params=_params(len(grid)))(*args)
        return list(res), []
    extra, m = ride.arrays(), len(ride.parts)

    def riding(*refs):
        a = n_in + len(extra)
        b = a + n_out
        srcs, lands, sems = refs[n_in:n_in + m], refs[b:b + m], refs[b + m + n_sc:]
        at = [pl.program_id(d) for d in range(len(grid))]

        @pl.when(functools.reduce(jnp.logical_and, [i == 0 for i in at]))
        def _():
            ride.start(srcs, lands, sems)

        if ride.relayed:
            step, n_steps = at[0], 1
            for i, g in zip(at[1:], grid[1:]):
                step = step * g + i
            for g in grid:
                n_steps *= g
            assert n_steps >= 3, "a two-level ride needs a middle grid step"

            @pl.when(step == n_steps // 2)
            def _():
                ride.relay(srcs, lands, sems)

        body(*refs[:n_in], *refs[a:b], *refs[b + m:b + m + n_sc])

        @pl.when(functools.reduce(jnp.logical_and, [i == g - 1 for i, g in zip(at, grid)]))
        def _():
            ride.wait(srcs, lands, sems)

    hbm = pl.BlockSpec(memory_space=pl.ANY)
    res = pl.pallas_call(
        riding, name=name, grid=grid, in_specs=in_specs + [hbm] * len(extra), out_specs=out_specs + [hbm] * m,
        out_shape=out_shape + ride.out_shapes(), scratch_shapes=list(scratch_shapes) + ride.scratch(),
        input_output_aliases=ride.aliases(n_in, n_out), compiler_params=_params(len(grid)),
    )(*args, *extra)
    return list(res[:n_out]), list(res[n_out:])


def _prenorm_inproj(h, gain, w_in_t, name, ride=None):
    T = h.shape[0]

    def body(h_ref, g_ref, w_ref, q_ref, k_ref, v_ref, ug_ref, hn_ref):
        hv = h_ref[...]
        r = lax.rsqrt(jnp.mean(hv * hv, axis=-1, keepdims=True) + EPS)
        hn = (hv * r * g_ref[...]).astype(BF16)
        hn_ref[...] = hn
        for j in range(N_CHUNK):
            u = _nt(hn, w_ref[j * CHUNK:(j + 1) * CHUNK, :])
            if j == 0:
                q_ref[...] = (u * (HEAD_DIM ** -0.5)).astype(BF16)
            elif j == 1:
                k_ref[...] = u.astype(BF16)
            elif j == 2:
                v_ref[...] = u.astype(BF16)
            else:
                ug_ref[:, (j - 3) * CHUNK:(j - 2) * CHUNK] = u

    act = jax.ShapeDtypeStruct((T, CHUNK), BF16)
    return _call(
        body, name=name, grid=(T // TM,),
        in_specs=[_rows(TM, D_MODEL), _whole((1, D_MODEL)), _whole((N_CHUNK * CHUNK, D_MODEL))],
        out_specs=[_rows(TM, CHUNK)] * 3 + [_rows(TM, 4 * CHUNK), _rows(TM, D_MODEL)],
        out_shape=[act, act, act, jax.ShapeDtypeStruct((T, 4 * CHUNK), F32),
                   jax.ShapeDtypeStruct((T, D_MODEL), BF16)],
        args=(h, gain, w_in_t,), ride=ride)


def _softplus_parts(z):
    ez = jnp.exp(jnp.minimum(z, SOFTPLUS_LINEAR_AT))
    t = 1.0 + ez
    return ez * pl.reciprocal(t, approx=True), jnp.where(z > SOFTPLUS_LINEAR_AT, z, jnp.log(t))


def _attn_fwd(qs, k, v, tri, name, ride=None):
    T = qs.shape[0]
    assert T // BLK <= FIRST_BLOCK_LANE, "one lane per key block below the lane of the first block"
    width = LANES * ATT_COLS
    chains = [(c, half) for c in range(ATT_COLS) for half in range(2)]

    def body(q_ref, k_ref, v_ref, m_ref, o_ref, cs_ref):
        qi = pl.program_id(1)
        lane = lax.broadcasted_iota(jnp.int32, (BLK, LANES), 1)
        first = lane < HEAD_DIM
        causal = (lax.broadcasted_iota(jnp.int32, (BLK, BLK), 1)
                  < lax.broadcasted_iota(jnp.int32, (BLK, BLK), 0))
        tri_m = m_ref[...]
        qh = {}
        for c in range(ATT_COLS):
            q = q_ref[:, c * LANES:(c + 1) * LANES]
            zero = jnp.zeros_like(q)
            qh[c, 0], qh[c, 1] = jnp.where(first, q, zero), jnp.where(first, zero, q)

        def step(kb, state, masked):
            carries, accs, cvals = state
            start = pl.multiple_of(kb * BLK, BLK)
            kblk = [k_ref[pl.ds(start, BLK), c * LANES:(c + 1) * LANES] for c in range(ATT_COLS)]
            vblk = [v_ref[pl.ds(start, BLK), c * LANES:(c + 1) * LANES] for c in range(ATT_COLS)]
            carries, accs, cvals = list(carries), list(accs), list(cvals)
            for g0 in range(0, len(chains), CHAIN_GROUP):
                ids = range(g0, g0 + CHAIN_GROUP)
                z = [_nt(qh[chains[n]], kblk[chains[n][0]]) for n in ids]
                sp = [_softplus_parts(zi)[1] for zi in z]
                if masked:
                    sp = [jnp.where(causal, s, 0.0) for s in sp]
                incl = [_dot_hilo(s, tri_m) for s in sp]
                a = [jnp.exp(zi - ii - carries[n]) for n, zi, ii in zip(ids, z, incl)]
                if masked:
                    a = [jnp.where(causal, ai, 0.0) for ai in a]
                for n, ai, ii in zip(ids, a, incl):
                    c, half = chains[n]
                    zero = jnp.zeros_like(vblk[c])
                    vh = jnp.where(first, vblk[c], zero) if half == 0 else jnp.where(first, zero, vblk[c])
                    accs[c] = accs[c] + _nn(ai.astype(BF16), vh)
                    cvals[c] = jnp.where(lane == kb + HEAD_DIM * half, carries[n], cvals[c])
                    carries[n] = carries[n] + ii[:, 0:1]
            return tuple(carries), tuple(accs), tuple(cvals)

        zeros = tuple(jnp.zeros((BLK, LANES), F32) for _ in range(ATT_COLS))
        state = (tuple(jnp.zeros((BLK, 1), F32) for _ in chains), zeros, zeros)
        state = step(qi, state, True)

        def reaches_further(st):
            it, (carries, _, _) = st
            least = functools.reduce(jnp.minimum, carries)
            return jnp.logical_and(it < qi, jnp.min(least) < DEAD_AT)

        done, state = lax.while_loop(reaches_further, lambda st: (st[0] + 1, step(qi - 1 - st[0], st[1], False)),
                                     (jnp.int32(0), state))
        first_block = (qi - done).astype(F32)
        for c in range(ATT_COLS):
            o_ref[:, c * LANES:(c + 1) * LANES] = state[1][c]
            cs_ref[:, c * LANES:(c + 1) * LANES] = jnp.where(lane == FIRST_BLOCK_LANE, first_block, state[2][c])

    blk = pl.BlockSpec((BLK, width), lambda j, i: (i, j))
    col = pl.BlockSpec((T, width), lambda j, i: (0, j))
    out = jax.ShapeDtypeStruct((T, ATTN_DIM), F32)
    return _call(
        body, name=name, grid=(ATTN_DIM // width, T // BLK),
        in_specs=[blk, col, col, _whole((BLK, BLK))],
        out_specs=[blk, blk], out_shape=[out, out],
        args=(qs, k, v, tri,), ride=ride)


def _shifted_copies(pad_ref, sh_ref):
    rows = sh_ref.shape[1]
    for b in range(SUBLANES):
        sh_ref[b] = pad_ref[b:b + rows, :]


def _shift_of(offset):
    return offset % SUBLANES, offset - offset % SUBLANES


def _conv_fwd(ug, dw_w, dw_b, ln_g, ln_b, name, ride=None):
    T = ug.shape[0]
    per = TM // HALO

    def body(cv_ref, cg_ref, cvh_ref, cgh_ref, w_ref, b_ref, g_ref, beta_ref, conv_ref, c2_ref, pad_ref, sh_ref):
        i = pl.program_id(0)
        halo = cvh_ref[...] * _sigmoid(cgh_ref[...])
        pad_ref[0:HALO, :] = jnp.where(i == 0, 0.0, halo)
        pad_ref[HALO:HALO + TM, :] = cv_ref[...] * _sigmoid(cg_ref[...])
        pad_ref[HALO + TM:, :] = jnp.zeros((SUBLANES, CONV_DIM), F32)
        _shifted_copies(pad_ref, sh_ref)
        taps = [w_ref[t:t + 1, :] for t in range(CONV_WIDTH)]

        def rows(j, _):
            r = pl.multiple_of(j * CONV_ROWS, CONV_ROWS)
            acc = jnp.zeros((CONV_ROWS, CONV_DIM), F32) + b_ref[...]
            for t in range(CONV_WIDTH):
                b, a = _shift_of(HALO - (CONV_WIDTH - 1) + t)
                acc = acc + taps[t] * sh_ref[b, pl.ds(r + a, CONV_ROWS), :]
            conv_ref[pl.ds(r, CONV_ROWS), :] = acc
            return 0

        lax.fori_loop(0, TM // CONV_ROWS, rows, 0)
        acc = conv_ref[...]
        mu = jnp.mean(acc, axis=-1, keepdims=True)
        xc = acc - mu
        rs = lax.rsqrt(jnp.mean(xc * xc, axis=-1, keepdims=True) + EPS)
        ln = xc * rs * g_ref[...] + beta_ref[...]
        c2_ref[...] = (ln * _sigmoid(ln)).astype(BF16)

    prev = lambda col: pl.BlockSpec((HALO, CHUNK), lambda i: (jnp.maximum(i * per - 1, 0), col))
    vec = _whole((1, CONV_DIM))
    return _call(
        body, name=name, grid=(T // TM,),
        in_specs=[_rows(TM, CHUNK, 1), _rows(TM, CHUNK, 2), prev(1), prev(2),
                  _whole((CONV_WIDTH, CONV_DIM)), vec, vec, vec],
        out_specs=[_rows(TM, CONV_DIM), _rows(TM, CONV_DIM)],
        out_shape=[jax.ShapeDtypeStruct((T, CONV_DIM), F32), jax.ShapeDtypeStruct((T, CONV_DIM), BF16)],
        scratch_shapes=[pltpu.VMEM((TM + HALO + SUBLANES, CONV_DIM), F32),
                        pltpu.VMEM((SUBLANES, TM + HALO, CONV_DIM), F32)],
        args=(ug, ug, ug, ug, dw_w, dw_b, ln_g, ln_b,), ride=ride)


def _mix_out_ple(o, ug, c2, h, p, head_mean, g_attn, g_conv, g_ple, w_pw, w_out, w_gate, w_ple, name, ride=None):
    T = h.shape[0]

    def body(o_ref, ga_ref, gc_ref, c2_ref, h_ref, p_ref, hm_ref, gao_ref, gco_ref, gpn_ref,
             wpw_ref, wout_ref, wg_ref, wple_ref,
             h2_ref, h1_ref, ycat_ref, hn2_ref, gate_ref, e_ref, c3_ref):
        ov = o_ref[...]
        rh = lax.rsqrt(_nn((ov * ov).astype(BF16), hm_ref[...]) + EPS)
        ga = ga_ref[...]
        ya = (ov * rh * gao_ref[...] * (ga * _sigmoid(ga))).astype(BF16)
        c3 = _nn(c2_ref[...], wpw_ref[...])
        c3_ref[...] = c3
        rc = lax.rsqrt(jnp.mean(c3 * c3, axis=-1, keepdims=True) + EPS)
        gc = gc_ref[...]
        yc = (c3 * rc * gco_ref[...] * (gc * _sigmoid(gc))).astype(BF16)
        ycat_ref[:, :ATTN_DIM] = ya
        ycat_ref[:, ATTN_DIM:] = yc
        h1 = h_ref[...] + _nn(ya, wout_ref[:ATTN_DIM, :]) + _nn(yc, wout_ref[ATTN_DIM:, :])
        h1_ref[...] = h1
        r1 = lax.rsqrt(jnp.mean(h1 * h1, axis=-1, keepdims=True) + EPS)
        hn2 = (h1 * r1 * gpn_ref[...]).astype(BF16)
        hn2_ref[...] = hn2
        gate = _sigmoid(_nn(hn2, wg_ref[...]))
        e = _nn(p_ref[...].astype(BF16), wple_ref[...])
        gate_ref[...] = gate
        e_ref[...] = e
        h2_ref[...] = h1 + e * gate

    f32 = lambda cols: jax.ShapeDtypeStruct((T, cols), F32)
    bf = lambda cols: jax.ShapeDtypeStruct((T, cols), BF16)
    return _call(
        body, name=name, grid=(T // TM,),
        in_specs=[_rows(TM, ATTN_DIM), _rows(TM, CHUNK, 0), _rows(TM, CHUNK, 3), _rows(TM, CONV_DIM),
                  _rows(TM, D_MODEL), _rows(TM, PLE_DIM), _whole((ATTN_DIM, ATTN_DIM)),
                  _whole((1, ATTN_DIM)), _whole((1, CONV_DIM)), _whole((1, D_MODEL)),
                  _whole((CONV_DIM, CONV_DIM)), _whole((D_MODEL, D_MODEL)), _whole((D_MODEL, D_MODEL)),
                  _whole((PLE_DIM, D_MODEL))],
        out_specs=[_rows(TM, D_MODEL), _rows(TM, D_MODEL), _rows(TM, D_MODEL), _rows(TM, D_MODEL),
                   _rows(TM, D_MODEL), _rows(TM, D_MODEL), _rows(TM, CONV_DIM)],
        out_shape=[f32(D_MODEL), f32(D_MODEL), bf(D_MODEL), bf(D_MODEL), f32(D_MODEL), f32(D_MODEL),
                   f32(CONV_DIM)],
        args=(o, ug, ug, c2, h, p, head_mean, g_attn, g_conv, g_ple, w_pw, w_out, w_gate, w_ple,), ride=ride)


def _final_loss(h, target, gain, name):
    T = h.shape[0]

    def body(h_ref, t_ref, g_ref, dh_ref, gsum_ref, loss_ref):
        @pl.when(pl.program_id(0) == 0)
        def _():
            gsum_ref[...] = jnp.zeros_like(gsum_ref)
            loss_ref[...] = jnp.zeros_like(loss_ref)

        hv = h_ref[...]
        r = lax.rsqrt(jnp.mean(hv * hv, axis=-1, keepdims=True) + EPS)
        xh = hv * r
        diff = xh * g_ref[...] - t_ref[...]
        loss_ref[...] += 0.5 * jnp.sum(jnp.mean(diff * diff, axis=-1, keepdims=True), axis=0, keepdims=True)
        dy = diff * (1.0 / D_MODEL)
        gsum_ref[...] += jnp.sum(dy * xh, axis=0, keepdims=True)
        dxh = dy * g_ref[...]
        dh_ref[...] = r * (dxh - xh * jnp.mean(dxh * xh, axis=-1, keepdims=True))

    return pl.pallas_call(
        body, name=name, grid=(T // TM,),
        in_specs=[_rows(TM, D_MODEL), _rows(TM, D_MODEL), _whole((1, D_MODEL))],
        out_specs=[_rows(TM, D_MODEL), _whole((1, D_MODEL)), _whole((1, LANES))],
        out_shape=[jax.ShapeDtypeStruct((T, D_MODEL), F32), jax.ShapeDtypeStruct((1, D_MODEL), F32),
                   jax.ShapeDtypeStruct((1, LANES), F32)],
        compiler_params=_params(1),
    )(h, target, gain)


def _ple_out_bwd(dh2, gate, e, h1, g_ple, w_gate, w_out, name, ride=None):
    T = dh2.shape[0]

    def body(dh2_ref, gate_ref, e_ref, h1_ref, gpn_ref, wg_ref, wout_ref,
             dh1_ref, dh1b_ref, dzg_ref, de_ref, dycat_ref, gsum_ref):
        @pl.when(pl.program_id(0) == 0)
        def _():
            gsum_ref[...] = jnp.zeros_like(gsum_ref)

        dh2v = dh2_ref[...]
        gate = gate_ref[...]
        de_ref[...] = (dh2v * gate).astype(BF16)
        dzg = (dh2v * e_ref[...] * gate * (1.0 - gate)).astype(BF16)
        dzg_ref[...] = dzg
        dhn2 = _nt(dzg, wg_ref[...])
        h1 = h1_ref[...]
        r1 = lax.rsqrt(jnp.mean(h1 * h1, axis=-1, keepdims=True) + EPS)
        xh = h1 * r1
        gsum_ref[...] += jnp.sum(dhn2 * xh, axis=0, keepdims=True)
        dxh = dhn2 * gpn_ref[...]
        dh1 = dh2v + r1 * (dxh - xh * jnp.mean(dxh * xh, axis=-1, keepdims=True))
        dh1_ref[...] = dh1
        dh1b = dh1.astype(BF16)
        dh1b_ref[...] = dh1b
        dycat_ref[...] = _nt(dh1b, wout_ref[...])

    f32 = jax.ShapeDtypeStruct((T, D_MODEL), F32)
    bf = jax.ShapeDtypeStruct((T, D_MODEL), BF16)
    full = _rows(TM, D_MODEL)
    return _call(
        body, name=name, grid=(T // TM,),
        in_specs=[full, full, full, full, _whole((1, D_MODEL)), _whole((D_MODEL, D_MODEL)),
                  _whole((D_MODEL, D_MODEL))],
        out_specs=[full, full, full, full, full, _whole((1, D_MODEL))],
        out_shape=[f32, bf, bf, bf, f32, jax.ShapeDtypeStruct((1, D_MODEL), F32)],
        args=(dh2, gate, e, h1, g_ple, w_gate, w_out,), ride=ride)


def _branch_bwd(dycat, o, ug, c3, conv, head_mean, g_attn, g_conv, ln_g, ln_b, w_pw, name, ride=None):
    T = o.shape[0]

    def body(dya_ref, dyc_ref, o_ref, ga_ref, gc_ref, c3_ref, conv_ref, hm_ref, gao_ref, gco_ref,
             lng_ref, lnb_ref, wpw_ref,
             do_ref, dga_ref, dgc_ref, dc3_ref, dconv_ref, sums_ref):
        @pl.when(pl.program_id(0) == 0)
        def _():
            sums_ref[...] = jnp.zeros_like(sums_ref)

        hm = hm_ref[...]
        col = lambda x: jnp.sum(x, axis=0, keepdims=True)
        ov = o_ref[...]
        rh = lax.rsqrt(_nn((ov * ov).astype(BF16), hm) + EPS)
        xh = ov * rh
        ga = ga_ref[...]
        sg = _sigmoid(ga)
        dya = dya_ref[...]
        don = dya * (ga * sg)
        dga_ref[...] = (dya * xh * gao_ref[...] * _dsilu(ga, sg)).astype(BF16)
        sums_ref[0:1, :] += col(don * xh)
        dxh = don * gao_ref[...]
        do_ref[...] = (rh * (dxh - xh * _dot_hilo(dxh * xh, hm))).astype(BF16)
        c3 = c3_ref[...]
        rc = lax.rsqrt(jnp.mean(c3 * c3, axis=-1, keepdims=True) + EPS)
        xh3 = c3 * rc
        gc = gc_ref[...]
        sgc = _sigmoid(gc)
        dyc = dyc_ref[...]
        dn3 = dyc * (gc * sgc)
        dgc_ref[...] = (dyc * xh3 * gco_ref[...] * _dsilu(gc, sgc)).astype(BF16)
        sums_ref[1:2, :] += col(dn3 * xh3)
        dxh3 = dn3 * gco_ref[...]
        dc3 = (rc * (dxh3 - xh3 * jnp.mean(dxh3 * xh3, axis=-1, keepdims=True))).astype(BF16)
        dc3_ref[...] = dc3
        dc2 = _nt(dc3, wpw_ref[...])
        cv = conv_ref[...]
        mu = jnp.mean(cv, axis=-1, keepdims=True)
        xc = cv - mu
        rs = lax.rsqrt(jnp.mean(xc * xc, axis=-1, keepdims=True) + EPS)
        xn = xc * rs
        ln = xn * lng_ref[...] + lnb_ref[...]
        dln = dc2 * _dsilu(ln, _sigmoid(ln))
        sums_ref[2:3, :] += col(dln * xn)
        sums_ref[3:4, :] += col(dln)
        dxn = dln * lng_ref[...]
        dconv = rs * (dxn - jnp.mean(dxn, axis=-1, keepdims=True)
                      - xn * jnp.mean(dxn * xn, axis=-1, keepdims=True))
        dconv_ref[...] = dconv
        sums_ref[4:5, :] += col(dconv)

    half = lambda dt: jax.ShapeDtypeStruct((T, CHUNK), dt)
    tile = _rows(TM, CHUNK)
    vec = _whole((1, CHUNK))
    return _call(
        body, name=name, grid=(T // TM,),
        in_specs=[_rows(TM, CHUNK, 0), _rows(TM, CHUNK, 1), tile, _rows(TM, CHUNK, 0), _rows(TM, CHUNK, 3),
                  tile, tile, _whole((ATTN_DIM, ATTN_DIM)), vec, vec, vec, vec, _whole((CONV_DIM, CONV_DIM))],
        out_specs=[tile, tile, tile, tile, tile, _whole((8, CHUNK))],
        out_shape=[half(BF16), half(BF16), half(BF16), half(BF16), half(F32),
                   jax.ShapeDtypeStruct((8, CHUNK), F32)],
        args=(dycat, dycat, o, ug, ug, c3, conv, head_mean, g_attn, g_conv, ln_g, ln_b, w_pw,), ride=ride)


def _conv_bwd(dconv, ug, dw_w, name, ride=None):
    T = dconv.shape[0]
    per = TM // HALO
    last = T // HALO - 1
    n_tiles = T // TM

    def body(d_ref, dn_ref, cv_ref, cg_ref, cvh_ref, cgh_ref, w_ref, dcv_ref, dcg_ref, dw_ref,
             dpad_ref, cpad_ref, dsh_ref, csh_ref, dw_acc):
        i = pl.program_id(0)

        @pl.when(i == 0)
        def _():
            dw_acc[...] = jnp.zeros_like(dw_acc)

        tail = jnp.zeros((SUBLANES, CONV_DIM), F32)
        dpad_ref[0:TM, :] = d_ref[...]
        dpad_ref[TM:TM + HALO, :] = jnp.where(i == n_tiles - 1, 0.0, dn_ref[...])
        dpad_ref[TM + HALO:, :] = tail
        halo = cvh_ref[...] * _sigmoid(cgh_ref[...])
        cpad_ref[0:HALO, :] = jnp.where(i == 0, 0.0, halo)
        cpad_ref[HALO:HALO + TM, :] = cv_ref[...] * _sigmoid(cg_ref[...])
        cpad_ref[HALO + TM:, :] = tail
        _shifted_copies(dpad_ref, dsh_ref)
        _shifted_copies(cpad_ref, csh_ref)
        taps = [w_ref[t:t + 1, :] for t in range(CONV_WIDTH)]

        def rows(j, _):
            r = pl.multiple_of(j * CONV_ROWS, CONV_ROWS)
            d = d_ref[pl.ds(r, CONV_ROWS), :]
            dc = jnp.zeros((CONV_ROWS, CONV_DIM), F32)
            for t in range(CONV_WIDTH):
                b, a = _shift_of(CONV_WIDTH - 1 - t)
                dc = dc + taps[t] * dsh_ref[b, pl.ds(r + a, CONV_ROWS), :]
                b, a = _shift_of(HALO - (CONV_WIDTH - 1) + t)
                prod = d * csh_ref[b, pl.ds(r + a, CONV_ROWS), :]
                dw_acc[t] += jnp.sum(prod.reshape(CONV_ROWS // SUBLANES, SUBLANES, CONV_DIM), axis=0)
            cv = cv_ref[pl.ds(r, CONV_ROWS), :]
            sg = _sigmoid(cg_ref[pl.ds(r, CONV_ROWS), :])
            dcv_ref[pl.ds(r, CONV_ROWS), :] = (dc * sg).astype(BF16)
            dcg_ref[pl.ds(r, CONV_ROWS), :] = (dc * cv * sg * (1.0 - sg)).astype(BF16)
            return 0

        lax.fori_loop(0, TM // CONV_ROWS, rows, 0)

        @pl.when(i == n_tiles - 1)
        def _():
            dw_ref[...] = jnp.zeros_like(dw_ref)
            for t in range(CONV_WIDTH):
                dw_ref[t:t + 1, :] = jnp.sum(dw_acc[t], axis=0, keepdims=True)

    prev = lambda col: pl.BlockSpec((HALO, CHUNK), lambda i: (jnp.maximum(i * per - 1, 0), col))
    nxt = pl.BlockSpec((HALO, CONV_DIM), lambda i: (jnp.minimum((i + 1) * per, last), 0))
    half = jax.ShapeDtypeStruct((T, CHUNK), BF16)
    return _call(
        body, name=name, grid=(T // TM,),
        in_specs=[_rows(TM, CONV_DIM), nxt, _rows(TM, CHUNK, 1), _rows(TM, CHUNK, 2), prev(1), prev(2),
                  _whole((CONV_WIDTH, CONV_DIM))],
        out_specs=[_rows(TM, CHUNK), _rows(TM, CHUNK), _whole((HALO, CONV_DIM))],
        out_shape=[half, half, jax.ShapeDtypeStruct((HALO, CONV_DIM), F32)],
        scratch_shapes=[pltpu.VMEM((TM + HALO + SUBLANES, CONV_DIM), F32),
                        pltpu.VMEM((TM + HALO + SUBLANES, CONV_DIM), F32),
                        pltpu.VMEM((SUBLANES, TM + HALO, CONV_DIM), F32),
                        pltpu.VMEM((SUBLANES, TM + HALO, CONV_DIM), F32),
                        pltpu.VMEM((HALO, SUBLANES, CONV_DIM), F32)],
        args=(dconv, dconv, ug, ug, ug, ug, dw_w,), ride=ride)


def _attn_bwd(qs, k, v, do, cs, tri, tri_t, name, ride=None):
    T = qs.shape[0]
    nq = T // BLK
    width = LANES * ATT_COLS
    chains = [(c, half) for c in range(ATT_COLS) for half in range(2)]

    def body(q_ref, k_ref, v_ref, do_ref, cs_ref, m_ref, mt_ref, dq_ref, dk_ref, dv_ref, dk_acc, dv_acc):
        qi = pl.program_id(1)

        @pl.when(qi == 0)
        def _():
            dk_acc[...] = jnp.zeros_like(dk_acc)
            dv_acc[...] = jnp.zeros_like(dv_acc)

        lane = lax.broadcasted_iota(jnp.int32, (BLK, LANES), 1)
        first = lane < HEAD_DIM
        causal = (lax.broadcasted_iota(jnp.int32, (BLK, BLK), 1)
                  < lax.broadcasted_iota(jnp.int32, (BLK, BLK), 0))
        tri_m = m_ref[...]
        tri_mt = mt_ref[...]

        def halves(x):
            zero = jnp.zeros_like(x)
            return jnp.where(first, x, zero), jnp.where(first, zero, x)

        qh, doh, cs = {}, {}, []
        for c in range(ATT_COLS):
            qh[c, 0], qh[c, 1] = halves(q_ref[:, c * LANES:(c + 1) * LANES])
            doh[c, 0], doh[c, 1] = halves(do_ref[:, c * LANES:(c + 1) * LANES])
            cs.append(cs_ref[:, c * LANES:(c + 1) * LANES])

        def step(kb, state, masked):
            prefixes, dq_accs = state
            start = pl.multiple_of(kb * BLK, BLK)
            kblk = [k_ref[pl.ds(start, BLK), c * LANES:(c + 1) * LANES] for c in range(ATT_COLS)]
            vblk = [v_ref[pl.ds(start, BLK), c * LANES:(c + 1) * LANES] for c in range(ATT_COLS)]
            prefixes, dq_accs = list(prefixes), list(dq_accs)
            for g0 in range(0, len(chains), CHAIN_GROUP):
                ids = range(g0, g0 + CHAIN_GROUP)
                grp = [chains[n] for n in ids]
                z = [_nt(qh[ch], kblk[ch[0]]) for ch in grp]
                da = [_nt(doh[ch], vblk[ch[0]]) for ch in grp]
                parts = [_softplus_parts(zi) for zi in z]
                sp = [pt[1] for pt in parts]
                if masked:
                    sp = [jnp.where(causal, s, 0.0) for s in sp]
                incl = [_dot_hilo(s, tri_m) for s in sp]
                carries = [jnp.sum(jnp.where(lane == kb + HEAD_DIM * half, cs[c], 0.0), axis=1, keepdims=True)
                           for c, half in grp]
                a = [jnp.exp(zi - ii - ci) for zi, ii, ci in zip(z, incl, carries)]
                if masked:
                    a = [jnp.where(causal, ai, 0.0) for ai in a]
                w = [ai * di for ai, di in zip(a, da)]
                pinc = [_nn(wi.astype(BF16), tri_mt) for wi in w]
                dz = [wi - pt[0] * (pi + prefixes[n]) for n, wi, pt, pi in zip(ids, w, parts, pinc)]
                if masked:
                    dz = [jnp.where(causal, d, 0.0) for d in dz]
                for j in range(0, CHAIN_GROUP, 2):
                    c = grp[j][0]
                    k0, k1 = halves(kblk[c])
                    dz0, dz1 = dz[j].astype(BF16), dz[j + 1].astype(BF16)
                    a0, a1 = a[j].astype(BF16), a[j + 1].astype(BF16)
                    dq_accs[c] = dq_accs[c] + _nn(dz0, k0) + _nn(dz1, k1)
                    dk_acc[pl.ds(start, BLK), c * LANES:(c + 1) * LANES] += _tn(dz0, qh[c, 0]) + _tn(dz1, qh[c, 1])
                    dv_acc[pl.ds(start, BLK), c * LANES:(c + 1) * LANES] += _tn(a0, doh[c, 0]) + _tn(a1, doh[c, 1])
                for n, pi in zip(ids, pinc):
                    prefixes[n] = prefixes[n] + pi[:, BLK - 1:BLK]
            return tuple(prefixes), tuple(dq_accs)

        state = (tuple(jnp.zeros((BLK, 1), F32) for _ in chains),
                 tuple(jnp.zeros((BLK, LANES), F32) for _ in range(ATT_COLS)))
        first_block = jnp.max(jnp.where(lane == FIRST_BLOCK_LANE, cs[0], 0.0)).astype(jnp.int32)
        state = lax.fori_loop(first_block, qi, lambda kb, st: step(kb, st, False), state)
        state = step(qi, state, True)
        for c in range(ATT_COLS):
            dq_ref[:, c * LANES:(c + 1) * LANES] = (state[1][c] * (HEAD_DIM ** -0.5)).astype(BF16)

        @pl.when(qi == nq - 1)
        def _():
            dk_ref[...] = dk_acc[...].astype(BF16)
            dv_ref[...] = dv_acc[...].astype(BF16)

    blk = pl.BlockSpec((BLK, width), lambda j, i: (i, j))
    col = pl.BlockSpec((T, width), lambda j, i: (0, j))
    out = jax.ShapeDtypeStruct((T, ATTN_DIM), BF16)
    return _call(
        body, name=name, grid=(ATTN_DIM // width, nq),
        in_specs=[blk, col, col, blk, blk, _whole((BLK, BLK)), _whole((BLK, BLK))],
        out_specs=[blk, col, col], out_shape=[out, out, out],
        scratch_shapes=[pltpu.VMEM((T, width), F32), pltpu.VMEM((T, width), F32)],
        args=(qs, k, v, do, cs, tri, tri_t,), ride=ride)


def _inproj_bwd(du, w_in_t, h, dh1, gain, name, ride=None):
    T = h.shape[0]

    def body(*refs):
        du_refs = refs[:N_CHUNK]
        w_ref, h_ref, dh1_ref, g_ref, dh_ref, gsum_ref = refs[N_CHUNK:]

        @pl.when(pl.program_id(0) == 0)
        def _():
            gsum_ref[...] = jnp.zeros_like(gsum_ref)

        dhn = jnp.zeros((TM, D_MODEL), F32)
        for j in range(N_CHUNK):
            dhn = dhn + _nn(du_refs[j][...], w_ref[j * CHUNK:(j + 1) * CHUNK, :])
        hv = h_ref[...]
        r = lax.rsqrt(jnp.mean(hv * hv, axis=-1, keepdims=True) + EPS)
        xh = hv * r
        gsum_ref[...] += jnp.sum(dhn * xh, axis=0, keepdims=True)
        dxh = dhn * g_ref[...]
        dh_ref[...] = dh1_ref[...] + r * (dxh - xh * jnp.mean(dxh * xh, axis=-1, keepdims=True))

    full = _rows(TM, D_MODEL)
    return _call(
        body, name=name, grid=(T // TM,),
        in_specs=[_rows(TM, CHUNK)] * N_CHUNK + [_whole((N_CHUNK * CHUNK, D_MODEL)), full, full,
                                                 _whole((1, D_MODEL))],
        out_specs=[full, _whole((1, D_MODEL))],
        out_shape=[jax.ShapeDtypeStruct((T, D_MODEL), F32), jax.ShapeDtypeStruct((1, D_MODEL), F32)],
        args=(*du, w_in_t, h, dh1, gain), ride=ride)


def _weight_grad(lhs_list, rhs, name, tk=CHUNK, ride=None):
    T, n_rhs = rhs.shape
    n = len(lhs_list)
    ka = lhs_list[0].shape[1]
    per = ka // tk

    def body(*refs):
        a_refs, b_ref, out_ref = refs[:n], refs[n], refs[n + 1]
        step = pl.program_id(0)
        for j in range(n):
            for s in range(per):
                @pl.when(step == j * per + s)
                def _(j=j, s=s):
                    out_ref[...] = _tn(a_refs[j][:, s * tk:(s + 1) * tk], b_ref[...]).astype(BF16)

    (grad,), landed = _call(
        body, name=name, grid=(n * per,),
        in_specs=[_whole((T, ka))] * n + [_whole((T, n_rhs))],
        out_specs=[pl.BlockSpec((tk, n_rhs), lambda i: (i, 0))],
        out_shape=[jax.ShapeDtypeStruct((n * ka, n_rhs), BF16)],
        args=(*lhs_list, rhs), ride=ride)
    return grad, landed


def _adamw_update(w, g, m, v):
    nm = ADAM_B1 * m + (1.0 - ADAM_B1) * g
    nv = ADAM_B2 * v + (1.0 - ADAM_B2) * (g * g)
    m_hat = nm / (1.0 - ADAM_B1 ** ADAM_STEP)
    v_hat = nv / (1.0 - ADAM_B2 ** ADAM_STEP)
    return -ADAM_LR * (m_hat / (jnp.sqrt(v_hat) + ADAM_EPS) + ADAM_WD * w), nm, nv


def _sum_adamw(slots, w, m, v, name):
    depth, R, C = w.shape
    tr = min(R, ADAMW_ROWS)

    def body(*refs):
        slot_refs, (w_ref, m_ref, v_ref, g_ref, d_ref, nm_ref, nv_ref) = refs[:depth], refs[depth:]
        for layer in range(depth):
            @pl.when(pl.program_id(0) == layer)
            def _(src=slot_refs[layer]):
                g = src[0].astype(F32)
                for s in range(1, src.shape[0]):
                    g = g + src[s].astype(F32)
                g_ref[0] = g
                d_ref[0], nm_ref[0], nv_ref[0] = _adamw_update(w_ref[0], g, m_ref[0], v_ref[0])

    slot_spec = lambda layer: pl.BlockSpec((slots[layer].shape[0], tr, C),
                                           lambda l, i: (0, jnp.where(l == layer, i, 0), 0))
    spec = pl.BlockSpec((1, tr, C), lambda l, i: (l, i, 0))
    out = jax.ShapeDtypeStruct((depth, R, C), F32)
    return pl.pallas_call(
        body, name=name, grid=(depth, R // tr),
        in_specs=[slot_spec(layer) for layer in range(depth)] + [spec] * 3,
        out_specs=[spec] * 4, out_shape=[out] * 4,
        compiler_params=_params(2),
    )(*slots, w, m, v)


def _adamw(w, g, m, v, name):
    R, C = w.shape
    tr = R
    for cand in (512, 256, 128, 64):
        if R % cand == 0 and R > cand:
            tr = cand
            break

    def body(w_ref, g_ref, m_ref, v_ref, d_ref, nm_ref, nv_ref):
        d_ref[...], nm_ref[...], nv_ref[...] = _adamw_update(w_ref[...], g_ref[...], m_ref[...], v_ref[...])

    spec = pl.BlockSpec((tr, C), lambda i: (i, 0))
    out = jax.ShapeDtypeStruct((R, C), F32)
    return pl.pallas_call(
        body, name=name, grid=(R // tr,),
        in_specs=[spec] * 4, out_specs=[spec] * 3, out_shape=[out, out, out],
        compiler_params=_params(1),
    )(w, g, m, v)


def _pack_small(values, scalar=None):
    pad = lambda a: jnp.pad(a, ((0, 0), (0, D_MODEL - a.shape[1])))
    last = jnp.zeros((1, D_MODEL), F32) if scalar is None else pad(scalar.reshape(1, 1))
    return jnp.concatenate([pad(values[name].reshape(rows, cols)) for name, _, rows, cols in SMALL_LAYOUT] + [last],
                           axis=0)


def _small_update(all_packs, state, name):
    n = len(SMALL_LAYOUT)

    def body(packs_ref, *refs):
        ins, outs = refs[:3 * n], refs[3 * n:]
        total = packs_ref[0]
        for s in range(1, N_DEV):
            total = total + packs_ref[s]
        for j, (_, at, rows, cols) in enumerate(SMALL_LAYOUT):
            g = total[at:at + rows, :cols]
            w_ref, m_ref, v_ref = ins[3 * j:3 * j + 3]
            outs[4 * j][...] = g
            outs[4 * j + 1][...], outs[4 * j + 2][...], outs[4 * j + 3][...] = _adamw_update(
                w_ref[...], g, m_ref[...], v_ref[...])
        outs[-2][...] = total[LOSS_ROW:LOSS_ROW + 1, :LANES]
        outs[-1][...] = total[SMALL_ROWS:, :]

    shapes = [jax.ShapeDtypeStruct((rows, cols), F32) for _, _, rows, cols in SMALL_LAYOUT for _ in range(4)]
    shapes += [jax.ShapeDtypeStruct((1, LANES), F32), jax.ShapeDtypeStruct((PACK_ROWS - SMALL_ROWS, D_MODEL), F32)]
    operands = [a for item in SMALL_LAYOUT for a in state[item[0]]]
    res = pl.pallas_call(body, name=name, out_shape=shapes, compiler_params=_params())(all_packs, *operands)
    per_name = {item[0]: tuple(res[4 * j:4 * j + 4]) for j, item in enumerate(SMALL_LAYOUT)}
    return per_name, res[-2][0, 0], res[-1]


def kernel(x, p, norm_g, w_in, attn_out_g, dw_w, dw_b, conv_ln_g, conv_ln_b, w_pw, conv_out_g, w_out, ple_norm_g, w_ple_gate, w_ple, final_g, loss_target, m_norm_g, m_w_in, m_attn_out_g, m_dw_w, m_dw_b, m_conv_ln_g, m_conv_ln_b, m_w_pw, m_conv_out_g, m_w_out, m_ple_norm_g, m_w_ple_gate, m_w_ple, m_final_g, v_norm_g, v_w_in, v_attn_out_g, v_dw_w, v_dw_b, v_conv_ln_g, v_conv_ln_b, v_w_pw, v_conv_out_g, v_w_out, v_ple_norm_g, v_w_ple_gate, v_w_ple, v_final_g):
    depth = w_in.shape[0]
    T = x.shape[1]
    given = dict(
        norm_g=norm_g, ple_norm_g=ple_norm_g, final_g=final_g, dw_b=dw_b, conv_ln_g=conv_ln_g, conv_ln_b=conv_ln_b,
        conv_out_g=conv_out_g, attn_out_g=attn_out_g,
        m_norm_g=m_norm_g, m_ple_norm_g=m_ple_norm_g, m_final_g=m_final_g, m_dw_b=m_dw_b, m_conv_ln_g=m_conv_ln_g,
        m_conv_ln_b=m_conv_ln_b, m_conv_out_g=m_conv_out_g, m_attn_out_g=m_attn_out_g,
        v_norm_g=v_norm_g, v_ple_norm_g=v_ple_norm_g, v_final_g=v_final_g, v_dw_b=v_dw_b, v_conv_ln_g=v_conv_ln_g,
        v_conv_ln_b=v_conv_ln_b, v_conv_out_g=v_conv_out_g, v_attn_out_g=v_attn_out_g)
    my_idx = 4 * lax.axis_index("x") + 2 * lax.axis_index("y") + lax.axis_index("c")

    ids = jnp.arange(BLK)
    tri = (ids[:, None] >= ids[None, :]).astype(BF16)
    tri_t = (ids[:, None] <= ids[None, :]).astype(BF16)
    hid = jnp.arange(ATTN_DIM) // HEAD_DIM
    head_mean = ((hid[:, None] == hid[None, :]).astype(F32) / HEAD_DIM).astype(BF16)

    w_names = ("w_in_t", "w_pw", "w_out", "w_gate", "w_ple")
    w_axes = dict(zip(w_names, (0, 0, 0, 0, 1)))
    shards = [dict(zip(w_names, (w_in[l].T.astype(BF16), w_pw[l].astype(BF16), w_out[l].astype(BF16),
                                 w_ple_gate[l].astype(BF16), w_ple[l].astype(BF16)))) for l in range(depth)]
    first = _all_gather([shards[0]["w_in_t"]] + [dw_w[l].T for l in range(depth)], [0] * (1 + depth),
                        "gather_weights_0")
    layers = []
    for l in range(depth):
        layers.append(dict(
            dw_w=first[1 + l].T,
            g_norm=norm_g[l][None], g_attn=jnp.tile(attn_out_g[l], N_HEADS)[None], dw_b=dw_b[l][None],
            ln_g=conv_ln_g[l][None], ln_b=conv_ln_b[l][None], g_conv=conv_out_g[l][None],
            g_ple=ple_norm_g[l][None], p=p[l, 0]))
    layers[0]["w_in_t"] = first[0]

    def rest_of(l):
        return [_Ride.gather2(shards[l][n], w_axes[n]) for n in w_names[1:]]

    h = x[0]
    saved = []
    for l, w in enumerate(layers):
        (qs, k, v, ug, hn), landed = _prenorm_inproj(h, w["g_norm"], w["w_in_t"], f"inproj_{l}",
                                                     _Ride(rest_of(l) if l == 0 else []))
        w.update(zip(w_names[1:], landed))
        ahead = [_Ride.gather2(shards[l + 1]["w_in_t"], 0)] if l + 1 < depth else []
        (o, cs), landed = _attn_fwd(qs, k, v, tri, f"attn_fwd_{l}", _Ride(ahead + (rest_of(l) if l > 0 else [])))
        if ahead:
            layers[l + 1]["w_in_t"] = landed.pop(0)
        w.update(zip(w_names[1:], landed))
        (conv, c2), _ = _conv_fwd(ug, w["dw_w"], w["dw_b"], w["ln_g"], w["ln_b"], f"conv_fwd_{l}")
        (h2, h1, ycat, hn2, gate, e, c3), _ = _mix_out_ple(
            o, ug, c2, h, w["p"], head_mean, w["g_attn"], w["g_conv"], w["g_ple"],
            w["w_pw"], w["w_out"], w["w_gate"], w["w_ple"], f"mix_{l}")
        saved.append(dict(h=h, qs=qs, k=k, v=v, ug=ug, hn=hn, o=o, cs=cs, conv=conv, c2=c2, h1=h1,
                          ycat=ycat, hn2=hn2, gate=gate, e=e, c3=c3))
        h = h2
    dh, g_final, loss_part = _final_loss(h, loss_target[0], final_g[None], "final_loss")

    small = {}
    dww_parts = [None] * depth
    slots = [dict() for _ in range(depth)]
    g_w_in = None
    for l in reversed(range(depth)):
        w, s = layers[l], saved[l]
        above = [None] if g_w_in is not None else []

        def part(i, above=above, g=g_w_in):
            return [_Ride.scatter(g, 0, above[0], *W_IN_GRAD_PARTS[i])] if above else []

        def scattered(grads, names):
            return [_Ride.scatter(grads[n], w_axes[n]) for n in names]

        (dh1, dh1b, dzg, de, dycat, g_ple_sum), landed = _ple_out_bwd(
            dh, s["gate"], s["e"], s["h1"], w["g_ple"], w["w_gate"], w["w_out"], f"ple_bwd_{l}", _Ride(part(0)))
        above[:1] = landed
        (do, dga, dgc, dc3, dconv, sums), landed = _branch_bwd(
            dycat, s["o"], s["ug"], s["c3"], s["conv"], head_mean, w["g_attn"], w["g_conv"],
            w["ln_g"], w["ln_b"], w["w_pw"], f"branch_bwd_{l}", _Ride(part(1)))
        above[:1] = landed
        grads = dict(
            w_pw=_weight_grad([s["c2"]], dc3, f"grad_w_pw_{l}")[0],
            w_out=_weight_grad([s["ycat"]], dh1b, f"grad_w_out_{l}")[0],
            w_gate=_weight_grad([s["hn2"]], dzg, f"grad_w_gate_{l}")[0],
            w_ple=_weight_grad([w["p"].astype(BF16)], de, f"grad_w_ple_{l}", tk=PLE_DIM)[0])
        (dcv, dcg, dww), landed = _conv_bwd(dconv, s["ug"], w["dw_w"], f"conv_bwd_{l}", _Ride(part(2)))
        above[:1] = landed
        (dq, dk, dv), landed = _attn_bwd(s["qs"], s["k"], s["v"], do, s["cs"], tri, tri_t, f"attn_bwd_{l}",
                                         _Ride(scattered(grads, w_names[1:])))
        slots[l].update(zip(w_names[1:], landed))
        du = [dq, dk, dv, dga, dcv, dcg, dgc]
        g_w_in_here, landed = _weight_grad(du, s["hn"], f"grad_w_in_{l}", ride=_Ride(part(3)))
        if above:
            slots[l + 1]["w_in_t"] = landed[0]
        tail = [_Ride.scatter_chips(_pair_reduce(g_w_in_here, f"pair_reduce_w_in_{l}"))] if l == 0 else []
        (dh, g_norm_sum), landed = _inproj_bwd(du, w["w_in_t"], s["h"], dh1, w["g_norm"], f"inproj_bwd_{l}",
                                               _Ride(tail))
        slots[l].update(zip(("w_in_t",), landed))
        g_w_in = g_w_in_here
        small[l] = dict(norm_g=g_norm_sum, ple_norm_g=g_ple_sum, attn_out_g=sums[0].reshape(N_HEADS, HEAD_DIM).sum(0),
                        conv_out_g=sums[1], conv_ln_g=sums[2], conv_ln_b=sums[3], dw_b=sums[4])
        dww_parts[l] = dww[:CONV_WIDTH]
    slots = [[sl[n] for n in w_names] for sl in slots]
    grad_x = dh[None]

    sums_of = {name: jnp.stack([small[l][name].reshape(-1) for l in range(depth)]) for name in small[0]}
    sums_of["final_g"] = g_final
    pack = jnp.concatenate([_pack_small(sums_of, scalar=loss_part[0, 0]), jnp.concatenate(dww_parts, axis=1),
                            jnp.zeros((PACK_ROWS - SMALL_ROWS - CONV_WIDTH, D_MODEL), F32)], axis=0)
    (all_packs,) = _all_gather([pack], [0], "gather_small_grads")
    state = {name: [given[pre + name].reshape(rows, cols) for pre in ("", "m_", "v_")]
             for name, _, rows, cols in SMALL_LAYOUT}
    updated, loss, dww_sum = _small_update(all_packs.reshape(N_DEV, PACK_ROWS, D_MODEL), state, "update_small")
    res = {kind: {name: val[k].reshape(given[name].shape) for name, val in updated.items()}
           for k, kind in enumerate("gdmv")}
    dww_full = dww_sum[:CONV_WIDTH].reshape(CONV_WIDTH, depth, CONV_DIM).transpose(1, 0, 2)
    g_dw_w = lax.dynamic_slice_in_dim(dww_full, my_idx * (CONV_DIM // N_DEV), CONV_DIM // N_DEV, axis=2)

    swap = lambda a: a.transpose(0, 2, 1)
    state = {"w_in": (w_in, m_w_in, v_w_in), "w_pw": (w_pw, m_w_pw, v_w_pw), "w_out": (w_out, m_w_out, v_w_out),
             "w_ple_gate": (w_ple_gate, m_w_ple_gate, v_w_ple_gate), "w_ple": (w_ple, m_w_ple, v_w_ple)}
    for at, name in enumerate(state):
        wv, mv, vv = [swap(a) for a in state[name]] if name == "w_in" else state[name]
        out = _sum_adamw([slots[l][at] for l in range(depth)], wv, mv, vv, f"adamw_{name}")
        out = [swap(a) for a in out] if name == "w_in" else out
        res["g"][name], res["d"][name], res["m"][name], res["v"][name] = out
    flat = lambda a: a.reshape(-1, a.shape[-1])
    res["g"]["dw_w"] = g_dw_w
    res["d"]["dw_w"], res["m"]["dw_w"], res["v"]["dw_w"] = [
        a.reshape(dw_w.shape) for a in _adamw(flat(dw_w), flat(g_dw_w), flat(m_dw_w), flat(v_dw_w), "adamw_dw_w")]

    order = ["norm_g", "w_in", "attn_out_g", "dw_w", "dw_b", "conv_ln_g", "conv_ln_b", "w_pw", "conv_out_g",
             "w_out", "ple_norm_g", "w_ple_gate", "w_ple", "final_g"]
    return (loss, grad_x, *[res["g"][n] for n in order], *[res["d"][n] for n in order],
            *[res["m"][n] for n in order], *[res["v"][n] for n in order])
```

```python
import functools

import jax
import jax.numpy as jnp
from jax import lax
from jax.experimental import pallas as pl
from jax.experimental.pallas import tpu as pltpu

F32 = jnp.float32
BF16 = jnp.bfloat16
MESH = pl.DeviceIdType.MESH

N_DEV = 8
D_MODEL = 1024
ATTN_DIM = 512
CONV_DIM = 512
HEAD_DIM = 64
N_HEADS = 8
CONV_WIDTH = 31
PLE_DIM = 256
CHUNK = 512
N_CHUNK = 7
EPS = 1e-6
ADAM_LR = 0.001
ADAM_B1 = 0.9
ADAM_B2 = 0.999
ADAM_EPS = 1e-08
ADAM_WD = 0.01
ADAM_STEP = 10

LANES = 128
BLK = 256
ATT_COLS = 4
CHAIN_GROUP = 4
SOFTPLUS_LINEAR_AT = 20.0
DEAD_AT = 110.0
FIRST_BLOCK_LANE = HEAD_DIM - 1
TM = 512
HALO = 32
SUBLANES = 8
CONV_ROWS = 32
ADAMW_ROWS = 64
VMEM_LIMIT = 56 * 1024 * 1024
SMALL_ROWS = 16
SMALL_LAYOUT = (("norm_g", 0, 2, D_MODEL), ("ple_norm_g", 2, 2, D_MODEL), ("final_g", 4, 1, D_MODEL),
                ("dw_b", 5, 2, CONV_DIM), ("conv_ln_g", 7, 2, CONV_DIM), ("conv_ln_b", 9, 2, CONV_DIM),
                ("conv_out_g", 11, 2, CONV_DIM), ("attn_out_g", 13, 2, HEAD_DIM))
LOSS_ROW = 15
W_IN_GRAD_PARTS = ((0, 96), (96, 80), (176, 144), (320, 128))
PACK_ROWS = 48


def _nn(a, b):
    return lax.dot_general(a, b, (((1,), (0,)), ((), ())), preferred_element_type=F32)


def _nt(a, b):
    return lax.dot_general(a, b, (((1,), (1,)), ((), ())), preferred_element_type=F32)


def _tn(a, b):
    return lax.dot_general(a, b, (((0,), (0,)), ((), ())), preferred_element_type=F32)


def _split(x):
    hi = x.astype(BF16)
    lo = (x - hi.astype(F32)).astype(BF16)
    return hi, lo


def _dot_hilo(x, m):
    hi, lo = _split(x)
    return _nn(hi, m) + _nn(lo, m)


def _sigmoid(x):
    return jax.nn.sigmoid(x)


def _dsilu(x, s):
    return s * (1.0 + x * (1.0 - s))


def _params(n_grid=0, vmem=VMEM_LIMIT):
    sem = ("arbitrary",) * n_grid if n_grid else None
    return pltpu.CompilerParams(dimension_semantics=sem, vmem_limit_bytes=vmem)


def _rows(tm, cols, col=0):
    return pl.BlockSpec((tm, cols), lambda i: (i, col))


def _whole(shape):
    zeros = (0,) * len(shape)
    return pl.BlockSpec(shape, lambda *_: zeros)


def _my_position():
    return lax.axis_index("x"), lax.axis_index("y"), lax.axis_index("c")


def _block(ref, axis, idx, size):
    start = pl.multiple_of(idx * size, size)
    if axis == 0:
        return ref.at[pl.ds(start, size), :]
    return ref.at[:, pl.ds(start, size)]


def _all_gather(shards, axes, name):
    n = len(shards)
    sizes = [s.shape[a] for s, a in zip(shards, axes)]

    def full_shape(s, a):
        shape = list(s.shape)
        shape[a] *= N_DEV
        return jax.ShapeDtypeStruct(tuple(shape), s.dtype)

    def body(*refs):
        ins, outs = refs[:n], refs[n:2 * n]
        send_sems, recv_sems, local_sems = refs[2 * n:]
        x, y, c = _my_position()
        me, sibling = (x, y, c), (x, y, 1 - c)
        chips = [(1 - x, y), (x, 1 - y), (1 - x, 1 - y)]

        def place(i, dev):
            return _block(outs[i], axes[i], 4 * dev[0] + 2 * dev[1] + dev[2], sizes[i])

        def copy(k, i, dev, to, src=None):
            return pltpu.make_async_remote_copy(
                src_ref=place(i, dev) if src is None else src, dst_ref=place(i, dev),
                send_sem=send_sems.at[k, i], recv_sem=recv_sems.at[k, i],
                device_id=to, device_id_type=MESH)

        mine = [pltpu.make_async_copy(ins[i], place(i, me), local_sems.at[i]) for i in range(n)]
        for cp in mine:
            cp.start()
        first = [copy(0, i, me, sibling, src=ins[i]) for i in range(n)]
        for j, chip in enumerate(chips):
            first += [copy(1 + j, i, me, (*chip, c), src=ins[i]) for i in range(n)]
        for cp in first:
            cp.start()
        passed = []
        for j, chip in enumerate(chips):
            for i in range(n):
                copy(1 + j, i, (*chip, c), me).wait_recv()
            hop = [copy(4 + j, i, (*chip, c), sibling) for i in range(n)]
            for cp in hop:
                cp.start()
            passed += hop
        for i in range(n):
            copy(0, i, sibling, me).wait_recv()
        for j, chip in enumerate(chips):
            for i in range(n):
                copy(4 + j, i, (*chip, 1 - c), me).wait_recv()
        for cp in first + passed:
            cp.wait_send()
        for cp in mine:
            cp.wait()

    any_spec = pl.BlockSpec(memory_space=pl.ANY)
    return pl.pallas_call(
        body, name=name,
        out_shape=[full_shape(s, a) for s, a in zip(shards, axes)],
        in_specs=[any_spec] * n, out_specs=[any_spec] * n,
        scratch_shapes=[pltpu.SemaphoreType.DMA((7, n)), pltpu.SemaphoreType.DMA((7, n)),
                        pltpu.SemaphoreType.DMA((n,))],
    )(*shards)


def _pair_reduce(g, name):
    n_chips = N_DEV // 2
    R, C = g.shape[0] // N_DEV, g.shape[1]

    def body(g_ref, out_ref, mine_ref, theirs_ref, send_sems, recv_sems, local_sems):
        x, y, c = _my_position()
        block = lambda d: g_ref.at[pl.ds(pl.multiple_of(d * R, 16), R), :]
        sends = [pltpu.make_async_remote_copy(
            src_ref=block(2 * j + 1 - c), dst_ref=theirs_ref.at[j], send_sem=send_sems.at[j],
            recv_sem=recv_sems.at[j], device_id=(x, y, 1 - c), device_id_type=MESH) for j in range(n_chips)]
        own = [pltpu.make_async_copy(block(2 * j + c), mine_ref.at[j], local_sems.at[j]) for j in range(n_chips)]
        for cp in sends + own:
            cp.start()
        for j in range(n_chips):
            own[j].wait()
            sends[j].wait_recv()
            out_ref[j] = (mine_ref[j].astype(F32) + theirs_ref[j].astype(F32)).astype(g.dtype)
        for cp in sends:
            cp.wait_send()

    half = pltpu.VMEM((n_chips, R, C), g.dtype)
    sems = pltpu.SemaphoreType.DMA((n_chips,))
    return pl.pallas_call(
        body, name=name, out_shape=jax.ShapeDtypeStruct((n_chips, R, C), g.dtype),
        in_specs=[pl.BlockSpec(memory_space=pl.ANY)], out_specs=pl.BlockSpec(memory_space=pltpu.VMEM),
        scratch_shapes=[half, half, sems, sems, sems], compiler_params=_params(),
    )(g)


class _Ride:
    def __init__(self, parts):
        self.parts = [p for p in parts if p is not None]

    @staticmethod
    def gather(src, axis, land=None, lo=0, n=None):
        return ("gather", src, land, axis, lo, src.shape[axis] if n is None else n)

    @staticmethod
    def gather2(src, axis, land=None, lo=0, n=None):
        return ("gather2", src, land, axis, lo, src.shape[axis] if n is None else n)

    @staticmethod
    def scatter(src, axis, land=None, lo=0, n=None):
        return ("scatter", src, land, axis, lo, src.shape[axis] // N_DEV if n is None else n)

    @staticmethod
    def scatter_chips(chip_sums):
        return ("scatter_chips", chip_sums, None, 0, 0, chip_sums.shape[1])

    def arrays(self):
        return [p[1] for p in self.parts] + [p[2] for p in self.parts if p[2] is not None]

    def out_shapes(self):
        out = []
        for kind, src, _, axis, _, _ in self.parts:
            shape = list(src.shape)
            if kind in ("gather", "gather2"):
                shape[axis] *= N_DEV
            elif kind == "scatter_chips":
                pass
            else:
                shape[axis] //= N_DEV
                shape = [N_DEV] + shape
            out.append(jax.ShapeDtypeStruct(tuple(shape), src.dtype))
        return out

    def aliases(self, n_in, n_out):
        m, out = len(self.parts), {}
        for j, p in enumerate(self.parts):
            if p[2] is not None:
                out[n_in + m + len(out)] = n_out + j
        return out

    def scratch(self):
        m = len(self.parts)
        return [pltpu.SemaphoreType.DMA((N_DEV - 1, m)), pltpu.SemaphoreType.DMA((N_DEV - 1, m)),
                pltpu.SemaphoreType.DMA((m,))]

    def _copies(self, src_refs, land_refs, sems):
        send_sems, recv_sems, local_sems = sems
        x, y, c = _my_position()
        my_idx = 4 * x + 2 * y + c
        own, sends, relays, lands = [], [], [], []
        for j, (kind, src, _, axis, lo, n) in enumerate(self.parts):
            if kind == "scatter_chips":
                for k in (0, 2, 4, 6):
                    px, py = (1 - x if k & 4 else x), (1 - y if k & 2 else y)
                    a, b = src_refs[j].at[2 * px + py], land_refs[j].at[2 * x + y]
                    if k == 0:
                        own.append(pltpu.make_async_copy(a, b, local_sems.at[j]))
                        continue
                    mk = lambda dst, a=a, k=k, j=j, to=(px, py, c): pltpu.make_async_remote_copy(
                        src_ref=a, dst_ref=dst, send_sem=send_sems.at[k - 1, j], recv_sem=recv_sems.at[k - 1, j],
                        device_id=to, device_id_type=MESH)
                    sends.append(mk(b))
                    lands.append(mk(land_refs[j].at[2 * px + py]))
                continue
            size = src.shape[axis] if kind in ("gather", "gather2") else src.shape[axis] // N_DEV
            align = 16 if axis == 0 else LANES

            def rows(ref, idx, lead=None, axis=axis, lo=lo, n=n, size=size, align=align):
                at = pl.ds(pl.multiple_of(idx * size + lo, align), n)
                where = (at, slice(None)) if axis == 0 else (slice(None), at)
                return ref.at[where] if lead is None else ref.at[(lead, *where)]

            def in_shard(ref):
                return rows(ref, 0)

            def in_slot(ref, s):
                return rows(ref, 0, lead=s)

            if kind == "gather2":
                chips = [(1 - x, y), (x, 1 - y), (1 - x, 1 - y)]
                place = lambda px, py, pc: rows(land_refs[j], 4 * px + 2 * py + pc)

                def copy(i, a, dst, to, j=j):
                    return pltpu.make_async_remote_copy(
                        src_ref=a, dst_ref=dst, send_sem=send_sems.at[i, j], recv_sem=recv_sems.at[i, j],
                        device_id=to, device_id_type=MESH)

                mine = in_shard(src_refs[j])
                own.append(pltpu.make_async_copy(mine, place(x, y, c), local_sems.at[j]))
                sends.append(copy(0, mine, place(x, y, c), (x, y, 1 - c)))
                lands.append(copy(0, mine, place(x, y, 1 - c), (x, y, 1 - c)))
                for i, (px, py) in enumerate(chips):
                    sends.append(copy(1 + i, mine, place(x, y, c), (px, py, c)))
                    relays.append((copy(1 + i, mine, place(px, py, c), (px, py, c)),
                                   copy(4 + i, place(px, py, c), place(px, py, c), (x, y, 1 - c))))
                    lands.append(copy(4 + i, mine, place(px, py, 1 - c), (x, y, 1 - c)))
                continue
            for k in range(N_DEV):
                px = 1 - x if k & 4 else x
                py = 1 - y if k & 2 else y
                pc = 1 - c if k & 1 else c
                peer_idx = 4 * px + 2 * py + pc
                if kind == "gather":
                    a, b, landed = in_shard(src_refs[j]), rows(land_refs[j], my_idx), rows(land_refs[j], peer_idx)
                else:
                    a, b, landed = rows(src_refs[j], peer_idx), in_slot(land_refs[j], my_idx), in_slot(land_refs[j], peer_idx)
                if k == 0:
                    own.append(pltpu.make_async_copy(a, b, local_sems.at[j]))
                    continue
                mk = lambda dst, a=a, k=k, j=j, to=(px, py, pc): pltpu.make_async_remote_copy(
                    src_ref=a, dst_ref=dst, send_sem=send_sems.at[k - 1, j], recv_sem=recv_sems.at[k - 1, j],
                    device_id=to, device_id_type=MESH)
                sends.append(mk(b))
                lands.append(mk(landed))
        return own, sends, relays, lands

    @property
    def relayed(self):
        return any(p[0] == "gather2" for p in self.parts)

    def start(self, src_refs, land_refs, sems):
        own, sends, _, _ = self._copies(src_refs, land_refs, sems)
        for cp in own + sends:
            cp.start()

    def relay(self, src_refs, land_refs, sems):
        for arrival, onward in self._copies(src_refs, land_refs, sems)[2]:
            arrival.wait_recv()
            onward.start()

    def wait(self, src_refs, land_refs, sems):
        own, sends, relays, lands = self._copies(src_refs, land_refs, sems)
        for cp in lands:
            cp.wait_recv()
        for cp in sends + [onward for _, onward in relays]:
            cp.wait_send()
        for cp in own:
            cp.wait()


def _call(body, *, name, grid, in_specs, out_specs, out_shape, args, scratch_shapes=(), ride=None):
    in_specs, out_specs, out_shape = list(in_specs), list(out_specs), list(out_shape)
    n_in, n_out, n_sc = len(in_specs), len(out_specs), len(scratch_shapes)
    if ride is None or not ride.parts:
        res = pl.pallas_call(body, name=name, grid=grid, in_specs=in_specs, out_specs=out_specs,
                             out_shape=out_shape, scratch_shapes=list(scratch_shapes),
                             compiler_params=_params(len(grid)))(*args)
        return list(res), []
    extra, m = ride.arrays(), len(ride.parts)

    def riding(*refs):
        a = n_in + len(extra)
        b = a + n_out
        srcs, lands, sems = refs[n_in:n_in + m], refs[b:b + m], refs[b + m + n_sc:]
        at = [pl.program_id(d) for d in range(len(grid))]

        @pl.when(functools.reduce(jnp.logical_and, [i == 0 for i in at]))
        def _():
            ride.start(srcs, lands, sems)

        if ride.relayed:
            step, n_steps = at[0], 1
            for i, g in zip(at[1:], grid[1:]):
                step = step * g + i
            for g in grid:
                n_steps *= g
            assert n_steps >= 2, "a two-level ride needs a grid step after the first"

            @pl.when(step == n_steps - 1)
            def _():
                ride.relay(srcs, lands, sems)

        body(*refs[:n_in], *refs[a:b], *refs[b + m:b + m + n_sc])

        @pl.when(functools.reduce(jnp.logical_and, [i == g - 1 for i, g in zip(at, grid)]))
        def _():
            ride.wait(srcs, lands, sems)

    hbm = pl.BlockSpec(memory_space=pl.ANY)
    res = pl.pallas_call(
        riding, name=name, grid=grid, in_specs=in_specs + [hbm] * len(extra), out_specs=out_specs + [hbm] * m,
        out_shape=out_shape + ride.out_shapes(), scratch_shapes=list(scratch_shapes) + ride.scratch(),
        input_output_aliases=ride.aliases(n_in, n_out), compiler_params=_params(len(grid)),
    )(*args, *extra)
    return list(res[:n_out]), list(res[n_out:])


def _prenorm_inproj(h, gain, w_in_t, name, ride=None):
    T = h.shape[0]

    def body(h_ref, g_ref, w_ref, q_ref, k_ref, v_ref, ug_ref, hn_ref):
        hv = h_ref[...]
        r = lax.rsqrt(jnp.mean(hv * hv, axis=-1, keepdims=True) + EPS)
        hn = (hv * r * g_ref[...]).astype(BF16)
        hn_ref[...] = hn
        for j in range(N_CHUNK):
            u = _nt(hn, w_ref[j * CHUNK:(j + 1) * CHUNK, :])
            if j == 0:
                q_ref[...] = (u * (HEAD_DIM ** -0.5)).astype(BF16)
            elif j == 1:
                k_ref[...] = u.astype(BF16)
            elif j == 2:
                v_ref[...] = u.astype(BF16)
            else:
                ug_ref[:, (j - 3) * CHUNK:(j - 2) * CHUNK] = u

    act = jax.ShapeDtypeStruct((T, CHUNK), BF16)
    return _call(
        body, name=name, grid=(T // TM,),
        in_specs=[_rows(TM, D_MODEL), _whole((1, D_MODEL)), _whole((N_CHUNK * CHUNK, D_MODEL))],
        out_specs=[_rows(TM, CHUNK)] * 3 + [_rows(TM, 4 * CHUNK), _rows(TM, D_MODEL)],
        out_shape=[act, act, act, jax.ShapeDtypeStruct((T, 4 * CHUNK), F32),
                   jax.ShapeDtypeStruct((T, D_MODEL), BF16)],
        args=(h, gain, w_in_t,), ride=ride)


def _softplus_parts(z):
    ez = jnp.exp(jnp.minimum(z, SOFTPLUS_LINEAR_AT))
    t = 1.0 + ez
    return ez * pl.reciprocal(t, approx=True), jnp.where(z > SOFTPLUS_LINEAR_AT, z, jnp.log(t))


def _attn_fwd(qs, k, v, tri, name, ride=None):
    T = qs.shape[0]
    assert T // BLK <= FIRST_BLOCK_LANE, "one lane per key block below the lane of the first block"
    width = LANES * ATT_COLS
    chains = [(c, half) for c in range(ATT_COLS) for half in range(2)]

    def body(q_ref, k_ref, v_ref, m_ref, o_ref, cs_ref):
        qi = pl.program_id(1)
        lane = lax.broadcasted_iota(jnp.int32, (BLK, LANES), 1)
        first = lane < HEAD_DIM
        causal = (lax.broadcasted_iota(jnp.int32, (BLK, BLK), 1)
                  < lax.broadcasted_iota(jnp.int32, (BLK, BLK), 0))
        tri_m = m_ref[...]
        qh = {}
        for c in range(ATT_COLS):
            q = q_ref[:, c * LANES:(c + 1) * LANES]
            zero = jnp.zeros_like(q)
            qh[c, 0], qh[c, 1] = jnp.where(first, q, zero), jnp.where(first, zero, q)

        def step(kb, state, masked):
            carries, accs, cvals = state
            start = pl.multiple_of(kb * BLK, BLK)
            kblk = [k_ref[pl.ds(start, BLK), c * LANES:(c + 1) * LANES] for c in range(ATT_COLS)]
            vblk = [v_ref[pl.ds(start, BLK), c * LANES:(c + 1) * LANES] for c in range(ATT_COLS)]
            carries, accs, cvals = list(carries), list(accs), list(cvals)
            for g0 in range(0, len(chains), CHAIN_GROUP):
                ids = range(g0, g0 + CHAIN_GROUP)
                z = [_nt(qh[chains[n]], kblk[chains[n][0]]) for n in ids]
                sp = [_softplus_parts(zi)[1] for zi in z]
                if masked:
                    sp = [jnp.where(causal, s, 0.0) for s in sp]
                incl = [_dot_hilo(s, tri_m) for s in sp]
                a = [jnp.exp(zi - ii - carries[n]) for n, zi, ii in zip(ids, z, incl)]
                if masked:
                    a = [jnp.where(causal, ai, 0.0) for ai in a]
                for n, ai, ii in zip(ids, a, incl):
                    c, half = chains[n]
                    zero = jnp.zeros_like(vblk[c])
                    vh = jnp.where(first, vblk[c], zero) if half == 0 else jnp.where(first, zero, vblk[c])
                    accs[c] = accs[c] + _nn(ai.astype(BF16), vh)
                    cvals[c] = jnp.where(lane == kb + HEAD_DIM * half, carries[n], cvals[c])
                    carries[n] = carries[n] + ii[:, 0:1]
            return tuple(carries), tuple(accs), tuple(cvals)

        zeros = tuple(jnp.zeros((BLK, LANES), F32) for _ in range(ATT_COLS))
        state = (tuple(jnp.zeros((BLK, 1), F32) for _ in chains), zeros, zeros)
        state = step(qi, state, True)

        def reaches_further(st):
            it, (carries, _, _) = st
            least = functools.reduce(jnp.minimum, carries)
            return jnp.logical_and(it < qi, jnp.min(least) < DEAD_AT)

        done, state = lax.while_loop(reaches_further, lambda st: (st[0] + 1, step(qi - 1 - st[0], st[1], False)),
                                     (jnp.int32(0), state))
        first_block = (qi - done).astype(F32)
        for c in range(ATT_COLS):
            o_ref[:, c * LANES:(c + 1) * LANES] = state[1][c]
            cs_ref[:, c * LANES:(c + 1) * LANES] = jnp.where(lane == FIRST_BLOCK_LANE, first_block, state[2][c])

    blk = pl.BlockSpec((BLK, width), lambda j, i: (i, j))
    col = pl.BlockSpec((T, width), lambda j, i: (0, j))
    out = jax.ShapeDtypeStruct((T, ATTN_DIM), F32)
    return _call(
        body, name=name, grid=(ATTN_DIM // width, T // BLK),
        in_specs=[blk, col, col, _whole((BLK, BLK))],
        out_specs=[blk, blk], out_shape=[out, out],
        args=(qs, k, v, tri,), ride=ride)


def _shifted_copies(pad_ref, sh_ref):
    rows = sh_ref.shape[1]
    for b in range(SUBLANES):
        sh_ref[b] = pad_ref[b:b + rows, :]


def _shift_of(offset):
    return offset % SUBLANES, offset - offset % SUBLANES


def _conv_fwd(ug, dw_w, dw_b, ln_g, ln_b, name, ride=None):
    T = ug.shape[0]
    per = TM // HALO

    def body(cv_ref, cg_ref, cvh_ref, cgh_ref, w_ref, b_ref, g_ref, beta_ref, conv_ref, c2_ref, pad_ref, sh_ref):
        i = pl.program_id(0)
        halo = cvh_ref[...] * _sigmoid(cgh_ref[...])
        pad_ref[0:HALO, :] = jnp.where(i == 0, 0.0, halo)
        pad_ref[HALO:HALO + TM, :] = cv_ref[...] * _sigmoid(cg_ref[...])
        pad_ref[HALO + TM:, :] = jnp.zeros((SUBLANES, CONV_DIM), F32)
        _shifted_copies(pad_ref, sh_ref)
        taps = [w_ref[t:t + 1, :] for t in range(CONV_WIDTH)]

        def rows(j, _):
            r = pl.multiple_of(j * CONV_ROWS, CONV_ROWS)
            acc = jnp.zeros((CONV_ROWS, CONV_DIM), F32) + b_ref[...]
            for t in range(CONV_WIDTH):
                b, a = _shift_of(HALO - (CONV_WIDTH - 1) + t)
                acc = acc + taps[t] * sh_ref[b, pl.ds(r + a, CONV_ROWS), :]
            conv_ref[pl.ds(r, CONV_ROWS), :] = acc
            return 0

        lax.fori_loop(0, TM // CONV_ROWS, rows, 0)
        acc = conv_ref[...]
        mu = jnp.mean(acc, axis=-1, keepdims=True)
        xc = acc - mu
        rs = lax.rsqrt(jnp.mean(xc * xc, axis=-1, keepdims=True) + EPS)
        ln = xc * rs * g_ref[...] + beta_ref[...]
        c2_ref[...] = (ln * _sigmoid(ln)).astype(BF16)

    prev = lambda col: pl.BlockSpec((HALO, CHUNK), lambda i: (jnp.maximum(i * per - 1, 0), col))
    vec = _whole((1, CONV_DIM))
    return _call(
        body, name=name, grid=(T // TM,),
        in_specs=[_rows(TM, CHUNK, 1), _rows(TM, CHUNK, 2), prev(1), prev(2),
                  _whole((CONV_WIDTH, CONV_DIM)), vec, vec, vec],
        out_specs=[_rows(TM, CONV_DIM), _rows(TM, CONV_DIM)],
        out_shape=[jax.ShapeDtypeStruct((T, CONV_DIM), F32), jax.ShapeDtypeStruct((T, CONV_DIM), BF16)],
        scratch_shapes=[pltpu.VMEM((TM + HALO + SUBLANES, CONV_DIM), F32),
                        pltpu.VMEM((SUBLANES, TM + HALO, CONV_DIM), F32)],
        args=(ug, ug, ug, ug, dw_w, dw_b, ln_g, ln_b,), ride=ride)


def _mix_out_ple(o, ug, c2, h, p, head_mean, g_attn, g_conv, g_ple, w_pw, w_out, w_gate, w_ple, name, ride=None):
    T = h.shape[0]

    def body(o_ref, ga_ref, gc_ref, c2_ref, h_ref, p_ref, hm_ref, gao_ref, gco_ref, gpn_ref,
             wpw_ref, wout_ref, wg_ref, wple_ref,
             h2_ref, h1_ref, ycat_ref, hn2_ref, gate_ref, e_ref, c3_ref):
        ov = o_ref[...]
        rh = lax.rsqrt(_nn((ov * ov).astype(BF16), hm_ref[...]) + EPS)
        ga = ga_ref[...]
        ya = (ov * rh * gao_ref[...] * (ga * _sigmoid(ga))).astype(BF16)
        c3 = _nn(c2_ref[...], wpw_ref[...])
        c3_ref[...] = c3
        rc = lax.rsqrt(jnp.mean(c3 * c3, axis=-1, keepdims=True) + EPS)
        gc = gc_ref[...]
        yc = (c3 * rc * gco_ref[...] * (gc * _sigmoid(gc))).astype(BF16)
        ycat_ref[:, :ATTN_DIM] = ya
        ycat_ref[:, ATTN_DIM:] = yc
        h1 = h_ref[...] + _nn(ya, wout_ref[:ATTN_DIM, :]) + _nn(yc, wout_ref[ATTN_DIM:, :])
        h1_ref[...] = h1
        r1 = lax.rsqrt(jnp.mean(h1 * h1, axis=-1, keepdims=True) + EPS)
        hn2 = (h1 * r1 * gpn_ref[...]).astype(BF16)
        hn2_ref[...] = hn2
        gate = _sigmoid(_nn(hn2, wg_ref[...]))
        e = _nn(p_ref[...].astype(BF16), wple_ref[...])
        gate_ref[...] = gate
        e_ref[...] = e
        h2_ref[...] = h1 + e * gate

    f32 = lambda cols: jax.ShapeDtypeStruct((T, cols), F32)
    bf = lambda cols: jax.ShapeDtypeStruct((T, cols), BF16)
    return _call(
        body, name=name, grid=(T // TM,),
        in_specs=[_rows(TM, ATTN_DIM), _rows(TM, CHUNK, 0), _rows(TM, CHUNK, 3), _rows(TM, CONV_DIM),
                  _rows(TM, D_MODEL), _rows(TM, PLE_DIM), _whole((ATTN_DIM, ATTN_DIM)),
                  _whole((1, ATTN_DIM)), _whole((1, CONV_DIM)), _whole((1, D_MODEL)),
                  _whole((CONV_DIM, CONV_DIM)), _whole((D_MODEL, D_MODEL)), _whole((D_MODEL, D_MODEL)),
                  _whole((PLE_DIM, D_MODEL))],
        out_specs=[_rows(TM, D_MODEL), _rows(TM, D_MODEL), _rows(TM, D_MODEL), _rows(TM, D_MODEL),
                   _rows(TM, D_MODEL), _rows(TM, D_MODEL), _rows(TM, CONV_DIM)],
        out_shape=[f32(D_MODEL), f32(D_MODEL), bf(D_MODEL), bf(D_MODEL), f32(D_MODEL), f32(D_MODEL),
                   f32(CONV_DIM)],
        args=(o, ug, ug, c2, h, p, head_mean, g_attn, g_conv, g_ple, w_pw, w_out, w_gate, w_ple,), ride=ride)


def _final_loss(h, target, gain, name):
    T = h.shape[0]

    def body(h_ref, t_ref, g_ref, dh_ref, gsum_ref, loss_ref):
        @pl.when(pl.program_id(0) == 0)
        def _():
            gsum_ref[...] = jnp.zeros_like(gsum_ref)
            loss_ref[...] = jnp.zeros_like(loss_ref)

        hv = h_ref[...]
        r = lax.rsqrt(jnp.mean(hv * hv, axis=-1, keepdims=True) + EPS)
        xh = hv * r
        diff = xh * g_ref[...] - t_ref[...]
        loss_ref[...] += 0.5 * jnp.sum(jnp.mean(diff * diff, axis=-1, keepdims=True), axis=0, keepdims=True)
        dy = diff * (1.0 / D_MODEL)
        gsum_ref[...] += jnp.sum(dy * xh, axis=0, keepdims=True)
        dxh = dy * g_ref[...]
        dh_ref[...] = r * (dxh - xh * jnp.mean(dxh * xh, axis=-1, keepdims=True))

    return pl.pallas_call(
        body, name=name, grid=(T // TM,),
        in_specs=[_rows(TM, D_MODEL), _rows(TM, D_MODEL), _whole((1, D_MODEL))],
        out_specs=[_rows(TM, D_MODEL), _whole((1, D_MODEL)), _whole((1, LANES))],
        out_shape=[jax.ShapeDtypeStruct((T, D_MODEL), F32), jax.ShapeDtypeStruct((1, D_MODEL), F32),
                   jax.ShapeDtypeStruct((1, LANES), F32)],
        compiler_params=_params(1),
    )(h, target, gain)


def _ple_out_bwd(dh2, gate, e, h1, g_ple, w_gate, w_out, name, ride=None):
    T = dh2.shape[0]

    def body(dh2_ref, gate_ref, e_ref, h1_ref, gpn_ref, wg_ref, wout_ref,
             dh1_ref, dh1b_ref, dzg_ref, de_ref, dycat_ref, gsum_ref):
        @pl.when(pl.program_id(0) == 0)
        def _():
            gsum_ref[...] = jnp.zeros_like(gsum_ref)

        dh2v = dh2_ref[...]
        gate = gate_ref[...]
        de_ref[...] = (dh2v * gate).astype(BF16)
        dzg = (dh2v * e_ref[...] * gate * (1.0 - gate)).astype(BF16)
        dzg_ref[...] = dzg
        dhn2 = _nt(dzg, wg_ref[...])
        h1 = h1_ref[...]
        r1 = lax.rsqrt(jnp.mean(h1 * h1, axis=-1, keepdims=True) + EPS)
        xh = h1 * r1
        gsum_ref[...] += jnp.sum(dhn2 * xh, axis=0, keepdims=True)
        dxh = dhn2 * gpn_ref[...]
        dh1 = dh2v + r1 * (dxh - xh * jnp.mean(dxh * xh, axis=-1, keepdims=True))
        dh1_ref[...] = dh1
        dh1b = dh1.astype(BF16)
        dh1b_ref[...] = dh1b
        dycat_ref[...] = _nt(dh1b, wout_ref[...])

    f32 = jax.ShapeDtypeStruct((T, D_MODEL), F32)
    bf = jax.ShapeDtypeStruct((T, D_MODEL), BF16)
    full = _rows(TM, D_MODEL)
    return _call(
        body, name=name, grid=(T // TM,),
        in_specs=[full, full, full, full, _whole((1, D_MODEL)), _whole((D_MODEL, D_MODEL)),
                  _whole((D_MODEL, D_MODEL))],
        out_specs=[full, full, full, full, full, _whole((1, D_MODEL))],
        out_shape=[f32, bf, bf, bf, f32, jax.ShapeDtypeStruct((1, D_MODEL), F32)],
        args=(dh2, gate, e, h1, g_ple, w_gate, w_out,), ride=ride)


def _branch_bwd(dycat, o, ug, c3, conv, head_mean, g_attn, g_conv, ln_g, ln_b, w_pw, name, ride=None):
    T = o.shape[0]

    def body(dya_ref, dyc_ref, o_ref, ga_ref, gc_ref, c3_ref, conv_ref, hm_ref, gao_ref, gco_ref,
             lng_ref, lnb_ref, wpw_ref,
             do_ref, dga_ref, dgc_ref, dc3_ref, dconv_ref, sums_ref):
        @pl.when(pl.program_id(0) == 0)
        def _():
            sums_ref[...] = jnp.zeros_like(sums_ref)

        hm = hm_ref[...]
        col = lambda x: jnp.sum(x, axis=0, keepdims=True)
        ov = o_ref[...]
        rh = lax.rsqrt(_nn((ov * ov).astype(BF16), hm) + EPS)
        xh = ov * rh
        ga = ga_ref[...]
        sg = _sigmoid(ga)
        dya = dya_ref[...]
        don = dya * (ga * sg)
        dga_ref[...] = (dya * xh * gao_ref[...] * _dsilu(ga, sg)).astype(BF16)
        sums_ref[0:1, :] += col(don * xh)
        dxh = don * gao_ref[...]
        do_ref[...] = (rh * (dxh - xh * _dot_hilo(dxh * xh, hm))).astype(BF16)
        c3 = c3_ref[...]
        rc = lax.rsqrt(jnp.mean(c3 * c3, axis=-1, keepdims=True) + EPS)
        xh3 = c3 * rc
        gc = gc_ref[...]
        sgc = _sigmoid(gc)
        dyc = dyc_ref[...]
        dn3 = dyc * (gc * sgc)
        dgc_ref[...] = (dyc * xh3 * gco_ref[...] * _dsilu(gc, sgc)).astype(BF16)
        sums_ref[1:2, :] += col(dn3 * xh3)
        dxh3 = dn3 * gco_ref[...]
        dc3 = (rc * (dxh3 - xh3 * jnp.mean(dxh3 * xh3, axis=-1, keepdims=True))).astype(BF16)
        dc3_ref[...] = dc3
        dc2 = _nt(dc3, wpw_ref[...])
        cv = conv_ref[...]
        mu = jnp.mean(cv, axis=-1, keepdims=True)
        xc = cv - mu
        rs = lax.rsqrt(jnp.mean(xc * xc, axis=-1, keepdims=True) + EPS)
        xn = xc * rs
        ln = xn * lng_ref[...] + lnb_ref[...]
        dln = dc2 * _dsilu(ln, _sigmoid(ln))
        sums_ref[2:3, :] += col(dln * xn)
        sums_ref[3:4, :] += col(dln)
        dxn = dln * lng_ref[...]
        dconv = rs * (dxn - jnp.mean(dxn, axis=-1, keepdims=True)
                      - xn * jnp.mean(dxn * xn, axis=-1, keepdims=True))
        dconv_ref[...] = dconv
        sums_ref[4:5, :] += col(dconv)

    half = lambda dt: jax.ShapeDtypeStruct((T, CHUNK), dt)
    tile = _rows(TM, CHUNK)
    vec = _whole((1, CHUNK))
    return _call(
        body, name=name, grid=(T // TM,),
        in_specs=[_rows(TM, CHUNK, 0), _rows(TM, CHUNK, 1), tile, _rows(TM, CHUNK, 0), _rows(TM, CHUNK, 3),
                  tile, tile, _whole((ATTN_DIM, ATTN_DIM)), vec, vec, vec, vec, _whole((CONV_DIM, CONV_DIM))],
        out_specs=[tile, tile, tile, tile, tile, _whole((8, CHUNK))],
        out_shape=[half(BF16), half(BF16), half(BF16), half(BF16), half(F32),
                   jax.ShapeDtypeStruct((8, CHUNK), F32)],
        args=(dycat, dycat, o, ug, ug, c3, conv, head_mean, g_attn, g_conv, ln_g, ln_b, w_pw,), ride=ride)


def _conv_bwd(dconv, ug, dw_w, name, ride=None):
    T = dconv.shape[0]
    per = TM // HALO
    last = T // HALO - 1
    n_tiles = T // TM

    def body(d_ref, dn_ref, cv_ref, cg_ref, cvh_ref, cgh_ref, w_ref, dcv_ref, dcg_ref, dw_ref,
             dpad_ref, cpad_ref, dsh_ref, csh_ref, dw_acc):
        i = pl.program_id(0)

        @pl.when(i == 0)
        def _():
            dw_acc[...] = jnp.zeros_like(dw_acc)

        tail = jnp.zeros((SUBLANES, CONV_DIM), F32)
        dpad_ref[0:TM, :] = d_ref[...]
        dpad_ref[TM:TM + HALO, :] = jnp.where(i == n_tiles - 1, 0.0, dn_ref[...])
        dpad_ref[TM + HALO:, :] = tail
        halo = cvh_ref[...] * _sigmoid(cgh_ref[...])
        cpad_ref[0:HALO, :] = jnp.where(i == 0, 0.0, halo)
        cpad_ref[HALO:HALO + TM, :] = cv_ref[...] * _sigmoid(cg_ref[...])
        cpad_ref[HALO + TM:, :] = tail
        _shifted_copies(dpad_ref, dsh_ref)
        _shifted_copies(cpad_ref, csh_ref)
        taps = [w_ref[t:t + 1, :] for t in range(CONV_WIDTH)]

        def rows(j, _):
            r = pl.multiple_of(j * CONV_ROWS, CONV_ROWS)
            d = d_ref[pl.ds(r, CONV_ROWS), :]
            dc = jnp.zeros((CONV_ROWS, CONV_DIM), F32)
            for t in range(CONV_WIDTH):
                b, a = _shift_of(CONV_WIDTH - 1 - t)
                dc = dc + taps[t] * dsh_ref[b, pl.ds(r + a, CONV_ROWS), :]
                b, a = _shift_of(HALO - (CONV_WIDTH - 1) + t)
                prod = d * csh_ref[b, pl.ds(r + a, CONV_ROWS), :]
                dw_acc[t] += jnp.sum(prod.reshape(CONV_ROWS // SUBLANES, SUBLANES, CONV_DIM), axis=0)
            cv = cv_ref[pl.ds(r, CONV_ROWS), :]
            sg = _sigmoid(cg_ref[pl.ds(r, CONV_ROWS), :])
            dcv_ref[pl.ds(r, CONV_ROWS), :] = (dc * sg).astype(BF16)
            dcg_ref[pl.ds(r, CONV_ROWS), :] = (dc * cv * sg * (1.0 - sg)).astype(BF16)
            return 0

        lax.fori_loop(0, TM // CONV_ROWS, rows, 0)

        @pl.when(i == n_tiles - 1)
        def _():
            dw_ref[...] = jnp.zeros_like(dw_ref)
            for t in range(CONV_WIDTH):
                dw_ref[t:t + 1, :] = jnp.sum(dw_acc[t], axis=0, keepdims=True)

    prev = lambda col: pl.BlockSpec((HALO, CHUNK), lambda i: (jnp.maximum(i * per - 1, 0), col))
    nxt = pl.BlockSpec((HALO, CONV_DIM), lambda i: (jnp.minimum((i + 1) * per, last), 0))
    half = jax.ShapeDtypeStruct((T, CHUNK), BF16)
    return _call(
        body, name=name, grid=(T // TM,),
        in_specs=[_rows(TM, CONV_DIM), nxt, _rows(TM, CHUNK, 1), _rows(TM, CHUNK, 2), prev(1), prev(2),
                  _whole((CONV_WIDTH, CONV_DIM))],
        out_specs=[_rows(TM, CHUNK), _rows(TM, CHUNK), _whole((HALO, CONV_DIM))],
        out_shape=[half, half, jax.ShapeDtypeStruct((HALO, CONV_DIM), F32)],
        scratch_shapes=[pltpu.VMEM((TM + HALO + SUBLANES, CONV_DIM), F32),
                        pltpu.VMEM((TM + HALO + SUBLANES, CONV_DIM), F32),
                        pltpu.VMEM((SUBLANES, TM + HALO, CONV_DIM), F32),
                        pltpu.VMEM((SUBLANES, TM + HALO, CONV_DIM), F32),
                        pltpu.VMEM((HALO, SUBLANES, CONV_DIM), F32)],
        args=(dconv, dconv, ug, ug, ug, ug, dw_w,), ride=ride)


def _attn_bwd(qs, k, v, do, cs, tri, tri_t, name, ride=None):
    T = qs.shape[0]
    nq = T // BLK
    width = LANES * ATT_COLS
    chains = [(c, half) for c in range(ATT_COLS) for half in range(2)]

    def body(q_ref, k_ref, v_ref, do_ref, cs_ref, m_ref, mt_ref, dq_ref, dk_ref, dv_ref, dk_acc, dv_acc):
        qi = pl.program_id(1)

        @pl.when(qi == 0)
        def _():
            dk_acc[...] = jnp.zeros_like(dk_acc)
            dv_acc[...] = jnp.zeros_like(dv_acc)

        lane = lax.broadcasted_iota(jnp.int32, (BLK, LANES), 1)
        first = lane < HEAD_DIM
        causal = (lax.broadcasted_iota(jnp.int32, (BLK, BLK), 1)
                  < lax.broadcasted_iota(jnp.int32, (BLK, BLK), 0))
        tri_m = m_ref[...]
        tri_mt = mt_ref[...]

        def halves(x):
            zero = jnp.zeros_like(x)
            return jnp.where(first, x, zero), jnp.where(first, zero, x)

        qh, doh, cs = {}, {}, []
        for c in range(ATT_COLS):
            qh[c, 0], qh[c, 1] = halves(q_ref[:, c * LANES:(c + 1) * LANES])
            doh[c, 0], doh[c, 1] = halves(do_ref[:, c * LANES:(c + 1) * LANES])
            cs.append(cs_ref[:, c * LANES:(c + 1) * LANES])

        def step(kb, state, masked):
            prefixes, dq_accs = state
            start = pl.multiple_of(kb * BLK, BLK)
            kblk = [k_ref[pl.ds(start, BLK), c * LANES:(c + 1) * LANES] for c in range(ATT_COLS)]
            vblk = [v_ref[pl.ds(start, BLK), c * LANES:(c + 1) * LANES] for c in range(ATT_COLS)]
            prefixes, dq_accs = list(prefixes), list(dq_accs)
            for g0 in range(0, len(chains), CHAIN_GROUP):
                ids = range(g0, g0 + CHAIN_GROUP)
                grp = [chains[n] for n in ids]
                z = [_nt(qh[ch], kblk[ch[0]]) for ch in grp]
                da = [_nt(doh[ch], vblk[ch[0]]) for ch in grp]
                parts = [_softplus_parts(zi) for zi in z]
                sp = [pt[1] for pt in parts]
                if masked:
                    sp = [jnp.where(causal, s, 0.0) for s in sp]
                incl = [_dot_hilo(s, tri_m) for s in sp]
                carries = [jnp.sum(jnp.where(lane == kb + HEAD_DIM * half, cs[c], 0.0), axis=1, keepdims=True)
                           for c, half in grp]
                a = [jnp.exp(zi - ii - ci) for zi, ii, ci in zip(z, incl, carries)]
                if masked:
                    a = [jnp.where(causal, ai, 0.0) for ai in a]
                w = [ai * di for ai, di in zip(a, da)]
                pinc = [_nn(wi.astype(BF16), tri_mt) for wi in w]
                dz = [wi - pt[0] * (pi + prefixes[n]) for n, wi, pt, pi in zip(ids, w, parts, pinc)]
                if masked:
                    dz = [jnp.where(causal, d, 0.0) for d in dz]
                for j in range(0, CHAIN_GROUP, 2):
                    c = grp[j][0]
                    k0, k1 = halves(kblk[c])
                    dz0, dz1 = dz[j].astype(BF16), dz[j + 1].astype(BF16)
                    a0, a1 = a[j].astype(BF16), a[j + 1].astype(BF16)
                    dq_accs[c] = dq_accs[c] + _nn(dz0, k0) + _nn(dz1, k1)
                    dk_acc[pl.ds(start, BLK), c * LANES:(c + 1) * LANES] += _tn(dz0, qh[c, 0]) + _tn(dz1, qh[c, 1])
                    dv_acc[pl.ds(start, BLK), c * LANES:(c + 1) * LANES] += _tn(a0, doh[c, 0]) + _tn(a1, doh[c, 1])
                for n, pi in zip(ids, pinc):
                    prefixes[n] = prefixes[n] + pi[:, BLK - 1:BLK]
            return tuple(prefixes), tuple(dq_accs)

        state = (tuple(jnp.zeros((BLK, 1), F32) for _ in chains),
                 tuple(jnp.zeros((BLK, LANES), F32) for _ in range(ATT_COLS)))
        first_block = jnp.max(jnp.where(lane == FIRST_BLOCK_LANE, cs[0], 0.0)).astype(jnp.int32)
        state = lax.fori_loop(first_block, qi, lambda kb, st: step(kb, st, False), state)
        state = step(qi, state, True)
        for c in range(ATT_COLS):
            dq_ref[:, c * LANES:(c + 1) * LANES] = (state[1][c] * (HEAD_DIM ** -0.5)).astype(BF16)

        @pl.when(qi == nq - 1)
        def _():
            dk_ref[...] = dk_acc[...].astype(BF16)
            dv_ref[...] = dv_acc[...].astype(BF16)

    blk = pl.BlockSpec((BLK, width), lambda j, i: (i, j))
    col = pl.BlockSpec((T, width), lambda j, i: (0, j))
    out = jax.ShapeDtypeStruct((T, ATTN_DIM), BF16)
    return _call(
        body, name=name, grid=(ATTN_DIM // width, nq),
        in_specs=[blk, col, col, blk, blk, _whole((BLK, BLK)), _whole((BLK, BLK))],
        out_specs=[blk, col, col], out_shape=[out, out, out],
        scratch_shapes=[pltpu.VMEM((T, width), F32), pltpu.VMEM((T, width), F32)],
        args=(qs, k, v, do, cs, tri, tri_t,), ride=ride)


def _inproj_bwd(du, w_in_t, h, dh1, gain, name, ride=None):
    T = h.shape[0]

    def body(*refs):
        du_refs = refs[:N_CHUNK]
        w_ref, h_ref, dh1_ref, g_ref, dh_ref, gsum_ref = refs[N_CHUNK:]

        @pl.when(pl.program_id(0) == 0)
        def _():
            gsum_ref[...] = jnp.zeros_like(gsum_ref)

        dhn = jnp.zeros((TM, D_MODEL), F32)
        for j in range(N_CHUNK):
            dhn = dhn + _nn(du_refs[j][...], w_ref[j * CHUNK:(j + 1) * CHUNK, :])
        hv = h_ref[...]
        r = lax.rsqrt(jnp.mean(hv * hv, axis=-1, keepdims=True) + EPS)
        xh = hv * r
        gsum_ref[...] += jnp.sum(dhn * xh, axis=0, keepdims=True)
        dxh = dhn * g_ref[...]
        dh_ref[...] = dh1_ref[...] + r * (dxh - xh * jnp.mean(dxh * xh, axis=-1, keepdims=True))

    full = _rows(TM, D_MODEL)
    return _call(
        body, name=name, grid=(T // TM,),
        in_specs=[_rows(TM, CHUNK)] * N_CHUNK + [_whole((N_CHUNK * CHUNK, D_MODEL)), full, full,
                                                 _whole((1, D_MODEL))],
        out_specs=[full, _whole((1, D_MODEL))],
        out_shape=[jax.ShapeDtypeStruct((T, D_MODEL), F32), jax.ShapeDtypeStruct((1, D_MODEL), F32)],
        args=(*du, w_in_t, h, dh1, gain), ride=ride)


def _weight_grad(lhs_list, rhs, name, tk=CHUNK, ride=None):
    T, n_rhs = rhs.shape
    n = len(lhs_list)
    ka = lhs_list[0].shape[1]
    per = ka // tk

    def body(*refs):
        a_refs, b_ref, out_ref = refs[:n], refs[n], refs[n + 1]
        step = pl.program_id(0)
        for j in range(n):
            for s in range(per):
                @pl.when(step == j * per + s)
                def _(j=j, s=s):
                    out_ref[...] = _tn(a_refs[j][:, s * tk:(s + 1) * tk], b_ref[...]).astype(BF16)

    (grad,), landed = _call(
        body, name=name, grid=(n * per,),
        in_specs=[_whole((T, ka))] * n + [_whole((T, n_rhs))],
        out_specs=[pl.BlockSpec((tk, n_rhs), lambda i: (i, 0))],
        out_shape=[jax.ShapeDtypeStruct((n * ka, n_rhs), BF16)],
        args=(*lhs_list, rhs), ride=ride)
    return grad, landed


def _adamw_update(w, g, m, v):
    nm = ADAM_B1 * m + (1.0 - ADAM_B1) * g
    nv = ADAM_B2 * v + (1.0 - ADAM_B2) * (g * g)
    m_hat = nm / (1.0 - ADAM_B1 ** ADAM_STEP)
    v_hat = nv / (1.0 - ADAM_B2 ** ADAM_STEP)
    return -ADAM_LR * (m_hat / (jnp.sqrt(v_hat) + ADAM_EPS) + ADAM_WD * w), nm, nv


def _sum_adamw(slots, w, m, v, name):
    depth, R, C = w.shape
    tr = min(R, ADAMW_ROWS)

    def body(*refs):
        slot_refs, (w_ref, m_ref, v_ref, g_ref, d_ref, nm_ref, nv_ref) = refs[:depth], refs[depth:]
        for layer in range(depth):
            @pl.when(pl.program_id(0) == layer)
            def _(src=slot_refs[layer]):
                g = src[0].astype(F32)
                for s in range(1, src.shape[0]):
                    g = g + src[s].astype(F32)
                g_ref[0] = g
                d_ref[0], nm_ref[0], nv_ref[0] = _adamw_update(w_ref[0], g, m_ref[0], v_ref[0])

    slot_spec = lambda layer: pl.BlockSpec((slots[layer].shape[0], tr, C),
                                           lambda l, i: (0, jnp.where(l == layer, i, 0), 0))
    spec = pl.BlockSpec((1, tr, C), lambda l, i: (l, i, 0))
    out = jax.ShapeDtypeStruct((depth, R, C), F32)
    return pl.pallas_call(
        body, name=name, grid=(depth, R // tr),
        in_specs=[slot_spec(layer) for layer in range(depth)] + [spec] * 3,
        out_specs=[spec] * 4, out_shape=[out] * 4,
        compiler_params=_params(2),
    )(*slots, w, m, v)


def _adamw(w, g, m, v, name):
    R, C = w.shape
    tr = R
    for cand in (512, 256, 128, 64):
        if R % cand == 0 and R > cand:
            tr = cand
            break

    def body(w_ref, g_ref, m_ref, v_ref, d_ref, nm_ref, nv_ref):
        d_ref[...], nm_ref[...], nv_ref[...] = _adamw_update(w_ref[...], g_ref[...], m_ref[...], v_ref[...])

    spec = pl.BlockSpec((tr, C), lambda i: (i, 0))
    out = jax.ShapeDtypeStruct((R, C), F32)
    return pl.pallas_call(
        body, name=name, grid=(R // tr,),
        in_specs=[spec] * 4, out_specs=[spec] * 3, out_shape=[out, out, out],
        compiler_params=_params(1),
    )(w, g, m, v)


def _pack_small(values, scalar=None):
    pad = lambda a: jnp.pad(a, ((0, 0), (0, D_MODEL - a.shape[1])))
    last = jnp.zeros((1, D_MODEL), F32) if scalar is None else pad(scalar.reshape(1, 1))
    return jnp.concatenate([pad(values[name].reshape(rows, cols)) for name, _, rows, cols in SMALL_LAYOUT] + [last],
                           axis=0)


def _small_update(all_packs, state, name):
    n = len(SMALL_LAYOUT)

    def body(packs_ref, *refs):
        ins, outs = refs[:3 * n], refs[3 * n:]
        total = packs_ref[0]
        for s in range(1, N_DEV):
            total = total + packs_ref[s]
        for j, (_, at, rows, cols) in enumerate(SMALL_LAYOUT):
            g = total[at:at + rows, :cols]
            w_ref, m_ref, v_ref = ins[3 * j:3 * j + 3]
            outs[4 * j][...] = g
            outs[4 * j + 1][...], outs[4 * j + 2][...], outs[4 * j + 3][...] = _adamw_update(
                w_ref[...], g, m_ref[...], v_ref[...])
        outs[-2][...] = total[LOSS_ROW:LOSS_ROW + 1, :LANES]
        outs[-1][...] = total[SMALL_ROWS:, :]

    shapes = [jax.ShapeDtypeStruct((rows, cols), F32) for _, _, rows, cols in SMALL_LAYOUT for _ in range(4)]
    shapes += [jax.ShapeDtypeStruct((1, LANES), F32), jax.ShapeDtypeStruct((PACK_ROWS - SMALL_ROWS, D_MODEL), F32)]
    operands = [a for item in SMALL_LAYOUT for a in state[item[0]]]
    res = pl.pallas_call(body, name=name, out_shape=shapes, compiler_params=_params())(all_packs, *operands)
    per_name = {item[0]: tuple(res[4 * j:4 * j + 4]) for j, item in enumerate(SMALL_LAYOUT)}
    return per_name, res[-2][0, 0], res[-1]


def kernel(x, p, norm_g, w_in, attn_out_g, dw_w, dw_b, conv_ln_g, conv_ln_b, w_pw, conv_out_g, w_out, ple_norm_g, w_ple_gate, w_ple, final_g, loss_target, m_norm_g, m_w_in, m_attn_out_g, m_dw_w, m_dw_b, m_conv_ln_g, m_conv_ln_b, m_w_pw, m_conv_out_g, m_w_out, m_ple_norm_g, m_w_ple_gate, m_w_ple, m_final_g, v_norm_g, v_w_in, v_attn_out_g, v_dw_w, v_dw_b, v_conv_ln_g, v_conv_ln_b, v_w_pw, v_conv_out_g, v_w_out, v_ple_norm_g, v_w_ple_gate, v_w_ple, v_final_g):
    depth = w_in.shape[0]
    T = x.shape[1]
    given = dict(
        norm_g=norm_g, ple_norm_g=ple_norm_g, final_g=final_g, dw_b=dw_b, conv_ln_g=conv_ln_g, conv_ln_b=conv_ln_b,
        conv_out_g=conv_out_g, attn_out_g=attn_out_g,
        m_norm_g=m_norm_g, m_ple_norm_g=m_ple_norm_g, m_final_g=m_final_g, m_dw_b=m_dw_b, m_conv_ln_g=m_conv_ln_g,
        m_conv_ln_b=m_conv_ln_b, m_conv_out_g=m_conv_out_g, m_attn_out_g=m_attn_out_g,
        v_norm_g=v_norm_g, v_ple_norm_g=v_ple_norm_g, v_final_g=v_final_g, v_dw_b=v_dw_b, v_conv_ln_g=v_conv_ln_g,
        v_conv_ln_b=v_conv_ln_b, v_conv_out_g=v_conv_out_g, v_attn_out_g=v_attn_out_g)
    my_idx = 4 * lax.axis_index("x") + 2 * lax.axis_index("y") + lax.axis_index("c")

    ids = jnp.arange(BLK)
    tri = (ids[:, None] >= ids[None, :]).astype(BF16)
    tri_t = (ids[:, None] <= ids[None, :]).astype(BF16)
    hid = jnp.arange(ATTN_DIM) // HEAD_DIM
    head_mean = ((hid[:, None] == hid[None, :]).astype(F32) / HEAD_DIM).astype(BF16)

    w_names = ("w_in_t", "w_pw", "w_out", "w_gate", "w_ple")
    w_axes = dict(zip(w_names, (0, 0, 0, 0, 1)))
    shards = [dict(zip(w_names, (w_in[l].T.astype(BF16), w_pw[l].astype(BF16), w_out[l].astype(BF16),
                                 w_ple_gate[l].astype(BF16), w_ple[l].astype(BF16)))) for l in range(depth)]
    first = _all_gather([shards[0]["w_in_t"]] + [dw_w[l].T for l in range(depth)], [0] * (1 + depth),
                        "gather_weights_0")
    layers = []
    for l in range(depth):
        layers.append(dict(
            dw_w=first[1 + l].T,
            g_norm=norm_g[l][None], g_attn=jnp.tile(attn_out_g[l], N_HEADS)[None], dw_b=dw_b[l][None],
            ln_g=conv_ln_g[l][None], ln_b=conv_ln_b[l][None], g_conv=conv_out_g[l][None],
            g_ple=ple_norm_g[l][None], p=p[l, 0]))
    layers[0]["w_in_t"] = first[0]

    def rest_of(l):
        return [_Ride.gather2(shards[l][n], w_axes[n]) for n in w_names[1:]]

    h = x[0]
    saved = []
    for l, w in enumerate(layers):
        (qs, k, v, ug, hn), landed = _prenorm_inproj(h, w["g_norm"], w["w_in_t"], f"inproj_{l}",
                                                     _Ride(rest_of(l) if l == 0 else []))
        w.update(zip(w_names[1:], landed))
        ahead = [_Ride.gather2(shards[l + 1]["w_in_t"], 0)] if l + 1 < depth else []
        (o, cs), landed = _attn_fwd(qs, k, v, tri, f"attn_fwd_{l}", _Ride(ahead + (rest_of(l) if l > 0 else [])))
        if ahead:
            layers[l + 1]["w_in_t"] = landed.pop(0)
        w.update(zip(w_names[1:], landed))
        (conv, c2), _ = _conv_fwd(ug, w["dw_w"], w["dw_b"], w["ln_g"], w["ln_b"], f"conv_fwd_{l}")
        (h2, h1, ycat, hn2, gate, e, c3), _ = _mix_out_ple(
            o, ug, c2, h, w["p"], head_mean, w["g_attn"], w["g_conv"], w["g_ple"],
            w["w_pw"], w["w_out"], w["w_gate"], w["w_ple"], f"mix_{l}")
        saved.append(dict(h=h, qs=qs, k=k, v=v, ug=ug, hn=hn, o=o, cs=cs, conv=conv, c2=c2, h1=h1,
                          ycat=ycat, hn2=hn2, gate=gate, e=e, c3=c3))
        h = h2
    dh, g_final, loss_part = _final_loss(h, loss_target[0], final_g[None], "final_loss")

    small = {}
    dww_parts = [None] * depth
    slots = [dict() for _ in range(depth)]
    g_w_in = None
    for l in reversed(range(depth)):
        w, s = layers[l], saved[l]
        above = [None] if g_w_in is not None else []

        def part(i, above=above, g=g_w_in):
            return [_Ride.scatter(g, 0, above[0], *W_IN_GRAD_PARTS[i])] if above else []

        def scattered(grads, names):
            return [_Ride.scatter(grads[n], w_axes[n]) for n in names]

        (dh1, dh1b, dzg, de, dycat, g_ple_sum), landed = _ple_out_bwd(
            dh, s["gate"], s["e"], s["h1"], w["g_ple"], w["w_gate"], w["w_out"], f"ple_bwd_{l}", _Ride(part(0)))
        above[:1] = landed
        (do, dga, dgc, dc3, dconv, sums), landed = _branch_bwd(
            dycat, s["o"], s["ug"], s["c3"], s["conv"], head_mean, w["g_attn"], w["g_conv"],
            w["ln_g"], w["ln_b"], w["w_pw"], f"branch_bwd_{l}", _Ride(part(1)))
        above[:1] = landed
        grads = dict(
            w_pw=_weight_grad([s["c2"]], dc3, f"grad_w_pw_{l}")[0],
            w_out=_weight_grad([s["ycat"]], dh1b, f"grad_w_out_{l}")[0],
            w_gate=_weight_grad([s["hn2"]], dzg, f"grad_w_gate_{l}")[0],
            w_ple=_weight_grad([w["p"].astype(BF16)], de, f"grad_w_ple_{l}", tk=PLE_DIM)[0])
        (dcv, dcg, dww), landed = _conv_bwd(dconv, s["ug"], w["dw_w"], f"conv_bwd_{l}", _Ride(part(2)))
        above[:1] = landed
        (dq, dk, dv), landed = _attn_bwd(s["qs"], s["k"], s["v"], do, s["cs"], tri, tri_t, f"attn_bwd_{l}",
                                         _Ride(scattered(grads, w_names[1:])))
        slots[l].update(zip(w_names[1:], landed))
        du = [dq, dk, dv, dga, dcv, dcg, dgc]
        g_w_in_here, landed = _weight_grad(du, s["hn"], f"grad_w_in_{l}", ride=_Ride(part(3)))
        if above:
            slots[l + 1]["w_in_t"] = landed[0]
        tail = [_Ride.scatter_chips(_pair_reduce(g_w_in_here, f"pair_reduce_w_in_{l}"))] if l == 0 else []
        (dh, g_norm_sum), landed = _inproj_bwd(du, w["w_in_t"], s["h"], dh1, w["g_norm"], f"inproj_bwd_{l}",
                                               _Ride(tail))
        slots[l].update(zip(("w_in_t",), landed))
        g_w_in = g_w_in_here
        small[l] = dict(norm_g=g_norm_sum, ple_norm_g=g_ple_sum, attn_out_g=sums[0].reshape(N_HEADS, HEAD_DIM).sum(0),
                        conv_out_g=sums[1], conv_ln_g=sums[2], conv_ln_b=sums[3], dw_b=sums[4])
        dww_parts[l] = dww[:CONV_WIDTH]
    slots = [[sl[n] for n in w_names] for sl in slots]
    grad_x = dh[None]

    sums_of = {name: jnp.stack([small[l][name].reshape(-1) for l in range(depth)]) for name in small[0]}
    sums_of["final_g"] = g_final
    pack = jnp.concatenate([_pack_small(sums_of, scalar=loss_part[0, 0]), jnp.concatenate(dww_parts, axis=1),
                            jnp.zeros((PACK_ROWS - SMALL_ROWS - CONV_WIDTH, D_MODEL), F32)], axis=0)
    (all_packs,) = _all_gather([pack], [0], "gather_small_grads")
    state = {name: [given[pre + name].reshape(rows, cols) for pre in ("", "m_", "v_")]
             for name, _, rows, cols in SMALL_LAYOUT}
    updated, loss, dww_sum = _small_update(all_packs.reshape(N_DEV, PACK_ROWS, D_MODEL), state, "update_small")
    res = {kind: {name: val[k].reshape(given[name].shape) for name, val in updated.items()}
           for k, kind in enumerate("gdmv")}
    dww_full = dww_sum[:CONV_WIDTH].reshape(CONV_WIDTH, depth, CONV_DIM).transpose(1, 0, 2)
    g_dw_w = lax.dynamic_slice_in_dim(dww_full, my_idx * (CONV_DIM // N_DEV), CONV_DIM // N_DEV, axis=2)

    swap = lambda a: a.transpose(0, 2, 1)
    state = {"w_in": (w_in, m_w_in, v_w_in), "w_pw": (w_pw, m_w_pw, v_w_pw), "w_out": (w_out, m_w_out, v_w_out),
             "w_ple_gate": (w_ple_gate, m_w_ple_gate, v_w_ple_gate), "w_ple": (w_ple, m_w_ple, v_w_ple)}
    for at, name in enumerate(state):
        wv, mv, vv = [swap(a) for a in state[name]] if name == "w_in" else state[name]
        out = _sum_adamw([slots[l][at] for l in range(depth)], wv, mv, vv, f"adamw_{name}")
        out = [swap(a) for a in out] if name == "w_in" else out
        res["g"][name], res["d"][name], res["m"][name], res["v"][name] = out
    flat = lambda a: a.reshape(-1, a.shape[-1])
    res["g"]["dw_w"] = g_dw_w
    res["d"]["dw_w"], res["m"]["dw_w"], res["v"]["dw_w"] = [
        a.reshape(dw_w.shape) for a in _adamw(flat(dw_w), flat(g_dw_w), flat(m_dw_w), flat(v_dw_w), "adamw_dw_w")]

    order = ["norm_g", "w_in", "attn_out_g", "dw_w", "dw_b", "conv_ln_g", "conv_ln_b", "w_pw", "conv_out_g",
             "w_out", "ple_norm_g", "w_ple_gate", "w_ple", "final_g"]
    return (loss, grad_x, *[res["g"][n] for n in order], *[res["d"][n] for n in order],
            *[res["m"][n] for n in order], *[res["v"][n] for n in order])
```

```python
import functools

import jax
import jax.numpy as jnp
from jax import lax
from jax.experimental import pallas as pl
from jax.experimental.pallas import tpu as pltpu

F32 = jnp.float32
BF16 = jnp.bfloat16
MESH = pl.DeviceIdType.MESH

N_DEV = 8
D_MODEL = 1024
ATTN_DIM = 512
CONV_DIM = 512
HEAD_DIM = 64
N_HEADS = 8
CONV_WIDTH = 31
PLE_DIM = 256
CHUNK = 512
N_CHUNK = 7
EPS = 1e-6
ADAM_LR = 0.001
ADAM_B1 = 0.9
ADAM_B2 = 0.999
ADAM_EPS = 1e-08
ADAM_WD = 0.01
ADAM_STEP = 10

LANES = 128
BLK = 256
ATT_COLS = 4
CHAIN_GROUP = 4
SOFTPLUS_LINEAR_AT = 20.0
DEAD_AT = 110.0
FIRST_BLOCK_LANE = HEAD_DIM - 1
TM = 512
HALO = 32
SUBLANES = 8
CONV_ROWS = 32
ADAMW_ROWS = 64
VMEM_LIMIT = 56 * 1024 * 1024
SMALL_ROWS = 16
SMALL_LAYOUT = (("norm_g", 0, 2, D_MODEL), ("ple_norm_g", 2, 2, D_MODEL), ("final_g", 4, 1, D_MODEL),
                ("dw_b", 5, 2, CONV_DIM), ("conv_ln_g", 7, 2, CONV_DIM), ("conv_ln_b", 9, 2, CONV_DIM),
                ("conv_out_g", 11, 2, CONV_DIM), ("attn_out_g", 13, 2, HEAD_DIM))
LOSS_ROW = 15
W_IN_ROWS_ON_ATTN = 288
W_IN_GRAD_PARTS = ((0, 96), (96, 80), (176, 144), (320, 128))
PACK_ROWS = 48


def _nn(a, b):
    return lax.dot_general(a, b, (((1,), (0,)), ((), ())), preferred_element_type=F32)


def _nt(a, b):
    return lax.dot_general(a, b, (((1,), (1,)), ((), ())), preferred_element_type=F32)


def _tn(a, b):
    return lax.dot_general(a, b, (((0,), (0,)), ((), ())), preferred_element_type=F32)


def _split(x):
    hi = x.astype(BF16)
    lo = (x - hi.astype(F32)).astype(BF16)
    return hi, lo


def _dot_hilo(x, m):
    hi, lo = _split(x)
    return _nn(hi, m) + _nn(lo, m)


def _sigmoid(x):
    return jax.nn.sigmoid(x)


def _dsilu(x, s):
    return s * (1.0 + x * (1.0 - s))


def _params(n_grid=0, vmem=VMEM_LIMIT):
    sem = ("arbitrary",) * n_grid if n_grid else None
    return pltpu.CompilerParams(dimension_semantics=sem, vmem_limit_bytes=vmem)


def _rows(tm, cols, col=0):
    return pl.BlockSpec((tm, cols), lambda i: (i, col))


def _whole(shape):
    zeros = (0,) * len(shape)
    return pl.BlockSpec(shape, lambda *_: zeros)


def _my_position():
    return lax.axis_index("x"), lax.axis_index("y"), lax.axis_index("c")


def _block(ref, axis, idx, size):
    start = pl.multiple_of(idx * size, size)
    if axis == 0:
        return ref.at[pl.ds(start, size), :]
    return ref.at[:, pl.ds(start, size)]


def _all_gather(shards, axes, name):
    n = len(shards)
    sizes = [s.shape[a] for s, a in zip(shards, axes)]

    def full_shape(s, a):
        shape = list(s.shape)
        shape[a] *= N_DEV
        return jax.ShapeDtypeStruct(tuple(shape), s.dtype)

    def body(*refs):
        ins, outs = refs[:n], refs[n:2 * n]
        send_sems, recv_sems, local_sems = refs[2 * n:]
        x, y, c = _my_position()
        me, sibling = (x, y, c), (x, y, 1 - c)
        chips = [(1 - x, y), (x, 1 - y), (1 - x, 1 - y)]

        def place(i, dev):
            return _block(outs[i], axes[i], 4 * dev[0] + 2 * dev[1] + dev[2], sizes[i])

        def copy(k, i, dev, to, src=None):
            return pltpu.make_async_remote_copy(
                src_ref=place(i, dev) if src is None else src, dst_ref=place(i, dev),
                send_sem=send_sems.at[k, i], recv_sem=recv_sems.at[k, i],
                device_id=to, device_id_type=MESH)

        mine = [pltpu.make_async_copy(ins[i], place(i, me), local_sems.at[i]) for i in range(n)]
        for cp in mine:
            cp.start()
        first = [copy(0, i, me, sibling, src=ins[i]) for i in range(n)]
        for j, chip in enumerate(chips):
            first += [copy(1 + j, i, me, (*chip, c), src=ins[i]) for i in range(n)]
        for cp in first:
            cp.start()
        passed = []
        for j, chip in enumerate(chips):
            for i in range(n):
                copy(1 + j, i, (*chip, c), me).wait_recv()
            hop = [copy(4 + j, i, (*chip, c), sibling) for i in range(n)]
            for cp in hop:
                cp.start()
            passed += hop
        for i in range(n):
            copy(0, i, sibling, me).wait_recv()
        for j, chip in enumerate(chips):
            for i in range(n):
                copy(4 + j, i, (*chip, 1 - c), me).wait_recv()
        for cp in first + passed:
            cp.wait_send()
        for cp in mine:
            cp.wait()

    any_spec = pl.BlockSpec(memory_space=pl.ANY)
    return pl.pallas_call(
        body, name=name,
        out_shape=[full_shape(s, a) for s, a in zip(shards, axes)],
        in_specs=[any_spec] * n, out_specs=[any_spec] * n,
        scratch_shapes=[pltpu.SemaphoreType.DMA((7, n)), pltpu.SemaphoreType.DMA((7, n)),
                        pltpu.SemaphoreType.DMA((n,))],
    )(*shards)


def _pair_reduce(g, name):
    n_chips = N_DEV // 2
    R, C = g.shape[0] // N_DEV, g.shape[1]

    def body(g_ref, out_ref, mine_ref, theirs_ref, send_sems, recv_sems, local_sems):
        x, y, c = _my_position()
        block = lambda d: g_ref.at[pl.ds(pl.multiple_of(d * R, 16), R), :]
        sends = [pltpu.make_async_remote_copy(
            src_ref=block(2 * j + 1 - c), dst_ref=theirs_ref.at[j], send_sem=send_sems.at[j],
            recv_sem=recv_sems.at[j], device_id=(x, y, 1 - c), device_id_type=MESH) for j in range(n_chips)]
        own = [pltpu.make_async_copy(block(2 * j + c), mine_ref.at[j], local_sems.at[j]) for j in range(n_chips)]
        for cp in sends + own:
            cp.start()
        for j in range(n_chips):
            own[j].wait()
            sends[j].wait_recv()
            out_ref[j] = (mine_ref[j].astype(F32) + theirs_ref[j].astype(F32)).astype(g.dtype)
        for cp in sends:
            cp.wait_send()

    half = pltpu.VMEM((n_chips, R, C), g.dtype)
    sems = pltpu.SemaphoreType.DMA((n_chips,))
    return pl.pallas_call(
        body, name=name, out_shape=jax.ShapeDtypeStruct((n_chips, R, C), g.dtype),
        in_specs=[pl.BlockSpec(memory_space=pl.ANY)], out_specs=pl.BlockSpec(memory_space=pltpu.VMEM),
        scratch_shapes=[half, half, sems, sems, sems], compiler_params=_params(),
    )(g)


class _Ride:
    def __init__(self, parts):
        self.parts = [p for p in parts if p is not None]

    @staticmethod
    def gather(src, axis, land=None, lo=0, n=None):
        return ("gather", src, land, axis, lo, src.shape[axis] if n is None else n)

    @staticmethod
    def gather2(src, axis, land=None, lo=0, n=None):
        return ("gather2", src, land, axis, lo, src.shape[axis] if n is None else n)

    @staticmethod
    def scatter(src, axis, land=None, lo=0, n=None):
        return ("scatter", src, land, axis, lo, src.shape[axis] // N_DEV if n is None else n)

    @staticmethod
    def scatter_chips(chip_sums):
        return ("scatter_chips", chip_sums, None, 0, 0, chip_sums.shape[1])

    def arrays(self):
        return [p[1] for p in self.parts] + [p[2] for p in self.parts if p[2] is not None]

    def out_shapes(self):
        out = []
        for kind, src, _, axis, _, _ in self.parts:
            shape = list(src.shape)
            if kind in ("gather", "gather2"):
                shape[axis] *= N_DEV
            elif kind == "scatter_chips":
                pass
            else:
                shape[axis] //= N_DEV
                shape = [N_DEV] + shape
            out.append(jax.ShapeDtypeStruct(tuple(shape), src.dtype))
        return out

    def aliases(self, n_in, n_out):
        m, out = len(self.parts), {}
        for j, p in enumerate(self.parts):
            if p[2] is not None:
                out[n_in + m + len(out)] = n_out + j
        return out

    def scratch(self):
        m = len(self.parts)
        return [pltpu.SemaphoreType.DMA((N_DEV - 1, m)), pltpu.SemaphoreType.DMA((N_DEV - 1, m)),
                pltpu.SemaphoreType.DMA((m,))]

    def _copies(self, src_refs, land_refs, sems):
        send_sems, recv_sems, local_sems = sems
        x, y, c = _my_position()
        my_idx = 4 * x + 2 * y + c
        own, sends, relays, lands = [], [], [], []
        for j, (kind, src, _, axis, lo, n) in enumerate(self.parts):
            if kind == "scatter_chips":
                for k in (0, 2, 4, 6):
                    px, py = (1 - x if k & 4 else x), (1 - y if k & 2 else y)
                    a, b = src_refs[j].at[2 * px + py], land_refs[j].at[2 * x + y]
                    if k == 0:
                        own.append(pltpu.make_async_copy(a, b, local_sems.at[j]))
                        continue
                    mk = lambda dst, a=a, k=k, j=j, to=(px, py, c): pltpu.make_async_remote_copy(
                        src_ref=a, dst_ref=dst, send_sem=send_sems.at[k - 1, j], recv_sem=recv_sems.at[k - 1, j],
                        device_id=to, device_id_type=MESH)
                    sends.append(mk(b))
                    lands.append(mk(land_refs[j].at[2 * px + py]))
                continue
            size = src.shape[axis] if kind in ("gather", "gather2") else src.shape[axis] // N_DEV
            align = 16 if axis == 0 else LANES

            def rows(ref, idx, lead=None, axis=axis, lo=lo, n=n, size=size, align=align):
                at = pl.ds(pl.multiple_of(idx * size + lo, align), n)
                where = (at, slice(None)) if axis == 0 else (slice(None), at)
                return ref.at[where] if lead is None else ref.at[(lead, *where)]

            def in_shard(ref):
                return rows(ref, 0)

            def in_slot(ref, s):
                return rows(ref, 0, lead=s)

            if kind == "gather2":
                chips = [(1 - x, y), (x, 1 - y), (1 - x, 1 - y)]
                place = lambda px, py, pc: rows(land_refs[j], 4 * px + 2 * py + pc)

                def copy(i, a, dst, to, j=j):
                    return pltpu.make_async_remote_copy(
                        src_ref=a, dst_ref=dst, send_sem=send_sems.at[i, j], recv_sem=recv_sems.at[i, j],
                        device_id=to, device_id_type=MESH)

                mine = in_shard(src_refs[j])
                own.append(pltpu.make_async_copy(mine, place(x, y, c), local_sems.at[j]))
                sends.append(copy(0, mine, place(x, y, c), (x, y, 1 - c)))
                lands.append(copy(0, mine, place(x, y, 1 - c), (x, y, 1 - c)))
                for i, (px, py) in enumerate(chips):
                    sends.append(copy(1 + i, mine, place(x, y, c), (px, py, c)))
                    relays.append((copy(1 + i, mine, place(px, py, c), (px, py, c)),
                                   copy(4 + i, place(px, py, c), place(px, py, c), (x, y, 1 - c))))
                    lands.append(copy(4 + i, mine, place(px, py, 1 - c), (x, y, 1 - c)))
                continue
            for k in range(N_DEV):
                px = 1 - x if k & 4 else x
                py = 1 - y if k & 2 else y
                pc = 1 - c if k & 1 else c
                peer_idx = 4 * px + 2 * py + pc
                if kind == "gather":
                    a, b, landed = in_shard(src_refs[j]), rows(land_refs[j], my_idx), rows(land_refs[j], peer_idx)
                else:
                    a, b, landed = rows(src_refs[j], peer_idx), in_slot(land_refs[j], my_idx), in_slot(land_refs[j], peer_idx)
                if k == 0:
                    own.append(pltpu.make_async_copy(a, b, local_sems.at[j]))
                    continue
                mk = lambda dst, a=a, k=k, j=j, to=(px, py, pc): pltpu.make_async_remote_copy(
                    src_ref=a, dst_ref=dst, send_sem=send_sems.at[k - 1, j], recv_sem=recv_sems.at[k - 1, j],
                    device_id=to, device_id_type=MESH)
                sends.append(mk(b))
                lands.append(mk(landed))
        return own, sends, relays, lands

    @property
    def relayed(self):
        return any(p[0] == "gather2" for p in self.parts)

    def start(self, src_refs, land_refs, sems):
        own, sends, _, _ = self._copies(src_refs, land_refs, sems)
        for cp in own + sends:
            cp.start()

    def relay(self, src_refs, land_refs, sems):
        for arrival, onward in self._copies(src_refs, land_refs, sems)[2]:
            arrival.wait_recv()
            onward.start()

    def wait(self, src_refs, land_refs, sems):
        own, sends, relays, lands = self._copies(src_refs, land_refs, sems)
        for cp in lands:
            cp.wait_recv()
        for cp in sends + [onward for _, onward in relays]:
            cp.wait_send()
        for cp in own:
            cp.wait()


def _call(body, *, name, grid, in_specs, out_specs, out_shape, args, scratch_shapes=(), ride=None):
    in_specs, out_specs, out_shape = list(in_specs), list(out_specs), list(out_shape)
    n_in, n_out, n_sc = len(in_specs), len(out_specs), len(scratch_shapes)
    if ride is None or not ride.parts:
        res = pl.pallas_call(body, name=name, grid=grid, in_specs=in_specs, out_specs=out_specs,
                             out_shape=out_shape, scratch_shapes=list(scratch_shapes),
                             compiler_params=_params(len(grid)))(*args)
        return list(res), []
    extra, m = ride.arrays(), len(ride.parts)

    def riding(*refs):
        a = n_in + len(extra)
        b = a + n_out
        srcs, lands, sems = refs[n_in:n_in + m], refs[b:b + m], refs[b + m + n_sc:]
        at = [pl.program_id(d) for d in range(len(grid))]

        @pl.when(functools.reduce(jnp.logical_and, [i == 0 for i in at]))
        def _():
            ride.start(srcs, lands, sems)

        if ride.relayed:
            step, n_steps = at[0], 1
            for i, g in zip(at[1:], grid[1:]):
                step = step * g + i
            for g in grid:
                n_steps *= g
            assert n_steps >= 2, "a two-level ride needs a grid step after the first"

            @pl.when(step == n_steps - 1)
            def _():
                ride.relay(srcs, lands, sems)

        body(*refs[:n_in], *refs[a:b], *refs[b + m:b + m + n_sc])

        @pl.when(functools.reduce(jnp.logical_and, [i == g - 1 for i, g in zip(at, grid)]))
        def _():
            ride.wait(srcs, lands, sems)

    hbm = pl.BlockSpec(memory_space=pl.ANY)
    res = pl.pallas_call(
        riding, name=name, grid=grid, in_specs=in_specs + [hbm] * len(extra), out_specs=out_specs + [hbm] * m,
        out_shape=out_shape + ride.out_shapes(), scratch_shapes=list(scratch_shapes) + ride.scratch(),
        input_output_aliases=ride.aliases(n_in, n_out), compiler_params=_params(len(grid)),
    )(*args, *extra)
    return list(res[:n_out]), list(res[n_out:])


def _prenorm_inproj(h, gain, w_in_t, name, ride=None):
    T = h.shape[0]

    def body(h_ref, g_ref, w_ref, q_ref, k_ref, v_ref, ug_ref, hn_ref):
        hv = h_ref[...]
        r = lax.rsqrt(jnp.mean(hv * hv, axis=-1, keepdims=True) + EPS)
        hn = (hv * r * g_ref[...]).astype(BF16)
        hn_ref[...] = hn
        for j in range(N_CHUNK):
            u = _nt(hn, w_ref[j * CHUNK:(j + 1) * CHUNK, :])
            if j == 0:
                q_ref[...] = (u * (HEAD_DIM ** -0.5)).astype(BF16)
            elif j == 1:
                k_ref[...] = u.astype(BF16)
            elif j == 2:
                v_ref[...] = u.astype(BF16)
            else:
                ug_ref[:, (j - 3) * CHUNK:(j - 2) * CHUNK] = u

    act = jax.ShapeDtypeStruct((T, CHUNK), BF16)
    return _call(
        body, name=name, grid=(T // TM,),
        in_specs=[_rows(TM, D_MODEL), _whole((1, D_MODEL)), _whole((N_CHUNK * CHUNK, D_MODEL))],
        out_specs=[_rows(TM, CHUNK)] * 3 + [_rows(TM, 4 * CHUNK), _rows(TM, D_MODEL)],
        out_shape=[act, act, act, jax.ShapeDtypeStruct((T, 4 * CHUNK), F32),
                   jax.ShapeDtypeStruct((T, D_MODEL), BF16)],
        args=(h, gain, w_in_t,), ride=ride)


def _softplus_parts(z):
    ez = jnp.exp(jnp.minimum(z, SOFTPLUS_LINEAR_AT))
    t = 1.0 + ez
    return ez * pl.reciprocal(t, approx=True), jnp.where(z > SOFTPLUS_LINEAR_AT, z, jnp.log(t))


def _attn_fwd(qs, k, v, tri, name, ride=None):
    T = qs.shape[0]
    assert T // BLK <= FIRST_BLOCK_LANE, "one lane per key block below the lane of the first block"
    width = LANES * ATT_COLS
    chains = [(c, half) for c in range(ATT_COLS) for half in range(2)]

    def body(q_ref, k_ref, v_ref, m_ref, o_ref, cs_ref):
        qi = pl.program_id(1)
        lane = lax.broadcasted_iota(jnp.int32, (BLK, LANES), 1)
        first = lane < HEAD_DIM
        causal = (lax.broadcasted_iota(jnp.int32, (BLK, BLK), 1)
                  < lax.broadcasted_iota(jnp.int32, (BLK, BLK), 0))
        tri_m = m_ref[...]
        qh = {}
        for c in range(ATT_COLS):
            q = q_ref[:, c * LANES:(c + 1) * LANES]
            zero = jnp.zeros_like(q)
            qh[c, 0], qh[c, 1] = jnp.where(first, q, zero), jnp.where(first, zero, q)

        def step(kb, state, masked):
            carries, accs, cvals = state
            start = pl.multiple_of(kb * BLK, BLK)
            kblk = [k_ref[pl.ds(start, BLK), c * LANES:(c + 1) * LANES] for c in range(ATT_COLS)]
            vblk = [v_ref[pl.ds(start, BLK), c * LANES:(c + 1) * LANES] for c in range(ATT_COLS)]
            carries, accs, cvals = list(carries), list(accs), list(cvals)
            for g0 in range(0, len(chains), CHAIN_GROUP):
                ids = range(g0, g0 + CHAIN_GROUP)
                z = [_nt(qh[chains[n]], kblk[chains[n][0]]) for n in ids]
                sp = [_softplus_parts(zi)[1] for zi in z]
                if masked:
                    sp = [jnp.where(causal, s, 0.0) for s in sp]
                incl = [_dot_hilo(s, tri_m) for s in sp]
                a = [jnp.exp(zi - ii - carries[n]) for n, zi, ii in zip(ids, z, incl)]
                if masked:
                    a = [jnp.where(causal, ai, 0.0) for ai in a]
                for n, ai, ii in zip(ids, a, incl):
                    c, half = chains[n]
                    zero = jnp.zeros_like(vblk[c])
                    vh = jnp.where(first, vblk[c], zero) if half == 0 else jnp.where(first, zero, vblk[c])
                    accs[c] = accs[c] + _nn(ai.astype(BF16), vh)
                    cvals[c] = jnp.where(lane == kb + HEAD_DIM * half, carries[n], cvals[c])
                    carries[n] = carries[n] + ii[:, 0:1]
            return tuple(carries), tuple(accs), tuple(cvals)

        zeros = tuple(jnp.zeros((BLK, LANES), F32) for _ in range(ATT_COLS))
        state = (tuple(jnp.zeros((BLK, 1), F32) for _ in chains), zeros, zeros)
        state = step(qi, state, True)

        def reaches_further(st):
            it, (carries, _, _) = st
            least = functools.reduce(jnp.minimum, carries)
            return jnp.logical_and(it < qi, jnp.min(least) < DEAD_AT)

        done, state = lax.while_loop(reaches_further, lambda st: (st[0] + 1, step(qi - 1 - st[0], st[1], False)),
                                     (jnp.int32(0), state))
        first_block = (qi - done).astype(F32)
        for c in range(ATT_COLS):
            o_ref[:, c * LANES:(c + 1) * LANES] = state[1][c]
            cs_ref[:, c * LANES:(c + 1) * LANES] = jnp.where(lane == FIRST_BLOCK_LANE, first_block, state[2][c])

    blk = pl.BlockSpec((BLK, width), lambda j, i: (i, j))
    col = pl.BlockSpec((T, width), lambda j, i: (0, j))
    out = jax.ShapeDtypeStruct((T, ATTN_DIM), F32)
    return _call(
        body, name=name, grid=(ATTN_DIM // width, T // BLK),
        in_specs=[blk, col, col, _whole((BLK, BLK))],
        out_specs=[blk, blk], out_shape=[out, out],
        args=(qs, k, v, tri,), ride=ride)


def _shifted_copies(pad_ref, sh_ref):
    rows = sh_ref.shape[1]
    for b in range(SUBLANES):
        sh_ref[b] = pad_ref[b:b + rows, :]


def _shift_of(offset):
    return offset % SUBLANES, offset - offset % SUBLANES


def _conv_fwd(ug, dw_w, dw_b, ln_g, ln_b, name, ride=None):
    T = ug.shape[0]
    per = TM // HALO

    def body(cv_ref, cg_ref, cvh_ref, cgh_ref, w_ref, b_ref, g_ref, beta_ref, conv_ref, c2_ref, pad_ref, sh_ref):
        i = pl.program_id(0)
        halo = cvh_ref[...] * _sigmoid(cgh_ref[...])
        pad_ref[0:HALO, :] = jnp.where(i == 0, 0.0, halo)
        pad_ref[HALO:HALO + TM, :] = cv_ref[...] * _sigmoid(cg_ref[...])
        pad_ref[HALO + TM:, :] = jnp.zeros((SUBLANES, CONV_DIM), F32)
        _shifted_copies(pad_ref, sh_ref)
        taps = [w_ref[t:t + 1, :] for t in range(CONV_WIDTH)]

        def rows(j, _):
            r = pl.multiple_of(j * CONV_ROWS, CONV_ROWS)
            acc = jnp.zeros((CONV_ROWS, CONV_DIM), F32) + b_ref[...]
            for t in range(CONV_WIDTH):
                b, a = _shift_of(HALO - (CONV_WIDTH - 1) + t)
                acc = acc + taps[t] * sh_ref[b, pl.ds(r + a, CONV_ROWS), :]
            conv_ref[pl.ds(r, CONV_ROWS), :] = acc
            return 0

        lax.fori_loop(0, TM // CONV_ROWS, rows, 0)
        acc = conv_ref[...]
        mu = jnp.mean(acc, axis=-1, keepdims=True)
        xc = acc - mu
        rs = lax.rsqrt(jnp.mean(xc * xc, axis=-1, keepdims=True) + EPS)
        ln = xc * rs * g_ref[...] + beta_ref[...]
        c2_ref[...] = (ln * _sigmoid(ln)).astype(BF16)

    prev = lambda col: pl.BlockSpec((HALO, CHUNK), lambda i: (jnp.maximum(i * per - 1, 0), col))
    vec = _whole((1, CONV_DIM))
    return _call(
        body, name=name, grid=(T // TM,),
        in_specs=[_rows(TM, CHUNK, 1), _rows(TM, CHUNK, 2), prev(1), prev(2),
                  _whole((CONV_WIDTH, CONV_DIM)), vec, vec, vec],
        out_specs=[_rows(TM, CONV_DIM), _rows(TM, CONV_DIM)],
        out_shape=[jax.ShapeDtypeStruct((T, CONV_DIM), F32), jax.ShapeDtypeStruct((T, CONV_DIM), BF16)],
        scratch_shapes=[pltpu.VMEM((TM + HALO + SUBLANES, CONV_DIM), F32),
                        pltpu.VMEM((SUBLANES, TM + HALO, CONV_DIM), F32)],
        args=(ug, ug, ug, ug, dw_w, dw_b, ln_g, ln_b,), ride=ride)


def _mix_out_ple(o, ug, c2, h, p, head_mean, g_attn, g_conv, g_ple, w_pw, w_out, w_gate, w_ple, name, ride=None):
    T = h.shape[0]

    def body(o_ref, ga_ref, gc_ref, c2_ref, h_ref, p_ref, hm_ref, gao_ref, gco_ref, gpn_ref,
             wpw_ref, wout_ref, wg_ref, wple_ref,
             h2_ref, h1_ref, ycat_ref, hn2_ref, gate_ref, e_ref, c3_ref):
        ov = o_ref[...]
        rh = lax.rsqrt(_nn((ov * ov).astype(BF16), hm_ref[...]) + EPS)
        ga = ga_ref[...]
        ya = (ov * rh * gao_ref[...] * (ga * _sigmoid(ga))).astype(BF16)
        c3 = _nn(c2_ref[...], wpw_ref[...])
        c3_ref[...] = c3
        rc = lax.rsqrt(jnp.mean(c3 * c3, axis=-1, keepdims=True) + EPS)
        gc = gc_ref[...]
        yc = (c3 * rc * gco_ref[...] * (gc * _sigmoid(gc))).astype(BF16)
        ycat_ref[:, :ATTN_DIM] = ya
        ycat_ref[:, ATTN_DIM:] = yc
        h1 = h_ref[...] + _nn(ya, wout_ref[:ATTN_DIM, :]) + _nn(yc, wout_ref[ATTN_DIM:, :])
        h1_ref[...] = h1
        r1 = lax.rsqrt(jnp.mean(h1 * h1, axis=-1, keepdims=True) + EPS)
        hn2 = (h1 * r1 * gpn_ref[...]).astype(BF16)
        hn2_ref[...] = hn2
        gate = _sigmoid(_nn(hn2, wg_ref[...]))
        e = _nn(p_ref[...].astype(BF16), wple_ref[...])
        gate_ref[...] = gate
        e_ref[...] = e
        h2_ref[...] = h1 + e * gate

    f32 = lambda cols: jax.ShapeDtypeStruct((T, cols), F32)
    bf = lambda cols: jax.ShapeDtypeStruct((T, cols), BF16)
    return _call(
        body, name=name, grid=(T // TM,),
        in_specs=[_rows(TM, ATTN_DIM), _rows(TM, CHUNK, 0), _rows(TM, CHUNK, 3), _rows(TM, CONV_DIM),
                  _rows(TM, D_MODEL), _rows(TM, PLE_DIM), _whole((ATTN_DIM, ATTN_DIM)),
                  _whole((1, ATTN_DIM)), _whole((1, CONV_DIM)), _whole((1, D_MODEL)),
                  _whole((CONV_DIM, CONV_DIM)), _whole((D_MODEL, D_MODEL)), _whole((D_MODEL, D_MODEL)),
                  _whole((PLE_DIM, D_MODEL))],
        out_specs=[_rows(TM, D_MODEL), _rows(TM, D_MODEL), _rows(TM, D_MODEL), _rows(TM, D_MODEL),
                   _rows(TM, D_MODEL), _rows(TM, D_MODEL), _rows(TM, CONV_DIM)],
        out_shape=[f32(D_MODEL), f32(D_MODEL), bf(D_MODEL), bf(D_MODEL), f32(D_MODEL), f32(D_MODEL),
                   f32(CONV_DIM)],
        args=(o, ug, ug, c2, h, p, head_mean, g_attn, g_conv, g_ple, w_pw, w_out, w_gate, w_ple,), ride=ride)


def _final_loss(h, target, gain, name):
    T = h.shape[0]

    def body(h_ref, t_ref, g_ref, dh_ref, gsum_ref, loss_ref):
        @pl.when(pl.program_id(0) == 0)
        def _():
            gsum_ref[...] = jnp.zeros_like(gsum_ref)
            loss_ref[...] = jnp.zeros_like(loss_ref)

        hv = h_ref[...]
        r = lax.rsqrt(jnp.mean(hv * hv, axis=-1, keepdims=True) + EPS)
        xh = hv * r
        diff = xh * g_ref[...] - t_ref[...]
        loss_ref[...] += 0.5 * jnp.sum(jnp.mean(diff * diff, axis=-1, keepdims=True), axis=0, keepdims=True)
        dy = diff * (1.0 / D_MODEL)
        gsum_ref[...] += jnp.sum(dy * xh, axis=0, keepdims=True)
        dxh = dy * g_ref[...]
        dh_ref[...] = r * (dxh - xh * jnp.mean(dxh * xh, axis=-1, keepdims=True))

    return pl.pallas_call(
        body, name=name, grid=(T // TM,),
        in_specs=[_rows(TM, D_MODEL), _rows(TM, D_MODEL), _whole((1, D_MODEL))],
        out_specs=[_rows(TM, D_MODEL), _whole((1, D_MODEL)), _whole((1, LANES))],
        out_shape=[jax.ShapeDtypeStruct((T, D_MODEL), F32), jax.ShapeDtypeStruct((1, D_MODEL), F32),
                   jax.ShapeDtypeStruct((1, LANES), F32)],
        compiler_params=_params(1),
    )(h, target, gain)


def _ple_out_bwd(dh2, gate, e, h1, g_ple, w_gate, w_out, name, ride=None):
    T = dh2.shape[0]

    def body(dh2_ref, gate_ref, e_ref, h1_ref, gpn_ref, wg_ref, wout_ref,
             dh1_ref, dh1b_ref, dzg_ref, de_ref, dycat_ref, gsum_ref):
        @pl.when(pl.program_id(0) == 0)
        def _():
            gsum_ref[...] = jnp.zeros_like(gsum_ref)

        dh2v = dh2_ref[...]
        gate = gate_ref[...]
        de_ref[...] = (dh2v * gate).astype(BF16)
        dzg = (dh2v * e_ref[...] * gate * (1.0 - gate)).astype(BF16)
        dzg_ref[...] = dzg
        dhn2 = _nt(dzg, wg_ref[...])
        h1 = h1_ref[...]
        r1 = lax.rsqrt(jnp.mean(h1 * h1, axis=-1, keepdims=True) + EPS)
        xh = h1 * r1
        gsum_ref[...] += jnp.sum(dhn2 * xh, axis=0, keepdims=True)
        dxh = dhn2 * gpn_ref[...]
        dh1 = dh2v + r1 * (dxh - xh * jnp.mean(dxh * xh, axis=-1, keepdims=True))
        dh1_ref[...] = dh1
        dh1b = dh1.astype(BF16)
        dh1b_ref[...] = dh1b
        dycat_ref[...] = _nt(dh1b, wout_ref[...])

    f32 = jax.ShapeDtypeStruct((T, D_MODEL), F32)
    bf = jax.ShapeDtypeStruct((T, D_MODEL), BF16)
    full = _rows(TM, D_MODEL)
    return _call(
        body, name=name, grid=(T // TM,),
        in_specs=[full, full, full, full, _whole((1, D_MODEL)), _whole((D_MODEL, D_MODEL)),
                  _whole((D_MODEL, D_MODEL))],
        out_specs=[full, full, full, full, full, _whole((1, D_MODEL))],
        out_shape=[f32, bf, bf, bf, f32, jax.ShapeDtypeStruct((1, D_MODEL), F32)],
        args=(dh2, gate, e, h1, g_ple, w_gate, w_out,), ride=ride)


def _branch_bwd(dycat, o, ug, c3, conv, head_mean, g_attn, g_conv, ln_g, ln_b, w_pw, name, ride=None):
    T = o.shape[0]

    def body(dya_ref, dyc_ref, o_ref, ga_ref, gc_ref, c3_ref, conv_ref, hm_ref, gao_ref, gco_ref,
             lng_ref, lnb_ref, wpw_ref,
             do_ref, dga_ref, dgc_ref, dc3_ref, dconv_ref, sums_ref):
        @pl.when(pl.program_id(0) == 0)
        def _():
            sums_ref[...] = jnp.zeros_like(sums_ref)

        hm = hm_ref[...]
        col = lambda x: jnp.sum(x, axis=0, keepdims=True)
        ov = o_ref[...]
        rh = lax.rsqrt(_nn((ov * ov).astype(BF16), hm) + EPS)
        xh = ov * rh
        ga = ga_ref[...]
        sg = _sigmoid(ga)
        dya = dya_ref[...]
        don = dya * (ga * sg)
        dga_ref[...] = (dya * xh * gao_ref[...] * _dsilu(ga, sg)).astype(BF16)
        sums_ref[0:1, :] += col(don * xh)
        dxh = don * gao_ref[...]
        do_ref[...] = (rh * (dxh - xh * _dot_hilo(dxh * xh, hm))).astype(BF16)
        c3 = c3_ref[...]
        rc = lax.rsqrt(jnp.mean(c3 * c3, axis=-1, keepdims=True) + EPS)
        xh3 = c3 * rc
        gc = gc_ref[...]
        sgc = _sigmoid(gc)
        dyc = dyc_ref[...]
        dn3 = dyc * (gc * sgc)
        dgc_ref[...] = (dyc * xh3 * gco_ref[...] * _dsilu(gc, sgc)).astype(BF16)
        sums_ref[1:2, :] += col(dn3 * xh3)
        dxh3 = dn3 * gco_ref[...]
        dc3 = (rc * (dxh3 - xh3 * jnp.mean(dxh3 * xh3, axis=-1, keepdims=True))).astype(BF16)
        dc3_ref[...] = dc3
        dc2 = _nt(dc3, wpw_ref[...])
        cv = conv_ref[...]
        mu = jnp.mean(cv, axis=-1, keepdims=True)
        xc = cv - mu
        rs = lax.rsqrt(jnp.mean(xc * xc, axis=-1, keepdims=True) + EPS)
        xn = xc * rs
        ln = xn * lng_ref[...] + lnb_ref[...]
        dln = dc2 * _dsilu(ln, _sigmoid(ln))
        sums_ref[2:3, :] += col(dln * xn)
        sums_ref[3:4, :] += col(dln)
        dxn = dln * lng_ref[...]
        dconv = rs * (dxn - jnp.mean(dxn, axis=-1, keepdims=True)
                      - xn * jnp.mean(dxn * xn, axis=-1, keepdims=True))
        dconv_ref[...] = dconv
        sums_ref[4:5, :] += col(dconv)

    half = lambda dt: jax.ShapeDtypeStruct((T, CHUNK), dt)
    tile = _rows(TM, CHUNK)
    vec = _whole((1, CHUNK))
    return _call(
        body, name=name, grid=(T // TM,),
        in_specs=[_rows(TM, CHUNK, 0), _rows(TM, CHUNK, 1), tile, _rows(TM, CHUNK, 0), _rows(TM, CHUNK, 3),
                  tile, tile, _whole((ATTN_DIM, ATTN_DIM)), vec, vec, vec, vec, _whole((CONV_DIM, CONV_DIM))],
        out_specs=[tile, tile, tile, tile, tile, _whole((8, CHUNK))],
        out_shape=[half(BF16), half(BF16), half(BF16), half(BF16), half(F32),
                   jax.ShapeDtypeStruct((8, CHUNK), F32)],
        args=(dycat, dycat, o, ug, ug, c3, conv, head_mean, g_attn, g_conv, ln_g, ln_b, w_pw,), ride=ride)


def _conv_bwd(dconv, ug, dw_w, name, ride=None):
    T = dconv.shape[0]
    per = TM // HALO
    last = T // HALO - 1
    n_tiles = T // TM

    def body(d_ref, dn_ref, cv_ref, cg_ref, cvh_ref, cgh_ref, w_ref, dcv_ref, dcg_ref, dw_ref,
             dpad_ref, cpad_ref, dsh_ref, csh_ref, dw_acc):
        i = pl.program_id(0)

        @pl.when(i == 0)
        def _():
            dw_acc[...] = jnp.zeros_like(dw_acc)

        tail = jnp.zeros((SUBLANES, CONV_DIM), F32)
        dpad_ref[0:TM, :] = d_ref[...]
        dpad_ref[TM:TM + HALO, :] = jnp.where(i == n_tiles - 1, 0.0, dn_ref[...])
        dpad_ref[TM + HALO:, :] = tail
        halo = cvh_ref[...] * _sigmoid(cgh_ref[...])
        cpad_ref[0:HALO, :] = jnp.where(i == 0, 0.0, halo)
        cpad_ref[HALO:HALO + TM, :] = cv_ref[...] * _sigmoid(cg_ref[...])
        cpad_ref[HALO + TM:, :] = tail
        _shifted_copies(dpad_ref, dsh_ref)
        _shifted_copies(cpad_ref, csh_ref)
        taps = [w_ref[t:t + 1, :] for t in range(CONV_WIDTH)]

        def rows(j, _):
            r = pl.multiple_of(j * CONV_ROWS, CONV_ROWS)
            d = d_ref[pl.ds(r, CONV_ROWS), :]
            dc = jnp.zeros((CONV_ROWS, CONV_DIM), F32)
            for t in range(CONV_WIDTH):
                b, a = _shift_of(CONV_WIDTH - 1 - t)
                dc = dc + taps[t] * dsh_ref[b, pl.ds(r + a, CONV_ROWS), :]
                b, a = _shift_of(HALO - (CONV_WIDTH - 1) + t)
                prod = d * csh_ref[b, pl.ds(r + a, CONV_ROWS), :]
                dw_acc[t] += jnp.sum(prod.reshape(CONV_ROWS // SUBLANES, SUBLANES, CONV_DIM), axis=0)
            cv = cv_ref[pl.ds(r, CONV_ROWS), :]
            sg = _sigmoid(cg_ref[pl.ds(r, CONV_ROWS), :])
            dcv_ref[pl.ds(r, CONV_ROWS), :] = (dc * sg).astype(BF16)
            dcg_ref[pl.ds(r, CONV_ROWS), :] = (dc * cv * sg * (1.0 - sg)).astype(BF16)
            return 0

        lax.fori_loop(0, TM // CONV_ROWS, rows, 0)

        @pl.when(i == n_tiles - 1)
        def _():
            dw_ref[...] = jnp.zeros_like(dw_ref)
            for t in range(CONV_WIDTH):
                dw_ref[t:t + 1, :] = jnp.sum(dw_acc[t], axis=0, keepdims=True)

    prev = lambda col: pl.BlockSpec((HALO, CHUNK), lambda i: (jnp.maximum(i * per - 1, 0), col))
    nxt = pl.BlockSpec((HALO, CONV_DIM), lambda i: (jnp.minimum((i + 1) * per, last), 0))
    half = jax.ShapeDtypeStruct((T, CHUNK), BF16)
    return _call(
        body, name=name, grid=(T // TM,),
        in_specs=[_rows(TM, CONV_DIM), nxt, _rows(TM, CHUNK, 1), _rows(TM, CHUNK, 2), prev(1), prev(2),
                  _whole((CONV_WIDTH, CONV_DIM))],
        out_specs=[_rows(TM, CHUNK), _rows(TM, CHUNK), _whole((HALO, CONV_DIM))],
        out_shape=[half, half, jax.ShapeDtypeStruct((HALO, CONV_DIM), F32)],
        scratch_shapes=[pltpu.VMEM((TM + HALO + SUBLANES, CONV_DIM), F32),
                        pltpu.VMEM((TM + HALO + SUBLANES, CONV_DIM), F32),
                        pltpu.VMEM((SUBLANES, TM + HALO, CONV_DIM), F32),
                        pltpu.VMEM((SUBLANES, TM + HALO, CONV_DIM), F32),
                        pltpu.VMEM((HALO, SUBLANES, CONV_DIM), F32)],
        args=(dconv, dconv, ug, ug, ug, ug, dw_w,), ride=ride)


def _attn_bwd(qs, k, v, do, cs, tri, tri_t, name, ride=None):
    T = qs.shape[0]
    nq = T // BLK
    width = LANES * ATT_COLS
    chains = [(c, half) for c in range(ATT_COLS) for half in range(2)]

    def body(q_ref, k_ref, v_ref, do_ref, cs_ref, m_ref, mt_ref, dq_ref, dk_ref, dv_ref, dk_acc, dv_acc):
        qi = pl.program_id(1)

        @pl.when(qi == 0)
        def _():
            dk_acc[...] = jnp.zeros_like(dk_acc)
            dv_acc[...] = jnp.zeros_like(dv_acc)

        lane = lax.broadcasted_iota(jnp.int32, (BLK, LANES), 1)
        first = lane < HEAD_DIM
        causal = (lax.broadcasted_iota(jnp.int32, (BLK, BLK), 1)
                  < lax.broadcasted_iota(jnp.int32, (BLK, BLK), 0))
        tri_m = m_ref[...]
        tri_mt = mt_ref[...]

        def halves(x):
            zero = jnp.zeros_like(x)
            return jnp.where(first, x, zero), jnp.where(first, zero, x)

        qh, doh, cs = {}, {}, []
        for c in range(ATT_COLS):
            qh[c, 0], qh[c, 1] = halves(q_ref[:, c * LANES:(c + 1) * LANES])
            doh[c, 0], doh[c, 1] = halves(do_ref[:, c * LANES:(c + 1) * LANES])
            cs.append(cs_ref[:, c * LANES:(c + 1) * LANES])

        def step(kb, state, masked):
            prefixes, dq_accs = state
            start = pl.multiple_of(kb * BLK, BLK)
            kblk = [k_ref[pl.ds(start, BLK), c * LANES:(c + 1) * LANES] for c in range(ATT_COLS)]
            vblk = [v_ref[pl.ds(start, BLK), c * LANES:(c + 1) * LANES] for c in range(ATT_COLS)]
            prefixes, dq_accs = list(prefixes), list(dq_accs)
            for g0 in range(0, len(chains), CHAIN_GROUP):
                ids = range(g0, g0 + CHAIN_GROUP)
                grp = [chains[n] for n in ids]
                z = [_nt(qh[ch], kblk[ch[0]]) for ch in grp]
                da = [_nt(doh[ch], vblk[ch[0]]) for ch in grp]
                parts = [_softplus_parts(zi) for zi in z]
                sp = [pt[1] for pt in parts]
                if masked:
                    sp = [jnp.where(causal, s, 0.0) for s in sp]
                incl = [_dot_hilo(s, tri_m) for s in sp]
                carries = [jnp.sum(jnp.where(lane == kb + HEAD_DIM * half, cs[c], 0.0), axis=1, keepdims=True)
                           for c, half in grp]
                a = [jnp.exp(zi - ii - ci) for zi, ii, ci in zip(z, incl, carries)]
                if masked:
                    a = [jnp.where(causal, ai, 0.0) for ai in a]
                w = [ai * di for ai, di in zip(a, da)]
                pinc = [_nn(wi.astype(BF16), tri_mt) for wi in w]
                dz = [wi - pt[0] * (pi + prefixes[n]) for n, wi, pt, pi in zip(ids, w, parts, pinc)]
                if masked:
                    dz = [jnp.where(causal, d, 0.0) for d in dz]
                for j in range(0, CHAIN_GROUP, 2):
                    c = grp[j][0]
                    k0, k1 = halves(kblk[c])
                    dz0, dz1 = dz[j].astype(BF16), dz[j + 1].astype(BF16)
                    a0, a1 = a[j].astype(BF16), a[j + 1].astype(BF16)
                    dq_accs[c] = dq_accs[c] + _nn(dz0, k0) + _nn(dz1, k1)
                    dk_acc[pl.ds(start, BLK), c * LANES:(c + 1) * LANES] += _tn(dz0, qh[c, 0]) + _tn(dz1, qh[c, 1])
                    dv_acc[pl.ds(start, BLK), c * LANES:(c + 1) * LANES] += _tn(a0, doh[c, 0]) + _tn(a1, doh[c, 1])
                for n, pi in zip(ids, pinc):
                    prefixes[n] = prefixes[n] + pi[:, BLK - 1:BLK]
            return tuple(prefixes), tuple(dq_accs)

        state = (tuple(jnp.zeros((BLK, 1), F32) for _ in chains),
                 tuple(jnp.zeros((BLK, LANES), F32) for _ in range(ATT_COLS)))
        first_block = jnp.max(jnp.where(lane == FIRST_BLOCK_LANE, cs[0], 0.0)).astype(jnp.int32)
        state = lax.fori_loop(first_block, qi, lambda kb, st: step(kb, st, False), state)
        state = step(qi, state, True)
        for c in range(ATT_COLS):
            dq_ref[:, c * LANES:(c + 1) * LANES] = (state[1][c] * (HEAD_DIM ** -0.5)).astype(BF16)

        @pl.when(qi == nq - 1)
        def _():
            dk_ref[...] = dk_acc[...].astype(BF16)
            dv_ref[...] = dv_acc[...].astype(BF16)

    blk = pl.BlockSpec((BLK, width), lambda j, i: (i, j))
    col = pl.BlockSpec((T, width), lambda j, i: (0, j))
    out = jax.ShapeDtypeStruct((T, ATTN_DIM), BF16)
    return _call(
        body, name=name, grid=(ATTN_DIM // width, nq),
        in_specs=[blk, col, col, blk, blk, _whole((BLK, BLK)), _whole((BLK, BLK))],
        out_specs=[blk, col, col], out_shape=[out, out, out],
        scratch_shapes=[pltpu.VMEM((T, width), F32), pltpu.VMEM((T, width), F32)],
        args=(qs, k, v, do, cs, tri, tri_t,), ride=ride)


def _inproj_bwd(du, w_in_t, h, dh1, gain, name, ride=None):
    T = h.shape[0]

    def body(*refs):
        du_refs = refs[:N_CHUNK]
        w_ref, h_ref, dh1_ref, g_ref, dh_ref, gsum_ref = refs[N_CHUNK:]

        @pl.when(pl.program_id(0) == 0)
        def _():
            gsum_ref[...] = jnp.zeros_like(gsum_ref)

        dhn = jnp.zeros((TM, D_MODEL), F32)
        for j in range(N_CHUNK):
            dhn = dhn + _nn(du_refs[j][...], w_ref[j * CHUNK:(j + 1) * CHUNK, :])
        hv = h_ref[...]
        r = lax.rsqrt(jnp.mean(hv * hv, axis=-1, keepdims=True) + EPS)
        xh = hv * r
        gsum_ref[...] += jnp.sum(dhn * xh, axis=0, keepdims=True)
        dxh = dhn * g_ref[...]
        dh_ref[...] = dh1_ref[...] + r * (dxh - xh * jnp.mean(dxh * xh, axis=-1, keepdims=True))

    full = _rows(TM, D_MODEL)
    return _call(
        body, name=name, grid=(T // TM,),
        in_specs=[_rows(TM, CHUNK)] * N_CHUNK + [_whole((N_CHUNK * CHUNK, D_MODEL)), full, full,
                                                 _whole((1, D_MODEL))],
        out_specs=[full, _whole((1, D_MODEL))],
        out_shape=[jax.ShapeDtypeStruct((T, D_MODEL), F32), jax.ShapeDtypeStruct((1, D_MODEL), F32)],
        args=(*du, w_in_t, h, dh1, gain), ride=ride)


def _weight_grad(lhs_list, rhs, name, tk=CHUNK, ride=None):
    T, n_rhs = rhs.shape
    n = len(lhs_list)
    ka = lhs_list[0].shape[1]
    per = ka // tk

    def body(*refs):
        a_refs, b_ref, out_ref = refs[:n], refs[n], refs[n + 1]
        step = pl.program_id(0)
        for j in range(n):
            for s in range(per):
                @pl.when(step == j * per + s)
                def _(j=j, s=s):
                    out_ref[...] = _tn(a_refs[j][:, s * tk:(s + 1) * tk], b_ref[...]).astype(BF16)

    (grad,), landed = _call(
        body, name=name, grid=(n * per,),
        in_specs=[_whole((T, ka))] * n + [_whole((T, n_rhs))],
        out_specs=[pl.BlockSpec((tk, n_rhs), lambda i: (i, 0))],
        out_shape=[jax.ShapeDtypeStruct((n * ka, n_rhs), BF16)],
        args=(*lhs_list, rhs), ride=ride)
    return grad, landed


def _adamw_update(w, g, m, v):
    nm = ADAM_B1 * m + (1.0 - ADAM_B1) * g
    nv = ADAM_B2 * v + (1.0 - ADAM_B2) * (g * g)
    m_hat = nm / (1.0 - ADAM_B1 ** ADAM_STEP)
    v_hat = nv / (1.0 - ADAM_B2 ** ADAM_STEP)
    return -ADAM_LR * (m_hat / (jnp.sqrt(v_hat) + ADAM_EPS) + ADAM_WD * w), nm, nv


def _sum_adamw(slots, w, m, v, name):
    depth, R, C = w.shape
    tr = min(R, ADAMW_ROWS)

    def body(*refs):
        slot_refs, (w_ref, m_ref, v_ref, g_ref, d_ref, nm_ref, nv_ref) = refs[:depth], refs[depth:]
        for layer in range(depth):
            @pl.when(pl.program_id(0) == layer)
            def _(src=slot_refs[layer]):
                g = src[0].astype(F32)
                for s in range(1, src.shape[0]):
                    g = g + src[s].astype(F32)
                g_ref[0] = g
                d_ref[0], nm_ref[0], nv_ref[0] = _adamw_update(w_ref[0], g, m_ref[0], v_ref[0])

    slot_spec = lambda layer: pl.BlockSpec((slots[layer].shape[0], tr, C),
                                           lambda l, i: (0, jnp.where(l == layer, i, 0), 0))
    spec = pl.BlockSpec((1, tr, C), lambda l, i: (l, i, 0))
    out = jax.ShapeDtypeStruct((depth, R, C), F32)
    return pl.pallas_call(
        body, name=name, grid=(depth, R // tr),
        in_specs=[slot_spec(layer) for layer in range(depth)] + [spec] * 3,
        out_specs=[spec] * 4, out_shape=[out] * 4,
        compiler_params=_params(2),
    )(*slots, w, m, v)


def _adamw(w, g, m, v, name):
    R, C = w.shape
    tr = R
    for cand in (512, 256, 128, 64):
        if R % cand == 0 and R > cand:
            tr = cand
            break

    def body(w_ref, g_ref, m_ref, v_ref, d_ref, nm_ref, nv_ref):
        d_ref[...], nm_ref[...], nv_ref[...] = _adamw_update(w_ref[...], g_ref[...], m_ref[...], v_ref[...])

    spec = pl.BlockSpec((tr, C), lambda i: (i, 0))
    out = jax.ShapeDtypeStruct((R, C), F32)
    return pl.pallas_call(
        body, name=name, grid=(R // tr,),
        in_specs=[spec] * 4, out_specs=[spec] * 3, out_shape=[out, out, out],
        compiler_params=_params(1),
    )(w, g, m, v)


def _pack_small(values, scalar=None):
    pad = lambda a: jnp.pad(a, ((0, 0), (0, D_MODEL - a.shape[1])))
    last = jnp.zeros((1, D_MODEL), F32) if scalar is None else pad(scalar.reshape(1, 1))
    return jnp.concatenate([pad(values[name].reshape(rows, cols)) for name, _, rows, cols in SMALL_LAYOUT] + [last],
                           axis=0)


def _small_update(all_packs, state, name):
    n = len(SMALL_LAYOUT)

    def body(packs_ref, *refs):
        ins, outs = refs[:3 * n], refs[3 * n:]
        total = packs_ref[0]
        for s in range(1, N_DEV):
            total = total + packs_ref[s]
        for j, (_, at, rows, cols) in enumerate(SMALL_LAYOUT):
            g = total[at:at + rows, :cols]
            w_ref, m_ref, v_ref = ins[3 * j:3 * j + 3]
            outs[4 * j][...] = g
            outs[4 * j + 1][...], outs[4 * j + 2][...], outs[4 * j + 3][...] = _adamw_update(
                w_ref[...], g, m_ref[...], v_ref[...])
        outs[-2][...] = total[LOSS_ROW:LOSS_ROW + 1, :LANES]
        outs[-1][...] = total[SMALL_ROWS:, :]

    shapes = [jax.ShapeDtypeStruct((rows, cols), F32) for _, _, rows, cols in SMALL_LAYOUT for _ in range(4)]
    shapes += [jax.ShapeDtypeStruct((1, LANES), F32), jax.ShapeDtypeStruct((PACK_ROWS - SMALL_ROWS, D_MODEL), F32)]
    operands = [a for item in SMALL_LAYOUT for a in state[item[0]]]
    res = pl.pallas_call(body, name=name, out_shape=shapes, compiler_params=_params())(all_packs, *operands)
    per_name = {item[0]: tuple(res[4 * j:4 * j + 4]) for j, item in enumerate(SMALL_LAYOUT)}
    return per_name, res[-2][0, 0], res[-1]


def kernel(x, p, norm_g, w_in, attn_out_g, dw_w, dw_b, conv_ln_g, conv_ln_b, w_pw, conv_out_g, w_out, ple_norm_g, w_ple_gate, w_ple, final_g, loss_target, m_norm_g, m_w_in, m_attn_out_g, m_dw_w, m_dw_b, m_conv_ln_g, m_conv_ln_b, m_w_pw, m_conv_out_g, m_w_out, m_ple_norm_g, m_w_ple_gate, m_w_ple, m_final_g, v_norm_g, v_w_in, v_attn_out_g, v_dw_w, v_dw_b, v_conv_ln_g, v_conv_ln_b, v_w_pw, v_conv_out_g, v_w_out, v_ple_norm_g, v_w_ple_gate, v_w_ple, v_final_g):
    depth = w_in.shape[0]
    T = x.shape[1]
    given = dict(
        norm_g=norm_g, ple_norm_g=ple_norm_g, final_g=final_g, dw_b=dw_b, conv_ln_g=conv_ln_g, conv_ln_b=conv_ln_b,
        conv_out_g=conv_out_g, attn_out_g=attn_out_g,
        m_norm_g=m_norm_g, m_ple_norm_g=m_ple_norm_g, m_final_g=m_final_g, m_dw_b=m_dw_b, m_conv_ln_g=m_conv_ln_g,
        m_conv_ln_b=m_conv_ln_b, m_conv_out_g=m_conv_out_g, m_attn_out_g=m_attn_out_g,
        v_norm_g=v_norm_g, v_ple_norm_g=v_ple_norm_g, v_final_g=v_final_g, v_dw_b=v_dw_b, v_conv_ln_g=v_conv_ln_g,
        v_conv_ln_b=v_conv_ln_b, v_conv_out_g=v_conv_out_g, v_attn_out_g=v_attn_out_g)
    my_idx = 4 * lax.axis_index("x") + 2 * lax.axis_index("y") + lax.axis_index("c")

    ids = jnp.arange(BLK)
    tri = (ids[:, None] >= ids[None, :]).astype(BF16)
    tri_t = (ids[:, None] <= ids[None, :]).astype(BF16)
    hid = jnp.arange(ATTN_DIM) // HEAD_DIM
    head_mean = ((hid[:, None] == hid[None, :]).astype(F32) / HEAD_DIM).astype(BF16)

    w_names = ("w_in_t", "w_pw", "w_out", "w_gate", "w_ple")
    w_axes = dict(zip(w_names, (0, 0, 0, 0, 1)))
    shards = [dict(zip(w_names, (w_in[l].T.astype(BF16), w_pw[l].astype(BF16), w_out[l].astype(BF16),
                                 w_ple_gate[l].astype(BF16), w_ple[l].astype(BF16)))) for l in range(depth)]
    first = _all_gather([shards[0]["w_in_t"]] + [dw_w[l].T for l in range(depth)], [0] * (1 + depth),
                        "gather_weights_0")
    layers = []
    for l in range(depth):
        layers.append(dict(
            dw_w=first[1 + l].T,
            g_norm=norm_g[l][None], g_attn=jnp.tile(attn_out_g[l], N_HEADS)[None], dw_b=dw_b[l][None],
            ln_g=conv_ln_g[l][None], ln_b=conv_ln_b[l][None], g_conv=conv_out_g[l][None],
            g_ple=ple_norm_g[l][None], p=p[l, 0]))
    layers[0]["w_in_t"] = first[0]

    def rest_of(l, names):
        return [_Ride.gather2(shards[l][n], w_axes[n]) for n in names]

    h = x[0]
    saved = []
    for l, w in enumerate(layers):
        early, late = (w_names[3:], w_names[1:3]) if l == 0 else ((), ())
        (qs, k, v, ug, hn), landed = _prenorm_inproj(h, w["g_norm"], w["w_in_t"], f"inproj_{l}",
                                                     _Ride(rest_of(l, early)))
        w.update(zip(early, landed))
        ahead = [_Ride.gather2(shards[l + 1]["w_in_t"], 0, None, 0, W_IN_ROWS_ON_ATTN)] if l + 1 < depth else []
        own = w_names[1:] if l > 0 else ()
        (o, cs), landed = _attn_fwd(qs, k, v, tri, f"attn_fwd_{l}", _Ride(ahead + rest_of(l, own)))
        w_in_next = landed[:len(ahead)]
        w.update(zip(own, landed[len(ahead):]))
        (conv, c2), landed = _conv_fwd(ug, w["dw_w"], w["dw_b"], w["ln_g"], w["ln_b"], f"conv_fwd_{l}",
                                       _Ride(rest_of(l, late)))
        w.update(zip(late, landed))
        tail = [_Ride.gather2(shards[l + 1]["w_in_t"], 0, a, W_IN_ROWS_ON_ATTN,
                              shards[l + 1]["w_in_t"].shape[0] - W_IN_ROWS_ON_ATTN) for a in w_in_next]
        (h2, h1, ycat, hn2, gate, e, c3), landed = _mix_out_ple(
            o, ug, c2, h, w["p"], head_mean, w["g_attn"], w["g_conv"], w["g_ple"],
            w["w_pw"], w["w_out"], w["w_gate"], w["w_ple"], f"mix_{l}", _Ride(tail))
        if landed:
            layers[l + 1]["w_in_t"] = landed[0]
        saved.append(dict(h=h, qs=qs, k=k, v=v, ug=ug, hn=hn, o=o, cs=cs, conv=conv, c2=c2, h1=h1,
                          ycat=ycat, hn2=hn2, gate=gate, e=e, c3=c3))
        h = h2
    dh, g_final, loss_part = _final_loss(h, loss_target[0], final_g[None], "final_loss")

    small = {}
    dww_parts = [None] * depth
    slots = [dict() for _ in range(depth)]
    g_w_in = None
    for l in reversed(range(depth)):
        w, s = layers[l], saved[l]
        above = [None] if g_w_in is not None else []

        def part(i, above=above, g=g_w_in):
            return [_Ride.scatter(g, 0, above[0], *W_IN_GRAD_PARTS[i])] if above else []

        def scattered(grads, names):
            return [_Ride.scatter(grads[n], w_axes[n]) for n in names]

        (dh1, dh1b, dzg, de, dycat, g_ple_sum), landed = _ple_out_bwd(
            dh, s["gate"], s["e"], s["h1"], w["g_ple"], w["w_gate"], w["w_out"], f"ple_bwd_{l}", _Ride(part(0)))
        above[:1] = landed
        (do, dga, dgc, dc3, dconv, sums), landed = _branch_bwd(
            dycat, s["o"], s["ug"], s["c3"], s["conv"], head_mean, w["g_attn"], w["g_conv"],
            w["ln_g"], w["ln_b"], w["w_pw"], f"branch_bwd_{l}", _Ride(part(1)))
        above[:1] = landed
        grads = dict(
            w_pw=_weight_grad([s["c2"]], dc3, f"grad_w_pw_{l}")[0],
            w_out=_weight_grad([s["ycat"]], dh1b, f"grad_w_out_{l}")[0],
            w_gate=_weight_grad([s["hn2"]], dzg, f"grad_w_gate_{l}")[0],
            w_ple=_weight_grad([w["p"].astype(BF16)], de, f"grad_w_ple_{l}", tk=PLE_DIM)[0])
        (dcv, dcg, dww), landed = _conv_bwd(dconv, s["ug"], w["dw_w"], f"conv_bwd_{l}", _Ride(part(2)))
        above[:1] = landed
        (dq, dk, dv), landed = _attn_bwd(s["qs"], s["k"], s["v"], do, s["cs"], tri, tri_t, f"attn_bwd_{l}",
                                         _Ride(scattered(grads, w_names[1:])))
        slots[l].update(zip(w_names[1:], landed))
        du = [dq, dk, dv, dga, dcv, dcg, dgc]
        g_w_in_here, landed = _weight_grad(du, s["hn"], f"grad_w_in_{l}", ride=_Ride(part(3)))
        if above:
            slots[l + 1]["w_in_t"] = landed[0]
        tail = [_Ride.scatter_chips(_pair_reduce(g_w_in_here, f"pair_reduce_w_in_{l}"))] if l == 0 else []
        (dh, g_norm_sum), landed = _inproj_bwd(du, w["w_in_t"], s["h"], dh1, w["g_norm"], f"inproj_bwd_{l}",
                                               _Ride(tail))
        slots[l].update(zip(("w_in_t",), landed))
        g_w_in = g_w_in_here
        small[l] = dict(norm_g=g_norm_sum, ple_norm_g=g_ple_sum, attn_out_g=sums[0].reshape(N_HEADS, HEAD_DIM).sum(0),
                        conv_out_g=sums[1], conv_ln_g=sums[2], conv_ln_b=sums[3], dw_b=sums[4])
        dww_parts[l] = dww[:CONV_WIDTH]
    slots = [[sl[n] for n in w_names] for sl in slots]
    grad_x = dh[None]

    sums_of = {name: jnp.stack([small[l][name].reshape(-1) for l in range(depth)]) for name in small[0]}
    sums_of["final_g"] = g_final
    pack = jnp.concatenate([_pack_small(sums_of, scalar=loss_part[0, 0]), jnp.concatenate(dww_parts, axis=1),
                            jnp.zeros((PACK_ROWS - SMALL_ROWS - CONV_WIDTH, D_MODEL), F32)], axis=0)
    (all_packs,) = _all_gather([pack], [0], "gather_small_grads")
    state = {name: [given[pre + name].reshape(rows, cols) for pre in ("", "m_", "v_")]
             for name, _, rows, cols in SMALL_LAYOUT}
    updated, loss, dww_sum = _small_update(all_packs.reshape(N_DEV, PACK_ROWS, D_MODEL), state, "update_small")
    res = {kind: {name: val[k].reshape(given[name].shape) for name, val in updated.items()}
           for k, kind in enumerate("gdmv")}
    dww_full = dww_sum[:CONV_WIDTH].reshape(CONV_WIDTH, depth, CONV_DIM).transpose(1, 0, 2)
    g_dw_w = lax.dynamic_slice_in_dim(dww_full, my_idx * (CONV_DIM // N_DEV), CONV_DIM // N_DEV, axis=2)

    swap = lambda a: a.transpose(0, 2, 1)
    state = {"w_in": (w_in, m_w_in, v_w_in), "w_pw": (w_pw, m_w_pw, v_w_pw), "w_out": (w_out, m_w_out, v_w_out),
             "w_ple_gate": (w_ple_gate, m_w_ple_gate, v_w_ple_gate), "w_ple": (w_ple, m_w_ple, v_w_ple)}
    for at, name in enumerate(state):
        wv, mv, vv = [swap(a) for a in state[name]] if name == "w_in" else state[name]
        out = _sum_adamw([slots[l][at] for l in range(depth)], wv, mv, vv, f"adamw_{name}")
        out = [swap(a) for a in out] if name == "w_in" else out
        res["g"][name], res["d"][name], res["m"][name], res["v"][name] = out
    flat = lambda a: a.reshape(-1, a.shape[-1])
    res["g"]["dw_w"] = g_dw_w
    res["d"]["dw_w"], res["m"]["dw_w"], res["v"]["dw_w"] = [
        a.reshape(dw_w.shape) for a in _adamw(flat(dw_w), flat(g_dw_w), flat(m_dw_w), flat(v_dw_w), "adamw_dw_w")]

    order = ["norm_g", "w_in", "attn_out_g", "dw_w", "dw_b", "conv_ln_g", "conv_ln_b", "w_pw", "conv_out_g",
             "w_out", "ple_norm_g", "w_ple_gate", "w_ple", "final_g"]
    return (loss, grad_x, *[res["g"][n] for n in order], *[res["d"][n] for n in order],
            *[res["m"][n] for n in order], *[res["v"][n] for n in order])
```

```python
import functools

import jax
import jax.numpy as jnp
from jax import lax
from jax.experimental import pallas as pl
from jax.experimental.pallas import tpu as pltpu

F32 = jnp.float32
BF16 = jnp.bfloat16
MESH = pl.DeviceIdType.MESH

N_DEV = 8
D_MODEL = 1024
ATTN_DIM = 512
CONV_DIM = 512
HEAD_DIM = 64
N_HEADS = 8
CONV_WIDTH = 31
PLE_DIM = 256
CHUNK = 512
N_CHUNK = 7
EPS = 1e-6
ADAM_LR = 0.001
ADAM_B1 = 0.9
ADAM_B2 = 0.999
ADAM_EPS = 1e-08
ADAM_WD = 0.01
ADAM_STEP = 10

LANES = 128
BLK = 256
ATT_COLS = 4
CHAIN_GROUP = 4
SOFTPLUS_LINEAR_AT = 20.0
DEAD_AT = 110.0
FIRST_BLOCK_LANE = HEAD_DIM - 1
TM = 512
HALO = 32
SUBLANES = 8
CONV_ROWS = 32
ADAMW_ROWS = (224, 128, 64)
VMEM_LIMIT = 56 * 1024 * 1024
SMALL_ROWS = 16
SMALL_LAYOUT = (("norm_g", 0, 2, D_MODEL), ("ple_norm_g", 2, 2, D_MODEL), ("final_g", 4, 1, D_MODEL),
                ("dw_b", 5, 2, CONV_DIM), ("conv_ln_g", 7, 2, CONV_DIM), ("conv_ln_b", 9, 2, CONV_DIM),
                ("conv_out_g", 11, 2, CONV_DIM), ("attn_out_g", 13, 2, HEAD_DIM))
LOSS_ROW = 15
W_IN_ROWS_ON_ATTN = 288
W_IN_GRAD_PARTS = ((0, 96), (96, 80), (176, 144), (320, 128))
PACK_ROWS = 48


def _nn(a, b):
    return lax.dot_general(a, b, (((1,), (0,)), ((), ())), preferred_element_type=F32)


def _nt(a, b):
    return lax.dot_general(a, b, (((1,), (1,)), ((), ())), preferred_element_type=F32)


def _tn(a, b):
    return lax.dot_general(a, b, (((0,), (0,)), ((), ())), preferred_element_type=F32)


def _split(x):
    hi = x.astype(BF16)
    lo = (x - hi.astype(F32)).astype(BF16)
    return hi, lo


def _dot_hilo(x, m):
    hi, lo = _split(x)
    return _nn(hi, m) + _nn(lo, m)


def _sigmoid(x):
    return jax.nn.sigmoid(x)


def _dsilu(x, s):
    return s * (1.0 + x * (1.0 - s))


def _params(n_grid=0, vmem=VMEM_LIMIT):
    sem = ("arbitrary",) * n_grid if n_grid else None
    return pltpu.CompilerParams(dimension_semantics=sem, vmem_limit_bytes=vmem)


def _rows(tm, cols, col=0):
    return pl.BlockSpec((tm, cols), lambda i: (i, col))


def _whole(shape):
    zeros = (0,) * len(shape)
    return pl.BlockSpec(shape, lambda *_: zeros)


def _my_position():
    return lax.axis_index("x"), lax.axis_index("y"), lax.axis_index("c")


def _block(ref, axis, idx, size):
    start = pl.multiple_of(idx * size, size)
    if axis == 0:
        return ref.at[pl.ds(start, size), :]
    return ref.at[:, pl.ds(start, size)]


def _all_gather(shards, axes, name):
    n = len(shards)
    sizes = [s.shape[a] for s, a in zip(shards, axes)]

    def full_shape(s, a):
        shape = list(s.shape)
        shape[a] *= N_DEV
        return jax.ShapeDtypeStruct(tuple(shape), s.dtype)

    def body(*refs):
        ins, outs = refs[:n], refs[n:2 * n]
        send_sems, recv_sems, local_sems = refs[2 * n:]
        x, y, c = _my_position()
        me, sibling = (x, y, c), (x, y, 1 - c)
        chips = [(1 - x, y), (x, 1 - y), (1 - x, 1 - y)]

        def place(i, dev):
            return _block(outs[i], axes[i], 4 * dev[0] + 2 * dev[1] + dev[2], sizes[i])

        def copy(k, i, dev, to, src=None):
            return pltpu.make_async_remote_copy(
                src_ref=place(i, dev) if src is None else src, dst_ref=place(i, dev),
                send_sem=send_sems.at[k, i], recv_sem=recv_sems.at[k, i],
                device_id=to, device_id_type=MESH)

        mine = [pltpu.make_async_copy(ins[i], place(i, me), local_sems.at[i]) for i in range(n)]
        for cp in mine:
            cp.start()
        first = [copy(0, i, me, sibling, src=ins[i]) for i in range(n)]
        for j, chip in enumerate(chips):
            first += [copy(1 + j, i, me, (*chip, c), src=ins[i]) for i in range(n)]
        for cp in first:
            cp.start()
        passed = []
        for j, chip in enumerate(chips):
            for i in range(n):
                copy(1 + j, i, (*chip, c), me).wait_recv()
            hop = [copy(4 + j, i, (*chip, c), sibling) for i in range(n)]
            for cp in hop:
                cp.start()
            passed += hop
        for i in range(n):
            copy(0, i, sibling, me).wait_recv()
        for j, chip in enumerate(chips):
            for i in range(n):
                copy(4 + j, i, (*chip, 1 - c), me).wait_recv()
        for cp in first + passed:
            cp.wait_send()
        for cp in mine:
            cp.wait()

    any_spec = pl.BlockSpec(memory_space=pl.ANY)
    return pl.pallas_call(
        body, name=name,
        out_shape=[full_shape(s, a) for s, a in zip(shards, axes)],
        in_specs=[any_spec] * n, out_specs=[any_spec] * n,
        scratch_shapes=[pltpu.SemaphoreType.DMA((7, n)), pltpu.SemaphoreType.DMA((7, n)),
                        pltpu.SemaphoreType.DMA((n,))],
    )(*shards)


def _pair_reduce(g, name):
    n_chips = N_DEV // 2
    R, C = g.shape[0] // N_DEV, g.shape[1]

    def body(g_ref, out_ref, mine_ref, theirs_ref, send_sems, recv_sems, local_sems):
        x, y, c = _my_position()
        block = lambda d: g_ref.at[pl.ds(pl.multiple_of(d * R, 16), R), :]
        sends = [pltpu.make_async_remote_copy(
            src_ref=block(2 * j + 1 - c), dst_ref=theirs_ref.at[j], send_sem=send_sems.at[j],
            recv_sem=recv_sems.at[j], device_id=(x, y, 1 - c), device_id_type=MESH) for j in range(n_chips)]
        own = [pltpu.make_async_copy(block(2 * j + c), mine_ref.at[j], local_sems.at[j]) for j in range(n_chips)]
        for cp in sends + own:
            cp.start()
        for j in range(n_chips):
            own[j].wait()
            sends[j].wait_recv()
            out_ref[j] = (mine_ref[j].astype(F32) + theirs_ref[j].astype(F32)).astype(g.dtype)
        for cp in sends:
            cp.wait_send()

    half = pltpu.VMEM((n_chips, R, C), g.dtype)
    sems = pltpu.SemaphoreType.DMA((n_chips,))
    return pl.pallas_call(
        body, name=name, out_shape=jax.ShapeDtypeStruct((n_chips, R, C), g.dtype),
        in_specs=[pl.BlockSpec(memory_space=pl.ANY)], out_specs=pl.BlockSpec(memory_space=pltpu.VMEM),
        scratch_shapes=[half, half, sems, sems, sems], compiler_params=_params(),
    )(g)


class _Ride:
    def __init__(self, parts):
        self.parts = [p for p in parts if p is not None]

    @staticmethod
    def gather(src, axis, land=None, lo=0, n=None):
        return ("gather", src, land, axis, lo, src.shape[axis] if n is None else n)

    @staticmethod
    def gather2(src, axis, land=None, lo=0, n=None):
        return ("gather2", src, land, axis, lo, src.shape[axis] if n is None else n)

    @staticmethod
    def scatter(src, axis, land=None, lo=0, n=None):
        return ("scatter", src, land, axis, lo, src.shape[axis] // N_DEV if n is None else n)

    @staticmethod
    def scatter_chips(chip_sums):
        return ("scatter_chips", chip_sums, None, 0, 0, chip_sums.shape[1])

    def arrays(self):
        return [p[1] for p in self.parts] + [p[2] for p in self.parts if p[2] is not None]

    def out_shapes(self):
        out = []
        for kind, src, _, axis, _, _ in self.parts:
            shape = list(src.shape)
            if kind in ("gather", "gather2"):
                shape[axis] *= N_DEV
            elif kind == "scatter_chips":
                pass
            else:
                shape[axis] //= N_DEV
                shape = [N_DEV] + shape
            out.append(jax.ShapeDtypeStruct(tuple(shape), src.dtype))
        return out

    def aliases(self, n_in, n_out):
        m, out = len(self.parts), {}
        for j, p in enumerate(self.parts):
            if p[2] is not None:
                out[n_in + m + len(out)] = n_out + j
        return out

    def scratch(self):
        m = len(self.parts)
        return [pltpu.SemaphoreType.DMA((N_DEV - 1, m)), pltpu.SemaphoreType.DMA((N_DEV - 1, m)),
                pltpu.SemaphoreType.DMA((m,))]

    def _copies(self, src_refs, land_refs, sems):
        send_sems, recv_sems, local_sems = sems
        x, y, c = _my_position()
        my_idx = 4 * x + 2 * y + c
        own, sends, relays, lands = [], [], [], []
        for j, (kind, src, _, axis, lo, n) in enumerate(self.parts):
            if kind == "scatter_chips":
                for k in (0, 2, 4, 6):
                    px, py = (1 - x if k & 4 else x), (1 - y if k & 2 else y)
                    a, b = src_refs[j].at[2 * px + py], land_refs[j].at[2 * x + y]
                    if k == 0:
                        own.append(pltpu.make_async_copy(a, b, local_sems.at[j]))
                        continue
                    mk = lambda dst, a=a, k=k, j=j, to=(px, py, c): pltpu.make_async_remote_copy(
                        src_ref=a, dst_ref=dst, send_sem=send_sems.at[k - 1, j], recv_sem=recv_sems.at[k - 1, j],
                        device_id=to, device_id_type=MESH)
                    sends.append(mk(b))
                    lands.append(mk(land_refs[j].at[2 * px + py]))
                continue
            size = src.shape[axis] if kind in ("gather", "gather2") else src.shape[axis] // N_DEV
            align = 16 if axis == 0 else LANES

            def rows(ref, idx, lead=None, axis=axis, lo=lo, n=n, size=size, align=align):
                at = pl.ds(pl.multiple_of(idx * size + lo, align), n)
                where = (at, slice(None)) if axis == 0 else (slice(None), at)
                return ref.at[where] if lead is None else ref.at[(lead, *where)]

            def in_shard(ref):
                return rows(ref, 0)

            def in_slot(ref, s):
                return rows(ref, 0, lead=s)

            if kind == "gather2":
                chips = [(1 - x, y), (x, 1 - y), (1 - x, 1 - y)]
                place = lambda px, py, pc: rows(land_refs[j], 4 * px + 2 * py + pc)

                def copy(i, a, dst, to, j=j):
                    return pltpu.make_async_remote_copy(
                        src_ref=a, dst_ref=dst, send_sem=send_sems.at[i, j], recv_sem=recv_sems.at[i, j],
                        device_id=to, device_id_type=MESH)

                mine = in_shard(src_refs[j])
                own.append(pltpu.make_async_copy(mine, place(x, y, c), local_sems.at[j]))
                sends.append(copy(0, mine, place(x, y, c), (x, y, 1 - c)))
                lands.append(copy(0, mine, place(x, y, 1 - c), (x, y, 1 - c)))
                for i, (px, py) in enumerate(chips):
                    sends.append(copy(1 + i, mine, place(x, y, c), (px, py, c)))
                    relays.append((copy(1 + i, mine, place(px, py, c), (px, py, c)),
                                   copy(4 + i, place(px, py, c), place(px, py, c), (x, y, 1 - c))))
                    lands.append(copy(4 + i, mine, place(px, py, 1 - c), (x, y, 1 - c)))
                continue
            for k in range(N_DEV):
                px = 1 - x if k & 4 else x
                py = 1 - y if k & 2 else y
                pc = 1 - c if k & 1 else c
                peer_idx = 4 * px + 2 * py + pc
                if kind == "gather":
                    a, b, landed = in_shard(src_refs[j]), rows(land_refs[j], my_idx), rows(land_refs[j], peer_idx)
                else:
                    a, b, landed = rows(src_refs[j], peer_idx), in_slot(land_refs[j], my_idx), in_slot(land_refs[j], peer_idx)
                if k == 0:
                    own.append(pltpu.make_async_copy(a, b, local_sems.at[j]))
                    continue
                mk = lambda dst, a=a, k=k, j=j, to=(px, py, pc): pltpu.make_async_remote_copy(
                    src_ref=a, dst_ref=dst, send_sem=send_sems.at[k - 1, j], recv_sem=recv_sems.at[k - 1, j],
                    device_id=to, device_id_type=MESH)
                sends.append(mk(b))
                lands.append(mk(landed))
        return own, sends, relays, lands

    @property
    def relayed(self):
        return any(p[0] == "gather2" for p in self.parts)

    def start(self, src_refs, land_refs, sems):
        own, sends, _, _ = self._copies(src_refs, land_refs, sems)
        for cp in own + sends:
            cp.start()

    def relay(self, src_refs, land_refs, sems):
        for arrival, onward in self._copies(src_refs, land_refs, sems)[2]:
            arrival.wait_recv()
            onward.start()

    def wait(self, src_refs, land_refs, sems):
        own, sends, relays, lands = self._copies(src_refs, land_refs, sems)
        for cp in lands:
            cp.wait_recv()
        for cp in sends + [onward for _, onward in relays]:
            cp.wait_send()
        for cp in own:
            cp.wait()


def _call(body, *, name, grid, in_specs, out_specs, out_shape, args, scratch_shapes=(), ride=None):
    in_specs, out_specs, out_shape = list(in_specs), list(out_specs), list(out_shape)
    n_in, n_out, n_sc = len(in_specs), len(out_specs), len(scratch_shapes)
    if ride is None or not ride.parts:
        res = pl.pallas_call(body, name=name, grid=grid, in_specs=in_specs, out_specs=out_specs,
                             out_shape=out_shape, scratch_shapes=list(scratch_shapes),
                             compiler_params=_params(len(grid)))(*args)
        return list(res), []
    extra, m = ride.arrays(), len(ride.parts)

    def riding(*refs):
        a = n_in + len(extra)
        b = a + n_out
        srcs, lands, sems = refs[n_in:n_in + m], refs[b:b + m], refs[b + m + n_sc:]
        at = [pl.program_id(d) for d in range(len(grid))]

        @pl.when(functools.reduce(jnp.logical_and, [i == 0 for i in at]))
        def _():
            ride.start(srcs, lands, sems)

        if ride.relayed:
            step, n_steps = at[0], 1
            for i, g in zip(at[1:], grid[1:]):
                step = step * g + i
            for g in grid:
                n_steps *= g
            assert n_steps >= 2, "a two-level ride needs a grid step after the first"

            @pl.when(step == n_steps - 1)
            def _():
                ride.relay(srcs, lands, sems)

        body(*refs[:n_in], *refs[a:b], *refs[b + m:b + m + n_sc])

        @pl.when(functools.reduce(jnp.logical_and, [i == g - 1 for i, g in zip(at, grid)]))
        def _():
            ride.wait(srcs, lands, sems)

    hbm = pl.BlockSpec(memory_space=pl.ANY)
    res = pl.pallas_call(
        riding, name=name, grid=grid, in_specs=in_specs + [hbm] * len(extra), out_specs=out_specs + [hbm] * m,
        out_shape=out_shape + ride.out_shapes(), scratch_shapes=list(scratch_shapes) + ride.scratch(),
        input_output_aliases=ride.aliases(n_in, n_out), compiler_params=_params(len(grid)),
    )(*args, *extra)
    return list(res[:n_out]), list(res[n_out:])


def _prenorm_inproj(h, gain, w_in_t, name, ride=None):
    T = h.shape[0]

    def body(h_ref, g_ref, w_ref, q_ref, k_ref, v_ref, ug_ref, hn_ref):
        hv = h_ref[...]
        r = lax.rsqrt(jnp.mean(hv * hv, axis=-1, keepdims=True) + EPS)
        hn = (hv * r * g_ref[...]).astype(BF16)
        hn_ref[...] = hn
        for j in range(N_CHUNK):
            u = _nt(hn, w_ref[j * CHUNK:(j + 1) * CHUNK, :])
            if j == 0:
                q_ref[...] = (u * (HEAD_DIM ** -0.5)).astype(BF16)
            elif j == 1:
                k_ref[...] = u.astype(BF16)
            elif j == 2:
                v_ref[...] = u.astype(BF16)
            else:
                ug_ref[:, (j - 3) * CHUNK:(j - 2) * CHUNK] = u

    act = jax.ShapeDtypeStruct((T, CHUNK), BF16)
    return _call(
        body, name=name, grid=(T // TM,),
        in_specs=[_rows(TM, D_MODEL), _whole((1, D_MODEL)), _whole((N_CHUNK * CHUNK, D_MODEL))],
        out_specs=[_rows(TM, CHUNK)] * 3 + [_rows(TM, 4 * CHUNK), _rows(TM, D_MODEL)],
        out_shape=[act, act, act, jax.ShapeDtypeStruct((T, 4 * CHUNK), F32),
                   jax.ShapeDtypeStruct((T, D_MODEL), BF16)],
        args=(h, gain, w_in_t,), ride=ride)


def _softplus_parts(z):
    ez = jnp.exp(jnp.minimum(z, SOFTPLUS_LINEAR_AT))
    t = 1.0 + ez
    return ez * pl.reciprocal(t, approx=True), jnp.where(z > SOFTPLUS_LINEAR_AT, z, jnp.log(t))


def _attn_fwd(qs, k, v, tri, name, ride=None):
    T = qs.shape[0]
    assert T // BLK <= FIRST_BLOCK_LANE, "one lane per key block below the lane of the first block"
    width = LANES * ATT_COLS
    chains = [(c, half) for c in range(ATT_COLS) for half in range(2)]

    def body(q_ref, k_ref, v_ref, m_ref, o_ref, cs_ref):
        qi = pl.program_id(1)
        lane = lax.broadcasted_iota(jnp.int32, (BLK, LANES), 1)
        first = lane < HEAD_DIM
        causal = (lax.broadcasted_iota(jnp.int32, (BLK, BLK), 1)
                  < lax.broadcasted_iota(jnp.int32, (BLK, BLK), 0))
        tri_m = m_ref[...]
        qh = {}
        for c in range(ATT_COLS):
            q = q_ref[:, c * LANES:(c + 1) * LANES]
            zero = jnp.zeros_like(q)
            qh[c, 0], qh[c, 1] = jnp.where(first, q, zero), jnp.where(first, zero, q)

        def step(kb, state, masked, rows=BLK):
            carries, accs, cvals = state
            top = lambda full: full[:rows]
            put = (lambda new, full: new) if rows == BLK else (lambda new, full: jnp.concatenate([new, full[rows:]], 0))
            start = pl.multiple_of(kb * BLK, BLK)
            kblk = [k_ref[pl.ds(start, BLK), c * LANES:(c + 1) * LANES] for c in range(ATT_COLS)]
            vblk = [v_ref[pl.ds(start, BLK), c * LANES:(c + 1) * LANES] for c in range(ATT_COLS)]
            carries, accs, cvals = list(carries), list(accs), list(cvals)
            for g0 in range(0, len(chains), CHAIN_GROUP):
                ids = range(g0, g0 + CHAIN_GROUP)
                z = [_nt(top(qh[chains[n]]), kblk[chains[n][0]]) for n in ids]
                sp = [_softplus_parts(zi)[1] for zi in z]
                if masked:
                    sp = [jnp.where(top(causal), s, 0.0) for s in sp]
                incl = [_dot_hilo(s, tri_m) for s in sp]
                a = [jnp.exp(zi - ii - top(carries[n])) for n, zi, ii in zip(ids, z, incl)]
                if masked:
                    a = [jnp.where(top(causal), ai, 0.0) for ai in a]
                for n, ai, ii in zip(ids, a, incl):
                    c, half = chains[n]
                    zero = jnp.zeros_like(vblk[c])
                    vh = jnp.where(first, vblk[c], zero) if half == 0 else jnp.where(first, zero, vblk[c])
                    accs[c] = put(top(accs[c]) + _nn(ai.astype(BF16), vh), accs[c])
                    cvals[c] = jnp.where(lane == kb + HEAD_DIM * half, carries[n], cvals[c])
                    carries[n] = put(top(carries[n]) + ii[:, 0:1], carries[n])
            return tuple(carries), tuple(accs), tuple(cvals)

        zeros = tuple(jnp.zeros((BLK, LANES), F32) for _ in range(ATT_COLS))
        state = (tuple(jnp.zeros((BLK, 1), F32) for _ in chains), zeros, zeros)
        state = step(qi, state, True)

        def least(carries, rows=slice(None)):
            return jnp.min(functools.reduce(jnp.minimum, [ci[rows] for ci in carries]))

        def reaches_further(st):
            it, (carries, _, _) = st
            return jnp.logical_and(it < qi, least(carries) < DEAD_AT)

        def advance(st):
            it, state = st
            state = lax.cond(least(state[0], slice(BLK // 2, None)) >= DEAD_AT,
                             lambda s: step(qi - 1 - it, s, False, BLK // 2),
                             lambda s: step(qi - 1 - it, s, False), state)
            return it + 1, state

        done, state = lax.while_loop(reaches_further, advance, (jnp.int32(0), state))
        first_block = (qi - done).astype(F32)
        for c in range(ATT_COLS):
            o_ref[:, c * LANES:(c + 1) * LANES] = state[1][c]
            cs_ref[:, c * LANES:(c + 1) * LANES] = jnp.where(lane == FIRST_BLOCK_LANE, first_block, state[2][c])

    blk = pl.BlockSpec((BLK, width), lambda j, i: (i, j))
    col = pl.BlockSpec((T, width), lambda j, i: (0, j))
    out = jax.ShapeDtypeStruct((T, ATTN_DIM), F32)
    return _call(
        body, name=name, grid=(ATTN_DIM // width, T // BLK),
        in_specs=[blk, col, col, _whole((BLK, BLK))],
        out_specs=[blk, blk], out_shape=[out, out],
        args=(qs, k, v, tri,), ride=ride)


def _shifted_copies(pad_ref, sh_ref):
    rows = sh_ref.shape[1]
    for b in range(SUBLANES):
        sh_ref[b] = pad_ref[b:b + rows, :]


def _shift_of(offset):
    return offset % SUBLANES, offset - offset % SUBLANES


def _conv_fwd(ug, dw_w, dw_b, ln_g, ln_b, name, ride=None):
    T = ug.shape[0]
    per = TM // HALO

    def body(cv_ref, cg_ref, cvh_ref, cgh_ref, w_ref, b_ref, g_ref, beta_ref, conv_ref, c2_ref, pad_ref, sh_ref):
        i = pl.program_id(0)
        halo = cvh_ref[...] * _sigmoid(cgh_ref[...])
        pad_ref[0:HALO, :] = jnp.where(i == 0, 0.0, halo)
        pad_ref[HALO:HALO + TM, :] = cv_ref[...] * _sigmoid(cg_ref[...])
        pad_ref[HALO + TM:, :] = jnp.zeros((SUBLANES, CONV_DIM), F32)
        _shifted_copies(pad_ref, sh_ref)
        taps = [w_ref[t:t + 1, :] for t in range(CONV_WIDTH)]

        def rows(j, _):
            r = pl.multiple_of(j * CONV_ROWS, CONV_ROWS)
            acc = jnp.zeros((CONV_ROWS, CONV_DIM), F32) + b_ref[...]
            for t in range(CONV_WIDTH):
                b, a = _shift_of(HALO - (CONV_WIDTH - 1) + t)
                acc = acc + taps[t] * sh_ref[b, pl.ds(r + a, CONV_ROWS), :]
            conv_ref[pl.ds(r, CONV_ROWS), :] = acc
            return 0

        lax.fori_loop(0, TM // CONV_ROWS, rows, 0)
        acc = conv_ref[...]
        mu = jnp.mean(acc, axis=-1, keepdims=True)
        xc = acc - mu
        rs = lax.rsqrt(jnp.mean(xc * xc, axis=-1, keepdims=True) + EPS)
        ln = xc * rs * g_ref[...] + beta_ref[...]
        c2_ref[...] = (ln * _sigmoid(ln)).astype(BF16)

    prev = lambda col: pl.BlockSpec((HALO, CHUNK), lambda i: (jnp.maximum(i * per - 1, 0), col))
    vec = _whole((1, CONV_DIM))
    return _call(
        body, name=name, grid=(T // TM,),
        in_specs=[_rows(TM, CHUNK, 1), _rows(TM, CHUNK, 2), prev(1), prev(2),
                  _whole((CONV_WIDTH, CONV_DIM)), vec, vec, vec],
        out_specs=[_rows(TM, CONV_DIM), _rows(TM, CONV_DIM)],
        out_shape=[jax.ShapeDtypeStruct((T, CONV_DIM), F32), jax.ShapeDtypeStruct((T, CONV_DIM), BF16)],
        scratch_shapes=[pltpu.VMEM((TM + HALO + SUBLANES, CONV_DIM), F32),
                        pltpu.VMEM((SUBLANES, TM + HALO, CONV_DIM), F32)],
        args=(ug, ug, ug, ug, dw_w, dw_b, ln_g, ln_b,), ride=ride)


def _mix_out_ple(o, ug, c2, h, p, head_mean, g_attn, g_conv, g_ple, w_pw, w_out, w_gate, w_ple, name, ride=None):
    T = h.shape[0]

    def body(o_ref, ga_ref, gc_ref, c2_ref, h_ref, p_ref, hm_ref, gao_ref, gco_ref, gpn_ref,
             wpw_ref, wout_ref, wg_ref, wple_ref,
             h2_ref, h1_ref, ycat_ref, hn2_ref, gate_ref, e_ref, c3_ref):
        ov = o_ref[...]
        rh = lax.rsqrt(_nn((ov * ov).astype(BF16), hm_ref[...]) + EPS)
        ga = ga_ref[...]
        ya = (ov * rh * gao_ref[...] * (ga * _sigmoid(ga))).astype(BF16)
        c3 = _nn(c2_ref[...], wpw_ref[...])
        c3_ref[...] = c3
        rc = lax.rsqrt(jnp.mean(c3 * c3, axis=-1, keepdims=True) + EPS)
        gc = gc_ref[...]
        yc = (c3 * rc * gco_ref[...] * (gc * _sigmoid(gc))).astype(BF16)
        ycat_ref[:, :ATTN_DIM] = ya
        ycat_ref[:, ATTN_DIM:] = yc
        h1 = h_ref[...] + _nn(ya, wout_ref[:ATTN_DIM, :]) + _nn(yc, wout_ref[ATTN_DIM:, :])
        h1_ref[...] = h1
        r1 = lax.rsqrt(jnp.mean(h1 * h1, axis=-1, keepdims=True) + EPS)
        hn2 = (h1 * r1 * gpn_ref[...]).astype(BF16)
        hn2_ref[...] = hn2
        gate = _sigmoid(_nn(hn2, wg_ref[...]))
        e = _nn(p_ref[...].astype(BF16), wple_ref[...])
        gate_ref[...] = gate
        e_ref[...] = e
        h2_ref[...] = h1 + e * gate

    f32 = lambda cols: jax.ShapeDtypeStruct((T, cols), F32)
    bf = lambda cols: jax.ShapeDtypeStruct((T, cols), BF16)
    return _call(
        body, name=name, grid=(T // TM,),
        in_specs=[_rows(TM, ATTN_DIM), _rows(TM, CHUNK, 0), _rows(TM, CHUNK, 3), _rows(TM, CONV_DIM),
                  _rows(TM, D_MODEL), _rows(TM, PLE_DIM), _whole((ATTN_DIM, ATTN_DIM)),
                  _whole((1, ATTN_DIM)), _whole((1, CONV_DIM)), _whole((1, D_MODEL)),
                  _whole((CONV_DIM, CONV_DIM)), _whole((D_MODEL, D_MODEL)), _whole((D_MODEL, D_MODEL)),
                  _whole((PLE_DIM, D_MODEL))],
        out_specs=[_rows(TM, D_MODEL), _rows(TM, D_MODEL), _rows(TM, D_MODEL), _rows(TM, D_MODEL),
                   _rows(TM, D_MODEL), _rows(TM, D_MODEL), _rows(TM, CONV_DIM)],
        out_shape=[f32(D_MODEL), f32(D_MODEL), bf(D_MODEL), bf(D_MODEL), f32(D_MODEL), f32(D_MODEL),
                   f32(CONV_DIM)],
        args=(o, ug, ug, c2, h, p, head_mean, g_attn, g_conv, g_ple, w_pw, w_out, w_gate, w_ple,), ride=ride)


def _final_loss(h, target, gain, name):
    T = h.shape[0]

    def body(h_ref, t_ref, g_ref, dh_ref, gsum_ref, loss_ref):
        @pl.when(pl.program_id(0) == 0)
        def _():
            gsum_ref[...] = jnp.zeros_like(gsum_ref)
            loss_ref[...] = jnp.zeros_like(loss_ref)

        hv = h_ref[...]
        r = lax.rsqrt(jnp.mean(hv * hv, axis=-1, keepdims=True) + EPS)
        xh = hv * r
        diff = xh * g_ref[...] - t_ref[...]
        loss_ref[...] += 0.5 * jnp.sum(jnp.mean(diff * diff, axis=-1, keepdims=True), axis=0, keepdims=True)
        dy = diff * (1.0 / D_MODEL)
        gsum_ref[...] += jnp.sum(dy * xh, axis=0, keepdims=True)
        dxh = dy * g_ref[...]
        dh_ref[...] = r * (dxh - xh * jnp.mean(dxh * xh, axis=-1, keepdims=True))

    return pl.pallas_call(
        body, name=name, grid=(T // TM,),
        in_specs=[_rows(TM, D_MODEL), _rows(TM, D_MODEL), _whole((1, D_MODEL))],
        out_specs=[_rows(TM, D_MODEL), _whole((1, D_MODEL)), _whole((1, LANES))],
        out_shape=[jax.ShapeDtypeStruct((T, D_MODEL), F32), jax.ShapeDtypeStruct((1, D_MODEL), F32),
                   jax.ShapeDtypeStruct((1, LANES), F32)],
        compiler_params=_params(1),
    )(h, target, gain)


def _ple_out_bwd(dh2, gate, e, h1, g_ple, w_gate, w_out, name, ride=None):
    T = dh2.shape[0]

    def body(dh2_ref, gate_ref, e_ref, h1_ref, gpn_ref, wg_ref, wout_ref,
             dh1_ref, dh1b_ref, dzg_ref, de_ref, dycat_ref, gsum_ref):
        @pl.when(pl.program_id(0) == 0)
        def _():
            gsum_ref[...] = jnp.zeros_like(gsum_ref)

        dh2v = dh2_ref[...]
        gate = gate_ref[...]
        de_ref[...] = (dh2v * gate).astype(BF16)
        dzg = (dh2v * e_ref[...] * gate * (1.0 - gate)).astype(BF16)
        dzg_ref[...] = dzg
        dhn2 = _nt(dzg, wg_ref[...])
        h1 = h1_ref[...]
        r1 = lax.rsqrt(jnp.mean(h1 * h1, axis=-1, keepdims=True) + EPS)
        xh = h1 * r1
        gsum_ref[...] += jnp.sum(dhn2 * xh, axis=0, keepdims=True)
        dxh = dhn2 * gpn_ref[...]
        dh1 = dh2v + r1 * (dxh - xh * jnp.mean(dxh * xh, axis=-1, keepdims=True))
        dh1_ref[...] = dh1
        dh1b = dh1.astype(BF16)
        dh1b_ref[...] = dh1b
        dycat_ref[...] = _nt(dh1b, wout_ref[...])

    f32 = jax.ShapeDtypeStruct((T, D_MODEL), F32)
    bf = jax.ShapeDtypeStruct((T, D_MODEL), BF16)
    full = _rows(TM, D_MODEL)
    return _call(
        body, name=name, grid=(T // TM,),
        in_specs=[full, full, full, full, _whole((1, D_MODEL)), _whole((D_MODEL, D_MODEL)),
                  _whole((D_MODEL, D_MODEL))],
        out_specs=[full, full, full, full, full, _whole((1, D_MODEL))],
        out_shape=[f32, bf, bf, bf, f32, jax.ShapeDtypeStruct((1, D_MODEL), F32)],
        args=(dh2, gate, e, h1, g_ple, w_gate, w_out,), ride=ride)


def _branch_bwd(dycat, o, ug, c3, conv, head_mean, g_attn, g_conv, ln_g, ln_b, w_pw, name, ride=None):
    T = o.shape[0]

    def body(dya_ref, dyc_ref, o_ref, ga_ref, gc_ref, c3_ref, conv_ref, hm_ref, gao_ref, gco_ref,
             lng_ref, lnb_ref, wpw_ref,
             do_ref, dga_ref, dgc_ref, dc3_ref, dconv_ref, sums_ref):
        @pl.when(pl.program_id(0) == 0)
        def _():
            sums_ref[...] = jnp.zeros_like(sums_ref)

        hm = hm_ref[...]
        col = lambda x: jnp.sum(x, axis=0, keepdims=True)
        ov = o_ref[...]
        rh = lax.rsqrt(_nn((ov * ov).astype(BF16), hm) + EPS)
        xh = ov * rh
        ga = ga_ref[...]
        sg = _sigmoid(ga)
        dya = dya_ref[...]
        don = dya * (ga * sg)
        dga_ref[...] = (dya * xh * gao_ref[...] * _dsilu(ga, sg)).astype(BF16)
        sums_ref[0:1, :] += col(don * xh)
        dxh = don * gao_ref[...]
        do_ref[...] = (rh * (dxh - xh * _dot_hilo(dxh * xh, hm))).astype(BF16)
        c3 = c3_ref[...]
        rc = lax.rsqrt(jnp.mean(c3 * c3, axis=-1, keepdims=True) + EPS)
        xh3 = c3 * rc
        gc = gc_ref[...]
        sgc = _sigmoid(gc)
        dyc = dyc_ref[...]
        dn3 = dyc * (gc * sgc)
        dgc_ref[...] = (dyc * xh3 * gco_ref[...] * _dsilu(gc, sgc)).astype(BF16)
        sums_ref[1:2, :] += col(dn3 * xh3)
        dxh3 = dn3 * gco_ref[...]
        dc3 = (rc * (dxh3 - xh3 * jnp.mean(dxh3 * xh3, axis=-1, keepdims=True))).astype(BF16)
        dc3_ref[...] = dc3
        dc2 = _nt(dc3, wpw_ref[...])
        cv = conv_ref[...]
        mu = jnp.mean(cv, axis=-1, keepdims=True)
        xc = cv - mu
        rs = lax.rsqrt(jnp.mean(xc * xc, axis=-1, keepdims=True) + EPS)
        xn = xc * rs
        ln = xn * lng_ref[...] + lnb_ref[...]
        dln = dc2 * _dsilu(ln, _sigmoid(ln))
        sums_ref[2:3, :] += col(dln * xn)
        sums_ref[3:4, :] += col(dln)
        dxn = dln * lng_ref[...]
        dconv = rs * (dxn - jnp.mean(dxn, axis=-1, keepdims=True)
                      - xn * jnp.mean(dxn * xn, axis=-1, keepdims=True))
        dconv_ref[...] = dconv
        sums_ref[4:5, :] += col(dconv)

    half = lambda dt: jax.ShapeDtypeStruct((T, CHUNK), dt)
    tile = _rows(TM, CHUNK)
    vec = _whole((1, CHUNK))
    return _call(
        body, name=name, grid=(T // TM,),
        in_specs=[_rows(TM, CHUNK, 0), _rows(TM, CHUNK, 1), tile, _rows(TM, CHUNK, 0), _rows(TM, CHUNK, 3),
                  tile, tile, _whole((ATTN_DIM, ATTN_DIM)), vec, vec, vec, vec, _whole((CONV_DIM, CONV_DIM))],
        out_specs=[tile, tile, tile, tile, tile, _whole((8, CHUNK))],
        out_shape=[half(BF16), half(BF16), half(BF16), half(BF16), half(F32),
                   jax.ShapeDtypeStruct((8, CHUNK), F32)],
        args=(dycat, dycat, o, ug, ug, c3, conv, head_mean, g_attn, g_conv, ln_g, ln_b, w_pw,), ride=ride)


def _conv_bwd(dconv, ug, dw_w, name, ride=None):
    T = dconv.shape[0]
    per = TM // HALO
    last = T // HALO - 1
    n_tiles = T // TM

    def body(d_ref, dn_ref, cv_ref, cg_ref, cvh_ref, cgh_ref, w_ref, dcv_ref, dcg_ref, dw_ref,
             dpad_ref, cpad_ref, dsh_ref, csh_ref, dw_acc):
        i = pl.program_id(0)

        @pl.when(i == 0)
        def _():
            dw_acc[...] = jnp.zeros_like(dw_acc)

        tail = jnp.zeros((SUBLANES, CONV_DIM), F32)
        dpad_ref[0:TM, :] = d_ref[...]
        dpad_ref[TM:TM + HALO, :] = jnp.where(i == n_tiles - 1, 0.0, dn_ref[...])
        dpad_ref[TM + HALO:, :] = tail
        halo = cvh_ref[...] * _sigmoid(cgh_ref[...])
        cpad_ref[0:HALO, :] = jnp.where(i == 0, 0.0, halo)
        cpad_ref[HALO:HALO + TM, :] = cv_ref[...] * _sigmoid(cg_ref[...])
        cpad_ref[HALO + TM:, :] = tail
        _shifted_copies(dpad_ref, dsh_ref)
        _shifted_copies(cpad_ref, csh_ref)
        taps = [w_ref[t:t + 1, :] for t in range(CONV_WIDTH)]

        def rows(j, _):
            r = pl.multiple_of(j * CONV_ROWS, CONV_ROWS)
            d = d_ref[pl.ds(r, CONV_ROWS), :]
            dc = jnp.zeros((CONV_ROWS, CONV_DIM), F32)
            for t in range(CONV_WIDTH):
                b, a = _shift_of(CONV_WIDTH - 1 - t)
                dc = dc + taps[t] * dsh_ref[b, pl.ds(r + a, CONV_ROWS), :]
                b, a = _shift_of(HALO - (CONV_WIDTH - 1) + t)
                prod = d * csh_ref[b, pl.ds(r + a, CONV_ROWS), :]
                dw_acc[t] += jnp.sum(prod.reshape(CONV_ROWS // SUBLANES, SUBLANES, CONV_DIM), axis=0)
            cv = cv_ref[pl.ds(r, CONV_ROWS), :]
            sg = _sigmoid(cg_ref[pl.ds(r, CONV_ROWS), :])
            dcv_ref[pl.ds(r, CONV_ROWS), :] = (dc * sg).astype(BF16)
            dcg_ref[pl.ds(r, CONV_ROWS), :] = (dc * cv * sg * (1.0 - sg)).astype(BF16)
            return 0

        lax.fori_loop(0, TM // CONV_ROWS, rows, 0)

        @pl.when(i == n_tiles - 1)
        def _():
            dw_ref[...] = jnp.zeros_like(dw_ref)
            for t in range(CONV_WIDTH):
                dw_ref[t:t + 1, :] = jnp.sum(dw_acc[t], axis=0, keepdims=True)

    prev = lambda col: pl.BlockSpec((HALO, CHUNK), lambda i: (jnp.maximum(i * per - 1, 0), col))
    nxt = pl.BlockSpec((HALO, CONV_DIM), lambda i: (jnp.minimum((i + 1) * per, last), 0))
    half = jax.ShapeDtypeStruct((T, CHUNK), BF16)
    return _call(
        body, name=name, grid=(T // TM,),
        in_specs=[_rows(TM, CONV_DIM), nxt, _rows(TM, CHUNK, 1), _rows(TM, CHUNK, 2), prev(1), prev(2),
                  _whole((CONV_WIDTH, CONV_DIM))],
        out_specs=[_rows(TM, CHUNK), _rows(TM, CHUNK), _whole((HALO, CONV_DIM))],
        out_shape=[half, half, jax.ShapeDtypeStruct((HALO, CONV_DIM), F32)],
        scratch_shapes=[pltpu.VMEM((TM + HALO + SUBLANES, CONV_DIM), F32),
                        pltpu.VMEM((TM + HALO + SUBLANES, CONV_DIM), F32),
                        pltpu.VMEM((SUBLANES, TM + HALO, CONV_DIM), F32),
                        pltpu.VMEM((SUBLANES, TM + HALO, CONV_DIM), F32),
                        pltpu.VMEM((HALO, SUBLANES, CONV_DIM), F32)],
        args=(dconv, dconv, ug, ug, ug, ug, dw_w,), ride=ride)


def _attn_bwd(qs, k, v, do, cs, tri, tri_t, name, ride=None):
    T = qs.shape[0]
    nq = T // BLK
    width = LANES * ATT_COLS
    chains = [(c, half) for c in range(ATT_COLS) for half in range(2)]

    def body(q_ref, k_ref, v_ref, do_ref, cs_ref, m_ref, mt_ref, dq_ref, dk_ref, dv_ref, dk_acc, dv_acc):
        qi = pl.program_id(1)

        @pl.when(qi == 0)
        def _():
            dk_acc[...] = jnp.zeros_like(dk_acc)
            dv_acc[...] = jnp.zeros_like(dv_acc)

        lane = lax.broadcasted_iota(jnp.int32, (BLK, LANES), 1)
        first = lane < HEAD_DIM
        causal = (lax.broadcasted_iota(jnp.int32, (BLK, BLK), 1)
                  < lax.broadcasted_iota(jnp.int32, (BLK, BLK), 0))
        tri_m = m_ref[...]
        tri_mt = mt_ref[...]

        def halves(x):
            zero = jnp.zeros_like(x)
            return jnp.where(first, x, zero), jnp.where(first, zero, x)

        qh, doh, cs = {}, {}, []
        for c in range(ATT_COLS):
            qh[c, 0], qh[c, 1] = halves(q_ref[:, c * LANES:(c + 1) * LANES])
            doh[c, 0], doh[c, 1] = halves(do_ref[:, c * LANES:(c + 1) * LANES])
            cs.append(cs_ref[:, c * LANES:(c + 1) * LANES])

        def carry_of(kb, ch, rows=slice(None)):
            c, half = ch
            return jnp.sum(jnp.where(lane[rows] == kb + HEAD_DIM * half, cs[c][rows], 0.0), axis=1, keepdims=True)

        def step(kb, state, masked, rows=BLK):
            prefixes, dq_accs = state
            top = lambda full: full[:rows]
            put = (lambda new, full: new) if rows == BLK else (lambda new, full: jnp.concatenate([new, full[rows:]], 0))
            start = pl.multiple_of(kb * BLK, BLK)
            kblk = [k_ref[pl.ds(start, BLK), c * LANES:(c + 1) * LANES] for c in range(ATT_COLS)]
            vblk = [v_ref[pl.ds(start, BLK), c * LANES:(c + 1) * LANES] for c in range(ATT_COLS)]
            prefixes, dq_accs = list(prefixes), list(dq_accs)
            for g0 in range(0, len(chains), CHAIN_GROUP):
                ids = range(g0, g0 + CHAIN_GROUP)
                grp = [chains[n] for n in ids]
                z = [_nt(top(qh[ch]), kblk[ch[0]]) for ch in grp]
                da = [_nt(top(doh[ch]), vblk[ch[0]]) for ch in grp]
                parts = [_softplus_parts(zi) for zi in z]
                sp = [pt[1] for pt in parts]
                if masked:
                    sp = [jnp.where(top(causal), s, 0.0) for s in sp]
                incl = [_dot_hilo(s, tri_m) for s in sp]
                carries = [carry_of(kb, ch, slice(0, rows)) for ch in grp]
                a = [jnp.exp(zi - ii - ci) for zi, ii, ci in zip(z, incl, carries)]
                if masked:
                    a = [jnp.where(top(causal), ai, 0.0) for ai in a]
                w = [ai * di for ai, di in zip(a, da)]
                pinc = [_nn(wi.astype(BF16), tri_mt) for wi in w]
                dz = [wi - pt[0] * (pi + top(prefixes[n])) for n, wi, pt, pi in zip(ids, w, parts, pinc)]
                if masked:
                    dz = [jnp.where(top(causal), d, 0.0) for d in dz]
                for j in range(0, CHAIN_GROUP, 2):
                    c = grp[j][0]
                    k0, k1 = halves(kblk[c])
                    dz0, dz1 = dz[j].astype(BF16), dz[j + 1].astype(BF16)
                    a0, a1 = a[j].astype(BF16), a[j + 1].astype(BF16)
                    dq_accs[c] = put(top(dq_accs[c]) + _nn(dz0, k0) + _nn(dz1, k1), dq_accs[c])
                    dk_acc[pl.ds(start, BLK), c * LANES:(c + 1) * LANES] += (_tn(dz0, top(qh[c, 0]))
                                                                              + _tn(dz1, top(qh[c, 1])))
                    dv_acc[pl.ds(start, BLK), c * LANES:(c + 1) * LANES] += (_tn(a0, top(doh[c, 0]))
                                                                              + _tn(a1, top(doh[c, 1])))
                for n, pi in zip(ids, pinc):
                    prefixes[n] = put(top(prefixes[n]) + pi[:, BLK - 1:BLK], prefixes[n])
            return tuple(prefixes), tuple(dq_accs)

        def off_diagonal(kb, state):
            later = slice(BLK // 2, None)
            least = jnp.min(functools.reduce(jnp.minimum, [carry_of(kb, ch, later) for ch in chains]))
            return lax.cond(least >= DEAD_AT, lambda s: step(kb, s, False, BLK // 2),
                            lambda s: step(kb, s, False), state)

        state = (tuple(jnp.zeros((BLK, 1), F32) for _ in chains),
                 tuple(jnp.zeros((BLK, LANES), F32) for _ in range(ATT_COLS)))
        first_block = jnp.max(jnp.where(lane == FIRST_BLOCK_LANE, cs[0], 0.0)).astype(jnp.int32)
        state = lax.fori_loop(first_block, qi, off_diagonal, state)
        state = step(qi, state, True)
        for c in range(ATT_COLS):
            dq_ref[:, c * LANES:(c + 1) * LANES] = (state[1][c] * (HEAD_DIM ** -0.5)).astype(BF16)

        @pl.when(qi == nq - 1)
        def _():
            dk_ref[...] = dk_acc[...].astype(BF16)
            dv_ref[...] = dv_acc[...].astype(BF16)

    blk = pl.BlockSpec((BLK, width), lambda j, i: (i, j))
    col = pl.BlockSpec((T, width), lambda j, i: (0, j))
    out = jax.ShapeDtypeStruct((T, ATTN_DIM), BF16)
    return _call(
        body, name=name, grid=(ATTN_DIM // width, nq),
        in_specs=[blk, col, col, blk, blk, _whole((BLK, BLK)), _whole((BLK, BLK))],
        out_specs=[blk, col, col], out_shape=[out, out, out],
        scratch_shapes=[pltpu.VMEM((T, width), F32), pltpu.VMEM((T, width), F32)],
        args=(qs, k, v, do, cs, tri, tri_t,), ride=ride)


def _inproj_bwd(du, w_in_t, h, dh1, gain, name, ride=None):
    T = h.shape[0]

    def body(*refs):
        du_refs = refs[:N_CHUNK]
        w_ref, h_ref, dh1_ref, g_ref, dh_ref, gsum_ref = refs[N_CHUNK:]

        @pl.when(pl.program_id(0) == 0)
        def _():
            gsum_ref[...] = jnp.zeros_like(gsum_ref)

        dhn = jnp.zeros((TM, D_MODEL), F32)
        for j in range(N_CHUNK):
            dhn = dhn + _nn(du_refs[j][...], w_ref[j * CHUNK:(j + 1) * CHUNK, :])
        hv = h_ref[...]
        r = lax.rsqrt(jnp.mean(hv * hv, axis=-1, keepdims=True) + EPS)
        xh = hv * r
        gsum_ref[...] += jnp.sum(dhn * xh, axis=0, keepdims=True)
        dxh = dhn * g_ref[...]
        dh_ref[...] = dh1_ref[...] + r * (dxh - xh * jnp.mean(dxh * xh, axis=-1, keepdims=True))

    full = _rows(TM, D_MODEL)
    return _call(
        body, name=name, grid=(T // TM,),
        in_specs=[_rows(TM, CHUNK)] * N_CHUNK + [_whole((N_CHUNK * CHUNK, D_MODEL)), full, full,
                                                 _whole((1, D_MODEL))],
        out_specs=[full, _whole((1, D_MODEL))],
        out_shape=[jax.ShapeDtypeStruct((T, D_MODEL), F32), jax.ShapeDtypeStruct((1, D_MODEL), F32)],
        args=(*du, w_in_t, h, dh1, gain), ride=ride)


def _weight_grad(lhs_list, rhs, name, tk=CHUNK, ride=None):
    T, n_rhs = rhs.shape
    n = len(lhs_list)
    ka = lhs_list[0].shape[1]
    per = ka // tk

    def body(*refs):
        a_refs, b_ref, out_ref = refs[:n], refs[n], refs[n + 1]
        step = pl.program_id(0)
        for j in range(n):
            for s in range(per):
                @pl.when(step == j * per + s)
                def _(j=j, s=s):
                    out_ref[...] = _tn(a_refs[j][:, s * tk:(s + 1) * tk], b_ref[...]).astype(BF16)

    (grad,), landed = _call(
        body, name=name, grid=(n * per,),
        in_specs=[_whole((T, ka))] * n + [_whole((T, n_rhs))],
        out_specs=[pl.BlockSpec((tk, n_rhs), lambda i: (i, 0))],
        out_shape=[jax.ShapeDtypeStruct((n * ka, n_rhs), BF16)],
        args=(*lhs_list, rhs), ride=ride)
    return grad, landed


def _adamw_update(w, g, m, v):
    nm = ADAM_B1 * m + (1.0 - ADAM_B1) * g
    nv = ADAM_B2 * v + (1.0 - ADAM_B2) * (g * g)
    m_hat = nm / (1.0 - ADAM_B1 ** ADAM_STEP)
    v_hat = nv / (1.0 - ADAM_B2 ** ADAM_STEP)
    return -ADAM_LR * (m_hat / (jnp.sqrt(v_hat) + ADAM_EPS) + ADAM_WD * w), nm, nv


def _sum_adamw(slots, w, m, v, name):
    depth, R, C = w.shape
    tr = next(rows for rows in ADAMW_ROWS if R % rows == 0)

    def body(*refs):
        slot_refs, (w_ref, m_ref, v_ref, g_ref, d_ref, nm_ref, nv_ref) = refs[:depth], refs[depth:]
        for layer in range(depth):
            @pl.when(pl.program_id(0) == layer)
            def _(src=slot_refs[layer]):
                g = src[0].astype(F32)
                for s in range(1, src.shape[0]):
                    g = g + src[s].astype(F32)
                g_ref[0] = g
                d_ref[0], nm_ref[0], nv_ref[0] = _adamw_update(w_ref[0], g, m_ref[0], v_ref[0])

    slot_spec = lambda layer: pl.BlockSpec((slots[layer].shape[0], tr, C),
                                           lambda l, i: (0, jnp.where(l == layer, i, 0), 0))
    spec = pl.BlockSpec((1, tr, C), lambda l, i: (l, i, 0))
    out = jax.ShapeDtypeStruct((depth, R, C), F32)
    return pl.pallas_call(
        body, name=name, grid=(depth, R // tr),
        in_specs=[slot_spec(layer) for layer in range(depth)] + [spec] * 3,
        out_specs=[spec] * 4, out_shape=[out] * 4,
        compiler_params=_params(2),
    )(*slots, w, m, v)


def _adamw(w, g, m, v, name):
    R, C = w.shape
    tr = R
    for cand in (512, 256, 128, 64):
        if R % cand == 0 and R > cand:
            tr = cand
            break

    def body(w_ref, g_ref, m_ref, v_ref, d_ref, nm_ref, nv_ref):
        d_ref[...], nm_ref[...], nv_ref[...] = _adamw_update(w_ref[...], g_ref[...], m_ref[...], v_ref[...])

    spec = pl.BlockSpec((tr, C), lambda i: (i, 0))
    out = jax.ShapeDtypeStruct((R, C), F32)
    return pl.pallas_call(
        body, name=name, grid=(R // tr,),
        in_specs=[spec] * 4, out_specs=[spec] * 3, out_shape=[out, out, out],
        compiler_params=_params(1),
    )(w, g, m, v)


def _pack_small(values, scalar=None):
    pad = lambda a: jnp.pad(a, ((0, 0), (0, D_MODEL - a.shape[1])))
    last = jnp.zeros((1, D_MODEL), F32) if scalar is None else pad(scalar.reshape(1, 1))
    return jnp.concatenate([pad(values[name].reshape(rows, cols)) for name, _, rows, cols in SMALL_LAYOUT] + [last],
                           axis=0)


def _small_update(all_packs, state, name):
    n = len(SMALL_LAYOUT)

    def body(packs_ref, *refs):
        ins, outs = refs[:3 * n], refs[3 * n:]
        total = packs_ref[0]
        for s in range(1, N_DEV):
            total = total + packs_ref[s]
        for j, (_, at, rows, cols) in enumerate(SMALL_LAYOUT):
            g = total[at:at + rows, :cols]
            w_ref, m_ref, v_ref = ins[3 * j:3 * j + 3]
            outs[4 * j][...] = g
            outs[4 * j + 1][...], outs[4 * j + 2][...], outs[4 * j + 3][...] = _adamw_update(
                w_ref[...], g, m_ref[...], v_ref[...])
        outs[-2][...] = total[LOSS_ROW:LOSS_ROW + 1, :LANES]
        outs[-1][...] = total[SMALL_ROWS:, :]

    shapes = [jax.ShapeDtypeStruct((rows, cols), F32) for _, _, rows, cols in SMALL_LAYOUT for _ in range(4)]
    shapes += [jax.ShapeDtypeStruct((1, LANES), F32), jax.ShapeDtypeStruct((PACK_ROWS - SMALL_ROWS, D_MODEL), F32)]
    operands = [a for item in SMALL_LAYOUT for a in state[item[0]]]
    res = pl.pallas_call(body, name=name, out_shape=shapes, compiler_params=_params())(all_packs, *operands)
    per_name = {item[0]: tuple(res[4 * j:4 * j + 4]) for j, item in enumerate(SMALL_LAYOUT)}
    return per_name, res[-2][0, 0], res[-1]


def kernel(x, p, norm_g, w_in, attn_out_g, dw_w, dw_b, conv_ln_g, conv_ln_b, w_pw, conv_out_g, w_out, ple_norm_g, w_ple_gate, w_ple, final_g, loss_target, m_norm_g, m_w_in, m_attn_out_g, m_dw_w, m_dw_b, m_conv_ln_g, m_conv_ln_b, m_w_pw, m_conv_out_g, m_w_out, m_ple_norm_g, m_w_ple_gate, m_w_ple, m_final_g, v_norm_g, v_w_in, v_attn_out_g, v_dw_w, v_dw_b, v_conv_ln_g, v_conv_ln_b, v_w_pw, v_conv_out_g, v_w_out, v_ple_norm_g, v_w_ple_gate, v_w_ple, v_final_g):
    depth = w_in.shape[0]
    T = x.shape[1]
    given = dict(
        norm_g=norm_g, ple_norm_g=ple_norm_g, final_g=final_g, dw_b=dw_b, conv_ln_g=conv_ln_g, conv_ln_b=conv_ln_b,
        conv_out_g=conv_out_g, attn_out_g=attn_out_g,
        m_norm_g=m_norm_g, m_ple_norm_g=m_ple_norm_g, m_final_g=m_final_g, m_dw_b=m_dw_b, m_conv_ln_g=m_conv_ln_g,
        m_conv_ln_b=m_conv_ln_b, m_conv_out_g=m_conv_out_g, m_attn_out_g=m_attn_out_g,
        v_norm_g=v_norm_g, v_ple_norm_g=v_ple_norm_g, v_final_g=v_final_g, v_dw_b=v_dw_b, v_conv_ln_g=v_conv_ln_g,
        v_conv_ln_b=v_conv_ln_b, v_conv_out_g=v_conv_out_g, v_attn_out_g=v_attn_out_g)
    my_idx = 4 * lax.axis_index("x") + 2 * lax.axis_index("y") + lax.axis_index("c")

    ids = jnp.arange(BLK)
    tri = (ids[:, None] >= ids[None, :]).astype(BF16)
    tri_t = (ids[:, None] <= ids[None, :]).astype(BF16)
    hid = jnp.arange(ATTN_DIM) // HEAD_DIM
    head_mean = ((hid[:, None] == hid[None, :]).astype(F32) / HEAD_DIM).astype(BF16)

    w_names = ("w_in_t", "w_pw", "w_out", "w_gate", "w_ple")
    w_axes = dict(zip(w_names, (0, 0, 0, 0, 1)))
    shards = [dict(zip(w_names, (w_in[l].T.astype(BF16), w_pw[l].astype(BF16), w_out[l].astype(BF16),
                                 w_ple_gate[l].astype(BF16), w_ple[l].astype(BF16)))) for l in range(depth)]
    first = _all_gather([shards[0]["w_in_t"]] + [dw_w[l].T for l in range(depth)], [0] * (1 + depth),
                        "gather_weights_0")
    layers = []
    for l in range(depth):
        layers.append(dict(
            dw_w=first[1 + l].T,
            g_norm=norm_g[l][None], g_attn=jnp.tile(attn_out_g[l], N_HEADS)[None], dw_b=dw_b[l][None],
            ln_g=conv_ln_g[l][None], ln_b=conv_ln_b[l][None], g_conv=conv_out_g[l][None],
            g_ple=ple_norm_g[l][None], p=p[l, 0]))
    layers[0]["w_in_t"] = first[0]

    def rest_of(l, names):
        return [_Ride.gather2(shards[l][n], w_axes[n]) for n in names]

    h = x[0]
    saved = []
    for l, w in enumerate(layers):
        early, late = (w_names[3:], w_names[1:3]) if l == 0 else ((), ())
        (qs, k, v, ug, hn), landed = _prenorm_inproj(h, w["g_norm"], w["w_in_t"], f"inproj_{l}",
                                                     _Ride(rest_of(l, early)))
        w.update(zip(early, landed))
        ahead = [_Ride.gather2(shards[l + 1]["w_in_t"], 0, None, 0, W_IN_ROWS_ON_ATTN)] if l + 1 < depth else []
        own = w_names[1:] if l > 0 else ()
        (o, cs), landed = _attn_fwd(qs, k, v, tri, f"attn_fwd_{l}", _Ride(ahead + rest_of(l, own)))
        w_in_next = landed[:len(ahead)]
        w.update(zip(own, landed[len(ahead):]))
        (conv, c2), landed = _conv_fwd(ug, w["dw_w"], w["dw_b"], w["ln_g"], w["ln_b"], f"conv_fwd_{l}",
                                       _Ride(rest_of(l, late)))
        w.update(zip(late, landed))
        tail = [_Ride.gather2(shards[l + 1]["w_in_t"], 0, a, W_IN_ROWS_ON_ATTN,
                              shards[l + 1]["w_in_t"].shape[0] - W_IN_ROWS_ON_ATTN) for a in w_in_next]
        (h2, h1, ycat, hn2, gate, e, c3), landed = _mix_out_ple(
            o, ug, c2, h, w["p"], head_mean, w["g_attn"], w["g_conv"], w["g_ple"],
            w["w_pw"], w["w_out"], w["w_gate"], w["w_ple"], f"mix_{l}", _Ride(tail))
        if landed:
            layers[l + 1]["w_in_t"] = landed[0]
        saved.append(dict(h=h, qs=qs, k=k, v=v, ug=ug, hn=hn, o=o, cs=cs, conv=conv, c2=c2, h1=h1,
                          ycat=ycat, hn2=hn2, gate=gate, e=e, c3=c3))
        h = h2
    dh, g_final, loss_part = _final_loss(h, loss_target[0], final_g[None], "final_loss")

    small = {}
    dww_parts = [None] * depth
    slots = [dict() for _ in range(depth)]
    g_w_in = None
    for l in reversed(range(depth)):
        w, s = layers[l], saved[l]
        above = [None] if g_w_in is not None else []

        def part(i, above=above, g=g_w_in):
            return [_Ride.scatter(g, 0, above[0], *W_IN_GRAD_PARTS[i])] if above else []

        def scattered(grads, names):
            return [_Ride.scatter(grads[n], w_axes[n]) for n in names]

        (dh1, dh1b, dzg, de, dycat, g_ple_sum), landed = _ple_out_bwd(
            dh, s["gate"], s["e"], s["h1"], w["g_ple"], w["w_gate"], w["w_out"], f"ple_bwd_{l}", _Ride(part(0)))
        above[:1] = landed
        (do, dga, dgc, dc3, dconv, sums), landed = _branch_bwd(
            dycat, s["o"], s["ug"], s["c3"], s["conv"], head_mean, w["g_attn"], w["g_conv"],
            w["ln_g"], w["ln_b"], w["w_pw"], f"branch_bwd_{l}", _Ride(part(1)))
        above[:1] = landed
        grads = dict(
            w_pw=_weight_grad([s["c2"]], dc3, f"grad_w_pw_{l}")[0],
            w_out=_weight_grad([s["ycat"]], dh1b, f"grad_w_out_{l}")[0],
            w_gate=_weight_grad([s["hn2"]], dzg, f"grad_w_gate_{l}")[0],
            w_ple=_weight_grad([w["p"].astype(BF16)], de, f"grad_w_ple_{l}", tk=PLE_DIM)[0])
        (dcv, dcg, dww), landed = _conv_bwd(dconv, s["ug"], w["dw_w"], f"conv_bwd_{l}", _Ride(part(2)))
        above[:1] = landed
        (dq, dk, dv), landed = _attn_bwd(s["qs"], s["k"], s["v"], do, s["cs"], tri, tri_t, f"attn_bwd_{l}",
                                         _Ride(scattered(grads, w_names[1:])))
        slots[l].update(zip(w_names[1:], landed))
        du = [dq, dk, dv, dga, dcv, dcg, dgc]
        g_w_in_here, landed = _weight_grad(du, s["hn"], f"grad_w_in_{l}", ride=_Ride(part(3)))
        if above:
            slots[l + 1]["w_in_t"] = landed[0]
        tail = [_Ride.scatter_chips(_pair_reduce(g_w_in_here, f"pair_reduce_w_in_{l}"))] if l == 0 else []
        (dh, g_norm_sum), landed = _inproj_bwd(du, w["w_in_t"], s["h"], dh1, w["g_norm"], f"inproj_bwd_{l}",
                                               _Ride(tail))
        slots[l].update(zip(("w_in_t",), landed))
        g_w_in = g_w_in_here
        small[l] = dict(norm_g=g_norm_sum, ple_norm_g=g_ple_sum, attn_out_g=sums[0].reshape(N_HEADS, HEAD_DIM).sum(0),
                        conv_out_g=sums[1], conv_ln_g=sums[2], conv_ln_b=sums[3], dw_b=sums[4])
        dww_parts[l] = dww[:CONV_WIDTH]
    slots = [[sl[n] for n in w_names] for sl in slots]
    grad_x = dh[None]

    sums_of = {name: jnp.stack([small[l][name].reshape(-1) for l in range(depth)]) for name in small[0]}
    sums_of["final_g"] = g_final
    pack = jnp.concatenate([_pack_small(sums_of, scalar=loss_part[0, 0]), jnp.concatenate(dww_parts, axis=1),
                            jnp.zeros((PACK_ROWS - SMALL_ROWS - CONV_WIDTH, D_MODEL), F32)], axis=0)
    (all_packs,) = _all_gather([pack], [0], "gather_small_grads")
    state = {name: [given[pre + name].reshape(rows, cols) for pre in ("", "m_", "v_")]
             for name, _, rows, cols in SMALL_LAYOUT}
    updated, loss, dww_sum = _small_update(all_packs.reshape(N_DEV, PACK_ROWS, D_MODEL), state, "update_small")
    res = {kind: {name: val[k].reshape(given[name].shape) for name, val in updated.items()}
           for k, kind in enumerate("gdmv")}
    dww_full = dww_sum[:CONV_WIDTH].reshape(CONV_WIDTH, depth, CONV_DIM).transpose(1, 0, 2)
    g_dw_w = lax.dynamic_slice_in_dim(dww_full, my_idx * (CONV_DIM // N_DEV), CONV_DIM // N_DEV, axis=2)

    swap = lambda a: a.transpose(0, 2, 1)
    state = {"w_in": (w_in, m_w_in, v_w_in), "w_pw": (w_pw, m_w_pw, v_w_pw), "w_out": (w_out, m_w_out, v_w_out),
             "w_ple_gate": (w_ple_gate, m_w_ple_gate, v_w_ple_gate), "w_ple": (w_ple, m_w_ple, v_w_ple)}
    for at, name in enumerate(state):
        wv, mv, vv = [swap(a) for a in state[name]] if name == "w_in" else state[name]
        out = _sum_adamw([slots[l][at] for l in range(depth)], wv, mv, vv, f"adamw_{name}")
        out = [swap(a) for a in out] if name == "w_in" else out
        res["g"][name], res["d"][name], res["m"][name], res["v"][name] = out
    flat = lambda a: a.reshape(-1, a.shape[-1])
    res["g"]["dw_w"] = g_dw_w
    res["d"]["dw_w"], res["m"]["dw_w"], res["v"]["dw_w"] = [
        a.reshape(dw_w.shape) for a in _adamw(flat(dw_w), flat(g_dw_w), flat(m_dw_w), flat(v_dw_w), "adamw_dw_w")]

    order = ["norm_g", "w_in", "attn_out_g", "dw_w", "dw_b", "conv_ln_g", "conv_ln_b", "w_pw", "conv_out_g",
             "w_out", "ple_norm_g", "w_ple_gate", "w_ple", "final_g"]
    return (loss, grad_x, *[res["g"][n] for n in order], *[res["d"][n] for n in order],
            *[res["m"][n] for n in order], *[res["v"][n] for n in order])
```

```python
import functools

import jax
import jax.numpy as jnp
from jax import lax
from jax.experimental import pallas as pl
from jax.experimental.pallas import tpu as pltpu

F32 = jnp.float32
BF16 = jnp.bfloat16
MESH = pl.DeviceIdType.MESH

N_DEV = 8
D_MODEL = 1024
ATTN_DIM = 512
CONV_DIM = 512
HEAD_DIM = 64
N_HEADS = 8
CONV_WIDTH = 31
PLE_DIM = 256
CHUNK = 512
N_CHUNK = 7
EPS = 1e-6
ADAM_LR = 0.001
ADAM_B1 = 0.9
ADAM_B2 = 0.999
ADAM_EPS = 1e-08
ADAM_WD = 0.01
ADAM_STEP = 10

LANES = 128
BLK = 256
ATT_COLS = 4
CHAIN_GROUP = 4
SOFTPLUS_LINEAR_AT = 20.0
DEAD_AT = 110.0
LIVE_ROWS = 192
FIRST_BLOCK_LANE = HEAD_DIM - 1
TM = 512
HALO = 32
SUBLANES = 8
CONV_ROWS = 32
ADAMW_ROWS = (224, 128, 64)
VMEM_LIMIT = 56 * 1024 * 1024
SMALL_ROWS = 16
SMALL_LAYOUT = (("norm_g", 0, 2, D_MODEL), ("ple_norm_g", 2, 2, D_MODEL), ("final_g", 4, 1, D_MODEL),
                ("dw_b", 5, 2, CONV_DIM), ("conv_ln_g", 7, 2, CONV_DIM), ("conv_ln_b", 9, 2, CONV_DIM),
                ("conv_out_g", 11, 2, CONV_DIM), ("attn_out_g", 13, 2, HEAD_DIM))
LOSS_ROW = 15
W_IN_ROWS_ON_ATTN = 288
W_IN_GRAD_PARTS = ((0, 96), (96, 80), (176, 144), (320, 128))
PACK_ROWS = 48


def _nn(a, b):
    return lax.dot_general(a, b, (((1,), (0,)), ((), ())), preferred_element_type=F32)


def _nt(a, b):
    return lax.dot_general(a, b, (((1,), (1,)), ((), ())), preferred_element_type=F32)


def _tn(a, b):
    return lax.dot_general(a, b, (((0,), (0,)), ((), ())), preferred_element_type=F32)


def _split(x):
    hi = x.astype(BF16)
    lo = (x - hi.astype(F32)).astype(BF16)
    return hi, lo


def _dot_hilo(x, m):
    hi, lo = _split(x)
    return _nn(hi, m) + _nn(lo, m)


def _sigmoid(x):
    return jax.nn.sigmoid(x)


def _dsilu(x, s):
    return s * (1.0 + x * (1.0 - s))


def _params(n_grid=0, vmem=VMEM_LIMIT):
    sem = ("arbitrary",) * n_grid if n_grid else None
    return pltpu.CompilerParams(dimension_semantics=sem, vmem_limit_bytes=vmem)


def _rows(tm, cols, col=0):
    return pl.BlockSpec((tm, cols), lambda i: (i, col))


def _whole(shape):
    zeros = (0,) * len(shape)
    return pl.BlockSpec(shape, lambda *_: zeros)


def _my_position():
    return lax.axis_index("x"), lax.axis_index("y"), lax.axis_index("c")


def _block(ref, axis, idx, size):
    start = pl.multiple_of(idx * size, size)
    if axis == 0:
        return ref.at[pl.ds(start, size), :]
    return ref.at[:, pl.ds(start, size)]


def _all_gather(shards, axes, name):
    n = len(shards)
    sizes = [s.shape[a] for s, a in zip(shards, axes)]

    def full_shape(s, a):
        shape = list(s.shape)
        shape[a] *= N_DEV
        return jax.ShapeDtypeStruct(tuple(shape), s.dtype)

    def body(*refs):
        ins, outs = refs[:n], refs[n:2 * n]
        send_sems, recv_sems, local_sems = refs[2 * n:]
        x, y, c = _my_position()
        me, sibling = (x, y, c), (x, y, 1 - c)
        chips = [(1 - x, y), (x, 1 - y), (1 - x, 1 - y)]

        def place(i, dev):
            return _block(outs[i], axes[i], 4 * dev[0] + 2 * dev[1] + dev[2], sizes[i])

        def copy(k, i, dev, to, src=None):
            return pltpu.make_async_remote_copy(
                src_ref=place(i, dev) if src is None else src, dst_ref=place(i, dev),
                send_sem=send_sems.at[k, i], recv_sem=recv_sems.at[k, i],
                device_id=to, device_id_type=MESH)

        mine = [pltpu.make_async_copy(ins[i], place(i, me), local_sems.at[i]) for i in range(n)]
        for cp in mine:
            cp.start()
        first = [copy(0, i, me, sibling, src=ins[i]) for i in range(n)]
        for j, chip in enumerate(chips):
            first += [copy(1 + j, i, me, (*chip, c), src=ins[i]) for i in range(n)]
        for cp in first:
            cp.start()
        passed = []
        for j, chip in enumerate(chips):
            for i in range(n):
                copy(1 + j, i, (*chip, c), me).wait_recv()
            hop = [copy(4 + j, i, (*chip, c), sibling) for i in range(n)]
            for cp in hop:
                cp.start()
            passed += hop
        for i in range(n):
            copy(0, i, sibling, me).wait_recv()
        for j, chip in enumerate(chips):
            for i in range(n):
                copy(4 + j, i, (*chip, 1 - c), me).wait_recv()
        for cp in first + passed:
            cp.wait_send()
        for cp in mine:
            cp.wait()

    any_spec = pl.BlockSpec(memory_space=pl.ANY)
    return pl.pallas_call(
        body, name=name,
        out_shape=[full_shape(s, a) for s, a in zip(shards, axes)],
        in_specs=[any_spec] * n, out_specs=[any_spec] * n,
        scratch_shapes=[pltpu.SemaphoreType.DMA((7, n)), pltpu.SemaphoreType.DMA((7, n)),
                        pltpu.SemaphoreType.DMA((n,))],
    )(*shards)


def _pair_reduce(g, name):
    n_chips = N_DEV // 2
    R, C = g.shape[0] // N_DEV, g.shape[1]

    def body(g_ref, out_ref, mine_ref, theirs_ref, send_sems, recv_sems, local_sems):
        x, y, c = _my_position()
        block = lambda d: g_ref.at[pl.ds(pl.multiple_of(d * R, 16), R), :]
        sends = [pltpu.make_async_remote_copy(
            src_ref=block(2 * j + 1 - c), dst_ref=theirs_ref.at[j], send_sem=send_sems.at[j],
            recv_sem=recv_sems.at[j], device_id=(x, y, 1 - c), device_id_type=MESH) for j in range(n_chips)]
        own = [pltpu.make_async_copy(block(2 * j + c), mine_ref.at[j], local_sems.at[j]) for j in range(n_chips)]
        for cp in sends + own:
            cp.start()
        for j in range(n_chips):
            own[j].wait()
            sends[j].wait_recv()
            out_ref[j] = (mine_ref[j].astype(F32) + theirs_ref[j].astype(F32)).astype(g.dtype)
        for cp in sends:
            cp.wait_send()

    half = pltpu.VMEM((n_chips, R, C), g.dtype)
    sems = pltpu.SemaphoreType.DMA((n_chips,))
    return pl.pallas_call(
        body, name=name, out_shape=jax.ShapeDtypeStruct((n_chips, R, C), g.dtype),
        in_specs=[pl.BlockSpec(memory_space=pl.ANY)], out_specs=pl.BlockSpec(memory_space=pltpu.VMEM),
        scratch_shapes=[half, half, sems, sems, sems], compiler_params=_params(),
    )(g)


class _Ride:
    def __init__(self, parts):
        self.parts = [p for p in parts if p is not None]

    @staticmethod
    def gather(src, axis, land=None, lo=0, n=None):
        return ("gather", src, land, axis, lo, src.shape[axis] if n is None else n)

    @staticmethod
    def gather2(src, axis, land=None, lo=0, n=None):
        return ("gather2", src, land, axis, lo, src.shape[axis] if n is None else n)

    @staticmethod
    def scatter(src, axis, land=None, lo=0, n=None):
        return ("scatter", src, land, axis, lo, src.shape[axis] // N_DEV if n is None else n)

    @staticmethod
    def scatter_chips(chip_sums):
        return ("scatter_chips", chip_sums, None, 0, 0, chip_sums.shape[1])

    def arrays(self):
        return [p[1] for p in self.parts] + [p[2] for p in self.parts if p[2] is not None]

    def out_shapes(self):
        out = []
        for kind, src, _, axis, _, _ in self.parts:
            shape = list(src.shape)
            if kind in ("gather", "gather2"):
                shape[axis] *= N_DEV
            elif kind == "scatter_chips":
                pass
            else:
                shape[axis] //= N_DEV
                shape = [N_DEV] + shape
            out.append(jax.ShapeDtypeStruct(tuple(shape), src.dtype))
        return out

    def aliases(self, n_in, n_out):
        m, out = len(self.parts), {}
        for j, p in enumerate(self.parts):
            if p[2] is not None:
                out[n_in + m + len(out)] = n_out + j
        return out

    def scratch(self):
        m = len(self.parts)
        return [pltpu.SemaphoreType.DMA((N_DEV - 1, m)), pltpu.SemaphoreType.DMA((N_DEV - 1, m)),
                pltpu.SemaphoreType.DMA((m,))]

    def _copies(self, src_refs, land_refs, sems):
        send_sems, recv_sems, local_sems = sems
        x, y, c = _my_position()
        my_idx = 4 * x + 2 * y + c
        own, sends, relays, lands = [], [], [], []
        for j, (kind, src, _, axis, lo, n) in enumerate(self.parts):
            if kind == "scatter_chips":
                for k in (0, 2, 4, 6):
                    px, py = (1 - x if k & 4 else x), (1 - y if k & 2 else y)
                    a, b = src_refs[j].at[2 * px + py], land_refs[j].at[2 * x + y]
                    if k == 0:
                        own.append(pltpu.make_async_copy(a, b, local_sems.at[j]))
                        continue
                    mk = lambda dst, a=a, k=k, j=j, to=(px, py, c): pltpu.make_async_remote_copy(
                        src_ref=a, dst_ref=dst, send_sem=send_sems.at[k - 1, j], recv_sem=recv_sems.at[k - 1, j],
                        device_id=to, device_id_type=MESH)
                    sends.append(mk(b))
                    lands.append(mk(land_refs[j].at[2 * px + py]))
                continue
            size = src.shape[axis] if kind in ("gather", "gather2") else src.shape[axis] // N_DEV
            align = 16 if axis == 0 else LANES

            def rows(ref, idx, lead=None, axis=axis, lo=lo, n=n, size=size, align=align):
                at = pl.ds(pl.multiple_of(idx * size + lo, align), n)
                where = (at, slice(None)) if axis == 0 else (slice(None), at)
                return ref.at[where] if lead is None else ref.at[(lead, *where)]

            def in_shard(ref):
                return rows(ref, 0)

            def in_slot(ref, s):
                return rows(ref, 0, lead=s)

            if kind == "gather2":
                chips = [(1 - x, y), (x, 1 - y), (1 - x, 1 - y)]
                place = lambda px, py, pc: rows(land_refs[j], 4 * px + 2 * py + pc)

                def copy(i, a, dst, to, j=j):
                    return pltpu.make_async_remote_copy(
                        src_ref=a, dst_ref=dst, send_sem=send_sems.at[i, j], recv_sem=recv_sems.at[i, j],
                        device_id=to, device_id_type=MESH)

                mine = in_shard(src_refs[j])
                own.append(pltpu.make_async_copy(mine, place(x, y, c), local_sems.at[j]))
                sends.append(copy(0, mine, place(x, y, c), (x, y, 1 - c)))
                lands.append(copy(0, mine, place(x, y, 1 - c), (x, y, 1 - c)))
                for i, (px, py) in enumerate(chips):
                    sends.append(copy(1 + i, mine, place(x, y, c), (px, py, c)))
                    relays.append((copy(1 + i, mine, place(px, py, c), (px, py, c)),
                                   copy(4 + i, place(px, py, c), place(px, py, c), (x, y, 1 - c))))
                    lands.append(copy(4 + i, mine, place(px, py, 1 - c), (x, y, 1 - c)))
                continue
            for k in range(N_DEV):
                px = 1 - x if k & 4 else x
                py = 1 - y if k & 2 else y
                pc = 1 - c if k & 1 else c
                peer_idx = 4 * px + 2 * py + pc
                if kind == "gather":
                    a, b, landed = in_shard(src_refs[j]), rows(land_refs[j], my_idx), rows(land_refs[j], peer_idx)
                else:
                    a, b, landed = rows(src_refs[j], peer_idx), in_slot(land_refs[j], my_idx), in_slot(land_refs[j], peer_idx)
                if k == 0:
                    own.append(pltpu.make_async_copy(a, b, local_sems.at[j]))
                    continue
                mk = lambda dst, a=a, k=k, j=j, to=(px, py, pc): pltpu.make_async_remote_copy(
                    src_ref=a, dst_ref=dst, send_sem=send_sems.at[k - 1, j], recv_sem=recv_sems.at[k - 1, j],
                    device_id=to, device_id_type=MESH)
                sends.append(mk(b))
                lands.append(mk(landed))
        return own, sends, relays, lands

    @property
    def relayed(self):
        return any(p[0] == "gather2" for p in self.parts)

    def start(self, src_refs, land_refs, sems):
        own, sends, _, _ = self._copies(src_refs, land_refs, sems)
        for cp in own + sends:
            cp.start()

    def relay(self, src_refs, land_refs, sems):
        for arrival, onward in self._copies(src_refs, land_refs, sems)[2]:
            arrival.wait_recv()
            onward.start()

    def wait(self, src_refs, land_refs, sems):
        own, sends, relays, lands = self._copies(src_refs, land_refs, sems)
        for cp in lands:
            cp.wait_recv()
        for cp in sends + [onward for _, onward in relays]:
            cp.wait_send()
        for cp in own:
            cp.wait()


def _call(body, *, name, grid, in_specs, out_specs, out_shape, args, scratch_shapes=(), ride=None):
    in_specs, out_specs, out_shape = list(in_specs), list(out_specs), list(out_shape)
    n_in, n_out, n_sc = len(in_specs), len(out_specs), len(scratch_shapes)
    if ride is None or not ride.parts:
        res = pl.pallas_call(body, name=name, grid=grid, in_specs=in_specs, out_specs=out_specs,
                             out_shape=out_shape, scratch_shapes=list(scratch_shapes),
                             compiler_params=_params(len(grid)))(*args)
        return list(res), []
    extra, m = ride.arrays(), len(ride.parts)

    def riding(*refs):
        a = n_in + len(extra)
        b = a + n_out
        srcs, lands, sems = refs[n_in:n_in + m], refs[b:b + m], refs[b + m + n_sc:]
        at = [pl.program_id(d) for d in range(len(grid))]

        @pl.when(functools.reduce(jnp.logical_and, [i == 0 for i in at]))
        def _():
            ride.start(srcs, lands, sems)

        if ride.relayed:
            step, n_steps = at[0], 1
            for i, g in zip(at[1:], grid[1:]):
                step = step * g + i
            for g in grid:
                n_steps *= g
            assert n_steps >= 2, "a two-level ride needs a grid step after the first"

            @pl.when(step == n_steps - 1)
            def _():
                ride.relay(srcs, lands, sems)

        body(*refs[:n_in], *refs[a:b], *refs[b + m:b + m + n_sc])

        @pl.when(functools.reduce(jnp.logical_and, [i == g - 1 for i, g in zip(at, grid)]))
        def _():
            ride.wait(srcs, lands, sems)

    hbm = pl.BlockSpec(memory_space=pl.ANY)
    res = pl.pallas_call(
        riding, name=name, grid=grid, in_specs=in_specs + [hbm] * len(extra), out_specs=out_specs + [hbm] * m,
        out_shape=out_shape + ride.out_shapes(), scratch_shapes=list(scratch_shapes) + ride.scratch(),
        input_output_aliases=ride.aliases(n_in, n_out), compiler_params=_params(len(grid)),
    )(*args, *extra)
    return list(res[:n_out]), list(res[n_out:])


def _prenorm_inproj(h, gain, w_in_t, name, ride=None):
    T = h.shape[0]

    def body(h_ref, g_ref, w_ref, q_ref, k_ref, v_ref, ug_ref, hn_ref):
        hv = h_ref[...]
        r = lax.rsqrt(jnp.mean(hv * hv, axis=-1, keepdims=True) + EPS)
        hn = (hv * r * g_ref[...]).astype(BF16)
        hn_ref[...] = hn
        for j in range(N_CHUNK):
            u = _nt(hn, w_ref[j * CHUNK:(j + 1) * CHUNK, :])
            if j == 0:
                q_ref[...] = (u * (HEAD_DIM ** -0.5)).astype(BF16)
            elif j == 1:
                k_ref[...] = u.astype(BF16)
            elif j == 2:
                v_ref[...] = u.astype(BF16)
            else:
                ug_ref[:, (j - 3) * CHUNK:(j - 2) * CHUNK] = u

    act = jax.ShapeDtypeStruct((T, CHUNK), BF16)
    return _call(
        body, name=name, grid=(T // TM,),
        in_specs=[_rows(TM, D_MODEL), _whole((1, D_MODEL)), _whole((N_CHUNK * CHUNK, D_MODEL))],
        out_specs=[_rows(TM, CHUNK)] * 3 + [_rows(TM, 4 * CHUNK), _rows(TM, D_MODEL)],
        out_shape=[act, act, act, jax.ShapeDtypeStruct((T, 4 * CHUNK), F32),
                   jax.ShapeDtypeStruct((T, D_MODEL), BF16)],
        args=(h, gain, w_in_t,), ride=ride)


def _softplus_parts(z):
    ez = jnp.exp(jnp.minimum(z, SOFTPLUS_LINEAR_AT))
    t = 1.0 + ez
    return ez * pl.reciprocal(t, approx=True), jnp.where(z > SOFTPLUS_LINEAR_AT, z, jnp.log(t))


def _attn_fwd(qs, k, v, tri, name, ride=None):
    T = qs.shape[0]
    assert T // BLK <= FIRST_BLOCK_LANE, "one lane per key block below the lane of the first block"
    width = LANES * ATT_COLS
    chains = [(c, half) for c in range(ATT_COLS) for half in range(2)]

    def body(q_ref, k_ref, v_ref, m_ref, o_ref, cs_ref):
        qi = pl.program_id(1)
        lane = lax.broadcasted_iota(jnp.int32, (BLK, LANES), 1)
        first = lane < HEAD_DIM
        causal = (lax.broadcasted_iota(jnp.int32, (BLK, BLK), 1)
                  < lax.broadcasted_iota(jnp.int32, (BLK, BLK), 0))
        tri_m = m_ref[...]
        qh = {}
        for c in range(ATT_COLS):
            q = q_ref[:, c * LANES:(c + 1) * LANES]
            zero = jnp.zeros_like(q)
            qh[c, 0], qh[c, 1] = jnp.where(first, q, zero), jnp.where(first, zero, q)

        def step(kb, state, masked, rows=BLK):
            carries, accs, cvals = state
            top = lambda full: full[:rows]
            put = (lambda new, full: new) if rows == BLK else (lambda new, full: jnp.concatenate([new, full[rows:]], 0))
            start = pl.multiple_of(kb * BLK, BLK)
            kblk = [k_ref[pl.ds(start, BLK), c * LANES:(c + 1) * LANES] for c in range(ATT_COLS)]
            vblk = [v_ref[pl.ds(start, BLK), c * LANES:(c + 1) * LANES] for c in range(ATT_COLS)]
            carries, accs, cvals = list(carries), list(accs), list(cvals)
            for g0 in range(0, len(chains), CHAIN_GROUP):
                ids = range(g0, g0 + CHAIN_GROUP)
                z = [_nt(top(qh[chains[n]]), kblk[chains[n][0]]) for n in ids]
                sp = [_softplus_parts(zi)[1] for zi in z]
                if masked:
                    sp = [jnp.where(top(causal), s, 0.0) for s in sp]
                incl = [_dot_hilo(s, tri_m) for s in sp]
                a = [jnp.exp(zi - ii - top(carries[n])) for n, zi, ii in zip(ids, z, incl)]
                if masked:
                    a = [jnp.where(top(causal), ai, 0.0) for ai in a]
                for n, ai, ii in zip(ids, a, incl):
                    c, half = chains[n]
                    zero = jnp.zeros_like(vblk[c])
                    vh = jnp.where(first, vblk[c], zero) if half == 0 else jnp.where(first, zero, vblk[c])
                    accs[c] = put(top(accs[c]) + _nn(ai.astype(BF16), vh), accs[c])
                    cvals[c] = jnp.where(lane == kb + HEAD_DIM * half, carries[n], cvals[c])
                    carries[n] = put(top(carries[n]) + ii[:, 0:1], carries[n])
            return tuple(carries), tuple(accs), tuple(cvals)

        zeros = tuple(jnp.zeros((BLK, LANES), F32) for _ in range(ATT_COLS))
        state = (tuple(jnp.zeros((BLK, 1), F32) for _ in chains), zeros, zeros)
        state = step(qi, state, True)

        def least(carries, rows=slice(None)):
            return jnp.min(functools.reduce(jnp.minimum, [ci[rows] for ci in carries]))

        def reaches_further(st):
            it, (carries, _, _) = st
            return jnp.logical_and(it < qi, least(carries) < DEAD_AT)

        def advance(st):
            it, state = st
            state = lax.cond(least(state[0], slice(LIVE_ROWS, None)) >= DEAD_AT,
                             lambda s: step(qi - 1 - it, s, False, LIVE_ROWS),
                             lambda s: step(qi - 1 - it, s, False), state)
            return it + 1, state

        done, state = lax.while_loop(reaches_further, advance, (jnp.int32(0), state))
        first_block = (qi - done).astype(F32)
        for c in range(ATT_COLS):
            o_ref[:, c * LANES:(c + 1) * LANES] = state[1][c]
            cs_ref[:, c * LANES:(c + 1) * LANES] = jnp.where(lane == FIRST_BLOCK_LANE, first_block, state[2][c])

    blk = pl.BlockSpec((BLK, width), lambda j, i: (i, j))
    col = pl.BlockSpec((T, width), lambda j, i: (0, j))
    out = jax.ShapeDtypeStruct((T, ATTN_DIM), F32)
    return _call(
        body, name=name, grid=(ATTN_DIM // width, T // BLK),
        in_specs=[blk, col, col, _whole((BLK, BLK))],
        out_specs=[blk, blk], out_shape=[out, out],
        args=(qs, k, v, tri,), ride=ride)


def _shifted_copies(pad_ref, sh_ref):
    rows = sh_ref.shape[1]
    for b in range(SUBLANES):
        sh_ref[b] = pad_ref[b:b + rows, :]


def _shift_of(offset):
    return offset % SUBLANES, offset - offset % SUBLANES


def _conv_fwd(ug, dw_w, dw_b, ln_g, ln_b, name, ride=None):
    T = ug.shape[0]
    per = TM // HALO

    def body(cv_ref, cg_ref, cvh_ref, cgh_ref, w_ref, b_ref, g_ref, beta_ref, conv_ref, c2_ref, pad_ref, sh_ref):
        i = pl.program_id(0)
        halo = cvh_ref[...] * _sigmoid(cgh_ref[...])
        pad_ref[0:HALO, :] = jnp.where(i == 0, 0.0, halo)
        pad_ref[HALO:HALO + TM, :] = cv_ref[...] * _sigmoid(cg_ref[...])
        pad_ref[HALO + TM:, :] = jnp.zeros((SUBLANES, CONV_DIM), F32)
        _shifted_copies(pad_ref, sh_ref)
        taps = [w_ref[t:t + 1, :] for t in range(CONV_WIDTH)]

        def rows(j, _):
            r = pl.multiple_of(j * CONV_ROWS, CONV_ROWS)
            acc = jnp.zeros((CONV_ROWS, CONV_DIM), F32) + b_ref[...]
            for t in range(CONV_WIDTH):
                b, a = _shift_of(HALO - (CONV_WIDTH - 1) + t)
                acc = acc + taps[t] * sh_ref[b, pl.ds(r + a, CONV_ROWS), :]
            conv_ref[pl.ds(r, CONV_ROWS), :] = acc
            return 0

        lax.fori_loop(0, TM // CONV_ROWS, rows, 0)
        acc = conv_ref[...]
        mu = jnp.mean(acc, axis=-1, keepdims=True)
        xc = acc - mu
        rs = lax.rsqrt(jnp.mean(xc * xc, axis=-1, keepdims=True) + EPS)
        ln = xc * rs * g_ref[...] + beta_ref[...]
        c2_ref[...] = (ln * _sigmoid(ln)).astype(BF16)

    prev = lambda col: pl.BlockSpec((HALO, CHUNK), lambda i: (jnp.maximum(i * per - 1, 0), col))
    vec = _whole((1, CONV_DIM))
    return _call(
        body, name=name, grid=(T // TM,),
        in_specs=[_rows(TM, CHUNK, 1), _rows(TM, CHUNK, 2), prev(1), prev(2),
                  _whole((CONV_WIDTH, CONV_DIM)), vec, vec, vec],
        out_specs=[_rows(TM, CONV_DIM), _rows(TM, CONV_DIM)],
        out_shape=[jax.ShapeDtypeStruct((T, CONV_DIM), F32), jax.ShapeDtypeStruct((T, CONV_DIM), BF16)],
        scratch_shapes=[pltpu.VMEM((TM + HALO + SUBLANES, CONV_DIM), F32),
                        pltpu.VMEM((SUBLANES, TM + HALO, CONV_DIM), F32)],
        args=(ug, ug, ug, ug, dw_w, dw_b, ln_g, ln_b,), ride=ride)


def _mix_out_ple(o, ug, c2, h, p, head_mean, g_attn, g_conv, g_ple, w_pw, w_out, w_gate, w_ple, name, ride=None):
    T = h.shape[0]

    def body(o_ref, ga_ref, gc_ref, c2_ref, h_ref, p_ref, hm_ref, gao_ref, gco_ref, gpn_ref,
             wpw_ref, wout_ref, wg_ref, wple_ref,
             h2_ref, h1_ref, ycat_ref, hn2_ref, gate_ref, e_ref, c3_ref):
        ov = o_ref[...]
        rh = lax.rsqrt(_nn((ov * ov).astype(BF16), hm_ref[...]) + EPS)
        ga = ga_ref[...]
        ya = (ov * rh * gao_ref[...] * (ga * _sigmoid(ga))).astype(BF16)
        c3 = _nn(c2_ref[...], wpw_ref[...])
        c3_ref[...] = c3
        rc = lax.rsqrt(jnp.mean(c3 * c3, axis=-1, keepdims=True) + EPS)
        gc = gc_ref[...]
        yc = (c3 * rc * gco_ref[...] * (gc * _sigmoid(gc))).astype(BF16)
        ycat_ref[:, :ATTN_DIM] = ya
        ycat_ref[:, ATTN_DIM:] = yc
        h1 = h_ref[...] + _nn(ya, wout_ref[:ATTN_DIM, :]) + _nn(yc, wout_ref[ATTN_DIM:, :])
        h1_ref[...] = h1
        r1 = lax.rsqrt(jnp.mean(h1 * h1, axis=-1, keepdims=True) + EPS)
        hn2 = (h1 * r1 * gpn_ref[...]).astype(BF16)
        hn2_ref[...] = hn2
        gate = _sigmoid(_nn(hn2, wg_ref[...]))
        e = _nn(p_ref[...].astype(BF16), wple_ref[...])
        gate_ref[...] = gate
        e_ref[...] = e
        h2_ref[...] = h1 + e * gate

    f32 = lambda cols: jax.ShapeDtypeStruct((T, cols), F32)
    bf = lambda cols: jax.ShapeDtypeStruct((T, cols), BF16)
    return _call(
        body, name=name, grid=(T // TM,),
        in_specs=[_rows(TM, ATTN_DIM), _rows(TM, CHUNK, 0), _rows(TM, CHUNK, 3), _rows(TM, CONV_DIM),
                  _rows(TM, D_MODEL), _rows(TM, PLE_DIM), _whole((ATTN_DIM, ATTN_DIM)),
                  _whole((1, ATTN_DIM)), _whole((1, CONV_DIM)), _whole((1, D_MODEL)),
                  _whole((CONV_DIM, CONV_DIM)), _whole((D_MODEL, D_MODEL)), _whole((D_MODEL, D_MODEL)),
                  _whole((PLE_DIM, D_MODEL))],
        out_specs=[_rows(TM, D_MODEL), _rows(TM, D_MODEL), _rows(TM, D_MODEL), _rows(TM, D_MODEL),
                   _rows(TM, D_MODEL), _rows(TM, D_MODEL), _rows(TM, CONV_DIM)],
        out_shape=[f32(D_MODEL), f32(D_MODEL), bf(D_MODEL), bf(D_MODEL), f32(D_MODEL), f32(D_MODEL),
                   f32(CONV_DIM)],
        args=(o, ug, ug, c2, h, p, head_mean, g_attn, g_conv, g_ple, w_pw, w_out, w_gate, w_ple,), ride=ride)


def _final_loss(h, target, gain, name):
    T = h.shape[0]

    def body(h_ref, t_ref, g_ref, dh_ref, gsum_ref, loss_ref):
        @pl.when(pl.program_id(0) == 0)
        def _():
            gsum_ref[...] = jnp.zeros_like(gsum_ref)
            loss_ref[...] = jnp.zeros_like(loss_ref)

        hv = h_ref[...]
        r = lax.rsqrt(jnp.mean(hv * hv, axis=-1, keepdims=True) + EPS)
        xh = hv * r
        diff = xh * g_ref[...] - t_ref[...]
        loss_ref[...] += 0.5 * jnp.sum(jnp.mean(diff * diff, axis=-1, keepdims=True), axis=0, keepdims=True)
        dy = diff * (1.0 / D_MODEL)
        gsum_ref[...] += jnp.sum(dy * xh, axis=0, keepdims=True)
        dxh = dy * g_ref[...]
        dh_ref[...] = r * (dxh - xh * jnp.mean(dxh * xh, axis=-1, keepdims=True))

    return pl.pallas_call(
        body, name=name, grid=(T // TM,),
        in_specs=[_rows(TM, D_MODEL), _rows(TM, D_MODEL), _whole((1, D_MODEL))],
        out_specs=[_rows(TM, D_MODEL), _whole((1, D_MODEL)), _whole((1, LANES))],
        out_shape=[jax.ShapeDtypeStruct((T, D_MODEL), F32), jax.ShapeDtypeStruct((1, D_MODEL), F32),
                   jax.ShapeDtypeStruct((1, LANES), F32)],
        compiler_params=_params(1),
    )(h, target, gain)


def _ple_out_bwd(dh2, gate, e, h1, g_ple, w_gate, w_out, name, ride=None):
    T = dh2.shape[0]

    def body(dh2_ref, gate_ref, e_ref, h1_ref, gpn_ref, wg_ref, wout_ref,
             dh1_ref, dh1b_ref, dzg_ref, de_ref, dycat_ref, gsum_ref):
        @pl.when(pl.program_id(0) == 0)
        def _():
            gsum_ref[...] = jnp.zeros_like(gsum_ref)

        dh2v = dh2_ref[...]
        gate = gate_ref[...]
        de_ref[...] = (dh2v * gate).astype(BF16)
        dzg = (dh2v * e_ref[...] * gate * (1.0 - gate)).astype(BF16)
        dzg_ref[...] = dzg
        dhn2 = _nt(dzg, wg_ref[...])
        h1 = h1_ref[...]
        r1 = lax.rsqrt(jnp.mean(h1 * h1, axis=-1, keepdims=True) + EPS)
        xh = h1 * r1
        gsum_ref[...] += jnp.sum(dhn2 * xh, axis=0, keepdims=True)
        dxh = dhn2 * gpn_ref[...]
        dh1 = dh2v + r1 * (dxh - xh * jnp.mean(dxh * xh, axis=-1, keepdims=True))
        dh1_ref[...] = dh1
        dh1b = dh1.astype(BF16)
        dh1b_ref[...] = dh1b
        dycat_ref[...] = _nt(dh1b, wout_ref[...])

    f32 = jax.ShapeDtypeStruct((T, D_MODEL), F32)
    bf = jax.ShapeDtypeStruct((T, D_MODEL), BF16)
    full = _rows(TM, D_MODEL)
    return _call(
        body, name=name, grid=(T // TM,),
        in_specs=[full, full, full, full, _whole((1, D_MODEL)), _whole((D_MODEL, D_MODEL)),
                  _whole((D_MODEL, D_MODEL))],
        out_specs=[full, full, full, full, full, _whole((1, D_MODEL))],
        out_shape=[f32, bf, bf, bf, f32, jax.ShapeDtypeStruct((1, D_MODEL), F32)],
        args=(dh2, gate, e, h1, g_ple, w_gate, w_out,), ride=ride)


def _branch_bwd(dycat, o, ug, c3, conv, head_mean, g_attn, g_conv, ln_g, ln_b, w_pw, name, ride=None):
    T = o.shape[0]

    def body(dya_ref, dyc_ref, o_ref, ga_ref, gc_ref, c3_ref, conv_ref, hm_ref, gao_ref, gco_ref,
             lng_ref, lnb_ref, wpw_ref,
             do_ref, dga_ref, dgc_ref, dc3_ref, dconv_ref, sums_ref):
        @pl.when(pl.program_id(0) == 0)
        def _():
            sums_ref[...] = jnp.zeros_like(sums_ref)

        hm = hm_ref[...]
        col = lambda x: jnp.sum(x, axis=0, keepdims=True)
        ov = o_ref[...]
        rh = lax.rsqrt(_nn((ov * ov).astype(BF16), hm) + EPS)
        xh = ov * rh
        ga = ga_ref[...]
        sg = _sigmoid(ga)
        dya = dya_ref[...]
        don = dya * (ga * sg)
        dga_ref[...] = (dya * xh * gao_ref[...] * _dsilu(ga, sg)).astype(BF16)
        sums_ref[0:1, :] += col(don * xh)
        dxh = don * gao_ref[...]
        do_ref[...] = (rh * (dxh - xh * _dot_hilo(dxh * xh, hm))).astype(BF16)
        c3 = c3_ref[...]
        rc = lax.rsqrt(jnp.mean(c3 * c3, axis=-1, keepdims=True) + EPS)
        xh3 = c3 * rc
        gc = gc_ref[...]
        sgc = _sigmoid(gc)
        dyc = dyc_ref[...]
        dn3 = dyc * (gc * sgc)
        dgc_ref[...] = (dyc * xh3 * gco_ref[...] * _dsilu(gc, sgc)).astype(BF16)
        sums_ref[1:2, :] += col(dn3 * xh3)
        dxh3 = dn3 * gco_ref[...]
        dc3 = (rc * (dxh3 - xh3 * jnp.mean(dxh3 * xh3, axis=-1, keepdims=True))).astype(BF16)
        dc3_ref[...] = dc3
        dc2 = _nt(dc3, wpw_ref[...])
        cv = conv_ref[...]
        mu = jnp.mean(cv, axis=-1, keepdims=True)
        xc = cv - mu
        rs = lax.rsqrt(jnp.mean(xc * xc, axis=-1, keepdims=True) + EPS)
        xn = xc * rs
        ln = xn * lng_ref[...] + lnb_ref[...]
        dln = dc2 * _dsilu(ln, _sigmoid(ln))
        sums_ref[2:3, :] += col(dln * xn)
        sums_ref[3:4, :] += col(dln)
        dxn = dln * lng_ref[...]
        dconv = rs * (dxn - jnp.mean(dxn, axis=-1, keepdims=True)
                      - xn * jnp.mean(dxn * xn, axis=-1, keepdims=True))
        dconv_ref[...] = dconv
        sums_ref[4:5, :] += col(dconv)

    half = lambda dt: jax.ShapeDtypeStruct((T, CHUNK), dt)
    tile = _rows(TM, CHUNK)
    vec = _whole((1, CHUNK))
    return _call(
        body, name=name, grid=(T // TM,),
        in_specs=[_rows(TM, CHUNK, 0), _rows(TM, CHUNK, 1), tile, _rows(TM, CHUNK, 0), _rows(TM, CHUNK, 3),
                  tile, tile, _whole((ATTN_DIM, ATTN_DIM)), vec, vec, vec, vec, _whole((CONV_DIM, CONV_DIM))],
        out_specs=[tile, tile, tile, tile, tile, _whole((8, CHUNK))],
        out_shape=[half(BF16), half(BF16), half(BF16), half(BF16), half(F32),
                   jax.ShapeDtypeStruct((8, CHUNK), F32)],
        args=(dycat, dycat, o, ug, ug, c3, conv, head_mean, g_attn, g_conv, ln_g, ln_b, w_pw,), ride=ride)


def _conv_bwd(dconv, ug, dw_w, name, ride=None):
    T = dconv.shape[0]
    per = TM // HALO
    last = T // HALO - 1
    n_tiles = T // TM

    def body(d_ref, dn_ref, cv_ref, cg_ref, cvh_ref, cgh_ref, w_ref, dcv_ref, dcg_ref, dw_ref,
             dpad_ref, cpad_ref, dsh_ref, csh_ref, dw_acc):
        i = pl.program_id(0)

        @pl.when(i == 0)
        def _():
            dw_acc[...] = jnp.zeros_like(dw_acc)

        tail = jnp.zeros((SUBLANES, CONV_DIM), F32)
        dpad_ref[0:TM, :] = d_ref[...]
        dpad_ref[TM:TM + HALO, :] = jnp.where(i == n_tiles - 1, 0.0, dn_ref[...])
        dpad_ref[TM + HALO:, :] = tail
        halo = cvh_ref[...] * _sigmoid(cgh_ref[...])
        cpad_ref[0:HALO, :] = jnp.where(i == 0, 0.0, halo)
        cpad_ref[HALO:HALO + TM, :] = cv_ref[...] * _sigmoid(cg_ref[...])
        cpad_ref[HALO + TM:, :] = tail
        _shifted_copies(dpad_ref, dsh_ref)
        _shifted_copies(cpad_ref, csh_ref)
        taps = [w_ref[t:t + 1, :] for t in range(CONV_WIDTH)]

        def rows(j, _):
            r = pl.multiple_of(j * CONV_ROWS, CONV_ROWS)
            d = d_ref[pl.ds(r, CONV_ROWS), :]
            dc = jnp.zeros((CONV_ROWS, CONV_DIM), F32)
            for t in range(CONV_WIDTH):
                b, a = _shift_of(CONV_WIDTH - 1 - t)
                dc = dc + taps[t] * dsh_ref[b, pl.ds(r + a, CONV_ROWS), :]
                b, a = _shift_of(HALO - (CONV_WIDTH - 1) + t)
                prod = d * csh_ref[b, pl.ds(r + a, CONV_ROWS), :]
                dw_acc[t] += jnp.sum(prod.reshape(CONV_ROWS // SUBLANES, SUBLANES, CONV_DIM), axis=0)
            cv = cv_ref[pl.ds(r, CONV_ROWS), :]
            sg = _sigmoid(cg_ref[pl.ds(r, CONV_ROWS), :])
            dcv_ref[pl.ds(r, CONV_ROWS), :] = (dc * sg).astype(BF16)
            dcg_ref[pl.ds(r, CONV_ROWS), :] = (dc * cv * sg * (1.0 - sg)).astype(BF16)
            return 0

        lax.fori_loop(0, TM // CONV_ROWS, rows, 0)

        @pl.when(i == n_tiles - 1)
        def _():
            dw_ref[...] = jnp.zeros_like(dw_ref)
            for t in range(CONV_WIDTH):
                dw_ref[t:t + 1, :] = jnp.sum(dw_acc[t], axis=0, keepdims=True)

    prev = lambda col: pl.BlockSpec((HALO, CHUNK), lambda i: (jnp.maximum(i * per - 1, 0), col))
    nxt = pl.BlockSpec((HALO, CONV_DIM), lambda i: (jnp.minimum((i + 1) * per, last), 0))
    half = jax.ShapeDtypeStruct((T, CHUNK), BF16)
    return _call(
        body, name=name, grid=(T // TM,),
        in_specs=[_rows(TM, CONV_DIM), nxt, _rows(TM, CHUNK, 1), _rows(TM, CHUNK, 2), prev(1), prev(2),
                  _whole((CONV_WIDTH, CONV_DIM))],
        out_specs=[_rows(TM, CHUNK), _rows(TM, CHUNK), _whole((HALO, CONV_DIM))],
        out_shape=[half, half, jax.ShapeDtypeStruct((HALO, CONV_DIM), F32)],
        scratch_shapes=[pltpu.VMEM((TM + HALO + SUBLANES, CONV_DIM), F32),
                        pltpu.VMEM((TM + HALO + SUBLANES, CONV_DIM), F32),
                        pltpu.VMEM((SUBLANES, TM + HALO, CONV_DIM), F32),
                        pltpu.VMEM((SUBLANES, TM + HALO, CONV_DIM), F32),
                        pltpu.VMEM((HALO, SUBLANES, CONV_DIM), F32)],
        args=(dconv, dconv, ug, ug, ug, ug, dw_w,), ride=ride)


def _attn_bwd(qs, k, v, do, cs, tri, tri_t, name, ride=None):
    T = qs.shape[0]
    nq = T // BLK
    width = LANES * ATT_COLS
    chains = [(c, half) for c in range(ATT_COLS) for half in range(2)]

    def body(q_ref, k_ref, v_ref, do_ref, cs_ref, m_ref, mt_ref, dq_ref, dk_ref, dv_ref, dk_acc, dv_acc):
        qi = pl.program_id(1)

        @pl.when(qi == 0)
        def _():
            dk_acc[...] = jnp.zeros_like(dk_acc)
            dv_acc[...] = jnp.zeros_like(dv_acc)

        lane = lax.broadcasted_iota(jnp.int32, (BLK, LANES), 1)
        first = lane < HEAD_DIM
        causal = (lax.broadcasted_iota(jnp.int32, (BLK, BLK), 1)
                  < lax.broadcasted_iota(jnp.int32, (BLK, BLK), 0))
        tri_m = m_ref[...]
        tri_mt = mt_ref[...]

        def halves(x):
            zero = jnp.zeros_like(x)
            return jnp.where(first, x, zero), jnp.where(first, zero, x)

        qh, doh, cs = {}, {}, []
        for c in range(ATT_COLS):
            qh[c, 0], qh[c, 1] = halves(q_ref[:, c * LANES:(c + 1) * LANES])
            doh[c, 0], doh[c, 1] = halves(do_ref[:, c * LANES:(c + 1) * LANES])
            cs.append(cs_ref[:, c * LANES:(c + 1) * LANES])

        def carry_of(kb, ch, rows=slice(None)):
            c, half = ch
            return jnp.sum(jnp.where(lane[rows] == kb + HEAD_DIM * half, cs[c][rows], 0.0), axis=1, keepdims=True)

        def step(kb, state, masked, rows=BLK):
            prefixes, dq_accs = state
            top = lambda full: full[:rows]
            put = (lambda new, full: new) if rows == BLK else (lambda new, full: jnp.concatenate([new, full[rows:]], 0))
            start = pl.multiple_of(kb * BLK, BLK)
            kblk = [k_ref[pl.ds(start, BLK), c * LANES:(c + 1) * LANES] for c in range(ATT_COLS)]
            vblk = [v_ref[pl.ds(start, BLK), c * LANES:(c + 1) * LANES] for c in range(ATT_COLS)]
            prefixes, dq_accs = list(prefixes), list(dq_accs)
            for g0 in range(0, len(chains), CHAIN_GROUP):
                ids = range(g0, g0 + CHAIN_GROUP)
                grp = [chains[n] for n in ids]
                z = [_nt(top(qh[ch]), kblk[ch[0]]) for ch in grp]
                da = [_nt(top(doh[ch]), vblk[ch[0]]) for ch in grp]
                parts = [_softplus_parts(zi) for zi in z]
                sp = [pt[1] for pt in parts]
                if masked:
                    sp = [jnp.where(top(causal), s, 0.0) for s in sp]
                incl = [_dot_hilo(s, tri_m) for s in sp]
                carries = [carry_of(kb, ch, slice(0, rows)) for ch in grp]
                a = [jnp.exp(zi - ii - ci) for zi, ii, ci in zip(z, incl, carries)]
                if masked:
                    a = [jnp.where(top(causal), ai, 0.0) for ai in a]
                w = [ai * di for ai, di in zip(a, da)]
                pinc = [_nn(wi.astype(BF16), tri_mt) for wi in w]
                dz = [wi - pt[0] * (pi + top(prefixes[n])) for n, wi, pt, pi in zip(ids, w, parts, pinc)]
                if masked:
                    dz = [jnp.where(top(causal), d, 0.0) for d in dz]
                for j in range(0, CHAIN_GROUP, 2):
                    c = grp[j][0]
                    k0, k1 = halves(kblk[c])
                    dz0, dz1 = dz[j].astype(BF16), dz[j + 1].astype(BF16)
                    a0, a1 = a[j].astype(BF16), a[j + 1].astype(BF16)
                    dq_accs[c] = put(top(dq_accs[c]) + _nn(dz0, k0) + _nn(dz1, k1), dq_accs[c])
                    dk_acc[pl.ds(start, BLK), c * LANES:(c + 1) * LANES] += (_tn(dz0, top(qh[c, 0]))
                                                                              + _tn(dz1, top(qh[c, 1])))
                    dv_acc[pl.ds(start, BLK), c * LANES:(c + 1) * LANES] += (_tn(a0, top(doh[c, 0]))
                                                                              + _tn(a1, top(doh[c, 1])))
                for n, pi in zip(ids, pinc):
                    prefixes[n] = put(top(prefixes[n]) + pi[:, BLK - 1:BLK], prefixes[n])
            return tuple(prefixes), tuple(dq_accs)

        def off_diagonal(kb, state):
            later = slice(LIVE_ROWS, None)
            least = jnp.min(functools.reduce(jnp.minimum, [carry_of(kb, ch, later) for ch in chains]))
            return lax.cond(least >= DEAD_AT, lambda s: step(kb, s, False, LIVE_ROWS),
                            lambda s: step(kb, s, False), state)

        state = (tuple(jnp.zeros((BLK, 1), F32) for _ in chains),
                 tuple(jnp.zeros((BLK, LANES), F32) for _ in range(ATT_COLS)))
        first_block = jnp.max(jnp.where(lane == FIRST_BLOCK_LANE, cs[0], 0.0)).astype(jnp.int32)
        state = lax.fori_loop(first_block, qi, off_diagonal, state)
        state = step(qi, state, True)
        for c in range(ATT_COLS):
            dq_ref[:, c * LANES:(c + 1) * LANES] = (state[1][c] * (HEAD_DIM ** -0.5)).astype(BF16)

        @pl.when(qi == nq - 1)
        def _():
            dk_ref[...] = dk_acc[...].astype(BF16)
            dv_ref[...] = dv_acc[...].astype(BF16)

    blk = pl.BlockSpec((BLK, width), lambda j, i: (i, j))
    col = pl.BlockSpec((T, width), lambda j, i: (0, j))
    out = jax.ShapeDtypeStruct((T, ATTN_DIM), BF16)
    return _call(
        body, name=name, grid=(ATTN_DIM // width, nq),
        in_specs=[blk, col, col, blk, blk, _whole((BLK, BLK)), _whole((BLK, BLK))],
        out_specs=[blk, col, col], out_shape=[out, out, out],
        scratch_shapes=[pltpu.VMEM((T, width), F32), pltpu.VMEM((T, width), F32)],
        args=(qs, k, v, do, cs, tri, tri_t,), ride=ride)


def _inproj_bwd(du, w_in_t, h, dh1, gain, name, ride=None):
    T = h.shape[0]

    def body(*refs):
        du_refs = refs[:N_CHUNK]
        w_ref, h_ref, dh1_ref, g_ref, dh_ref, gsum_ref = refs[N_CHUNK:]

        @pl.when(pl.program_id(0) == 0)
        def _():
            gsum_ref[...] = jnp.zeros_like(gsum_ref)

        dhn = jnp.zeros((TM, D_MODEL), F32)
        for j in range(N_CHUNK):
            dhn = dhn + _nn(du_refs[j][...], w_ref[j * CHUNK:(j + 1) * CHUNK, :])
        hv = h_ref[...]
        r = lax.rsqrt(jnp.mean(hv * hv, axis=-1, keepdims=True) + EPS)
        xh = hv * r
        gsum_ref[...] += jnp.sum(dhn * xh, axis=0, keepdims=True)
        dxh = dhn * g_ref[...]
        dh_ref[...] = dh1_ref[...] + r * (dxh - xh * jnp.mean(dxh * xh, axis=-1, keepdims=True))

    full = _rows(TM, D_MODEL)
    return _call(
        body, name=name, grid=(T // TM,),
        in_specs=[_rows(TM, CHUNK)] * N_CHUNK + [_whole((N_CHUNK * CHUNK, D_MODEL)), full, full,
                                                 _whole((1, D_MODEL))],
        out_specs=[full, _whole((1, D_MODEL))],
        out_shape=[jax.ShapeDtypeStruct((T, D_MODEL), F32), jax.ShapeDtypeStruct((1, D_MODEL), F32)],
        args=(*du, w_in_t, h, dh1, gain), ride=ride)


def _weight_grad(lhs_list, rhs, name, tk=CHUNK, ride=None):
    T, n_rhs = rhs.shape
    n = len(lhs_list)
    ka = lhs_list[0].shape[1]
    per = ka // tk

    def body(*refs):
        a_refs, b_ref, out_ref = refs[:n], refs[n], refs[n + 1]
        step = pl.program_id(0)
        for j in range(n):
            for s in range(per):
                @pl.when(step == j * per + s)
                def _(j=j, s=s):
                    out_ref[...] = _tn(a_refs[j][:, s * tk:(s + 1) * tk], b_ref[...]).astype(BF16)

    (grad,), landed = _call(
        body, name=name, grid=(n * per,),
        in_specs=[_whole((T, ka))] * n + [_whole((T, n_rhs))],
        out_specs=[pl.BlockSpec((tk, n_rhs), lambda i: (i, 0))],
        out_shape=[jax.ShapeDtypeStruct((n * ka, n_rhs), BF16)],
        args=(*lhs_list, rhs), ride=ride)
    return grad, landed


def _adamw_update(w, g, m, v):
    nm = ADAM_B1 * m + (1.0 - ADAM_B1) * g
    nv = ADAM_B2 * v + (1.0 - ADAM_B2) * (g * g)
    m_hat = nm / (1.0 - ADAM_B1 ** ADAM_STEP)
    v_hat = nv / (1.0 - ADAM_B2 ** ADAM_STEP)
    return -ADAM_LR * (m_hat / (jnp.sqrt(v_hat) + ADAM_EPS) + ADAM_WD * w), nm, nv


def _sum_adamw(slots, w, m, v, name):
    depth, R, C = w.shape
    tr = next(rows for rows in ADAMW_ROWS if R % rows == 0)

    def body(*refs):
        slot_refs, (w_ref, m_ref, v_ref, g_ref, d_ref, nm_ref, nv_ref) = refs[:depth], refs[depth:]
        for layer in range(depth):
            @pl.when(pl.program_id(0) == layer)
            def _(src=slot_refs[layer]):
                g = src[0].astype(F32)
                for s in range(1, src.shape[0]):
                    g = g + src[s].astype(F32)
                g_ref[0] = g
                d_ref[0], nm_ref[0], nv_ref[0] = _adamw_update(w_ref[0], g, m_ref[0], v_ref[0])

    slot_spec = lambda layer: pl.BlockSpec((slots[layer].shape[0], tr, C),
                                           lambda l, i: (0, jnp.where(l == layer, i, 0), 0))
    spec = pl.BlockSpec((1, tr, C), lambda l, i: (l, i, 0))
    out = jax.ShapeDtypeStruct((depth, R, C), F32)
    return pl.pallas_call(
        body, name=name, grid=(depth, R // tr),
        in_specs=[slot_spec(layer) for layer in range(depth)] + [spec] * 3,
        out_specs=[spec] * 4, out_shape=[out] * 4,
        compiler_params=_params(2),
    )(*slots, w, m, v)


def _adamw(w, g, m, v, name):
    R, C = w.shape
    tr = R
    for cand in (512, 256, 128, 64):
        if R % cand == 0 and R > cand:
            tr = cand
            break

    def body(w_ref, g_ref, m_ref, v_ref, d_ref, nm_ref, nv_ref):
        d_ref[...], nm_ref[...], nv_ref[...] = _adamw_update(w_ref[...], g_ref[...], m_ref[...], v_ref[...])

    spec = pl.BlockSpec((tr, C), lambda i: (i, 0))
    out = jax.ShapeDtypeStruct((R, C), F32)
    return pl.pallas_call(
        body, name=name, grid=(R // tr,),
        in_specs=[spec] * 4, out_specs=[spec] * 3, out_shape=[out, out, out],
        compiler_params=_params(1),
    )(w, g, m, v)


def _pack_small(values, scalar=None):
    pad = lambda a: jnp.pad(a, ((0, 0), (0, D_MODEL - a.shape[1])))
    last = jnp.zeros((1, D_MODEL), F32) if scalar is None else pad(scalar.reshape(1, 1))
    return jnp.concatenate([pad(values[name].reshape(rows, cols)) for name, _, rows, cols in SMALL_LAYOUT] + [last],
                           axis=0)


def _small_update(all_packs, state, name):
    n = len(SMALL_LAYOUT)

    def body(packs_ref, *refs):
        ins, outs = refs[:3 * n], refs[3 * n:]
        total = packs_ref[0]
        for s in range(1, N_DEV):
            total = total + packs_ref[s]
        for j, (_, at, rows, cols) in enumerate(SMALL_LAYOUT):
            g = total[at:at + rows, :cols]
            w_ref, m_ref, v_ref = ins[3 * j:3 * j + 3]
            outs[4 * j][...] = g
            outs[4 * j + 1][...], outs[4 * j + 2][...], outs[4 * j + 3][...] = _adamw_update(
                w_ref[...], g, m_ref[...], v_ref[...])
        outs[-2][...] = total[LOSS_ROW:LOSS_ROW + 1, :LANES]
        outs[-1][...] = total[SMALL_ROWS:, :]

    shapes = [jax.ShapeDtypeStruct((rows, cols), F32) for _, _, rows, cols in SMALL_LAYOUT for _ in range(4)]
    shapes += [jax.ShapeDtypeStruct((1, LANES), F32), jax.ShapeDtypeStruct((PACK_ROWS - SMALL_ROWS, D_MODEL), F32)]
    operands = [a for item in SMALL_LAYOUT for a in state[item[0]]]
    res = pl.pallas_call(body, name=name, out_shape=shapes, compiler_params=_params())(all_packs, *operands)
    per_name = {item[0]: tuple(res[4 * j:4 * j + 4]) for j, item in enumerate(SMALL_LAYOUT)}
    return per_name, res[-2][0, 0], res[-1]


def kernel(x, p, norm_g, w_in, attn_out_g, dw_w, dw_b, conv_ln_g, conv_ln_b, w_pw, conv_out_g, w_out, ple_norm_g, w_ple_gate, w_ple, final_g, loss_target, m_norm_g, m_w_in, m_attn_out_g, m_dw_w, m_dw_b, m_conv_ln_g, m_conv_ln_b, m_w_pw, m_conv_out_g, m_w_out, m_ple_norm_g, m_w_ple_gate, m_w_ple, m_final_g, v_norm_g, v_w_in, v_attn_out_g, v_dw_w, v_dw_b, v_conv_ln_g, v_conv_ln_b, v_w_pw, v_conv_out_g, v_w_out, v_ple_norm_g, v_w_ple_gate, v_w_ple, v_final_g):
    depth = w_in.shape[0]
    T = x.shape[1]
    given = dict(
        norm_g=norm_g, ple_norm_g=ple_norm_g, final_g=final_g, dw_b=dw_b, conv_ln_g=conv_ln_g, conv_ln_b=conv_ln_b,
        conv_out_g=conv_out_g, attn_out_g=attn_out_g,
        m_norm_g=m_norm_g, m_ple_norm_g=m_ple_norm_g, m_final_g=m_final_g, m_dw_b=m_dw_b, m_conv_ln_g=m_conv_ln_g,
        m_conv_ln_b=m_conv_ln_b, m_conv_out_g=m_conv_out_g, m_attn_out_g=m_attn_out_g,
        v_norm_g=v_norm_g, v_ple_norm_g=v_ple_norm_g, v_final_g=v_final_g, v_dw_b=v_dw_b, v_conv_ln_g=v_conv_ln_g,
        v_conv_ln_b=v_conv_ln_b, v_conv_out_g=v_conv_out_g, v_attn_out_g=v_attn_out_g)
    my_idx = 4 * lax.axis_index("x") + 2 * lax.axis_index("y") + lax.axis_index("c")

    ids = jnp.arange(BLK)
    tri = (ids[:, None] >= ids[None, :]).astype(BF16)
    tri_t = (ids[:, None] <= ids[None, :]).astype(BF16)
    hid = jnp.arange(ATTN_DIM) // HEAD_DIM
    head_mean = ((hid[:, None] == hid[None, :]).astype(F32) / HEAD_DIM).astype(BF16)

    w_names = ("w_in_t", "w_pw", "w_out", "w_gate", "w_ple")
    w_axes = dict(zip(w_names, (0, 0, 0, 0, 1)))
    shards = [dict(zip(w_names, (w_in[l].T.astype(BF16), w_pw[l].astype(BF16), w_out[l].astype(BF16),
                                 w_ple_gate[l].astype(BF16), w_ple[l].astype(BF16)))) for l in range(depth)]
    first = _all_gather([shards[0]["w_in_t"]] + [dw_w[l].T for l in range(depth)], [0] * (1 + depth),
                        "gather_weights_0")
    layers = []
    for l in range(depth):
        layers.append(dict(
            dw_w=first[1 + l].T,
            g_norm=norm_g[l][None], g_attn=jnp.tile(attn_out_g[l], N_HEADS)[None], dw_b=dw_b[l][None],
            ln_g=conv_ln_g[l][None], ln_b=conv_ln_b[l][None], g_conv=conv_out_g[l][None],
            g_ple=ple_norm_g[l][None], p=p[l, 0]))
    layers[0]["w_in_t"] = first[0]

    def rest_of(l, names):
        return [_Ride.gather2(shards[l][n], w_axes[n]) for n in names]

    h = x[0]
    saved = []
    for l, w in enumerate(layers):
        early, late = (w_names[3:], w_names[1:3]) if l == 0 else ((), ())
        (qs, k, v, ug, hn), landed = _prenorm_inproj(h, w["g_norm"], w["w_in_t"], f"inproj_{l}",
                                                     _Ride(rest_of(l, early)))
        w.update(zip(early, landed))
        ahead = [_Ride.gather2(shards[l + 1]["w_in_t"], 0, None, 0, W_IN_ROWS_ON_ATTN)] if l + 1 < depth else []
        own = w_names[1:] if l > 0 else ()
        (o, cs), landed = _attn_fwd(qs, k, v, tri, f"attn_fwd_{l}", _Ride(ahead + rest_of(l, own)))
        w_in_next = landed[:len(ahead)]
        w.update(zip(own, landed[len(ahead):]))
        (conv, c2), landed = _conv_fwd(ug, w["dw_w"], w["dw_b"], w["ln_g"], w["ln_b"], f"conv_fwd_{l}",
                                       _Ride(rest_of(l, late)))
        w.update(zip(late, landed))
        tail = [_Ride.gather2(shards[l + 1]["w_in_t"], 0, a, W_IN_ROWS_ON_ATTN,
                              shards[l + 1]["w_in_t"].shape[0] - W_IN_ROWS_ON_ATTN) for a in w_in_next]
        (h2, h1, ycat, hn2, gate, e, c3), landed = _mix_out_ple(
            o, ug, c2, h, w["p"], head_mean, w["g_attn"], w["g_conv"], w["g_ple"],
            w["w_pw"], w["w_out"], w["w_gate"], w["w_ple"], f"mix_{l}", _Ride(tail))
        if landed:
            layers[l + 1]["w_in_t"] = landed[0]
        saved.append(dict(h=h, qs=qs, k=k, v=v, ug=ug, hn=hn, o=o, cs=cs, conv=conv, c2=c2, h1=h1,
                          ycat=ycat, hn2=hn2, gate=gate, e=e, c3=c3))
        h = h2
    dh, g_final, loss_part = _final_loss(h, loss_target[0], final_g[None], "final_loss")

    small = {}
    dww_parts = [None] * depth
    slots = [dict() for _ in range(depth)]
    g_w_in = None
    for l in reversed(range(depth)):
        w, s = layers[l], saved[l]
        above = [None] if g_w_in is not None else []

        def part(i, above=above, g=g_w_in):
            return [_Ride.scatter(g, 0, above[0], *W_IN_GRAD_PARTS[i])] if above else []

        def scattered(grads, names):
            return [_Ride.scatter(grads[n], w_axes[n]) for n in names]

        (dh1, dh1b, dzg, de, dycat, g_ple_sum), landed = _ple_out_bwd(
            dh, s["gate"], s["e"], s["h1"], w["g_ple"], w["w_gate"], w["w_out"], f"ple_bwd_{l}", _Ride(part(0)))
        above[:1] = landed
        (do, dga, dgc, dc3, dconv, sums), landed = _branch_bwd(
            dycat, s["o"], s["ug"], s["c3"], s["conv"], head_mean, w["g_attn"], w["g_conv"],
            w["ln_g"], w["ln_b"], w["w_pw"], f"branch_bwd_{l}", _Ride(part(1)))
        above[:1] = landed
        grads = dict(
            w_pw=_weight_grad([s["c2"]], dc3, f"grad_w_pw_{l}")[0],
            w_out=_weight_grad([s["ycat"]], dh1b, f"grad_w_out_{l}")[0],
            w_gate=_weight_grad([s["hn2"]], dzg, f"grad_w_gate_{l}")[0],
            w_ple=_weight_grad([w["p"].astype(BF16)], de, f"grad_w_ple_{l}", tk=PLE_DIM)[0])
        (dcv, dcg, dww), landed = _conv_bwd(dconv, s["ug"], w["dw_w"], f"conv_bwd_{l}", _Ride(part(2)))
        above[:1] = landed
        (dq, dk, dv), landed = _attn_bwd(s["qs"], s["k"], s["v"], do, s["cs"], tri, tri_t, f"attn_bwd_{l}",
                                         _Ride(scattered(grads, w_names[1:])))
        slots[l].update(zip(w_names[1:], landed))
        du = [dq, dk, dv, dga, dcv, dcg, dgc]
        g_w_in_here, landed = _weight_grad(du, s["hn"], f"grad_w_in_{l}", ride=_Ride(part(3)))
        if above:
            slots[l + 1]["w_in_t"] = landed[0]
        tail = [_Ride.scatter_chips(_pair_reduce(g_w_in_here, f"pair_reduce_w_in_{l}"))] if l == 0 else []
        (dh, g_norm_sum), landed = _inproj_bwd(du, w["w_in_t"], s["h"], dh1, w["g_norm"], f"inproj_bwd_{l}",
                                               _Ride(tail))
        slots[l].update(zip(("w_in_t",), landed))
        g_w_in = g_w_in_here
        small[l] = dict(norm_g=g_norm_sum, ple_norm_g=g_ple_sum, attn_out_g=sums[0].reshape(N_HEADS, HEAD_DIM).sum(0),
                        conv_out_g=sums[1], conv_ln_g=sums[2], conv_ln_b=sums[3], dw_b=sums[4])
        dww_parts[l] = dww[:CONV_WIDTH]
    slots = [[sl[n] for n in w_names] for sl in slots]
    grad_x = dh[None]

    sums_of = {name: jnp.stack([small[l][name].reshape(-1) for l in range(depth)]) for name in small[0]}
    sums_of["final_g"] = g_final
    pack = jnp.concatenate([_pack_small(sums_of, scalar=loss_part[0, 0]), jnp.concatenate(dww_parts, axis=1),
                            jnp.zeros((PACK_ROWS - SMALL_ROWS - CONV_WIDTH, D_MODEL), F32)], axis=0)
    (all_packs,) = _all_gather([pack], [0], "gather_small_grads")
    state = {name: [given[pre + name].reshape(rows, cols) for pre in ("", "m_", "v_")]
             for name, _, rows, cols in SMALL_LAYOUT}
    updated, loss, dww_sum = _small_update(all_packs.reshape(N_DEV, PACK_ROWS, D_MODEL), state, "update_small")
    res = {kind: {name: val[k].reshape(given[name].shape) for name, val in updated.items()}
           for k, kind in enumerate("gdmv")}
    dww_full = dww_sum[:CONV_WIDTH].reshape(CONV_WIDTH, depth, CONV_DIM).transpose(1, 0, 2)
    g_dw_w = lax.dynamic_slice_in_dim(dww_full, my_idx * (CONV_DIM // N_DEV), CONV_DIM // N_DEV, axis=2)

    swap = lambda a: a.transpose(0, 2, 1)
    state = {"w_in": (w_in, m_w_in, v_w_in), "w_pw": (w_pw, m_w_pw, v_w_pw), "w_out": (w_out, m_w_out, v_w_out),
             "w_ple_gate": (w_ple_gate, m_w_ple_gate, v_w_ple_gate), "w_ple": (w_ple, m_w_ple, v_w_ple)}
    for at, name in enumerate(state):
        wv, mv, vv = [swap(a) for a in state[name]] if name == "w_in" else state[name]
        out = _sum_adamw([slots[l][at] for l in range(depth)], wv, mv, vv, f"adamw_{name}")
        out = [swap(a) for a in out] if name == "w_in" else out
        res["g"][name], res["d"][name], res["m"][name], res["v"][name] = out
    flat = lambda a: a.reshape(-1, a.shape[-1])
    res["g"]["dw_w"] = g_dw_w
    res["d"]["dw_w"], res["m"]["dw_w"], res["v"]["dw_w"] = [
        a.reshape(dw_w.shape) for a in _adamw(flat(dw_w), flat(g_dw_w), flat(m_dw_w), flat(v_dw_w), "adamw_dw_w")]

    order = ["norm_g", "w_in", "attn_out_g", "dw_w", "dw_b", "conv_ln_g", "conv_ln_b", "w_pw", "conv_out_g",
             "w_out", "ple_norm_g", "w_ple_gate", "w_ple", "final_g"]
    return (loss, grad_x, *[res["g"][n] for n in order], *[res["d"][n] for n in order],
            *[res["m"][n] for n in order], *[res["v"][n] for n in order])
```

```python
import functools

import jax
import jax.numpy as jnp
from jax import lax
from jax.experimental import pallas as pl
from jax.experimental.pallas import tpu as pltpu

F32 = jnp.float32
BF16 = jnp.bfloat16
MESH = pl.DeviceIdType.MESH

N_DEV = 8
D_MODEL = 1024
ATTN_DIM = 512
CONV_DIM = 512
HEAD_DIM = 64
N_HEADS = 8
CONV_WIDTH = 31
PLE_DIM = 256
CHUNK = 512
N_CHUNK = 7
EPS = 1e-6
ADAM_LR = 0.001
ADAM_B1 = 0.9
ADAM_B2 = 0.999
ADAM_EPS = 1e-08
ADAM_WD = 0.01
ADAM_STEP = 10

LANES = 128
BLK = 256
ATT_COLS = 4
CHAIN_GROUP = 4
SOFTPLUS_LINEAR_AT = 20.0
DEAD_AT = 110.0
FIRST_BLOCK_LANE = HEAD_DIM - 1
TM = 512
HALO = 32
SUBLANES = 8
CONV_ROWS = 32
ADAMW_ROWS = (224, 128, 64)
VMEM_LIMIT = 56 * 1024 * 1024
SMALL_ROWS = 16
SMALL_LAYOUT = (("norm_g", 0, 2, D_MODEL), ("ple_norm_g", 2, 2, D_MODEL), ("final_g", 4, 1, D_MODEL),
                ("dw_b", 5, 2, CONV_DIM), ("conv_ln_g", 7, 2, CONV_DIM), ("conv_ln_b", 9, 2, CONV_DIM),
                ("conv_out_g", 11, 2, CONV_DIM), ("attn_out_g", 13, 2, HEAD_DIM))
LOSS_ROW = 15
W_IN_ROWS_ON_ATTN = 288
W_IN_GRAD_PARTS = ((0, 96), (96, 80), (176, 144), (320, 128))
PACK_ROWS = 48


def _nn(a, b):
    return lax.dot_general(a, b, (((1,), (0,)), ((), ())), preferred_element_type=F32)


def _nt(a, b):
    return lax.dot_general(a, b, (((1,), (1,)), ((), ())), preferred_element_type=F32)


def _tn(a, b):
    return lax.dot_general(a, b, (((0,), (0,)), ((), ())), preferred_element_type=F32)


def _split(x):
    hi = x.astype(BF16)
    lo = (x - hi.astype(F32)).astype(BF16)
    return hi, lo


def _dot_hilo(x, m):
    hi, lo = _split(x)
    return _nn(hi, m) + _nn(lo, m)


def _sigmoid(x):
    return jax.nn.sigmoid(x)


def _dsilu(x, s):
    return s * (1.0 + x * (1.0 - s))


def _params(n_grid=0, vmem=VMEM_LIMIT):
    sem = ("arbitrary",) * n_grid if n_grid else None
    return pltpu.CompilerParams(dimension_semantics=sem, vmem_limit_bytes=vmem)


def _rows(tm, cols, col=0):
    return pl.BlockSpec((tm, cols), lambda i: (i, col))


def _whole(shape):
    zeros = (0,) * len(shape)
    return pl.BlockSpec(shape, lambda *_: zeros)


def _my_position():
    return lax.axis_index("x"), lax.axis_index("y"), lax.axis_index("c")


def _block(ref, axis, idx, size):
    start = pl.multiple_of(idx * size, size)
    if axis == 0:
        return ref.at[pl.ds(start, size), :]
    return ref.at[:, pl.ds(start, size)]


def _all_gather(shards, axes, name):
    n = len(shards)
    sizes = [s.shape[a] for s, a in zip(shards, axes)]

    def full_shape(s, a):
        shape = list(s.shape)
        shape[a] *= N_DEV
        return jax.ShapeDtypeStruct(tuple(shape), s.dtype)

    def body(*refs):
        ins, outs = refs[:n], refs[n:2 * n]
        send_sems, recv_sems, local_sems = refs[2 * n:]
        x, y, c = _my_position()
        me, sibling = (x, y, c), (x, y, 1 - c)
        chips = [(1 - x, y), (x, 1 - y), (1 - x, 1 - y)]

        def place(i, dev):
            return _block(outs[i], axes[i], 4 * dev[0] + 2 * dev[1] + dev[2], sizes[i])

        def copy(k, i, dev, to, src=None):
            return pltpu.make_async_remote_copy(
                src_ref=place(i, dev) if src is None else src, dst_ref=place(i, dev),
                send_sem=send_sems.at[k, i], recv_sem=recv_sems.at[k, i],
                device_id=to, device_id_type=MESH)

        mine = [pltpu.make_async_copy(ins[i], place(i, me), local_sems.at[i]) for i in range(n)]
        for cp in mine:
            cp.start()
        first = [copy(0, i, me, sibling, src=ins[i]) for i in range(n)]
        for j, chip in enumerate(chips):
            first += [copy(1 + j, i, me, (*chip, c), src=ins[i]) for i in range(n)]
        for cp in first:
            cp.start()
        passed = []
        for j, chip in enumerate(chips):
            for i in range(n):
                copy(1 + j, i, (*chip, c), me).wait_recv()
            hop = [copy(4 + j, i, (*chip, c), sibling) for i in range(n)]
            for cp in hop:
                cp.start()
            passed += hop
        for i in range(n):
            copy(0, i, sibling, me).wait_recv()
        for j, chip in enumerate(chips):
            for i in range(n):
                copy(4 + j, i, (*chip, 1 - c), me).wait_recv()
        for cp in first + passed:
            cp.wait_send()
        for cp in mine:
            cp.wait()

    any_spec = pl.BlockSpec(memory_space=pl.ANY)
    return pl.pallas_call(
        body, name=name,
        out_shape=[full_shape(s, a) for s, a in zip(shards, axes)],
        in_specs=[any_spec] * n, out_specs=[any_spec] * n,
        scratch_shapes=[pltpu.SemaphoreType.DMA((7, n)), pltpu.SemaphoreType.DMA((7, n)),
                        pltpu.SemaphoreType.DMA((n,))],
    )(*shards)


def _pair_reduce(g, name):
    n_chips = N_DEV // 2
    R, C = g.shape[0] // N_DEV, g.shape[1]

    def body(g_ref, out_ref, mine_ref, theirs_ref, send_sems, recv_sems, local_sems):
        x, y, c = _my_position()
        block = lambda d: g_ref.at[pl.ds(pl.multiple_of(d * R, 16), R), :]
        sends = [pltpu.make_async_remote_copy(
            src_ref=block(2 * j + 1 - c), dst_ref=theirs_ref.at[j], send_sem=send_sems.at[j],
            recv_sem=recv_sems.at[j], device_id=(x, y, 1 - c), device_id_type=MESH) for j in range(n_chips)]
        own = [pltpu.make_async_copy(block(2 * j + c), mine_ref.at[j], local_sems.at[j]) for j in range(n_chips)]
        for cp in sends + own:
            cp.start()
        for j in range(n_chips):
            own[j].wait()
            sends[j].wait_recv()
            out_ref[j] = (mine_ref[j].astype(F32) + theirs_ref[j].astype(F32)).astype(g.dtype)
        for cp in sends:
            cp.wait_send()

    half = pltpu.VMEM((n_chips, R, C), g.dtype)
    sems = pltpu.SemaphoreType.DMA((n_chips,))
    return pl.pallas_call(
        body, name=name, out_shape=jax.ShapeDtypeStruct((n_chips, R, C), g.dtype),
        in_specs=[pl.BlockSpec(memory_space=pl.ANY)], out_specs=pl.BlockSpec(memory_space=pltpu.VMEM),
        scratch_shapes=[half, half, sems, sems, sems], compiler_params=_params(),
    )(g)


class _Ride:
    def __init__(self, parts):
        self.parts = [p for p in parts if p is not None]

    @staticmethod
    def gather(src, axis, land=None, lo=0, n=None):
        return ("gather", src, land, axis, lo, src.shape[axis] if n is None else n)

    @staticmethod
    def gather2(src, axis, land=None, lo=0, n=None):
        return ("gather2", src, land, axis, lo, src.shape[axis] if n is None else n)

    @staticmethod
    def scatter(src, axis, land=None, lo=0, n=None):
        return ("scatter", src, land, axis, lo, src.shape[axis] // N_DEV if n is None else n)

    @staticmethod
    def scatter_chips(chip_sums):
        return ("scatter_chips", chip_sums, None, 0, 0, chip_sums.shape[1])

    def arrays(self):
        return [p[1] for p in self.parts] + [p[2] for p in self.parts if p[2] is not None]

    def out_shapes(self):
        out = []
        for kind, src, _, axis, _, _ in self.parts:
            shape = list(src.shape)
            if kind in ("gather", "gather2"):
                shape[axis] *= N_DEV
            elif kind == "scatter_chips":
                pass
            else:
                shape[axis] //= N_DEV
                shape = [N_DEV] + shape
            out.append(jax.ShapeDtypeStruct(tuple(shape), src.dtype))
        return out

    def aliases(self, n_in, n_out):
        m, out = len(self.parts), {}
        for j, p in enumerate(self.parts):
            if p[2] is not None:
                out[n_in + m + len(out)] = n_out + j
        return out

    def scratch(self):
        m = len(self.parts)
        return [pltpu.SemaphoreType.DMA((N_DEV - 1, m)), pltpu.SemaphoreType.DMA((N_DEV - 1, m)),
                pltpu.SemaphoreType.DMA((m,))]

    def _copies(self, src_refs, land_refs, sems):
        send_sems, recv_sems, local_sems = sems
        x, y, c = _my_position()
        my_idx = 4 * x + 2 * y + c
        own, sends, relays, lands = [], [], [], []
        for j, (kind, src, _, axis, lo, n) in enumerate(self.parts):
            if kind == "scatter_chips":
                for k in (0, 2, 4, 6):
                    px, py = (1 - x if k & 4 else x), (1 - y if k & 2 else y)
                    a, b = src_refs[j].at[2 * px + py], land_refs[j].at[2 * x + y]
                    if k == 0:
                        own.append(pltpu.make_async_copy(a, b, local_sems.at[j]))
                        continue
                    mk = lambda dst, a=a, k=k, j=j, to=(px, py, c): pltpu.make_async_remote_copy(
                        src_ref=a, dst_ref=dst, send_sem=send_sems.at[k - 1, j], recv_sem=recv_sems.at[k - 1, j],
                        device_id=to, device_id_type=MESH)
                    sends.append(mk(b))
                    lands.append(mk(land_refs[j].at[2 * px + py]))
                continue
            size = src.shape[axis] if kind in ("gather", "gather2") else src.shape[axis] // N_DEV
            align = 16 if axis == 0 else LANES

            def rows(ref, idx, lead=None, axis=axis, lo=lo, n=n, size=size, align=align):
                at = pl.ds(pl.multiple_of(idx * size + lo, align), n)
                where = (at, slice(None)) if axis == 0 else (slice(None), at)
                return ref.at[where] if lead is None else ref.at[(lead, *where)]

            def in_shard(ref):
                return rows(ref, 0)

            def in_slot(ref, s):
                return rows(ref, 0, lead=s)

            if kind == "gather2":
                chips = [(1 - x, y), (x, 1 - y), (1 - x, 1 - y)]
                place = lambda px, py, pc: rows(land_refs[j], 4 * px + 2 * py + pc)

                def copy(i, a, dst, to, j=j):
                    return pltpu.make_async_remote_copy(
                        src_ref=a, dst_ref=dst, send_sem=send_sems.at[i, j], recv_sem=recv_sems.at[i, j],
                        device_id=to, device_id_type=MESH)

                mine = in_shard(src_refs[j])
                own.append(pltpu.make_async_copy(mine, place(x, y, c), local_sems.at[j]))
                sends.append(copy(0, mine, place(x, y, c), (x, y, 1 - c)))
                lands.append(copy(0, mine, place(x, y, 1 - c), (x, y, 1 - c)))
                for i, (px, py) in enumerate(chips):
                    sends.append(copy(1 + i, mine, place(x, y, c), (px, py, c)))
                    relays.append((copy(1 + i, mine, place(px, py, c), (px, py, c)),
                                   copy(4 + i, place(px, py, c), place(px, py, c), (x, y, 1 - c))))
                    lands.append(copy(4 + i, mine, place(px, py, 1 - c), (x, y, 1 - c)))
                continue
            for k in range(N_DEV):
                px = 1 - x if k & 4 else x
                py = 1 - y if k & 2 else y
                pc = 1 - c if k & 1 else c
                peer_idx = 4 * px + 2 * py + pc
                if kind == "gather":
                    a, b, landed = in_shard(src_refs[j]), rows(land_refs[j], my_idx), rows(land_refs[j], peer_idx)
                else:
                    a, b, landed = rows(src_refs[j], peer_idx), in_slot(land_refs[j], my_idx), in_slot(land_refs[j], peer_idx)
                if k == 0:
                    own.append(pltpu.make_async_copy(a, b, local_sems.at[j]))
                    continue
                mk = lambda dst, a=a, k=k, j=j, to=(px, py, pc): pltpu.make_async_remote_copy(
                    src_ref=a, dst_ref=dst, send_sem=send_sems.at[k - 1, j], recv_sem=recv_sems.at[k - 1, j],
                    device_id=to, device_id_type=MESH)
                sends.append(mk(b))
                lands.append(mk(landed))
        return own, sends, relays, lands

    @property
    def relayed(self):
        return any(p[0] == "gather2" for p in self.parts)

    def start(self, src_refs, land_refs, sems):
        own, sends, _, _ = self._copies(src_refs, land_refs, sems)
        for cp in own + sends:
            cp.start()

    def relay(self, src_refs, land_refs, sems):
        for arrival, onward in self._copies(src_refs, land_refs, sems)[2]:
            arrival.wait_recv()
            onward.start()

    def wait(self, src_refs, land_refs, sems):
        own, sends, relays, lands = self._copies(src_refs, land_refs, sems)
        for cp in lands:
            cp.wait_recv()
        for cp in sends + [onward for _, onward in relays]:
            cp.wait_send()
        for cp in own:
            cp.wait()


def _call(body, *, name, grid, in_specs, out_specs, out_shape, args, scratch_shapes=(), ride=None):
    in_specs, out_specs, out_shape = list(in_specs), list(out_specs), list(out_shape)
    n_in, n_out, n_sc = len(in_specs), len(out_specs), len(scratch_shapes)
    if ride is None or not ride.parts:
        res = pl.pallas_call(body, name=name, grid=grid, in_specs=in_specs, out_specs=out_specs,
                             out_shape=out_shape, scratch_shapes=list(scratch_shapes),
                             compiler_params=_params(len(grid)))(*args)
        return list(res), []
    extra, m = ride.arrays(), len(ride.parts)

    def riding(*refs):
        a = n_in + len(extra)
        b = a + n_out
        srcs, lands, sems = refs[n_in:n_in + m], refs[b:b + m], refs[b + m + n_sc:]
        at = [pl.program_id(d) for d in range(len(grid))]

        @pl.when(functools.reduce(jnp.logical_and, [i == 0 for i in at]))
        def _():
            ride.start(srcs, lands, sems)

        if ride.relayed:
            step, n_steps = at[0], 1
            for i, g in zip(at[1:], grid[1:]):
                step = step * g + i
            for g in grid:
                n_steps *= g
            assert n_steps >= 2, "a two-level ride needs a grid step after the first"

            @pl.when(step == n_steps - 1)
            def _():
                ride.relay(srcs, lands, sems)

        body(*refs[:n_in], *refs[a:b], *refs[b + m:b + m + n_sc])

        @pl.when(functools.reduce(jnp.logical_and, [i == g - 1 for i, g in zip(at, grid)]))
        def _():
            ride.wait(srcs, lands, sems)

    hbm = pl.BlockSpec(memory_space=pl.ANY)
    res = pl.pallas_call(
        riding, name=name, grid=grid, in_specs=in_specs + [hbm] * len(extra), out_specs=out_specs + [hbm] * m,
        out_shape=out_shape + ride.out_shapes(), scratch_shapes=list(scratch_shapes) + ride.scratch(),
        input_output_aliases=ride.aliases(n_in, n_out), compiler_params=_params(len(grid)),
    )(*args, *extra)
    return list(res[:n_out]), list(res[n_out:])


def _prenorm_inproj(h, gain, w_in_t, name, ride=None):
    T = h.shape[0]

    def body(h_ref, g_ref, w_ref, q_ref, k_ref, v_ref, ug_ref, hn_ref):
        hv = h_ref[...]
        r = lax.rsqrt(jnp.mean(hv * hv, axis=-1, keepdims=True) + EPS)
        hn = (hv * r * g_ref[...]).astype(BF16)
        hn_ref[...] = hn
        for j in range(N_CHUNK):
            u = _nt(hn, w_ref[j * CHUNK:(j + 1) * CHUNK, :])
            if j == 0:
                q_ref[...] = (u * (HEAD_DIM ** -0.5)).astype(BF16)
            elif j == 1:
                k_ref[...] = u.astype(BF16)
            elif j == 2:
                v_ref[...] = u.astype(BF16)
            else:
                ug_ref[:, (j - 3) * CHUNK:(j - 2) * CHUNK] = u

    act = jax.ShapeDtypeStruct((T, CHUNK), BF16)
    return _call(
        body, name=name, grid=(T // TM,),
        in_specs=[_rows(TM, D_MODEL), _whole((1, D_MODEL)), _whole((N_CHUNK * CHUNK, D_MODEL))],
        out_specs=[_rows(TM, CHUNK)] * 3 + [_rows(TM, 4 * CHUNK), _rows(TM, D_MODEL)],
        out_shape=[act, act, act, jax.ShapeDtypeStruct((T, 4 * CHUNK), F32),
                   jax.ShapeDtypeStruct((T, D_MODEL), BF16)],
        args=(h, gain, w_in_t,), ride=ride)


def _softplus_parts(z):
    ez = jnp.exp(jnp.minimum(z, SOFTPLUS_LINEAR_AT))
    t = 1.0 + ez
    return ez * pl.reciprocal(t, approx=True), jnp.where(z > SOFTPLUS_LINEAR_AT, z, jnp.log(t))


def _attn_fwd(qs, k, v, tri, name, ride=None):
    T = qs.shape[0]
    assert T // BLK <= FIRST_BLOCK_LANE, "one lane per key block below the lane of the first block"
    width = LANES * ATT_COLS
    chains = [(c, half) for c in range(ATT_COLS) for half in range(2)]

    def body(q_ref, k_ref, v_ref, m_ref, o_ref, cs_ref):
        qi = pl.program_id(1)
        lane = lax.broadcasted_iota(jnp.int32, (BLK, LANES), 1)
        first = lane < HEAD_DIM
        causal = (lax.broadcasted_iota(jnp.int32, (BLK, BLK), 1)
                  < lax.broadcasted_iota(jnp.int32, (BLK, BLK), 0))
        tri_m = m_ref[...]
        qh = {}
        for c in range(ATT_COLS):
            q = q_ref[:, c * LANES:(c + 1) * LANES]
            zero = jnp.zeros_like(q)
            qh[c, 0], qh[c, 1] = jnp.where(first, q, zero), jnp.where(first, zero, q)

        def step(kb, state, masked):
            carries, accs, cvals = state
            start = pl.multiple_of(kb * BLK, BLK)
            kblk = [k_ref[pl.ds(start, BLK), c * LANES:(c + 1) * LANES] for c in range(ATT_COLS)]
            vblk = [v_ref[pl.ds(start, BLK), c * LANES:(c + 1) * LANES] for c in range(ATT_COLS)]
            carries, accs, cvals = list(carries), list(accs), list(cvals)
            for g0 in range(0, len(chains), CHAIN_GROUP):
                ids = range(g0, g0 + CHAIN_GROUP)
                z = [_nt(qh[chains[n]], kblk[chains[n][0]]) for n in ids]
                sp = [_softplus_parts(zi)[1] for zi in z]
                if masked:
                    sp = [jnp.where(causal, s, 0.0) for s in sp]
                incl = [_dot_hilo(s, tri_m) for s in sp]
                a = [jnp.exp(zi - ii - carries[n]) for n, zi, ii in zip(ids, z, incl)]
                if masked:
                    a = [jnp.where(causal, ai, 0.0) for ai in a]
                for n, ai, ii in zip(ids, a, incl):
                    c, half = chains[n]
                    zero = jnp.zeros_like(vblk[c])
                    vh = jnp.where(first, vblk[c], zero) if half == 0 else jnp.where(first, zero, vblk[c])
                    accs[c] = accs[c] + _nn(ai.astype(BF16), vh)
                    cvals[c] = jnp.where(lane == kb + HEAD_DIM * half, carries[n], cvals[c])
                    carries[n] = carries[n] + ii[:, 0:1]
            return tuple(carries), tuple(accs), tuple(cvals)

        zeros = tuple(jnp.zeros((BLK, LANES), F32) for _ in range(ATT_COLS))
        state = (tuple(jnp.zeros((BLK, 1), F32) for _ in chains), zeros, zeros)
        state = step(qi, state, True)

        def reaches_further(st):
            it, (carries, _, _) = st
            least = functools.reduce(jnp.minimum, carries)
            return jnp.logical_and(it < qi, jnp.min(least) < DEAD_AT)

        done, state = lax.while_loop(reaches_further, lambda st: (st[0] + 1, step(qi - 1 - st[0], st[1], False)),
                                     (jnp.int32(0), state))
        first_block = (qi - done).astype(F32)
        for c in range(ATT_COLS):
            o_ref[:, c * LANES:(c + 1) * LANES] = state[1][c]
            cs_ref[:, c * LANES:(c + 1) * LANES] = jnp.where(lane == FIRST_BLOCK_LANE, first_block, state[2][c])

    blk = pl.BlockSpec((BLK, width), lambda j, i: (i, j))
    col = pl.BlockSpec((T, width), lambda j, i: (0, j))
    out = jax.ShapeDtypeStruct((T, ATTN_DIM), F32)
    return _call(
        body, name=name, grid=(ATTN_DIM // width, T // BLK),
        in_specs=[blk, col, col, _whole((BLK, BLK))],
        out_specs=[blk, blk], out_shape=[out, out],
        args=(qs, k, v, tri,), ride=ride)


def _shifted_copies(pad_ref, sh_ref):
    rows = sh_ref.shape[1]
    for b in range(SUBLANES):
        sh_ref[b] = pad_ref[b:b + rows, :]


def _shift_of(offset):
    return offset % SUBLANES, offset - offset % SUBLANES


def _conv_fwd(ug, dw_w, dw_b, ln_g, ln_b, name, ride=None):
    T = ug.shape[0]
    per = TM // HALO

    def body(cv_ref, cg_ref, cvh_ref, cgh_ref, w_ref, b_ref, g_ref, beta_ref, conv_ref, c2_ref, pad_ref, sh_ref):
        i = pl.program_id(0)
        halo = cvh_ref[...] * _sigmoid(cgh_ref[...])
        pad_ref[0:HALO, :] = jnp.where(i == 0, 0.0, halo)
        pad_ref[HALO:HALO + TM, :] = cv_ref[...] * _sigmoid(cg_ref[...])
        pad_ref[HALO + TM:, :] = jnp.zeros((SUBLANES, CONV_DIM), F32)
        _shifted_copies(pad_ref, sh_ref)
        taps = [w_ref[t:t + 1, :] for t in range(CONV_WIDTH)]

        def rows(j, _):
            r = pl.multiple_of(j * CONV_ROWS, CONV_ROWS)
            acc = jnp.zeros((CONV_ROWS, CONV_DIM), F32) + b_ref[...]
            for t in range(CONV_WIDTH):
                b, a = _shift_of(HALO - (CONV_WIDTH - 1) + t)
                acc = acc + taps[t] * sh_ref[b, pl.ds(r + a, CONV_ROWS), :]
            conv_ref[pl.ds(r, CONV_ROWS), :] = acc
            return 0

        lax.fori_loop(0, TM // CONV_ROWS, rows, 0)
        acc = conv_ref[...]
        mu = jnp.mean(acc, axis=-1, keepdims=True)
        xc = acc - mu
        rs = lax.rsqrt(jnp.mean(xc * xc, axis=-1, keepdims=True) + EPS)
        ln = xc * rs * g_ref[...] + beta_ref[...]
        c2_ref[...] = (ln * _sigmoid(ln)).astype(BF16)

    prev = lambda col: pl.BlockSpec((HALO, CHUNK), lambda i: (jnp.maximum(i * per - 1, 0), col))
    vec = _whole((1, CONV_DIM))
    return _call(
        body, name=name, grid=(T // TM,),
        in_specs=[_rows(TM, CHUNK, 1), _rows(TM, CHUNK, 2), prev(1), prev(2),
                  _whole((CONV_WIDTH, CONV_DIM)), vec, vec, vec],
        out_specs=[_rows(TM, CONV_DIM), _rows(TM, CONV_DIM)],
        out_shape=[jax.ShapeDtypeStruct((T, CONV_DIM), F32), jax.ShapeDtypeStruct((T, CONV_DIM), BF16)],
        scratch_shapes=[pltpu.VMEM((TM + HALO + SUBLANES, CONV_DIM), F32),
                        pltpu.VMEM((SUBLANES, TM + HALO, CONV_DIM), F32)],
        args=(ug, ug, ug, ug, dw_w, dw_b, ln_g, ln_b,), ride=ride)


def _mix_out_ple(o, ug, c2, h, p, head_mean, g_attn, g_conv, g_ple, w_pw, w_out, w_gate, w_ple, name, ride=None):
    T = h.shape[0]

    def body(o_ref, ga_ref, gc_ref, c2_ref, h_ref, p_ref, hm_ref, gao_ref, gco_ref, gpn_ref,
             wpw_ref, wout_ref, wg_ref, wple_ref,
             h2_ref, h1_ref, ycat_ref, hn2_ref, gate_ref, e_ref, c3_ref):
        ov = o_ref[...]
        rh = lax.rsqrt(_nn((ov * ov).astype(BF16), hm_ref[...]) + EPS)
        ga = ga_ref[...]
        ya = (ov * rh * gao_ref[...] * (ga * _sigmoid(ga))).astype(BF16)
        c3 = _nn(c2_ref[...], wpw_ref[...])
        c3_ref[...] = c3
        rc = lax.rsqrt(jnp.mean(c3 * c3, axis=-1, keepdims=True) + EPS)
        gc = gc_ref[...]
        yc = (c3 * rc * gco_ref[...] * (gc * _sigmoid(gc))).astype(BF16)
        ycat_ref[:, :ATTN_DIM] = ya
        ycat_ref[:, ATTN_DIM:] = yc
        h1 = h_ref[...] + _nn(ya, wout_ref[:ATTN_DIM, :]) + _nn(yc, wout_ref[ATTN_DIM:, :])
        h1_ref[...] = h1
        r1 = lax.rsqrt(jnp.mean(h1 * h1, axis=-1, keepdims=True) + EPS)
        hn2 = (h1 * r1 * gpn_ref[...]).astype(BF16)
        hn2_ref[...] = hn2
        gate = _sigmoid(_nn(hn2, wg_ref[...]))
        e = _nn(p_ref[...].astype(BF16), wple_ref[...])
        gate_ref[...] = gate
        e_ref[...] = e
        h2_ref[...] = h1 + e * gate

    f32 = lambda cols: jax.ShapeDtypeStruct((T, cols), F32)
    bf = lambda cols: jax.ShapeDtypeStruct((T, cols), BF16)
    return _call(
        body, name=name, grid=(T // TM,),
        in_specs=[_rows(TM, ATTN_DIM), _rows(TM, CHUNK, 0), _rows(TM, CHUNK, 3), _rows(TM, CONV_DIM),
                  _rows(TM, D_MODEL), _rows(TM, PLE_DIM), _whole((ATTN_DIM, ATTN_DIM)),
                  _whole((1, ATTN_DIM)), _whole((1, CONV_DIM)), _whole((1, D_MODEL)),
                  _whole((CONV_DIM, CONV_DIM)), _whole((D_MODEL, D_MODEL)), _whole((D_MODEL, D_MODEL)),
                  _whole((PLE_DIM, D_MODEL))],
        out_specs=[_rows(TM, D_MODEL), _rows(TM, D_MODEL), _rows(TM, D_MODEL), _rows(TM, D_MODEL),
                   _rows(TM, D_MODEL), _rows(TM, D_MODEL), _rows(TM, CONV_DIM)],
        out_shape=[f32(D_MODEL), f32(D_MODEL), bf(D_MODEL), bf(D_MODEL), f32(D_MODEL), f32(D_MODEL),
                   f32(CONV_DIM)],
        args=(o, ug, ug, c2, h, p, head_mean, g_attn, g_conv, g_ple, w_pw, w_out, w_gate, w_ple,), ride=ride)


def _final_loss(h, target, gain, name):
    T = h.shape[0]

    def body(h_ref, t_ref, g_ref, dh_ref, gsum_ref, loss_ref):
        @pl.when(pl.program_id(0) == 0)
        def _():
            gsum_ref[...] = jnp.zeros_like(gsum_ref)
            loss_ref[...] = jnp.zeros_like(loss_ref)

        hv = h_ref[...]
        r = lax.rsqrt(jnp.mean(hv * hv, axis=-1, keepdims=True) + EPS)
        xh = hv * r
        diff = xh * g_ref[...] - t_ref[...]
        loss_ref[...] += 0.5 * jnp.sum(jnp.mean(diff * diff, axis=-1, keepdims=True), axis=0, keepdims=True)
        dy = diff * (1.0 / D_MODEL)
        gsum_ref[...] += jnp.sum(dy * xh, axis=0, keepdims=True)
        dxh = dy * g_ref[...]
        dh_ref[...] = r * (dxh - xh * jnp.mean(dxh * xh, axis=-1, keepdims=True))

    return pl.pallas_call(
        body, name=name, grid=(T // TM,),
        in_specs=[_rows(TM, D_MODEL), _rows(TM, D_MODEL), _whole((1, D_MODEL))],
        out_specs=[_rows(TM, D_MODEL), _whole((1, D_MODEL)), _whole((1, LANES))],
        out_shape=[jax.ShapeDtypeStruct((T, D_MODEL), F32), jax.ShapeDtypeStruct((1, D_MODEL), F32),
                   jax.ShapeDtypeStruct((1, LANES), F32)],
        compiler_params=_params(1),
    )(h, target, gain)


def _ple_out_bwd(dh2, gate, e, h1, g_ple, w_gate, w_out, name, ride=None):
    T = dh2.shape[0]

    def body(dh2_ref, gate_ref, e_ref, h1_ref, gpn_ref, wg_ref, wout_ref,
             dh1_ref, dh1b_ref, dzg_ref, de_ref, dycat_ref, gsum_ref):
        @pl.when(pl.program_id(0) == 0)
        def _():
            gsum_ref[...] = jnp.zeros_like(gsum_ref)

        dh2v = dh2_ref[...]
        gate = gate_ref[...]
        de_ref[...] = (dh2v * gate).astype(BF16)
        dzg = (dh2v * e_ref[...] * gate * (1.0 - gate)).astype(BF16)
        dzg_ref[...] = dzg
        dhn2 = _nt(dzg, wg_ref[...])
        h1 = h1_ref[...]
        r1 = lax.rsqrt(jnp.mean(h1 * h1, axis=-1, keepdims=True) + EPS)
        xh = h1 * r1
        gsum_ref[...] += jnp.sum(dhn2 * xh, axis=0, keepdims=True)
        dxh = dhn2 * gpn_ref[...]
        dh1 = dh2v + r1 * (dxh - xh * jnp.mean(dxh * xh, axis=-1, keepdims=True))
        dh1_ref[...] = dh1
        dh1b = dh1.astype(BF16)
        dh1b_ref[...] = dh1b
        dycat_ref[...] = _nt(dh1b, wout_ref[...])

    f32 = jax.ShapeDtypeStruct((T, D_MODEL), F32)
    bf = jax.ShapeDtypeStruct((T, D_MODEL), BF16)
    full = _rows(TM, D_MODEL)
    return _call(
        body, name=name, grid=(T // TM,),
        in_specs=[full, full, full, full, _whole((1, D_MODEL)), _whole((D_MODEL, D_MODEL)),
                  _whole((D_MODEL, D_MODEL))],
        out_specs=[full, full, full, full, full, _whole((1, D_MODEL))],
        out_shape=[f32, bf, bf, bf, f32, jax.ShapeDtypeStruct((1, D_MODEL), F32)],
        args=(dh2, gate, e, h1, g_ple, w_gate, w_out,), ride=ride)


def _branch_bwd(dycat, o, ug, c3, conv, head_mean, g_attn, g_conv, ln_g, ln_b, w_pw, name, ride=None):
    T = o.shape[0]

    def body(dya_ref, dyc_ref, o_ref, ga_ref, gc_ref, c3_ref, conv_ref, hm_ref, gao_ref, gco_ref,
             lng_ref, lnb_ref, wpw_ref,
             do_ref, dga_ref, dgc_ref, dc3_ref, dconv_ref, sums_ref):
        @pl.when(pl.program_id(0) == 0)
        def _():
            sums_ref[...] = jnp.zeros_like(sums_ref)

        hm = hm_ref[...]
        col = lambda x: jnp.sum(x, axis=0, keepdims=True)
        ov = o_ref[...]
        rh = lax.rsqrt(_nn((ov * ov).astype(BF16), hm) + EPS)
        xh = ov * rh
        ga = ga_ref[...]
        sg = _sigmoid(ga)
        dya = dya_ref[...]
        don = dya * (ga * sg)
        dga_ref[...] = (dya * xh * gao_ref[...] * _dsilu(ga, sg)).astype(BF16)
        sums_ref[0:1, :] += col(don * xh)
        dxh = don * gao_ref[...]
        do_ref[...] = (rh * (dxh - xh * _dot_hilo(dxh * xh, hm))).astype(BF16)
        c3 = c3_ref[...]
        rc = lax.rsqrt(jnp.mean(c3 * c3, axis=-1, keepdims=True) + EPS)
        xh3 = c3 * rc
        gc = gc_ref[...]
        sgc = _sigmoid(gc)
        dyc = dyc_ref[...]
        dn3 = dyc * (gc * sgc)
        dgc_ref[...] = (dyc * xh3 * gco_ref[...] * _dsilu(gc, sgc)).astype(BF16)
        sums_ref[1:2, :] += col(dn3 * xh3)
        dxh3 = dn3 * gco_ref[...]
        dc3 = (rc * (dxh3 - xh3 * jnp.mean(dxh3 * xh3, axis=-1, keepdims=True))).astype(BF16)
        dc3_ref[...] = dc3
        dc2 = _nt(dc3, wpw_ref[...])
        cv = conv_ref[...]
        mu = jnp.mean(cv, axis=-1, keepdims=True)
        xc = cv - mu
        rs = lax.rsqrt(jnp.mean(xc * xc, axis=-1, keepdims=True) + EPS)
        xn = xc * rs
        ln = xn * lng_ref[...] + lnb_ref[...]
        dln = dc2 * _dsilu(ln, _sigmoid(ln))
        sums_ref[2:3, :] += col(dln * xn)
        sums_ref[3:4, :] += col(dln)
        dxn = dln * lng_ref[...]
        dconv = rs * (dxn - jnp.mean(dxn, axis=-1, keepdims=True)
                      - xn * jnp.mean(dxn * xn, axis=-1, keepdims=True))
        dconv_ref[...] = dconv
        sums_ref[4:5, :] += col(dconv)

    half = lambda dt: jax.ShapeDtypeStruct((T, CHUNK), dt)
    tile = _rows(TM, CHUNK)
    vec = _whole((1, CHUNK))
    return _call(
        body, name=name, grid=(T // TM,),
        in_specs=[_rows(TM, CHUNK, 0), _rows(TM, CHUNK, 1), tile, _rows(TM, CHUNK, 0), _rows(TM, CHUNK, 3),
                  tile, tile, _whole((ATTN_DIM, ATTN_DIM)), vec, vec, vec, vec, _whole((CONV_DIM, CONV_DIM))],
        out_specs=[tile, tile, tile, tile, tile, _whole((8, CHUNK))],
        out_shape=[half(BF16), half(BF16), half(BF16), half(BF16), half(F32),
                   jax.ShapeDtypeStruct((8, CHUNK), F32)],
        args=(dycat, dycat, o, ug, ug, c3, conv, head_mean, g_attn, g_conv, ln_g, ln_b, w_pw,), ride=ride)


def _conv_bwd(dconv, ug, dw_w, name, ride=None):
    T = dconv.shape[0]
    per = TM // HALO
    last = T // HALO - 1
    n_tiles = T // TM

    def body(d_ref, dn_ref, cv_ref, cg_ref, cvh_ref, cgh_ref, w_ref, dcv_ref, dcg_ref, dw_ref,
             dpad_ref, cpad_ref, dsh_ref, csh_ref, dw_acc):
        i = pl.program_id(0)

        @pl.when(i == 0)
        def _():
            dw_acc[...] = jnp.zeros_like(dw_acc)

        tail = jnp.zeros((SUBLANES, CONV_DIM), F32)
        dpad_ref[0:TM, :] = d_ref[...]
        dpad_ref[TM:TM + HALO, :] = jnp.where(i == n_tiles - 1, 0.0, dn_ref[...])
        dpad_ref[TM + HALO:, :] = tail
        halo = cvh_ref[...] * _sigmoid(cgh_ref[...])
        cpad_ref[0:HALO, :] = jnp.where(i == 0, 0.0, halo)
        cpad_ref[HALO:HALO + TM, :] = cv_ref[...] * _sigmoid(cg_ref[...])
        cpad_ref[HALO + TM:, :] = tail
        _shifted_copies(dpad_ref, dsh_ref)
        _shifted_copies(cpad_ref, csh_ref)
        taps = [w_ref[t:t + 1, :] for t in range(CONV_WIDTH)]

        def rows(j, _):
            r = pl.multiple_of(j * CONV_ROWS, CONV_ROWS)
            d = d_ref[pl.ds(r, CONV_ROWS), :]
            dc = jnp.zeros((CONV_ROWS, CONV_DIM), F32)
            for t in range(CONV_WIDTH):
                b, a = _shift_of(CONV_WIDTH - 1 - t)
                dc = dc + taps[t] * dsh_ref[b, pl.ds(r + a, CONV_ROWS), :]
                b, a = _shift_of(HALO - (CONV_WIDTH - 1) + t)
                prod = d * csh_ref[b, pl.ds(r + a, CONV_ROWS), :]
                dw_acc[t] += jnp.sum(prod.reshape(CONV_ROWS // SUBLANES, SUBLANES, CONV_DIM), axis=0)
            cv = cv_ref[pl.ds(r, CONV_ROWS), :]
            sg = _sigmoid(cg_ref[pl.ds(r, CONV_ROWS), :])
            dcv_ref[pl.ds(r, CONV_ROWS), :] = (dc * sg).astype(BF16)
            dcg_ref[pl.ds(r, CONV_ROWS), :] = (dc * cv * sg * (1.0 - sg)).astype(BF16)
            return 0

        lax.fori_loop(0, TM // CONV_ROWS, rows, 0)

        @pl.when(i == n_tiles - 1)
        def _():
            dw_ref[...] = jnp.zeros_like(dw_ref)
            for t in range(CONV_WIDTH):
                dw_ref[t:t + 1, :] = jnp.sum(dw_acc[t], axis=0, keepdims=True)

    prev = lambda col: pl.BlockSpec((HALO, CHUNK), lambda i: (jnp.maximum(i * per - 1, 0), col))
    nxt = pl.BlockSpec((HALO, CONV_DIM), lambda i: (jnp.minimum((i + 1) * per, last), 0))
    half = jax.ShapeDtypeStruct((T, CHUNK), BF16)
    return _call(
        body, name=name, grid=(T // TM,),
        in_specs=[_rows(TM, CONV_DIM), nxt, _rows(TM, CHUNK, 1), _rows(TM, CHUNK, 2), prev(1), prev(2),
                  _whole((CONV_WIDTH, CONV_DIM))],
        out_specs=[_rows(TM, CHUNK), _rows(TM, CHUNK), _whole((HALO, CONV_DIM))],
        out_shape=[half, half, jax.ShapeDtypeStruct((HALO, CONV_DIM), F32)],
        scratch_shapes=[pltpu.VMEM((TM + HALO + SUBLANES, CONV_DIM), F32),
                        pltpu.VMEM((TM + HALO + SUBLANES, CONV_DIM), F32),
                        pltpu.VMEM((SUBLANES, TM + HALO, CONV_DIM), F32),
                        pltpu.VMEM((SUBLANES, TM + HALO, CONV_DIM), F32),
                        pltpu.VMEM((HALO, SUBLANES, CONV_DIM), F32)],
        args=(dconv, dconv, ug, ug, ug, ug, dw_w,), ride=ride)


def _attn_bwd(qs, k, v, do, cs, tri, tri_t, name, ride=None):
    T = qs.shape[0]
    nq = T // BLK
    width = LANES * ATT_COLS
    chains = [(c, half) for c in range(ATT_COLS) for half in range(2)]

    def body(q_ref, k_ref, v_ref, do_ref, cs_ref, m_ref, mt_ref, dq_ref, dk_ref, dv_ref, dk_acc, dv_acc):
        qi = pl.program_id(1)

        @pl.when(qi == 0)
        def _():
            dk_acc[...] = jnp.zeros_like(dk_acc)
            dv_acc[...] = jnp.zeros_like(dv_acc)

        lane = lax.broadcasted_iota(jnp.int32, (BLK, LANES), 1)
        first = lane < HEAD_DIM
        causal = (lax.broadcasted_iota(jnp.int32, (BLK, BLK), 1)
                  < lax.broadcasted_iota(jnp.int32, (BLK, BLK), 0))
        tri_m = m_ref[...]
        tri_mt = mt_ref[...]

        def halves(x):
            zero = jnp.zeros_like(x)
            return jnp.where(first, x, zero), jnp.where(first, zero, x)

        qh, doh, cs = {}, {}, []
        for c in range(ATT_COLS):
            qh[c, 0], qh[c, 1] = halves(q_ref[:, c * LANES:(c + 1) * LANES])
            doh[c, 0], doh[c, 1] = halves(do_ref[:, c * LANES:(c + 1) * LANES])
            cs.append(cs_ref[:, c * LANES:(c + 1) * LANES])

        def step(kb, state, masked):
            prefixes, dq_accs = state
            start = pl.multiple_of(kb * BLK, BLK)
            kblk = [k_ref[pl.ds(start, BLK), c * LANES:(c + 1) * LANES] for c in range(ATT_COLS)]
            vblk = [v_ref[pl.ds(start, BLK), c * LANES:(c + 1) * LANES] for c in range(ATT_COLS)]
            prefixes, dq_accs = list(prefixes), list(dq_accs)
            for g0 in range(0, len(chains), CHAIN_GROUP):
                ids = range(g0, g0 + CHAIN_GROUP)
                grp = [chains[n] for n in ids]
                z = [_nt(qh[ch], kblk[ch[0]]) for ch in grp]
                da = [_nt(doh[ch], vblk[ch[0]]) for ch in grp]
                parts = [_softplus_parts(zi) for zi in z]
                sp = [pt[1] for pt in parts]
                if masked:
                    sp = [jnp.where(causal, s, 0.0) for s in sp]
                incl = [_dot_hilo(s, tri_m) for s in sp]
                carries = [jnp.sum(jnp.where(lane == kb + HEAD_DIM * half, cs[c], 0.0), axis=1, keepdims=True)
                           for c, half in grp]
                a = [jnp.exp(zi - ii - ci) for zi, ii, ci in zip(z, incl, carries)]
                if masked:
                    a = [jnp.where(causal, ai, 0.0) for ai in a]
                w = [ai * di for ai, di in zip(a, da)]
                pinc = [_nn(wi.astype(BF16), tri_mt) for wi in w]
                dz = [wi - pt[0] * (pi + prefixes[n]) for n, wi, pt, pi in zip(ids, w, parts, pinc)]
                if masked:
                    dz = [jnp.where(causal, d, 0.0) for d in dz]
                for j in range(0, CHAIN_GROUP, 2):
                    c = grp[j][0]
                    k0, k1 = halves(kblk[c])
                    dz0, dz1 = dz[j].astype(BF16), dz[j + 1].astype(BF16)
                    a0, a1 = a[j].astype(BF16), a[j + 1].astype(BF16)
                    dq_accs[c] = dq_accs[c] + _nn(dz0, k0) + _nn(dz1, k1)
                    dk_acc[pl.ds(start, BLK), c * LANES:(c + 1) * LANES] += _tn(dz0, qh[c, 0]) + _tn(dz1, qh[c, 1])
                    dv_acc[pl.ds(start, BLK), c * LANES:(c + 1) * LANES] += _tn(a0, doh[c, 0]) + _tn(a1, doh[c, 1])
                for n, pi in zip(ids, pinc):
                    prefixes[n] = prefixes[n] + pi[:, BLK - 1:BLK]
            return tuple(prefixes), tuple(dq_accs)

        state = (tuple(jnp.zeros((BLK, 1), F32) for _ in chains),
                 tuple(jnp.zeros((BLK, LANES), F32) for _ in range(ATT_COLS)))
        first_block = jnp.max(jnp.where(lane == FIRST_BLOCK_LANE, cs[0], 0.0)).astype(jnp.int32)
        state = lax.fori_loop(first_block, qi, lambda kb, st: step(kb, st, False), state)
        state = step(qi, state, True)
        for c in range(ATT_COLS):
            dq_ref[:, c * LANES:(c + 1) * LANES] = (state[1][c] * (HEAD_DIM ** -0.5)).astype(BF16)

        @pl.when(qi == nq - 1)
        def _():
            dk_ref[...] = dk_acc[...].astype(BF16)
            dv_ref[...] = dv_acc[...].astype(BF16)

    blk = pl.BlockSpec((BLK, width), lambda j, i: (i, j))
    col = pl.BlockSpec((T, width), lambda j, i: (0, j))
    out = jax.ShapeDtypeStruct((T, ATTN_DIM), BF16)
    return _call(
        body, name=name, grid=(ATTN_DIM // width, nq),
        in_specs=[blk, col, col, blk, blk, _whole((BLK, BLK)), _whole((BLK, BLK))],
        out_specs=[blk, col, col], out_shape=[out, out, out],
        scratch_shapes=[pltpu.VMEM((T, width), F32), pltpu.VMEM((T, width), F32)],
        args=(qs, k, v, do, cs, tri, tri_t,), ride=ride)


def _inproj_bwd(du, w_in_t, h, dh1, gain, name, ride=None):
    T = h.shape[0]

    def body(*refs):
        du_refs = refs[:N_CHUNK]
        w_ref, h_ref, dh1_ref, g_ref, dh_ref, gsum_ref = refs[N_CHUNK:]

        @pl.when(pl.program_id(0) == 0)
        def _():
            gsum_ref[...] = jnp.zeros_like(gsum_ref)

        dhn = jnp.zeros((TM, D_MODEL), F32)
        for j in range(N_CHUNK):
            dhn = dhn + _nn(du_refs[j][...], w_ref[j * CHUNK:(j + 1) * CHUNK, :])
        hv = h_ref[...]
        r = lax.rsqrt(jnp.mean(hv * hv, axis=-1, keepdims=True) + EPS)
        xh = hv * r
        gsum_ref[...] += jnp.sum(dhn * xh, axis=0, keepdims=True)
        dxh = dhn * g_ref[...]
        dh_ref[...] = dh1_ref[...] + r * (dxh - xh * jnp.mean(dxh * xh, axis=-1, keepdims=True))

    full = _rows(TM, D_MODEL)
    return _call(
        body, name=name, grid=(T // TM,),
        in_specs=[_rows(TM, CHUNK)] * N_CHUNK + [_whole((N_CHUNK * CHUNK, D_MODEL)), full, full,
                                                 _whole((1, D_MODEL))],
        out_specs=[full, _whole((1, D_MODEL))],
        out_shape=[jax.ShapeDtypeStruct((T, D_MODEL), F32), jax.ShapeDtypeStruct((1, D_MODEL), F32)],
        args=(*du, w_in_t, h, dh1, gain), ride=ride)


def _weight_grad(lhs_list, rhs, name, tk=CHUNK, ride=None):
    T, n_rhs = rhs.shape
    n = len(lhs_list)
    ka = lhs_list[0].shape[1]
    per = ka // tk

    def body(*refs):
        a_refs, b_ref, out_ref = refs[:n], refs[n], refs[n + 1]
        step = pl.program_id(0)
        for j in range(n):
            for s in range(per):
                @pl.when(step == j * per + s)
                def _(j=j, s=s):
                    out_ref[...] = _tn(a_refs[j][:, s * tk:(s + 1) * tk], b_ref[...]).astype(BF16)

    (grad,), landed = _call(
        body, name=name, grid=(n * per,),
        in_specs=[_whole((T, ka))] * n + [_whole((T, n_rhs))],
        out_specs=[pl.BlockSpec((tk, n_rhs), lambda i: (i, 0))],
        out_shape=[jax.ShapeDtypeStruct((n * ka, n_rhs), BF16)],
        args=(*lhs_list, rhs), ride=ride)
    return grad, landed


def _adamw_update(w, g, m, v):
    nm = ADAM_B1 * m + (1.0 - ADAM_B1) * g
    nv = ADAM_B2 * v + (1.0 - ADAM_B2) * (g * g)
    m_hat = nm / (1.0 - ADAM_B1 ** ADAM_STEP)
    v_hat = nv / (1.0 - ADAM_B2 ** ADAM_STEP)
    return -ADAM_LR * (m_hat / (jnp.sqrt(v_hat) + ADAM_EPS) + ADAM_WD * w), nm, nv


def _sum_adamw(slots, w, m, v, name):
    depth, R, C = w.shape
    tr = next(rows for rows in ADAMW_ROWS if R % rows == 0)

    def body(*refs):
        slot_refs, (w_ref, m_ref, v_ref, g_ref, d_ref, nm_ref, nv_ref) = refs[:depth], refs[depth:]
        for layer in range(depth):
            @pl.when(pl.program_id(0) == layer)
            def _(src=slot_refs[layer]):
                g = src[0].astype(F32)
                for s in range(1, src.shape[0]):
                    g = g + src[s].astype(F32)
                g_ref[0] = g
                d_ref[0], nm_ref[0], nv_ref[0] = _adamw_update(w_ref[0], g, m_ref[0], v_ref[0])

    slot_spec = lambda layer: pl.BlockSpec((slots[layer].shape[0], tr, C),
                                           lambda l, i: (0, jnp.where(l == layer, i, 0), 0))
    spec = pl.BlockSpec((1, tr, C), lambda l, i: (l, i, 0))
    out = jax.ShapeDtypeStruct((depth, R, C), F32)
    return pl.pallas_call(
        body, name=name, grid=(depth, R // tr),
        in_specs=[slot_spec(layer) for layer in range(depth)] + [spec] * 3,
        out_specs=[spec] * 4, out_shape=[out] * 4,
        compiler_params=_params(2),
    )(*slots, w, m, v)


def _adamw(w, g, m, v, name):
    R, C = w.shape
    tr = R
    for cand in (512, 256, 128, 64):
        if R % cand == 0 and R > cand:
            tr = cand
            break

    def body(w_ref, g_ref, m_ref, v_ref, d_ref, nm_ref, nv_ref):
        d_ref[...], nm_ref[...], nv_ref[...] = _adamw_update(w_ref[...], g_ref[...], m_ref[...], v_ref[...])

    spec = pl.BlockSpec((tr, C), lambda i: (i, 0))
    out = jax.ShapeDtypeStruct((R, C), F32)
    return pl.pallas_call(
        body, name=name, grid=(R // tr,),
        in_specs=[spec] * 4, out_specs=[spec] * 3, out_shape=[out, out, out],
        compiler_params=_params(1),
    )(w, g, m, v)


def _pack_small(values, scalar=None):
    pad = lambda a: jnp.pad(a, ((0, 0), (0, D_MODEL - a.shape[1])))
    last = jnp.zeros((1, D_MODEL), F32) if scalar is None else pad(scalar.reshape(1, 1))
    return jnp.concatenate([pad(values[name].reshape(rows, cols)) for name, _, rows, cols in SMALL_LAYOUT] + [last],
                           axis=0)


def _small_update(all_packs, state, name):
    n = len(SMALL_LAYOUT)

    def body(packs_ref, *refs):
        ins, outs = refs[:3 * n], refs[3 * n:]
        total = packs_ref[0]
        for s in range(1, N_DEV):
            total = total + packs_ref[s]
        for j, (_, at, rows, cols) in enumerate(SMALL_LAYOUT):
            g = total[at:at + rows, :cols]
            w_ref, m_ref, v_ref = ins[3 * j:3 * j + 3]
            outs[4 * j][...] = g
            outs[4 * j + 1][...], outs[4 * j + 2][...], outs[4 * j + 3][...] = _adamw_update(
                w_ref[...], g, m_ref[...], v_ref[...])
        outs[-2][...] = total[LOSS_ROW:LOSS_ROW + 1, :LANES]
        outs[-1][...] = total[SMALL_ROWS:, :]

    shapes = [jax.ShapeDtypeStruct((rows, cols), F32) for _, _, rows, cols in SMALL_LAYOUT for _ in range(4)]
    shapes += [jax.ShapeDtypeStruct((1, LANES), F32), jax.ShapeDtypeStruct((PACK_ROWS - SMALL_ROWS, D_MODEL), F32)]
    operands = [a for item in SMALL_LAYOUT for a in state[item[0]]]
    res = pl.pallas_call(body, name=name, out_shape=shapes, compiler_params=_params())(all_packs, *operands)
    per_name = {item[0]: tuple(res[4 * j:4 * j + 4]) for j, item in enumerate(SMALL_LAYOUT)}
    return per_name, res[-2][0, 0], res[-1]


def kernel(x, p, norm_g, w_in, attn_out_g, dw_w, dw_b, conv_ln_g, conv_ln_b, w_pw, conv_out_g, w_out, ple_norm_g, w_ple_gate, w_ple, final_g, loss_target, m_norm_g, m_w_in, m_attn_out_g, m_dw_w, m_dw_b, m_conv_ln_g, m_conv_ln_b, m_w_pw, m_conv_out_g, m_w_out, m_ple_norm_g, m_w_ple_gate, m_w_ple, m_final_g, v_norm_g, v_w_in, v_attn_out_g, v_dw_w, v_dw_b, v_conv_ln_g, v_conv_ln_b, v_w_pw, v_conv_out_g, v_w_out, v_ple_norm_g, v_w_ple_gate, v_w_ple, v_final_g):
    depth = w_in.shape[0]
    T = x.shape[1]
    given = dict(
        norm_g=norm_g, ple_norm_g=ple_norm_g, final_g=final_g, dw_b=dw_b, conv_ln_g=conv_ln_g, conv_ln_b=conv_ln_b,
        conv_out_g=conv_out_g, attn_out_g=attn_out_g,
        m_norm_g=m_norm_g, m_ple_norm_g=m_ple_norm_g, m_final_g=m_final_g, m_dw_b=m_dw_b, m_conv_ln_g=m_conv_ln_g,
        m_conv_ln_b=m_conv_ln_b, m_conv_out_g=m_conv_out_g, m_attn_out_g=m_attn_out_g,
        v_norm_g=v_norm_g, v_ple_norm_g=v_ple_norm_g, v_final_g=v_final_g, v_dw_b=v_dw_b, v_conv_ln_g=v_conv_ln_g,
        v_conv_ln_b=v_conv_ln_b, v_conv_out_g=v_conv_out_g, v_attn_out_g=v_attn_out_g)
    my_idx = 4 * lax.axis_index("x") + 2 * lax.axis_index("y") + lax.axis_index("c")

    ids = jnp.arange(BLK)
    tri = (ids[:, None] >= ids[None, :]).astype(BF16)
    tri_t = (ids[:, None] <= ids[None, :]).astype(BF16)
    hid = jnp.arange(ATTN_DIM) // HEAD_DIM
    head_mean = ((hid[:, None] == hid[None, :]).astype(F32) / HEAD_DIM).astype(BF16)

    w_names = ("w_in_t", "w_pw", "w_out", "w_gate", "w_ple")
    w_axes = dict(zip(w_names, (0, 0, 0, 0, 1)))
    shards = [dict(zip(w_names, (w_in[l].T.astype(BF16), w_pw[l].astype(BF16), w_out[l].astype(BF16),
                                 w_ple_gate[l].astype(BF16), w_ple[l].astype(BF16)))) for l in range(depth)]
    first = _all_gather([shards[0]["w_in_t"]] + [dw_w[l].T for l in range(depth)], [0] * (1 + depth),
                        "gather_weights_0")
    layers = []
    for l in range(depth):
        layers.append(dict(
            dw_w=first[1 + l].T,
            g_norm=norm_g[l][None], g_attn=jnp.tile(attn_out_g[l], N_HEADS)[None], dw_b=dw_b[l][None],
            ln_g=conv_ln_g[l][None], ln_b=conv_ln_b[l][None], g_conv=conv_out_g[l][None],
            g_ple=ple_norm_g[l][None], p=p[l, 0]))
    layers[0]["w_in_t"] = first[0]

    def rest_of(l, names):
        return [_Ride.gather2(shards[l][n], w_axes[n]) for n in names]

    h = x[0]
    saved = []
    for l, w in enumerate(layers):
        early, late = (w_names[3:], w_names[1:3]) if l == 0 else ((), ())
        (qs, k, v, ug, hn), landed = _prenorm_inproj(h, w["g_norm"], w["w_in_t"], f"inproj_{l}",
                                                     _Ride(rest_of(l, early)))
        w.update(zip(early, landed))
        ahead = [_Ride.gather2(shards[l + 1]["w_in_t"], 0, None, 0, W_IN_ROWS_ON_ATTN)] if l + 1 < depth else []
        own = w_names[1:] if l > 0 else ()
        (o, cs), landed = _attn_fwd(qs, k, v, tri, f"attn_fwd_{l}", _Ride(ahead + rest_of(l, own)))
        w_in_next = landed[:len(ahead)]
        w.update(zip(own, landed[len(ahead):]))
        (conv, c2), landed = _conv_fwd(ug, w["dw_w"], w["dw_b"], w["ln_g"], w["ln_b"], f"conv_fwd_{l}",
                                       _Ride(rest_of(l, late)))
        w.update(zip(late, landed))
        tail = [_Ride.gather2(shards[l + 1]["w_in_t"], 0, a, W_IN_ROWS_ON_ATTN,
                              shards[l + 1]["w_in_t"].shape[0] - W_IN_ROWS_ON_ATTN) for a in w_in_next]
        (h2, h1, ycat, hn2, gate, e, c3), landed = _mix_out_ple(
            o, ug, c2, h, w["p"], head_mean, w["g_attn"], w["g_conv"], w["g_ple"],
            w["w_pw"], w["w_out"], w["w_gate"], w["w_ple"], f"mix_{l}", _Ride(tail))
        if landed:
            layers[l + 1]["w_in_t"] = landed[0]
        saved.append(dict(h=h, qs=qs, k=k, v=v, ug=ug, hn=hn, o=o, cs=cs, conv=conv, c2=c2, h1=h1,
                          ycat=ycat, hn2=hn2, gate=gate, e=e, c3=c3))
        h = h2
    dh, g_final, loss_part = _final_loss(h, loss_target[0], final_g[None], "final_loss")

    small = {}
    dww_parts = [None] * depth
    slots = [dict() for _ in range(depth)]
    g_w_in = None
    for l in reversed(range(depth)):
        w, s = layers[l], saved[l]
        above = [None] if g_w_in is not None else []

        def part(i, above=above, g=g_w_in):
            return [_Ride.scatter(g, 0, above[0], *W_IN_GRAD_PARTS[i])] if above else []

        def scattered(grads, names):
            return [_Ride.scatter(grads[n], w_axes[n]) for n in names]

        (dh1, dh1b, dzg, de, dycat, g_ple_sum), landed = _ple_out_bwd(
            dh, s["gate"], s["e"], s["h1"], w["g_ple"], w["w_gate"], w["w_out"], f"ple_bwd_{l}", _Ride(part(0)))
        above[:1] = landed
        (do, dga, dgc, dc3, dconv, sums), landed = _branch_bwd(
            dycat, s["o"], s["ug"], s["c3"], s["conv"], head_mean, w["g_attn"], w["g_conv"],
            w["ln_g"], w["ln_b"], w["w_pw"], f"branch_bwd_{l}", _Ride(part(1)))
        above[:1] = landed
        grads = dict(
            w_pw=_weight_grad([s["c2"]], dc3, f"grad_w_pw_{l}")[0],
            w_out=_weight_grad([s["ycat"]], dh1b, f"grad_w_out_{l}")[0],
            w_gate=_weight_grad([s["hn2"]], dzg, f"grad_w_gate_{l}")[0],
            w_ple=_weight_grad([w["p"].astype(BF16)], de, f"grad_w_ple_{l}", tk=PLE_DIM)[0])
        (dcv, dcg, dww), landed = _conv_bwd(dconv, s["ug"], w["dw_w"], f"conv_bwd_{l}", _Ride(part(2)))
        above[:1] = landed
        (dq, dk, dv), landed = _attn_bwd(s["qs"], s["k"], s["v"], do, s["cs"], tri, tri_t, f"attn_bwd_{l}",
                                         _Ride(scattered(grads, w_names[1:])))
        slots[l].update(zip(w_names[1:], landed))
        du = [dq, dk, dv, dga, dcv, dcg, dgc]
        g_w_in_here, landed = _weight_grad(du, s["hn"], f"grad_w_in_{l}", ride=_Ride(part(3)))
        if above:
            slots[l + 1]["w_in_t"] = landed[0]
        tail = [_Ride.scatter_chips(_pair_reduce(g_w_in_here, f"pair_reduce_w_in_{l}"))] if l == 0 else []
        (dh, g_norm_sum), landed = _inproj_bwd(du, w["w_in_t"], s["h"], dh1, w["g_norm"], f"inproj_bwd_{l}",
                                               _Ride(tail))
        slots[l].update(zip(("w_in_t",), landed))
        g_w_in = g_w_in_here
        small[l] = dict(norm_g=g_norm_sum, ple_norm_g=g_ple_sum, attn_out_g=sums[0].reshape(N_HEADS, HEAD_DIM).sum(0),
                        conv_out_g=sums[1], conv_ln_g=sums[2], conv_ln_b=sums[3], dw_b=sums[4])
        dww_parts[l] = dww[:CONV_WIDTH]
    slots = [[sl[n] for n in w_names] for sl in slots]
    grad_x = dh[None]

    sums_of = {name: jnp.stack([small[l][name].reshape(-1) for l in range(depth)]) for name in small[0]}
    sums_of["final_g"] = g_final
    pack = jnp.concatenate([_pack_small(sums_of, scalar=loss_part[0, 0]), jnp.concatenate(dww_parts, axis=1),
                            jnp.zeros((PACK_ROWS - SMALL_ROWS - CONV_WIDTH, D_MODEL), F32)], axis=0)
    (all_packs,) = _all_gather([pack], [0], "gather_small_grads")
    state = {name: [given[pre + name].reshape(rows, cols) for pre in ("", "m_", "v_")]
             for name, _, rows, cols in SMALL_LAYOUT}
    updated, loss, dww_sum = _small_update(all_packs.reshape(N_DEV, PACK_ROWS, D_MODEL), state, "update_small")
    res = {kind: {name: val[k].reshape(given[name].shape) for name, val in updated.items()}
           for k, kind in enumerate("gdmv")}
    dww_full = dww_sum[:CONV_WIDTH].reshape(CONV_WIDTH, depth, CONV_DIM).transpose(1, 0, 2)
    g_dw_w = lax.dynamic_slice_in_dim(dww_full, my_idx * (CONV_DIM // N_DEV), CONV_DIM // N_DEV, axis=2)

    swap = lambda a: a.transpose(0, 2, 1)
    state = {"w_in": (w_in, m_w_in, v_w_in), "w_pw": (w_pw, m_w_pw, v_w_pw), "w_out": (w_out, m_w_out, v_w_out),
             "w_ple_gate": (w_ple_gate, m_w_ple_gate, v_w_ple_gate), "w_ple": (w_ple, m_w_ple, v_w_ple)}
    for at, name in enumerate(state):
        wv, mv, vv = [swap(a) for a in state[name]] if name == "w_in" else state[name]
        out = _sum_adamw([slots[l][at] for l in range(depth)], wv, mv, vv, f"adamw_{name}")
        out = [swap(a) for a in out] if name == "w_in" else out
        res["g"][name], res["d"][name], res["m"][name], res["v"][name] = out
    flat = lambda a: a.reshape(-1, a.shape[-1])
    res["g"]["dw_w"] = g_dw_w
    res["d"]["dw_w"], res["m"]["dw_w"], res["v"]["dw_w"] = [
        a.reshape(dw_w.shape) for a in _adamw(flat(dw_w), flat(g_dw_w), flat(m_dw_w), flat(v_dw_w), "adamw_dw_w")]

    order = ["norm_g", "w_in", "attn_out_g", "dw_w", "dw_b", "conv_ln_g", "conv_ln_b", "w_pw", "conv_out_g",
             "w_out", "ple_norm_g", "w_ple_gate", "w_ple", "final_g"]
    return (loss, grad_x, *[res["g"][n] for n in order], *[res["d"][n] for n in order],
            *[res["m"][n] for n in order], *[res["v"][n] for n in order])
```

```python
import functools

import jax
import jax.numpy as jnp
from jax import lax
from jax.experimental import pallas as pl
from jax.experimental.pallas import tpu as pltpu

F32 = jnp.float32
BF16 = jnp.bfloat16
MESH = pl.DeviceIdType.MESH

N_DEV = 8
D_MODEL = 1024
ATTN_DIM = 512
CONV_DIM = 512
HEAD_DIM = 64
N_HEADS = 8
CONV_WIDTH = 31
PLE_DIM = 256
CHUNK = 512
N_CHUNK = 7
EPS = 1e-6
ADAM_LR = 0.001
ADAM_B1 = 0.9
ADAM_B2 = 0.999
ADAM_EPS = 1e-08
ADAM_WD = 0.01
ADAM_STEP = 10

LANES = 128
BLK = 256
ATT_COLS = 4
CHAIN_GROUP = 4
SOFTPLUS_LINEAR_AT = 20.0
DEAD_AT = 110.0
FIRST_BLOCK_LANE = HEAD_DIM - 1
TM = 512
HALO = 32
SUBLANES = 8
CONV_ROWS = 32
ADAMW_ROWS = (224, 128, 64)
VMEM_LIMIT = 56 * 1024 * 1024
SMALL_ROWS = 16
SMALL_LAYOUT = (("norm_g", 0, 2, D_MODEL), ("ple_norm_g", 2, 2, D_MODEL), ("final_g", 4, 1, D_MODEL),
                ("dw_b", 5, 2, CONV_DIM), ("conv_ln_g", 7, 2, CONV_DIM), ("conv_ln_b", 9, 2, CONV_DIM),
                ("conv_out_g", 11, 2, CONV_DIM), ("attn_out_g", 13, 2, HEAD_DIM))
LOSS_ROW = 15
W_IN_ROWS_ON_ATTN = 288
W_IN_GRAD_PARTS = ((0, 96), (96, 80), (176, 144), (320, 128))
PACK_ROWS = 48


def _nn(a, b):
    return lax.dot_general(a, b, (((1,), (0,)), ((), ())), preferred_element_type=F32)


def _nt(a, b):
    return lax.dot_general(a, b, (((1,), (1,)), ((), ())), preferred_element_type=F32)


def _tn(a, b):
    return lax.dot_general(a, b, (((0,), (0,)), ((), ())), preferred_element_type=F32)


def _split(x):
    hi = x.astype(BF16)
    lo = (x - hi.astype(F32)).astype(BF16)
    return hi, lo


def _dot_hilo(x, m):
    hi, lo = _split(x)
    return _nn(hi, m) + _nn(lo, m)


def _sigmoid(x):
    return jax.nn.sigmoid(x)


def _dsilu(x, s):
    return s * (1.0 + x * (1.0 - s))


def _params(n_grid=0, vmem=VMEM_LIMIT):
    sem = ("arbitrary",) * n_grid if n_grid else None
    return pltpu.CompilerParams(dimension_semantics=sem, vmem_limit_bytes=vmem)


def _rows(tm, cols, col=0):
    return pl.BlockSpec((tm, cols), lambda i: (i, col))


def _whole(shape):
    zeros = (0,) * len(shape)
    return pl.BlockSpec(shape, lambda *_: zeros)


def _my_position():
    return lax.axis_index("x"), lax.axis_index("y"), lax.axis_index("c")


def _block(ref, axis, idx, size):
    start = pl.multiple_of(idx * size, size)
    if axis == 0:
        return ref.at[pl.ds(start, size), :]
    return ref.at[:, pl.ds(start, size)]


def _all_gather(shards, axes, name):
    n = len(shards)
    sizes = [s.shape[a] for s, a in zip(shards, axes)]

    def full_shape(s, a):
        shape = list(s.shape)
        shape[a] *= N_DEV
        return jax.ShapeDtypeStruct(tuple(shape), s.dtype)

    def body(*refs):
        ins, outs = refs[:n], refs[n:2 * n]
        send_sems, recv_sems, local_sems = refs[2 * n:]
        x, y, c = _my_position()
        me, sibling = (x, y, c), (x, y, 1 - c)
        chips = [(1 - x, y), (x, 1 - y), (1 - x, 1 - y)]

        def place(i, dev):
            return _block(outs[i], axes[i], 4 * dev[0] + 2 * dev[1] + dev[2], sizes[i])

        def copy(k, i, dev, to, src=None):
            return pltpu.make_async_remote_copy(
                src_ref=place(i, dev) if src is None else src, dst_ref=place(i, dev),
                send_sem=send_sems.at[k, i], recv_sem=recv_sems.at[k, i],
                device_id=to, device_id_type=MESH)

        mine = [pltpu.make_async_copy(ins[i], place(i, me), local_sems.at[i]) for i in range(n)]
        for cp in mine:
            cp.start()
        first = [copy(0, i, me, sibling, src=ins[i]) for i in range(n)]
        for j, chip in enumerate(chips):
            first += [copy(1 + j, i, me, (*chip, c), src=ins[i]) for i in range(n)]
        for cp in first:
            cp.start()
        passed = []
        for j, chip in enumerate(chips):
            for i in range(n):
                copy(1 + j, i, (*chip, c), me).wait_recv()
            hop = [copy(4 + j, i, (*chip, c), sibling) for i in range(n)]
            for cp in hop:
                cp.start()
            passed += hop
        for i in range(n):
            copy(0, i, sibling, me).wait_recv()
        for j, chip in enumerate(chips):
            for i in range(n):
                copy(4 + j, i, (*chip, 1 - c), me).wait_recv()
        for cp in first + passed:
            cp.wait_send()
        for cp in mine:
            cp.wait()

    any_spec = pl.BlockSpec(memory_space=pl.ANY)
    return pl.pallas_call(
        body, name=name,
        out_shape=[full_shape(s, a) for s, a in zip(shards, axes)],
        in_specs=[any_spec] * n, out_specs=[any_spec] * n,
        scratch_shapes=[pltpu.SemaphoreType.DMA((7, n)), pltpu.SemaphoreType.DMA((7, n)),
                        pltpu.SemaphoreType.DMA((n,))],
    )(*shards)


def _pair_reduce(g, name):
    n_chips = N_DEV // 2
    R, C = g.shape[0] // N_DEV, g.shape[1]

    def body(g_ref, out_ref, mine_ref, theirs_ref, send_sems, recv_sems, local_sems):
        x, y, c = _my_position()
        block = lambda d: g_ref.at[pl.ds(pl.multiple_of(d * R, 16), R), :]
        sends = [pltpu.make_async_remote_copy(
            src_ref=block(2 * j + 1 - c), dst_ref=theirs_ref.at[j], send_sem=send_sems.at[j],
            recv_sem=recv_sems.at[j], device_id=(x, y, 1 - c), device_id_type=MESH) for j in range(n_chips)]
        own = [pltpu.make_async_copy(block(2 * j + c), mine_ref.at[j], local_sems.at[j]) for j in range(n_chips)]
        for cp in sends + own:
            cp.start()
        for j in range(n_chips):
            own[j].wait()
            sends[j].wait_recv()
            out_ref[j] = (mine_ref[j].astype(F32) + theirs_ref[j].astype(F32)).astype(g.dtype)
        for cp in sends:
            cp.wait_send()

    half = pltpu.VMEM((n_chips, R, C), g.dtype)
    sems = pltpu.SemaphoreType.DMA((n_chips,))
    return pl.pallas_call(
        body, name=name, out_shape=jax.ShapeDtypeStruct((n_chips, R, C), g.dtype),
        in_specs=[pl.BlockSpec(memory_space=pl.ANY)], out_specs=pl.BlockSpec(memory_space=pltpu.VMEM),
        scratch_shapes=[half, half, sems, sems, sems], compiler_params=_params(),
    )(g)


class _Ride:
    def __init__(self, parts):
        self.parts = [p for p in parts if p is not None]

    @staticmethod
    def gather(src, axis, land=None, lo=0, n=None):
        return ("gather", src, land, axis, lo, src.shape[axis] if n is None else n)

    @staticmethod
    def gather2(src, axis, land=None, lo=0, n=None):
        return ("gather2", src, land, axis, lo, src.shape[axis] if n is None else n)

    @staticmethod
    def scatter(src, axis, land=None, lo=0, n=None):
        return ("scatter", src, land, axis, lo, src.shape[axis] // N_DEV if n is None else n)

    @staticmethod
    def scatter_chips(chip_sums):
        return ("scatter_chips", chip_sums, None, 0, 0, chip_sums.shape[1])

    def arrays(self):
        return [p[1] for p in self.parts] + [p[2] for p in self.parts if p[2] is not None]

    def out_shapes(self):
        out = []
        for kind, src, _, axis, _, _ in self.parts:
            shape = list(src.shape)
            if kind in ("gather", "gather2"):
                shape[axis] *= N_DEV
            elif kind == "scatter_chips":
                pass
            else:
                shape[axis] //= N_DEV
                shape = [N_DEV] + shape
            out.append(jax.ShapeDtypeStruct(tuple(shape), src.dtype))
        return out

    def aliases(self, n_in, n_out):
        m, out = len(self.parts), {}
        for j, p in enumerate(self.parts):
            if p[2] is not None:
                out[n_in + m + len(out)] = n_out + j
        return out

    def scratch(self):
        m = len(self.parts)
        return [pltpu.SemaphoreType.DMA((N_DEV - 1, m)), pltpu.SemaphoreType.DMA((N_DEV - 1, m)),
                pltpu.SemaphoreType.DMA((m,))]

    def _copies(self, src_refs, land_refs, sems):
        send_sems, recv_sems, local_sems = sems
        x, y, c = _my_position()
        my_idx = 4 * x + 2 * y + c
        own, sends, relays, lands = [], [], [], []
        for j, (kind, src, _, axis, lo, n) in enumerate(self.parts):
            if kind == "scatter_chips":
                for k in (0, 2, 4, 6):
                    px, py = (1 - x if k & 4 else x), (1 - y if k & 2 else y)
                    a, b = src_refs[j].at[2 * px + py], land_refs[j].at[2 * x + y]
                    if k == 0:
                        own.append(pltpu.make_async_copy(a, b, local_sems.at[j]))
                        continue
                    mk = lambda dst, a=a, k=k, j=j, to=(px, py, c): pltpu.make_async_remote_copy(
                        src_ref=a, dst_ref=dst, send_sem=send_sems.at[k - 1, j], recv_sem=recv_sems.at[k - 1, j],
                        device_id=to, device_id_type=MESH)
                    sends.append(mk(b))
                    lands.append(mk(land_refs[j].at[2 * px + py]))
                continue
            size = src.shape[axis] if kind in ("gather", "gather2") else src.shape[axis] // N_DEV
            align = 16 if axis == 0 else LANES

            def rows(ref, idx, lead=None, axis=axis, lo=lo, n=n, size=size, align=align):
                at = pl.ds(pl.multiple_of(idx * size + lo, align), n)
                where = (at, slice(None)) if axis == 0 else (slice(None), at)
                return ref.at[where] if lead is None else ref.at[(lead, *where)]

            def in_shard(ref):
                return rows(ref, 0)

            def in_slot(ref, s):
                return rows(ref, 0, lead=s)

            if kind == "gather2":
                chips = [(1 - x, y), (x, 1 - y), (1 - x, 1 - y)]
                place = lambda px, py, pc: rows(land_refs[j], 4 * px + 2 * py + pc)

                def copy(i, a, dst, to, j=j):
                    return pltpu.make_async_remote_copy(
                        src_ref=a, dst_ref=dst, send_sem=send_sems.at[i, j], recv_sem=recv_sems.at[i, j],
                        device_id=to, device_id_type=MESH)

                mine = in_shard(src_refs[j])
                own.append(pltpu.make_async_copy(mine, place(x, y, c), local_sems.at[j]))
                sends.append(copy(0, mine, place(x, y, c), (x, y, 1 - c)))
                lands.append(copy(0, mine, place(x, y, 1 - c), (x, y, 1 - c)))
                for i, (px, py) in enumerate(chips):
                    sends.append(copy(1 + i, mine, place(x, y, c), (px, py, c)))
                    relays.append((copy(1 + i, mine, place(px, py, c), (px, py, c)),
                                   copy(4 + i, place(px, py, c), place(px, py, c), (x, y, 1 - c))))
                    lands.append(copy(4 + i, mine, place(px, py, 1 - c), (x, y, 1 - c)))
                continue
            for k in range(N_DEV):
                px = 1 - x if k & 4 else x
                py = 1 - y if k & 2 else y
                pc = 1 - c if k & 1 else c
                peer_idx = 4 * px + 2 * py + pc
                if kind == "gather":
                    a, b, landed = in_shard(src_refs[j]), rows(land_refs[j], my_idx), rows(land_refs[j], peer_idx)
                else:
                    a, b, landed = rows(src_refs[j], peer_idx), in_slot(land_refs[j], my_idx), in_slot(land_refs[j], peer_idx)
                if k == 0:
                    own.append(pltpu.make_async_copy(a, b, local_sems.at[j]))
                    continue
                mk = lambda dst, a=a, k=k, j=j, to=(px, py, pc): pltpu.make_async_remote_copy(
                    src_ref=a, dst_ref=dst, send_sem=send_sems.at[k - 1, j], recv_sem=recv_sems.at[k - 1, j],
                    device_id=to, device_id_type=MESH)
                sends.append(mk(b))
                lands.append(mk(landed))
        return own, sends, relays, lands

    @property
    def relayed(self):
        return any(p[0] == "gather2" for p in self.parts)

    def start(self, src_refs, land_refs, sems):
        own, sends, _, _ = self._copies(src_refs, land_refs, sems)
        for cp in own + sends:
            cp.start()

    def relay(self, src_refs, land_refs, sems):
        for arrival, onward in self._copies(src_refs, land_refs, sems)[2]:
            arrival.wait_recv()
            onward.start()

    def wait(self, src_refs, land_refs, sems):
        own, sends, relays, lands = self._copies(src_refs, land_refs, sems)
        for cp in lands:
            cp.wait_recv()
        for cp in sends + [onward for _, onward in relays]:
            cp.wait_send()
        for cp in own:
            cp.wait()


def _call(body, *, name, grid, in_specs, out_specs, out_shape, args, scratch_shapes=(), ride=None):
    in_specs, out_specs, out_shape = list(in_specs), list(out_specs), list(out_shape)
    n_in, n_out, n_sc = len(in_specs), len(out_specs), len(scratch_shapes)
    if ride is None or not ride.parts:
        res = pl.pallas_call(body, name=name, grid=grid, in_specs=in_specs, out_specs=out_specs,
                             out_shape=out_shape, scratch_shapes=list(scratch_shapes),
                             compiler_params=_params(len(grid)))(*args)
        return list(res), []
    extra, m = ride.arrays(), len(ride.parts)

    def riding(*refs):
        a = n_in + len(extra)
        b = a + n_out
        srcs, lands, sems = refs[n_in:n_in + m], refs[b:b + m], refs[b + m + n_sc:]
        at = [pl.program_id(d) for d in range(len(grid))]

        @pl.when(functools.reduce(jnp.logical_and, [i == 0 for i in at]))
        def _():
            ride.start(srcs, lands, sems)

        if ride.relayed:
            step, n_steps = at[0], 1
            for i, g in zip(at[1:], grid[1:]):
                step = step * g + i
            for g in grid:
                n_steps *= g
            assert n_steps >= 2, "a two-level ride needs a grid step after the first"

            @pl.when(step == n_steps - 1)
            def _():
                ride.relay(srcs, lands, sems)

        body(*refs[:n_in], *refs[a:b], *refs[b + m:b + m + n_sc])

        @pl.when(functools.reduce(jnp.logical_and, [i == g - 1 for i, g in zip(at, grid)]))
        def _():
            ride.wait(srcs, lands, sems)

    hbm = pl.BlockSpec(memory_space=pl.ANY)
    res = pl.pallas_call(
        riding, name=name, grid=grid, in_specs=in_specs + [hbm] * len(extra), out_specs=out_specs + [hbm] * m,
        out_shape=out_shape + ride.out_shapes(), scratch_shapes=list(scratch_shapes) + ride.scratch(),
        input_output_aliases=ride.aliases(n_in, n_out), compiler_params=_params(len(grid)),
    )(*args, *extra)
    return list(res[:n_out]), list(res[n_out:])


def _prenorm_inproj(h, gain, w_in_t, name, ride=None):
    T = h.shape[0]

    def body(h_ref, g_ref, w_ref, q_ref, k_ref, v_ref, ug_ref, hn_ref):
        hv = h_ref[...]
        r = lax.rsqrt(jnp.mean(hv * hv, axis=-1, keepdims=True) + EPS)
        hn = (hv * r * g_ref[...]).astype(BF16)
        hn_ref[...] = hn
        for j in range(N_CHUNK):
            u = _nt(hn, w_ref[j * CHUNK:(j + 1) * CHUNK, :])
            if j == 0:
                q_ref[...] = (u * (HEAD_DIM ** -0.5)).astype(BF16)
            elif j == 1:
                k_ref[...] = u.astype(BF16)
            elif j == 2:
                v_ref[...] = u.astype(BF16)
            else:
                ug_ref[:, (j - 3) * CHUNK:(j - 2) * CHUNK] = u.astype(BF16)

    act = jax.ShapeDtypeStruct((T, CHUNK), BF16)
    return _call(
        body, name=name, grid=(T // TM,),
        in_specs=[_rows(TM, D_MODEL), _whole((1, D_MODEL)), _whole((N_CHUNK * CHUNK, D_MODEL))],
        out_specs=[_rows(TM, CHUNK)] * 3 + [_rows(TM, 4 * CHUNK), _rows(TM, D_MODEL)],
        out_shape=[act, act, act, jax.ShapeDtypeStruct((T, 4 * CHUNK), BF16),
                   jax.ShapeDtypeStruct((T, D_MODEL), BF16)],
        args=(h, gain, w_in_t,), ride=ride)


def _softplus_parts(z):
    ez = jnp.exp(jnp.minimum(z, SOFTPLUS_LINEAR_AT))
    t = 1.0 + ez
    return ez * pl.reciprocal(t, approx=True), jnp.where(z > SOFTPLUS_LINEAR_AT, z, jnp.log(t))


def _attn_fwd(qs, k, v, tri, name, ride=None):
    T = qs.shape[0]
    assert T // BLK <= FIRST_BLOCK_LANE, "one lane per key block below the lane of the first block"
    width = LANES * ATT_COLS
    chains = [(c, half) for c in range(ATT_COLS) for half in range(2)]

    def body(q_ref, k_ref, v_ref, m_ref, o_ref, cs_ref):
        qi = pl.program_id(1)
        lane = lax.broadcasted_iota(jnp.int32, (BLK, LANES), 1)
        first = lane < HEAD_DIM
        causal = (lax.broadcasted_iota(jnp.int32, (BLK, BLK), 1)
                  < lax.broadcasted_iota(jnp.int32, (BLK, BLK), 0))
        tri_m = m_ref[...]
        qh = {}
        for c in range(ATT_COLS):
            q = q_ref[:, c * LANES:(c + 1) * LANES]
            zero = jnp.zeros_like(q)
            qh[c, 0], qh[c, 1] = jnp.where(first, q, zero), jnp.where(first, zero, q)

        def step(kb, state, masked):
            carries, accs, cvals = state
            start = pl.multiple_of(kb * BLK, BLK)
            kblk = [k_ref[pl.ds(start, BLK), c * LANES:(c + 1) * LANES] for c in range(ATT_COLS)]
            vblk = [v_ref[pl.ds(start, BLK), c * LANES:(c + 1) * LANES] for c in range(ATT_COLS)]
            carries, accs, cvals = list(carries), list(accs), list(cvals)
            for g0 in range(0, len(chains), CHAIN_GROUP):
                ids = range(g0, g0 + CHAIN_GROUP)
                z = [_nt(qh[chains[n]], kblk[chains[n][0]]) for n in ids]
                sp = [_softplus_parts(zi)[1] for zi in z]
                if masked:
                    sp = [jnp.where(causal, s, 0.0) for s in sp]
                incl = [_dot_hilo(s, tri_m) for s in sp]
                a = [jnp.exp(zi - ii - carries[n]) for n, zi, ii in zip(ids, z, incl)]
                if masked:
                    a = [jnp.where(causal, ai, 0.0) for ai in a]
                for n, ai, ii in zip(ids, a, incl):
                    c, half = chains[n]
                    zero = jnp.zeros_like(vblk[c])
                    vh = jnp.where(first, vblk[c], zero) if half == 0 else jnp.where(first, zero, vblk[c])
                    accs[c] = accs[c] + _nn(ai.astype(BF16), vh)
                    cvals[c] = jnp.where(lane == kb + HEAD_DIM * half, carries[n], cvals[c])
                    carries[n] = carries[n] + ii[:, 0:1]
            return tuple(carries), tuple(accs), tuple(cvals)

        zeros = tuple(jnp.zeros((BLK, LANES), F32) for _ in range(ATT_COLS))
        state = (tuple(jnp.zeros((BLK, 1), F32) for _ in chains), zeros, zeros)
        state = step(qi, state, True)

        def reaches_further(st):
            it, (carries, _, _) = st
            least = functools.reduce(jnp.minimum, carries)
            return jnp.logical_and(it < qi, jnp.min(least) < DEAD_AT)

        done, state = lax.while_loop(reaches_further, lambda st: (st[0] + 1, step(qi - 1 - st[0], st[1], False)),
                                     (jnp.int32(0), state))
        first_block = (qi - done).astype(F32)
        for c in range(ATT_COLS):
            o_ref[:, c * LANES:(c + 1) * LANES] = state[1][c]
            cs_ref[:, c * LANES:(c + 1) * LANES] = jnp.where(lane == FIRST_BLOCK_LANE, first_block, state[2][c])

    blk = pl.BlockSpec((BLK, width), lambda j, i: (i, j))
    col = pl.BlockSpec((T, width), lambda j, i: (0, j))
    out = jax.ShapeDtypeStruct((T, ATTN_DIM), F32)
    return _call(
        body, name=name, grid=(ATTN_DIM // width, T // BLK),
        in_specs=[blk, col, col, _whole((BLK, BLK))],
        out_specs=[blk, blk], out_shape=[out, out],
        args=(qs, k, v, tri,), ride=ride)


def _shifted_copies(pad_ref, sh_ref):
    rows = sh_ref.shape[1]
    for b in range(SUBLANES):
        sh_ref[b] = pad_ref[b:b + rows, :]


def _shift_of(offset):
    return offset % SUBLANES, offset - offset % SUBLANES


def _conv_fwd(ug, dw_w, dw_b, ln_g, ln_b, name, ride=None):
    T = ug.shape[0]
    per = TM // HALO

    def body(cv_ref, cg_ref, cvh_ref, cgh_ref, w_ref, b_ref, g_ref, beta_ref, conv_ref, c2_ref, pad_ref, sh_ref):
        i = pl.program_id(0)
        halo = cvh_ref[...].astype(F32) * _sigmoid(cgh_ref[...].astype(F32))
        pad_ref[0:HALO, :] = jnp.where(i == 0, 0.0, halo)
        pad_ref[HALO:HALO + TM, :] = cv_ref[...].astype(F32) * _sigmoid(cg_ref[...].astype(F32))
        pad_ref[HALO + TM:, :] = jnp.zeros((SUBLANES, CONV_DIM), F32)
        _shifted_copies(pad_ref, sh_ref)
        taps = [w_ref[t:t + 1, :] for t in range(CONV_WIDTH)]

        def rows(j, _):
            r = pl.multiple_of(j * CONV_ROWS, CONV_ROWS)
            acc = jnp.zeros((CONV_ROWS, CONV_DIM), F32) + b_ref[...]
            for t in range(CONV_WIDTH):
                b, a = _shift_of(HALO - (CONV_WIDTH - 1) + t)
                acc = acc + taps[t] * sh_ref[b, pl.ds(r + a, CONV_ROWS), :]
            conv_ref[pl.ds(r, CONV_ROWS), :] = acc
            return 0

        lax.fori_loop(0, TM // CONV_ROWS, rows, 0)
        acc = conv_ref[...]
        mu = jnp.mean(acc, axis=-1, keepdims=True)
        xc = acc - mu
        rs = lax.rsqrt(jnp.mean(xc * xc, axis=-1, keepdims=True) + EPS)
        ln = xc * rs * g_ref[...] + beta_ref[...]
        c2_ref[...] = (ln * _sigmoid(ln)).astype(BF16)

    prev = lambda col: pl.BlockSpec((HALO, CHUNK), lambda i: (jnp.maximum(i * per - 1, 0), col))
    vec = _whole((1, CONV_DIM))
    return _call(
        body, name=name, grid=(T // TM,),
        in_specs=[_rows(TM, CHUNK, 1), _rows(TM, CHUNK, 2), prev(1), prev(2),
                  _whole((CONV_WIDTH, CONV_DIM)), vec, vec, vec],
        out_specs=[_rows(TM, CONV_DIM), _rows(TM, CONV_DIM)],
        out_shape=[jax.ShapeDtypeStruct((T, CONV_DIM), F32), jax.ShapeDtypeStruct((T, CONV_DIM), BF16)],
        scratch_shapes=[pltpu.VMEM((TM + HALO + SUBLANES, CONV_DIM), F32),
                        pltpu.VMEM((SUBLANES, TM + HALO, CONV_DIM), F32)],
        args=(ug, ug, ug, ug, dw_w, dw_b, ln_g, ln_b,), ride=ride)


def _mix_out_ple(o, ug, c2, h, p, head_mean, g_attn, g_conv, g_ple, w_pw, w_out, w_gate, w_ple, name, ride=None):
    T = h.shape[0]

    def body(o_ref, ga_ref, gc_ref, c2_ref, h_ref, p_ref, hm_ref, gao_ref, gco_ref, gpn_ref,
             wpw_ref, wout_ref, wg_ref, wple_ref,
             h2_ref, h1_ref, ycat_ref, hn2_ref, gate_ref, e_ref, c3_ref):
        ov = o_ref[...]
        rh = lax.rsqrt(_nn((ov * ov).astype(BF16), hm_ref[...]) + EPS)
        ga = ga_ref[...].astype(F32)
        ya = (ov * rh * gao_ref[...] * (ga * _sigmoid(ga))).astype(BF16)
        c3 = _nn(c2_ref[...], wpw_ref[...])
        c3_ref[...] = c3
        rc = lax.rsqrt(jnp.mean(c3 * c3, axis=-1, keepdims=True) + EPS)
        gc = gc_ref[...].astype(F32)
        yc = (c3 * rc * gco_ref[...] * (gc * _sigmoid(gc))).astype(BF16)
        ycat_ref[:, :ATTN_DIM] = ya
        ycat_ref[:, ATTN_DIM:] = yc
        h1 = h_ref[...] + _nn(ya, wout_ref[:ATTN_DIM, :]) + _nn(yc, wout_ref[ATTN_DIM:, :])
        h1_ref[...] = h1
        r1 = lax.rsqrt(jnp.mean(h1 * h1, axis=-1, keepdims=True) + EPS)
        hn2 = (h1 * r1 * gpn_ref[...]).astype(BF16)
        hn2_ref[...] = hn2
        gate = _sigmoid(_nn(hn2, wg_ref[...]))
        e = _nn(p_ref[...].astype(BF16), wple_ref[...])
        gate_ref[...] = gate
        e_ref[...] = e
        h2_ref[...] = h1 + e * gate

    f32 = lambda cols: jax.ShapeDtypeStruct((T, cols), F32)
    bf = lambda cols: jax.ShapeDtypeStruct((T, cols), BF16)
    return _call(
        body, name=name, grid=(T // TM,),
        in_specs=[_rows(TM, ATTN_DIM), _rows(TM, CHUNK, 0), _rows(TM, CHUNK, 3), _rows(TM, CONV_DIM),
                  _rows(TM, D_MODEL), _rows(TM, PLE_DIM), _whole((ATTN_DIM, ATTN_DIM)),
                  _whole((1, ATTN_DIM)), _whole((1, CONV_DIM)), _whole((1, D_MODEL)),
                  _whole((CONV_DIM, CONV_DIM)), _whole((D_MODEL, D_MODEL)), _whole((D_MODEL, D_MODEL)),
                  _whole((PLE_DIM, D_MODEL))],
        out_specs=[_rows(TM, D_MODEL), _rows(TM, D_MODEL), _rows(TM, D_MODEL), _rows(TM, D_MODEL),
                   _rows(TM, D_MODEL), _rows(TM, D_MODEL), _rows(TM, CONV_DIM)],
        out_shape=[f32(D_MODEL), f32(D_MODEL), bf(D_MODEL), bf(D_MODEL), f32(D_MODEL), f32(D_MODEL),
                   f32(CONV_DIM)],
        args=(o, ug, ug, c2, h, p, head_mean, g_attn, g_conv, g_ple, w_pw, w_out, w_gate, w_ple,), ride=ride)


def _final_loss(h, target, gain, name):
    T = h.shape[0]

    def body(h_ref, t_ref, g_ref, dh_ref, gsum_ref, loss_ref):
        @pl.when(pl.program_id(0) == 0)
        def _():
            gsum_ref[...] = jnp.zeros_like(gsum_ref)
            loss_ref[...] = jnp.zeros_like(loss_ref)

        hv = h_ref[...]
        r = lax.rsqrt(jnp.mean(hv * hv, axis=-1, keepdims=True) + EPS)
        xh = hv * r
        diff = xh * g_ref[...] - t_ref[...]
        loss_ref[...] += 0.5 * jnp.sum(jnp.mean(diff * diff, axis=-1, keepdims=True), axis=0, keepdims=True)
        dy = diff * (1.0 / D_MODEL)
        gsum_ref[...] += jnp.sum(dy * xh, axis=0, keepdims=True)
        dxh = dy * g_ref[...]
        dh_ref[...] = r * (dxh - xh * jnp.mean(dxh * xh, axis=-1, keepdims=True))

    return pl.pallas_call(
        body, name=name, grid=(T // TM,),
        in_specs=[_rows(TM, D_MODEL), _rows(TM, D_MODEL), _whole((1, D_MODEL))],
        out_specs=[_rows(TM, D_MODEL), _whole((1, D_MODEL)), _whole((1, LANES))],
        out_shape=[jax.ShapeDtypeStruct((T, D_MODEL), F32), jax.ShapeDtypeStruct((1, D_MODEL), F32),
                   jax.ShapeDtypeStruct((1, LANES), F32)],
        compiler_params=_params(1),
    )(h, target, gain)


def _ple_out_bwd(dh2, gate, e, h1, g_ple, w_gate, w_out, name, ride=None):
    T = dh2.shape[0]

    def body(dh2_ref, gate_ref, e_ref, h1_ref, gpn_ref, wg_ref, wout_ref,
             dh1_ref, dh1b_ref, dzg_ref, de_ref, dycat_ref, gsum_ref):
        @pl.when(pl.program_id(0) == 0)
        def _():
            gsum_ref[...] = jnp.zeros_like(gsum_ref)

        dh2v = dh2_ref[...]
        gate = gate_ref[...]
        de_ref[...] = (dh2v * gate).astype(BF16)
        dzg = (dh2v * e_ref[...] * gate * (1.0 - gate)).astype(BF16)
        dzg_ref[...] = dzg
        dhn2 = _nt(dzg, wg_ref[...])
        h1 = h1_ref[...]
        r1 = lax.rsqrt(jnp.mean(h1 * h1, axis=-1, keepdims=True) + EPS)
        xh = h1 * r1
        gsum_ref[...] += jnp.sum(dhn2 * xh, axis=0, keepdims=True)
        dxh = dhn2 * gpn_ref[...]
        dh1 = dh2v + r1 * (dxh - xh * jnp.mean(dxh * xh, axis=-1, keepdims=True))
        dh1_ref[...] = dh1
        dh1b = dh1.astype(BF16)
        dh1b_ref[...] = dh1b
        dycat_ref[...] = _nt(dh1b, wout_ref[...])

    f32 = jax.ShapeDtypeStruct((T, D_MODEL), F32)
    bf = jax.ShapeDtypeStruct((T, D_MODEL), BF16)
    full = _rows(TM, D_MODEL)
    return _call(
        body, name=name, grid=(T // TM,),
        in_specs=[full, full, full, full, _whole((1, D_MODEL)), _whole((D_MODEL, D_MODEL)),
                  _whole((D_MODEL, D_MODEL))],
        out_specs=[full, full, full, full, full, _whole((1, D_MODEL))],
        out_shape=[f32, bf, bf, bf, f32, jax.ShapeDtypeStruct((1, D_MODEL), F32)],
        args=(dh2, gate, e, h1, g_ple, w_gate, w_out,), ride=ride)


def _branch_bwd(dycat, o, ug, c3, conv, head_mean, g_attn, g_conv, ln_g, ln_b, w_pw, name, ride=None):
    T = o.shape[0]

    def body(dya_ref, dyc_ref, o_ref, ga_ref, gc_ref, c3_ref, conv_ref, hm_ref, gao_ref, gco_ref,
             lng_ref, lnb_ref, wpw_ref,
             do_ref, dga_ref, dgc_ref, dc3_ref, dconv_ref, sums_ref):
        @pl.when(pl.program_id(0) == 0)
        def _():
            sums_ref[...] = jnp.zeros_like(sums_ref)

        hm = hm_ref[...]
        col = lambda x: jnp.sum(x, axis=0, keepdims=True)
        ov = o_ref[...]
        rh = lax.rsqrt(_nn((ov * ov).astype(BF16), hm) + EPS)
        xh = ov * rh
        ga = ga_ref[...].astype(F32)
        sg = _sigmoid(ga)
        dya = dya_ref[...]
        don = dya * (ga * sg)
        dga_ref[...] = (dya * xh * gao_ref[...] * _dsilu(ga, sg)).astype(BF16)
        sums_ref[0:1, :] += col(don * xh)
        dxh = don * gao_ref[...]
        do_ref[...] = (rh * (dxh - xh * _dot_hilo(dxh * xh, hm))).astype(BF16)
        c3 = c3_ref[...]
        rc = lax.rsqrt(jnp.mean(c3 * c3, axis=-1, keepdims=True) + EPS)
        xh3 = c3 * rc
        gc = gc_ref[...].astype(F32)
        sgc = _sigmoid(gc)
        dyc = dyc_ref[...]
        dn3 = dyc * (gc * sgc)
        dgc_ref[...] = (dyc * xh3 * gco_ref[...] * _dsilu(gc, sgc)).astype(BF16)
        sums_ref[1:2, :] += col(dn3 * xh3)
        dxh3 = dn3 * gco_ref[...]
        dc3 = (rc * (dxh3 - xh3 * jnp.mean(dxh3 * xh3, axis=-1, keepdims=True))).astype(BF16)
        dc3_ref[...] = dc3
        dc2 = _nt(dc3, wpw_ref[...])
        cv = conv_ref[...]
        mu = jnp.mean(cv, axis=-1, keepdims=True)
        xc = cv - mu
        rs = lax.rsqrt(jnp.mean(xc * xc, axis=-1, keepdims=True) + EPS)
        xn = xc * rs
        ln = xn * lng_ref[...] + lnb_ref[...]
        dln = dc2 * _dsilu(ln, _sigmoid(ln))
        sums_ref[2:3, :] += col(dln * xn)
        sums_ref[3:4, :] += col(dln)
        dxn = dln * lng_ref[...]
        dconv = rs * (dxn - jnp.mean(dxn, axis=-1, keepdims=True)
                      - xn * jnp.mean(dxn * xn, axis=-1, keepdims=True))
        dconv_ref[...] = dconv
        sums_ref[4:5, :] += col(dconv)

    half = lambda dt: jax.ShapeDtypeStruct((T, CHUNK), dt)
    tile = _rows(TM, CHUNK)
    vec = _whole((1, CHUNK))
    return _call(
        body, name=name, grid=(T // TM,),
        in_specs=[_rows(TM, CHUNK, 0), _rows(TM, CHUNK, 1), tile, _rows(TM, CHUNK, 0), _rows(TM, CHUNK, 3),
                  tile, tile, _whole((ATTN_DIM, ATTN_DIM)), vec, vec, vec, vec, _whole((CONV_DIM, CONV_DIM))],
        out_specs=[tile, tile, tile, tile, tile, _whole((8, CHUNK))],
        out_shape=[half(BF16), half(BF16), half(BF16), half(BF16), half(F32),
                   jax.ShapeDtypeStruct((8, CHUNK), F32)],
        args=(dycat, dycat, o, ug, ug, c3, conv, head_mean, g_attn, g_conv, ln_g, ln_b, w_pw,), ride=ride)


def _conv_bwd(dconv, ug, dw_w, name, ride=None):
    T = dconv.shape[0]
    per = TM // HALO
    last = T // HALO - 1
    n_tiles = T // TM

    def body(d_ref, dn_ref, cv_ref, cg_ref, cvh_ref, cgh_ref, w_ref, dcv_ref, dcg_ref, dw_ref,
             dpad_ref, cpad_ref, dsh_ref, csh_ref, dw_acc):
        i = pl.program_id(0)

        @pl.when(i == 0)
        def _():
            dw_acc[...] = jnp.zeros_like(dw_acc)

        tail = jnp.zeros((SUBLANES, CONV_DIM), F32)
        dpad_ref[0:TM, :] = d_ref[...]
        dpad_ref[TM:TM + HALO, :] = jnp.where(i == n_tiles - 1, 0.0, dn_ref[...])
        dpad_ref[TM + HALO:, :] = tail
        halo = cvh_ref[...].astype(F32) * _sigmoid(cgh_ref[...].astype(F32))
        cpad_ref[0:HALO, :] = jnp.where(i == 0, 0.0, halo)
        cpad_ref[HALO:HALO + TM, :] = cv_ref[...].astype(F32) * _sigmoid(cg_ref[...].astype(F32))
        cpad_ref[HALO + TM:, :] = tail
        _shifted_copies(dpad_ref, dsh_ref)
        _shifted_copies(cpad_ref, csh_ref)
        taps = [w_ref[t:t + 1, :] for t in range(CONV_WIDTH)]

        def rows(j, _):
            r = pl.multiple_of(j * CONV_ROWS, CONV_ROWS)
            d = d_ref[pl.ds(r, CONV_ROWS), :]
            dc = jnp.zeros((CONV_ROWS, CONV_DIM), F32)
            for t in range(CONV_WIDTH):
                b, a = _shift_of(CONV_WIDTH - 1 - t)
                dc = dc + taps[t] * dsh_ref[b, pl.ds(r + a, CONV_ROWS), :]
                b, a = _shift_of(HALO - (CONV_WIDTH - 1) + t)
                prod = d * csh_ref[b, pl.ds(r + a, CONV_ROWS), :]
                dw_acc[t] += jnp.sum(prod.reshape(CONV_ROWS // SUBLANES, SUBLANES, CONV_DIM), axis=0)
            cv = cv_ref[pl.ds(r, CONV_ROWS), :].astype(F32)
            sg = _sigmoid(cg_ref[pl.ds(r, CONV_ROWS), :].astype(F32))
            dcv_ref[pl.ds(r, CONV_ROWS), :] = (dc * sg).astype(BF16)
            dcg_ref[pl.ds(r, CONV_ROWS), :] = (dc * cv * sg * (1.0 - sg)).astype(BF16)
            return 0

        lax.fori_loop(0, TM // CONV_ROWS, rows, 0)

        @pl.when(i == n_tiles - 1)
        def _():
            dw_ref[...] = jnp.zeros_like(dw_ref)
            for t in range(CONV_WIDTH):
                dw_ref[t:t + 1, :] = jnp.sum(dw_acc[t], axis=0, keepdims=True)

    prev = lambda col: pl.BlockSpec((HALO, CHUNK), lambda i: (jnp.maximum(i * per - 1, 0), col))
    nxt = pl.BlockSpec((HALO, CONV_DIM), lambda i: (jnp.minimum((i + 1) * per, last), 0))
    half = jax.ShapeDtypeStruct((T, CHUNK), BF16)
    return _call(
        body, name=name, grid=(T // TM,),
        in_specs=[_rows(TM, CONV_DIM), nxt, _rows(TM, CHUNK, 1), _rows(TM, CHUNK, 2), prev(1), prev(2),
                  _whole((CONV_WIDTH, CONV_DIM))],
        out_specs=[_rows(TM, CHUNK), _rows(TM, CHUNK), _whole((HALO, CONV_DIM))],
        out_shape=[half, half, jax.ShapeDtypeStruct((HALO, CONV_DIM), F32)],
        scratch_shapes=[pltpu.VMEM((TM + HALO + SUBLANES, CONV_DIM), F32),
                        pltpu.VMEM((TM + HALO + SUBLANES, CONV_DIM), F32),
                        pltpu.VMEM((SUBLANES, TM + HALO, CONV_DIM), F32),
                        pltpu.VMEM((SUBLANES, TM + HALO, CONV_DIM), F32),
                        pltpu.VMEM((HALO, SUBLANES, CONV_DIM), F32)],
        args=(dconv, dconv, ug, ug, ug, ug, dw_w,), ride=ride)


def _attn_bwd(qs, k, v, do, cs, tri, tri_t, name, ride=None):
    T = qs.shape[0]
    nq = T // BLK
    width = LANES * ATT_COLS
    chains = [(c, half) for c in range(ATT_COLS) for half in range(2)]

    def body(q_ref, k_ref, v_ref, do_ref, cs_ref, m_ref, mt_ref, dq_ref, dk_ref, dv_ref, dk_acc, dv_acc):
        qi = pl.program_id(1)

        @pl.when(qi == 0)
        def _():
            dk_acc[...] = jnp.zeros_like(dk_acc)
            dv_acc[...] = jnp.zeros_like(dv_acc)

        lane = lax.broadcasted_iota(jnp.int32, (BLK, LANES), 1)
        first = lane < HEAD_DIM
        causal = (lax.broadcasted_iota(jnp.int32, (BLK, BLK), 1)
                  < lax.broadcasted_iota(jnp.int32, (BLK, BLK), 0))
        tri_m = m_ref[...]
        tri_mt = mt_ref[...]

        def halves(x):
            zero = jnp.zeros_like(x)
            return jnp.where(first, x, zero), jnp.where(first, zero, x)

        qh, doh, cs = {}, {}, []
        for c in range(ATT_COLS):
            qh[c, 0], qh[c, 1] = halves(q_ref[:, c * LANES:(c + 1) * LANES])
            doh[c, 0], doh[c, 1] = halves(do_ref[:, c * LANES:(c + 1) * LANES])
            cs.append(cs_ref[:, c * LANES:(c + 1) * LANES])

        def step(kb, state, masked):
            prefixes, dq_accs = state
            start = pl.multiple_of(kb * BLK, BLK)
            kblk = [k_ref[pl.ds(start, BLK), c * LANES:(c + 1) * LANES] for c in range(ATT_COLS)]
            vblk = [v_ref[pl.ds(start, BLK), c * LANES:(c + 1) * LANES] for c in range(ATT_COLS)]
            prefixes, dq_accs = list(prefixes), list(dq_accs)
            for g0 in range(0, len(chains), CHAIN_GROUP):
                ids = range(g0, g0 + CHAIN_GROUP)
                grp = [chains[n] for n in ids]
                z = [_nt(qh[ch], kblk[ch[0]]) for ch in grp]
                da = [_nt(doh[ch], vblk[ch[0]]) for ch in grp]
                parts = [_softplus_parts(zi) for zi in z]
                sp = [pt[1] for pt in parts]
                if masked:
                    sp = [jnp.where(causal, s, 0.0) for s in sp]
                incl = [_dot_hilo(s, tri_m) for s in sp]
                carries = [jnp.sum(jnp.where(lane == kb + HEAD_DIM * half, cs[c], 0.0), axis=1, keepdims=True)
                           for c, half in grp]
                a = [jnp.exp(zi - ii - ci) for zi, ii, ci in zip(z, incl, carries)]
                if masked:
                    a = [jnp.where(causal, ai, 0.0) for ai in a]
                w = [ai * di for ai, di in zip(a, da)]
                pinc = [_nn(wi.astype(BF16), tri_mt) for wi in w]
                dz = [wi - pt[0] * (pi + prefixes[n]) for n, wi, pt, pi in zip(ids, w, parts, pinc)]
                if masked:
                    dz = [jnp.where(causal, d, 0.0) for d in dz]
                for j in range(0, CHAIN_GROUP, 2):
                    c = grp[j][0]
                    k0, k1 = halves(kblk[c])
                    dz0, dz1 = dz[j].astype(BF16), dz[j + 1].astype(BF16)
                    a0, a1 = a[j].astype(BF16), a[j + 1].astype(BF16)
                    dq_accs[c] = dq_accs[c] + _nn(dz0, k0) + _nn(dz1, k1)
                    dk_acc[pl.ds(start, BLK), c * LANES:(c + 1) * LANES] += _tn(dz0, qh[c, 0]) + _tn(dz1, qh[c, 1])
                    dv_acc[pl.ds(start, BLK), c * LANES:(c + 1) * LANES] += _tn(a0, doh[c, 0]) + _tn(a1, doh[c, 1])
                for n, pi in zip(ids, pinc):
                    prefixes[n] = prefixes[n] + pi[:, BLK - 1:BLK]
            return tuple(prefixes), tuple(dq_accs)

        state = (tuple(jnp.zeros((BLK, 1), F32) for _ in chains),
                 tuple(jnp.zeros((BLK, LANES), F32) for _ in range(ATT_COLS)))
        first_block = jnp.max(jnp.where(lane == FIRST_BLOCK_LANE, cs[0], 0.0)).astype(jnp.int32)
        state = lax.fori_loop(first_block, qi, lambda kb, st: step(kb, st, False), state)
        state = step(qi, state, True)
        for c in range(ATT_COLS):
            dq_ref[:, c * LANES:(c + 1) * LANES] = (state[1][c] * (HEAD_DIM ** -0.5)).astype(BF16)

        @pl.when(qi == nq - 1)
        def _():
            dk_ref[...] = dk_acc[...].astype(BF16)
            dv_ref[...] = dv_acc[...].astype(BF16)

    blk = pl.BlockSpec((BLK, width), lambda j, i: (i, j))
    col = pl.BlockSpec((T, width), lambda j, i: (0, j))
    out = jax.ShapeDtypeStruct((T, ATTN_DIM), BF16)
    return _call(
        body, name=name, grid=(ATTN_DIM // width, nq),
        in_specs=[blk, col, col, blk, blk, _whole((BLK, BLK)), _whole((BLK, BLK))],
        out_specs=[blk, col, col], out_shape=[out, out, out],
        scratch_shapes=[pltpu.VMEM((T, width), F32), pltpu.VMEM((T, width), F32)],
        args=(qs, k, v, do, cs, tri, tri_t,), ride=ride)


def _inproj_bwd(du, w_in_t, h, dh1, gain, name, ride=None):
    T = h.shape[0]

    def body(*refs):
        du_refs = refs[:N_CHUNK]
        w_ref, h_ref, dh1_ref, g_ref, dh_ref, gsum_ref = refs[N_CHUNK:]

        @pl.when(pl.program_id(0) == 0)
        def _():
            gsum_ref[...] = jnp.zeros_like(gsum_ref)

        dhn = jnp.zeros((TM, D_MODEL), F32)
        for j in range(N_CHUNK):
            dhn = dhn + _nn(du_refs[j][...], w_ref[j * CHUNK:(j + 1) * CHUNK, :])
        hv = h_ref[...]
        r = lax.rsqrt(jnp.mean(hv * hv, axis=-1, keepdims=True) + EPS)
        xh = hv * r
        gsum_ref[...] += jnp.sum(dhn * xh, axis=0, keepdims=True)
        dxh = dhn * g_ref[...]
        dh_ref[...] = dh1_ref[...] + r * (dxh - xh * jnp.mean(dxh * xh, axis=-1, keepdims=True))

    full = _rows(TM, D_MODEL)
    return _call(
        body, name=name, grid=(T // TM,),
        in_specs=[_rows(TM, CHUNK)] * N_CHUNK + [_whole((N_CHUNK * CHUNK, D_MODEL)), full, full,
                                                 _whole((1, D_MODEL))],
        out_specs=[full, _whole((1, D_MODEL))],
        out_shape=[jax.ShapeDtypeStruct((T, D_MODEL), F32), jax.ShapeDtypeStruct((1, D_MODEL), F32)],
        args=(*du, w_in_t, h, dh1, gain), ride=ride)


def _weight_grad(lhs_list, rhs, name, tk=CHUNK, ride=None):
    T, n_rhs = rhs.shape
    n = len(lhs_list)
    ka = lhs_list[0].shape[1]
    per = ka // tk

    def body(*refs):
        a_refs, b_ref, out_ref = refs[:n], refs[n], refs[n + 1]
        step = pl.program_id(0)
        for j in range(n):
            for s in range(per):
                @pl.when(step == j * per + s)
                def _(j=j, s=s):
                    out_ref[...] = _tn(a_refs[j][:, s * tk:(s + 1) * tk], b_ref[...]).astype(BF16)

    (grad,), landed = _call(
        body, name=name, grid=(n * per,),
        in_specs=[_whole((T, ka))] * n + [_whole((T, n_rhs))],
        out_specs=[pl.BlockSpec((tk, n_rhs), lambda i: (i, 0))],
        out_shape=[jax.ShapeDtypeStruct((n * ka, n_rhs), BF16)],
        args=(*lhs_list, rhs), ride=ride)
    return grad, landed


def _adamw_update(w, g, m, v):
    nm = ADAM_B1 * m + (1.0 - ADAM_B1) * g
    nv = ADAM_B2 * v + (1.0 - ADAM_B2) * (g * g)
    m_hat = nm / (1.0 - ADAM_B1 ** ADAM_STEP)
    v_hat = nv / (1.0 - ADAM_B2 ** ADAM_STEP)
    return -ADAM_LR * (m_hat / (jnp.sqrt(v_hat) + ADAM_EPS) + ADAM_WD * w), nm, nv


def _sum_adamw(slots, w, m, v, name):
    depth, R, C = w.shape
    tr = next(rows for rows in ADAMW_ROWS if R % rows == 0)

    def body(*refs):
        slot_refs, (w_ref, m_ref, v_ref, g_ref, d_ref, nm_ref, nv_ref) = refs[:depth], refs[depth:]
        for layer in range(depth):
            @pl.when(pl.program_id(0) == layer)
            def _(src=slot_refs[layer]):
                g = src[0].astype(F32)
                for s in range(1, src.shape[0]):
                    g = g + src[s].astype(F32)
                g_ref[0] = g
                d_ref[0], nm_ref[0], nv_ref[0] = _adamw_update(w_ref[0], g, m_ref[0], v_ref[0])

    slot_spec = lambda layer: pl.BlockSpec((slots[layer].shape[0], tr, C),
                                           lambda l, i: (0, jnp.where(l == layer, i, 0), 0))
    spec = pl.BlockSpec((1, tr, C), lambda l, i: (l, i, 0))
    out = jax.ShapeDtypeStruct((depth, R, C), F32)
    return pl.pallas_call(
        body, name=name, grid=(depth, R // tr),
        in_specs=[slot_spec(layer) for layer in range(depth)] + [spec] * 3,
        out_specs=[spec] * 4, out_shape=[out] * 4,
        compiler_params=_params(2),
    )(*slots, w, m, v)


def _adamw(w, g, m, v, name):
    R, C = w.shape
    tr = R
    for cand in (512, 256, 128, 64):
        if R % cand == 0 and R > cand:
            tr = cand
            break

    def body(w_ref, g_ref, m_ref, v_ref, d_ref, nm_ref, nv_ref):
        d_ref[...], nm_ref[...], nv_ref[...] = _adamw_update(w_ref[...], g_ref[...], m_ref[...], v_ref[...])

    spec = pl.BlockSpec((tr, C), lambda i: (i, 0))
    out = jax.ShapeDtypeStruct((R, C), F32)
    return pl.pallas_call(
        body, name=name, grid=(R // tr,),
        in_specs=[spec] * 4, out_specs=[spec] * 3, out_shape=[out, out, out],
        compiler_params=_params(1),
    )(w, g, m, v)


def _pack_small(values, scalar=None):
    pad = lambda a: jnp.pad(a, ((0, 0), (0, D_MODEL - a.shape[1])))
    last = jnp.zeros((1, D_MODEL), F32) if scalar is None else pad(scalar.reshape(1, 1))
    return jnp.concatenate([pad(values[name].reshape(rows, cols)) for name, _, rows, cols in SMALL_LAYOUT] + [last],
                           axis=0)


def _small_update(all_packs, state, name):
    n = len(SMALL_LAYOUT)

    def body(packs_ref, *refs):
        ins, outs = refs[:3 * n], refs[3 * n:]
        total = packs_ref[0]
        for s in range(1, N_DEV):
            total = total + packs_ref[s]
        for j, (_, at, rows, cols) in enumerate(SMALL_LAYOUT):
            g = total[at:at + rows, :cols]
            w_ref, m_ref, v_ref = ins[3 * j:3 * j + 3]
            outs[4 * j][...] = g
            outs[4 * j + 1][...], outs[4 * j + 2][...], outs[4 * j + 3][...] = _adamw_update(
                w_ref[...], g, m_ref[...], v_ref[...])
        outs[-2][...] = total[LOSS_ROW:LOSS_ROW + 1, :LANES]
        outs[-1][...] = total[SMALL_ROWS:, :]

    shapes = [jax.ShapeDtypeStruct((rows, cols), F32) for _, _, rows, cols in SMALL_LAYOUT for _ in range(4)]
    shapes += [jax.ShapeDtypeStruct((1, LANES), F32), jax.ShapeDtypeStruct((PACK_ROWS - SMALL_ROWS, D_MODEL), F32)]
    operands = [a for item in SMALL_LAYOUT for a in state[item[0]]]
    res = pl.pallas_call(body, name=name, out_shape=shapes, compiler_params=_params())(all_packs, *operands)
    per_name = {item[0]: tuple(res[4 * j:4 * j + 4]) for j, item in enumerate(SMALL_LAYOUT)}
    return per_name, res[-2][0, 0], res[-1]


def kernel(x, p, norm_g, w_in, attn_out_g, dw_w, dw_b, conv_ln_g, conv_ln_b, w_pw, conv_out_g, w_out, ple_norm_g, w_ple_gate, w_ple, final_g, loss_target, m_norm_g, m_w_in, m_attn_out_g, m_dw_w, m_dw_b, m_conv_ln_g, m_conv_ln_b, m_w_pw, m_conv_out_g, m_w_out, m_ple_norm_g, m_w_ple_gate, m_w_ple, m_final_g, v_norm_g, v_w_in, v_attn_out_g, v_dw_w, v_dw_b, v_conv_ln_g, v_conv_ln_b, v_w_pw, v_conv_out_g, v_w_out, v_ple_norm_g, v_w_ple_gate, v_w_ple, v_final_g):
    depth = w_in.shape[0]
    T = x.shape[1]
    given = dict(
        norm_g=norm_g, ple_norm_g=ple_norm_g, final_g=final_g, dw_b=dw_b, conv_ln_g=conv_ln_g, conv_ln_b=conv_ln_b,
        conv_out_g=conv_out_g, attn_out_g=attn_out_g,
        m_norm_g=m_norm_g, m_ple_norm_g=m_ple_norm_g, m_final_g=m_final_g, m_dw_b=m_dw_b, m_conv_ln_g=m_conv_ln_g,
        m_conv_ln_b=m_conv_ln_b, m_conv_out_g=m_conv_out_g, m_attn_out_g=m_attn_out_g,
        v_norm_g=v_norm_g, v_ple_norm_g=v_ple_norm_g, v_final_g=v_final_g, v_dw_b=v_dw_b, v_conv_ln_g=v_conv_ln_g,
        v_conv_ln_b=v_conv_ln_b, v_conv_out_g=v_conv_out_g, v_attn_out_g=v_attn_out_g)
    my_idx = 4 * lax.axis_index("x") + 2 * lax.axis_index("y") + lax.axis_index("c")

    ids = jnp.arange(BLK)
    tri = (ids[:, None] >= ids[None, :]).astype(BF16)
    tri_t = (ids[:, None] <= ids[None, :]).astype(BF16)
    hid = jnp.arange(ATTN_DIM) // HEAD_DIM
    head_mean = ((hid[:, None] == hid[None, :]).astype(F32) / HEAD_DIM).astype(BF16)

    w_names = ("w_in_t", "w_pw", "w_out", "w_gate", "w_ple")
    w_axes = dict(zip(w_names, (0, 0, 0, 0, 1)))
    shards = [dict(zip(w_names, (w_in[l].T.astype(BF16), w_pw[l].astype(BF16), w_out[l].astype(BF16),
                                 w_ple_gate[l].astype(BF16), w_ple[l].astype(BF16)))) for l in range(depth)]
    first = _all_gather([shards[0]["w_in_t"]] + [dw_w[l].T for l in range(depth)], [0] * (1 + depth),
                        "gather_weights_0")
    layers = []
    for l in range(depth):
        layers.append(dict(
            dw_w=first[1 + l].T,
            g_norm=norm_g[l][None], g_attn=jnp.tile(attn_out_g[l], N_HEADS)[None], dw_b=dw_b[l][None],
            ln_g=conv_ln_g[l][None], ln_b=conv_ln_b[l][None], g_conv=conv_out_g[l][None],
            g_ple=ple_norm_g[l][None], p=p[l, 0]))
    layers[0]["w_in_t"] = first[0]

    def rest_of(l, names):
        return [_Ride.gather2(shards[l][n], w_axes[n]) for n in names]

    h = x[0]
    saved = []
    for l, w in enumerate(layers):
        early, late = (w_names[3:], w_names[1:3]) if l == 0 else ((), ())
        (qs, k, v, ug, hn), landed = _prenorm_inproj(h, w["g_norm"], w["w_in_t"], f"inproj_{l}",
                                                     _Ride(rest_of(l, early)))
        w.update(zip(early, landed))
        ahead = [_Ride.gather2(shards[l + 1]["w_in_t"], 0, None, 0, W_IN_ROWS_ON_ATTN)] if l + 1 < depth else []
        own = w_names[1:] if l > 0 else ()
        (o, cs), landed = _attn_fwd(qs, k, v, tri, f"attn_fwd_{l}", _Ride(ahead + rest_of(l, own)))
        w_in_next = landed[:len(ahead)]
        w.update(zip(own, landed[len(ahead):]))
        (conv, c2), landed = _conv_fwd(ug, w["dw_w"], w["dw_b"], w["ln_g"], w["ln_b"], f"conv_fwd_{l}",
                                       _Ride(rest_of(l, late)))
        w.update(zip(late, landed))
        tail = [_Ride.gather2(shards[l + 1]["w_in_t"], 0, a, W_IN_ROWS_ON_ATTN,
                              shards[l + 1]["w_in_t"].shape[0] - W_IN_ROWS_ON_ATTN) for a in w_in_next]
        (h2, h1, ycat, hn2, gate, e, c3), landed = _mix_out_ple(
            o, ug, c2, h, w["p"], head_mean, w["g_attn"], w["g_conv"], w["g_ple"],
            w["w_pw"], w["w_out"], w["w_gate"], w["w_ple"], f"mix_{l}", _Ride(tail))
        if landed:
            layers[l + 1]["w_in_t"] = landed[0]
        saved.append(dict(h=h, qs=qs, k=k, v=v, ug=ug, hn=hn, o=o, cs=cs, conv=conv, c2=c2, h1=h1,
                          ycat=ycat, hn2=hn2, gate=gate, e=e, c3=c3))
        h = h2
    dh, g_final, loss_part = _final_loss(h, loss_target[0], final_g[None], "final_loss")

    small = {}
    dww_parts = [None] * depth
    slots = [dict() for _ in range(depth)]
    g_w_in = None
    for l in reversed(range(depth)):
        w, s = layers[l], saved[l]
        above = [None] if g_w_in is not None else []

        def part(i, above=above, g=g_w_in):
            return [_Ride.scatter(g, 0, above[0], *W_IN_GRAD_PARTS[i])] if above else []

        def scattered(grads, names):
            return [_Ride.scatter(grads[n], w_axes[n]) for n in names]

        (dh1, dh1b, dzg, de, dycat, g_ple_sum), landed = _ple_out_bwd(
            dh, s["gate"], s["e"], s["h1"], w["g_ple"], w["w_gate"], w["w_out"], f"ple_bwd_{l}", _Ride(part(0)))
        above[:1] = landed
        (do, dga, dgc, dc3, dconv, sums), landed = _branch_bwd(
            dycat, s["o"], s["ug"], s["c3"], s["conv"], head_mean, w["g_attn"], w["g_conv"],
            w["ln_g"], w["ln_b"], w["w_pw"], f"branch_bwd_{l}", _Ride(part(1)))
        above[:1] = landed
        grads = dict(
            w_pw=_weight_grad([s["c2"]], dc3, f"grad_w_pw_{l}")[0],
            w_out=_weight_grad([s["ycat"]], dh1b, f"grad_w_out_{l}")[0],
            w_gate=_weight_grad([s["hn2"]], dzg, f"grad_w_gate_{l}")[0],
            w_ple=_weight_grad([w["p"].astype(BF16)], de, f"grad_w_ple_{l}", tk=PLE_DIM)[0])
        (dcv, dcg, dww), landed = _conv_bwd(dconv, s["ug"], w["dw_w"], f"conv_bwd_{l}", _Ride(part(2)))
        above[:1] = landed
        (dq, dk, dv), landed = _attn_bwd(s["qs"], s["k"], s["v"], do, s["cs"], tri, tri_t, f"attn_bwd_{l}",
                                         _Ride(scattered(grads, w_names[1:])))
        slots[l].update(zip(w_names[1:], landed))
        du = [dq, dk, dv, dga, dcv, dcg, dgc]
        g_w_in_here, landed = _weight_grad(du, s["hn"], f"grad_w_in_{l}", ride=_Ride(part(3)))
        if above:
            slots[l + 1]["w_in_t"] = landed[0]
        tail = [_Ride.scatter_chips(_pair_reduce(g_w_in_here, f"pair_reduce_w_in_{l}"))] if l == 0 else []
        (dh, g_norm_sum), landed = _inproj_bwd(du, w["w_in_t"], s["h"], dh1, w["g_norm"], f"inproj_bwd_{l}",
                                               _Ride(tail))
        slots[l].update(zip(("w_in_t",), landed))
        g_w_in = g_w_in_here
        small[l] = dict(norm_g=g_norm_sum, ple_norm_g=g_ple_sum, attn_out_g=sums[0].reshape(N_HEADS, HEAD_DIM).sum(0),
                        conv_out_g=sums[1], conv_ln_g=sums[2], conv_ln_b=sums[3], dw_b=sums[4])
        dww_parts[l] = dww[:CONV_WIDTH]
    slots = [[sl[n] for n in w_names] for sl in slots]
    grad_x = dh[None]

    sums_of = {name: jnp.stack([small[l][name].reshape(-1) for l in range(depth)]) for name in small[0]}
    sums_of["final_g"] = g_final
    pack = jnp.concatenate([_pack_small(sums_of, scalar=loss_part[0, 0]), jnp.concatenate(dww_parts, axis=1),
                            jnp.zeros((PACK_ROWS - SMALL_ROWS - CONV_WIDTH, D_MODEL), F32)], axis=0)
    (all_packs,) = _all_gather([pack], [0], "gather_small_grads")
    state = {name: [given[pre + name].reshape(rows, cols) for pre in ("", "m_", "v_")]
             for name, _, rows, cols in SMALL_LAYOUT}
    updated, loss, dww_sum = _small_update(all_packs.reshape(N_DEV, PACK_ROWS, D_MODEL), state, "update_small")
    res = {kind: {name: val[k].reshape(given[name].shape) for name, val in updated.items()}
           for k, kind in enumerate("gdmv")}
    dww_full = dww_sum[:CONV_WIDTH].reshape(CONV_WIDTH, depth, CONV_DIM).transpose(1, 0, 2)
    g_dw_w = lax.dynamic_slice_in_dim(dww_full, my_idx * (CONV_DIM // N_DEV), CONV_DIM // N_DEV, axis=2)

    swap = lambda a: a.transpose(0, 2, 1)
    state = {"w_in": (w_in, m_w_in, v_w_in), "w_pw": (w_pw, m_w_pw, v_w_pw), "w_out": (w_out, m_w_out, v_w_out),
             "w_ple_gate": (w_ple_gate, m_w_ple_gate, v_w_ple_gate), "w_ple": (w_ple, m_w_ple, v_w_ple)}
    for at, name in enumerate(state):
        wv, mv, vv = [swap(a) for a in state[name]] if name == "w_in" else state[name]
        out = _sum_adamw([slots[l][at] for l in range(depth)], wv, mv, vv, f"adamw_{name}")
        out = [swap(a) for a in out] if name == "w_in" else out
        res["g"][name], res["d"][name], res["m"][name], res["v"][name] = out
    flat = lambda a: a.reshape(-1, a.shape[-1])
    res["g"]["dw_w"] = g_dw_w
    res["d"]["dw_w"], res["m"]["dw_w"], res["v"]["dw_w"] = [
        a.reshape(dw_w.shape) for a in _adamw(flat(dw_w), flat(g_dw_w), flat(m_dw_w), flat(v_dw_w), "adamw_dw_w")]

    order = ["norm_g", "w_in", "attn_out_g", "dw_w", "dw_b", "conv_ln_g", "conv_ln_b", "w_pw", "conv_out_g",
             "w_out", "ple_norm_g", "w_ple_gate", "w_ple", "final_g"]
    return (loss, grad_x, *[res["g"][n] for n in order], *[res["d"][n] for n in order],
            *[res["m"][n] for n in order], *[res["v"][n] for n in order])
```

```python
import functools

import jax
import jax.numpy as jnp
from jax import lax
from jax.experimental import pallas as pl
from jax.experimental.pallas import tpu as pltpu

F32 = jnp.float32
BF16 = jnp.bfloat16
MESH = pl.DeviceIdType.MESH

N_DEV = 8
D_MODEL = 1024
ATTN_DIM = 512
CONV_DIM = 512
HEAD_DIM = 64
N_HEADS = 8
CONV_WIDTH = 31
PLE_DIM = 256
CHUNK = 512
N_CHUNK = 7
EPS = 1e-6
ADAM_LR = 0.001
ADAM_B1 = 0.9
ADAM_B2 = 0.999
ADAM_EPS = 1e-08
ADAM_WD = 0.01
ADAM_STEP = 10

LANES = 128
BLK = 256
ATT_COLS = 4
CHAIN_GROUP = 4
SOFTPLUS_LINEAR_AT = 20.0
DEAD_AT = 110.0
FIRST_BLOCK_LANE = HEAD_DIM - 1
TM = 512
HALO = 32
SUBLANES = 8
CONV_ROWS = 32
ADAMW_ROWS = (224, 128, 64)
VMEM_LIMIT = 56 * 1024 * 1024
SMALL_ROWS = 16
SMALL_LAYOUT = (("norm_g", 0, 2, D_MODEL), ("ple_norm_g", 2, 2, D_MODEL), ("final_g", 4, 1, D_MODEL),
                ("dw_b", 5, 2, CONV_DIM), ("conv_ln_g", 7, 2, CONV_DIM), ("conv_ln_b", 9, 2, CONV_DIM),
                ("conv_out_g", 11, 2, CONV_DIM), ("attn_out_g", 13, 2, HEAD_DIM))
LOSS_ROW = 15
W_IN_ROWS_ON_ATTN = 288
W_IN_GRAD_PARTS = ((0, 112), (112, 96), (208, 64), (272, 176))
PACK_ROWS = 48


def _nn(a, b):
    return lax.dot_general(a, b, (((1,), (0,)), ((), ())), preferred_element_type=F32)


def _nt(a, b):
    return lax.dot_general(a, b, (((1,), (1,)), ((), ())), preferred_element_type=F32)


def _tn(a, b):
    return lax.dot_general(a, b, (((0,), (0,)), ((), ())), preferred_element_type=F32)


def _split(x):
    hi = x.astype(BF16)
    lo = (x - hi.astype(F32)).astype(BF16)
    return hi, lo


def _dot_hilo(x, m):
    hi, lo = _split(x)
    return _nn(hi, m) + _nn(lo, m)


def _sigmoid(x):
    return jax.nn.sigmoid(x)


def _dsilu(x, s):
    return s * (1.0 + x * (1.0 - s))


def _params(n_grid=0, vmem=VMEM_LIMIT):
    sem = ("arbitrary",) * n_grid if n_grid else None
    return pltpu.CompilerParams(dimension_semantics=sem, vmem_limit_bytes=vmem)


def _rows(tm, cols, col=0):
    return pl.BlockSpec((tm, cols), lambda i: (i, col))


def _whole(shape):
    zeros = (0,) * len(shape)
    return pl.BlockSpec(shape, lambda *_: zeros)


def _my_position():
    return lax.axis_index("x"), lax.axis_index("y"), lax.axis_index("c")


def _block(ref, axis, idx, size):
    start = pl.multiple_of(idx * size, size)
    if axis == 0:
        return ref.at[pl.ds(start, size), :]
    return ref.at[:, pl.ds(start, size)]


def _all_gather(shards, axes, name):
    n = len(shards)
    sizes = [s.shape[a] for s, a in zip(shards, axes)]

    def full_shape(s, a):
        shape = list(s.shape)
        shape[a] *= N_DEV
        return jax.ShapeDtypeStruct(tuple(shape), s.dtype)

    def body(*refs):
        ins, outs = refs[:n], refs[n:2 * n]
        send_sems, recv_sems, local_sems = refs[2 * n:]
        x, y, c = _my_position()
        me, sibling = (x, y, c), (x, y, 1 - c)
        chips = [(1 - x, y), (x, 1 - y), (1 - x, 1 - y)]

        def place(i, dev):
            return _block(outs[i], axes[i], 4 * dev[0] + 2 * dev[1] + dev[2], sizes[i])

        def copy(k, i, dev, to, src=None):
            return pltpu.make_async_remote_copy(
                src_ref=place(i, dev) if src is None else src, dst_ref=place(i, dev),
                send_sem=send_sems.at[k, i], recv_sem=recv_sems.at[k, i],
                device_id=to, device_id_type=MESH)

        mine = [pltpu.make_async_copy(ins[i], place(i, me), local_sems.at[i]) for i in range(n)]
        for cp in mine:
            cp.start()
        first = [copy(0, i, me, sibling, src=ins[i]) for i in range(n)]
        for j, chip in enumerate(chips):
            first += [copy(1 + j, i, me, (*chip, c), src=ins[i]) for i in range(n)]
        for cp in first:
            cp.start()
        passed = []
        for j, chip in enumerate(chips):
            for i in range(n):
                copy(1 + j, i, (*chip, c), me).wait_recv()
            hop = [copy(4 + j, i, (*chip, c), sibling) for i in range(n)]
            for cp in hop:
                cp.start()
            passed += hop
        for i in range(n):
            copy(0, i, sibling, me).wait_recv()
        for j, chip in enumerate(chips):
            for i in range(n):
                copy(4 + j, i, (*chip, 1 - c), me).wait_recv()
        for cp in first + passed:
            cp.wait_send()
        for cp in mine:
            cp.wait()

    any_spec = pl.BlockSpec(memory_space=pl.ANY)
    return pl.pallas_call(
        body, name=name,
        out_shape=[full_shape(s, a) for s, a in zip(shards, axes)],
        in_specs=[any_spec] * n, out_specs=[any_spec] * n,
        scratch_shapes=[pltpu.SemaphoreType.DMA((7, n)), pltpu.SemaphoreType.DMA((7, n)),
                        pltpu.SemaphoreType.DMA((n,))],
    )(*shards)


def _pair_reduce(g, name):
    n_chips = N_DEV // 2
    R, C = g.shape[0] // N_DEV, g.shape[1]

    def body(g_ref, out_ref, mine_ref, theirs_ref, send_sems, recv_sems, local_sems):
        x, y, c = _my_position()
        block = lambda d: g_ref.at[pl.ds(pl.multiple_of(d * R, 16), R), :]
        sends = [pltpu.make_async_remote_copy(
            src_ref=block(2 * j + 1 - c), dst_ref=theirs_ref.at[j], send_sem=send_sems.at[j],
            recv_sem=recv_sems.at[j], device_id=(x, y, 1 - c), device_id_type=MESH) for j in range(n_chips)]
        own = [pltpu.make_async_copy(block(2 * j + c), mine_ref.at[j], local_sems.at[j]) for j in range(n_chips)]
        for cp in sends + own:
            cp.start()
        for j in range(n_chips):
            own[j].wait()
            sends[j].wait_recv()
            out_ref[j] = (mine_ref[j].astype(F32) + theirs_ref[j].astype(F32)).astype(g.dtype)
        for cp in sends:
            cp.wait_send()

    half = pltpu.VMEM((n_chips, R, C), g.dtype)
    sems = pltpu.SemaphoreType.DMA((n_chips,))
    return pl.pallas_call(
        body, name=name, out_shape=jax.ShapeDtypeStruct((n_chips, R, C), g.dtype),
        in_specs=[pl.BlockSpec(memory_space=pl.ANY)], out_specs=pl.BlockSpec(memory_space=pltpu.VMEM),
        scratch_shapes=[half, half, sems, sems, sems], compiler_params=_params(),
    )(g)


class _Ride:
    def __init__(self, parts):
        self.parts = [p for p in parts if p is not None]

    @staticmethod
    def gather(src, axis, land=None, lo=0, n=None):
        return ("gather", src, land, axis, lo, src.shape[axis] if n is None else n)

    @staticmethod
    def gather2(src, axis, land=None, lo=0, n=None):
        return ("gather2", src, land, axis, lo, src.shape[axis] if n is None else n)

    @staticmethod
    def scatter(src, axis, land=None, lo=0, n=None):
        return ("scatter", src, land, axis, lo, src.shape[axis] // N_DEV if n is None else n)

    @staticmethod
    def scatter_chips(chip_sums):
        return ("scatter_chips", chip_sums, None, 0, 0, chip_sums.shape[1])

    def arrays(self):
        return [p[1] for p in self.parts] + [p[2] for p in self.parts if p[2] is not None]

    def out_shapes(self):
        out = []
        for kind, src, _, axis, _, _ in self.parts:
            shape = list(src.shape)
            if kind in ("gather", "gather2"):
                shape[axis] *= N_DEV
            elif kind == "scatter_chips":
                pass
            else:
                shape[axis] //= N_DEV
                shape = [N_DEV] + shape
            out.append(jax.ShapeDtypeStruct(tuple(shape), src.dtype))
        return out

    def aliases(self, n_in, n_out):
        m, out = len(self.parts), {}
        for j, p in enumerate(self.parts):
            if p[2] is not None:
                out[n_in + m + len(out)] = n_out + j
        return out

    def scratch(self):
        m = len(self.parts)
        return [pltpu.SemaphoreType.DMA((N_DEV - 1, m)), pltpu.SemaphoreType.DMA((N_DEV - 1, m)),
                pltpu.SemaphoreType.DMA((m,))]

    def _copies(self, src_refs, land_refs, sems):
        send_sems, recv_sems, local_sems = sems
        x, y, c = _my_position()
        my_idx = 4 * x + 2 * y + c
        own, sends, relays, lands = [], [], [], []
        for j, (kind, src, _, axis, lo, n) in enumerate(self.parts):
            if kind == "scatter_chips":
                for k in (0, 2, 4, 6):
                    px, py = (1 - x if k & 4 else x), (1 - y if k & 2 else y)
                    a, b = src_refs[j].at[2 * px + py], land_refs[j].at[2 * x + y]
                    if k == 0:
                        own.append(pltpu.make_async_copy(a, b, local_sems.at[j]))
                        continue
                    mk = lambda dst, a=a, k=k, j=j, to=(px, py, c): pltpu.make_async_remote_copy(
                        src_ref=a, dst_ref=dst, send_sem=send_sems.at[k - 1, j], recv_sem=recv_sems.at[k - 1, j],
                        device_id=to, device_id_type=MESH)
                    sends.append(mk(b))
                    lands.append(mk(land_refs[j].at[2 * px + py]))
                continue
            size = src.shape[axis] if kind in ("gather", "gather2") else src.shape[axis] // N_DEV
            align = 16 if axis == 0 else LANES

            def rows(ref, idx, lead=None, axis=axis, lo=lo, n=n, size=size, align=align):
                at = pl.ds(pl.multiple_of(idx * size + lo, align), n)
                where = (at, slice(None)) if axis == 0 else (slice(None), at)
                return ref.at[where] if lead is None else ref.at[(lead, *where)]

            def in_shard(ref):
                return rows(ref, 0)

            def in_slot(ref, s):
                return rows(ref, 0, lead=s)

            if kind == "gather2":
                chips = [(1 - x, y), (x, 1 - y), (1 - x, 1 - y)]
                place = lambda px, py, pc: rows(land_refs[j], 4 * px + 2 * py + pc)

                def copy(i, a, dst, to, j=j):
                    return pltpu.make_async_remote_copy(
                        src_ref=a, dst_ref=dst, send_sem=send_sems.at[i, j], recv_sem=recv_sems.at[i, j],
                        device_id=to, device_id_type=MESH)

                mine = in_shard(src_refs[j])
                own.append(pltpu.make_async_copy(mine, place(x, y, c), local_sems.at[j]))
                sends.append(copy(0, mine, place(x, y, c), (x, y, 1 - c)))
                lands.append(copy(0, mine, place(x, y, 1 - c), (x, y, 1 - c)))
                for i, (px, py) in enumerate(chips):
                    sends.append(copy(1 + i, mine, place(x, y, c), (px, py, c)))
                    relays.append((copy(1 + i, mine, place(px, py, c), (px, py, c)),
                                   copy(4 + i, place(px, py, c), place(px, py, c), (x, y, 1 - c))))
                    lands.append(copy(4 + i, mine, place(px, py, 1 - c), (x, y, 1 - c)))
                continue
            for k in range(N_DEV):
                px = 1 - x if k & 4 else x
                py = 1 - y if k & 2 else y
                pc = 1 - c if k & 1 else c
                peer_idx = 4 * px + 2 * py + pc
                if kind == "gather":
                    a, b, landed = in_shard(src_refs[j]), rows(land_refs[j], my_idx), rows(land_refs[j], peer_idx)
                else:
                    a, b, landed = rows(src_refs[j], peer_idx), in_slot(land_refs[j], my_idx), in_slot(land_refs[j], peer_idx)
                if k == 0:
                    own.append(pltpu.make_async_copy(a, b, local_sems.at[j]))
                    continue
                mk = lambda dst, a=a, k=k, j=j, to=(px, py, pc): pltpu.make_async_remote_copy(
                    src_ref=a, dst_ref=dst, send_sem=send_sems.at[k - 1, j], recv_sem=recv_sems.at[k - 1, j],
                    device_id=to, device_id_type=MESH)
                sends.append(mk(b))
                lands.append(mk(landed))
        return own, sends, relays, lands

    @property
    def relayed(self):
        return any(p[0] == "gather2" for p in self.parts)

    def start(self, src_refs, land_refs, sems):
        own, sends, _, _ = self._copies(src_refs, land_refs, sems)
        for cp in own + sends:
            cp.start()

    def relay(self, src_refs, land_refs, sems):
        for arrival, onward in self._copies(src_refs, land_refs, sems)[2]:
            arrival.wait_recv()
            onward.start()

    def wait(self, src_refs, land_refs, sems):
        own, sends, relays, lands = self._copies(src_refs, land_refs, sems)
        for cp in lands:
            cp.wait_recv()
        for cp in sends + [onward for _, onward in relays]:
            cp.wait_send()
        for cp in own:
            cp.wait()


def _call(body, *, name, grid, in_specs, out_specs, out_shape, args, scratch_shapes=(), ride=None):
    in_specs, out_specs, out_shape = list(in_specs), list(out_specs), list(out_shape)
    n_in, n_out, n_sc = len(in_specs), len(out_specs), len(scratch_shapes)
    if ride is None or not ride.parts:
        res = pl.pallas_call(body, name=name, grid=grid, in_specs=in_specs, out_specs=out_specs,
                             out_shape=out_shape, scratch_shapes=list(scratch_shapes),
                             compiler_params=_params(len(grid)))(*args)
        return list(res), []
    extra, m = ride.arrays(), len(ride.parts)

    def riding(*refs):
        a = n_in + len(extra)
        b = a + n_out
        srcs, lands, sems = refs[n_in:n_in + m], refs[b:b + m], refs[b + m + n_sc:]
        at = [pl.program_id(d) for d in range(len(grid))]

        @pl.when(functools.reduce(jnp.logical_and, [i == 0 for i in at]))
        def _():
            ride.start(srcs, lands, sems)

        if ride.relayed:
            step, n_steps = at[0], 1
            for i, g in zip(at[1:], grid[1:]):
                step = step * g + i
            for g in grid:
                n_steps *= g
            assert n_steps >= 2, "a two-level ride needs a grid step after the first"

            @pl.when(step == n_steps - 1)
            def _():
                ride.relay(srcs, lands, sems)

        body(*refs[:n_in], *refs[a:b], *refs[b + m:b + m + n_sc])

        @pl.when(functools.reduce(jnp.logical_and, [i == g - 1 for i, g in zip(at, grid)]))
        def _():
            ride.wait(srcs, lands, sems)

    hbm = pl.BlockSpec(memory_space=pl.ANY)
    res = pl.pallas_call(
        riding, name=name, grid=grid, in_specs=in_specs + [hbm] * len(extra), out_specs=out_specs + [hbm] * m,
        out_shape=out_shape + ride.out_shapes(), scratch_shapes=list(scratch_shapes) + ride.scratch(),
        input_output_aliases=ride.aliases(n_in, n_out), compiler_params=_params(len(grid)),
    )(*args, *extra)
    return list(res[:n_out]), list(res[n_out:])


def _prenorm_inproj(h, gain, w_in_t, name, ride=None):
    T = h.shape[0]

    def body(h_ref, g_ref, w_ref, q_ref, k_ref, v_ref, ug_ref, hn_ref):
        hv = h_ref[...]
        r = lax.rsqrt(jnp.mean(hv * hv, axis=-1, keepdims=True) + EPS)
        hn = (hv * r * g_ref[...]).astype(BF16)
        hn_ref[...] = hn
        for j in range(N_CHUNK):
            u = _nt(hn, w_ref[j * CHUNK:(j + 1) * CHUNK, :])
            if j == 0:
                q_ref[...] = (u * (HEAD_DIM ** -0.5)).astype(BF16)
            elif j == 1:
                k_ref[...] = u.astype(BF16)
            elif j == 2:
                v_ref[...] = u.astype(BF16)
            else:
                ug_ref[:, (j - 3) * CHUNK:(j - 2) * CHUNK] = u.astype(BF16)

    act = jax.ShapeDtypeStruct((T, CHUNK), BF16)
    return _call(
        body, name=name, grid=(T // TM,),
        in_specs=[_rows(TM, D_MODEL), _whole((1, D_MODEL)), _whole((N_CHUNK * CHUNK, D_MODEL))],
        out_specs=[_rows(TM, CHUNK)] * 3 + [_rows(TM, 4 * CHUNK), _rows(TM, D_MODEL)],
        out_shape=[act, act, act, jax.ShapeDtypeStruct((T, 4 * CHUNK), BF16),
                   jax.ShapeDtypeStruct((T, D_MODEL), BF16)],
        args=(h, gain, w_in_t,), ride=ride)


def _softplus_parts(z):
    ez = jnp.exp(jnp.minimum(z, SOFTPLUS_LINEAR_AT))
    t = 1.0 + ez
    return ez * pl.reciprocal(t, approx=True), jnp.where(z > SOFTPLUS_LINEAR_AT, z, jnp.log(t))


def _attn_fwd(qs, k, v, tri, name, ride=None):
    T = qs.shape[0]
    assert T // BLK <= FIRST_BLOCK_LANE, "one lane per key block below the lane of the first block"
    width = LANES * ATT_COLS
    chains = [(c, half) for c in range(ATT_COLS) for half in range(2)]

    def body(q_ref, k_ref, v_ref, m_ref, o_ref, cs_ref):
        qi = pl.program_id(1)
        lane = lax.broadcasted_iota(jnp.int32, (BLK, LANES), 1)
        first = lane < HEAD_DIM
        causal = (lax.broadcasted_iota(jnp.int32, (BLK, BLK), 1)
                  < lax.broadcasted_iota(jnp.int32, (BLK, BLK), 0))
        tri_m = m_ref[...]
        qh = {}
        for c in range(ATT_COLS):
            q = q_ref[:, c * LANES:(c + 1) * LANES]
            zero = jnp.zeros_like(q)
            qh[c, 0], qh[c, 1] = jnp.where(first, q, zero), jnp.where(first, zero, q)

        def step(kb, state, masked):
            carries, accs, cvals = state
            start = pl.multiple_of(kb * BLK, BLK)
            kblk = [k_ref[pl.ds(start, BLK), c * LANES:(c + 1) * LANES] for c in range(ATT_COLS)]
            vblk = [v_ref[pl.ds(start, BLK), c * LANES:(c + 1) * LANES] for c in range(ATT_COLS)]
            carries, accs, cvals = list(carries), list(accs), list(cvals)
            for g0 in range(0, len(chains), CHAIN_GROUP):
                ids = range(g0, g0 + CHAIN_GROUP)
                z = [_nt(qh[chains[n]], kblk[chains[n][0]]) for n in ids]
                sp = [_softplus_parts(zi)[1] for zi in z]
                if masked:
                    sp = [jnp.where(causal, s, 0.0) for s in sp]
                incl = [_dot_hilo(s, tri_m) for s in sp]
                a = [jnp.exp(zi - ii - carries[n]) for n, zi, ii in zip(ids, z, incl)]
                if masked:
                    a = [jnp.where(causal, ai, 0.0) for ai in a]
                for n, ai, ii in zip(ids, a, incl):
                    c, half = chains[n]
                    zero = jnp.zeros_like(vblk[c])
                    vh = jnp.where(first, vblk[c], zero) if half == 0 else jnp.where(first, zero, vblk[c])
                    accs[c] = accs[c] + _nn(ai.astype(BF16), vh)
                    cvals[c] = jnp.where(lane == kb + HEAD_DIM * half, carries[n], cvals[c])
                    carries[n] = carries[n] + ii[:, 0:1]
            return tuple(carries), tuple(accs), tuple(cvals)

        zeros = tuple(jnp.zeros((BLK, LANES), F32) for _ in range(ATT_COLS))
        state = (tuple(jnp.zeros((BLK, 1), F32) for _ in chains), zeros, zeros)
        state = step(qi, state, True)

        def reaches_further(st):
            it, (carries, _, _) = st
            least = functools.reduce(jnp.minimum, carries)
            return jnp.logical_and(it < qi, jnp.min(least) < DEAD_AT)

        done, state = lax.while_loop(reaches_further, lambda st: (st[0] + 1, step(qi - 1 - st[0], st[1], False)),
                                     (jnp.int32(0), state))
        first_block = (qi - done).astype(F32)
        for c in range(ATT_COLS):
            o_ref[:, c * LANES:(c + 1) * LANES] = state[1][c]
            cs_ref[:, c * LANES:(c + 1) * LANES] = jnp.where(lane == FIRST_BLOCK_LANE, first_block, state[2][c])

    blk = pl.BlockSpec((BLK, width), lambda j, i: (i, j))
    col = pl.BlockSpec((T, width), lambda j, i: (0, j))
    out = jax.ShapeDtypeStruct((T, ATTN_DIM), F32)
    return _call(
        body, name=name, grid=(ATTN_DIM // width, T // BLK),
        in_specs=[blk, col, col, _whole((BLK, BLK))],
        out_specs=[blk, blk], out_shape=[out, out],
        args=(qs, k, v, tri,), ride=ride)


def _shifted_copies(pad_ref, sh_ref):
    rows = sh_ref.shape[1]
    for b in range(SUBLANES):
        sh_ref[b] = pad_ref[b:b + rows, :]


def _shift_of(offset):
    return offset % SUBLANES, offset - offset % SUBLANES


def _conv_fwd(ug, dw_w, dw_b, ln_g, ln_b, name, ride=None):
    T = ug.shape[0]
    per = TM // HALO

    def body(cv_ref, cg_ref, cvh_ref, cgh_ref, w_ref, b_ref, g_ref, beta_ref, conv_ref, c2_ref, pad_ref, sh_ref):
        i = pl.program_id(0)
        halo = cvh_ref[...].astype(F32) * _sigmoid(cgh_ref[...].astype(F32))
        pad_ref[0:HALO, :] = jnp.where(i == 0, 0.0, halo)
        pad_ref[HALO:HALO + TM, :] = cv_ref[...].astype(F32) * _sigmoid(cg_ref[...].astype(F32))
        pad_ref[HALO + TM:, :] = jnp.zeros((SUBLANES, CONV_DIM), F32)
        _shifted_copies(pad_ref, sh_ref)
        taps = [w_ref[t:t + 1, :] for t in range(CONV_WIDTH)]

        def rows(j, _):
            r = pl.multiple_of(j * CONV_ROWS, CONV_ROWS)
            acc = jnp.zeros((CONV_ROWS, CONV_DIM), F32) + b_ref[...]
            for t in range(CONV_WIDTH):
                b, a = _shift_of(HALO - (CONV_WIDTH - 1) + t)
                acc = acc + taps[t] * sh_ref[b, pl.ds(r + a, CONV_ROWS), :]
            conv_ref[pl.ds(r, CONV_ROWS), :] = acc
            return 0

        lax.fori_loop(0, TM // CONV_ROWS, rows, 0)
        acc = conv_ref[...]
        mu = jnp.mean(acc, axis=-1, keepdims=True)
        xc = acc - mu
        rs = lax.rsqrt(jnp.mean(xc * xc, axis=-1, keepdims=True) + EPS)
        ln = xc * rs * g_ref[...] + beta_ref[...]
        c2_ref[...] = (ln * _sigmoid(ln)).astype(BF16)

    prev = lambda col: pl.BlockSpec((HALO, CHUNK), lambda i: (jnp.maximum(i * per - 1, 0), col))
    vec = _whole((1, CONV_DIM))
    return _call(
        body, name=name, grid=(T // TM,),
        in_specs=[_rows(TM, CHUNK, 1), _rows(TM, CHUNK, 2), prev(1), prev(2),
                  _whole((CONV_WIDTH, CONV_DIM)), vec, vec, vec],
        out_specs=[_rows(TM, CONV_DIM), _rows(TM, CONV_DIM)],
        out_shape=[jax.ShapeDtypeStruct((T, CONV_DIM), F32), jax.ShapeDtypeStruct((T, CONV_DIM), BF16)],
        scratch_shapes=[pltpu.VMEM((TM + HALO + SUBLANES, CONV_DIM), F32),
                        pltpu.VMEM((SUBLANES, TM + HALO, CONV_DIM), F32)],
        args=(ug, ug, ug, ug, dw_w, dw_b, ln_g, ln_b,), ride=ride)


def _mix_out_ple(o, ug, c2, h, p, head_mean, g_attn, g_conv, g_ple, w_pw, w_out, w_gate, w_ple, name, ride=None):
    T = h.shape[0]

    def body(o_ref, ga_ref, gc_ref, c2_ref, h_ref, p_ref, hm_ref, gao_ref, gco_ref, gpn_ref,
             wpw_ref, wout_ref, wg_ref, wple_ref,
             h2_ref, h1_ref, ycat_ref, hn2_ref, gate_ref, e_ref, c3_ref):
        ov = o_ref[...]
        rh = lax.rsqrt(_nn((ov * ov).astype(BF16), hm_ref[...]) + EPS)
        ga = ga_ref[...].astype(F32)
        ya = (ov * rh * gao_ref[...] * (ga * _sigmoid(ga))).astype(BF16)
        c3 = _nn(c2_ref[...], wpw_ref[...])
        c3_ref[...] = c3
        rc = lax.rsqrt(jnp.mean(c3 * c3, axis=-1, keepdims=True) + EPS)
        gc = gc_ref[...].astype(F32)
        yc = (c3 * rc * gco_ref[...] * (gc * _sigmoid(gc))).astype(BF16)
        ycat_ref[:, :ATTN_DIM] = ya
        ycat_ref[:, ATTN_DIM:] = yc
        h1 = h_ref[...] + _nn(ya, wout_ref[:ATTN_DIM, :]) + _nn(yc, wout_ref[ATTN_DIM:, :])
        h1_ref[...] = h1
        r1 = lax.rsqrt(jnp.mean(h1 * h1, axis=-1, keepdims=True) + EPS)
        hn2 = (h1 * r1 * gpn_ref[...]).astype(BF16)
        hn2_ref[...] = hn2
        gate = _sigmoid(_nn(hn2, wg_ref[...]))
        e = _nn(p_ref[...].astype(BF16), wple_ref[...])
        gate_ref[...] = gate
        e_ref[...] = e
        h2_ref[...] = h1 + e * gate

    f32 = lambda cols: jax.ShapeDtypeStruct((T, cols), F32)
    bf = lambda cols: jax.ShapeDtypeStruct((T, cols), BF16)
    return _call(
        body, name=name, grid=(T // TM,),
        in_specs=[_rows(TM, ATTN_DIM), _rows(TM, CHUNK, 0), _rows(TM, CHUNK, 3), _rows(TM, CONV_DIM),
                  _rows(TM, D_MODEL), _rows(TM, PLE_DIM), _whole((ATTN_DIM, ATTN_DIM)),
                  _whole((1, ATTN_DIM)), _whole((1, CONV_DIM)), _whole((1, D_MODEL)),
                  _whole((CONV_DIM, CONV_DIM)), _whole((D_MODEL, D_MODEL)), _whole((D_MODEL, D_MODEL)),
                  _whole((PLE_DIM, D_MODEL))],
        out_specs=[_rows(TM, D_MODEL), _rows(TM, D_MODEL), _rows(TM, D_MODEL), _rows(TM, D_MODEL),
                   _rows(TM, D_MODEL), _rows(TM, D_MODEL), _rows(TM, CONV_DIM)],
        out_shape=[f32(D_MODEL), f32(D_MODEL), bf(D_MODEL), bf(D_MODEL), f32(D_MODEL), f32(D_MODEL),
                   f32(CONV_DIM)],
        args=(o, ug, ug, c2, h, p, head_mean, g_attn, g_conv, g_ple, w_pw, w_out, w_gate, w_ple,), ride=ride)


def _final_loss(h, target, gain, name):
    T = h.shape[0]

    def body(h_ref, t_ref, g_ref, dh_ref, gsum_ref, loss_ref):
        @pl.when(pl.program_id(0) == 0)
        def _():
            gsum_ref[...] = jnp.zeros_like(gsum_ref)
            loss_ref[...] = jnp.zeros_like(loss_ref)

        hv = h_ref[...]
        r = lax.rsqrt(jnp.mean(hv * hv, axis=-1, keepdims=True) + EPS)
        xh = hv * r
        diff = xh * g_ref[...] - t_ref[...]
        loss_ref[...] += 0.5 * jnp.sum(jnp.mean(diff * diff, axis=-1, keepdims=True), axis=0, keepdims=True)
        dy = diff * (1.0 / D_MODEL)
        gsum_ref[...] += jnp.sum(dy * xh, axis=0, keepdims=True)
        dxh = dy * g_ref[...]
        dh_ref[...] = r * (dxh - xh * jnp.mean(dxh * xh, axis=-1, keepdims=True))

    return pl.pallas_call(
        body, name=name, grid=(T // TM,),
        in_specs=[_rows(TM, D_MODEL), _rows(TM, D_MODEL), _whole((1, D_MODEL))],
        out_specs=[_rows(TM, D_MODEL), _whole((1, D_MODEL)), _whole((1, LANES))],
        out_shape=[jax.ShapeDtypeStruct((T, D_MODEL), F32), jax.ShapeDtypeStruct((1, D_MODEL), F32),
                   jax.ShapeDtypeStruct((1, LANES), F32)],
        compiler_params=_params(1),
    )(h, target, gain)


def _ple_out_bwd(dh2, gate, e, h1, g_ple, w_gate, w_out, name, ride=None):
    T = dh2.shape[0]

    def body(dh2_ref, gate_ref, e_ref, h1_ref, gpn_ref, wg_ref, wout_ref,
             dh1_ref, dh1b_ref, dzg_ref, de_ref, dycat_ref, gsum_ref):
        @pl.when(pl.program_id(0) == 0)
        def _():
            gsum_ref[...] = jnp.zeros_like(gsum_ref)

        dh2v = dh2_ref[...]
        gate = gate_ref[...]
        de_ref[...] = (dh2v * gate).astype(BF16)
        dzg = (dh2v * e_ref[...] * gate * (1.0 - gate)).astype(BF16)
        dzg_ref[...] = dzg
        dhn2 = _nt(dzg, wg_ref[...])
        h1 = h1_ref[...]
        r1 = lax.rsqrt(jnp.mean(h1 * h1, axis=-1, keepdims=True) + EPS)
        xh = h1 * r1
        gsum_ref[...] += jnp.sum(dhn2 * xh, axis=0, keepdims=True)
        dxh = dhn2 * gpn_ref[...]
        dh1 = dh2v + r1 * (dxh - xh * jnp.mean(dxh * xh, axis=-1, keepdims=True))
        dh1_ref[...] = dh1
        dh1b = dh1.astype(BF16)
        dh1b_ref[...] = dh1b
        dycat_ref[...] = _nt(dh1b, wout_ref[...])

    f32 = jax.ShapeDtypeStruct((T, D_MODEL), F32)
    bf = jax.ShapeDtypeStruct((T, D_MODEL), BF16)
    full = _rows(TM, D_MODEL)
    return _call(
        body, name=name, grid=(T // TM,),
        in_specs=[full, full, full, full, _whole((1, D_MODEL)), _whole((D_MODEL, D_MODEL)),
                  _whole((D_MODEL, D_MODEL))],
        out_specs=[full, full, full, full, full, _whole((1, D_MODEL))],
        out_shape=[f32, bf, bf, bf, f32, jax.ShapeDtypeStruct((1, D_MODEL), F32)],
        args=(dh2, gate, e, h1, g_ple, w_gate, w_out,), ride=ride)


def _branch_bwd(dycat, o, ug, c3, conv, head_mean, g_attn, g_conv, ln_g, ln_b, w_pw, name, ride=None):
    T = o.shape[0]

    def body(dya_ref, dyc_ref, o_ref, ga_ref, gc_ref, c3_ref, conv_ref, hm_ref, gao_ref, gco_ref,
             lng_ref, lnb_ref, wpw_ref,
             do_ref, dga_ref, dgc_ref, dc3_ref, dconv_ref, sums_ref):
        @pl.when(pl.program_id(0) == 0)
        def _():
            sums_ref[...] = jnp.zeros_like(sums_ref)

        hm = hm_ref[...]
        col = lambda x: jnp.sum(x, axis=0, keepdims=True)
        ov = o_ref[...]
        rh = lax.rsqrt(_nn((ov * ov).astype(BF16), hm) + EPS)
        xh = ov * rh
        ga = ga_ref[...].astype(F32)
        sg = _sigmoid(ga)
        dya = dya_ref[...]
        don = dya * (ga * sg)
        dga_ref[...] = (dya * xh * gao_ref[...] * _dsilu(ga, sg)).astype(BF16)
        sums_ref[0:1, :] += col(don * xh)
        dxh = don * gao_ref[...]
        do_ref[...] = (rh * (dxh - xh * _dot_hilo(dxh * xh, hm))).astype(BF16)
        c3 = c3_ref[...]
        rc = lax.rsqrt(jnp.mean(c3 * c3, axis=-1, keepdims=True) + EPS)
        xh3 = c3 * rc
        gc = gc_ref[...].astype(F32)
        sgc = _sigmoid(gc)
        dyc = dyc_ref[...]
        dn3 = dyc * (gc * sgc)
        dgc_ref[...] = (dyc * xh3 * gco_ref[...] * _dsilu(gc, sgc)).astype(BF16)
        sums_ref[1:2, :] += col(dn3 * xh3)
        dxh3 = dn3 * gco_ref[...]
        dc3 = (rc * (dxh3 - xh3 * jnp.mean(dxh3 * xh3, axis=-1, keepdims=True))).astype(BF16)
        dc3_ref[...] = dc3
        dc2 = _nt(dc3, wpw_ref[...])
        cv = conv_ref[...]
        mu = jnp.mean(cv, axis=-1, keepdims=True)
        xc = cv - mu
        rs = lax.rsqrt(jnp.mean(xc * xc, axis=-1, keepdims=True) + EPS)
        xn = xc * rs
        ln = xn * lng_ref[...] + lnb_ref[...]
        dln = dc2 * _dsilu(ln, _sigmoid(ln))
        sums_ref[2:3, :] += col(dln * xn)
        sums_ref[3:4, :] += col(dln)
        dxn = dln * lng_ref[...]
        dconv = rs * (dxn - jnp.mean(dxn, axis=-1, keepdims=True)
                      - xn * jnp.mean(dxn * xn, axis=-1, keepdims=True))
        dconv_ref[...] = dconv
        sums_ref[4:5, :] += col(dconv)

    half = lambda dt: jax.ShapeDtypeStruct((T, CHUNK), dt)
    tile = _rows(TM, CHUNK)
    vec = _whole((1, CHUNK))
    return _call(
        body, name=name, grid=(T // TM,),
        in_specs=[_rows(TM, CHUNK, 0), _rows(TM, CHUNK, 1), tile, _rows(TM, CHUNK, 0), _rows(TM, CHUNK, 3),
                  tile, tile, _whole((ATTN_DIM, ATTN_DIM)), vec, vec, vec, vec, _whole((CONV_DIM, CONV_DIM))],
        out_specs=[tile, tile, tile, tile, tile, _whole((8, CHUNK))],
        out_shape=[half(BF16), half(BF16), half(BF16), half(BF16), half(F32),
                   jax.ShapeDtypeStruct((8, CHUNK), F32)],
        args=(dycat, dycat, o, ug, ug, c3, conv, head_mean, g_attn, g_conv, ln_g, ln_b, w_pw,), ride=ride)


def _conv_bwd(dconv, ug, dw_w, name, ride=None):
    T = dconv.shape[0]
    per = TM // HALO
    last = T // HALO - 1
    n_tiles = T // TM

    def body(d_ref, dn_ref, cv_ref, cg_ref, cvh_ref, cgh_ref, w_ref, dcv_ref, dcg_ref, dw_ref,
             dpad_ref, cpad_ref, dsh_ref, csh_ref, dw_acc):
        i = pl.program_id(0)

        @pl.when(i == 0)
        def _():
            dw_acc[...] = jnp.zeros_like(dw_acc)

        tail = jnp.zeros((SUBLANES, CONV_DIM), F32)
        dpad_ref[0:TM, :] = d_ref[...]
        dpad_ref[TM:TM + HALO, :] = jnp.where(i == n_tiles - 1, 0.0, dn_ref[...])
        dpad_ref[TM + HALO:, :] = tail
        halo = cvh_ref[...].astype(F32) * _sigmoid(cgh_ref[...].astype(F32))
        cpad_ref[0:HALO, :] = jnp.where(i == 0, 0.0, halo)
        cpad_ref[HALO:HALO + TM, :] = cv_ref[...].astype(F32) * _sigmoid(cg_ref[...].astype(F32))
        cpad_ref[HALO + TM:, :] = tail
        _shifted_copies(dpad_ref, dsh_ref)
        _shifted_copies(cpad_ref, csh_ref)
        taps = [w_ref[t:t + 1, :] for t in range(CONV_WIDTH)]

        def rows(j, _):
            r = pl.multiple_of(j * CONV_ROWS, CONV_ROWS)
            d = d_ref[pl.ds(r, CONV_ROWS), :]
            dc = jnp.zeros((CONV_ROWS, CONV_DIM), F32)
            for t in range(CONV_WIDTH):
                b, a = _shift_of(CONV_WIDTH - 1 - t)
                dc = dc + taps[t] * dsh_ref[b, pl.ds(r + a, CONV_ROWS), :]
                b, a = _shift_of(HALO - (CONV_WIDTH - 1) + t)
                prod = d * csh_ref[b, pl.ds(r + a, CONV_ROWS), :]
                dw_acc[t] += jnp.sum(prod.reshape(CONV_ROWS // SUBLANES, SUBLANES, CONV_DIM), axis=0)
            cv = cv_ref[pl.ds(r, CONV_ROWS), :].astype(F32)
            sg = _sigmoid(cg_ref[pl.ds(r, CONV_ROWS), :].astype(F32))
            dcv_ref[pl.ds(r, CONV_ROWS), :] = (dc * sg).astype(BF16)
            dcg_ref[pl.ds(r, CONV_ROWS), :] = (dc * cv * sg * (1.0 - sg)).astype(BF16)
            return 0

        lax.fori_loop(0, TM // CONV_ROWS, rows, 0)

        @pl.when(i == n_tiles - 1)
        def _():
            dw_ref[...] = jnp.zeros_like(dw_ref)
            for t in range(CONV_WIDTH):
                dw_ref[t:t + 1, :] = jnp.sum(dw_acc[t], axis=0, keepdims=True)

    prev = lambda col: pl.BlockSpec((HALO, CHUNK), lambda i: (jnp.maximum(i * per - 1, 0), col))
    nxt = pl.BlockSpec((HALO, CONV_DIM), lambda i: (jnp.minimum((i + 1) * per, last), 0))
    half = jax.ShapeDtypeStruct((T, CHUNK), BF16)
    return _call(
        body, name=name, grid=(T // TM,),
        in_specs=[_rows(TM, CONV_DIM), nxt, _rows(TM, CHUNK, 1), _rows(TM, CHUNK, 2), prev(1), prev(2),
                  _whole((CONV_WIDTH, CONV_DIM))],
        out_specs=[_rows(TM, CHUNK), _rows(TM, CHUNK), _whole((HALO, CONV_DIM))],
        out_shape=[half, half, jax.ShapeDtypeStruct((HALO, CONV_DIM), F32)],
        scratch_shapes=[pltpu.VMEM((TM + HALO + SUBLANES, CONV_DIM), F32),
                        pltpu.VMEM((TM + HALO + SUBLANES, CONV_DIM), F32),
                        pltpu.VMEM((SUBLANES, TM + HALO, CONV_DIM), F32),
                        pltpu.VMEM((SUBLANES, TM + HALO, CONV_DIM), F32),
                        pltpu.VMEM((HALO, SUBLANES, CONV_DIM), F32)],
        args=(dconv, dconv, ug, ug, ug, ug, dw_w,), ride=ride)


def _attn_bwd(qs, k, v, do, cs, tri, tri_t, name, ride=None):
    T = qs.shape[0]
    nq = T // BLK
    width = LANES * ATT_COLS
    chains = [(c, half) for c in range(ATT_COLS) for half in range(2)]

    def body(q_ref, k_ref, v_ref, do_ref, cs_ref, m_ref, mt_ref, dq_ref, dk_ref, dv_ref, dk_acc, dv_acc):
        qi = pl.program_id(1)

        @pl.when(qi == 0)
        def _():
            dk_acc[...] = jnp.zeros_like(dk_acc)
            dv_acc[...] = jnp.zeros_like(dv_acc)

        lane = lax.broadcasted_iota(jnp.int32, (BLK, LANES), 1)
        first = lane < HEAD_DIM
        causal = (lax.broadcasted_iota(jnp.int32, (BLK, BLK), 1)
                  < lax.broadcasted_iota(jnp.int32, (BLK, BLK), 0))
        tri_m = m_ref[...]
        tri_mt = mt_ref[...]

        def halves(x):
            zero = jnp.zeros_like(x)
            return jnp.where(first, x, zero), jnp.where(first, zero, x)

        qh, doh, cs = {}, {}, []
        for c in range(ATT_COLS):
            qh[c, 0], qh[c, 1] = halves(q_ref[:, c * LANES:(c + 1) * LANES])
            doh[c, 0], doh[c, 1] = halves(do_ref[:, c * LANES:(c + 1) * LANES])
            cs.append(cs_ref[:, c * LANES:(c + 1) * LANES])

        def step(kb, state, masked):
            prefixes, dq_accs = state
            start = pl.multiple_of(kb * BLK, BLK)
            kblk = [k_ref[pl.ds(start, BLK), c * LANES:(c + 1) * LANES] for c in range(ATT_COLS)]
            vblk = [v_ref[pl.ds(start, BLK), c * LANES:(c + 1) * LANES] for c in range(ATT_COLS)]
            prefixes, dq_accs = list(prefixes), list(dq_accs)
            for g0 in range(0, len(chains), CHAIN_GROUP):
                ids = range(g0, g0 + CHAIN_GROUP)
                grp = [chains[n] for n in ids]
                z = [_nt(qh[ch], kblk[ch[0]]) for ch in grp]
                da = [_nt(doh[ch], vblk[ch[0]]) for ch in grp]
                parts = [_softplus_parts(zi) for zi in z]
                sp = [pt[1] for pt in parts]
                if masked:
                    sp = [jnp.where(causal, s, 0.0) for s in sp]
                incl = [_dot_hilo(s, tri_m) for s in sp]
                carries = [jnp.sum(jnp.where(lane == kb + HEAD_DIM * half, cs[c], 0.0), axis=1, keepdims=True)
                           for c, half in grp]
                a = [jnp.exp(zi - ii - ci) for zi, ii, ci in zip(z, incl, carries)]
                if masked:
                    a = [jnp.where(causal, ai, 0.0) for ai in a]
                w = [ai * di for ai, di in zip(a, da)]
                pinc = [_nn(wi.astype(BF16), tri_mt) for wi in w]
                dz = [wi - pt[0] * (pi + prefixes[n]) for n, wi, pt, pi in zip(ids, w, parts, pinc)]
                if masked:
                    dz = [jnp.where(causal, d, 0.0) for d in dz]
                for j in range(0, CHAIN_GROUP, 2):
                    c = grp[j][0]
                    k0, k1 = halves(kblk[c])
                    dz0, dz1 = dz[j].astype(BF16), dz[j + 1].astype(BF16)
                    a0, a1 = a[j].astype(BF16), a[j + 1].astype(BF16)
                    dq_accs[c] = dq_accs[c] + _nn(dz0, k0) + _nn(dz1, k1)
                    dk_acc[pl.ds(start, BLK), c * LANES:(c + 1) * LANES] += _tn(dz0, qh[c, 0]) + _tn(dz1, qh[c, 1])
                    dv_acc[pl.ds(start, BLK), c * LANES:(c + 1) * LANES] += _tn(a0, doh[c, 0]) + _tn(a1, doh[c, 1])
                for n, pi in zip(ids, pinc):
                    prefixes[n] = prefixes[n] + pi[:, BLK - 1:BLK]
            return tuple(prefixes), tuple(dq_accs)

        state = (tuple(jnp.zeros((BLK, 1), F32) for _ in chains),
                 tuple(jnp.zeros((BLK, LANES), F32) for _ in range(ATT_COLS)))
        first_block = jnp.max(jnp.where(lane == FIRST_BLOCK_LANE, cs[0], 0.0)).astype(jnp.int32)
        state = lax.fori_loop(first_block, qi, lambda kb, st: step(kb, st, False), state)
        state = step(qi, state, True)
        for c in range(ATT_COLS):
            dq_ref[:, c * LANES:(c + 1) * LANES] = (state[1][c] * (HEAD_DIM ** -0.5)).astype(BF16)

        @pl.when(qi == nq - 1)
        def _():
            dk_ref[...] = dk_acc[...].astype(BF16)
            dv_ref[...] = dv_acc[...].astype(BF16)

    blk = pl.BlockSpec((BLK, width), lambda j, i: (i, j))
    col = pl.BlockSpec((T, width), lambda j, i: (0, j))
    out = jax.ShapeDtypeStruct((T, ATTN_DIM), BF16)
    return _call(
        body, name=name, grid=(ATTN_DIM // width, nq),
        in_specs=[blk, col, col, blk, blk, _whole((BLK, BLK)), _whole((BLK, BLK))],
        out_specs=[blk, col, col], out_shape=[out, out, out],
        scratch_shapes=[pltpu.VMEM((T, width), F32), pltpu.VMEM((T, width), F32)],
        args=(qs, k, v, do, cs, tri, tri_t,), ride=ride)


def _inproj_bwd(du, w_in_t, h, dh1, gain, name, ride=None):
    T = h.shape[0]

    def body(*refs):
        du_refs = refs[:N_CHUNK]
        w_ref, h_ref, dh1_ref, g_ref, dh_ref, gsum_ref = refs[N_CHUNK:]

        @pl.when(pl.program_id(0) == 0)
        def _():
            gsum_ref[...] = jnp.zeros_like(gsum_ref)

        dhn = jnp.zeros((TM, D_MODEL), F32)
        for j in range(N_CHUNK):
            dhn = dhn + _nn(du_refs[j][...], w_ref[j * CHUNK:(j + 1) * CHUNK, :])
        hv = h_ref[...]
        r = lax.rsqrt(jnp.mean(hv * hv, axis=-1, keepdims=True) + EPS)
        xh = hv * r
        gsum_ref[...] += jnp.sum(dhn * xh, axis=0, keepdims=True)
        dxh = dhn * g_ref[...]
        dh_ref[...] = dh1_ref[...] + r * (dxh - xh * jnp.mean(dxh * xh, axis=-1, keepdims=True))

    full = _rows(TM, D_MODEL)
    return _call(
        body, name=name, grid=(T // TM,),
        in_specs=[_rows(TM, CHUNK)] * N_CHUNK + [_whole((N_CHUNK * CHUNK, D_MODEL)), full, full,
                                                 _whole((1, D_MODEL))],
        out_specs=[full, _whole((1, D_MODEL))],
        out_shape=[jax.ShapeDtypeStruct((T, D_MODEL), F32), jax.ShapeDtypeStruct((1, D_MODEL), F32)],
        args=(*du, w_in_t, h, dh1, gain), ride=ride)


def _weight_grad(lhs_list, rhs, name, tk=CHUNK, ride=None):
    T, n_rhs = rhs.shape
    n = len(lhs_list)
    ka = lhs_list[0].shape[1]
    per = ka // tk

    def body(*refs):
        a_refs, b_ref, out_ref = refs[:n], refs[n], refs[n + 1]
        step = pl.program_id(0)
        for j in range(n):
            for s in range(per):
                @pl.when(step == j * per + s)
                def _(j=j, s=s):
                    out_ref[...] = _tn(a_refs[j][:, s * tk:(s + 1) * tk], b_ref[...]).astype(BF16)

    (grad,), landed = _call(
        body, name=name, grid=(n * per,),
        in_specs=[_whole((T, ka))] * n + [_whole((T, n_rhs))],
        out_specs=[pl.BlockSpec((tk, n_rhs), lambda i: (i, 0))],
        out_shape=[jax.ShapeDtypeStruct((n * ka, n_rhs), BF16)],
        args=(*lhs_list, rhs), ride=ride)
    return grad, landed


def _adamw_update(w, g, m, v):
    nm = ADAM_B1 * m + (1.0 - ADAM_B1) * g
    nv = ADAM_B2 * v + (1.0 - ADAM_B2) * (g * g)
    m_hat = nm / (1.0 - ADAM_B1 ** ADAM_STEP)
    v_hat = nv / (1.0 - ADAM_B2 ** ADAM_STEP)
    return -ADAM_LR * (m_hat / (jnp.sqrt(v_hat) + ADAM_EPS) + ADAM_WD * w), nm, nv


def _sum_adamw(slots, w, m, v, name):
    depth, R, C = w.shape
    tr = next(rows for rows in ADAMW_ROWS if R % rows == 0)

    def body(*refs):
        slot_refs, (w_ref, m_ref, v_ref, g_ref, d_ref, nm_ref, nv_ref) = refs[:depth], refs[depth:]
        for layer in range(depth):
            @pl.when(pl.program_id(0) == layer)
            def _(src=slot_refs[layer]):
                g = src[0].astype(F32)
                for s in range(1, src.shape[0]):
                    g = g + src[s].astype(F32)
                g_ref[0] = g
                d_ref[0], nm_ref[0], nv_ref[0] = _adamw_update(w_ref[0], g, m_ref[0], v_ref[0])

    slot_spec = lambda layer: pl.BlockSpec((slots[layer].shape[0], tr, C),
                                           lambda l, i: (0, jnp.where(l == layer, i, 0), 0))
    spec = pl.BlockSpec((1, tr, C), lambda l, i: (l, i, 0))
    out = jax.ShapeDtypeStruct((depth, R, C), F32)
    return pl.pallas_call(
        body, name=name, grid=(depth, R // tr),
        in_specs=[slot_spec(layer) for layer in range(depth)] + [spec] * 3,
        out_specs=[spec] * 4, out_shape=[out] * 4,
        compiler_params=_params(2),
    )(*slots, w, m, v)


def _adamw(w, g, m, v, name):
    R, C = w.shape
    tr = R
    for cand in (512, 256, 128, 64):
        if R % cand == 0 and R > cand:
            tr = cand
            break

    def body(w_ref, g_ref, m_ref, v_ref, d_ref, nm_ref, nv_ref):
        d_ref[...], nm_ref[...], nv_ref[...] = _adamw_update(w_ref[...], g_ref[...], m_ref[...], v_ref[...])

    spec = pl.BlockSpec((tr, C), lambda i: (i, 0))
    out = jax.ShapeDtypeStruct((R, C), F32)
    return pl.pallas_call(
        body, name=name, grid=(R // tr,),
        in_specs=[spec] * 4, out_specs=[spec] * 3, out_shape=[out, out, out],
        compiler_params=_params(1),
    )(w, g, m, v)


def _pack_small(values, scalar=None):
    pad = lambda a: jnp.pad(a, ((0, 0), (0, D_MODEL - a.shape[1])))
    last = jnp.zeros((1, D_MODEL), F32) if scalar is None else pad(scalar.reshape(1, 1))
    return jnp.concatenate([pad(values[name].reshape(rows, cols)) for name, _, rows, cols in SMALL_LAYOUT] + [last],
                           axis=0)


def _small_update(all_packs, state, name):
    n = len(SMALL_LAYOUT)

    def body(packs_ref, *refs):
        ins, outs = refs[:3 * n], refs[3 * n:]
        total = packs_ref[0]
        for s in range(1, N_DEV):
            total = total + packs_ref[s]
        for j, (_, at, rows, cols) in enumerate(SMALL_LAYOUT):
            g = total[at:at + rows, :cols]
            w_ref, m_ref, v_ref = ins[3 * j:3 * j + 3]
            outs[4 * j][...] = g
            outs[4 * j + 1][...], outs[4 * j + 2][...], outs[4 * j + 3][...] = _adamw_update(
                w_ref[...], g, m_ref[...], v_ref[...])
        outs[-2][...] = total[LOSS_ROW:LOSS_ROW + 1, :LANES]
        outs[-1][...] = total[SMALL_ROWS:, :]

    shapes = [jax.ShapeDtypeStruct((rows, cols), F32) for _, _, rows, cols in SMALL_LAYOUT for _ in range(4)]
    shapes += [jax.ShapeDtypeStruct((1, LANES), F32), jax.ShapeDtypeStruct((PACK_ROWS - SMALL_ROWS, D_MODEL), F32)]
    operands = [a for item in SMALL_LAYOUT for a in state[item[0]]]
    res = pl.pallas_call(body, name=name, out_shape=shapes, compiler_params=_params())(all_packs, *operands)
    per_name = {item[0]: tuple(res[4 * j:4 * j + 4]) for j, item in enumerate(SMALL_LAYOUT)}
    return per_name, res[-2][0, 0], res[-1]


def kernel(x, p, norm_g, w_in, attn_out_g, dw_w, dw_b, conv_ln_g, conv_ln_b, w_pw, conv_out_g, w_out, ple_norm_g, w_ple_gate, w_ple, final_g, loss_target, m_norm_g, m_w_in, m_attn_out_g, m_dw_w, m_dw_b, m_conv_ln_g, m_conv_ln_b, m_w_pw, m_conv_out_g, m_w_out, m_ple_norm_g, m_w_ple_gate, m_w_ple, m_final_g, v_norm_g, v_w_in, v_attn_out_g, v_dw_w, v_dw_b, v_conv_ln_g, v_conv_ln_b, v_w_pw, v_conv_out_g, v_w_out, v_ple_norm_g, v_w_ple_gate, v_w_ple, v_final_g):
    depth = w_in.shape[0]
    T = x.shape[1]
    given = dict(
        norm_g=norm_g, ple_norm_g=ple_norm_g, final_g=final_g, dw_b=dw_b, conv_ln_g=conv_ln_g, conv_ln_b=conv_ln_b,
        conv_out_g=conv_out_g, attn_out_g=attn_out_g,
        m_norm_g=m_norm_g, m_ple_norm_g=m_ple_norm_g, m_final_g=m_final_g, m_dw_b=m_dw_b, m_conv_ln_g=m_conv_ln_g,
        m_conv_ln_b=m_conv_ln_b, m_conv_out_g=m_conv_out_g, m_attn_out_g=m_attn_out_g,
        v_norm_g=v_norm_g, v_ple_norm_g=v_ple_norm_g, v_final_g=v_final_g, v_dw_b=v_dw_b, v_conv_ln_g=v_conv_ln_g,
        v_conv_ln_b=v_conv_ln_b, v_conv_out_g=v_conv_out_g, v_attn_out_g=v_attn_out_g)
    my_idx = 4 * lax.axis_index("x") + 2 * lax.axis_index("y") + lax.axis_index("c")

    ids = jnp.arange(BLK)
    tri = (ids[:, None] >= ids[None, :]).astype(BF16)
    tri_t = (ids[:, None] <= ids[None, :]).astype(BF16)
    hid = jnp.arange(ATTN_DIM) // HEAD_DIM
    head_mean = ((hid[:, None] == hid[None, :]).astype(F32) / HEAD_DIM).astype(BF16)

    w_names = ("w_in_t", "w_pw", "w_out", "w_gate", "w_ple")
    w_axes = dict(zip(w_names, (0, 0, 0, 0, 1)))
    shards = [dict(zip(w_names, (w_in[l].T.astype(BF16), w_pw[l].astype(BF16), w_out[l].astype(BF16),
                                 w_ple_gate[l].astype(BF16), w_ple[l].astype(BF16)))) for l in range(depth)]
    first = _all_gather([shards[0]["w_in_t"]] + [dw_w[l].T for l in range(depth)], [0] * (1 + depth),
                        "gather_weights_0")
    layers = []
    for l in range(depth):
        layers.append(dict(
            dw_w=first[1 + l].T,
            g_norm=norm_g[l][None], g_attn=jnp.tile(attn_out_g[l], N_HEADS)[None], dw_b=dw_b[l][None],
            ln_g=conv_ln_g[l][None], ln_b=conv_ln_b[l][None], g_conv=conv_out_g[l][None],
            g_ple=ple_norm_g[l][None], p=p[l, 0]))
    layers[0]["w_in_t"] = first[0]

    def rest_of(l, names):
        return [_Ride.gather2(shards[l][n], w_axes[n]) for n in names]

    h = x[0]
    saved = []
    for l, w in enumerate(layers):
        early, late = (w_names[3:], w_names[1:3]) if l == 0 else ((), ())
        (qs, k, v, ug, hn), landed = _prenorm_inproj(h, w["g_norm"], w["w_in_t"], f"inproj_{l}",
                                                     _Ride(rest_of(l, early)))
        w.update(zip(early, landed))
        ahead = [_Ride.gather2(shards[l + 1]["w_in_t"], 0, None, 0, W_IN_ROWS_ON_ATTN)] if l + 1 < depth else []
        own = w_names[1:] if l > 0 else ()
        (o, cs), landed = _attn_fwd(qs, k, v, tri, f"attn_fwd_{l}", _Ride(ahead + rest_of(l, own)))
        w_in_next = landed[:len(ahead)]
        w.update(zip(own, landed[len(ahead):]))
        (conv, c2), landed = _conv_fwd(ug, w["dw_w"], w["dw_b"], w["ln_g"], w["ln_b"], f"conv_fwd_{l}",
                                       _Ride(rest_of(l, late)))
        w.update(zip(late, landed))
        tail = [_Ride.gather2(shards[l + 1]["w_in_t"], 0, a, W_IN_ROWS_ON_ATTN,
                              shards[l + 1]["w_in_t"].shape[0] - W_IN_ROWS_ON_ATTN) for a in w_in_next]
        (h2, h1, ycat, hn2, gate, e, c3), landed = _mix_out_ple(
            o, ug, c2, h, w["p"], head_mean, w["g_attn"], w["g_conv"], w["g_ple"],
            w["w_pw"], w["w_out"], w["w_gate"], w["w_ple"], f"mix_{l}", _Ride(tail))
        if landed:
            layers[l + 1]["w_in_t"] = landed[0]
        saved.append(dict(h=h, qs=qs, k=k, v=v, ug=ug, hn=hn, o=o, cs=cs, conv=conv, c2=c2, h1=h1,
                          ycat=ycat, hn2=hn2, gate=gate, e=e, c3=c3))
        h = h2
    dh, g_final, loss_part = _final_loss(h, loss_target[0], final_g[None], "final_loss")

    small = {}
    dww_parts = [None] * depth
    slots = [dict() for _ in range(depth)]
    g_w_in = None
    pending = []
    for l in reversed(range(depth)):
        w, s = layers[l], saved[l]
        above = pending

        def part(i, above=above, g=g_w_in):
            return [_Ride.scatter(g, 0, above[0], *W_IN_GRAD_PARTS[i])] if above else []

        def scattered(grads, names):
            return [_Ride.scatter(grads[n], w_axes[n]) for n in names]

        (dh1, dh1b, dzg, de, dycat, g_ple_sum), landed = _ple_out_bwd(
            dh, s["gate"], s["e"], s["h1"], w["g_ple"], w["w_gate"], w["w_out"], f"ple_bwd_{l}", _Ride(part(1)))
        above[:1] = landed
        (do, dga, dgc, dc3, dconv, sums), landed = _branch_bwd(
            dycat, s["o"], s["ug"], s["c3"], s["conv"], head_mean, w["g_attn"], w["g_conv"],
            w["ln_g"], w["ln_b"], w["w_pw"], f"branch_bwd_{l}", _Ride(part(2)))
        above[:1] = landed
        grads = dict(
            w_pw=_weight_grad([s["c2"]], dc3, f"grad_w_pw_{l}")[0],
            w_out=_weight_grad([s["ycat"]], dh1b, f"grad_w_out_{l}")[0],
            w_gate=_weight_grad([s["hn2"]], dzg, f"grad_w_gate_{l}")[0],
            w_ple=_weight_grad([w["p"].astype(BF16)], de, f"grad_w_ple_{l}", tk=PLE_DIM)[0])
        (dcv, dcg, dww), landed = _conv_bwd(dconv, s["ug"], w["dw_w"], f"conv_bwd_{l}", _Ride(part(3)))
        if above:
            slots[l + 1]["w_in_t"] = landed[0]
        (dq, dk, dv), landed = _attn_bwd(s["qs"], s["k"], s["v"], do, s["cs"], tri, tri_t, f"attn_bwd_{l}",
                                         _Ride(scattered(grads, w_names[1:])))
        slots[l].update(zip(w_names[1:], landed))
        du = [dq, dk, dv, dga, dcv, dcg, dgc]
        g_w_in_here, _ = _weight_grad(du, s["hn"], f"grad_w_in_{l}")
        if l == 0:
            tail = [_Ride.scatter_chips(_pair_reduce(g_w_in_here, f"pair_reduce_w_in_{l}"))]
        else:
            tail = [_Ride.scatter(g_w_in_here, 0, None, *W_IN_GRAD_PARTS[0])]
        (dh, g_norm_sum), landed = _inproj_bwd(du, w["w_in_t"], s["h"], dh1, w["g_norm"], f"inproj_bwd_{l}",
                                               _Ride(tail))
        if l == 0:
            slots[l]["w_in_t"] = landed[0]
        pending = landed if l > 0 else []
        g_w_in = g_w_in_here
        small[l] = dict(norm_g=g_norm_sum, ple_norm_g=g_ple_sum, attn_out_g=sums[0].reshape(N_HEADS, HEAD_DIM).sum(0),
                        conv_out_g=sums[1], conv_ln_g=sums[2], conv_ln_b=sums[3], dw_b=sums[4])
        dww_parts[l] = dww[:CONV_WIDTH]
    slots = [[sl[n] for n in w_names] for sl in slots]
    grad_x = dh[None]

    sums_of = {name: jnp.stack([small[l][name].reshape(-1) for l in range(depth)]) for name in small[0]}
    sums_of["final_g"] = g_final
    pack = jnp.concatenate([_pack_small(sums_of, scalar=loss_part[0, 0]), jnp.concatenate(dww_parts, axis=1),
                            jnp.zeros((PACK_ROWS - SMALL_ROWS - CONV_WIDTH, D_MODEL), F32)], axis=0)
    (all_packs,) = _all_gather([pack], [0], "gather_small_grads")
    state = {name: [given[pre + name].reshape(rows, cols) for pre in ("", "m_", "v_")]
             for name, _, rows, cols in SMALL_LAYOUT}
    updated, loss, dww_sum = _small_update(all_packs.reshape(N_DEV, PACK_ROWS, D_MODEL), state, "update_small")
    res = {kind: {name: val[k].reshape(given[name].shape) for name, val in updated.items()}
           for k, kind in enumerate("gdmv")}
    dww_full = dww_sum[:CONV_WIDTH].reshape(CONV_WIDTH, depth, CONV_DIM).transpose(1, 0, 2)
    g_dw_w = lax.dynamic_slice_in_dim(dww_full, my_idx * (CONV_DIM // N_DEV), CONV_DIM // N_DEV, axis=2)

    swap = lambda a: a.transpose(0, 2, 1)
    state = {"w_in": (w_in, m_w_in, v_w_in), "w_pw": (w_pw, m_w_pw, v_w_pw), "w_out": (w_out, m_w_out, v_w_out),
             "w_ple_gate": (w_ple_gate, m_w_ple_gate, v_w_ple_gate), "w_ple": (w_ple, m_w_ple, v_w_ple)}
    for at, name in enumerate(state):
        wv, mv, vv = [swap(a) for a in state[name]] if name == "w_in" else state[name]
        out = _sum_adamw([slots[l][at] for l in range(depth)], wv, mv, vv, f"adamw_{name}")
        out = [swap(a) for a in out] if name == "w_in" else out
        res["g"][name], res["d"][name], res["m"][name], res["v"][name] = out
    flat = lambda a: a.reshape(-1, a.shape[-1])
    res["g"]["dw_w"] = g_dw_w
    res["d"]["dw_w"], res["m"]["dw_w"], res["v"]["dw_w"] = [
        a.reshape(dw_w.shape) for a in _adamw(flat(dw_w), flat(g_dw_w), flat(m_dw_w), flat(v_dw_w), "adamw_dw_w")]

    order = ["norm_g", "w_in", "attn_out_g", "dw_w", "dw_b", "conv_ln_g", "conv_ln_b", "w_pw", "conv_out_g",
             "w_out", "ple_norm_g", "w_ple_gate", "w_ple", "final_g"]
    return (loss, grad_x, *[res["g"][n] for n in order], *[res["d"][n] for n in order],
            *[res["m"][n] for n in order], *[res["v"][n] for n in order])
```

```python
import functools

import jax
import jax.numpy as jnp
from jax import lax
from jax.experimental import pallas as pl
from jax.experimental.pallas import tpu as pltpu

F32 = jnp.float32
BF16 = jnp.bfloat16
MESH = pl.DeviceIdType.MESH

N_DEV = 8
D_MODEL = 1024
ATTN_DIM = 512
CONV_DIM = 512
HEAD_DIM = 64
N_HEADS = 8
CONV_WIDTH = 31
PLE_DIM = 256
CHUNK = 512
N_CHUNK = 7
EPS = 1e-6
ADAM_LR = 0.001
ADAM_B1 = 0.9
ADAM_B2 = 0.999
ADAM_EPS = 1e-08
ADAM_WD = 0.01
ADAM_STEP = 10

LANES = 128
BLK = 256
ATT_COLS = 4
CHAIN_GROUP = 4
SOFTPLUS_LINEAR_AT = 20.0
DEAD_AT = 110.0
FIRST_BLOCK_LANE = HEAD_DIM - 1
TM = 512
HALO = 32
SUBLANES = 8
CONV_ROWS = 32
ADAMW_ROWS = (224, 128, 64)
VMEM_LIMIT = 56 * 1024 * 1024
SMALL_ROWS = 16
SMALL_LAYOUT = (("norm_g", 0, 2, D_MODEL), ("ple_norm_g", 2, 2, D_MODEL), ("final_g", 4, 1, D_MODEL),
                ("dw_b", 5, 2, CONV_DIM), ("conv_ln_g", 7, 2, CONV_DIM), ("conv_ln_b", 9, 2, CONV_DIM),
                ("conv_out_g", 11, 2, CONV_DIM), ("attn_out_g", 13, 2, HEAD_DIM))
LOSS_ROW = 15
W_IN_ROWS_ON_ATTN = 288
W_IN_GRAD_PARTS = ((0, 112), (112, 96), (208, 64), (272, 176))
PACK_ROWS = 48


def _nn(a, b):
    return lax.dot_general(a, b, (((1,), (0,)), ((), ())), preferred_element_type=F32)


def _nt(a, b):
    return lax.dot_general(a, b, (((1,), (1,)), ((), ())), preferred_element_type=F32)


def _tn(a, b):
    return lax.dot_general(a, b, (((0,), (0,)), ((), ())), preferred_element_type=F32)


def _split(x):
    hi = x.astype(BF16)
    lo = (x - hi.astype(F32)).astype(BF16)
    return hi, lo


def _dot_hilo(x, m):
    hi, lo = _split(x)
    return _nn(hi, m) + _nn(lo, m)


def _sigmoid(x):
    return jax.nn.sigmoid(x)


def _dsilu(x, s):
    return s * (1.0 + x * (1.0 - s))


def _params(n_grid=0, vmem=VMEM_LIMIT):
    sem = ("arbitrary",) * n_grid if n_grid else None
    return pltpu.CompilerParams(dimension_semantics=sem, vmem_limit_bytes=vmem)


def _rows(tm, cols, col=0):
    return pl.BlockSpec((tm, cols), lambda i: (i, col))


def _whole(shape):
    zeros = (0,) * len(shape)
    return pl.BlockSpec(shape, lambda *_: zeros)


def _my_position():
    return lax.axis_index("x"), lax.axis_index("y"), lax.axis_index("c")


def _block(ref, axis, idx, size):
    start = pl.multiple_of(idx * size, size)
    if axis == 0:
        return ref.at[pl.ds(start, size), :]
    return ref.at[:, pl.ds(start, size)]


def _all_gather(shards, axes, name):
    n = len(shards)
    sizes = [s.shape[a] for s, a in zip(shards, axes)]

    def full_shape(s, a):
        shape = list(s.shape)
        shape[a] *= N_DEV
        return jax.ShapeDtypeStruct(tuple(shape), s.dtype)

    def body(*refs):
        ins, outs = refs[:n], refs[n:2 * n]
        send_sems, recv_sems, local_sems = refs[2 * n:]
        x, y, c = _my_position()
        me, sibling = (x, y, c), (x, y, 1 - c)
        chips = [(1 - x, y), (x, 1 - y), (1 - x, 1 - y)]

        def place(i, dev):
            return _block(outs[i], axes[i], 4 * dev[0] + 2 * dev[1] + dev[2], sizes[i])

        def copy(k, i, dev, to, src=None):
            return pltpu.make_async_remote_copy(
                src_ref=place(i, dev) if src is None else src, dst_ref=place(i, dev),
                send_sem=send_sems.at[k, i], recv_sem=recv_sems.at[k, i],
                device_id=to, device_id_type=MESH)

        mine = [pltpu.make_async_copy(ins[i], place(i, me), local_sems.at[i]) for i in range(n)]
        for cp in mine:
            cp.start()
        first = [copy(0, i, me, sibling, src=ins[i]) for i in range(n)]
        for j, chip in enumerate(chips):
            first += [copy(1 + j, i, me, (*chip, c), src=ins[i]) for i in range(n)]
        for cp in first:
            cp.start()
        passed = []
        for j, chip in enumerate(chips):
            for i in range(n):
                copy(1 + j, i, (*chip, c), me).wait_recv()
            hop = [copy(4 + j, i, (*chip, c), sibling) for i in range(n)]
            for cp in hop:
                cp.start()
            passed += hop
        for i in range(n):
            copy(0, i, sibling, me).wait_recv()
        for j, chip in enumerate(chips):
            for i in range(n):
                copy(4 + j, i, (*chip, 1 - c), me).wait_recv()
        for cp in first + passed:
            cp.wait_send()
        for cp in mine:
            cp.wait()

    any_spec = pl.BlockSpec(memory_space=pl.ANY)
    return pl.pallas_call(
        body, name=name,
        out_shape=[full_shape(s, a) for s, a in zip(shards, axes)],
        in_specs=[any_spec] * n, out_specs=[any_spec] * n,
        scratch_shapes=[pltpu.SemaphoreType.DMA((7, n)), pltpu.SemaphoreType.DMA((7, n)),
                        pltpu.SemaphoreType.DMA((n,))],
    )(*shards)


def _pair_reduce(g, name):
    n_chips = N_DEV // 2
    R, C = g.shape[0] // N_DEV, g.shape[1]

    def body(g_ref, out_ref, mine_ref, theirs_ref, send_sems, recv_sems, local_sems):
        x, y, c = _my_position()
        block = lambda d: g_ref.at[pl.ds(pl.multiple_of(d * R, 16), R), :]
        sends = [pltpu.make_async_remote_copy(
            src_ref=block(2 * j + 1 - c), dst_ref=theirs_ref.at[j], send_sem=send_sems.at[j],
            recv_sem=recv_sems.at[j], device_id=(x, y, 1 - c), device_id_type=MESH) for j in range(n_chips)]
        own = [pltpu.make_async_copy(block(2 * j + c), mine_ref.at[j], local_sems.at[j]) for j in range(n_chips)]
        for cp in sends + own:
            cp.start()
        for j in range(n_chips):
            own[j].wait()
            sends[j].wait_recv()
            out_ref[j] = (mine_ref[j].astype(F32) + theirs_ref[j].astype(F32)).astype(g.dtype)
        for cp in sends:
            cp.wait_send()

    half = pltpu.VMEM((n_chips, R, C), g.dtype)
    sems = pltpu.SemaphoreType.DMA((n_chips,))
    return pl.pallas_call(
        body, name=name, out_shape=jax.ShapeDtypeStruct((n_chips, R, C), g.dtype),
        in_specs=[pl.BlockSpec(memory_space=pl.ANY)], out_specs=pl.BlockSpec(memory_space=pltpu.VMEM),
        scratch_shapes=[half, half, sems, sems, sems], compiler_params=_params(),
    )(g)


class _Ride:
    def __init__(self, parts):
        self.parts = [p for p in parts if p is not None]

    @staticmethod
    def gather(src, axis, land=None, lo=0, n=None):
        return ("gather", src, land, axis, lo, src.shape[axis] if n is None else n)

    @staticmethod
    def gather2(src, axis, land=None, lo=0, n=None):
        return ("gather2", src, land, axis, lo, src.shape[axis] if n is None else n)

    @staticmethod
    def scatter(src, axis, land=None, lo=0, n=None):
        return ("scatter", src, land, axis, lo, src.shape[axis] // N_DEV if n is None else n)

    @staticmethod
    def scatter_chips(chip_sums):
        return ("scatter_chips", chip_sums, None, 0, 0, chip_sums.shape[1])

    def arrays(self):
        return [p[1] for p in self.parts] + [p[2] for p in self.parts if p[2] is not None]

    def out_shapes(self):
        out = []
        for kind, src, _, axis, _, _ in self.parts:
            shape = list(src.shape)
            if kind in ("gather", "gather2"):
                shape[axis] *= N_DEV
            elif kind == "scatter_chips":
                pass
            else:
                shape[axis] //= N_DEV
                shape = [N_DEV] + shape
            out.append(jax.ShapeDtypeStruct(tuple(shape), src.dtype))
        return out

    def aliases(self, n_in, n_out):
        m, out = len(self.parts), {}
        for j, p in enumerate(self.parts):
            if p[2] is not None:
                out[n_in + m + len(out)] = n_out + j
        return out

    def scratch(self):
        m = len(self.parts)
        return [pltpu.SemaphoreType.DMA((N_DEV - 1, m)), pltpu.SemaphoreType.DMA((N_DEV - 1, m)),
                pltpu.SemaphoreType.DMA((m,))]

    def _copies(self, src_refs, land_refs, sems):
        send_sems, recv_sems, local_sems = sems
        x, y, c = _my_position()
        my_idx = 4 * x + 2 * y + c
        own, sends, relays, lands = [], [], [], []
        for j, (kind, src, _, axis, lo, n) in enumerate(self.parts):
            if kind == "scatter_chips":
                for k in (0, 2, 4, 6):
                    px, py = (1 - x if k & 4 else x), (1 - y if k & 2 else y)
                    a, b = src_refs[j].at[2 * px + py], land_refs[j].at[2 * x + y]
                    if k == 0:
                        own.append(pltpu.make_async_copy(a, b, local_sems.at[j]))
                        continue
                    mk = lambda dst, a=a, k=k, j=j, to=(px, py, c): pltpu.make_async_remote_copy(
                        src_ref=a, dst_ref=dst, send_sem=send_sems.at[k - 1, j], recv_sem=recv_sems.at[k - 1, j],
                        device_id=to, device_id_type=MESH)
                    sends.append(mk(b))
                    lands.append(mk(land_refs[j].at[2 * px + py]))
                continue
            size = src.shape[axis] if kind in ("gather", "gather2") else src.shape[axis] // N_DEV
            align = 16 if axis == 0 else LANES

            def rows(ref, idx, lead=None, axis=axis, lo=lo, n=n, size=size, align=align):
                at = pl.ds(pl.multiple_of(idx * size + lo, align), n)
                where = (at, slice(None)) if axis == 0 else (slice(None), at)
                return ref.at[where] if lead is None else ref.at[(lead, *where)]

            def in_shard(ref):
                return rows(ref, 0)

            def in_slot(ref, s):
                return rows(ref, 0, lead=s)

            if kind == "gather2":
                chips = [(1 - x, y), (x, 1 - y), (1 - x, 1 - y)]
                place = lambda px, py, pc: rows(land_refs[j], 4 * px + 2 * py + pc)

                def copy(i, a, dst, to, j=j):
                    return pltpu.make_async_remote_copy(
                        src_ref=a, dst_ref=dst, send_sem=send_sems.at[i, j], recv_sem=recv_sems.at[i, j],
                        device_id=to, device_id_type=MESH)

                mine = in_shard(src_refs[j])
                own.append(pltpu.make_async_copy(mine, place(x, y, c), local_sems.at[j]))
                sends.append(copy(0, mine, place(x, y, c), (x, y, 1 - c)))
                lands.append(copy(0, mine, place(x, y, 1 - c), (x, y, 1 - c)))
                for i, (px, py) in enumerate(chips):
                    sends.append(copy(1 + i, mine, place(x, y, c), (px, py, c)))
                    relays.append((copy(1 + i, mine, place(px, py, c), (px, py, c)),
                                   copy(4 + i, place(px, py, c), place(px, py, c), (x, y, 1 - c))))
                    lands.append(copy(4 + i, mine, place(px, py, 1 - c), (x, y, 1 - c)))
                continue
            for k in range(N_DEV):
                px = 1 - x if k & 4 else x
                py = 1 - y if k & 2 else y
                pc = 1 - c if k & 1 else c
                peer_idx = 4 * px + 2 * py + pc
                if kind == "gather":
                    a, b, landed = in_shard(src_refs[j]), rows(land_refs[j], my_idx), rows(land_refs[j], peer_idx)
                else:
                    a, b, landed = rows(src_refs[j], peer_idx), in_slot(land_refs[j], my_idx), in_slot(land_refs[j], peer_idx)
                if k == 0:
                    own.append(pltpu.make_async_copy(a, b, local_sems.at[j]))
                    continue
                mk = lambda dst, a=a, k=k, j=j, to=(px, py, pc): pltpu.make_async_remote_copy(
                    src_ref=a, dst_ref=dst, send_sem=send_sems.at[k - 1, j], recv_sem=recv_sems.at[k - 1, j],
                    device_id=to, device_id_type=MESH)
                sends.append(mk(b))
                lands.append(mk(landed))
        return own, sends, relays, lands

    @property
    def relayed(self):
        return any(p[0] == "gather2" for p in self.parts)

    def start(self, src_refs, land_refs, sems):
        own, sends, _, _ = self._copies(src_refs, land_refs, sems)
        for cp in own + sends:
            cp.start()

    def relay(self, src_refs, land_refs, sems):
        for arrival, onward in self._copies(src_refs, land_refs, sems)[2]:
            arrival.wait_recv()
            onward.start()

    def wait(self, src_refs, land_refs, sems):
        own, sends, relays, lands = self._copies(src_refs, land_refs, sems)
        for cp in lands:
            cp.wait_recv()
        for cp in sends + [onward for _, onward in relays]:
            cp.wait_send()
        for cp in own:
            cp.wait()


def _call(body, *, name, grid, in_specs, out_specs, out_shape, args, scratch_shapes=(), ride=None):
    in_specs, out_specs, out_shape = list(in_specs), list(out_specs), list(out_shape)
    n_in, n_out, n_sc = len(in_specs), len(out_specs), len(scratch_shapes)
    if ride is None or not ride.parts:
        res = pl.pallas_call(body, name=name, grid=grid, in_specs=in_specs, out_specs=out_specs,
                             out_shape=out_shape, scratch_shapes=list(scratch_shapes),
                             compiler_params=_params(len(grid)))(*args)
        return list(res), []
    extra, m = ride.arrays(), len(ride.parts)

    def riding(*refs):
        a = n_in + len(extra)
        b = a + n_out
        srcs, lands, sems = refs[n_in:n_in + m], refs[b:b + m], refs[b + m + n_sc:]
        at = [pl.program_id(d) for d in range(len(grid))]

        @pl.when(functools.reduce(jnp.logical_and, [i == 0 for i in at]))
        def _():
            ride.start(srcs, lands, sems)

        if ride.relayed:
            step, n_steps = at[0], 1
            for i, g in zip(at[1:], grid[1:]):
                step = step * g + i
            for g in grid:
                n_steps *= g
            assert n_steps >= 2, "a two-level ride needs a grid step after the first"

            @pl.when(step == n_steps - 1)
            def _():
                ride.relay(srcs, lands, sems)

        body(*refs[:n_in], *refs[a:b], *refs[b + m:b + m + n_sc])

        @pl.when(functools.reduce(jnp.logical_and, [i == g - 1 for i, g in zip(at, grid)]))
        def _():
            ride.wait(srcs, lands, sems)

    hbm = pl.BlockSpec(memory_space=pl.ANY)
    res = pl.pallas_call(
        riding, name=name, grid=grid, in_specs=in_specs + [hbm] * len(extra), out_specs=out_specs + [hbm] * m,
        out_shape=out_shape + ride.out_shapes(), scratch_shapes=list(scratch_shapes) + ride.scratch(),
        input_output_aliases=ride.aliases(n_in, n_out), compiler_params=_params(len(grid)),
    )(*args, *extra)
    return list(res[:n_out]), list(res[n_out:])


def _prenorm_inproj(h, gain, w_in_t, name, ride=None):
    T = h.shape[0]

    def body(h_ref, g_ref, w_ref, q_ref, k_ref, v_ref, ug_ref, hn_ref):
        hv = h_ref[...]
        r = lax.rsqrt(jnp.mean(hv * hv, axis=-1, keepdims=True) + EPS)
        hn = (hv * r * g_ref[...]).astype(BF16)
        hn_ref[...] = hn
        for j in range(N_CHUNK):
            u = _nt(hn, w_ref[j * CHUNK:(j + 1) * CHUNK, :])
            if j == 0:
                q_ref[...] = (u * (HEAD_DIM ** -0.5)).astype(BF16)
            elif j == 1:
                k_ref[...] = u.astype(BF16)
            elif j == 2:
                v_ref[...] = u.astype(BF16)
            else:
                ug_ref[:, (j - 3) * CHUNK:(j - 2) * CHUNK] = u.astype(BF16)

    act = jax.ShapeDtypeStruct((T, CHUNK), BF16)
    return _call(
        body, name=name, grid=(T // TM,),
        in_specs=[_rows(TM, D_MODEL), _whole((1, D_MODEL)), _whole((N_CHUNK * CHUNK, D_MODEL))],
        out_specs=[_rows(TM, CHUNK)] * 3 + [_rows(TM, 4 * CHUNK), _rows(TM, D_MODEL)],
        out_shape=[act, act, act, jax.ShapeDtypeStruct((T, 4 * CHUNK), BF16),
                   jax.ShapeDtypeStruct((T, D_MODEL), BF16)],
        args=(h, gain, w_in_t,), ride=ride)


def _softplus_parts(z):
    ez = jnp.exp(jnp.minimum(z, SOFTPLUS_LINEAR_AT))
    t = 1.0 + ez
    return ez * pl.reciprocal(t, approx=True), jnp.where(z > SOFTPLUS_LINEAR_AT, z, jnp.log(t))


def _attn_fwd(qs, k, v, tri, name, ride=None):
    T = qs.shape[0]
    assert T // BLK <= FIRST_BLOCK_LANE, "one lane per key block below the lane of the first block"
    width = LANES * ATT_COLS
    chains = [(c, half) for c in range(ATT_COLS) for half in range(2)]

    def body(q_ref, k_ref, v_ref, m_ref, o_ref, cs_ref, a_ref, b_ref):
        qi = pl.program_id(1)
        lane = lax.broadcasted_iota(jnp.int32, (BLK, LANES), 1)
        first = lane < HEAD_DIM
        causal = (lax.broadcasted_iota(jnp.int32, (BLK, BLK), 1)
                  < lax.broadcasted_iota(jnp.int32, (BLK, BLK), 0))
        tri_m = m_ref[...]
        qh = {}
        for c in range(ATT_COLS):
            q = q_ref[:, c * LANES:(c + 1) * LANES]
            zero = jnp.zeros_like(q)
            qh[c, 0], qh[c, 1] = jnp.where(first, q, zero), jnp.where(first, zero, q)

        def step(kb, state, masked):
            carries, accs, cvals = state
            start = pl.multiple_of(kb * BLK, BLK)
            kblk = [k_ref[pl.ds(start, BLK), c * LANES:(c + 1) * LANES] for c in range(ATT_COLS)]
            vblk = [v_ref[pl.ds(start, BLK), c * LANES:(c + 1) * LANES] for c in range(ATT_COLS)]
            carries, accs, cvals = list(carries), list(accs), list(cvals)
            for g0 in range(0, len(chains), CHAIN_GROUP):
                ids = range(g0, g0 + CHAIN_GROUP)
                z = [_nt(qh[chains[n]], kblk[chains[n][0]]) for n in ids]
                parts = [_softplus_parts(zi) for zi in z]
                sp = [pt[1] for pt in parts]
                if masked:
                    sp = [jnp.where(causal, s, 0.0) for s in sp]
                incl = [_dot_hilo(s, tri_m) for s in sp]
                a = [jnp.exp(zi - ii - carries[n]) for n, zi, ii in zip(ids, z, incl)]
                if masked:
                    a = [jnp.where(causal, ai, 0.0) for ai in a]
                    for n, ai, pt in zip(ids, a, parts):
                        a_ref[:, n * BLK:(n + 1) * BLK] = ai.astype(BF16)
                        b_ref[:, n * BLK:(n + 1) * BLK] = pt[0].astype(BF16)
                for n, ai, ii in zip(ids, a, incl):
                    c, half = chains[n]
                    zero = jnp.zeros_like(vblk[c])
                    vh = jnp.where(first, vblk[c], zero) if half == 0 else jnp.where(first, zero, vblk[c])
                    accs[c] = accs[c] + _nn(ai.astype(BF16), vh)
                    cvals[c] = jnp.where(lane == kb + HEAD_DIM * half, carries[n], cvals[c])
                    carries[n] = carries[n] + ii[:, 0:1]
            return tuple(carries), tuple(accs), tuple(cvals)

        zeros = tuple(jnp.zeros((BLK, LANES), F32) for _ in range(ATT_COLS))
        state = (tuple(jnp.zeros((BLK, 1), F32) for _ in chains), zeros, zeros)
        state = step(qi, state, True)

        def reaches_further(st):
            it, (carries, _, _) = st
            least = functools.reduce(jnp.minimum, carries)
            return jnp.logical_and(it < qi, jnp.min(least) < DEAD_AT)

        done, state = lax.while_loop(reaches_further, lambda st: (st[0] + 1, step(qi - 1 - st[0], st[1], False)),
                                     (jnp.int32(0), state))
        first_block = (qi - done).astype(F32)
        for c in range(ATT_COLS):
            o_ref[:, c * LANES:(c + 1) * LANES] = state[1][c]
            cs_ref[:, c * LANES:(c + 1) * LANES] = jnp.where(lane == FIRST_BLOCK_LANE, first_block, state[2][c])

    blk = pl.BlockSpec((BLK, width), lambda j, i: (i, j))
    col = pl.BlockSpec((T, width), lambda j, i: (0, j))
    out = jax.ShapeDtypeStruct((T, ATTN_DIM), F32)
    diag = pl.BlockSpec((BLK, len(chains) * BLK), lambda j, i: (i, j))
    kept = jax.ShapeDtypeStruct((T, N_HEADS * BLK), BF16)
    return _call(
        body, name=name, grid=(ATTN_DIM // width, T // BLK),
        in_specs=[blk, col, col, _whole((BLK, BLK))],
        out_specs=[blk, blk, diag, diag], out_shape=[out, out, kept, kept],
        args=(qs, k, v, tri,), ride=ride)


def _shifted_copies(pad_ref, sh_ref):
    rows = sh_ref.shape[1]
    for b in range(SUBLANES):
        sh_ref[b] = pad_ref[b:b + rows, :]


def _shift_of(offset):
    return offset % SUBLANES, offset - offset % SUBLANES


def _conv_fwd(ug, dw_w, dw_b, ln_g, ln_b, name, ride=None):
    T = ug.shape[0]
    per = TM // HALO

    def body(cv_ref, cg_ref, cvh_ref, cgh_ref, w_ref, b_ref, g_ref, beta_ref, conv_ref, c2_ref, pad_ref, sh_ref):
        i = pl.program_id(0)
        halo = cvh_ref[...].astype(F32) * _sigmoid(cgh_ref[...].astype(F32))
        pad_ref[0:HALO, :] = jnp.where(i == 0, 0.0, halo)
        pad_ref[HALO:HALO + TM, :] = cv_ref[...].astype(F32) * _sigmoid(cg_ref[...].astype(F32))
        pad_ref[HALO + TM:, :] = jnp.zeros((SUBLANES, CONV_DIM), F32)
        _shifted_copies(pad_ref, sh_ref)
        taps = [w_ref[t:t + 1, :] for t in range(CONV_WIDTH)]

        def rows(j, _):
            r = pl.multiple_of(j * CONV_ROWS, CONV_ROWS)
            acc = jnp.zeros((CONV_ROWS, CONV_DIM), F32) + b_ref[...]
            for t in range(CONV_WIDTH):
                b, a = _shift_of(HALO - (CONV_WIDTH - 1) + t)
                acc = acc + taps[t] * sh_ref[b, pl.ds(r + a, CONV_ROWS), :]
            conv_ref[pl.ds(r, CONV_ROWS), :] = acc
            return 0

        lax.fori_loop(0, TM // CONV_ROWS, rows, 0)
        acc = conv_ref[...]
        mu = jnp.mean(acc, axis=-1, keepdims=True)
        xc = acc - mu
        rs = lax.rsqrt(jnp.mean(xc * xc, axis=-1, keepdims=True) + EPS)
        ln = xc * rs * g_ref[...] + beta_ref[...]
        c2_ref[...] = (ln * _sigmoid(ln)).astype(BF16)

    prev = lambda col: pl.BlockSpec((HALO, CHUNK), lambda i: (jnp.maximum(i * per - 1, 0), col))
    vec = _whole((1, CONV_DIM))
    return _call(
        body, name=name, grid=(T // TM,),
        in_specs=[_rows(TM, CHUNK, 1), _rows(TM, CHUNK, 2), prev(1), prev(2),
                  _whole((CONV_WIDTH, CONV_DIM)), vec, vec, vec],
        out_specs=[_rows(TM, CONV_DIM), _rows(TM, CONV_DIM)],
        out_shape=[jax.ShapeDtypeStruct((T, CONV_DIM), F32), jax.ShapeDtypeStruct((T, CONV_DIM), BF16)],
        scratch_shapes=[pltpu.VMEM((TM + HALO + SUBLANES, CONV_DIM), F32),
                        pltpu.VMEM((SUBLANES, TM + HALO, CONV_DIM), F32)],
        args=(ug, ug, ug, ug, dw_w, dw_b, ln_g, ln_b,), ride=ride)


def _mix_out_ple(o, ug, c2, h, p, head_mean, g_attn, g_conv, g_ple, w_pw, w_out, w_gate, w_ple, name, ride=None):
    T = h.shape[0]

    def body(o_ref, ga_ref, gc_ref, c2_ref, h_ref, p_ref, hm_ref, gao_ref, gco_ref, gpn_ref,
             wpw_ref, wout_ref, wg_ref, wple_ref,
             h2_ref, h1_ref, ycat_ref, hn2_ref, gate_ref, e_ref, c3_ref):
        ov = o_ref[...]
        rh = lax.rsqrt(_nn((ov * ov).astype(BF16), hm_ref[...]) + EPS)
        ga = ga_ref[...].astype(F32)
        ya = (ov * rh * gao_ref[...] * (ga * _sigmoid(ga))).astype(BF16)
        c3 = _nn(c2_ref[...], wpw_ref[...])
        c3_ref[...] = c3
        rc = lax.rsqrt(jnp.mean(c3 * c3, axis=-1, keepdims=True) + EPS)
        gc = gc_ref[...].astype(F32)
        yc = (c3 * rc * gco_ref[...] * (gc * _sigmoid(gc))).astype(BF16)
        ycat_ref[:, :ATTN_DIM] = ya
        ycat_ref[:, ATTN_DIM:] = yc
        h1 = h_ref[...] + _nn(ya, wout_ref[:ATTN_DIM, :]) + _nn(yc, wout_ref[ATTN_DIM:, :])
        h1_ref[...] = h1
        r1 = lax.rsqrt(jnp.mean(h1 * h1, axis=-1, keepdims=True) + EPS)
        hn2 = (h1 * r1 * gpn_ref[...]).astype(BF16)
        hn2_ref[...] = hn2
        gate = _sigmoid(_nn(hn2, wg_ref[...]))
        e = _nn(p_ref[...].astype(BF16), wple_ref[...])
        gate_ref[...] = gate
        e_ref[...] = e
        h2_ref[...] = h1 + e * gate

    f32 = lambda cols: jax.ShapeDtypeStruct((T, cols), F32)
    bf = lambda cols: jax.ShapeDtypeStruct((T, cols), BF16)
    return _call(
        body, name=name, grid=(T // TM,),
        in_specs=[_rows(TM, ATTN_DIM), _rows(TM, CHUNK, 0), _rows(TM, CHUNK, 3), _rows(TM, CONV_DIM),
                  _rows(TM, D_MODEL), _rows(TM, PLE_DIM), _whole((ATTN_DIM, ATTN_DIM)),
                  _whole((1, ATTN_DIM)), _whole((1, CONV_DIM)), _whole((1, D_MODEL)),
                  _whole((CONV_DIM, CONV_DIM)), _whole((D_MODEL, D_MODEL)), _whole((D_MODEL, D_MODEL)),
                  _whole((PLE_DIM, D_MODEL))],
        out_specs=[_rows(TM, D_MODEL), _rows(TM, D_MODEL), _rows(TM, D_MODEL), _rows(TM, D_MODEL),
                   _rows(TM, D_MODEL), _rows(TM, D_MODEL), _rows(TM, CONV_DIM)],
        out_shape=[f32(D_MODEL), f32(D_MODEL), bf(D_MODEL), bf(D_MODEL), f32(D_MODEL), f32(D_MODEL),
                   f32(CONV_DIM)],
        args=(o, ug, ug, c2, h, p, head_mean, g_attn, g_conv, g_ple, w_pw, w_out, w_gate, w_ple,), ride=ride)


def _final_loss(h, target, gain, name):
    T = h.shape[0]

    def body(h_ref, t_ref, g_ref, dh_ref, gsum_ref, loss_ref):
        @pl.when(pl.program_id(0) == 0)
        def _():
            gsum_ref[...] = jnp.zeros_like(gsum_ref)
            loss_ref[...] = jnp.zeros_like(loss_ref)

        hv = h_ref[...]
        r = lax.rsqrt(jnp.mean(hv * hv, axis=-1, keepdims=True) + EPS)
        xh = hv * r
        diff = xh * g_ref[...] - t_ref[...]
        loss_ref[...] += 0.5 * jnp.sum(jnp.mean(diff * diff, axis=-1, keepdims=True), axis=0, keepdims=True)
        dy = diff * (1.0 / D_MODEL)
        gsum_ref[...] += jnp.sum(dy * xh, axis=0, keepdims=True)
        dxh = dy * g_ref[...]
        dh_ref[...] = r * (dxh - xh * jnp.mean(dxh * xh, axis=-1, keepdims=True))

    return pl.pallas_call(
        body, name=name, grid=(T // TM,),
        in_specs=[_rows(TM, D_MODEL), _rows(TM, D_MODEL), _whole((1, D_MODEL))],
        out_specs=[_rows(TM, D_MODEL), _whole((1, D_MODEL)), _whole((1, LANES))],
        out_shape=[jax.ShapeDtypeStruct((T, D_MODEL), F32), jax.ShapeDtypeStruct((1, D_MODEL), F32),
                   jax.ShapeDtypeStruct((1, LANES), F32)],
        compiler_params=_params(1),
    )(h, target, gain)


def _ple_out_bwd(dh2, gate, e, h1, g_ple, w_gate, w_out, name, ride=None):
    T = dh2.shape[0]

    def body(dh2_ref, gate_ref, e_ref, h1_ref, gpn_ref, wg_ref, wout_ref,
             dh1_ref, dh1b_ref, dzg_ref, de_ref, dycat_ref, gsum_ref):
        @pl.when(pl.program_id(0) == 0)
        def _():
            gsum_ref[...] = jnp.zeros_like(gsum_ref)

        dh2v = dh2_ref[...]
        gate = gate_ref[...]
        de_ref[...] = (dh2v * gate).astype(BF16)
        dzg = (dh2v * e_ref[...] * gate * (1.0 - gate)).astype(BF16)
        dzg_ref[...] = dzg
        dhn2 = _nt(dzg, wg_ref[...])
        h1 = h1_ref[...]
        r1 = lax.rsqrt(jnp.mean(h1 * h1, axis=-1, keepdims=True) + EPS)
        xh = h1 * r1
        gsum_ref[...] += jnp.sum(dhn2 * xh, axis=0, keepdims=True)
        dxh = dhn2 * gpn_ref[...]
        dh1 = dh2v + r1 * (dxh - xh * jnp.mean(dxh * xh, axis=-1, keepdims=True))
        dh1_ref[...] = dh1
        dh1b = dh1.astype(BF16)
        dh1b_ref[...] = dh1b
        dycat_ref[...] = _nt(dh1b, wout_ref[...])

    f32 = jax.ShapeDtypeStruct((T, D_MODEL), F32)
    bf = jax.ShapeDtypeStruct((T, D_MODEL), BF16)
    full = _rows(TM, D_MODEL)
    return _call(
        body, name=name, grid=(T // TM,),
        in_specs=[full, full, full, full, _whole((1, D_MODEL)), _whole((D_MODEL, D_MODEL)),
                  _whole((D_MODEL, D_MODEL))],
        out_specs=[full, full, full, full, full, _whole((1, D_MODEL))],
        out_shape=[f32, bf, bf, bf, f32, jax.ShapeDtypeStruct((1, D_MODEL), F32)],
        args=(dh2, gate, e, h1, g_ple, w_gate, w_out,), ride=ride)


def _branch_bwd(dycat, o, ug, c3, conv, head_mean, g_attn, g_conv, ln_g, ln_b, w_pw, name, ride=None):
    T = o.shape[0]

    def body(dya_ref, dyc_ref, o_ref, ga_ref, gc_ref, c3_ref, conv_ref, hm_ref, gao_ref, gco_ref,
             lng_ref, lnb_ref, wpw_ref,
             do_ref, dga_ref, dgc_ref, dc3_ref, dconv_ref, sums_ref):
        @pl.when(pl.program_id(0) == 0)
        def _():
            sums_ref[...] = jnp.zeros_like(sums_ref)

        hm = hm_ref[...]
        col = lambda x: jnp.sum(x, axis=0, keepdims=True)
        ov = o_ref[...]
        rh = lax.rsqrt(_nn((ov * ov).astype(BF16), hm) + EPS)
        xh = ov * rh
        ga = ga_ref[...].astype(F32)
        sg = _sigmoid(ga)
        dya = dya_ref[...]
        don = dya * (ga * sg)
        dga_ref[...] = (dya * xh * gao_ref[...] * _dsilu(ga, sg)).astype(BF16)
        sums_ref[0:1, :] += col(don * xh)
        dxh = don * gao_ref[...]
        do_ref[...] = (rh * (dxh - xh * _dot_hilo(dxh * xh, hm))).astype(BF16)
        c3 = c3_ref[...]
        rc = lax.rsqrt(jnp.mean(c3 * c3, axis=-1, keepdims=True) + EPS)
        xh3 = c3 * rc
        gc = gc_ref[...].astype(F32)
        sgc = _sigmoid(gc)
        dyc = dyc_ref[...]
        dn3 = dyc * (gc * sgc)
        dgc_ref[...] = (dyc * xh3 * gco_ref[...] * _dsilu(gc, sgc)).astype(BF16)
        sums_ref[1:2, :] += col(dn3 * xh3)
        dxh3 = dn3 * gco_ref[...]
        dc3 = (rc * (dxh3 - xh3 * jnp.mean(dxh3 * xh3, axis=-1, keepdims=True))).astype(BF16)
        dc3_ref[...] = dc3
        dc2 = _nt(dc3, wpw_ref[...])
        cv = conv_ref[...]
        mu = jnp.mean(cv, axis=-1, keepdims=True)
        xc = cv - mu
        rs = lax.rsqrt(jnp.mean(xc * xc, axis=-1, keepdims=True) + EPS)
        xn = xc * rs
        ln = xn * lng_ref[...] + lnb_ref[...]
        dln = dc2 * _dsilu(ln, _sigmoid(ln))
        sums_ref[2:3, :] += col(dln * xn)
        sums_ref[3:4, :] += col(dln)
        dxn = dln * lng_ref[...]
        dconv = rs * (dxn - jnp.mean(dxn, axis=-1, keepdims=True)
                      - xn * jnp.mean(dxn * xn, axis=-1, keepdims=True))
        dconv_ref[...] = dconv
        sums_ref[4:5, :] += col(dconv)

    half = lambda dt: jax.ShapeDtypeStruct((T, CHUNK), dt)
    tile = _rows(TM, CHUNK)
    vec = _whole((1, CHUNK))
    return _call(
        body, name=name, grid=(T // TM,),
        in_specs=[_rows(TM, CHUNK, 0), _rows(TM, CHUNK, 1), tile, _rows(TM, CHUNK, 0), _rows(TM, CHUNK, 3),
                  tile, tile, _whole((ATTN_DIM, ATTN_DIM)), vec, vec, vec, vec, _whole((CONV_DIM, CONV_DIM))],
        out_specs=[tile, tile, tile, tile, tile, _whole((8, CHUNK))],
        out_shape=[half(BF16), half(BF16), half(BF16), half(BF16), half(F32),
                   jax.ShapeDtypeStruct((8, CHUNK), F32)],
        args=(dycat, dycat, o, ug, ug, c3, conv, head_mean, g_attn, g_conv, ln_g, ln_b, w_pw,), ride=ride)


def _conv_bwd(dconv, ug, dw_w, name, ride=None):
    T = dconv.shape[0]
    per = TM // HALO
    last = T // HALO - 1
    n_tiles = T // TM

    def body(d_ref, dn_ref, cv_ref, cg_ref, cvh_ref, cgh_ref, w_ref, dcv_ref, dcg_ref, dw_ref,
             dpad_ref, cpad_ref, dsh_ref, csh_ref, dw_acc):
        i = pl.program_id(0)

        @pl.when(i == 0)
        def _():
            dw_acc[...] = jnp.zeros_like(dw_acc)

        tail = jnp.zeros((SUBLANES, CONV_DIM), F32)
        dpad_ref[0:TM, :] = d_ref[...]
        dpad_ref[TM:TM + HALO, :] = jnp.where(i == n_tiles - 1, 0.0, dn_ref[...])
        dpad_ref[TM + HALO:, :] = tail
        halo = cvh_ref[...].astype(F32) * _sigmoid(cgh_ref[...].astype(F32))
        cpad_ref[0:HALO, :] = jnp.where(i == 0, 0.0, halo)
        cpad_ref[HALO:HALO + TM, :] = cv_ref[...].astype(F32) * _sigmoid(cg_ref[...].astype(F32))
        cpad_ref[HALO + TM:, :] = tail
        _shifted_copies(dpad_ref, dsh_ref)
        _shifted_copies(cpad_ref, csh_ref)
        taps = [w_ref[t:t + 1, :] for t in range(CONV_WIDTH)]

        def rows(j, _):
            r = pl.multiple_of(j * CONV_ROWS, CONV_ROWS)
            d = d_ref[pl.ds(r, CONV_ROWS), :]
            dc = jnp.zeros((CONV_ROWS, CONV_DIM), F32)
            for t in range(CONV_WIDTH):
                b, a = _shift_of(CONV_WIDTH - 1 - t)
                dc = dc + taps[t] * dsh_ref[b, pl.ds(r + a, CONV_ROWS), :]
                b, a = _shift_of(HALO - (CONV_WIDTH - 1) + t)
                prod = d * csh_ref[b, pl.ds(r + a, CONV_ROWS), :]
                dw_acc[t] += jnp.sum(prod.reshape(CONV_ROWS // SUBLANES, SUBLANES, CONV_DIM), axis=0)
            cv = cv_ref[pl.ds(r, CONV_ROWS), :].astype(F32)
            sg = _sigmoid(cg_ref[pl.ds(r, CONV_ROWS), :].astype(F32))
            dcv_ref[pl.ds(r, CONV_ROWS), :] = (dc * sg).astype(BF16)
            dcg_ref[pl.ds(r, CONV_ROWS), :] = (dc * cv * sg * (1.0 - sg)).astype(BF16)
            return 0

        lax.fori_loop(0, TM // CONV_ROWS, rows, 0)

        @pl.when(i == n_tiles - 1)
        def _():
            dw_ref[...] = jnp.zeros_like(dw_ref)
            for t in range(CONV_WIDTH):
                dw_ref[t:t + 1, :] = jnp.sum(dw_acc[t], axis=0, keepdims=True)

    prev = lambda col: pl.BlockSpec((HALO, CHUNK), lambda i: (jnp.maximum(i * per - 1, 0), col))
    nxt = pl.BlockSpec((HALO, CONV_DIM), lambda i: (jnp.minimum((i + 1) * per, last), 0))
    half = jax.ShapeDtypeStruct((T, CHUNK), BF16)
    return _call(
        body, name=name, grid=(T // TM,),
        in_specs=[_rows(TM, CONV_DIM), nxt, _rows(TM, CHUNK, 1), _rows(TM, CHUNK, 2), prev(1), prev(2),
                  _whole((CONV_WIDTH, CONV_DIM))],
        out_specs=[_rows(TM, CHUNK), _rows(TM, CHUNK), _whole((HALO, CONV_DIM))],
        out_shape=[half, half, jax.ShapeDtypeStruct((HALO, CONV_DIM), F32)],
        scratch_shapes=[pltpu.VMEM((TM + HALO + SUBLANES, CONV_DIM), F32),
                        pltpu.VMEM((TM + HALO + SUBLANES, CONV_DIM), F32),
                        pltpu.VMEM((SUBLANES, TM + HALO, CONV_DIM), F32),
                        pltpu.VMEM((SUBLANES, TM + HALO, CONV_DIM), F32),
                        pltpu.VMEM((HALO, SUBLANES, CONV_DIM), F32)],
        args=(dconv, dconv, ug, ug, ug, ug, dw_w,), ride=ride)


def _attn_bwd(qs, k, v, do, cs, a_diag, b_diag, tri, tri_t, name, ride=None):
    T = qs.shape[0]
    nq = T // BLK
    width = LANES * ATT_COLS
    chains = [(c, half) for c in range(ATT_COLS) for half in range(2)]

    def body(q_ref, k_ref, v_ref, do_ref, cs_ref, ad_ref, bd_ref, m_ref, mt_ref, dq_ref, dk_ref, dv_ref, dk_acc, dv_acc):
        qi = pl.program_id(1)

        @pl.when(qi == 0)
        def _():
            dk_acc[...] = jnp.zeros_like(dk_acc)
            dv_acc[...] = jnp.zeros_like(dv_acc)

        lane = lax.broadcasted_iota(jnp.int32, (BLK, LANES), 1)
        first = lane < HEAD_DIM
        causal = (lax.broadcasted_iota(jnp.int32, (BLK, BLK), 1)
                  < lax.broadcasted_iota(jnp.int32, (BLK, BLK), 0))
        tri_m = m_ref[...]
        tri_mt = mt_ref[...]

        def halves(x):
            zero = jnp.zeros_like(x)
            return jnp.where(first, x, zero), jnp.where(first, zero, x)

        qh, doh, cs = {}, {}, []
        for c in range(ATT_COLS):
            qh[c, 0], qh[c, 1] = halves(q_ref[:, c * LANES:(c + 1) * LANES])
            doh[c, 0], doh[c, 1] = halves(do_ref[:, c * LANES:(c + 1) * LANES])
            cs.append(cs_ref[:, c * LANES:(c + 1) * LANES])

        def step(kb, state, masked):
            prefixes, dq_accs = state
            start = pl.multiple_of(kb * BLK, BLK)
            kblk = [k_ref[pl.ds(start, BLK), c * LANES:(c + 1) * LANES] for c in range(ATT_COLS)]
            vblk = [v_ref[pl.ds(start, BLK), c * LANES:(c + 1) * LANES] for c in range(ATT_COLS)]
            prefixes, dq_accs = list(prefixes), list(dq_accs)
            for g0 in range(0, len(chains), CHAIN_GROUP):
                ids = range(g0, g0 + CHAIN_GROUP)
                grp = [chains[n] for n in ids]
                da = [_nt(doh[ch], vblk[ch[0]]) for ch in grp]
                if masked:
                    a = [ad_ref[:, n * BLK:(n + 1) * BLK].astype(F32) for n in ids]
                    beta = [bd_ref[:, n * BLK:(n + 1) * BLK].astype(F32) for n in ids]
                else:
                    z = [_nt(qh[ch], kblk[ch[0]]) for ch in grp]
                    parts = [_softplus_parts(zi) for zi in z]
                    beta = [pt[0] for pt in parts]
                    incl = [_dot_hilo(pt[1], tri_m) for pt in parts]
                    carries = [jnp.sum(jnp.where(lane == kb + HEAD_DIM * half, cs[c], 0.0), axis=1, keepdims=True)
                               for c, half in grp]
                    a = [jnp.exp(zi - ii - ci) for zi, ii, ci in zip(z, incl, carries)]
                w = [ai * di for ai, di in zip(a, da)]
                pinc = [_nn(wi.astype(BF16), tri_mt) for wi in w]
                dz = [wi - bi * (pi + prefixes[n]) for n, wi, bi, pi in zip(ids, w, beta, pinc)]
                if masked:
                    dz = [jnp.where(causal, d, 0.0) for d in dz]
                for j in range(0, CHAIN_GROUP, 2):
                    c = grp[j][0]
                    k0, k1 = halves(kblk[c])
                    dz0, dz1 = dz[j].astype(BF16), dz[j + 1].astype(BF16)
                    a0, a1 = a[j].astype(BF16), a[j + 1].astype(BF16)
                    dq_accs[c] = dq_accs[c] + _nn(dz0, k0) + _nn(dz1, k1)
                    dk_acc[pl.ds(start, BLK), c * LANES:(c + 1) * LANES] += _tn(dz0, qh[c, 0]) + _tn(dz1, qh[c, 1])
                    dv_acc[pl.ds(start, BLK), c * LANES:(c + 1) * LANES] += _tn(a0, doh[c, 0]) + _tn(a1, doh[c, 1])
                for n, pi in zip(ids, pinc):
                    prefixes[n] = prefixes[n] + pi[:, BLK - 1:BLK]
            return tuple(prefixes), tuple(dq_accs)

        state = (tuple(jnp.zeros((BLK, 1), F32) for _ in chains),
                 tuple(jnp.zeros((BLK, LANES), F32) for _ in range(ATT_COLS)))
        first_block = jnp.max(jnp.where(lane == FIRST_BLOCK_LANE, cs[0], 0.0)).astype(jnp.int32)
        state = lax.fori_loop(first_block, qi, lambda kb, st: step(kb, st, False), state)
        state = step(qi, state, True)
        for c in range(ATT_COLS):
            dq_ref[:, c * LANES:(c + 1) * LANES] = (state[1][c] * (HEAD_DIM ** -0.5)).astype(BF16)

        @pl.when(qi == nq - 1)
        def _():
            dk_ref[...] = dk_acc[...].astype(BF16)
            dv_ref[...] = dv_acc[...].astype(BF16)

    blk = pl.BlockSpec((BLK, width), lambda j, i: (i, j))
    col = pl.BlockSpec((T, width), lambda j, i: (0, j))
    out = jax.ShapeDtypeStruct((T, ATTN_DIM), BF16)
    diag = pl.BlockSpec((BLK, len(chains) * BLK), lambda j, i: (i, j))
    return _call(
        body, name=name, grid=(ATTN_DIM // width, nq),
        in_specs=[blk, col, col, blk, blk, diag, diag, _whole((BLK, BLK)), _whole((BLK, BLK))],
        out_specs=[blk, col, col], out_shape=[out, out, out],
        scratch_shapes=[pltpu.VMEM((T, width), F32), pltpu.VMEM((T, width), F32)],
        args=(qs, k, v, do, cs, a_diag, b_diag, tri, tri_t,), ride=ride)


def _inproj_bwd(du, w_in_t, h, dh1, gain, name, ride=None):
    T = h.shape[0]

    def body(*refs):
        du_refs = refs[:N_CHUNK]
        w_ref, h_ref, dh1_ref, g_ref, dh_ref, gsum_ref = refs[N_CHUNK:]

        @pl.when(pl.program_id(0) == 0)
        def _():
            gsum_ref[...] = jnp.zeros_like(gsum_ref)

        dhn = jnp.zeros((TM, D_MODEL), F32)
        for j in range(N_CHUNK):
            dhn = dhn + _nn(du_refs[j][...], w_ref[j * CHUNK:(j + 1) * CHUNK, :])
        hv = h_ref[...]
        r = lax.rsqrt(jnp.mean(hv * hv, axis=-1, keepdims=True) + EPS)
        xh = hv * r
        gsum_ref[...] += jnp.sum(dhn * xh, axis=0, keepdims=True)
        dxh = dhn * g_ref[...]
        dh_ref[...] = dh1_ref[...] + r * (dxh - xh * jnp.mean(dxh * xh, axis=-1, keepdims=True))

    full = _rows(TM, D_MODEL)
    return _call(
        body, name=name, grid=(T // TM,),
        in_specs=[_rows(TM, CHUNK)] * N_CHUNK + [_whole((N_CHUNK * CHUNK, D_MODEL)), full, full,
                                                 _whole((1, D_MODEL))],
        out_specs=[full, _whole((1, D_MODEL))],
        out_shape=[jax.ShapeDtypeStruct((T, D_MODEL), F32), jax.ShapeDtypeStruct((1, D_MODEL), F32)],
        args=(*du, w_in_t, h, dh1, gain), ride=ride)


def _weight_grad(lhs_list, rhs, name, tk=CHUNK, ride=None):
    T, n_rhs = rhs.shape
    n = len(lhs_list)
    ka = lhs_list[0].shape[1]
    per = ka // tk

    def body(*refs):
        a_refs, b_ref, out_ref = refs[:n], refs[n], refs[n + 1]
        step = pl.program_id(0)
        for j in range(n):
            for s in range(per):
                @pl.when(step == j * per + s)
                def _(j=j, s=s):
                    out_ref[...] = _tn(a_refs[j][:, s * tk:(s + 1) * tk], b_ref[...]).astype(BF16)

    (grad,), landed = _call(
        body, name=name, grid=(n * per,),
        in_specs=[_whole((T, ka))] * n + [_whole((T, n_rhs))],
        out_specs=[pl.BlockSpec((tk, n_rhs), lambda i: (i, 0))],
        out_shape=[jax.ShapeDtypeStruct((n * ka, n_rhs), BF16)],
        args=(*lhs_list, rhs), ride=ride)
    return grad, landed


def _adamw_update(w, g, m, v):
    nm = ADAM_B1 * m + (1.0 - ADAM_B1) * g
    nv = ADAM_B2 * v + (1.0 - ADAM_B2) * (g * g)
    m_hat = nm / (1.0 - ADAM_B1 ** ADAM_STEP)
    v_hat = nv / (1.0 - ADAM_B2 ** ADAM_STEP)
    return -ADAM_LR * (m_hat / (jnp.sqrt(v_hat) + ADAM_EPS) + ADAM_WD * w), nm, nv


def _sum_adamw(slots, w, m, v, name):
    depth, R, C = w.shape
    tr = next(rows for rows in ADAMW_ROWS if R % rows == 0)

    def body(*refs):
        slot_refs, (w_ref, m_ref, v_ref, g_ref, d_ref, nm_ref, nv_ref) = refs[:depth], refs[depth:]
        for layer in range(depth):
            @pl.when(pl.program_id(0) == layer)
            def _(src=slot_refs[layer]):
                g = src[0].astype(F32)
                for s in range(1, src.shape[0]):
                    g = g + src[s].astype(F32)
                g_ref[0] = g
                d_ref[0], nm_ref[0], nv_ref[0] = _adamw_update(w_ref[0], g, m_ref[0], v_ref[0])

    slot_spec = lambda layer: pl.BlockSpec((slots[layer].shape[0], tr, C),
                                           lambda l, i: (0, jnp.where(l == layer, i, 0), 0))
    spec = pl.BlockSpec((1, tr, C), lambda l, i: (l, i, 0))
    out = jax.ShapeDtypeStruct((depth, R, C), F32)
    return pl.pallas_call(
        body, name=name, grid=(depth, R // tr),
        in_specs=[slot_spec(layer) for layer in range(depth)] + [spec] * 3,
        out_specs=[spec] * 4, out_shape=[out] * 4,
        compiler_params=_params(2),
    )(*slots, w, m, v)


def _adamw(w, g, m, v, name):
    R, C = w.shape
    tr = R
    for cand in (512, 256, 128, 64):
        if R % cand == 0 and R > cand:
            tr = cand
            break

    def body(w_ref, g_ref, m_ref, v_ref, d_ref, nm_ref, nv_ref):
        d_ref[...], nm_ref[...], nv_ref[...] = _adamw_update(w_ref[...], g_ref[...], m_ref[...], v_ref[...])

    spec = pl.BlockSpec((tr, C), lambda i: (i, 0))
    out = jax.ShapeDtypeStruct((R, C), F32)
    return pl.pallas_call(
        body, name=name, grid=(R // tr,),
        in_specs=[spec] * 4, out_specs=[spec] * 3, out_shape=[out, out, out],
        compiler_params=_params(1),
    )(w, g, m, v)


def _pack_small(values, scalar=None):
    pad = lambda a: jnp.pad(a, ((0, 0), (0, D_MODEL - a.shape[1])))
    last = jnp.zeros((1, D_MODEL), F32) if scalar is None else pad(scalar.reshape(1, 1))
    return jnp.concatenate([pad(values[name].reshape(rows, cols)) for name, _, rows, cols in SMALL_LAYOUT] + [last],
                           axis=0)


def _small_update(all_packs, state, name):
    n = len(SMALL_LAYOUT)

    def body(packs_ref, *refs):
        ins, outs = refs[:3 * n], refs[3 * n:]
        total = packs_ref[0]
        for s in range(1, N_DEV):
            total = total + packs_ref[s]
        for j, (_, at, rows, cols) in enumerate(SMALL_LAYOUT):
            g = total[at:at + rows, :cols]
            w_ref, m_ref, v_ref = ins[3 * j:3 * j + 3]
            outs[4 * j][...] = g
            outs[4 * j + 1][...], outs[4 * j + 2][...], outs[4 * j + 3][...] = _adamw_update(
                w_ref[...], g, m_ref[...], v_ref[...])
        outs[-2][...] = total[LOSS_ROW:LOSS_ROW + 1, :LANES]
        outs[-1][...] = total[SMALL_ROWS:, :]

    shapes = [jax.ShapeDtypeStruct((rows, cols), F32) for _, _, rows, cols in SMALL_LAYOUT for _ in range(4)]
    shapes += [jax.ShapeDtypeStruct((1, LANES), F32), jax.ShapeDtypeStruct((PACK_ROWS - SMALL_ROWS, D_MODEL), F32)]
    operands = [a for item in SMALL_LAYOUT for a in state[item[0]]]
    res = pl.pallas_call(body, name=name, out_shape=shapes, compiler_params=_params())(all_packs, *operands)
    per_name = {item[0]: tuple(res[4 * j:4 * j + 4]) for j, item in enumerate(SMALL_LAYOUT)}
    return per_name, res[-2][0, 0], res[-1]


def kernel(x, p, norm_g, w_in, attn_out_g, dw_w, dw_b, conv_ln_g, conv_ln_b, w_pw, conv_out_g, w_out, ple_norm_g, w_ple_gate, w_ple, final_g, loss_target, m_norm_g, m_w_in, m_attn_out_g, m_dw_w, m_dw_b, m_conv_ln_g, m_conv_ln_b, m_w_pw, m_conv_out_g, m_w_out, m_ple_norm_g, m_w_ple_gate, m_w_ple, m_final_g, v_norm_g, v_w_in, v_attn_out_g, v_dw_w, v_dw_b, v_conv_ln_g, v_conv_ln_b, v_w_pw, v_conv_out_g, v_w_out, v_ple_norm_g, v_w_ple_gate, v_w_ple, v_final_g):
    depth = w_in.shape[0]
    T = x.shape[1]
    given = dict(
        norm_g=norm_g, ple_norm_g=ple_norm_g, final_g=final_g, dw_b=dw_b, conv_ln_g=conv_ln_g, conv_ln_b=conv_ln_b,
        conv_out_g=conv_out_g, attn_out_g=attn_out_g,
        m_norm_g=m_norm_g, m_ple_norm_g=m_ple_norm_g, m_final_g=m_final_g, m_dw_b=m_dw_b, m_conv_ln_g=m_conv_ln_g,
        m_conv_ln_b=m_conv_ln_b, m_conv_out_g=m_conv_out_g, m_attn_out_g=m_attn_out_g,
        v_norm_g=v_norm_g, v_ple_norm_g=v_ple_norm_g, v_final_g=v_final_g, v_dw_b=v_dw_b, v_conv_ln_g=v_conv_ln_g,
        v_conv_ln_b=v_conv_ln_b, v_conv_out_g=v_conv_out_g, v_attn_out_g=v_attn_out_g)
    my_idx = 4 * lax.axis_index("x") + 2 * lax.axis_index("y") + lax.axis_index("c")

    ids = jnp.arange(BLK)
    tri = (ids[:, None] >= ids[None, :]).astype(BF16)
    tri_t = (ids[:, None] <= ids[None, :]).astype(BF16)
    hid = jnp.arange(ATTN_DIM) // HEAD_DIM
    head_mean = ((hid[:, None] == hid[None, :]).astype(F32) / HEAD_DIM).astype(BF16)

    w_names = ("w_in_t", "w_pw", "w_out", "w_gate", "w_ple")
    w_axes = dict(zip(w_names, (0, 0, 0, 0, 1)))
    shards = [dict(zip(w_names, (w_in[l].T.astype(BF16), w_pw[l].astype(BF16), w_out[l].astype(BF16),
                                 w_ple_gate[l].astype(BF16), w_ple[l].astype(BF16)))) for l in range(depth)]
    first = _all_gather([shards[0]["w_in_t"]] + [dw_w[l].T for l in range(depth)], [0] * (1 + depth),
                        "gather_weights_0")
    layers = []
    for l in range(depth):
        layers.append(dict(
            dw_w=first[1 + l].T,
            g_norm=norm_g[l][None], g_attn=jnp.tile(attn_out_g[l], N_HEADS)[None], dw_b=dw_b[l][None],
            ln_g=conv_ln_g[l][None], ln_b=conv_ln_b[l][None], g_conv=conv_out_g[l][None],
            g_ple=ple_norm_g[l][None], p=p[l, 0]))
    layers[0]["w_in_t"] = first[0]

    def rest_of(l, names):
        return [_Ride.gather2(shards[l][n], w_axes[n]) for n in names]

    h = x[0]
    saved = []
    for l, w in enumerate(layers):
        early, late = (w_names[3:], w_names[1:3]) if l == 0 else ((), ())
        (qs, k, v, ug, hn), landed = _prenorm_inproj(h, w["g_norm"], w["w_in_t"], f"inproj_{l}",
                                                     _Ride(rest_of(l, early)))
        w.update(zip(early, landed))
        ahead = [_Ride.gather2(shards[l + 1]["w_in_t"], 0, None, 0, W_IN_ROWS_ON_ATTN)] if l + 1 < depth else []
        own = w_names[1:] if l > 0 else ()
        (o, cs, a_diag, b_diag), landed = _attn_fwd(qs, k, v, tri, f"attn_fwd_{l}", _Ride(ahead + rest_of(l, own)))
        w_in_next = landed[:len(ahead)]
        w.update(zip(own, landed[len(ahead):]))
        (conv, c2), landed = _conv_fwd(ug, w["dw_w"], w["dw_b"], w["ln_g"], w["ln_b"], f"conv_fwd_{l}",
                                       _Ride(rest_of(l, late)))
        w.update(zip(late, landed))
        tail = [_Ride.gather2(shards[l + 1]["w_in_t"], 0, a, W_IN_ROWS_ON_ATTN,
                              shards[l + 1]["w_in_t"].shape[0] - W_IN_ROWS_ON_ATTN) for a in w_in_next]
        (h2, h1, ycat, hn2, gate, e, c3), landed = _mix_out_ple(
            o, ug, c2, h, w["p"], head_mean, w["g_attn"], w["g_conv"], w["g_ple"],
            w["w_pw"], w["w_out"], w["w_gate"], w["w_ple"], f"mix_{l}", _Ride(tail))
        if landed:
            layers[l + 1]["w_in_t"] = landed[0]
        saved.append(dict(h=h, qs=qs, k=k, v=v, ug=ug, hn=hn, o=o, cs=cs, a_diag=a_diag, b_diag=b_diag, conv=conv, c2=c2, h1=h1,
                          ycat=ycat, hn2=hn2, gate=gate, e=e, c3=c3))
        h = h2
    dh, g_final, loss_part = _final_loss(h, loss_target[0], final_g[None], "final_loss")

    small = {}
    dww_parts = [None] * depth
    slots = [dict() for _ in range(depth)]
    g_w_in = None
    pending = []
    for l in reversed(range(depth)):
        w, s = layers[l], saved[l]
        above = pending

        def part(i, above=above, g=g_w_in):
            return [_Ride.scatter(g, 0, above[0], *W_IN_GRAD_PARTS[i])] if above else []

        def scattered(grads, names):
            return [_Ride.scatter(grads[n], w_axes[n]) for n in names]

        (dh1, dh1b, dzg, de, dycat, g_ple_sum), landed = _ple_out_bwd(
            dh, s["gate"], s["e"], s["h1"], w["g_ple"], w["w_gate"], w["w_out"], f"ple_bwd_{l}", _Ride(part(1)))
        above[:1] = landed
        (do, dga, dgc, dc3, dconv, sums), landed = _branch_bwd(
            dycat, s["o"], s["ug"], s["c3"], s["conv"], head_mean, w["g_attn"], w["g_conv"],
            w["ln_g"], w["ln_b"], w["w_pw"], f"branch_bwd_{l}", _Ride(part(2)))
        above[:1] = landed
        grads = dict(
            w_pw=_weight_grad([s["c2"]], dc3, f"grad_w_pw_{l}")[0],
            w_out=_weight_grad([s["ycat"]], dh1b, f"grad_w_out_{l}")[0],
            w_gate=_weight_grad([s["hn2"]], dzg, f"grad_w_gate_{l}")[0],
            w_ple=_weight_grad([w["p"].astype(BF16)], de, f"grad_w_ple_{l}", tk=PLE_DIM)[0])
        (dcv, dcg, dww), landed = _conv_bwd(dconv, s["ug"], w["dw_w"], f"conv_bwd_{l}", _Ride(part(3)))
        if above:
            slots[l + 1]["w_in_t"] = landed[0]
        (dq, dk, dv), landed = _attn_bwd(s["qs"], s["k"], s["v"], do, s["cs"], s["a_diag"], s["b_diag"], tri, tri_t,
                                         f"attn_bwd_{l}",
                                         _Ride(scattered(grads, w_names[1:])))
        slots[l].update(zip(w_names[1:], landed))
        du = [dq, dk, dv, dga, dcv, dcg, dgc]
        g_w_in_here, _ = _weight_grad(du, s["hn"], f"grad_w_in_{l}")
        if l == 0:
            tail = [_Ride.scatter_chips(_pair_reduce(g_w_in_here, f"pair_reduce_w_in_{l}"))]
        else:
            tail = [_Ride.scatter(g_w_in_here, 0, None, *W_IN_GRAD_PARTS[0])]
        (dh, g_norm_sum), landed = _inproj_bwd(du, w["w_in_t"], s["h"], dh1, w["g_norm"], f"inproj_bwd_{l}",
                                               _Ride(tail))
        if l == 0:
            slots[l]["w_in_t"] = landed[0]
        pending = landed if l > 0 else []
        g_w_in = g_w_in_here
        small[l] = dict(norm_g=g_norm_sum, ple_norm_g=g_ple_sum, attn_out_g=sums[0].reshape(N_HEADS, HEAD_DIM).sum(0),
                        conv_out_g=sums[1], conv_ln_g=sums[2], conv_ln_b=sums[3], dw_b=sums[4])
        dww_parts[l] = dww[:CONV_WIDTH]
    slots = [[sl[n] for n in w_names] for sl in slots]
    grad_x = dh[None]

    sums_of = {name: jnp.stack([small[l][name].reshape(-1) for l in range(depth)]) for name in small[0]}
    sums_of["final_g"] = g_final
    pack = jnp.concatenate([_pack_small(sums_of, scalar=loss_part[0, 0]), jnp.concatenate(dww_parts, axis=1),
                            jnp.zeros((PACK_ROWS - SMALL_ROWS - CONV_WIDTH, D_MODEL), F32)], axis=0)
    (all_packs,) = _all_gather([pack], [0], "gather_small_grads")
    state = {name: [given[pre + name].reshape(rows, cols) for pre in ("", "m_", "v_")]
             for name, _, rows, cols in SMALL_LAYOUT}
    updated, loss, dww_sum = _small_update(all_packs.reshape(N_DEV, PACK_ROWS, D_MODEL), state, "update_small")
    res = {kind: {name: val[k].reshape(given[name].shape) for name, val in updated.items()}
           for k, kind in enumerate("gdmv")}
    dww_full = dww_sum[:CONV_WIDTH].reshape(CONV_WIDTH, depth, CONV_DIM).transpose(1, 0, 2)
    g_dw_w = lax.dynamic_slice_in_dim(dww_full, my_idx * (CONV_DIM // N_DEV), CONV_DIM // N_DEV, axis=2)

    swap = lambda a: a.transpose(0, 2, 1)
    state = {"w_in": (w_in, m_w_in, v_w_in), "w_pw": (w_pw, m_w_pw, v_w_pw), "w_out": (w_out, m_w_out, v_w_out),
             "w_ple_gate": (w_ple_gate, m_w_ple_gate, v_w_ple_gate), "w_ple": (w_ple, m_w_ple, v_w_ple)}
    for at, name in enumerate(state):
        wv, mv, vv = [swap(a) for a in state[name]] if name == "w_in" else state[name]
        out = _sum_adamw([slots[l][at] for l in range(depth)], wv, mv, vv, f"adamw_{name}")
        out = [swap(a) for a in out] if name == "w_in" else out
        res["g"][name], res["d"][name], res["m"][name], res["v"][name] = out
    flat = lambda a: a.reshape(-1, a.shape[-1])
    res["g"]["dw_w"] = g_dw_w
    res["d"]["dw_w"], res["m"]["dw_w"], res["v"]["dw_w"] = [
        a.reshape(dw_w.shape) for a in _adamw(flat(dw_w), flat(g_dw_w), flat(m_dw_w), flat(v_dw_w), "adamw_dw_w")]

    order = ["norm_g", "w_in", "attn_out_g", "dw_w", "dw_b", "conv_ln_g", "conv_ln_b", "w_pw", "conv_out_g",
             "w_out", "ple_norm_g", "w_ple_gate", "w_ple", "final_g"]
    return (loss, grad_x, *[res["g"][n] for n in order], *[res["d"][n] for n in order],
            *[res["m"][n] for n in order], *[res["v"][n] for n in order])
```

```python
import functools

import jax
import jax.numpy as jnp
from jax import lax
from jax.experimental import pallas as pl
from jax.experimental.pallas import tpu as pltpu

F32 = jnp.float32
BF16 = jnp.bfloat16
MESH = pl.DeviceIdType.MESH

N_DEV = 8
D_MODEL = 1024
ATTN_DIM = 512
CONV_DIM = 512
HEAD_DIM = 64
N_HEADS = 8
CONV_WIDTH = 31
PLE_DIM = 256
CHUNK = 512
N_CHUNK = 7
EPS = 1e-6
ADAM_LR = 0.001
ADAM_B1 = 0.9
ADAM_B2 = 0.999
ADAM_EPS = 1e-08
ADAM_WD = 0.01
ADAM_STEP = 10

LANES = 128
BLK = 256
ATT_COLS = 4
CHAIN_GROUP = 4
SOFTPLUS_LINEAR_AT = 20.0
DEAD_AT = 110.0
FIRST_BLOCK_LANE = HEAD_DIM - 1
TM = 512
HALO = 32
SUBLANES = 8
CONV_ROWS = 32
ADAMW_ROWS = (224, 128, 64)
VMEM_LIMIT = 56 * 1024 * 1024
SMALL_ROWS = 16
SMALL_LAYOUT = (("norm_g", 0, 2, D_MODEL), ("ple_norm_g", 2, 2, D_MODEL), ("final_g", 4, 1, D_MODEL),
                ("dw_b", 5, 2, CONV_DIM), ("conv_ln_g", 7, 2, CONV_DIM), ("conv_ln_b", 9, 2, CONV_DIM),
                ("conv_out_g", 11, 2, CONV_DIM), ("attn_out_g", 13, 2, HEAD_DIM))
LOSS_ROW = 15
W_IN_ROWS_ON_ATTN = 352
W_IN_GRAD_PARTS = ((0, 112), (112, 96), (208, 64), (272, 176))
PACK_ROWS = 48


def _nn(a, b):
    return lax.dot_general(a, b, (((1,), (0,)), ((), ())), preferred_element_type=F32)


def _nt(a, b):
    return lax.dot_general(a, b, (((1,), (1,)), ((), ())), preferred_element_type=F32)


def _tn(a, b):
    return lax.dot_general(a, b, (((0,), (0,)), ((), ())), preferred_element_type=F32)


def _split(x):
    hi = x.astype(BF16)
    lo = (x - hi.astype(F32)).astype(BF16)
    return hi, lo


def _dot_hilo(x, m):
    hi, lo = _split(x)
    return _nn(hi, m) + _nn(lo, m)


def _sigmoid(x):
    return jax.nn.sigmoid(x)


def _dsilu(x, s):
    return s * (1.0 + x * (1.0 - s))


def _params(n_grid=0, vmem=VMEM_LIMIT):
    sem = ("arbitrary",) * n_grid if n_grid else None
    return pltpu.CompilerParams(dimension_semantics=sem, vmem_limit_bytes=vmem)


def _rows(tm, cols, col=0):
    return pl.BlockSpec((tm, cols), lambda i: (i, col))


def _whole(shape):
    zeros = (0,) * len(shape)
    return pl.BlockSpec(shape, lambda *_: zeros)


def _my_position():
    return lax.axis_index("x"), lax.axis_index("y"), lax.axis_index("c")


def _block(ref, axis, idx, size):
    start = pl.multiple_of(idx * size, size)
    if axis == 0:
        return ref.at[pl.ds(start, size), :]
    return ref.at[:, pl.ds(start, size)]


def _all_gather(shards, axes, name):
    n = len(shards)
    sizes = [s.shape[a] for s, a in zip(shards, axes)]

    def full_shape(s, a):
        shape = list(s.shape)
        shape[a] *= N_DEV
        return jax.ShapeDtypeStruct(tuple(shape), s.dtype)

    def body(*refs):
        ins, outs = refs[:n], refs[n:2 * n]
        send_sems, recv_sems, local_sems = refs[2 * n:]
        x, y, c = _my_position()
        me, sibling = (x, y, c), (x, y, 1 - c)
        chips = [(1 - x, y), (x, 1 - y), (1 - x, 1 - y)]

        def place(i, dev):
            return _block(outs[i], axes[i], 4 * dev[0] + 2 * dev[1] + dev[2], sizes[i])

        def copy(k, i, dev, to, src=None):
            return pltpu.make_async_remote_copy(
                src_ref=place(i, dev) if src is None else src, dst_ref=place(i, dev),
                send_sem=send_sems.at[k, i], recv_sem=recv_sems.at[k, i],
                device_id=to, device_id_type=MESH)

        mine = [pltpu.make_async_copy(ins[i], place(i, me), local_sems.at[i]) for i in range(n)]
        for cp in mine:
            cp.start()
        first = [copy(0, i, me, sibling, src=ins[i]) for i in range(n)]
        for j, chip in enumerate(chips):
            first += [copy(1 + j, i, me, (*chip, c), src=ins[i]) for i in range(n)]
        for cp in first:
            cp.start()
        passed = []
        for j, chip in enumerate(chips):
            for i in range(n):
                copy(1 + j, i, (*chip, c), me).wait_recv()
            hop = [copy(4 + j, i, (*chip, c), sibling) for i in range(n)]
            for cp in hop:
                cp.start()
            passed += hop
        for i in range(n):
            copy(0, i, sibling, me).wait_recv()
        for j, chip in enumerate(chips):
            for i in range(n):
                copy(4 + j, i, (*chip, 1 - c), me).wait_recv()
        for cp in first + passed:
            cp.wait_send()
        for cp in mine:
            cp.wait()

    any_spec = pl.BlockSpec(memory_space=pl.ANY)
    return pl.pallas_call(
        body, name=name,
        out_shape=[full_shape(s, a) for s, a in zip(shards, axes)],
        in_specs=[any_spec] * n, out_specs=[any_spec] * n,
        scratch_shapes=[pltpu.SemaphoreType.DMA((7, n)), pltpu.SemaphoreType.DMA((7, n)),
                        pltpu.SemaphoreType.DMA((n,))],
    )(*shards)


def _pair_reduce(g, name):
    n_chips = N_DEV // 2
    R, C = g.shape[0] // N_DEV, g.shape[1]

    def body(g_ref, out_ref, mine_ref, theirs_ref, send_sems, recv_sems, local_sems):
        x, y, c = _my_position()
        block = lambda d: g_ref.at[pl.ds(pl.multiple_of(d * R, 16), R), :]
        sends = [pltpu.make_async_remote_copy(
            src_ref=block(2 * j + 1 - c), dst_ref=theirs_ref.at[j], send_sem=send_sems.at[j],
            recv_sem=recv_sems.at[j], device_id=(x, y, 1 - c), device_id_type=MESH) for j in range(n_chips)]
        own = [pltpu.make_async_copy(block(2 * j + c), mine_ref.at[j], local_sems.at[j]) for j in range(n_chips)]
        for cp in sends + own:
            cp.start()
        for j in range(n_chips):
            own[j].wait()
            sends[j].wait_recv()
            out_ref[j] = (mine_ref[j].astype(F32) + theirs_ref[j].astype(F32)).astype(g.dtype)
        for cp in sends:
            cp.wait_send()

    half = pltpu.VMEM((n_chips, R, C), g.dtype)
    sems = pltpu.SemaphoreType.DMA((n_chips,))
    return pl.pallas_call(
        body, name=name, out_shape=jax.ShapeDtypeStruct((n_chips, R, C), g.dtype),
        in_specs=[pl.BlockSpec(memory_space=pl.ANY)], out_specs=pl.BlockSpec(memory_space=pltpu.VMEM),
        scratch_shapes=[half, half, sems, sems, sems], compiler_params=_params(),
    )(g)


class _Ride:
    def __init__(self, parts):
        self.parts = [p for p in parts if p is not None]

    @staticmethod
    def gather(src, axis, land=None, lo=0, n=None):
        return ("gather", src, land, axis, lo, src.shape[axis] if n is None else n)

    @staticmethod
    def gather2(src, axis, land=None, lo=0, n=None):
        return ("gather2", src, land, axis, lo, src.shape[axis] if n is None else n)

    @staticmethod
    def scatter(src, axis, land=None, lo=0, n=None):
        return ("scatter", src, land, axis, lo, src.shape[axis] // N_DEV if n is None else n)

    @staticmethod
    def scatter_chips(chip_sums):
        return ("scatter_chips", chip_sums, None, 0, 0, chip_sums.shape[1])

    def arrays(self):
        return [p[1] for p in self.parts] + [p[2] for p in self.parts if p[2] is not None]

    def out_shapes(self):
        out = []
        for kind, src, _, axis, _, _ in self.parts:
            shape = list(src.shape)
            if kind in ("gather", "gather2"):
                shape[axis] *= N_DEV
            elif kind == "scatter_chips":
                pass
            else:
                shape[axis] //= N_DEV
                shape = [N_DEV] + shape
            out.append(jax.ShapeDtypeStruct(tuple(shape), src.dtype))
        return out

    def aliases(self, n_in, n_out):
        m, out = len(self.parts), {}
        for j, p in enumerate(self.parts):
            if p[2] is not None:
                out[n_in + m + len(out)] = n_out + j
        return out

    def scratch(self):
        m = len(self.parts)
        return [pltpu.SemaphoreType.DMA((N_DEV - 1, m)), pltpu.SemaphoreType.DMA((N_DEV - 1, m)),
                pltpu.SemaphoreType.DMA((m,))]

    def _copies(self, src_refs, land_refs, sems):
        send_sems, recv_sems, local_sems = sems
        x, y, c = _my_position()
        my_idx = 4 * x + 2 * y + c
        own, sends, relays, lands = [], [], [], []
        for j, (kind, src, _, axis, lo, n) in enumerate(self.parts):
            if kind == "scatter_chips":
                for k in (0, 2, 4, 6):
                    px, py = (1 - x if k & 4 else x), (1 - y if k & 2 else y)
                    a, b = src_refs[j].at[2 * px + py], land_refs[j].at[2 * x + y]
                    if k == 0:
                        own.append(pltpu.make_async_copy(a, b, local_sems.at[j]))
                        continue
                    mk = lambda dst, a=a, k=k, j=j, to=(px, py, c): pltpu.make_async_remote_copy(
                        src_ref=a, dst_ref=dst, send_sem=send_sems.at[k - 1, j], recv_sem=recv_sems.at[k - 1, j],
                        device_id=to, device_id_type=MESH)
                    sends.append(mk(b))
                    lands.append(mk(land_refs[j].at[2 * px + py]))
                continue
            size = src.shape[axis] if kind in ("gather", "gather2") else src.shape[axis] // N_DEV
            align = 16 if axis == 0 else LANES

            def rows(ref, idx, lead=None, axis=axis, lo=lo, n=n, size=size, align=align):
                at = pl.ds(pl.multiple_of(idx * size + lo, align), n)
                where = (at, slice(None)) if axis == 0 else (slice(None), at)
                return ref.at[where] if lead is None else ref.at[(lead, *where)]

            def in_shard(ref):
                return rows(ref, 0)

            def in_slot(ref, s):
                return rows(ref, 0, lead=s)

            if kind == "gather2":
                chips = [(1 - x, y), (x, 1 - y), (1 - x, 1 - y)]
                place = lambda px, py, pc: rows(land_refs[j], 4 * px + 2 * py + pc)

                def copy(i, a, dst, to, j=j):
                    return pltpu.make_async_remote_copy(
                        src_ref=a, dst_ref=dst, send_sem=send_sems.at[i, j], recv_sem=recv_sems.at[i, j],
                        device_id=to, device_id_type=MESH)

                mine = in_shard(src_refs[j])
                own.append(pltpu.make_async_copy(mine, place(x, y, c), local_sems.at[j]))
                sends.append(copy(0, mine, place(x, y, c), (x, y, 1 - c)))
                lands.append(copy(0, mine, place(x, y, 1 - c), (x, y, 1 - c)))
                for i, (px, py) in enumerate(chips):
                    sends.append(copy(1 + i, mine, place(x, y, c), (px, py, c)))
                    relays.append((copy(1 + i, mine, place(px, py, c), (px, py, c)),
                                   copy(4 + i, place(px, py, c), place(px, py, c), (x, y, 1 - c))))
                    lands.append(copy(4 + i, mine, place(px, py, 1 - c), (x, y, 1 - c)))
                continue
            for k in range(N_DEV):
                px = 1 - x if k & 4 else x
                py = 1 - y if k & 2 else y
                pc = 1 - c if k & 1 else c
                peer_idx = 4 * px + 2 * py + pc
                if kind == "gather":
                    a, b, landed = in_shard(src_refs[j]), rows(land_refs[j], my_idx), rows(land_refs[j], peer_idx)
                else:
                    a, b, landed = rows(src_refs[j], peer_idx), in_slot(land_refs[j], my_idx), in_slot(land_refs[j], peer_idx)
                if k == 0:
                    own.append(pltpu.make_async_copy(a, b, local_sems.at[j]))
                    continue
                mk = lambda dst, a=a, k=k, j=j, to=(px, py, pc): pltpu.make_async_remote_copy(
                    src_ref=a, dst_ref=dst, send_sem=send_sems.at[k - 1, j], recv_sem=recv_sems.at[k - 1, j],
                    device_id=to, device_id_type=MESH)
                sends.append(mk(b))
                lands.append(mk(landed))
        return own, sends, relays, lands

    @property
    def relayed(self):
        return any(p[0] == "gather2" for p in self.parts)

    def start(self, src_refs, land_refs, sems):
        own, sends, _, _ = self._copies(src_refs, land_refs, sems)
        for cp in own + sends:
            cp.start()

    def relay(self, src_refs, land_refs, sems):
        for arrival, onward in self._copies(src_refs, land_refs, sems)[2]:
            arrival.wait_recv()
            onward.start()

    def wait(self, src_refs, land_refs, sems):
        own, sends, relays, lands = self._copies(src_refs, land_refs, sems)
        for cp in lands:
            cp.wait_recv()
        for cp in sends + [onward for _, onward in relays]:
            cp.wait_send()
        for cp in own:
            cp.wait()


def _call(body, *, name, grid, in_specs, out_specs, out_shape, args, scratch_shapes=(), ride=None):
    in_specs, out_specs, out_shape = list(in_specs), list(out_specs), list(out_shape)
    n_in, n_out, n_sc = len(in_specs), len(out_specs), len(scratch_shapes)
    if ride is None or not ride.parts:
        res = pl.pallas_call(body, name=name, grid=grid, in_specs=in_specs, out_specs=out_specs,
                             out_shape=out_shape, scratch_shapes=list(scratch_shapes),
                             compiler_params=_params(len(grid)))(*args)
        return list(res), []
    extra, m = ride.arrays(), len(ride.parts)

    def riding(*refs):
        a = n_in + len(extra)
        b = a + n_out
        srcs, lands, sems = refs[n_in:n_in + m], refs[b:b + m], refs[b + m + n_sc:]
        at = [pl.program_id(d) for d in range(len(grid))]

        @pl.when(functools.reduce(jnp.logical_and, [i == 0 for i in at]))
        def _():
            ride.start(srcs, lands, sems)

        if ride.relayed:
            step, n_steps = at[0], 1
            for i, g in zip(at[1:], grid[1:]):
                step = step * g + i
            for g in grid:
                n_steps *= g
            assert n_steps >= 2, "a two-level ride needs a grid step after the first"

            @pl.when(step == n_steps - 1)
            def _():
                ride.relay(srcs, lands, sems)

        body(*refs[:n_in], *refs[a:b], *refs[b + m:b + m + n_sc])

        @pl.when(functools.reduce(jnp.logical_and, [i == g - 1 for i, g in zip(at, grid)]))
        def _():
            ride.wait(srcs, lands, sems)

    hbm = pl.BlockSpec(memory_space=pl.ANY)
    res = pl.pallas_call(
        riding, name=name, grid=grid, in_specs=in_specs + [hbm] * len(extra), out_specs=out_specs + [hbm] * m,
        out_shape=out_shape + ride.out_shapes(), scratch_shapes=list(scratch_shapes) + ride.scratch(),
        input_output_aliases=ride.aliases(n_in, n_out), compiler_params=_params(len(grid)),
    )(*args, *extra)
    return list(res[:n_out]), list(res[n_out:])


def _prenorm_inproj(h, gain, w_in_t, name, ride=None):
    T = h.shape[0]

    def body(h_ref, g_ref, w_ref, q_ref, k_ref, v_ref, ug_ref, hn_ref):
        hv = h_ref[...]
        r = lax.rsqrt(jnp.mean(hv * hv, axis=-1, keepdims=True) + EPS)
        hn = (hv * r * g_ref[...]).astype(BF16)
        hn_ref[...] = hn
        for j in range(N_CHUNK):
            u = _nt(hn, w_ref[j * CHUNK:(j + 1) * CHUNK, :])
            if j == 0:
                q_ref[...] = (u * (HEAD_DIM ** -0.5)).astype(BF16)
            elif j == 1:
                k_ref[...] = u.astype(BF16)
            elif j == 2:
                v_ref[...] = u.astype(BF16)
            else:
                ug_ref[:, (j - 3) * CHUNK:(j - 2) * CHUNK] = u.astype(BF16)

    act = jax.ShapeDtypeStruct((T, CHUNK), BF16)
    return _call(
        body, name=name, grid=(T // TM,),
        in_specs=[_rows(TM, D_MODEL), _whole((1, D_MODEL)), _whole((N_CHUNK * CHUNK, D_MODEL))],
        out_specs=[_rows(TM, CHUNK)] * 3 + [_rows(TM, 4 * CHUNK), _rows(TM, D_MODEL)],
        out_shape=[act, act, act, jax.ShapeDtypeStruct((T, 4 * CHUNK), BF16),
                   jax.ShapeDtypeStruct((T, D_MODEL), BF16)],
        args=(h, gain, w_in_t,), ride=ride)


def _softplus_parts(z):
    ez = jnp.exp(jnp.minimum(z, SOFTPLUS_LINEAR_AT))
    t = 1.0 + ez
    return ez * pl.reciprocal(t, approx=True), jnp.where(z > SOFTPLUS_LINEAR_AT, z, jnp.log(t))


def _attn_fwd(qs, k, v, tri, name, ride=None):
    T = qs.shape[0]
    assert T // BLK <= FIRST_BLOCK_LANE, "one lane per key block below the lane of the first block"
    width = LANES * ATT_COLS
    chains = [(c, half) for c in range(ATT_COLS) for half in range(2)]

    def body(q_ref, k_ref, v_ref, m_ref, o_ref, cs_ref):
        qi = pl.program_id(1)
        lane = lax.broadcasted_iota(jnp.int32, (BLK, LANES), 1)
        first = lane < HEAD_DIM
        causal = (lax.broadcasted_iota(jnp.int32, (BLK, BLK), 1)
                  < lax.broadcasted_iota(jnp.int32, (BLK, BLK), 0))
        tri_m = m_ref[...]
        qh = {}
        for c in range(ATT_COLS):
            q = q_ref[:, c * LANES:(c + 1) * LANES]
            zero = jnp.zeros_like(q)
            qh[c, 0], qh[c, 1] = jnp.where(first, q, zero), jnp.where(first, zero, q)

        def step(kb, state, masked):
            carries, accs, cvals = state
            start = pl.multiple_of(kb * BLK, BLK)
            kblk = [k_ref[pl.ds(start, BLK), c * LANES:(c + 1) * LANES] for c in range(ATT_COLS)]
            vblk = [v_ref[pl.ds(start, BLK), c * LANES:(c + 1) * LANES] for c in range(ATT_COLS)]
            carries, accs, cvals = list(carries), list(accs), list(cvals)
            for g0 in range(0, len(chains), CHAIN_GROUP):
                ids = range(g0, g0 + CHAIN_GROUP)
                z = [_nt(qh[chains[n]], kblk[chains[n][0]]) for n in ids]
                sp = [_softplus_parts(zi)[1] for zi in z]
                if masked:
                    sp = [jnp.where(causal, s, 0.0) for s in sp]
                incl = [_dot_hilo(s, tri_m) for s in sp]
                a = [jnp.exp(zi - ii - carries[n]) for n, zi, ii in zip(ids, z, incl)]
                if masked:
                    a = [jnp.where(causal, ai, 0.0) for ai in a]
                for n, ai, ii in zip(ids, a, incl):
                    c, half = chains[n]
                    zero = jnp.zeros_like(vblk[c])
                    vh = jnp.where(first, vblk[c], zero) if half == 0 else jnp.where(first, zero, vblk[c])
                    accs[c] = accs[c] + _nn(ai.astype(BF16), vh)
                    cvals[c] = jnp.where(lane == kb + HEAD_DIM * half, carries[n], cvals[c])
                    carries[n] = carries[n] + ii[:, 0:1]
            return tuple(carries), tuple(accs), tuple(cvals)

        zeros = tuple(jnp.zeros((BLK, LANES), F32) for _ in range(ATT_COLS))
        state = (tuple(jnp.zeros((BLK, 1), F32) for _ in chains), zeros, zeros)
        state = step(qi, state, True)

        def reaches_further(st):
            it, (carries, _, _) = st
            least = functools.reduce(jnp.minimum, carries)
            return jnp.logical_and(it < qi, jnp.min(least) < DEAD_AT)

        done, state = lax.while_loop(reaches_further, lambda st: (st[0] + 1, step(qi - 1 - st[0], st[1], False)),
                                     (jnp.int32(0), state))
        first_block = (qi - done).astype(F32)
        for c in range(ATT_COLS):
            o_ref[:, c * LANES:(c + 1) * LANES] = state[1][c]
            cs_ref[:, c * LANES:(c + 1) * LANES] = jnp.where(lane == FIRST_BLOCK_LANE, first_block, state[2][c])

    blk = pl.BlockSpec((BLK, width), lambda j, i: (i, j))
    col = pl.BlockSpec((T, width), lambda j, i: (0, j))
    out = jax.ShapeDtypeStruct((T, ATTN_DIM), F32)
    return _call(
        body, name=name, grid=(ATTN_DIM // width, T // BLK),
        in_specs=[blk, col, col, _whole((BLK, BLK))],
        out_specs=[blk, blk], out_shape=[out, out],
        args=(qs, k, v, tri,), ride=ride)


def _shifted_copies(pad_ref, sh_ref):
    rows = sh_ref.shape[1]
    for b in range(SUBLANES):
        sh_ref[b] = pad_ref[b:b + rows, :]


def _shift_of(offset):
    return offset % SUBLANES, offset - offset % SUBLANES


def _conv_fwd(ug, dw_w, dw_b, ln_g, ln_b, name, ride=None):
    T = ug.shape[0]
    per = TM // HALO

    def body(cv_ref, cg_ref, cvh_ref, cgh_ref, w_ref, b_ref, g_ref, beta_ref, conv_ref, c2_ref, pad_ref, sh_ref):
        i = pl.program_id(0)
        halo = cvh_ref[...].astype(F32) * _sigmoid(cgh_ref[...].astype(F32))
        pad_ref[0:HALO, :] = jnp.where(i == 0, 0.0, halo)
        pad_ref[HALO:HALO + TM, :] = cv_ref[...].astype(F32) * _sigmoid(cg_ref[...].astype(F32))
        pad_ref[HALO + TM:, :] = jnp.zeros((SUBLANES, CONV_DIM), F32)
        _shifted_copies(pad_ref, sh_ref)
        taps = [w_ref[t:t + 1, :] for t in range(CONV_WIDTH)]

        def rows(j, _):
            r = pl.multiple_of(j * CONV_ROWS, CONV_ROWS)
            acc = jnp.zeros((CONV_ROWS, CONV_DIM), F32) + b_ref[...]
            for t in range(CONV_WIDTH):
                b, a = _shift_of(HALO - (CONV_WIDTH - 1) + t)
                acc = acc + taps[t] * sh_ref[b, pl.ds(r + a, CONV_ROWS), :]
            conv_ref[pl.ds(r, CONV_ROWS), :] = acc
            return 0

        lax.fori_loop(0, TM // CONV_ROWS, rows, 0)
        acc = conv_ref[...]
        mu = jnp.mean(acc, axis=-1, keepdims=True)
        xc = acc - mu
        rs = lax.rsqrt(jnp.mean(xc * xc, axis=-1, keepdims=True) + EPS)
        ln = xc * rs * g_ref[...] + beta_ref[...]
        c2_ref[...] = (ln * _sigmoid(ln)).astype(BF16)

    prev = lambda col: pl.BlockSpec((HALO, CHUNK), lambda i: (jnp.maximum(i * per - 1, 0), col))
    vec = _whole((1, CONV_DIM))
    return _call(
        body, name=name, grid=(T // TM,),
        in_specs=[_rows(TM, CHUNK, 1), _rows(TM, CHUNK, 2), prev(1), prev(2),
                  _whole((CONV_WIDTH, CONV_DIM)), vec, vec, vec],
        out_specs=[_rows(TM, CONV_DIM), _rows(TM, CONV_DIM)],
        out_shape=[jax.ShapeDtypeStruct((T, CONV_DIM), F32), jax.ShapeDtypeStruct((T, CONV_DIM), BF16)],
        scratch_shapes=[pltpu.VMEM((TM + HALO + SUBLANES, CONV_DIM), F32),
                        pltpu.VMEM((SUBLANES, TM + HALO, CONV_DIM), F32)],
        args=(ug, ug, ug, ug, dw_w, dw_b, ln_g, ln_b,), ride=ride)


def _mix_out_ple(o, ug, c2, h, p, head_mean, g_attn, g_conv, g_ple, w_pw, w_out, w_gate, w_ple, name, ride=None):
    T = h.shape[0]

    def body(o_ref, ga_ref, gc_ref, c2_ref, h_ref, p_ref, hm_ref, gao_ref, gco_ref, gpn_ref,
             wpw_ref, wout_ref, wg_ref, wple_ref,
             h2_ref, h1_ref, ycat_ref, hn2_ref, gate_ref, e_ref, c3_ref):
        ov = o_ref[...]
        rh = lax.rsqrt(_nn((ov * ov).astype(BF16), hm_ref[...]) + EPS)
        ga = ga_ref[...].astype(F32)
        ya = (ov * rh * gao_ref[...] * (ga * _sigmoid(ga))).astype(BF16)
        c3 = _nn(c2_ref[...], wpw_ref[...])
        c3_ref[...] = c3
        rc = lax.rsqrt(jnp.mean(c3 * c3, axis=-1, keepdims=True) + EPS)
        gc = gc_ref[...].astype(F32)
        yc = (c3 * rc * gco_ref[...] * (gc * _sigmoid(gc))).astype(BF16)
        ycat_ref[:, :ATTN_DIM] = ya
        ycat_ref[:, ATTN_DIM:] = yc
        h1 = h_ref[...] + _nn(ya, wout_ref[:ATTN_DIM, :]) + _nn(yc, wout_ref[ATTN_DIM:, :])
        h1_ref[...] = h1
        r1 = lax.rsqrt(jnp.mean(h1 * h1, axis=-1, keepdims=True) + EPS)
        hn2 = (h1 * r1 * gpn_ref[...]).astype(BF16)
        hn2_ref[...] = hn2
        gate = _sigmoid(_nn(hn2, wg_ref[...]))
        e = _nn(p_ref[...].astype(BF16), wple_ref[...])
        gate_ref[...] = gate
        e_ref[...] = e
        h2_ref[...] = h1 + e * gate

    f32 = lambda cols: jax.ShapeDtypeStruct((T, cols), F32)
    bf = lambda cols: jax.ShapeDtypeStruct((T, cols), BF16)
    return _call(
        body, name=name, grid=(T // TM,),
        in_specs=[_rows(TM, ATTN_DIM), _rows(TM, CHUNK, 0), _rows(TM, CHUNK, 3), _rows(TM, CONV_DIM),
                  _rows(TM, D_MODEL), _rows(TM, PLE_DIM), _whole((ATTN_DIM, ATTN_DIM)),
                  _whole((1, ATTN_DIM)), _whole((1, CONV_DIM)), _whole((1, D_MODEL)),
                  _whole((CONV_DIM, CONV_DIM)), _whole((D_MODEL, D_MODEL)), _whole((D_MODEL, D_MODEL)),
                  _whole((PLE_DIM, D_MODEL))],
        out_specs=[_rows(TM, D_MODEL), _rows(TM, D_MODEL), _rows(TM, D_MODEL), _rows(TM, D_MODEL),
                   _rows(TM, D_MODEL), _rows(TM, D_MODEL), _rows(TM, CONV_DIM)],
        out_shape=[f32(D_MODEL), f32(D_MODEL), bf(D_MODEL), bf(D_MODEL), f32(D_MODEL), f32(D_MODEL),
                   f32(CONV_DIM)],
        args=(o, ug, ug, c2, h, p, head_mean, g_attn, g_conv, g_ple, w_pw, w_out, w_gate, w_ple,), ride=ride)


def _final_loss(h, target, gain, name):
    T = h.shape[0]

    def body(h_ref, t_ref, g_ref, dh_ref, gsum_ref, loss_ref):
        @pl.when(pl.program_id(0) == 0)
        def _():
            gsum_ref[...] = jnp.zeros_like(gsum_ref)
            loss_ref[...] = jnp.zeros_like(loss_ref)

        hv = h_ref[...]
        r = lax.rsqrt(jnp.mean(hv * hv, axis=-1, keepdims=True) + EPS)
        xh = hv * r
        diff = xh * g_ref[...] - t_ref[...]
        loss_ref[...] += 0.5 * jnp.sum(jnp.mean(diff * diff, axis=-1, keepdims=True), axis=0, keepdims=True)
        dy = diff * (1.0 / D_MODEL)
        gsum_ref[...] += jnp.sum(dy * xh, axis=0, keepdims=True)
        dxh = dy * g_ref[...]
        dh_ref[...] = r * (dxh - xh * jnp.mean(dxh * xh, axis=-1, keepdims=True))

    return pl.pallas_call(
        body, name=name, grid=(T // TM,),
        in_specs=[_rows(TM, D_MODEL), _rows(TM, D_MODEL), _whole((1, D_MODEL))],
        out_specs=[_rows(TM, D_MODEL), _whole((1, D_MODEL)), _whole((1, LANES))],
        out_shape=[jax.ShapeDtypeStruct((T, D_MODEL), F32), jax.ShapeDtypeStruct((1, D_MODEL), F32),
                   jax.ShapeDtypeStruct((1, LANES), F32)],
        compiler_params=_params(1),
    )(h, target, gain)


def _ple_out_bwd(dh2, gate, e, h1, g_ple, w_gate, w_out, name, ride=None):
    T = dh2.shape[0]

    def body(dh2_ref, gate_ref, e_ref, h1_ref, gpn_ref, wg_ref, wout_ref,
             dh1_ref, dh1b_ref, dzg_ref, de_ref, dycat_ref, gsum_ref):
        @pl.when(pl.program_id(0) == 0)
        def _():
            gsum_ref[...] = jnp.zeros_like(gsum_ref)

        dh2v = dh2_ref[...]
        gate = gate_ref[...]
        de_ref[...] = (dh2v * gate).astype(BF16)
        dzg = (dh2v * e_ref[...] * gate * (1.0 - gate)).astype(BF16)
        dzg_ref[...] = dzg
        dhn2 = _nt(dzg, wg_ref[...])
        h1 = h1_ref[...]
        r1 = lax.rsqrt(jnp.mean(h1 * h1, axis=-1, keepdims=True) + EPS)
        xh = h1 * r1
        gsum_ref[...] += jnp.sum(dhn2 * xh, axis=0, keepdims=True)
        dxh = dhn2 * gpn_ref[...]
        dh1 = dh2v + r1 * (dxh - xh * jnp.mean(dxh * xh, axis=-1, keepdims=True))
        dh1_ref[...] = dh1
        dh1b = dh1.astype(BF16)
        dh1b_ref[...] = dh1b
        dycat_ref[...] = _nt(dh1b, wout_ref[...])

    f32 = jax.ShapeDtypeStruct((T, D_MODEL), F32)
    bf = jax.ShapeDtypeStruct((T, D_MODEL), BF16)
    full = _rows(TM, D_MODEL)
    return _call(
        body, name=name, grid=(T // TM,),
        in_specs=[full, full, full, full, _whole((1, D_MODEL)), _whole((D_MODEL, D_MODEL)),
                  _whole((D_MODEL, D_MODEL))],
        out_specs=[full, full, full, full, full, _whole((1, D_MODEL))],
        out_shape=[f32, bf, bf, bf, f32, jax.ShapeDtypeStruct((1, D_MODEL), F32)],
        args=(dh2, gate, e, h1, g_ple, w_gate, w_out,), ride=ride)


def _branch_bwd(dycat, o, ug, c3, conv, head_mean, g_attn, g_conv, ln_g, ln_b, w_pw, name, ride=None):
    T = o.shape[0]

    def body(dya_ref, dyc_ref, o_ref, ga_ref, gc_ref, c3_ref, conv_ref, hm_ref, gao_ref, gco_ref,
             lng_ref, lnb_ref, wpw_ref,
             do_ref, dga_ref, dgc_ref, dc3_ref, dconv_ref, sums_ref):
        @pl.when(pl.program_id(0) == 0)
        def _():
            sums_ref[...] = jnp.zeros_like(sums_ref)

        hm = hm_ref[...]
        col = lambda x: jnp.sum(x, axis=0, keepdims=True)
        ov = o_ref[...]
        rh = lax.rsqrt(_nn((ov * ov).astype(BF16), hm) + EPS)
        xh = ov * rh
        ga = ga_ref[...].astype(F32)
        sg = _sigmoid(ga)
        dya = dya_ref[...]
        don = dya * (ga * sg)
        dga_ref[...] = (dya * xh * gao_ref[...] * _dsilu(ga, sg)).astype(BF16)
        sums_ref[0:1, :] += col(don * xh)
        dxh = don * gao_ref[...]
        do_ref[...] = (rh * (dxh - xh * _dot_hilo(dxh * xh, hm))).astype(BF16)
        c3 = c3_ref[...]
        rc = lax.rsqrt(jnp.mean(c3 * c3, axis=-1, keepdims=True) + EPS)
        xh3 = c3 * rc
        gc = gc_ref[...].astype(F32)
        sgc = _sigmoid(gc)
        dyc = dyc_ref[...]
        dn3 = dyc * (gc * sgc)
        dgc_ref[...] = (dyc * xh3 * gco_ref[...] * _dsilu(gc, sgc)).astype(BF16)
        sums_ref[1:2, :] += col(dn3 * xh3)
        dxh3 = dn3 * gco_ref[...]
        dc3 = (rc * (dxh3 - xh3 * jnp.mean(dxh3 * xh3, axis=-1, keepdims=True))).astype(BF16)
        dc3_ref[...] = dc3
        dc2 = _nt(dc3, wpw_ref[...])
        cv = conv_ref[...]
        mu = jnp.mean(cv, axis=-1, keepdims=True)
        xc = cv - mu
        rs = lax.rsqrt(jnp.mean(xc * xc, axis=-1, keepdims=True) + EPS)
        xn = xc * rs
        ln = xn * lng_ref[...] + lnb_ref[...]
        dln = dc2 * _dsilu(ln, _sigmoid(ln))
        sums_ref[2:3, :] += col(dln * xn)
        sums_ref[3:4, :] += col(dln)
        dxn = dln * lng_ref[...]
        dconv = rs * (dxn - jnp.mean(dxn, axis=-1, keepdims=True)
                      - xn * jnp.mean(dxn * xn, axis=-1, keepdims=True))
        dconv_ref[...] = dconv
        sums_ref[4:5, :] += col(dconv)

    half = lambda dt: jax.ShapeDtypeStruct((T, CHUNK), dt)
    tile = _rows(TM, CHUNK)
    vec = _whole((1, CHUNK))
    return _call(
        body, name=name, grid=(T // TM,),
        in_specs=[_rows(TM, CHUNK, 0), _rows(TM, CHUNK, 1), tile, _rows(TM, CHUNK, 0), _rows(TM, CHUNK, 3),
                  tile, tile, _whole((ATTN_DIM, ATTN_DIM)), vec, vec, vec, vec, _whole((CONV_DIM, CONV_DIM))],
        out_specs=[tile, tile, tile, tile, tile, _whole((8, CHUNK))],
        out_shape=[half(BF16), half(BF16), half(BF16), half(BF16), half(F32),
                   jax.ShapeDtypeStruct((8, CHUNK), F32)],
        args=(dycat, dycat, o, ug, ug, c3, conv, head_mean, g_attn, g_conv, ln_g, ln_b, w_pw,), ride=ride)


def _conv_bwd(dconv, ug, dw_w, name, ride=None):
    T = dconv.shape[0]
    per = TM // HALO
    last = T // HALO - 1
    n_tiles = T // TM

    def body(d_ref, dn_ref, cv_ref, cg_ref, cvh_ref, cgh_ref, w_ref, dcv_ref, dcg_ref, dw_ref,
             dpad_ref, cpad_ref, dsh_ref, csh_ref, dw_acc):
        i = pl.program_id(0)

        @pl.when(i == 0)
        def _():
            dw_acc[...] = jnp.zeros_like(dw_acc)

        tail = jnp.zeros((SUBLANES, CONV_DIM), F32)
        dpad_ref[0:TM, :] = d_ref[...]
        dpad_ref[TM:TM + HALO, :] = jnp.where(i == n_tiles - 1, 0.0, dn_ref[...])
        dpad_ref[TM + HALO:, :] = tail
        halo = cvh_ref[...].astype(F32) * _sigmoid(cgh_ref[...].astype(F32))
        cpad_ref[0:HALO, :] = jnp.where(i == 0, 0.0, halo)
        cpad_ref[HALO:HALO + TM, :] = cv_ref[...].astype(F32) * _sigmoid(cg_ref[...].astype(F32))
        cpad_ref[HALO + TM:, :] = tail
        _shifted_copies(dpad_ref, dsh_ref)
        _shifted_copies(cpad_ref, csh_ref)
        taps = [w_ref[t:t + 1, :] for t in range(CONV_WIDTH)]

        def rows(j, _):
            r = pl.multiple_of(j * CONV_ROWS, CONV_ROWS)
            d = d_ref[pl.ds(r, CONV_ROWS), :]
            dc = jnp.zeros((CONV_ROWS, CONV_DIM), F32)
            for t in range(CONV_WIDTH):
                b, a = _shift_of(CONV_WIDTH - 1 - t)
                dc = dc + taps[t] * dsh_ref[b, pl.ds(r + a, CONV_ROWS), :]
                b, a = _shift_of(HALO - (CONV_WIDTH - 1) + t)
                prod = d * csh_ref[b, pl.ds(r + a, CONV_ROWS), :]
                dw_acc[t] += jnp.sum(prod.reshape(CONV_ROWS // SUBLANES, SUBLANES, CONV_DIM), axis=0)
            cv = cv_ref[pl.ds(r, CONV_ROWS), :].astype(F32)
            sg = _sigmoid(cg_ref[pl.ds(r, CONV_ROWS), :].astype(F32))
            dcv_ref[pl.ds(r, CONV_ROWS), :] = (dc * sg).astype(BF16)
            dcg_ref[pl.ds(r, CONV_ROWS), :] = (dc * cv * sg * (1.0 - sg)).astype(BF16)
            return 0

        lax.fori_loop(0, TM // CONV_ROWS, rows, 0)

        @pl.when(i == n_tiles - 1)
        def _():
            dw_ref[...] = jnp.zeros_like(dw_ref)
            for t in range(CONV_WIDTH):
                dw_ref[t:t + 1, :] = jnp.sum(dw_acc[t], axis=0, keepdims=True)

    prev = lambda col: pl.BlockSpec((HALO, CHUNK), lambda i: (jnp.maximum(i * per - 1, 0), col))
    nxt = pl.BlockSpec((HALO, CONV_DIM), lambda i: (jnp.minimum((i + 1) * per, last), 0))
    half = jax.ShapeDtypeStruct((T, CHUNK), BF16)
    return _call(
        body, name=name, grid=(T // TM,),
        in_specs=[_rows(TM, CONV_DIM), nxt, _rows(TM, CHUNK, 1), _rows(TM, CHUNK, 2), prev(1), prev(2),
                  _whole((CONV_WIDTH, CONV_DIM))],
        out_specs=[_rows(TM, CHUNK), _rows(TM, CHUNK), _whole((HALO, CONV_DIM))],
        out_shape=[half, half, jax.ShapeDtypeStruct((HALO, CONV_DIM), F32)],
        scratch_shapes=[pltpu.VMEM((TM + HALO + SUBLANES, CONV_DIM), F32),
                        pltpu.VMEM((TM + HALO + SUBLANES, CONV_DIM), F32),
                        pltpu.VMEM((SUBLANES, TM + HALO, CONV_DIM), F32),
                        pltpu.VMEM((SUBLANES, TM + HALO, CONV_DIM), F32),
                        pltpu.VMEM((HALO, SUBLANES, CONV_DIM), F32)],
        args=(dconv, dconv, ug, ug, ug, ug, dw_w,), ride=ride)


def _attn_bwd(qs, k, v, do, cs, tri, tri_t, name, ride=None):
    T = qs.shape[0]
    nq = T // BLK
    width = LANES * ATT_COLS
    chains = [(c, half) for c in range(ATT_COLS) for half in range(2)]

    def body(q_ref, k_ref, v_ref, do_ref, cs_ref, m_ref, mt_ref, dq_ref, dk_ref, dv_ref, dk_acc, dv_acc):
        qi = pl.program_id(1)

        @pl.when(qi == 0)
        def _():
            dk_acc[...] = jnp.zeros_like(dk_acc)
            dv_acc[...] = jnp.zeros_like(dv_acc)

        lane = lax.broadcasted_iota(jnp.int32, (BLK, LANES), 1)
        first = lane < HEAD_DIM
        causal = (lax.broadcasted_iota(jnp.int32, (BLK, BLK), 1)
                  < lax.broadcasted_iota(jnp.int32, (BLK, BLK), 0))
        tri_m = m_ref[...]
        tri_mt = mt_ref[...]

        def halves(x):
            zero = jnp.zeros_like(x)
            return jnp.where(first, x, zero), jnp.where(first, zero, x)

        qh, doh, cs = {}, {}, []
        for c in range(ATT_COLS):
            qh[c, 0], qh[c, 1] = halves(q_ref[:, c * LANES:(c + 1) * LANES])
            doh[c, 0], doh[c, 1] = halves(do_ref[:, c * LANES:(c + 1) * LANES])
            cs.append(cs_ref[:, c * LANES:(c + 1) * LANES])

        def step(kb, state, masked):
            prefixes, dq_accs = state
            start = pl.multiple_of(kb * BLK, BLK)
            kblk = [k_ref[pl.ds(start, BLK), c * LANES:(c + 1) * LANES] for c in range(ATT_COLS)]
            vblk = [v_ref[pl.ds(start, BLK), c * LANES:(c + 1) * LANES] for c in range(ATT_COLS)]
            prefixes, dq_accs = list(prefixes), list(dq_accs)
            for g0 in range(0, len(chains), CHAIN_GROUP):
                ids = range(g0, g0 + CHAIN_GROUP)
                grp = [chains[n] for n in ids]
                z = [_nt(qh[ch], kblk[ch[0]]) for ch in grp]
                da = [_nt(doh[ch], vblk[ch[0]]) for ch in grp]
                parts = [_softplus_parts(zi) for zi in z]
                sp = [pt[1] for pt in parts]
                if masked:
                    sp = [jnp.where(causal, s, 0.0) for s in sp]
                incl = [_dot_hilo(s, tri_m) for s in sp]
                carries = [jnp.sum(jnp.where(lane == kb + HEAD_DIM * half, cs[c], 0.0), axis=1, keepdims=True)
                           for c, half in grp]
                a = [jnp.exp(zi - ii - ci) for zi, ii, ci in zip(z, incl, carries)]
                if masked:
                    a = [jnp.where(causal, ai, 0.0) for ai in a]
                w = [ai * di for ai, di in zip(a, da)]
                pinc = [_nn(wi.astype(BF16), tri_mt) for wi in w]
                dz = [wi - pt[0] * (pi + prefixes[n]) for n, wi, pt, pi in zip(ids, w, parts, pinc)]
                if masked:
                    dz = [jnp.where(causal, d, 0.0) for d in dz]
                for j in range(0, CHAIN_GROUP, 2):
                    c = grp[j][0]
                    k0, k1 = halves(kblk[c])
                    dz0, dz1 = dz[j].astype(BF16), dz[j + 1].astype(BF16)
                    a0, a1 = a[j].astype(BF16), a[j + 1].astype(BF16)
                    dq_accs[c] = dq_accs[c] + _nn(dz0, k0) + _nn(dz1, k1)
                    dk_acc[pl.ds(start, BLK), c * LANES:(c + 1) * LANES] += _tn(dz0, qh[c, 0]) + _tn(dz1, qh[c, 1])
                    dv_acc[pl.ds(start, BLK), c * LANES:(c + 1) * LANES] += _tn(a0, doh[c, 0]) + _tn(a1, doh[c, 1])
                for n, pi in zip(ids, pinc):
                    prefixes[n] = prefixes[n] + pi[:, BLK - 1:BLK]
            return tuple(prefixes), tuple(dq_accs)

        state = (tuple(jnp.zeros((BLK, 1), F32) for _ in chains),
                 tuple(jnp.zeros((BLK, LANES), F32) for _ in range(ATT_COLS)))
        first_block = jnp.max(jnp.where(lane == FIRST_BLOCK_LANE, cs[0], 0.0)).astype(jnp.int32)
        state = lax.fori_loop(first_block, qi, lambda kb, st: step(kb, st, False), state)
        state = step(qi, state, True)
        for c in range(ATT_COLS):
            dq_ref[:, c * LANES:(c + 1) * LANES] = (state[1][c] * (HEAD_DIM ** -0.5)).astype(BF16)

        @pl.when(qi == nq - 1)
        def _():
            dk_ref[...] = dk_acc[...].astype(BF16)
            dv_ref[...] = dv_acc[...].astype(BF16)

    blk = pl.BlockSpec((BLK, width), lambda j, i: (i, j))
    col = pl.BlockSpec((T, width), lambda j, i: (0, j))
    out = jax.ShapeDtypeStruct((T, ATTN_DIM), BF16)
    return _call(
        body, name=name, grid=(ATTN_DIM // width, nq),
        in_specs=[blk, col, col, blk, blk, _whole((BLK, BLK)), _whole((BLK, BLK))],
        out_specs=[blk, col, col], out_shape=[out, out, out],
        scratch_shapes=[pltpu.VMEM((T, width), F32), pltpu.VMEM((T, width), F32)],
        args=(qs, k, v, do, cs, tri, tri_t,), ride=ride)


def _inproj_bwd(du, w_in_t, h, dh1, gain, name, ride=None):
    T = h.shape[0]

    def body(*refs):
        du_refs = refs[:N_CHUNK]
        w_ref, h_ref, dh1_ref, g_ref, dh_ref, gsum_ref = refs[N_CHUNK:]

        @pl.when(pl.program_id(0) == 0)
        def _():
            gsum_ref[...] = jnp.zeros_like(gsum_ref)

        dhn = jnp.zeros((TM, D_MODEL), F32)
        for j in range(N_CHUNK):
            dhn = dhn + _nn(du_refs[j][...], w_ref[j * CHUNK:(j + 1) * CHUNK, :])
        hv = h_ref[...]
        r = lax.rsqrt(jnp.mean(hv * hv, axis=-1, keepdims=True) + EPS)
        xh = hv * r
        gsum_ref[...] += jnp.sum(dhn * xh, axis=0, keepdims=True)
        dxh = dhn * g_ref[...]
        dh_ref[...] = dh1_ref[...] + r * (dxh - xh * jnp.mean(dxh * xh, axis=-1, keepdims=True))

    full = _rows(TM, D_MODEL)
    return _call(
        body, name=name, grid=(T // TM,),
        in_specs=[_rows(TM, CHUNK)] * N_CHUNK + [_whole((N_CHUNK * CHUNK, D_MODEL)), full, full,
                                                 _whole((1, D_MODEL))],
        out_specs=[full, _whole((1, D_MODEL))],
        out_shape=[jax.ShapeDtypeStruct((T, D_MODEL), F32), jax.ShapeDtypeStruct((1, D_MODEL), F32)],
        args=(*du, w_in_t, h, dh1, gain), ride=ride)


def _weight_grad(lhs_list, rhs, name, tk=CHUNK, ride=None):
    T, n_rhs = rhs.shape
    n = len(lhs_list)
    ka = lhs_list[0].shape[1]
    per = ka // tk

    def body(*refs):
        a_refs, b_ref, out_ref = refs[:n], refs[n], refs[n + 1]
        step = pl.program_id(0)
        for j in range(n):
            for s in range(per):
                @pl.when(step == j * per + s)
                def _(j=j, s=s):
                    out_ref[...] = _tn(a_refs[j][:, s * tk:(s + 1) * tk], b_ref[...]).astype(BF16)

    (grad,), landed = _call(
        body, name=name, grid=(n * per,),
        in_specs=[_whole((T, ka))] * n + [_whole((T, n_rhs))],
        out_specs=[pl.BlockSpec((tk, n_rhs), lambda i: (i, 0))],
        out_shape=[jax.ShapeDtypeStruct((n * ka, n_rhs), BF16)],
        args=(*lhs_list, rhs), ride=ride)
    return grad, landed


def _adamw_update(w, g, m, v):
    nm = ADAM_B1 * m + (1.0 - ADAM_B1) * g
    nv = ADAM_B2 * v + (1.0 - ADAM_B2) * (g * g)
    m_hat = nm / (1.0 - ADAM_B1 ** ADAM_STEP)
    v_hat = nv / (1.0 - ADAM_B2 ** ADAM_STEP)
    return -ADAM_LR * (m_hat / (jnp.sqrt(v_hat) + ADAM_EPS) + ADAM_WD * w), nm, nv


def _sum_adamw(slots, w, m, v, name):
    depth, R, C = w.shape
    tr = next(rows for rows in ADAMW_ROWS if R % rows == 0)

    def body(*refs):
        slot_refs, (w_ref, m_ref, v_ref, g_ref, d_ref, nm_ref, nv_ref) = refs[:depth], refs[depth:]
        for layer in range(depth):
            @pl.when(pl.program_id(0) == layer)
            def _(src=slot_refs[layer]):
                g = src[0].astype(F32)
                for s in range(1, src.shape[0]):
                    g = g + src[s].astype(F32)
                g_ref[0] = g
                d_ref[0], nm_ref[0], nv_ref[0] = _adamw_update(w_ref[0], g, m_ref[0], v_ref[0])

    slot_spec = lambda layer: pl.BlockSpec((slots[layer].shape[0], tr, C),
                                           lambda l, i: (0, jnp.where(l == layer, i, 0), 0))
    spec = pl.BlockSpec((1, tr, C), lambda l, i: (l, i, 0))
    out = jax.ShapeDtypeStruct((depth, R, C), F32)
    return pl.pallas_call(
        body, name=name, grid=(depth, R // tr),
        in_specs=[slot_spec(layer) for layer in range(depth)] + [spec] * 3,
        out_specs=[spec] * 4, out_shape=[out] * 4,
        compiler_params=_params(2),
    )(*slots, w, m, v)


def _adamw(w, g, m, v, name):
    R, C = w.shape
    tr = R
    for cand in (512, 256, 128, 64):
        if R % cand == 0 and R > cand:
            tr = cand
            break

    def body(w_ref, g_ref, m_ref, v_ref, d_ref, nm_ref, nv_ref):
        d_ref[...], nm_ref[...], nv_ref[...] = _adamw_update(w_ref[...], g_ref[...], m_ref[...], v_ref[...])

    spec = pl.BlockSpec((tr, C), lambda i: (i, 0))
    out = jax.ShapeDtypeStruct((R, C), F32)
    return pl.pallas_call(
        body, name=name, grid=(R // tr,),
        in_specs=[spec] * 4, out_specs=[spec] * 3, out_shape=[out, out, out],
        compiler_params=_params(1),
    )(w, g, m, v)


def _pack_small(values, scalar=None):
    pad = lambda a: jnp.pad(a, ((0, 0), (0, D_MODEL - a.shape[1])))
    last = jnp.zeros((1, D_MODEL), F32) if scalar is None else pad(scalar.reshape(1, 1))
    return jnp.concatenate([pad(values[name].reshape(rows, cols)) for name, _, rows, cols in SMALL_LAYOUT] + [last],
                           axis=0)


def _small_update(all_packs, state, name):
    n = len(SMALL_LAYOUT)

    def body(packs_ref, *refs):
        ins, outs = refs[:3 * n], refs[3 * n:]
        total = packs_ref[0]
        for s in range(1, N_DEV):
            total = total + packs_ref[s]
        for j, (_, at, rows, cols) in enumerate(SMALL_LAYOUT):
            g = total[at:at + rows, :cols]
            w_ref, m_ref, v_ref = ins[3 * j:3 * j + 3]
            outs[4 * j][...] = g
            outs[4 * j + 1][...], outs[4 * j + 2][...], outs[4 * j + 3][...] = _adamw_update(
                w_ref[...], g, m_ref[...], v_ref[...])
        outs[-2][...] = total[LOSS_ROW:LOSS_ROW + 1, :LANES]
        outs[-1][...] = total[SMALL_ROWS:, :]

    shapes = [jax.ShapeDtypeStruct((rows, cols), F32) for _, _, rows, cols in SMALL_LAYOUT for _ in range(4)]
    shapes += [jax.ShapeDtypeStruct((1, LANES), F32), jax.ShapeDtypeStruct((PACK_ROWS - SMALL_ROWS, D_MODEL), F32)]
    operands = [a for item in SMALL_LAYOUT for a in state[item[0]]]
    res = pl.pallas_call(body, name=name, out_shape=shapes, compiler_params=_params())(all_packs, *operands)
    per_name = {item[0]: tuple(res[4 * j:4 * j + 4]) for j, item in enumerate(SMALL_LAYOUT)}
    return per_name, res[-2][0, 0], res[-1]


def kernel(x, p, norm_g, w_in, attn_out_g, dw_w, dw_b, conv_ln_g, conv_ln_b, w_pw, conv_out_g, w_out, ple_norm_g, w_ple_gate, w_ple, final_g, loss_target, m_norm_g, m_w_in, m_attn_out_g, m_dw_w, m_dw_b, m_conv_ln_g, m_conv_ln_b, m_w_pw, m_conv_out_g, m_w_out, m_ple_norm_g, m_w_ple_gate, m_w_ple, m_final_g, v_norm_g, v_w_in, v_attn_out_g, v_dw_w, v_dw_b, v_conv_ln_g, v_conv_ln_b, v_w_pw, v_conv_out_g, v_w_out, v_ple_norm_g, v_w_ple_gate, v_w_ple, v_final_g):
    depth = w_in.shape[0]
    T = x.shape[1]
    given = dict(
        norm_g=norm_g, ple_norm_g=ple_norm_g, final_g=final_g, dw_b=dw_b, conv_ln_g=conv_ln_g, conv_ln_b=conv_ln_b,
        conv_out_g=conv_out_g, attn_out_g=attn_out_g,
        m_norm_g=m_norm_g, m_ple_norm_g=m_ple_norm_g, m_final_g=m_final_g, m_dw_b=m_dw_b, m_conv_ln_g=m_conv_ln_g,
        m_conv_ln_b=m_conv_ln_b, m_conv_out_g=m_conv_out_g, m_attn_out_g=m_attn_out_g,
        v_norm_g=v_norm_g, v_ple_norm_g=v_ple_norm_g, v_final_g=v_final_g, v_dw_b=v_dw_b, v_conv_ln_g=v_conv_ln_g,
        v_conv_ln_b=v_conv_ln_b, v_conv_out_g=v_conv_out_g, v_attn_out_g=v_attn_out_g)
    my_idx = 4 * lax.axis_index("x") + 2 * lax.axis_index("y") + lax.axis_index("c")

    ids = jnp.arange(BLK)
    tri = (ids[:, None] >= ids[None, :]).astype(BF16)
    tri_t = (ids[:, None] <= ids[None, :]).astype(BF16)
    hid = jnp.arange(ATTN_DIM) // HEAD_DIM
    head_mean = ((hid[:, None] == hid[None, :]).astype(F32) / HEAD_DIM).astype(BF16)

    w_names = ("w_in_t", "w_pw", "w_out", "w_gate", "w_ple")
    w_axes = dict(zip(w_names, (0, 0, 0, 0, 1)))
    shards = [dict(zip(w_names, (w_in[l].T.astype(BF16), w_pw[l].astype(BF16), w_out[l].astype(BF16),
                                 w_ple_gate[l].astype(BF16), w_ple[l].astype(BF16)))) for l in range(depth)]
    first = _all_gather([shards[0]["w_in_t"]] + [dw_w[l].T for l in range(depth)], [0] * (1 + depth),
                        "gather_weights_0")
    layers = []
    for l in range(depth):
        layers.append(dict(
            dw_w=first[1 + l].T,
            g_norm=norm_g[l][None], g_attn=jnp.tile(attn_out_g[l], N_HEADS)[None], dw_b=dw_b[l][None],
            ln_g=conv_ln_g[l][None], ln_b=conv_ln_b[l][None], g_conv=conv_out_g[l][None],
            g_ple=ple_norm_g[l][None], p=p[l, 0]))
    layers[0]["w_in_t"] = first[0]

    def rest_of(l, names):
        return [_Ride.gather2(shards[l][n], w_axes[n]) for n in names]

    h = x[0]
    saved = []
    for l, w in enumerate(layers):
        early, late = (w_names[3:], w_names[1:3]) if l == 0 else ((), ())
        (qs, k, v, ug, hn), landed = _prenorm_inproj(h, w["g_norm"], w["w_in_t"], f"inproj_{l}",
                                                     _Ride(rest_of(l, early)))
        w.update(zip(early, landed))
        ahead = [_Ride.gather2(shards[l + 1]["w_in_t"], 0, None, 0, W_IN_ROWS_ON_ATTN)] if l + 1 < depth else []
        own = w_names[1:] if l > 0 else ()
        (o, cs), landed = _attn_fwd(qs, k, v, tri, f"attn_fwd_{l}", _Ride(ahead + rest_of(l, own)))
        w_in_next = landed[:len(ahead)]
        w.update(zip(own, landed[len(ahead):]))
        (conv, c2), landed = _conv_fwd(ug, w["dw_w"], w["dw_b"], w["ln_g"], w["ln_b"], f"conv_fwd_{l}",
                                       _Ride(rest_of(l, late)))
        w.update(zip(late, landed))
        tail = [_Ride.gather2(shards[l + 1]["w_in_t"], 0, a, W_IN_ROWS_ON_ATTN,
                              shards[l + 1]["w_in_t"].shape[0] - W_IN_ROWS_ON_ATTN) for a in w_in_next]
        (h2, h1, ycat, hn2, gate, e, c3), landed = _mix_out_ple(
            o, ug, c2, h, w["p"], head_mean, w["g_attn"], w["g_conv"], w["g_ple"],
            w["w_pw"], w["w_out"], w["w_gate"], w["w_ple"], f"mix_{l}", _Ride(tail))
        if landed:
            layers[l + 1]["w_in_t"] = landed[0]
        saved.append(dict(h=h, qs=qs, k=k, v=v, ug=ug, hn=hn, o=o, cs=cs, conv=conv, c2=c2, h1=h1,
                          ycat=ycat, hn2=hn2, gate=gate, e=e, c3=c3))
        h = h2
    dh, g_final, loss_part = _final_loss(h, loss_target[0], final_g[None], "final_loss")

    small = {}
    dww_parts = [None] * depth
    slots = [dict() for _ in range(depth)]
    g_w_in = None
    pending = []
    for l in reversed(range(depth)):
        w, s = layers[l], saved[l]
        above = pending

        def part(i, above=above, g=g_w_in):
            return [_Ride.scatter(g, 0, above[0], *W_IN_GRAD_PARTS[i])] if above else []

        def scattered(grads, names):
            return [_Ride.scatter(grads[n], w_axes[n]) for n in names]

        (dh1, dh1b, dzg, de, dycat, g_ple_sum), landed = _ple_out_bwd(
            dh, s["gate"], s["e"], s["h1"], w["g_ple"], w["w_gate"], w["w_out"], f"ple_bwd_{l}", _Ride(part(1)))
        above[:1] = landed
        (do, dga, dgc, dc3, dconv, sums), landed = _branch_bwd(
            dycat, s["o"], s["ug"], s["c3"], s["conv"], head_mean, w["g_attn"], w["g_conv"],
            w["ln_g"], w["ln_b"], w["w_pw"], f"branch_bwd_{l}", _Ride(part(2)))
        above[:1] = landed
        grads = dict(
            w_pw=_weight_grad([s["c2"]], dc3, f"grad_w_pw_{l}")[0],
            w_out=_weight_grad([s["ycat"]], dh1b, f"grad_w_out_{l}")[0],
            w_gate=_weight_grad([s["hn2"]], dzg, f"grad_w_gate_{l}")[0],
            w_ple=_weight_grad([w["p"].astype(BF16)], de, f"grad_w_ple_{l}", tk=PLE_DIM)[0])
        (dcv, dcg, dww), landed = _conv_bwd(dconv, s["ug"], w["dw_w"], f"conv_bwd_{l}", _Ride(part(3)))
        if above:
            slots[l + 1]["w_in_t"] = landed[0]
        (dq, dk, dv), landed = _attn_bwd(s["qs"], s["k"], s["v"], do, s["cs"], tri, tri_t, f"attn_bwd_{l}",
                                         _Ride(scattered(grads, w_names[1:])))
        slots[l].update(zip(w_names[1:], landed))
        du = [dq, dk, dv, dga, dcv, dcg, dgc]
        g_w_in_here, _ = _weight_grad(du, s["hn"], f"grad_w_in_{l}")
        if l == 0:
            tail = [_Ride.scatter_chips(_pair_reduce(g_w_in_here, f"pair_reduce_w_in_{l}"))]
        else:
            tail = [_Ride.scatter(g_w_in_here, 0, None, *W_IN_GRAD_PARTS[0])]
        (dh, g_norm_sum), landed = _inproj_bwd(du, w["w_in_t"], s["h"], dh1, w["g_norm"], f"inproj_bwd_{l}",
                                               _Ride(tail))
        if l == 0:
            slots[l]["w_in_t"] = landed[0]
        pending = landed if l > 0 else []
        g_w_in = g_w_in_here
        small[l] = dict(norm_g=g_norm_sum, ple_norm_g=g_ple_sum, attn_out_g=sums[0].reshape(N_HEADS, HEAD_DIM).sum(0),
                        conv_out_g=sums[1], conv_ln_g=sums[2], conv_ln_b=sums[3], dw_b=sums[4])
        dww_parts[l] = dww[:CONV_WIDTH]
    slots = [[sl[n] for n in w_names] for sl in slots]
    grad_x = dh[None]

    sums_of = {name: jnp.stack([small[l][name].reshape(-1) for l in range(depth)]) for name in small[0]}
    sums_of["final_g"] = g_final
    pack = jnp.concatenate([_pack_small(sums_of, scalar=loss_part[0, 0]), jnp.concatenate(dww_parts, axis=1),
                            jnp.zeros((PACK_ROWS - SMALL_ROWS - CONV_WIDTH, D_MODEL), F32)], axis=0)
    (all_packs,) = _all_gather([pack], [0], "gather_small_grads")
    state = {name: [given[pre + name].reshape(rows, cols) for pre in ("", "m_", "v_")]
             for name, _, rows, cols in SMALL_LAYOUT}
    updated, loss, dww_sum = _small_update(all_packs.reshape(N_DEV, PACK_ROWS, D_MODEL), state, "update_small")
    res = {kind: {name: val[k].reshape(given[name].shape) for name, val in updated.items()}
           for k, kind in enumerate("gdmv")}
    dww_full = dww_sum[:CONV_WIDTH].reshape(CONV_WIDTH, depth, CONV_DIM).transpose(1, 0, 2)
    g_dw_w = lax.dynamic_slice_in_dim(dww_full, my_idx * (CONV_DIM // N_DEV), CONV_DIM // N_DEV, axis=2)

    swap = lambda a: a.transpose(0, 2, 1)
    state = {"w_in": (w_in, m_w_in, v_w_in), "w_pw": (w_pw, m_w_pw, v_w_pw), "w_out": (w_out, m_w_out, v_w_out),
             "w_ple_gate": (w_ple_gate, m_w_ple_gate, v_w_ple_gate), "w_ple": (w_ple, m_w_ple, v_w_ple)}
    for at, name in enumerate(state):
        wv, mv, vv = [swap(a) for a in state[name]] if name == "w_in" else state[name]
        out = _sum_adamw([slots[l][at] for l in range(depth)], wv, mv, vv, f"adamw_{name}")
        out = [swap(a) for a in out] if name == "w_in" else out
        res["g"][name], res["d"][name], res["m"][name], res["v"][name] = out
    flat = lambda a: a.reshape(-1, a.shape[-1])
    res["g"]["dw_w"] = g_dw_w
    res["d"]["dw_w"], res["m"]["dw_w"], res["v"]["dw_w"] = [
        a.reshape(dw_w.shape) for a in _adamw(flat(dw_w), flat(g_dw_w), flat(m_dw_w), flat(v_dw_w), "adamw_dw_w")]

    order = ["norm_g", "w_in", "attn_out_g", "dw_w", "dw_b", "conv_ln_g", "conv_ln_b", "w_pw", "conv_out_g",
             "w_out", "ple_norm_g", "w_ple_gate", "w_ple", "final_g"]
    return (loss, grad_x, *[res["g"][n] for n in order], *[res["d"][n] for n in order],
            *[res["m"][n] for n in order], *[res["v"][n] for n in order])
```

```python
import functools

import jax
import jax.numpy as jnp
from jax import lax
from jax.experimental import pallas as pl
from jax.experimental.pallas import tpu as pltpu

F32 = jnp.float32
BF16 = jnp.bfloat16
MESH = pl.DeviceIdType.MESH

N_DEV = 8
D_MODEL = 1024
ATTN_DIM = 512
CONV_DIM = 512
HEAD_DIM = 64
N_HEADS = 8
CONV_WIDTH = 31
PLE_DIM = 256
CHUNK = 512
N_CHUNK = 7
EPS = 1e-6
ADAM_LR = 0.001
ADAM_B1 = 0.9
ADAM_B2 = 0.999
ADAM_EPS = 1e-08
ADAM_WD = 0.01
ADAM_STEP = 10

LANES = 128
BLK = 256
ATT_COLS = 4
CHAIN_GROUP = 4
SOFTPLUS_LINEAR_AT = 20.0
DEAD_AT = 110.0
FIRST_BLOCK_LANE = HEAD_DIM - 1
TM = 512
HALO = 32
SUBLANES = 8
CONV_ROWS = 32
ADAMW_ROWS = (224, 128, 64)
VMEM_LIMIT = 56 * 1024 * 1024
SMALL_ROWS = 16
SMALL_LAYOUT = (("norm_g", 0, 2, D_MODEL), ("ple_norm_g", 2, 2, D_MODEL), ("final_g", 4, 1, D_MODEL),
                ("dw_b", 5, 2, CONV_DIM), ("conv_ln_g", 7, 2, CONV_DIM), ("conv_ln_b", 9, 2, CONV_DIM),
                ("conv_out_g", 11, 2, CONV_DIM), ("attn_out_g", 13, 2, HEAD_DIM))
LOSS_ROW = 15
W_IN_ROWS_ON_ATTN = 288
W_IN_GRAD_PARTS = ((0, 112), (112, 96), (208, 64), (272, 176))
PACK_ROWS = 48


def _nn(a, b):
    return lax.dot_general(a, b, (((1,), (0,)), ((), ())), preferred_element_type=F32)


def _nt(a, b):
    return lax.dot_general(a, b, (((1,), (1,)), ((), ())), preferred_element_type=F32)


def _tn(a, b):
    return lax.dot_general(a, b, (((0,), (0,)), ((), ())), preferred_element_type=F32)


def _split(x):
    hi = x.astype(BF16)
    lo = (x - hi.astype(F32)).astype(BF16)
    return hi, lo


def _dot_hilo(x, m):
    hi, lo = _split(x)
    return _nn(hi, m) + _nn(lo, m)


def _sigmoid(x):
    return jax.nn.sigmoid(x)


def _dsilu(x, s):
    return s * (1.0 + x * (1.0 - s))


def _params(n_grid=0, vmem=VMEM_LIMIT):
    sem = ("arbitrary",) * n_grid if n_grid else None
    return pltpu.CompilerParams(dimension_semantics=sem, vmem_limit_bytes=vmem)


def _rows(tm, cols, col=0):
    return pl.BlockSpec((tm, cols), lambda i: (i, col))


def _whole(shape):
    zeros = (0,) * len(shape)
    return pl.BlockSpec(shape, lambda *_: zeros)


def _my_position():
    return lax.axis_index("x"), lax.axis_index("y"), lax.axis_index("c")


def _block(ref, axis, idx, size):
    start = pl.multiple_of(idx * size, size)
    if axis == 0:
        return ref.at[pl.ds(start, size), :]
    return ref.at[:, pl.ds(start, size)]


def _all_gather(shards, axes, name):
    n = len(shards)
    sizes = [s.shape[a] for s, a in zip(shards, axes)]

    def full_shape(s, a):
        shape = list(s.shape)
        shape[a] *= N_DEV
        return jax.ShapeDtypeStruct(tuple(shape), s.dtype)

    def body(*refs):
        ins, outs = refs[:n], refs[n:2 * n]
        send_sems, recv_sems, local_sems = refs[2 * n:]
        x, y, c = _my_position()
        me, sibling = (x, y, c), (x, y, 1 - c)
        chips = [(1 - x, y), (x, 1 - y), (1 - x, 1 - y)]

        def place(i, dev):
            return _block(outs[i], axes[i], 4 * dev[0] + 2 * dev[1] + dev[2], sizes[i])

        def copy(k, i, dev, to, src=None):
            return pltpu.make_async_remote_copy(
                src_ref=place(i, dev) if src is None else src, dst_ref=place(i, dev),
                send_sem=send_sems.at[k, i], recv_sem=recv_sems.at[k, i],
                device_id=to, device_id_type=MESH)

        mine = [pltpu.make_async_copy(ins[i], place(i, me), local_sems.at[i]) for i in range(n)]
        for cp in mine:
            cp.start()
        first = [copy(0, i, me, sibling, src=ins[i]) for i in range(n)]
        for j, chip in enumerate(chips):
            first += [copy(1 + j, i, me, (*chip, c), src=ins[i]) for i in range(n)]
        for cp in first:
            cp.start()
        passed = []
        for j, chip in enumerate(chips):
            for i in range(n):
                copy(1 + j, i, (*chip, c), me).wait_recv()
            hop = [copy(4 + j, i, (*chip, c), sibling) for i in range(n)]
            for cp in hop:
                cp.start()
            passed += hop
        for i in range(n):
            copy(0, i, sibling, me).wait_recv()
        for j, chip in enumerate(chips):
            for i in range(n):
                copy(4 + j, i, (*chip, 1 - c), me).wait_recv()
        for cp in first + passed:
            cp.wait_send()
        for cp in mine:
            cp.wait()

    any_spec = pl.BlockSpec(memory_space=pl.ANY)
    return pl.pallas_call(
        body, name=name,
        out_shape=[full_shape(s, a) for s, a in zip(shards, axes)],
        in_specs=[any_spec] * n, out_specs=[any_spec] * n,
        scratch_shapes=[pltpu.SemaphoreType.DMA((7, n)), pltpu.SemaphoreType.DMA((7, n)),
                        pltpu.SemaphoreType.DMA((n,))],
    )(*shards)


def _pair_reduce(g, name):
    n_chips = N_DEV // 2
    R, C = g.shape[0] // N_DEV, g.shape[1]

    def body(g_ref, out_ref, mine_ref, theirs_ref, send_sems, recv_sems, local_sems):
        x, y, c = _my_position()
        block = lambda d: g_ref.at[pl.ds(pl.multiple_of(d * R, 16), R), :]
        sends = [pltpu.make_async_remote_copy(
            src_ref=block(2 * j + 1 - c), dst_ref=theirs_ref.at[j], send_sem=send_sems.at[j],
            recv_sem=recv_sems.at[j], device_id=(x, y, 1 - c), device_id_type=MESH) for j in range(n_chips)]
        own = [pltpu.make_async_copy(block(2 * j + c), mine_ref.at[j], local_sems.at[j]) for j in range(n_chips)]
        for cp in sends + own:
            cp.start()
        for j in range(n_chips):
            own[j].wait()
            sends[j].wait_recv()
            out_ref[j] = (mine_ref[j].astype(F32) + theirs_ref[j].astype(F32)).astype(g.dtype)
        for cp in sends:
            cp.wait_send()

    half = pltpu.VMEM((n_chips, R, C), g.dtype)
    sems = pltpu.SemaphoreType.DMA((n_chips,))
    return pl.pallas_call(
        body, name=name, out_shape=jax.ShapeDtypeStruct((n_chips, R, C), g.dtype),
        in_specs=[pl.BlockSpec(memory_space=pl.ANY)], out_specs=pl.BlockSpec(memory_space=pltpu.VMEM),
        scratch_shapes=[half, half, sems, sems, sems], compiler_params=_params(),
    )(g)


class _Ride:
    def __init__(self, parts):
        self.parts = [p for p in parts if p is not None]

    @staticmethod
    def gather(src, axis, land=None, lo=0, n=None):
        return ("gather", src, land, axis, lo, src.shape[axis] if n is None else n)

    @staticmethod
    def gather2(src, axis, land=None, lo=0, n=None):
        return ("gather2", src, land, axis, lo, src.shape[axis] if n is None else n)

    @staticmethod
    def scatter(src, axis, land=None, lo=0, n=None):
        return ("scatter", src, land, axis, lo, src.shape[axis] // N_DEV if n is None else n)

    @staticmethod
    def scatter_chips(chip_sums):
        return ("scatter_chips", chip_sums, None, 0, 0, chip_sums.shape[1])

    def arrays(self):
        return [p[1] for p in self.parts] + [p[2] for p in self.parts if p[2] is not None]

    def out_shapes(self):
        out = []
        for kind, src, _, axis, _, _ in self.parts:
            shape = list(src.shape)
            if kind in ("gather", "gather2"):
                shape[axis] *= N_DEV
            elif kind == "scatter_chips":
                pass
            else:
                shape[axis] //= N_DEV
                shape = [N_DEV] + shape
            out.append(jax.ShapeDtypeStruct(tuple(shape), src.dtype))
        return out

    def aliases(self, n_in, n_out):
        m, out = len(self.parts), {}
        for j, p in enumerate(self.parts):
            if p[2] is not None:
                out[n_in + m + len(out)] = n_out + j
        return out

    def scratch(self):
        m = len(self.parts)
        return [pltpu.SemaphoreType.DMA((N_DEV - 1, m)), pltpu.SemaphoreType.DMA((N_DEV - 1, m)),
                pltpu.SemaphoreType.DMA((m,))]

    def _copies(self, src_refs, land_refs, sems):
        send_sems, recv_sems, local_sems = sems
        x, y, c = _my_position()
        my_idx = 4 * x + 2 * y + c
        own, sends, relays, lands = [], [], [], []
        for j, (kind, src, _, axis, lo, n) in enumerate(self.parts):
            if kind == "scatter_chips":
                for k in (0, 2, 4, 6):
                    px, py = (1 - x if k & 4 else x), (1 - y if k & 2 else y)
                    a, b = src_refs[j].at[2 * px + py], land_refs[j].at[2 * x + y]
                    if k == 0:
                        own.append(pltpu.make_async_copy(a, b, local_sems.at[j]))
                        continue
                    mk = lambda dst, a=a, k=k, j=j, to=(px, py, c): pltpu.make_async_remote_copy(
                        src_ref=a, dst_ref=dst, send_sem=send_sems.at[k - 1, j], recv_sem=recv_sems.at[k - 1, j],
                        device_id=to, device_id_type=MESH)
                    sends.append(mk(b))
                    lands.append(mk(land_refs[j].at[2 * px + py]))
                continue
            size = src.shape[axis] if kind in ("gather", "gather2") else src.shape[axis] // N_DEV
            align = 16 if axis == 0 else LANES

            def rows(ref, idx, lead=None, axis=axis, lo=lo, n=n, size=size, align=align):
                at = pl.ds(pl.multiple_of(idx * size + lo, align), n)
                where = (at, slice(None)) if axis == 0 else (slice(None), at)
                return ref.at[where] if lead is None else ref.at[(lead, *where)]

            def in_shard(ref):
                return rows(ref, 0)

            def in_slot(ref, s):
                return rows(ref, 0, lead=s)

            if kind == "gather2":
                chips = [(1 - x, y), (x, 1 - y), (1 - x, 1 - y)]
                place = lambda px, py, pc: rows(land_refs[j], 4 * px + 2 * py + pc)

                def copy(i, a, dst, to, j=j):
                    return pltpu.make_async_remote_copy(
                        src_ref=a, dst_ref=dst, send_sem=send_sems.at[i, j], recv_sem=recv_sems.at[i, j],
                        device_id=to, device_id_type=MESH)

                mine = in_shard(src_refs[j])
                own.append(pltpu.make_async_copy(mine, place(x, y, c), local_sems.at[j]))
                sends.append(copy(0, mine, place(x, y, c), (x, y, 1 - c)))
                lands.append(copy(0, mine, place(x, y, 1 - c), (x, y, 1 - c)))
                for i, (px, py) in enumerate(chips):
                    sends.append(copy(1 + i, mine, place(x, y, c), (px, py, c)))
                    relays.append((copy(1 + i, mine, place(px, py, c), (px, py, c)),
                                   copy(4 + i, place(px, py, c), place(px, py, c), (x, y, 1 - c))))
                    lands.append(copy(4 + i, mine, place(px, py, 1 - c), (x, y, 1 - c)))
                continue
            for k in range(N_DEV):
                px = 1 - x if k & 4 else x
                py = 1 - y if k & 2 else y
                pc = 1 - c if k & 1 else c
                peer_idx = 4 * px + 2 * py + pc
                if kind == "gather":
                    a, b, landed = in_shard(src_refs[j]), rows(land_refs[j], my_idx), rows(land_refs[j], peer_idx)
                else:
                    a, b, landed = rows(src_refs[j], peer_idx), in_slot(land_refs[j], my_idx), in_slot(land_refs[j], peer_idx)
                if k == 0:
                    own.append(pltpu.make_async_copy(a, b, local_sems.at[j]))
                    continue
                mk = lambda dst, a=a, k=k, j=j, to=(px, py, pc): pltpu.make_async_remote_copy(
                    src_ref=a, dst_ref=dst, send_sem=send_sems.at[k - 1, j], recv_sem=recv_sems.at[k - 1, j],
                    device_id=to, device_id_type=MESH)
                sends.append(mk(b))
                lands.append(mk(landed))
        return own, sends, relays, lands

    @property
    def relayed(self):
        return any(p[0] == "gather2" for p in self.parts)

    def start(self, src_refs, land_refs, sems):
        own, sends, _, _ = self._copies(src_refs, land_refs, sems)
        for cp in own + sends:
            cp.start()

    def relay(self, src_refs, land_refs, sems):
        for arrival, onward in self._copies(src_refs, land_refs, sems)[2]:
            arrival.wait_recv()
            onward.start()

    def wait(self, src_refs, land_refs, sems):
        own, sends, relays, lands = self._copies(src_refs, land_refs, sems)
        for cp in lands:
            cp.wait_recv()
        for cp in sends + [onward for _, onward in relays]:
            cp.wait_send()
        for cp in own:
            cp.wait()


def _call(body, *, name, grid, in_specs, out_specs, out_shape, args, scratch_shapes=(), ride=None):
    in_specs, out_specs, out_shape = list(in_specs), list(out_specs), list(out_shape)
    n_in, n_out, n_sc = len(in_specs), len(out_specs), len(scratch_shapes)
    if ride is None or not ride.parts:
        res = pl.pallas_call(body, name=name, grid=grid, in_specs=in_specs, out_specs=out_specs,
                             out_shape=out_shape, scratch_shapes=list(scratch_shapes),
                             compiler_params=_params(len(grid)))(*args)
        return list(res), []
    extra, m = ride.arrays(), len(ride.parts)

    def riding(*refs):
        a = n_in + len(extra)
        b = a + n_out
        srcs, lands, sems = refs[n_in:n_in + m], refs[b:b + m], refs[b + m + n_sc:]
        at = [pl.program_id(d) for d in range(len(grid))]

        @pl.when(functools.reduce(jnp.logical_and, [i == 0 for i in at]))
        def _():
            ride.start(srcs, lands, sems)

        if ride.relayed:
            step, n_steps = at[0], 1
            for i, g in zip(at[1:], grid[1:]):
                step = step * g + i
            for g in grid:
                n_steps *= g
            assert n_steps >= 2, "a two-level ride needs a grid step after the first"

            @pl.when(step == n_steps - 1)
            def _():
                ride.relay(srcs, lands, sems)

        body(*refs[:n_in], *refs[a:b], *refs[b + m:b + m + n_sc])

        @pl.when(functools.reduce(jnp.logical_and, [i == g - 1 for i, g in zip(at, grid)]))
        def _():
            ride.wait(srcs, lands, sems)

    hbm = pl.BlockSpec(memory_space=pl.ANY)
    res = pl.pallas_call(
        riding, name=name, grid=grid, in_specs=in_specs + [hbm] * len(extra), out_specs=out_specs + [hbm] * m,
        out_shape=out_shape + ride.out_shapes(), scratch_shapes=list(scratch_shapes) + ride.scratch(),
        input_output_aliases=ride.aliases(n_in, n_out), compiler_params=_params(len(grid)),
    )(*args, *extra)
    return list(res[:n_out]), list(res[n_out:])


def _prenorm_inproj(h, gain, w_in_t, name, ride=None):
    T = h.shape[0]

    def body(h_ref, g_ref, w_ref, q_ref, k_ref, v_ref, ug_ref, hn_ref):
        hv = h_ref[...]
        r = lax.rsqrt(jnp.mean(hv * hv, axis=-1, keepdims=True) + EPS)
        hn = (hv * r * g_ref[...]).astype(BF16)
        hn_ref[...] = hn
        for j in range(N_CHUNK):
            u = _nt(hn, w_ref[j * CHUNK:(j + 1) * CHUNK, :])
            if j == 0:
                q_ref[...] = (u * (HEAD_DIM ** -0.5)).astype(BF16)
            elif j == 1:
                k_ref[...] = u.astype(BF16)
            elif j == 2:
                v_ref[...] = u.astype(BF16)
            else:
                ug_ref[:, (j - 3) * CHUNK:(j - 2) * CHUNK] = u.astype(BF16)

    act = jax.ShapeDtypeStruct((T, CHUNK), BF16)
    return _call(
        body, name=name, grid=(T // TM,),
        in_specs=[_rows(TM, D_MODEL), _whole((1, D_MODEL)), _whole((N_CHUNK * CHUNK, D_MODEL))],
        out_specs=[_rows(TM, CHUNK)] * 3 + [_rows(TM, 4 * CHUNK), _rows(TM, D_MODEL)],
        out_shape=[act, act, act, jax.ShapeDtypeStruct((T, 4 * CHUNK), BF16),
                   jax.ShapeDtypeStruct((T, D_MODEL), BF16)],
        args=(h, gain, w_in_t,), ride=ride)


def _softplus_parts(z):
    ez = jnp.exp(jnp.minimum(z, SOFTPLUS_LINEAR_AT))
    t = 1.0 + ez
    return ez * pl.reciprocal(t, approx=True), jnp.where(z > SOFTPLUS_LINEAR_AT, z, jnp.log(t))


def _attn_fwd(qs, k, v, tri, name, ride=None):
    T = qs.shape[0]
    assert T // BLK <= FIRST_BLOCK_LANE, "one lane per key block below the lane of the first block"
    width = LANES * ATT_COLS
    chains = [(c, half) for c in range(ATT_COLS) for half in range(2)]

    def body(q_ref, k_ref, v_ref, m_ref, o_ref, cs_ref):
        qi = pl.program_id(1)
        lane = lax.broadcasted_iota(jnp.int32, (BLK, LANES), 1)
        first = lane < HEAD_DIM
        causal = (lax.broadcasted_iota(jnp.int32, (BLK, BLK), 1)
                  < lax.broadcasted_iota(jnp.int32, (BLK, BLK), 0))
        tri_m = m_ref[...]
        qh = {}
        for c in range(ATT_COLS):
            q = q_ref[:, c * LANES:(c + 1) * LANES]
            zero = jnp.zeros_like(q)
            qh[c, 0], qh[c, 1] = jnp.where(first, q, zero), jnp.where(first, zero, q)

        def step(kb, state, masked):
            carries, accs, cvals = state
            start = pl.multiple_of(kb * BLK, BLK)
            kblk = [k_ref[pl.ds(start, BLK), c * LANES:(c + 1) * LANES] for c in range(ATT_COLS)]
            vblk = [v_ref[pl.ds(start, BLK), c * LANES:(c + 1) * LANES] for c in range(ATT_COLS)]
            carries, accs, cvals = list(carries), list(accs), list(cvals)
            for g0 in range(0, len(chains), CHAIN_GROUP):
                ids = range(g0, g0 + CHAIN_GROUP)
                z = [_nt(qh[chains[n]], kblk[chains[n][0]]) for n in ids]
                sp = [_softplus_parts(zi)[1] for zi in z]
                if masked:
                    sp = [jnp.where(causal, s, 0.0) for s in sp]
                incl = [_dot_hilo(s, tri_m) for s in sp]
                a = [jnp.exp(zi - ii - carries[n]) for n, zi, ii in zip(ids, z, incl)]
                if masked:
                    a = [jnp.where(causal, ai, 0.0) for ai in a]
                for n, ai, ii in zip(ids, a, incl):
                    c, half = chains[n]
                    zero = jnp.zeros_like(vblk[c])
                    vh = jnp.where(first, vblk[c], zero) if half == 0 else jnp.where(first, zero, vblk[c])
                    accs[c] = accs[c] + _nn(ai.astype(BF16), vh)
                    cvals[c] = jnp.where(lane == kb + HEAD_DIM * half, carries[n], cvals[c])
                    carries[n] = carries[n] + ii[:, 0:1]
            return tuple(carries), tuple(accs), tuple(cvals)

        zeros = tuple(jnp.zeros((BLK, LANES), F32) for _ in range(ATT_COLS))
        state = (tuple(jnp.zeros((BLK, 1), F32) for _ in chains), zeros, zeros)
        state = step(qi, state, True)

        def reaches_further(st):
            it, (carries, _, _) = st
            least = functools.reduce(jnp.minimum, carries)
            return jnp.logical_and(it < qi, jnp.min(least) < DEAD_AT)

        done, state = lax.while_loop(reaches_further, lambda st: (st[0] + 1, step(qi - 1 - st[0], st[1], False)),
                                     (jnp.int32(0), state))
        first_block = (qi - done).astype(F32)
        for c in range(ATT_COLS):
            o_ref[:, c * LANES:(c + 1) * LANES] = state[1][c]
            cs_ref[:, c * LANES:(c + 1) * LANES] = jnp.where(lane == FIRST_BLOCK_LANE, first_block, state[2][c])

    blk = pl.BlockSpec((BLK, width), lambda j, i: (i, j))
    col = pl.BlockSpec((T, width), lambda j, i: (0, j))
    out = jax.ShapeDtypeStruct((T, ATTN_DIM), F32)
    return _call(
        body, name=name, grid=(ATTN_DIM // width, T // BLK),
        in_specs=[blk, col, col, _whole((BLK, BLK))],
        out_specs=[blk, blk], out_shape=[out, out],
        args=(qs, k, v, tri,), ride=ride)


def _shifted_copies(pad_ref, sh_ref):
    rows = sh_ref.shape[1]
    for b in range(SUBLANES):
        sh_ref[b] = pad_ref[b:b + rows, :]


def _shift_of(offset):
    return offset % SUBLANES, offset - offset % SUBLANES


def _conv_fwd(ug, dw_w, dw_b, ln_g, ln_b, name, ride=None):
    T = ug.shape[0]
    per = TM // HALO

    def body(cv_ref, cg_ref, cvh_ref, cgh_ref, w_ref, b_ref, g_ref, beta_ref, conv_ref, c2_ref, pad_ref, sh_ref):
        i = pl.program_id(0)
        halo = cvh_ref[...].astype(F32) * _sigmoid(cgh_ref[...].astype(F32))
        pad_ref[0:HALO, :] = jnp.where(i == 0, 0.0, halo)
        pad_ref[HALO:HALO + TM, :] = cv_ref[...].astype(F32) * _sigmoid(cg_ref[...].astype(F32))
        pad_ref[HALO + TM:, :] = jnp.zeros((SUBLANES, CONV_DIM), F32)
        _shifted_copies(pad_ref, sh_ref)
        taps = [w_ref[t:t + 1, :] for t in range(CONV_WIDTH)]

        def rows(j, _):
            r = pl.multiple_of(j * CONV_ROWS, CONV_ROWS)
            acc = jnp.zeros((CONV_ROWS, CONV_DIM), F32) + b_ref[...]
            for t in range(CONV_WIDTH):
                b, a = _shift_of(HALO - (CONV_WIDTH - 1) + t)
                acc = acc + taps[t] * sh_ref[b, pl.ds(r + a, CONV_ROWS), :]
            conv_ref[pl.ds(r, CONV_ROWS), :] = acc
            return 0

        lax.fori_loop(0, TM // CONV_ROWS, rows, 0)
        acc = conv_ref[...]
        mu = jnp.mean(acc, axis=-1, keepdims=True)
        xc = acc - mu
        rs = lax.rsqrt(jnp.mean(xc * xc, axis=-1, keepdims=True) + EPS)
        ln = xc * rs * g_ref[...] + beta_ref[...]
        c2_ref[...] = (ln * _sigmoid(ln)).astype(BF16)

    prev = lambda col: pl.BlockSpec((HALO, CHUNK), lambda i: (jnp.maximum(i * per - 1, 0), col))
    vec = _whole((1, CONV_DIM))
    return _call(
        body, name=name, grid=(T // TM,),
        in_specs=[_rows(TM, CHUNK, 1), _rows(TM, CHUNK, 2), prev(1), prev(2),
                  _whole((CONV_WIDTH, CONV_DIM)), vec, vec, vec],
        out_specs=[_rows(TM, CONV_DIM), _rows(TM, CONV_DIM)],
        out_shape=[jax.ShapeDtypeStruct((T, CONV_DIM), F32), jax.ShapeDtypeStruct((T, CONV_DIM), BF16)],
        scratch_shapes=[pltpu.VMEM((TM + HALO + SUBLANES, CONV_DIM), F32),
                        pltpu.VMEM((SUBLANES, TM + HALO, CONV_DIM), F32)],
        args=(ug, ug, ug, ug, dw_w, dw_b, ln_g, ln_b,), ride=ride)


def _mix_out_ple(o, ug, c2, h, p, head_mean, g_attn, g_conv, g_ple, w_pw, w_out, w_gate, w_ple, name, ride=None,
                 loss_head=None):
    T = h.shape[0]
    n_in = 14 + (2 if loss_head else 0)

    def body(*refs):
        (o_ref, ga_ref, gc_ref, c2_ref, h_ref, p_ref, hm_ref, gao_ref, gco_ref, gpn_ref,
         wpw_ref, wout_ref, wg_ref, wple_ref) = refs[:14]
        h2_ref, h1_ref, ycat_ref, hn2_ref, gate_ref, e_ref, c3_ref = refs[n_in:n_in + 7]
        ov = o_ref[...]
        rh = lax.rsqrt(_nn((ov * ov).astype(BF16), hm_ref[...]) + EPS)
        ga = ga_ref[...].astype(F32)
        ya = (ov * rh * gao_ref[...] * (ga * _sigmoid(ga))).astype(BF16)
        c3 = _nn(c2_ref[...], wpw_ref[...])
        c3_ref[...] = c3
        rc = lax.rsqrt(jnp.mean(c3 * c3, axis=-1, keepdims=True) + EPS)
        gc = gc_ref[...].astype(F32)
        yc = (c3 * rc * gco_ref[...] * (gc * _sigmoid(gc))).astype(BF16)
        ycat_ref[:, :ATTN_DIM] = ya
        ycat_ref[:, ATTN_DIM:] = yc
        h1 = h_ref[...] + _nn(ya, wout_ref[:ATTN_DIM, :]) + _nn(yc, wout_ref[ATTN_DIM:, :])
        h1_ref[...] = h1
        r1 = lax.rsqrt(jnp.mean(h1 * h1, axis=-1, keepdims=True) + EPS)
        hn2 = (h1 * r1 * gpn_ref[...]).astype(BF16)
        hn2_ref[...] = hn2
        gate = _sigmoid(_nn(hn2, wg_ref[...]))
        e = _nn(p_ref[...].astype(BF16), wple_ref[...])
        gate_ref[...] = gate
        e_ref[...] = e
        h2 = h1 + e * gate
        if loss_head is None:
            h2_ref[...] = h2
            return
        t_ref, g_ref = refs[14:16]
        gsum_ref, loss_ref = refs[n_in + 7:n_in + 9]

        @pl.when(pl.program_id(0) == 0)
        def _():
            gsum_ref[...] = jnp.zeros_like(gsum_ref)
            loss_ref[...] = jnp.zeros_like(loss_ref)

        r = lax.rsqrt(jnp.mean(h2 * h2, axis=-1, keepdims=True) + EPS)
        xh = h2 * r
        diff = xh * g_ref[...] - t_ref[...]
        loss_ref[...] += 0.5 * jnp.sum(jnp.mean(diff * diff, axis=-1, keepdims=True), axis=0, keepdims=True)
        dy = diff * (1.0 / D_MODEL)
        gsum_ref[...] += jnp.sum(dy * xh, axis=0, keepdims=True)
        dxh = dy * g_ref[...]
        h2_ref[...] = r * (dxh - xh * jnp.mean(dxh * xh, axis=-1, keepdims=True))

    f32 = lambda cols: jax.ShapeDtypeStruct((T, cols), F32)
    bf = lambda cols: jax.ShapeDtypeStruct((T, cols), BF16)
    head = bool(loss_head)
    return _call(
        body, name=name, grid=(T // TM,),
        in_specs=[_rows(TM, ATTN_DIM), _rows(TM, CHUNK, 0), _rows(TM, CHUNK, 3), _rows(TM, CONV_DIM),
                  _rows(TM, D_MODEL), _rows(TM, PLE_DIM), _whole((ATTN_DIM, ATTN_DIM)),
                  _whole((1, ATTN_DIM)), _whole((1, CONV_DIM)), _whole((1, D_MODEL)),
                  _whole((CONV_DIM, CONV_DIM)), _whole((D_MODEL, D_MODEL)), _whole((D_MODEL, D_MODEL)),
                  _whole((PLE_DIM, D_MODEL))] + [_rows(TM, D_MODEL), _whole((1, D_MODEL))] * head,
        out_specs=[_rows(TM, D_MODEL), _rows(TM, D_MODEL), _rows(TM, D_MODEL), _rows(TM, D_MODEL),
                   _rows(TM, D_MODEL), _rows(TM, D_MODEL), _rows(TM, CONV_DIM)]
        + [_whole((1, D_MODEL)), _whole((1, LANES))] * head,
        out_shape=[f32(D_MODEL), f32(D_MODEL), bf(D_MODEL), bf(D_MODEL), f32(D_MODEL), f32(D_MODEL),
                   f32(CONV_DIM)] + [jax.ShapeDtypeStruct((1, D_MODEL), F32), jax.ShapeDtypeStruct((1, LANES), F32)] * head,
        args=(o, ug, ug, c2, h, p, head_mean, g_attn, g_conv, g_ple, w_pw, w_out, w_gate, w_ple, *(loss_head or ())),
        ride=ride)


def _ple_out_bwd(dh2, gate, e, h1, g_ple, w_gate, w_out, name, ride=None):
    T = dh2.shape[0]

    def body(dh2_ref, gate_ref, e_ref, h1_ref, gpn_ref, wg_ref, wout_ref,
             dh1_ref, dh1b_ref, dzg_ref, de_ref, dycat_ref, gsum_ref):
        @pl.when(pl.program_id(0) == 0)
        def _():
            gsum_ref[...] = jnp.zeros_like(gsum_ref)

        dh2v = dh2_ref[...]
        gate = gate_ref[...]
        de_ref[...] = (dh2v * gate).astype(BF16)
        dzg = (dh2v * e_ref[...] * gate * (1.0 - gate)).astype(BF16)
        dzg_ref[...] = dzg
        dhn2 = _nt(dzg, wg_ref[...])
        h1 = h1_ref[...]
        r1 = lax.rsqrt(jnp.mean(h1 * h1, axis=-1, keepdims=True) + EPS)
        xh = h1 * r1
        gsum_ref[...] += jnp.sum(dhn2 * xh, axis=0, keepdims=True)
        dxh = dhn2 * gpn_ref[...]
        dh1 = dh2v + r1 * (dxh - xh * jnp.mean(dxh * xh, axis=-1, keepdims=True))
        dh1_ref[...] = dh1
        dh1b = dh1.astype(BF16)
        dh1b_ref[...] = dh1b
        dycat_ref[...] = _nt(dh1b, wout_ref[...])

    f32 = jax.ShapeDtypeStruct((T, D_MODEL), F32)
    bf = jax.ShapeDtypeStruct((T, D_MODEL), BF16)
    full = _rows(TM, D_MODEL)
    return _call(
        body, name=name, grid=(T // TM,),
        in_specs=[full, full, full, full, _whole((1, D_MODEL)), _whole((D_MODEL, D_MODEL)),
                  _whole((D_MODEL, D_MODEL))],
        out_specs=[full, full, full, full, full, _whole((1, D_MODEL))],
        out_shape=[f32, bf, bf, bf, f32, jax.ShapeDtypeStruct((1, D_MODEL), F32)],
        args=(dh2, gate, e, h1, g_ple, w_gate, w_out,), ride=ride)


def _branch_bwd(dycat, o, ug, c3, conv, head_mean, g_attn, g_conv, ln_g, ln_b, w_pw, name, ride=None):
    T = o.shape[0]

    def body(dya_ref, dyc_ref, o_ref, ga_ref, gc_ref, c3_ref, conv_ref, hm_ref, gao_ref, gco_ref,
             lng_ref, lnb_ref, wpw_ref,
             do_ref, dga_ref, dgc_ref, dc3_ref, dconv_ref, sums_ref):
        @pl.when(pl.program_id(0) == 0)
        def _():
            sums_ref[...] = jnp.zeros_like(sums_ref)

        hm = hm_ref[...]
        col = lambda x: jnp.sum(x, axis=0, keepdims=True)
        ov = o_ref[...]
        rh = lax.rsqrt(_nn((ov * ov).astype(BF16), hm) + EPS)
        xh = ov * rh
        ga = ga_ref[...].astype(F32)
        sg = _sigmoid(ga)
        dya = dya_ref[...]
        don = dya * (ga * sg)
        dga_ref[...] = (dya * xh * gao_ref[...] * _dsilu(ga, sg)).astype(BF16)
        sums_ref[0:1, :] += col(don * xh)
        dxh = don * gao_ref[...]
        do_ref[...] = (rh * (dxh - xh * _dot_hilo(dxh * xh, hm))).astype(BF16)
        c3 = c3_ref[...]
        rc = lax.rsqrt(jnp.mean(c3 * c3, axis=-1, keepdims=True) + EPS)
        xh3 = c3 * rc
        gc = gc_ref[...].astype(F32)
        sgc = _sigmoid(gc)
        dyc = dyc_ref[...]
        dn3 = dyc * (gc * sgc)
        dgc_ref[...] = (dyc * xh3 * gco_ref[...] * _dsilu(gc, sgc)).astype(BF16)
        sums_ref[1:2, :] += col(dn3 * xh3)
        dxh3 = dn3 * gco_ref[...]
        dc3 = (rc * (dxh3 - xh3 * jnp.mean(dxh3 * xh3, axis=-1, keepdims=True))).astype(BF16)
        dc3_ref[...] = dc3
        dc2 = _nt(dc3, wpw_ref[...])
        cv = conv_ref[...]
        mu = jnp.mean(cv, axis=-1, keepdims=True)
        xc = cv - mu
        rs = lax.rsqrt(jnp.mean(xc * xc, axis=-1, keepdims=True) + EPS)
        xn = xc * rs
        ln = xn * lng_ref[...] + lnb_ref[...]
        dln = dc2 * _dsilu(ln, _sigmoid(ln))
        sums_ref[2:3, :] += col(dln * xn)
        sums_ref[3:4, :] += col(dln)
        dxn = dln * lng_ref[...]
        dconv = rs * (dxn - jnp.mean(dxn, axis=-1, keepdims=True)
                      - xn * jnp.mean(dxn * xn, axis=-1, keepdims=True))
        dconv_ref[...] = dconv
        sums_ref[4:5, :] += col(dconv)

    half = lambda dt: jax.ShapeDtypeStruct((T, CHUNK), dt)
    tile = _rows(TM, CHUNK)
    vec = _whole((1, CHUNK))
    return _call(
        body, name=name, grid=(T // TM,),
        in_specs=[_rows(TM, CHUNK, 0), _rows(TM, CHUNK, 1), tile, _rows(TM, CHUNK, 0), _rows(TM, CHUNK, 3),
                  tile, tile, _whole((ATTN_DIM, ATTN_DIM)), vec, vec, vec, vec, _whole((CONV_DIM, CONV_DIM))],
        out_specs=[tile, tile, tile, tile, tile, _whole((8, CHUNK))],
        out_shape=[half(BF16), half(BF16), half(BF16), half(BF16), half(F32),
                   jax.ShapeDtypeStruct((8, CHUNK), F32)],
        args=(dycat, dycat, o, ug, ug, c3, conv, head_mean, g_attn, g_conv, ln_g, ln_b, w_pw,), ride=ride)


def _conv_bwd(dconv, ug, dw_w, name, ride=None):
    T = dconv.shape[0]
    per = TM // HALO
    last = T // HALO - 1
    n_tiles = T // TM

    def body(d_ref, dn_ref, cv_ref, cg_ref, cvh_ref, cgh_ref, w_ref, dcv_ref, dcg_ref, dw_ref,
             dpad_ref, cpad_ref, dsh_ref, csh_ref, dw_acc):
        i = pl.program_id(0)

        @pl.when(i == 0)
        def _():
            dw_acc[...] = jnp.zeros_like(dw_acc)

        tail = jnp.zeros((SUBLANES, CONV_DIM), F32)
        dpad_ref[0:TM, :] = d_ref[...]
        dpad_ref[TM:TM + HALO, :] = jnp.where(i == n_tiles - 1, 0.0, dn_ref[...])
        dpad_ref[TM + HALO:, :] = tail
        halo = cvh_ref[...].astype(F32) * _sigmoid(cgh_ref[...].astype(F32))
        cpad_ref[0:HALO, :] = jnp.where(i == 0, 0.0, halo)
        cpad_ref[HALO:HALO + TM, :] = cv_ref[...].astype(F32) * _sigmoid(cg_ref[...].astype(F32))
        cpad_ref[HALO + TM:, :] = tail
        _shifted_copies(dpad_ref, dsh_ref)
        _shifted_copies(cpad_ref, csh_ref)
        taps = [w_ref[t:t + 1, :] for t in range(CONV_WIDTH)]

        def rows(j, _):
            r = pl.multiple_of(j * CONV_ROWS, CONV_ROWS)
            d = d_ref[pl.ds(r, CONV_ROWS), :]
            dc = jnp.zeros((CONV_ROWS, CONV_DIM), F32)
            for t in range(CONV_WIDTH):
                b, a = _shift_of(CONV_WIDTH - 1 - t)
                dc = dc + taps[t] * dsh_ref[b, pl.ds(r + a, CONV_ROWS), :]
                b, a = _shift_of(HALO - (CONV_WIDTH - 1) + t)
                prod = d * csh_ref[b, pl.ds(r + a, CONV_ROWS), :]
                dw_acc[t] += jnp.sum(prod.reshape(CONV_ROWS // SUBLANES, SUBLANES, CONV_DIM), axis=0)
            cv = cv_ref[pl.ds(r, CONV_ROWS), :].astype(F32)
            sg = _sigmoid(cg_ref[pl.ds(r, CONV_ROWS), :].astype(F32))
            dcv_ref[pl.ds(r, CONV_ROWS), :] = (dc * sg).astype(BF16)
            dcg_ref[pl.ds(r, CONV_ROWS), :] = (dc * cv * sg * (1.0 - sg)).astype(BF16)
            return 0

        lax.fori_loop(0, TM // CONV_ROWS, rows, 0)

        @pl.when(i == n_tiles - 1)
        def _():
            dw_ref[...] = jnp.zeros_like(dw_ref)
            for t in range(CONV_WIDTH):
                dw_ref[t:t + 1, :] = jnp.sum(dw_acc[t], axis=0, keepdims=True)

    prev = lambda col: pl.BlockSpec((HALO, CHUNK), lambda i: (jnp.maximum(i * per - 1, 0), col))
    nxt = pl.BlockSpec((HALO, CONV_DIM), lambda i: (jnp.minimum((i + 1) * per, last), 0))
    half = jax.ShapeDtypeStruct((T, CHUNK), BF16)
    return _call(
        body, name=name, grid=(T // TM,),
        in_specs=[_rows(TM, CONV_DIM), nxt, _rows(TM, CHUNK, 1), _rows(TM, CHUNK, 2), prev(1), prev(2),
                  _whole((CONV_WIDTH, CONV_DIM))],
        out_specs=[_rows(TM, CHUNK), _rows(TM, CHUNK), _whole((HALO, CONV_DIM))],
        out_shape=[half, half, jax.ShapeDtypeStruct((HALO, CONV_DIM), F32)],
        scratch_shapes=[pltpu.VMEM((TM + HALO + SUBLANES, CONV_DIM), F32),
                        pltpu.VMEM((TM + HALO + SUBLANES, CONV_DIM), F32),
                        pltpu.VMEM((SUBLANES, TM + HALO, CONV_DIM), F32),
                        pltpu.VMEM((SUBLANES, TM + HALO, CONV_DIM), F32),
                        pltpu.VMEM((HALO, SUBLANES, CONV_DIM), F32)],
        args=(dconv, dconv, ug, ug, ug, ug, dw_w,), ride=ride)


def _attn_bwd(qs, k, v, do, cs, tri, tri_t, name, ride=None):
    T = qs.shape[0]
    nq = T // BLK
    width = LANES * ATT_COLS
    chains = [(c, half) for c in range(ATT_COLS) for half in range(2)]

    def body(q_ref, k_ref, v_ref, do_ref, cs_ref, m_ref, mt_ref, dq_ref, dk_ref, dv_ref, dk_acc, dv_acc):
        qi = pl.program_id(1)

        @pl.when(qi == 0)
        def _():
            dk_acc[...] = jnp.zeros_like(dk_acc)
            dv_acc[...] = jnp.zeros_like(dv_acc)

        lane = lax.broadcasted_iota(jnp.int32, (BLK, LANES), 1)
        first = lane < HEAD_DIM
        causal = (lax.broadcasted_iota(jnp.int32, (BLK, BLK), 1)
                  < lax.broadcasted_iota(jnp.int32, (BLK, BLK), 0))
        tri_m = m_ref[...]
        tri_mt = mt_ref[...]

        def halves(x):
            zero = jnp.zeros_like(x)
            return jnp.where(first, x, zero), jnp.where(first, zero, x)

        qh, doh, cs = {}, {}, []
        for c in range(ATT_COLS):
            qh[c, 0], qh[c, 1] = halves(q_ref[:, c * LANES:(c + 1) * LANES])
            doh[c, 0], doh[c, 1] = halves(do_ref[:, c * LANES:(c + 1) * LANES])
            cs.append(cs_ref[:, c * LANES:(c + 1) * LANES])

        def step(kb, state, masked):
            prefixes, dq_accs = state
            start = pl.multiple_of(kb * BLK, BLK)
            kblk = [k_ref[pl.ds(start, BLK), c * LANES:(c + 1) * LANES] for c in range(ATT_COLS)]
            vblk = [v_ref[pl.ds(start, BLK), c * LANES:(c + 1) * LANES] for c in range(ATT_COLS)]
            prefixes, dq_accs = list(prefixes), list(dq_accs)
            for g0 in range(0, len(chains), CHAIN_GROUP):
                ids = range(g0, g0 + CHAIN_GROUP)
                grp = [chains[n] for n in ids]
                z = [_nt(qh[ch], kblk[ch[0]]) for ch in grp]
                da = [_nt(doh[ch], vblk[ch[0]]) for ch in grp]
                parts = [_softplus_parts(zi) for zi in z]
                sp = [pt[1] for pt in parts]
                if masked:
                    sp = [jnp.where(causal, s, 0.0) for s in sp]
                incl = [_dot_hilo(s, tri_m) for s in sp]
                carries = [jnp.sum(jnp.where(lane == kb + HEAD_DIM * half, cs[c], 0.0), axis=1, keepdims=True)
                           for c, half in grp]
                a = [jnp.exp(zi - ii - ci) for zi, ii, ci in zip(z, incl, carries)]
                if masked:
                    a = [jnp.where(causal, ai, 0.0) for ai in a]
                w = [ai * di for ai, di in zip(a, da)]
                pinc = [_nn(wi.astype(BF16), tri_mt) for wi in w]
                dz = [wi - pt[0] * (pi + prefixes[n]) for n, wi, pt, pi in zip(ids, w, parts, pinc)]
                if masked:
                    dz = [jnp.where(causal, d, 0.0) for d in dz]
                for j in range(0, CHAIN_GROUP, 2):
                    c = grp[j][0]
                    k0, k1 = halves(kblk[c])
                    dz0, dz1 = dz[j].astype(BF16), dz[j + 1].astype(BF16)
                    a0, a1 = a[j].astype(BF16), a[j + 1].astype(BF16)
                    dq_accs[c] = dq_accs[c] + _nn(dz0, k0) + _nn(dz1, k1)
                    dk_acc[pl.ds(start, BLK), c * LANES:(c + 1) * LANES] += _tn(dz0, qh[c, 0]) + _tn(dz1, qh[c, 1])
                    dv_acc[pl.ds(start, BLK), c * LANES:(c + 1) * LANES] += _tn(a0, doh[c, 0]) + _tn(a1, doh[c, 1])
                for n, pi in zip(ids, pinc):
                    prefixes[n] = prefixes[n] + pi[:, BLK - 1:BLK]
            return tuple(prefixes), tuple(dq_accs)

        state = (tuple(jnp.zeros((BLK, 1), F32) for _ in chains),
                 tuple(jnp.zeros((BLK, LANES), F32) for _ in range(ATT_COLS)))
        first_block = jnp.max(jnp.where(lane == FIRST_BLOCK_LANE, cs[0], 0.0)).astype(jnp.int32)
        state = lax.fori_loop(first_block, qi, lambda kb, st: step(kb, st, False), state)
        state = step(qi, state, True)
        for c in range(ATT_COLS):
            dq_ref[:, c * LANES:(c + 1) * LANES] = (state[1][c] * (HEAD_DIM ** -0.5)).astype(BF16)

        @pl.when(qi == nq - 1)
        def _():
            dk_ref[...] = dk_acc[...].astype(BF16)
            dv_ref[...] = dv_acc[...].astype(BF16)

    blk = pl.BlockSpec((BLK, width), lambda j, i: (i, j))
    col = pl.BlockSpec((T, width), lambda j, i: (0, j))
    out = jax.ShapeDtypeStruct((T, ATTN_DIM), BF16)
    return _call(
        body, name=name, grid=(ATTN_DIM // width, nq),
        in_specs=[blk, col, col, blk, blk, _whole((BLK, BLK)), _whole((BLK, BLK))],
        out_specs=[blk, col, col], out_shape=[out, out, out],
        scratch_shapes=[pltpu.VMEM((T, width), F32), pltpu.VMEM((T, width), F32)],
        args=(qs, k, v, do, cs, tri, tri_t,), ride=ride)


def _inproj_bwd(du, w_in_t, h, dh1, gain, name, ride=None):
    T = h.shape[0]

    def body(*refs):
        du_refs = refs[:N_CHUNK]
        w_ref, h_ref, dh1_ref, g_ref, dh_ref, gsum_ref = refs[N_CHUNK:]

        @pl.when(pl.program_id(0) == 0)
        def _():
            gsum_ref[...] = jnp.zeros_like(gsum_ref)

        dhn = jnp.zeros((TM, D_MODEL), F32)
        for j in range(N_CHUNK):
            dhn = dhn + _nn(du_refs[j][...], w_ref[j * CHUNK:(j + 1) * CHUNK, :])
        hv = h_ref[...]
        r = lax.rsqrt(jnp.mean(hv * hv, axis=-1, keepdims=True) + EPS)
        xh = hv * r
        gsum_ref[...] += jnp.sum(dhn * xh, axis=0, keepdims=True)
        dxh = dhn * g_ref[...]
        dh_ref[...] = dh1_ref[...] + r * (dxh - xh * jnp.mean(dxh * xh, axis=-1, keepdims=True))

    full = _rows(TM, D_MODEL)
    return _call(
        body, name=name, grid=(T // TM,),
        in_specs=[_rows(TM, CHUNK)] * N_CHUNK + [_whole((N_CHUNK * CHUNK, D_MODEL)), full, full,
                                                 _whole((1, D_MODEL))],
        out_specs=[full, _whole((1, D_MODEL))],
        out_shape=[jax.ShapeDtypeStruct((T, D_MODEL), F32), jax.ShapeDtypeStruct((1, D_MODEL), F32)],
        args=(*du, w_in_t, h, dh1, gain), ride=ride)


def _weight_grad(lhs_list, rhs, name, tk=CHUNK, ride=None):
    T, n_rhs = rhs.shape
    n = len(lhs_list)
    ka = lhs_list[0].shape[1]
    per = ka // tk

    def body(*refs):
        a_refs, b_ref, out_ref = refs[:n], refs[n], refs[n + 1]
        step = pl.program_id(0)
        for j in range(n):
            for s in range(per):
                @pl.when(step == j * per + s)
                def _(j=j, s=s):
                    out_ref[...] = _tn(a_refs[j][:, s * tk:(s + 1) * tk], b_ref[...]).astype(BF16)

    (grad,), landed = _call(
        body, name=name, grid=(n * per,),
        in_specs=[_whole((T, ka))] * n + [_whole((T, n_rhs))],
        out_specs=[pl.BlockSpec((tk, n_rhs), lambda i: (i, 0))],
        out_shape=[jax.ShapeDtypeStruct((n * ka, n_rhs), BF16)],
        args=(*lhs_list, rhs), ride=ride)
    return grad, landed


def _adamw_update(w, g, m, v):
    nm = ADAM_B1 * m + (1.0 - ADAM_B1) * g
    nv = ADAM_B2 * v + (1.0 - ADAM_B2) * (g * g)
    m_hat = nm / (1.0 - ADAM_B1 ** ADAM_STEP)
    v_hat = nv / (1.0 - ADAM_B2 ** ADAM_STEP)
    return -ADAM_LR * (m_hat / (jnp.sqrt(v_hat) + ADAM_EPS) + ADAM_WD * w), nm, nv


def _sum_adamw(slots, w, m, v, name):
    depth, R, C = w.shape
    tr = next(rows for rows in ADAMW_ROWS if R % rows == 0)

    def body(*refs):
        slot_refs, (w_ref, m_ref, v_ref, g_ref, d_ref, nm_ref, nv_ref) = refs[:depth], refs[depth:]
        for layer in range(depth):
            @pl.when(pl.program_id(0) == layer)
            def _(src=slot_refs[layer]):
                g = src[0].astype(F32)
                for s in range(1, src.shape[0]):
                    g = g + src[s].astype(F32)
                g_ref[0] = g
                d_ref[0], nm_ref[0], nv_ref[0] = _adamw_update(w_ref[0], g, m_ref[0], v_ref[0])

    slot_spec = lambda layer: pl.BlockSpec((slots[layer].shape[0], tr, C),
                                           lambda l, i: (0, jnp.where(l == layer, i, 0), 0))
    spec = pl.BlockSpec((1, tr, C), lambda l, i: (l, i, 0))
    out = jax.ShapeDtypeStruct((depth, R, C), F32)
    return pl.pallas_call(
        body, name=name, grid=(depth, R // tr),
        in_specs=[slot_spec(layer) for layer in range(depth)] + [spec] * 3,
        out_specs=[spec] * 4, out_shape=[out] * 4,
        compiler_params=_params(2),
    )(*slots, w, m, v)


def _adamw(w, g, m, v, name):
    R, C = w.shape
    tr = R
    for cand in (512, 256, 128, 64):
        if R % cand == 0 and R > cand:
            tr = cand
            break

    def body(w_ref, g_ref, m_ref, v_ref, d_ref, nm_ref, nv_ref):
        d_ref[...], nm_ref[...], nv_ref[...] = _adamw_update(w_ref[...], g_ref[...], m_ref[...], v_ref[...])

    spec = pl.BlockSpec((tr, C), lambda i: (i, 0))
    out = jax.ShapeDtypeStruct((R, C), F32)
    return pl.pallas_call(
        body, name=name, grid=(R // tr,),
        in_specs=[spec] * 4, out_specs=[spec] * 3, out_shape=[out, out, out],
        compiler_params=_params(1),
    )(w, g, m, v)


def _pack_small(values, scalar=None):
    pad = lambda a: jnp.pad(a, ((0, 0), (0, D_MODEL - a.shape[1])))
    last = jnp.zeros((1, D_MODEL), F32) if scalar is None else pad(scalar.reshape(1, 1))
    return jnp.concatenate([pad(values[name].reshape(rows, cols)) for name, _, rows, cols in SMALL_LAYOUT] + [last],
                           axis=0)


def _small_update(all_packs, state, name):
    n = len(SMALL_LAYOUT)

    def body(packs_ref, *refs):
        ins, outs = refs[:3 * n], refs[3 * n:]
        total = packs_ref[0]
        for s in range(1, N_DEV):
            total = total + packs_ref[s]
        for j, (_, at, rows, cols) in enumerate(SMALL_LAYOUT):
            g = total[at:at + rows, :cols]
            w_ref, m_ref, v_ref = ins[3 * j:3 * j + 3]
            outs[4 * j][...] = g
            outs[4 * j + 1][...], outs[4 * j + 2][...], outs[4 * j + 3][...] = _adamw_update(
                w_ref[...], g, m_ref[...], v_ref[...])
        outs[-2][...] = total[LOSS_ROW:LOSS_ROW + 1, :LANES]
        outs[-1][...] = total[SMALL_ROWS:, :]

    shapes = [jax.ShapeDtypeStruct((rows, cols), F32) for _, _, rows, cols in SMALL_LAYOUT for _ in range(4)]
    shapes += [jax.ShapeDtypeStruct((1, LANES), F32), jax.ShapeDtypeStruct((PACK_ROWS - SMALL_ROWS, D_MODEL), F32)]
    operands = [a for item in SMALL_LAYOUT for a in state[item[0]]]
    res = pl.pallas_call(body, name=name, out_shape=shapes, compiler_params=_params())(all_packs, *operands)
    per_name = {item[0]: tuple(res[4 * j:4 * j + 4]) for j, item in enumerate(SMALL_LAYOUT)}
    return per_name, res[-2][0, 0], res[-1]


def kernel(x, p, norm_g, w_in, attn_out_g, dw_w, dw_b, conv_ln_g, conv_ln_b, w_pw, conv_out_g, w_out, ple_norm_g, w_ple_gate, w_ple, final_g, loss_target, m_norm_g, m_w_in, m_attn_out_g, m_dw_w, m_dw_b, m_conv_ln_g, m_conv_ln_b, m_w_pw, m_conv_out_g, m_w_out, m_ple_norm_g, m_w_ple_gate, m_w_ple, m_final_g, v_norm_g, v_w_in, v_attn_out_g, v_dw_w, v_dw_b, v_conv_ln_g, v_conv_ln_b, v_w_pw, v_conv_out_g, v_w_out, v_ple_norm_g, v_w_ple_gate, v_w_ple, v_final_g):
    depth = w_in.shape[0]
    T = x.shape[1]
    given = dict(
        norm_g=norm_g, ple_norm_g=ple_norm_g, final_g=final_g, dw_b=dw_b, conv_ln_g=conv_ln_g, conv_ln_b=conv_ln_b,
        conv_out_g=conv_out_g, attn_out_g=attn_out_g,
        m_norm_g=m_norm_g, m_ple_norm_g=m_ple_norm_g, m_final_g=m_final_g, m_dw_b=m_dw_b, m_conv_ln_g=m_conv_ln_g,
        m_conv_ln_b=m_conv_ln_b, m_conv_out_g=m_conv_out_g, m_attn_out_g=m_attn_out_g,
        v_norm_g=v_norm_g, v_ple_norm_g=v_ple_norm_g, v_final_g=v_final_g, v_dw_b=v_dw_b, v_conv_ln_g=v_conv_ln_g,
        v_conv_ln_b=v_conv_ln_b, v_conv_out_g=v_conv_out_g, v_attn_out_g=v_attn_out_g)
    my_idx = 4 * lax.axis_index("x") + 2 * lax.axis_index("y") + lax.axis_index("c")

    ids = jnp.arange(BLK)
    tri = (ids[:, None] >= ids[None, :]).astype(BF16)
    tri_t = (ids[:, None] <= ids[None, :]).astype(BF16)
    hid = jnp.arange(ATTN_DIM) // HEAD_DIM
    head_mean = ((hid[:, None] == hid[None, :]).astype(F32) / HEAD_DIM).astype(BF16)

    w_names = ("w_in_t", "w_pw", "w_out", "w_gate", "w_ple")
    w_axes = dict(zip(w_names, (0, 0, 0, 0, 1)))
    shards = [dict(zip(w_names, (w_in[l].T.astype(BF16), w_pw[l].astype(BF16), w_out[l].astype(BF16),
                                 w_ple_gate[l].astype(BF16), w_ple[l].astype(BF16)))) for l in range(depth)]
    first = _all_gather([shards[0]["w_in_t"]] + [dw_w[l].T for l in range(depth)], [0] * (1 + depth),
                        "gather_weights_0")
    layers = []
    for l in range(depth):
        layers.append(dict(
            dw_w=first[1 + l].T,
            g_norm=norm_g[l][None], g_attn=jnp.tile(attn_out_g[l], N_HEADS)[None], dw_b=dw_b[l][None],
            ln_g=conv_ln_g[l][None], ln_b=conv_ln_b[l][None], g_conv=conv_out_g[l][None],
            g_ple=ple_norm_g[l][None], p=p[l, 0]))
    layers[0]["w_in_t"] = first[0]

    def rest_of(l, names):
        return [_Ride.gather2(shards[l][n], w_axes[n]) for n in names]

    h = x[0]
    saved = []
    for l, w in enumerate(layers):
        early, late = (w_names[3:], w_names[1:3]) if l == 0 else ((), ())
        (qs, k, v, ug, hn), landed = _prenorm_inproj(h, w["g_norm"], w["w_in_t"], f"inproj_{l}",
                                                     _Ride(rest_of(l, early)))
        w.update(zip(early, landed))
        ahead = [_Ride.gather2(shards[l + 1]["w_in_t"], 0, None, 0, W_IN_ROWS_ON_ATTN)] if l + 1 < depth else []
        own = w_names[1:] if l > 0 else ()
        (o, cs), landed = _attn_fwd(qs, k, v, tri, f"attn_fwd_{l}", _Ride(ahead + rest_of(l, own)))
        w_in_next = landed[:len(ahead)]
        w.update(zip(own, landed[len(ahead):]))
        (conv, c2), landed = _conv_fwd(ug, w["dw_w"], w["dw_b"], w["ln_g"], w["ln_b"], f"conv_fwd_{l}",
                                       _Ride(rest_of(l, late)))
        w.update(zip(late, landed))
        tail = [_Ride.gather2(shards[l + 1]["w_in_t"], 0, a, W_IN_ROWS_ON_ATTN,
                              shards[l + 1]["w_in_t"].shape[0] - W_IN_ROWS_ON_ATTN) for a in w_in_next]
        last = l + 1 == depth
        (h2, h1, ycat, hn2, gate, e, c3, *loss_out), landed = _mix_out_ple(
            o, ug, c2, h, w["p"], head_mean, w["g_attn"], w["g_conv"], w["g_ple"],
            w["w_pw"], w["w_out"], w["w_gate"], w["w_ple"], f"mix_{l}", _Ride(tail),
            loss_head=(loss_target[0], final_g[None]) if last else None)
        if landed:
            layers[l + 1]["w_in_t"] = landed[0]
        saved.append(dict(h=h, qs=qs, k=k, v=v, ug=ug, hn=hn, o=o, cs=cs, conv=conv, c2=c2, h1=h1,
                          ycat=ycat, hn2=hn2, gate=gate, e=e, c3=c3))
        h = h2
    dh, (g_final, loss_part) = h, loss_out

    small = {}
    dww_parts = [None] * depth
    slots = [dict() for _ in range(depth)]
    g_w_in = None
    pending = []
    for l in reversed(range(depth)):
        w, s = layers[l], saved[l]
        above = pending

        def part(i, above=above, g=g_w_in):
            return [_Ride.scatter(g, 0, above[0], *W_IN_GRAD_PARTS[i])] if above else []

        def scattered(grads, names):
            return [_Ride.scatter(grads[n], w_axes[n]) for n in names]

        (dh1, dh1b, dzg, de, dycat, g_ple_sum), landed = _ple_out_bwd(
            dh, s["gate"], s["e"], s["h1"], w["g_ple"], w["w_gate"], w["w_out"], f"ple_bwd_{l}", _Ride(part(1)))
        above[:1] = landed
        (do, dga, dgc, dc3, dconv, sums), landed = _branch_bwd(
            dycat, s["o"], s["ug"], s["c3"], s["conv"], head_mean, w["g_attn"], w["g_conv"],
            w["ln_g"], w["ln_b"], w["w_pw"], f"branch_bwd_{l}", _Ride(part(2)))
        above[:1] = landed
        grads = dict(
            w_pw=_weight_grad([s["c2"]], dc3, f"grad_w_pw_{l}")[0],
            w_out=_weight_grad([s["ycat"]], dh1b, f"grad_w_out_{l}")[0],
            w_gate=_weight_grad([s["hn2"]], dzg, f"grad_w_gate_{l}")[0],
            w_ple=_weight_grad([w["p"].astype(BF16)], de, f"grad_w_ple_{l}", tk=PLE_DIM)[0])
        (dcv, dcg, dww), landed = _conv_bwd(dconv, s["ug"], w["dw_w"], f"conv_bwd_{l}", _Ride(part(3)))
        if above:
            slots[l + 1]["w_in_t"] = landed[0]
        (dq, dk, dv), landed = _attn_bwd(s["qs"], s["k"], s["v"], do, s["cs"], tri, tri_t, f"attn_bwd_{l}",
                                         _Ride(scattered(grads, w_names[1:])))
        slots[l].update(zip(w_names[1:], landed))
        du = [dq, dk, dv, dga, dcv, dcg, dgc]
        g_w_in_here, _ = _weight_grad(du, s["hn"], f"grad_w_in_{l}")
        if l == 0:
            tail = [_Ride.scatter_chips(_pair_reduce(g_w_in_here, f"pair_reduce_w_in_{l}"))]
        else:
            tail = [_Ride.scatter(g_w_in_here, 0, None, *W_IN_GRAD_PARTS[0])]
        (dh, g_norm_sum), landed = _inproj_bwd(du, w["w_in_t"], s["h"], dh1, w["g_norm"], f"inproj_bwd_{l}",
                                               _Ride(tail))
        if l == 0:
            slots[l]["w_in_t"] = landed[0]
        pending = landed if l > 0 else []
        g_w_in = g_w_in_here
        small[l] = dict(norm_g=g_norm_sum, ple_norm_g=g_ple_sum, attn_out_g=sums[0].reshape(N_HEADS, HEAD_DIM).sum(0),
                        conv_out_g=sums[1], conv_ln_g=sums[2], conv_ln_b=sums[3], dw_b=sums[4])
        dww_parts[l] = dww[:CONV_WIDTH]
    slots = [[sl[n] for n in w_names] for sl in slots]
    grad_x = dh[None]

    sums_of = {name: jnp.stack([small[l][name].reshape(-1) for l in range(depth)]) for name in small[0]}
    sums_of["final_g"] = g_final
    pack = jnp.concatenate([_pack_small(sums_of, scalar=loss_part[0, 0]), jnp.concatenate(dww_parts, axis=1),
                            jnp.zeros((PACK_ROWS - SMALL_ROWS - CONV_WIDTH, D_MODEL), F32)], axis=0)
    (all_packs,) = _all_gather([pack], [0], "gather_small_grads")
    state = {name: [given[pre + name].reshape(rows, cols) for pre in ("", "m_", "v_")]
             for name, _, rows, cols in SMALL_LAYOUT}
    updated, loss, dww_sum = _small_update(all_packs.reshape(N_DEV, PACK_ROWS, D_MODEL), state, "update_small")
    res = {kind: {name: val[k].reshape(given[name].shape) for name, val in updated.items()}
           for k, kind in enumerate("gdmv")}
    dww_full = dww_sum[:CONV_WIDTH].reshape(CONV_WIDTH, depth, CONV_DIM).transpose(1, 0, 2)
    g_dw_w = lax.dynamic_slice_in_dim(dww_full, my_idx * (CONV_DIM // N_DEV), CONV_DIM // N_DEV, axis=2)

    swap = lambda a: a.transpose(0, 2, 1)
    state = {"w_in": (w_in, m_w_in, v_w_in), "w_pw": (w_pw, m_w_pw, v_w_pw), "w_out": (w_out, m_w_out, v_w_out),
             "w_ple_gate": (w_ple_gate, m_w_ple_gate, v_w_ple_gate), "w_ple": (w_ple, m_w_ple, v_w_ple)}
    for at, name in enumerate(state):
        wv, mv, vv = [swap(a) for a in state[name]] if name == "w_in" else state[name]
        out = _sum_adamw([slots[l][at] for l in range(depth)], wv, mv, vv, f"adamw_{name}")
        out = [swap(a) for a in out] if name == "w_in" else out
        res["g"][name], res["d"][name], res["m"][name], res["v"][name] = out
    flat = lambda a: a.reshape(-1, a.shape[-1])
    res["g"]["dw_w"] = g_dw_w
    res["d"]["dw_w"], res["m"]["dw_w"], res["v"]["dw_w"] = [
        a.reshape(dw_w.shape) for a in _adamw(flat(dw_w), flat(g_dw_w), flat(m_dw_w), flat(v_dw_w), "adamw_dw_w")]

    order = ["norm_g", "w_in", "attn_out_g", "dw_w", "dw_b", "conv_ln_g", "conv_ln_b", "w_pw", "conv_out_g",
             "w_out", "ple_norm_g", "w_ple_gate", "w_ple", "final_g"]
    return (loss, grad_x, *[res["g"][n] for n in order], *[res["d"][n] for n in order],
            *[res["m"][n] for n in order], *[res["v"][n] for n in order])
```
